```python
import math
import jax, jax.numpy as jnp
from jax import lax
import numpy as np

D_MODEL = 2048
BATCH = 8
SEQ = 8192
DEPTH = 1

MEM_LEN = 256
SWA_HEADS = 16
SWA_KV_HEADS = 2
SWA_HEAD_DIM = 64
WINDOW = 128
BLOCK = 128
REL_BUCKETS = 32
REL_MAX_DIST = 128
GDN_HEADS = 8
GDN_HEAD_DIM = 128
GDN_CONV = 4
GDN_CHUNK = 64
MEM_HEADS = 4
MEM_HEAD_DIM = 128
D_FF = 5504
FFN_CONV = 3

NORM_EPS = 1e-5
DEEPNORM_ALPHA = (2 * DEPTH) ** 0.25
DEEPNORM_BETA = (8 * DEPTH) ** -0.25
NEG_INF = -1e30

SWA_Q = SWA_HEADS * SWA_HEAD_DIM
SWA_KV = SWA_KV_HEADS * SWA_HEAD_DIM
GDN_W = GDN_HEADS * GDN_HEAD_DIM
MEM_W = MEM_HEADS * MEM_HEAD_DIM
IN_WIDTHS = (SWA_Q, SWA_KV, SWA_KV, GDN_W, GDN_W, GDN_W, GDN_W, GDN_HEADS, GDN_HEADS, D_MODEL, D_MODEL)
IN_DIM = sum(IN_WIDTHS)

kernel_name = "hybrid_swa_gdn_gated_merge_deepnorm"


def split_columns(t, widths):
    points, acc = [], 0
    for w in widths[:-1]:
        acc += w
        points.append(acc)
    return jnp.split(t, points, axis=-1)


def layer_norm(x, g, b):
    xf = x.astype(jnp.float32)
    mu = jnp.mean(xf, axis=-1, keepdims=True)
    var = jnp.mean(jnp.square(xf - mu), axis=-1, keepdims=True)
    y = (xf - mu) * lax.rsqrt(var + NORM_EPS) * g.astype(jnp.float32) + b.astype(jnp.float32)
    return y.astype(x.dtype)


def causal_dwconv(x, w):
    width, ch = w.shape
    return lax.conv_general_dilated(
        x, w[:, None, :].astype(x.dtype), window_strides=(1,), padding=[(width - 1, 0)],
        dimension_numbers=('NWC', 'WIO', 'NWC'), feature_group_count=ch)


def t5_causal_bucket(dist):
    max_exact = REL_BUCKETS // 2
    d = jnp.maximum(dist, 1).astype(jnp.float32)
    large = max_exact + (jnp.log(d / max_exact) / math.log(REL_MAX_DIST / max_exact)
                         * (REL_BUCKETS - max_exact)).astype(jnp.int32)
    large = jnp.minimum(large, REL_BUCKETS - 1)
    return jnp.where(dist < max_exact, dist, large)


def swa_attention(q, k, v, sinks, rel_bias):
    b, s = q.shape[:2]
    nb = s // BLOCK
    grp = SWA_HEADS // SWA_KV_HEADS
    qb = q.reshape(b, nb, BLOCK, SWA_KV_HEADS, grp, SWA_HEAD_DIM)

    def band(t):
        tb = t.reshape(b, nb, BLOCK, SWA_KV_HEADS, SWA_HEAD_DIM)
        prev = jnp.pad(tb[:, :-1], ((0, 0), (1, 0), (0, 0), (0, 0), (0, 0)))
        return jnp.concatenate([prev, tb], axis=2)

    kb, vb = band(k), band(v)
    scores = jnp.einsum('bnqhgd,bnkhd->bnhgqk', qb, kb).astype(jnp.float32) * (SWA_HEAD_DIM ** -0.5)

    qi = jnp.arange(BLOCK)[:, None]
    kj = jnp.arange(2 * BLOCK)[None, :]
    dist = qi + BLOCK - kj
    in_window = (dist >= 0) & (dist < WINDOW)
    has_prev = (jnp.arange(nb)[:, None, None] > 0) | (kj >= BLOCK)[None]
    mask = in_window[None] & has_prev
    bias = rel_bias.astype(jnp.float32)[t5_causal_bucket(jnp.maximum(dist, 0))]
    bias = bias.transpose(2, 0, 1).reshape(SWA_KV_HEADS, grp, BLOCK, 2 * BLOCK)
    scores = jnp.where(mask[None, :, None, None], scores + bias, NEG_INF)

    sink = jnp.broadcast_to(sinks.astype(jnp.float32).reshape(1, 1, SWA_KV_HEADS, grp, 1, 1),
                            scores.shape[:-1] + (1,))
    probs = jax.nn.softmax(jnp.concatenate([scores, sink], axis=-1), axis=-1)[..., :-1]
    out = jnp.einsum('bnhgqk,bnkhd->bnqhgd', probs.astype(vb.dtype), vb)
    return out.reshape(b, s, SWA_Q)


def l2norm(t):
    return t * lax.rsqrt(jnp.sum(jnp.square(t), axis=-1, keepdims=True) + 1e-6)


def gated_delta_rule(q, k, v, g, beta):
    b, s, h, dk = q.shape
    dv = v.shape[-1]
    c = GDN_CHUNK
    n = s // c
    q = q * (dk ** -0.5)

    def chunk(t):
        return jnp.swapaxes(t.reshape((b, n, c, h) + t.shape[3:]), 2, 3)

    qc, kc, vc = chunk(q), chunk(k), chunk(v)
    gc = jnp.cumsum(chunk(g), axis=-1)
    bc = chunk(beta)
    kbeta = kc * bc[..., None]
    vbeta = vc * bc[..., None]

    tril = jnp.tril(jnp.ones((c, c), dtype=bool))
    strict = jnp.tril(jnp.ones((c, c), dtype=bool), -1)
    diff = gc[..., :, None] - gc[..., None, :]
    decay = jnp.where(tril, jnp.exp(jnp.where(tril, diff, 0.0)), 0.0)

    a_low = jnp.where(strict, jnp.einsum('bnhid,bnhjd->bnhij', kbeta, kc) * decay, 0.0)
    t_mat = a_low + jnp.eye(c, dtype=jnp.float32)
    rhs = jnp.concatenate([vbeta, kbeta * jnp.exp(gc)[..., None]], axis=-1)
    sol = lax.linalg.triangular_solve(t_mat, rhs, left_side=True, lower=True)
    u, w = sol[..., :dv], sol[..., dv:]

    attn_intra = jnp.where(tril, jnp.einsum('bnhid,bnhjd->bnhij', qc, kc) * decay, 0.0)
    q_dec = qc * jnp.exp(gc)[..., None]
    k_dec = kc * jnp.exp(gc[..., -1:] - gc)[..., None]
    g_last = jnp.exp(gc[..., -1])

    def step(state, inp):
        qd, kd, uu, ww, ai, gl = inp
        v_new = uu - jnp.einsum('bhck,bhkv->bhcv', ww, state)
        o = jnp.einsum('bhck,bhkv->bhcv', qd, state) + jnp.einsum('bhij,bhjv->bhiv', ai, v_new)
        state = state * gl[..., None, None] + jnp.einsum('bhck,bhcv->bhkv', kd, v_new)
        return state, o

    xs = tuple(jnp.moveaxis(t, 1, 0) for t in (q_dec, k_dec, u, w, attn_intra, g_last))
    state0 = jnp.zeros((b, h, dk, dv), jnp.float32)
    _, o = lax.scan(step, state0, xs)
    return jnp.swapaxes(jnp.moveaxis(o, 0, 1), 2, 3).reshape(b, s, h, dv)


def gated_deltanet(gq, gk, gv, gz, gb, ga, conv_w, a_log, dt_bias, norm_w):
    b, s, _ = gq.shape
    dtype = gq.dtype
    qkv = jax.nn.silu(causal_dwconv(jnp.concatenate([gq, gk, gv], axis=-1), conv_w))
    q, k, v = split_columns(qkv.astype(jnp.float32), (GDN_W, GDN_W, GDN_W))
    q = l2norm(q.reshape(b, s, GDN_HEADS, GDN_HEAD_DIM))
    k = l2norm(k.reshape(b, s, GDN_HEADS, GDN_HEAD_DIM))
    v = v.reshape(b, s, GDN_HEADS, GDN_HEAD_DIM)
    beta = jax.nn.sigmoid(gb.astype(jnp.float32))
    g = -jnp.exp(a_log.astype(jnp.float32)) * jax.nn.softplus(ga.astype(jnp.float32) + dt_bias.astype(jnp.float32))
    o = gated_delta_rule(q, k, v, g, beta)
    o = o * lax.rsqrt(jnp.mean(jnp.square(o), axis=-1, keepdims=True) + 1e-6) * norm_w.astype(jnp.float32)
    z = jax.nn.silu(gz.astype(jnp.float32)).reshape(b, s, GDN_HEADS, GDN_HEAD_DIM)
    return (o * z).reshape(b, s, GDN_W).astype(dtype)


def memory_attention(x, mem, w_q, w_kv, w_o):
    b, s, _ = x.shape
    m = mem.shape[1]
    q = (x @ w_q).reshape(b, s, MEM_HEADS, MEM_HEAD_DIM)
    k, v = split_columns(mem @ w_kv, (MEM_W, MEM_W))
    k = k.reshape(b, m, MEM_HEADS, MEM_HEAD_DIM)
    v = v.reshape(b, m, MEM_HEADS, MEM_HEAD_DIM)
    scores = jnp.einsum('bshd,bmhd->bhsm', q, k).astype(jnp.float32) * (MEM_HEAD_DIM ** -0.5)
    p = jax.nn.softmax(scores, axis=-1)
    o = jnp.einsum('bhsm,bmhd->bshd', p.astype(v.dtype), v).reshape(b, s, MEM_W)
    return o @ w_o


def _fwd_setup_inputs(seed: int = 0) -> dict:
    key = jax.random.key(seed)
    ks = jax.random.split(key, 32)
    f32 = jnp.float32
    L = DEPTH

    def nrm(k, shape, scale):
        return jax.random.normal(k, shape, f32) * scale

    dt = jnp.exp(jax.random.uniform(ks[7], (L, GDN_HEADS), f32, math.log(1e-3), math.log(1e-1)))
    return {
        "x": nrm(ks[0], (BATCH, SEQ, D_MODEL), 1.0),
        "mem": nrm(ks[1], (BATCH, MEM_LEN, D_MODEL), 1.0),
        "w_in": nrm(ks[2], (L, D_MODEL, IN_DIM), D_MODEL ** -0.5),
        "rel_bias": nrm(ks[3], (REL_BUCKETS, SWA_HEADS), 0.5),
        "swa_sinks": nrm(ks[4], (L, SWA_HEADS), 1.0),
        "gdn_conv_w": nrm(ks[5], (L, GDN_CONV, 3 * GDN_W), GDN_CONV ** -0.5),
        "gdn_a_log": jnp.log(jax.random.uniform(ks[6], (L, GDN_HEADS), f32, 1.0, 16.0)),
        "gdn_dt_bias": dt + jnp.log(-jnp.expm1(-dt)),
        "gdn_norm_w": 1.0 + nrm(ks[8], (L, GDN_HEAD_DIM), 0.02),
        "w_br_swa": nrm(ks[9], (L, SWA_Q, D_MODEL), SWA_Q ** -0.5),
        "w_br_gdn": nrm(ks[10], (L, GDN_W, D_MODEL), GDN_W ** -0.5),
        "w_mix_o": nrm(ks[11], (L, D_MODEL, D_MODEL), D_MODEL ** -0.5 * DEEPNORM_BETA),
        "ln1_g": 1.0 + nrm(ks[12], (L, D_MODEL), 0.02),
        "ln1_b": nrm(ks[13], (L, D_MODEL), 0.02),
        "w_mem_q": nrm(ks[14], (L, D_MODEL, MEM_W), D_MODEL ** -0.5),
        "w_mem_kv": nrm(ks[15], (L, D_MODEL, 2 * MEM_W), D_MODEL ** -0.5),
        "w_mem_o": nrm(ks[16], (L, MEM_W, D_MODEL), MEM_W ** -0.5 * DEEPNORM_BETA),
        "ln2_g": 1.0 + nrm(ks[17], (L, D_MODEL), 0.02),
        "ln2_b": nrm(ks[18], (L, D_MODEL), 0.02),
        "w_up": nrm(ks[19], (L, D_MODEL, 2 * D_FF), D_MODEL ** -0.5),
        "ffn_conv_w": nrm(ks[20], (L, FFN_CONV, 2 * D_FF), FFN_CONV ** -0.5),
        "ffn_conv_b": nrm(ks[21], (L, 2 * D_FF), 0.02),
        "w_down": nrm(ks[22], (L, D_FF, D_MODEL), D_FF ** -0.5 * DEEPNORM_BETA),
        "ln3_g": 1.0 + nrm(ks[23], (L, D_MODEL), 0.02),
        "ln3_b": nrm(ks[24], (L, D_MODEL), 0.02),
    }


def _fwd_reference(x, mem, w_in, rel_bias, swa_sinks, gdn_conv_w, gdn_a_log, gdn_dt_bias, gdn_norm_w,
              w_br_swa, w_br_gdn, w_mix_o, ln1_g, ln1_b, w_mem_q, w_mem_kv, w_mem_o, ln2_g, ln2_b,
              w_up, ffn_conv_w, ffn_conv_b, w_down, ln3_g, ln3_b):
    b, s, _ = x.shape
    for l in range(DEPTH):
        proj = x @ w_in[l]
        (sq, sk, sv, gq, gk, gv, gz, gb, ga, gate_swa, gate_gdn) = split_columns(proj, IN_WIDTHS)
        y_swa = swa_attention(
            sq.reshape(b, s, SWA_HEADS, SWA_HEAD_DIM),
            sk.reshape(b, s, SWA_KV_HEADS, SWA_HEAD_DIM),
            sv.reshape(b, s, SWA_KV_HEADS, SWA_HEAD_DIM),
            swa_sinks[l], rel_bias) @ w_br_swa[l]
        y_gdn = gated_deltanet(gq, gk, gv, gz, gb, ga, gdn_conv_w[l], gdn_a_log[l],
                               gdn_dt_bias[l], gdn_norm_w[l]) @ w_br_gdn[l]
        mixed = jax.nn.sigmoid(gate_swa) * y_swa + jax.nn.sigmoid(gate_gdn) * y_gdn
        x = layer_norm(DEEPNORM_ALPHA * x + mixed @ w_mix_o[l], ln1_g[l], ln1_b[l])
        c = memory_attention(x, mem, w_mem_q[l], w_mem_kv[l], w_mem_o[l])
        x = layer_norm(DEEPNORM_ALPHA * x + c, ln2_g[l], ln2_b[l])
        hcat = causal_dwconv(x @ w_up[l], ffn_conv_w[l]) + ffn_conv_b[l]
        h_gate, h_up = split_columns(hcat, (D_FF, D_FF))
        f = (jax.nn.silu(h_gate) * h_up) @ w_down[l]
        x = layer_norm(DEEPNORM_ALPHA * x + f, ln3_g[l], ln3_b[l])
    return x


import jax as _jax
import jax.numpy as _jnp

TWIN_FORMAT = 'train_step'
FWD_PARAMS = ['x', 'mem', 'w_in', 'rel_bias', 'swa_sinks', 'gdn_conv_w', 'gdn_a_log', 'gdn_dt_bias', 'gdn_norm_w', 'w_br_swa', 'w_br_gdn', 'w_mix_o', 'ln1_g', 'ln1_b', 'w_mem_q', 'w_mem_kv', 'w_mem_o', 'ln2_g', 'ln2_b', 'w_up', 'ffn_conv_w', 'ffn_conv_b', 'w_down', 'ln3_g', 'ln3_b']
TWIN_WEIGHTS = ['w_in', 'rel_bias', 'swa_sinks', 'gdn_conv_w', 'gdn_a_log', 'gdn_dt_bias', 'gdn_norm_w', 'w_br_swa', 'w_br_gdn', 'w_mix_o', 'ln1_g', 'ln1_b', 'w_mem_q', 'w_mem_kv', 'w_mem_o', 'ln2_g', 'ln2_b', 'w_up', 'ffn_conv_w', 'ffn_conv_b', 'w_down', 'ln3_g', 'ln3_b']
TWIN_DIFF_INPUT = 'x'
TWIN_INPUTS = ['x', 'mem', 'w_in', 'rel_bias', 'swa_sinks', 'gdn_conv_w', 'gdn_a_log', 'gdn_dt_bias', 'gdn_norm_w', 'w_br_swa', 'w_br_gdn', 'w_mix_o', 'ln1_g', 'ln1_b', 'w_mem_q', 'w_mem_kv', 'w_mem_o', 'ln2_g', 'ln2_b', 'w_up', 'ffn_conv_w', 'ffn_conv_b', 'w_down', 'ln3_g', 'ln3_b', 'loss_target', 'm_w_in', 'm_rel_bias', 'm_swa_sinks', 'm_gdn_conv_w', 'm_gdn_a_log', 'm_gdn_dt_bias', 'm_gdn_norm_w', 'm_w_br_swa', 'm_w_br_gdn', 'm_w_mix_o', 'm_ln1_g', 'm_ln1_b', 'm_w_mem_q', 'm_w_mem_kv', 'm_w_mem_o', 'm_ln2_g', 'm_ln2_b', 'm_w_up', 'm_ffn_conv_w', 'm_ffn_conv_b', 'm_w_down', 'm_ln3_g', 'm_ln3_b', 'v_w_in', 'v_rel_bias', 'v_swa_sinks', 'v_gdn_conv_w', 'v_gdn_a_log', 'v_gdn_dt_bias', 'v_gdn_norm_w', 'v_w_br_swa', 'v_w_br_gdn', 'v_w_mix_o', 'v_ln1_g', 'v_ln1_b', 'v_w_mem_q', 'v_w_mem_kv', 'v_w_mem_o', 'v_ln2_g', 'v_ln2_b', 'v_w_up', 'v_ffn_conv_w', 'v_ffn_conv_b', 'v_w_down', 'v_ln3_g', 'v_ln3_b']
TWIN_OUTPUTS = ['loss', 'grad_x', 'grad_w_in', 'grad_rel_bias', 'grad_swa_sinks', 'grad_gdn_conv_w', 'grad_gdn_a_log', 'grad_gdn_dt_bias', 'grad_gdn_norm_w', 'grad_w_br_swa', 'grad_w_br_gdn', 'grad_w_mix_o', 'grad_ln1_g', 'grad_ln1_b', 'grad_w_mem_q', 'grad_w_mem_kv', 'grad_w_mem_o', 'grad_ln2_g', 'grad_ln2_b', 'grad_w_up', 'grad_ffn_conv_w', 'grad_ffn_conv_b', 'grad_w_down', 'grad_ln3_g', 'grad_ln3_b', 'delta_w_in', 'delta_rel_bias', 'delta_swa_sinks', 'delta_gdn_conv_w', 'delta_gdn_a_log', 'delta_gdn_dt_bias', 'delta_gdn_norm_w', 'delta_w_br_swa', 'delta_w_br_gdn', 'delta_w_mix_o', 'delta_ln1_g', 'delta_ln1_b', 'delta_w_mem_q', 'delta_w_mem_kv', 'delta_w_mem_o', 'delta_ln2_g', 'delta_ln2_b', 'delta_w_up', 'delta_ffn_conv_w', 'delta_ffn_conv_b', 'delta_w_down', 'delta_ln3_g', 'delta_ln3_b', 'new_m_w_in', 'new_m_rel_bias', 'new_m_swa_sinks', 'new_m_gdn_conv_w', 'new_m_gdn_a_log', 'new_m_gdn_dt_bias', 'new_m_gdn_norm_w', 'new_m_w_br_swa', 'new_m_w_br_gdn', 'new_m_w_mix_o', 'new_m_ln1_g', 'new_m_ln1_b', 'new_m_w_mem_q', 'new_m_w_mem_kv', 'new_m_w_mem_o', 'new_m_ln2_g', 'new_m_ln2_b', 'new_m_w_up', 'new_m_ffn_conv_w', 'new_m_ffn_conv_b', 'new_m_w_down', 'new_m_ln3_g', 'new_m_ln3_b', 'new_v_w_in', 'new_v_rel_bias', 'new_v_swa_sinks', 'new_v_gdn_conv_w', 'new_v_gdn_a_log', 'new_v_gdn_dt_bias', 'new_v_gdn_norm_w', 'new_v_w_br_swa', 'new_v_w_br_gdn', 'new_v_w_mix_o', 'new_v_ln1_g', 'new_v_ln1_b', 'new_v_w_mem_q', 'new_v_w_mem_kv', 'new_v_w_mem_o', 'new_v_ln2_g', 'new_v_ln2_b', 'new_v_w_up', 'new_v_ffn_conv_w', 'new_v_ffn_conv_b', 'new_v_w_down', 'new_v_ln3_g', 'new_v_ln3_b']
TWIN_LEAF_KINDS = {'loss': 'loss', 'grad_x': 'grad_x', 'grad_w_in': 'grad_w', 'grad_rel_bias': 'grad_w', 'grad_swa_sinks': 'grad_w', 'grad_gdn_conv_w': 'grad_w', 'grad_gdn_a_log': 'grad_w', 'grad_gdn_dt_bias': 'grad_w', 'grad_gdn_norm_w': 'grad_w', 'grad_w_br_swa': 'grad_w', 'grad_w_br_gdn': 'grad_w', 'grad_w_mix_o': 'grad_w', 'grad_ln1_g': 'grad_w', 'grad_ln1_b': 'grad_w', 'grad_w_mem_q': 'grad_w', 'grad_w_mem_kv': 'grad_w', 'grad_w_mem_o': 'grad_w', 'grad_ln2_g': 'grad_w', 'grad_ln2_b': 'grad_w', 'grad_w_up': 'grad_w', 'grad_ffn_conv_w': 'grad_w', 'grad_ffn_conv_b': 'grad_w', 'grad_w_down': 'grad_w', 'grad_ln3_g': 'grad_w', 'grad_ln3_b': 'grad_w', 'delta_w_in': 'delta_w', 'delta_rel_bias': 'delta_w', 'delta_swa_sinks': 'delta_w', 'delta_gdn_conv_w': 'delta_w', 'delta_gdn_a_log': 'delta_w', 'delta_gdn_dt_bias': 'delta_w', 'delta_gdn_norm_w': 'delta_w', 'delta_w_br_swa': 'delta_w', 'delta_w_br_gdn': 'delta_w', 'delta_w_mix_o': 'delta_w', 'delta_ln1_g': 'delta_w', 'delta_ln1_b': 'delta_w', 'delta_w_mem_q': 'delta_w', 'delta_w_mem_kv': 'delta_w', 'delta_w_mem_o': 'delta_w', 'delta_ln2_g': 'delta_w', 'delta_ln2_b': 'delta_w', 'delta_w_up': 'delta_w', 'delta_ffn_conv_w': 'delta_w', 'delta_ffn_conv_b': 'delta_w', 'delta_w_down': 'delta_w', 'delta_ln3_g': 'delta_w', 'delta_ln3_b': 'delta_w', 'new_m_w_in': 'new_m', 'new_m_rel_bias': 'new_m', 'new_m_swa_sinks': 'new_m', 'new_m_gdn_conv_w': 'new_m', 'new_m_gdn_a_log': 'new_m', 'new_m_gdn_dt_bias': 'new_m', 'new_m_gdn_norm_w': 'new_m', 'new_m_w_br_swa': 'new_m', 'new_m_w_br_gdn': 'new_m', 'new_m_w_mix_o': 'new_m', 'new_m_ln1_g': 'new_m', 'new_m_ln1_b': 'new_m', 'new_m_w_mem_q': 'new_m', 'new_m_w_mem_kv': 'new_m', 'new_m_w_mem_o': 'new_m', 'new_m_ln2_g': 'new_m', 'new_m_ln2_b': 'new_m', 'new_m_w_up': 'new_m', 'new_m_ffn_conv_w': 'new_m', 'new_m_ffn_conv_b': 'new_m', 'new_m_w_down': 'new_m', 'new_m_ln3_g': 'new_m', 'new_m_ln3_b': 'new_m', 'new_v_w_in': 'new_v', 'new_v_rel_bias': 'new_v', 'new_v_swa_sinks': 'new_v', 'new_v_gdn_conv_w': 'new_v', 'new_v_gdn_a_log': 'new_v', 'new_v_gdn_dt_bias': 'new_v', 'new_v_gdn_norm_w': 'new_v', 'new_v_w_br_swa': 'new_v', 'new_v_w_br_gdn': 'new_v', 'new_v_w_mix_o': 'new_v', 'new_v_ln1_g': 'new_v', 'new_v_ln1_b': 'new_v', 'new_v_w_mem_q': 'new_v', 'new_v_w_mem_kv': 'new_v', 'new_v_w_mem_o': 'new_v', 'new_v_ln2_g': 'new_v', 'new_v_ln2_b': 'new_v', 'new_v_w_up': 'new_v', 'new_v_ffn_conv_w': 'new_v', 'new_v_ffn_conv_b': 'new_v', 'new_v_w_down': 'new_v', 'new_v_ln3_g': 'new_v', 'new_v_ln3_b': 'new_v'}


def _forward(args):
    return _fwd_reference(*[args[k] for k in FWD_PARAMS])


def _output_shape():
    def fwd():
        inp = _fwd_setup_inputs(0)
        return _fwd_reference(*[inp[k] for k in FWD_PARAMS])
    out = _jax.eval_shape(fwd)
    return out.shape, out.dtype

N_MICROBATCH = 1
ADAM_LR = 0.001
ADAM_B1 = 0.9
ADAM_B2 = 0.999
ADAM_EPS = 1e-08
ADAM_WD = 0.01
ADAM_STEP = 10
PER_EXAMPLE_BATCH_AXIS = {'x': 0, 'mem': 0, 'loss_target': 0}
SHARED_INPUTS = []
_WEIGHT_DTYPES = {'w_in': _jnp.float32, 'rel_bias': _jnp.float32, 'swa_sinks': _jnp.float32, 'gdn_conv_w': _jnp.float32, 'gdn_a_log': _jnp.float32, 'gdn_dt_bias': _jnp.float32, 'gdn_norm_w': _jnp.float32, 'w_br_swa': _jnp.float32, 'w_br_gdn': _jnp.float32, 'w_mix_o': _jnp.float32, 'ln1_g': _jnp.float32, 'ln1_b': _jnp.float32, 'w_mem_q': _jnp.float32, 'w_mem_kv': _jnp.float32, 'w_mem_o': _jnp.float32, 'ln2_g': _jnp.float32, 'ln2_b': _jnp.float32, 'w_up': _jnp.float32, 'ffn_conv_w': _jnp.float32, 'ffn_conv_b': _jnp.float32, 'w_down': _jnp.float32, 'ln3_g': _jnp.float32, 'ln3_b': _jnp.float32}
MOMENT_SCALE = {'w_in': 1.731031e-02, 'rel_bias': 1.269791e-02, 'swa_sinks': 6.391685e-03, 'gdn_conv_w': 2.290206e-02, 'gdn_a_log': 1.623494e-01, 'gdn_dt_bias': 1.598558e-01, 'gdn_norm_w': 8.241655e-02, 'w_br_swa': 7.695908e-03, 'w_br_gdn': 2.353806e-02, 'w_mix_o': 3.924554e-02, 'ln1_g': 1.023871e+00, 'ln1_b': 4.871007e-01, 'w_mem_q': 1.346006e-02, 'w_mem_kv': 1.463350e-02, 'w_mem_o': 1.310650e-02, 'ln2_g': 1.031198e+00, 'ln2_b': 4.884717e-01, 'w_up': 2.274331e-02, 'ffn_conv_w': 2.290969e-02, 'ffn_conv_b': 2.871605e-02, 'w_down': 6.183746e-02, 'ln3_g': 3.200690e+01, 'ln3_b': 1.309486e+00}


def _to_microbatches(a, axis):
    t = _jnp.moveaxis(a, axis, 0)
    t = t.reshape((N_MICROBATCH, t.shape[0] // N_MICROBATCH) + t.shape[1:])
    return _jnp.moveaxis(t, 1, axis + 1)


def setup_inputs(seed: int = 0) -> dict:
    inp = _fwd_setup_inputs(seed)
    key = _jax.random.fold_in(_jax.random.key(seed), 7919)
    shape, _ = _output_shape()
    out = dict(inp)
    out["loss_target"] = _jax.random.normal(_jax.random.fold_in(key, 0), shape, _jnp.float32)
    for i, name in enumerate(TWIN_WEIGHTS):
        w = inp[name].astype(_jnp.float32)
        if MOMENT_SCALE is None:
            s = _jnp.sqrt(_jnp.mean(_jnp.square(w)) + 1e-30)
        else:
            s = MOMENT_SCALE[name]
        km, kv = _jax.random.split(_jax.random.fold_in(key, i + 1))
        out[name] = w
        out["m_" + name] = s * _jax.random.normal(km, w.shape, _jnp.float32)
        out["v_" + name] = (s * s) * _jax.random.uniform(kv, w.shape, _jnp.float32, 0.5, 1.5)
    if N_MICROBATCH > 1:
        for name, axis in PER_EXAMPLE_BATCH_AXIS.items():
            out[name] = _to_microbatches(out[name], axis)
    return {'x': out['x'], 'mem': out['mem'], 'w_in': out['w_in'], 'rel_bias': out['rel_bias'], 'swa_sinks': out['swa_sinks'], 'gdn_conv_w': out['gdn_conv_w'], 'gdn_a_log': out['gdn_a_log'], 'gdn_dt_bias': out['gdn_dt_bias'], 'gdn_norm_w': out['gdn_norm_w'], 'w_br_swa': out['w_br_swa'], 'w_br_gdn': out['w_br_gdn'], 'w_mix_o': out['w_mix_o'], 'ln1_g': out['ln1_g'], 'ln1_b': out['ln1_b'], 'w_mem_q': out['w_mem_q'], 'w_mem_kv': out['w_mem_kv'], 'w_mem_o': out['w_mem_o'], 'ln2_g': out['ln2_g'], 'ln2_b': out['ln2_b'], 'w_up': out['w_up'], 'ffn_conv_w': out['ffn_conv_w'], 'ffn_conv_b': out['ffn_conv_b'], 'w_down': out['w_down'], 'ln3_g': out['ln3_g'], 'ln3_b': out['ln3_b'], 'loss_target': out['loss_target'], 'm_w_in': out['m_w_in'], 'm_rel_bias': out['m_rel_bias'], 'm_swa_sinks': out['m_swa_sinks'], 'm_gdn_conv_w': out['m_gdn_conv_w'], 'm_gdn_a_log': out['m_gdn_a_log'], 'm_gdn_dt_bias': out['m_gdn_dt_bias'], 'm_gdn_norm_w': out['m_gdn_norm_w'], 'm_w_br_swa': out['m_w_br_swa'], 'm_w_br_gdn': out['m_w_br_gdn'], 'm_w_mix_o': out['m_w_mix_o'], 'm_ln1_g': out['m_ln1_g'], 'm_ln1_b': out['m_ln1_b'], 'm_w_mem_q': out['m_w_mem_q'], 'm_w_mem_kv': out['m_w_mem_kv'], 'm_w_mem_o': out['m_w_mem_o'], 'm_ln2_g': out['m_ln2_g'], 'm_ln2_b': out['m_ln2_b'], 'm_w_up': out['m_w_up'], 'm_ffn_conv_w': out['m_ffn_conv_w'], 'm_ffn_conv_b': out['m_ffn_conv_b'], 'm_w_down': out['m_w_down'], 'm_ln3_g': out['m_ln3_g'], 'm_ln3_b': out['m_ln3_b'], 'v_w_in': out['v_w_in'], 'v_rel_bias': out['v_rel_bias'], 'v_swa_sinks': out['v_swa_sinks'], 'v_gdn_conv_w': out['v_gdn_conv_w'], 'v_gdn_a_log': out['v_gdn_a_log'], 'v_gdn_dt_bias': out['v_gdn_dt_bias'], 'v_gdn_norm_w': out['v_gdn_norm_w'], 'v_w_br_swa': out['v_w_br_swa'], 'v_w_br_gdn': out['v_w_br_gdn'], 'v_w_mix_o': out['v_w_mix_o'], 'v_ln1_g': out['v_ln1_g'], 'v_ln1_b': out['v_ln1_b'], 'v_w_mem_q': out['v_w_mem_q'], 'v_w_mem_kv': out['v_w_mem_kv'], 'v_w_mem_o': out['v_w_mem_o'], 'v_ln2_g': out['v_ln2_g'], 'v_ln2_b': out['v_ln2_b'], 'v_w_up': out['v_w_up'], 'v_ffn_conv_w': out['v_ffn_conv_w'], 'v_ffn_conv_b': out['v_ffn_conv_b'], 'v_w_down': out['v_w_down'], 'v_ln3_g': out['v_ln3_g'], 'v_ln3_b': out['v_ln3_b']}


def _loss(weights, diff, rest, loss_target):
    with _jax.named_scope("forward"):
        args = {**rest, TWIN_DIFF_INPUT: diff, **{k: w.astype(_WEIGHT_DTYPES[k]) for k, w in weights.items()}}
        y = _forward(args)
    with _jax.named_scope("loss_head"):
        err = _jnp.square(y.astype(_jnp.float32) - loss_target)
        return 0.5 * _jnp.sum(_jnp.mean(err, axis=-1)) if err.ndim else 0.5 * err


def _adamw(w, g, m, v):
    m = ADAM_B1 * m + (1.0 - ADAM_B1) * g
    v = ADAM_B2 * v + (1.0 - ADAM_B2) * _jnp.square(g)
    m_hat = m / (1.0 - ADAM_B1 ** ADAM_STEP)
    v_hat = v / (1.0 - ADAM_B2 ** ADAM_STEP)
    delta = -ADAM_LR * (m_hat / (_jnp.sqrt(v_hat) + ADAM_EPS) + ADAM_WD * w)
    return delta, m, v


def reference(x, mem, w_in, rel_bias, swa_sinks, gdn_conv_w, gdn_a_log, gdn_dt_bias, gdn_norm_w, w_br_swa, w_br_gdn, w_mix_o, ln1_g, ln1_b, w_mem_q, w_mem_kv, w_mem_o, ln2_g, ln2_b, w_up, ffn_conv_w, ffn_conv_b, w_down, ln3_g, ln3_b, loss_target, m_w_in, m_rel_bias, m_swa_sinks, m_gdn_conv_w, m_gdn_a_log, m_gdn_dt_bias, m_gdn_norm_w, m_w_br_swa, m_w_br_gdn, m_w_mix_o, m_ln1_g, m_ln1_b, m_w_mem_q, m_w_mem_kv, m_w_mem_o, m_ln2_g, m_ln2_b, m_w_up, m_ffn_conv_w, m_ffn_conv_b, m_w_down, m_ln3_g, m_ln3_b, v_w_in, v_rel_bias, v_swa_sinks, v_gdn_conv_w, v_gdn_a_log, v_gdn_dt_bias, v_gdn_norm_w, v_w_br_swa, v_w_br_gdn, v_w_mix_o, v_ln1_g, v_ln1_b, v_w_mem_q, v_w_mem_kv, v_w_mem_o, v_ln2_g, v_ln2_b, v_w_up, v_ffn_conv_w, v_ffn_conv_b, v_w_down, v_ln3_g, v_ln3_b):
    given = dict(x=x, mem=mem, w_in=w_in, rel_bias=rel_bias, swa_sinks=swa_sinks, gdn_conv_w=gdn_conv_w, gdn_a_log=gdn_a_log, gdn_dt_bias=gdn_dt_bias, gdn_norm_w=gdn_norm_w, w_br_swa=w_br_swa, w_br_gdn=w_br_gdn, w_mix_o=w_mix_o, ln1_g=ln1_g, ln1_b=ln1_b, w_mem_q=w_mem_q, w_mem_kv=w_mem_kv, w_mem_o=w_mem_o, ln2_g=ln2_g, ln2_b=ln2_b, w_up=w_up, ffn_conv_w=ffn_conv_w, ffn_conv_b=ffn_conv_b, w_down=w_down, ln3_g=ln3_g, ln3_b=ln3_b, loss_target=loss_target, m_w_in=m_w_in, m_rel_bias=m_rel_bias, m_swa_sinks=m_swa_sinks, m_gdn_conv_w=m_gdn_conv_w, m_gdn_a_log=m_gdn_a_log, m_gdn_dt_bias=m_gdn_dt_bias, m_gdn_norm_w=m_gdn_norm_w, m_w_br_swa=m_w_br_swa, m_w_br_gdn=m_w_br_gdn, m_w_mix_o=m_w_mix_o, m_ln1_g=m_ln1_g, m_ln1_b=m_ln1_b, m_w_mem_q=m_w_mem_q, m_w_mem_kv=m_w_mem_kv, m_w_mem_o=m_w_mem_o, m_ln2_g=m_ln2_g, m_ln2_b=m_ln2_b, m_w_up=m_w_up, m_ffn_conv_w=m_ffn_conv_w, m_ffn_conv_b=m_ffn_conv_b, m_w_down=m_w_down, m_ln3_g=m_ln3_g, m_ln3_b=m_ln3_b, v_w_in=v_w_in, v_rel_bias=v_rel_bias, v_swa_sinks=v_swa_sinks, v_gdn_conv_w=v_gdn_conv_w, v_gdn_a_log=v_gdn_a_log, v_gdn_dt_bias=v_gdn_dt_bias, v_gdn_norm_w=v_gdn_norm_w, v_w_br_swa=v_w_br_swa, v_w_br_gdn=v_w_br_gdn, v_w_mix_o=v_w_mix_o, v_ln1_g=v_ln1_g, v_ln1_b=v_ln1_b, v_w_mem_q=v_w_mem_q, v_w_mem_kv=v_w_mem_kv, v_w_mem_o=v_w_mem_o, v_ln2_g=v_ln2_g, v_ln2_b=v_ln2_b, v_w_up=v_w_up, v_ffn_conv_w=v_ffn_conv_w, v_ffn_conv_b=v_ffn_conv_b, v_w_down=v_w_down, v_ln3_g=v_ln3_g, v_ln3_b=v_ln3_b)
    weights = {n: given[n] for n in TWIN_WEIGHTS}
    shared = {n: given[n] for n in SHARED_INPUTS}
    per_example = {n: given[n] for n in ['x', 'mem']}
    grad_fn = _jax.value_and_grad(_loss, argnums=(0, 1))

    def one_microbatch(ex, loss_target):
        ex = dict(ex)
        diff = ex.pop(TWIN_DIFF_INPUT)
        return grad_fn(weights, diff, {**shared, **ex}, loss_target)

    if N_MICROBATCH == 1:
        loss, (grad_w, grad_x) = one_microbatch(per_example, given["loss_target"])
    else:
        def body(carry, xs):
            loss_sum, grad_sum = carry
            l_k, (gw_k, gx_k) = one_microbatch(xs[0], xs[1])
            with _jax.named_scope("update"):
                return (loss_sum + l_k, _jax.tree.map(_jnp.add, grad_sum, gw_k)), gx_k

        init = (_jnp.zeros((), _jnp.float32), _jax.tree.map(_jnp.zeros_like, weights))
        (loss, grad_w), grad_x = _jax.lax.scan(body, init, (per_example, given["loss_target"]))
    with _jax.named_scope("update"):
        delta_w, new_m, new_v = {}, {}, {}
        for n in TWIN_WEIGHTS:
            delta_w[n], new_m[n], new_v[n] = _adamw(weights[n], grad_w[n], given["m_" + n], given["v_" + n])
    return (loss, grad_x, *[grad_w[n] for n in TWIN_WEIGHTS], *[delta_w[n] for n in TWIN_WEIGHTS],
            *[new_m[n] for n in TWIN_WEIGHTS], *[new_v[n] for n in TWIN_WEIGHTS])
```

```python
import functools
import math

import jax
import jax.numpy as jnp
from jax import lax
from jax.experimental import pallas as pl
from jax.experimental.pallas import tpu as pltpu

f32 = jnp.float32
bf16 = jnp.bfloat16
HI = lax.Precision.HIGHEST
MESH = pl.DeviceIdType.MESH

D_MODEL = 2048
N_DEV = 8
SWA_HEADS, SWA_KV_HEADS, SWA_HEAD_DIM, BLOCK = 16, 2, 64, 128
REL_BUCKETS, REL_MAX_DIST = 32, 128
GDN_HEADS, GDN_HEAD_DIM, GDN_CONV, GDN_CHUNK = 8, 128, 4, 64
MEM_HEADS, MEM_HEAD_DIM = 4, 128
D_FF = 5504
FF_SHARD = 2 * D_FF // N_DEV
FF_PAD = 1408
NORM_EPS = 1e-5
ALPHA = 2.0 ** 0.25
NEG_INF = -1e30
SWA_Q, SWA_KV, GDN_W, MEM_W = 1024, 128, 1024, 512
IN_DIM = 9488
HALO = 8

ADAM_LR, ADAM_B1, ADAM_B2, ADAM_EPS, ADAM_WD, ADAM_STEP = 0.001, 0.9, 0.999, 1e-08, 0.01, 10

PACK_COLS = 1024
PACK_ROWS = 8192
SMALL_ROWS = 32

SHARDED = (
    ("w_in", (2048, 1186), 1), ("w_br_swa", (1024, 256), 1), ("w_br_gdn", (1024, 256), 1),
    ("w_mix_o", (256, 2048), 0), ("w_mem_q", (256, 512), 0), ("w_mem_kv", (256, 1024), 0),
    ("w_mem_o", (512, 256), 1), ("w_up", (2048, 1376), 1), ("w_down", (688, 2048), 0),
    ("ffn_conv_w", (3, 1376), 1), ("gdn_conv_w", (4, 384), 1),
)
SMALL = (
    ("rel_bias", (32, 16)), ("swa_sinks", (1, 16)), ("gdn_a_log", (1, 8)), ("gdn_dt_bias", (1, 8)),
    ("gdn_norm_w", (1, 128)), ("ln1_g", (1, 2048)), ("ln1_b", (1, 2048)), ("ln2_g", (1, 2048)),
    ("ln2_b", (1, 2048)), ("ln3_g", (1, 2048)), ("ln3_b", (1, 2048)), ("ffn_conv_b", (1, 11008)),
)
WEIGHTS = ("w_in", "rel_bias", "swa_sinks", "gdn_conv_w", "gdn_a_log", "gdn_dt_bias", "gdn_norm_w", "w_br_swa",
           "w_br_gdn", "w_mix_o", "ln1_g", "ln1_b", "w_mem_q", "w_mem_kv", "w_mem_o", "ln2_g", "ln2_b", "w_up",
           "ffn_conv_w", "ffn_conv_b", "w_down", "ln3_g", "ln3_b")


def _tile(n, target, align):
    if n <= target:
        return n
    t = (target // align) * align
    while t >= align:
        if n % t == 0:
            return t
        t -= align
    return n


VMEM_LIMIT_BYTES = 56 * 1024 * 1024


def _params(*sem):
    return pltpu.CompilerParams(dimension_semantics=sem, vmem_limit_bytes=VMEM_LIMIT_BYTES)


def _sigmoid(v):
    return jax.nn.sigmoid(v)


def _d16(a, b, dims):
    return lax.dot_general(a.astype(bf16), b.astype(bf16), (dims, ((), ())), preferred_element_type=f32)


NN = ((1,), (0,))
NT = ((1,), (1,))
TN = ((0,), (0,))


def mm(a, b, mode, *, name, add=None, add_scale=1.0, out_dtype=f32, hi=False, tm=1024, tn=1024, tk=512):
    if mode == "nn":
        (m, k), (_, n) = a.shape, b.shape
    elif mode == "nt":
        (m, k), (n, _) = a.shape, b.shape
    else:
        (k, m), (_, n) = a.shape, b.shape
    tm, tn, tk = _tile(m, tm, 8 if mode != "tn" else 128), _tile(n, tn, 128), _tile(k, tk, 128 if mode != "tn" else 8)
    nk = k // tk
    dims = {"nn": NN, "nt": NT, "tn": TN}[mode]
    a_spec = pl.BlockSpec((tk, tm), lambda i, j, kk: (kk, i)) if mode == "tn" else pl.BlockSpec((tm, tk), lambda i, j, kk: (i, kk))
    b_spec = pl.BlockSpec((tn, tk), lambda i, j, kk: (j, kk)) if mode == "nt" else pl.BlockSpec((tk, tn), lambda i, j, kk: (kk, j))
    o_spec = pl.BlockSpec((tm, tn), lambda i, j, kk: (i, j))
    has_add = add is not None

    def body(*refs):
        if has_add:
            a_ref, b_ref, add_ref, o_ref, acc_ref = refs
        else:
            a_ref, b_ref, o_ref, acc_ref = refs
        kk = pl.program_id(2)

        @pl.when(kk == 0)
        def _():
            acc_ref[...] = jnp.zeros_like(acc_ref)

        if hi:
            acc_ref[...] += lax.dot_general(a_ref[...], b_ref[...], (dims, ((), ())), precision=HI,
                                            preferred_element_type=f32)
        else:
            acc_ref[...] += _d16(a_ref[...], b_ref[...], dims)

        @pl.when(kk == nk - 1)
        def _():
            r = acc_ref[...]
            if has_add:
                r = r + add_scale * add_ref[...]
            o_ref[...] = r.astype(out_dtype)

    return pl.pallas_call(
        body, name=name, grid=(m // tm, n // tn, nk),
        in_specs=[a_spec, b_spec] + ([o_spec] if has_add else []), out_specs=o_spec,
        out_shape=jax.ShapeDtypeStruct((m, n), out_dtype),
        scratch_shapes=[pltpu.VMEM((tm, tn), f32)],
        compiler_params=_params("parallel", "parallel", "arbitrary"),
    )(*((a, b, add) if has_add else (a, b)))


def cast_bf16(a, *, name):
    m, n = a.shape
    tm = _tile(m, 512, 16)

    def body(a_ref, o_ref):
        o_ref[...] = a_ref[...].astype(bf16)

    return pl.pallas_call(body, name=name, grid=(m // tm,), in_specs=[pl.BlockSpec((tm, n), lambda i: (i, 0))],
                          out_specs=pl.BlockSpec((tm, n), lambda i: (i, 0)), out_shape=jax.ShapeDtypeStruct((m, n), bf16),
                          compiler_params=_params("parallel"))(a)


def _ln_stats(z):
    mu = jnp.mean(z, axis=-1, keepdims=True)
    zc = z - mu
    var = jnp.mean(zc * zc, axis=-1, keepdims=True)
    rstd = lax.rsqrt(var + NORM_EPS)
    return zc * rstd, rstd


def ln_fwd(z, g, b, *, name):
    s, d = z.shape
    tm = _tile(s, 256, 16)

    def body(z_ref, g_ref, b_ref, y_ref, yb_ref):
        xhat, _ = _ln_stats(z_ref[...])
        y = xhat * g_ref[...] + b_ref[...]
        y_ref[...] = y
        yb_ref[...] = y.astype(bf16)

    row = pl.BlockSpec((tm, d), lambda i: (i, 0))
    vec = pl.BlockSpec((1, d), lambda i: (0, 0))
    return pl.pallas_call(body, name=name, grid=(s // tm,), in_specs=[row, vec, vec], out_specs=[row, row],
                          out_shape=[jax.ShapeDtypeStruct((s, d), f32), jax.ShapeDtypeStruct((s, d), bf16)],
                          compiler_params=_params("parallel"))(z, g, b)


def _ln_bwd_tile(dy, z, g):
    xhat, rstd = _ln_stats(z)
    dxh = dy * g
    m1 = jnp.mean(dxh, axis=-1, keepdims=True)
    m2 = jnp.mean(dxh * xhat, axis=-1, keepdims=True)
    dz = rstd * (dxh - m1 - xhat * m2)
    return dz, jnp.sum(dy * xhat, axis=0, keepdims=True), jnp.sum(dy, axis=0, keepdims=True)


def ln_bwd(dy, z, g, *, name):
    s, d = z.shape
    tm = _tile(s, 256, 16)

    def body(dy_ref, z_ref, g_ref, dz_ref, dzb_ref, dg_ref, db_ref):
        @pl.when(pl.program_id(0) == 0)
        def _():
            dg_ref[...] = jnp.zeros_like(dg_ref)
            db_ref[...] = jnp.zeros_like(db_ref)

        dz, dg, db = _ln_bwd_tile(dy_ref[...], z_ref[...], g_ref[...])
        dz_ref[...] = dz
        dzb_ref[...] = dz.astype(bf16)
        dg_ref[...] += dg
        db_ref[...] += db

    row = pl.BlockSpec((tm, d), lambda i: (i, 0))
    vec = pl.BlockSpec((1, d), lambda i: (0, 0))
    return pl.pallas_call(body, name=name, grid=(s // tm,), in_specs=[row, row, vec], out_specs=[row, row, vec, vec],
                          out_shape=[jax.ShapeDtypeStruct((s, d), f32), jax.ShapeDtypeStruct((s, d), bf16),
                                     jax.ShapeDtypeStruct((1, d), f32), jax.ShapeDtypeStruct((1, d), f32)],
                          compiler_params=_params("arbitrary"))(dy, z, g)


def ln_loss(z, target, g, b, *, name):
    s, d = z.shape
    tm = _tile(s, 256, 16)
    nt = s // tm

    def body(z_ref, t_ref, g_ref, b_ref, dz_ref, dzb_ref, dg_ref, db_ref, loss_ref, lacc_ref):
        i = pl.program_id(0)

        @pl.when(i == 0)
        def _():
            dg_ref[...] = jnp.zeros_like(dg_ref)
            db_ref[...] = jnp.zeros_like(db_ref)
            lacc_ref[...] = jnp.zeros_like(lacc_ref)

        zv, gv = z_ref[...], g_ref[...]
        xhat, _ = _ln_stats(zv)
        err = xhat * gv + b_ref[...] - t_ref[...]
        lacc_ref[...] += jnp.sum(err * err, axis=0, keepdims=True)
        dz, dg, db = _ln_bwd_tile(err * (1.0 / d), zv, gv)
        dz_ref[...] = dz
        dzb_ref[...] = dz.astype(bf16)
        dg_ref[...] += dg
        db_ref[...] += db

        @pl.when(i == nt - 1)
        def _():
            loss_ref[...] = (0.5 / d) * jnp.sum(lacc_ref[...], axis=1, keepdims=True)

    row = pl.BlockSpec((tm, d), lambda i: (i, 0))
    vec = pl.BlockSpec((1, d), lambda i: (0, 0))
    return pl.pallas_call(body, name=name, grid=(nt,), in_specs=[row, row, vec, vec],
                          out_specs=[row, row, vec, vec, pl.BlockSpec((1, 1), lambda i: (0, 0))],
                          out_shape=[jax.ShapeDtypeStruct((s, d), f32), jax.ShapeDtypeStruct((s, d), bf16),
                                     jax.ShapeDtypeStruct((1, d), f32), jax.ShapeDtypeStruct((1, d), f32),
                                     jax.ShapeDtypeStruct((1, 1), f32)],
                          scratch_shapes=[pltpu.VMEM((1, d), f32)],
                          compiler_params=_params("arbitrary"))(z, target, g, b)


def merge_fwd(gates, ys, yg, *, name):
    s, d = ys.shape
    tm = _tile(s, 256, 16)

    def body(gt_ref, ys_ref, yg_ref, o_ref):
        o_ref[...] = (_sigmoid(gt_ref[:, :d]) * ys_ref[...] + _sigmoid(gt_ref[:, d:]) * yg_ref[...]).astype(bf16)

    row = pl.BlockSpec((tm, d), lambda i: (i, 0))
    return pl.pallas_call(body, name=name, grid=(s // tm,), in_specs=[pl.BlockSpec((tm, 2 * d), lambda i: (i, 0)), row, row],
                          out_specs=row, out_shape=jax.ShapeDtypeStruct((s, d), bf16),
                          compiler_params=_params("parallel"))(gates, ys, yg)


def merge_bwd(gates, ys, yg, dmix, *, name):
    s, d = ys.shape
    tm = _tile(s, 256, 16)

    def body(gt_ref, ys_ref, yg_ref, dm_ref, dys_ref, dyg_ref, dgt_ref):
        dm = dm_ref[...]
        sa, sb = _sigmoid(gt_ref[:, :d]), _sigmoid(gt_ref[:, d:])
        dys_ref[...] = (dm * sa).astype(bf16)
        dyg_ref[...] = (dm * sb).astype(bf16)
        dgt_ref[:, :d] = (dm * ys_ref[...] * sa * (1.0 - sa)).astype(bf16)
        dgt_ref[:, d:] = (dm * yg_ref[...] * sb * (1.0 - sb)).astype(bf16)

    row = pl.BlockSpec((tm, d), lambda i: (i, 0))
    wide = pl.BlockSpec((tm, 2 * d), lambda i: (i, 0))
    return pl.pallas_call(body, name=name, grid=(s // tm,), in_specs=[wide, row, row, row], out_specs=[row, row, wide],
                          out_shape=[jax.ShapeDtypeStruct((s, d), bf16), jax.ShapeDtypeStruct((s, d), bf16),
                                     jax.ShapeDtypeStruct((s, 2 * d), bf16)],
                          compiler_params=_params("parallel"))(gates, ys, yg, dmix)


def _shift_down(ext, j):
    return ext if j == 0 else pltpu.roll(ext, j, 0)


def _shift_up(ext, j):
    return ext if j == 0 else pltpu.roll(ext, ext.shape[0] - j, 0)


def _causal_conv(ext, w_ref, width):
    acc = None
    for j in range(width):
        term = w_ref[j:j + 1, :] * _shift_down(ext, width - 1 - j)
        acc = term if acc is None else acc + term
    return acc[HALO:]


def _conv_input_grad(dy_ext, w_ref, width, rows):
    acc = None
    for j in range(width):
        term = w_ref[j:j + 1, :] * _shift_up(dy_ext, width - 1 - j)
        acc = term if acc is None else acc + term
    return acc[:rows]


def _conv_weight_grad(dy, ext, width, rows):
    return [jnp.sum(dy * _shift_down(ext, width - 1 - j)[HALO:HALO + rows], axis=0, keepdims=True) for j in range(width)]


def _rows_to_block(rows, n_rows, cols):
    r = lax.broadcasted_iota(jnp.int32, (n_rows, cols), 0)
    out = jnp.zeros((n_rows, cols), f32)
    for j, v in enumerate(rows):
        out = out + jnp.where(r == j, v, 0.0)
    return out


def _silu_and_grad(v):
    sg = _sigmoid(v)
    return v * sg, sg * (1.0 + v * (1.0 - sg))


def ffn_act_fwd(hpre, cwb, *, name):
    s = hpre.shape[0]
    tm = _tile(s, 256, 16)
    hb = tm // HALO

    def body(hg_ref, hgp_ref, hu_ref, hup_ref, cg_ref, cu_ref, o_ref):
        first = pl.program_id(1) == 0

        def conv(h_ref, hp_ref, c_ref):
            prev = jnp.where(first, 0.0, hp_ref[...])
            return _causal_conv(jnp.concatenate([prev, h_ref[...]], axis=0), c_ref.at[0], 3) + c_ref[0, 3:4, :]

        g = conv(hg_ref, hgp_ref, cg_ref)
        u = conv(hu_ref, hup_ref, cu_ref)
        o_ref[...] = (g * _sigmoid(g) * u).astype(bf16)

    def tile(off):
        return pl.BlockSpec((tm, FF_PAD), lambda j, i: (i, j + off))

    def halo(off):
        return pl.BlockSpec((HALO, FF_PAD), lambda j, i: (jnp.maximum(i * hb - 1, 0), j + off))

    def taps(off):
        return pl.BlockSpec((1, 8, FF_PAD), lambda j, i: (j + off, 0, 0))

    return pl.pallas_call(body, name=name, grid=(4, s // tm),
                          in_specs=[tile(0), halo(0), tile(4), halo(4), taps(0), taps(4)],
                          out_specs=pl.BlockSpec((tm, FF_PAD), lambda j, i: (i, j)),
                          out_shape=jax.ShapeDtypeStruct((s, 4 * FF_PAD), bf16),
                          compiler_params=_params("parallel", "parallel"))(hpre, hpre, hpre, hpre, cwb, cwb)


def ffn_act_bwd(hpre, dact, cwb, *, name):
    s = hpre.shape[0]
    tm = _tile(s, 256, 16)
    hb = tm // HALO
    nt = s // tm
    last_hb = s // HALO - 1

    def body(hg_ref, hgp_ref, hgn_ref, hu_ref, hup_ref, hun_ref, d_ref, dn_ref, cg_ref, cu_ref,
             dhg_ref, dhu_ref, dcg_ref, dcu_ref):
        i = pl.program_id(1)
        first, last = i == 0, i == nt - 1

        @pl.when(first)
        def _():
            dcg_ref[...] = jnp.zeros_like(dcg_ref)
            dcu_ref[...] = jnp.zeros_like(dcu_ref)

        def ext_of(h_ref, hp_ref, hn_ref):
            return jnp.concatenate([jnp.where(first, 0.0, hp_ref[...]), h_ref[...], hn_ref[...]], axis=0)

        eg, eu = ext_of(hg_ref, hgp_ref, hgn_ref), ext_of(hu_ref, hup_ref, hun_ref)
        g = _causal_conv(eg, cg_ref.at[0], 3) + cg_ref[0, 3:4, :]
        u = _causal_conv(eu, cu_ref.at[0], 3) + cu_ref[0, 3:4, :]
        d = jnp.concatenate([d_ref[...], jnp.where(last, 0.0, dn_ref[...])], axis=0)
        act, dact_dg = _silu_and_grad(g)
        dg = d * u * dact_dg
        du = d * act
        dhg_ref[...] = _conv_input_grad(dg, cg_ref.at[0], 3, tm).astype(bf16)
        dhu_ref[...] = _conv_input_grad(du, cu_ref.at[0], 3, tm).astype(bf16)
        dgt, dut = dg[:tm], du[:tm]
        dcg_ref[0] += _rows_to_block(_conv_weight_grad(dgt, eg, 3, tm) + [jnp.sum(dgt, axis=0, keepdims=True)], 8, FF_PAD)
        dcu_ref[0] += _rows_to_block(_conv_weight_grad(dut, eu, 3, tm) + [jnp.sum(dut, axis=0, keepdims=True)], 8, FF_PAD)

    def tile(off):
        return pl.BlockSpec((tm, FF_PAD), lambda j, i: (i, j + off))

    def prev(off):
        return pl.BlockSpec((HALO, FF_PAD), lambda j, i: (jnp.maximum(i * hb - 1, 0), j + off))

    def nxt(off):
        return pl.BlockSpec((HALO, FF_PAD), lambda j, i: (jnp.minimum((i + 1) * hb, last_hb), j + off))

    def taps(off):
        return pl.BlockSpec((1, 8, FF_PAD), lambda j, i: (j + off, 0, 0))

    dhg, dhu, dcg, dcu = pl.pallas_call(
        body, name=name, grid=(4, nt),
        in_specs=[tile(0), prev(0), nxt(0), tile(4), prev(4), nxt(4), tile(0), nxt(0), taps(0), taps(4)],
        out_specs=[tile(0), tile(0), taps(0), taps(0)],
        out_shape=[jax.ShapeDtypeStruct((s, 4 * FF_PAD), bf16), jax.ShapeDtypeStruct((s, 4 * FF_PAD), bf16),
                   jax.ShapeDtypeStruct((4, 8, FF_PAD), f32), jax.ShapeDtypeStruct((4, 8, FF_PAD), f32)],
        compiler_params=_params("parallel", "arbitrary"),
    )(hpre, hpre, hpre, hpre, hpre, hpre, dact, dact, cwb, cwb)
    return jnp.concatenate([dhg, dhu], axis=1), jnp.concatenate([dcg, dcu], axis=0)


MEM_SCALE = MEM_HEAD_DIM ** -0.5


def _softmax_rows(sc):
    m = jnp.max(sc, axis=-1, keepdims=True)
    e = jnp.exp(sc - m)
    return e / jnp.sum(e, axis=-1, keepdims=True)


def memattn_fwd(qm, kv, *, name):
    s = qm.shape[0]
    mlen = kv.shape[0]
    tm = _tile(s, 512, 16)

    def body(q_ref, kv_ref, o_ref):
        for h in range(MEM_HEADS):
            lo = h * MEM_HEAD_DIM
            q = q_ref[:, lo:lo + MEM_HEAD_DIM]
            k = kv_ref[:, lo:lo + MEM_HEAD_DIM]
            v = kv_ref[:, MEM_W + lo:MEM_W + lo + MEM_HEAD_DIM]
            p = _softmax_rows(_d16(q, k, NT) * MEM_SCALE)
            o_ref[:, lo:lo + MEM_HEAD_DIM] = _d16(p, v, NN).astype(bf16)

    return pl.pallas_call(body, name=name, grid=(s // tm,),
                          in_specs=[pl.BlockSpec((tm, MEM_W), lambda i: (i, 0)), pl.BlockSpec((mlen, 2 * MEM_W), lambda i: (0, 0))],
                          out_specs=pl.BlockSpec((tm, MEM_W), lambda i: (i, 0)),
                          out_shape=jax.ShapeDtypeStruct((s, MEM_W), bf16), compiler_params=_params("parallel"))(qm, kv)


def memattn_bwd(qm, kv, dout, *, name):
    s = qm.shape[0]
    mlen = kv.shape[0]
    tm = _tile(s, 512, 16)

    def body(q_ref, kv_ref, do_ref, dq_ref, dkv_ref):
        @pl.when(pl.program_id(0) == 0)
        def _():
            dkv_ref[...] = jnp.zeros_like(dkv_ref)

        for h in range(MEM_HEADS):
            lo = h * MEM_HEAD_DIM
            q = q_ref[:, lo:lo + MEM_HEAD_DIM]
            k = kv_ref[:, lo:lo + MEM_HEAD_DIM]
            v = kv_ref[:, MEM_W + lo:MEM_W + lo + MEM_HEAD_DIM]
            do = do_ref[:, lo:lo + MEM_HEAD_DIM]
            p = _softmax_rows(_d16(q, k, NT) * MEM_SCALE)
            dp = _d16(do, v, NT)
            ds = p * (dp - jnp.sum(p * dp, axis=-1, keepdims=True)) * MEM_SCALE
            dq_ref[:, lo:lo + MEM_HEAD_DIM] = _d16(ds, k, NN).astype(bf16)
            dkv_ref[:, lo:lo + MEM_HEAD_DIM] += _d16(ds, q, TN)
            dkv_ref[:, MEM_W + lo:MEM_W + lo + MEM_HEAD_DIM] += _d16(p, do, TN)

    row = pl.BlockSpec((tm, MEM_W), lambda i: (i, 0))
    full = pl.BlockSpec((mlen, 2 * MEM_W), lambda i: (0, 0))
    return pl.pallas_call(body, name=name, grid=(s // tm,), in_specs=[row, full, row], out_specs=[row, full],
                          out_shape=[jax.ShapeDtypeStruct((s, MEM_W), bf16), jax.ShapeDtypeStruct((mlen, 2 * MEM_W), f32)],
                          compiler_params=_params("arbitrary"))(qm, kv, dout)


SWA_SCALE = SWA_HEAD_DIM ** -0.5
SWA_GROUP = SWA_HEADS // SWA_KV_HEADS
SWA_IN_W = 1408
K_COL, V_COL, BA_COL = SWA_Q // 128, SWA_Q // 128 + 1, SWA_Q // 128 + 2


def _swa_mask(n):
    qi = lax.broadcasted_iota(jnp.int32, (BLOCK, 2 * BLOCK), 0)
    kj = lax.broadcasted_iota(jnp.int32, (BLOCK, 2 * BLOCK), 1)
    dist = qi + BLOCK - kj
    return (dist >= 0) & (dist < BLOCK) & ((n > 0) | (kj >= BLOCK))


def _swa_probs(q, k, bias, sink, mask):
    sc = jnp.where(mask, _d16(q, k, NT) * SWA_SCALE + bias, NEG_INF)
    m = jnp.maximum(jnp.max(sc, axis=-1, keepdims=True), sink)
    e = jnp.exp(sc - m)
    es = jnp.exp(sink - m)
    inv = 1.0 / (jnp.sum(e, axis=-1, keepdims=True) + es)
    return e * inv, es * inv


def _swa_specs():
    q_spec = pl.BlockSpec((BLOCK, SWA_Q), lambda n: (n, 0))

    def band(col):
        return [pl.BlockSpec((BLOCK, SWA_KV), lambda n: (jnp.maximum(n - 1, 0), col)),
                pl.BlockSpec((BLOCK, SWA_KV), lambda n: (n, col))]

    bias_spec = pl.BlockSpec((SWA_HEADS, BLOCK, 2 * BLOCK), lambda n: (0, 0, 0))
    sink_spec = pl.BlockSpec((1, SWA_HEADS), lambda n: (0, 0))
    return [q_spec] + band(K_COL) + band(V_COL) + [bias_spec, sink_spec]


def swa_fwd(swa_in, bias, sinks, *, name):
    s = swa_in.shape[0]

    def body(q_ref, kp_ref, kc_ref, vp_ref, vc_ref, bias_ref, sink_ref, o_ref):
        mask = _swa_mask(pl.program_id(0))
        kb = jnp.concatenate([kp_ref[...], kc_ref[...]], axis=0)
        vb = jnp.concatenate([vp_ref[...], vc_ref[...]], axis=0)
        for h in range(SWA_HEADS):
            lo, kl = h * SWA_HEAD_DIM, (h // SWA_GROUP) * SWA_HEAD_DIM
            p, _ = _swa_probs(q_ref[:, lo:lo + SWA_HEAD_DIM], kb[:, kl:kl + SWA_HEAD_DIM], bias_ref[h],
                              sink_ref[:, h:h + 1], mask)
            o_ref[:, lo:lo + SWA_HEAD_DIM] = _d16(p, vb[:, kl:kl + SWA_HEAD_DIM], NN).astype(bf16)

    return pl.pallas_call(body, name=name, grid=(s // BLOCK,), in_specs=_swa_specs(),
                          out_specs=pl.BlockSpec((BLOCK, SWA_Q), lambda n: (n, 0)),
                          out_shape=jax.ShapeDtypeStruct((s, SWA_Q), bf16),
                          compiler_params=_params("parallel"))(swa_in, swa_in, swa_in, swa_in, swa_in, bias, sinks)


def swa_bwd(swa_in, bias, sinks, dout, *, name):
    s = swa_in.shape[0]

    def body(q_ref, kp_ref, kc_ref, vp_ref, vc_ref, bias_ref, sink_ref, do_ref,
             dq_ref, dkc_ref, dkp_ref, dvc_ref, dvp_ref, dbias_ref, dsink_ref):
        n = pl.program_id(0)

        @pl.when(n == 0)
        def _():
            dbias_ref[...] = jnp.zeros_like(dbias_ref)
            dsink_ref[...] = jnp.zeros_like(dsink_ref)

        mask = _swa_mask(n)
        kb = jnp.concatenate([kp_ref[...], kc_ref[...]], axis=0)
        vb = jnp.concatenate([vp_ref[...], vc_ref[...]], axis=0)
        lane = lax.broadcasted_iota(jnp.int32, (1, 128), 1)
        dsink = jnp.zeros((1, 128), f32)
        for g in range(SWA_KV_HEADS):
            kl = g * SWA_HEAD_DIM
            k, v = kb[:, kl:kl + SWA_HEAD_DIM], vb[:, kl:kl + SWA_HEAD_DIM]
            dk = jnp.zeros((2 * BLOCK, SWA_HEAD_DIM), f32)
            dv = jnp.zeros((2 * BLOCK, SWA_HEAD_DIM), f32)
            for hh in range(SWA_GROUP):
                h = g * SWA_GROUP + hh
                lo = h * SWA_HEAD_DIM
                q = q_ref[:, lo:lo + SWA_HEAD_DIM]
                do = do_ref[:, lo:lo + SWA_HEAD_DIM]
                p, ps = _swa_probs(q, k, bias_ref[h], sink_ref[:, h:h + 1], mask)
                dp = _d16(do, v, NT)
                delta = jnp.sum(p * dp, axis=-1, keepdims=True)
                ds = p * (dp - delta)
                dbias_ref[h] += ds
                dsink = dsink + jnp.where(lane == h, -jnp.sum(ps * delta, axis=0, keepdims=True), 0.0)
                dss = ds * SWA_SCALE
                dq_ref[:, lo:lo + SWA_HEAD_DIM] = _d16(dss, k, NN)
                dk = dk + _d16(dss, q, TN)
                dv = dv + _d16(p, do, TN)
            dkp_ref[:, kl:kl + SWA_HEAD_DIM] = dk[:BLOCK]
            dkc_ref[:, kl:kl + SWA_HEAD_DIM] = dk[BLOCK:]
            dvp_ref[:, kl:kl + SWA_HEAD_DIM] = dv[:BLOCK]
            dvc_ref[:, kl:kl + SWA_HEAD_DIM] = dv[BLOCK:]
        dsink_ref[...] += dsink

    qs = pl.BlockSpec((BLOCK, SWA_Q), lambda n: (n, 0))
    ks = pl.BlockSpec((BLOCK, SWA_KV), lambda n: (n, 0))
    return pl.pallas_call(
        body, name=name, grid=(s // BLOCK,), in_specs=_swa_specs() + [qs],
        out_specs=[qs, ks, ks, ks, ks, pl.BlockSpec((SWA_HEADS, BLOCK, 2 * BLOCK), lambda n: (0, 0, 0)),
                   pl.BlockSpec((1, 128), lambda n: (0, 0))],
        out_shape=[jax.ShapeDtypeStruct((s, SWA_Q), f32)] + [jax.ShapeDtypeStruct((s, SWA_KV), f32)] * 4
        + [jax.ShapeDtypeStruct((SWA_HEADS, BLOCK, 2 * BLOCK), f32), jax.ShapeDtypeStruct((1, 128), f32)],
        compiler_params=_params("arbitrary"),
    )(swa_in, swa_in, swa_in, swa_in, swa_in, bias, sinks, dout)


def swa_in_grad(dq, dkc, dkp, dvc, dvp, dba, *, name):
    s = dq.shape[0]
    nb = s // BLOCK

    def body(dq_ref, dkc_ref, dkp_ref, dvc_ref, dvp_ref, dba_ref, o_ref):
        has_next = pl.program_id(0) < nb - 1
        o_ref[:, :SWA_Q] = dq_ref[...].astype(bf16)
        o_ref[:, SWA_Q:SWA_Q + SWA_KV] = (dkc_ref[...] + jnp.where(has_next, dkp_ref[...], 0.0)).astype(bf16)
        o_ref[:, SWA_Q + SWA_KV:SWA_Q + 2 * SWA_KV] = (dvc_ref[...] + jnp.where(has_next, dvp_ref[...], 0.0)).astype(bf16)
        o_ref[:, SWA_Q + 2 * SWA_KV:] = dba_ref[...].astype(bf16)

    cur = pl.BlockSpec((BLOCK, SWA_KV), lambda n: (n, 0))
    nxt = pl.BlockSpec((BLOCK, SWA_KV), lambda n: (jnp.minimum(n + 1, nb - 1), 0))
    return pl.pallas_call(body, name=name, grid=(nb,),
                          in_specs=[pl.BlockSpec((BLOCK, SWA_Q), lambda n: (n, 0)), cur, nxt, cur, nxt, cur],
                          out_specs=pl.BlockSpec((BLOCK, SWA_IN_W), lambda n: (n, 0)),
                          out_shape=jax.ShapeDtypeStruct((s, SWA_IN_W), bf16),
                          compiler_params=_params("parallel"))(dq, dkc, dkp, dvc, dvp, dba)


def _bucket_onehot():
    qi = jnp.arange(BLOCK)[:, None]
    kj = jnp.arange(2 * BLOCK)[None, :]
    dist = jnp.maximum(qi + BLOCK - kj, 0)
    max_exact = REL_BUCKETS // 2
    dd = jnp.maximum(dist, 1).astype(f32)
    large = max_exact + (jnp.log(dd / max_exact) / math.log(REL_MAX_DIST / max_exact) * (REL_BUCKETS - max_exact)).astype(jnp.int32)
    bucket = jnp.where(dist < max_exact, dist, jnp.minimum(large, REL_BUCKETS - 1)).reshape(-1)
    return (bucket[None, :] == jnp.arange(REL_BUCKETS)[:, None]).astype(f32)


def _gbeta_fn(ba, alog_row, dt_row):
    col = lax.broadcasted_iota(jnp.int32, ba.shape, 1)
    v = ba + dt_row
    softplus = jnp.maximum(v, 0.0) + jnp.log(1.0 + jnp.exp(-jnp.abs(v)))
    g = -jnp.exp(alog_row) * softplus
    return jnp.where(col < GDN_HEADS, _sigmoid(ba), jnp.where(col < 2 * GDN_HEADS, g, 0.0))


def gbeta_fwd(swa_in, alog_row, dt_row, *, name):
    s = swa_in.shape[0]
    tm = _tile(s, 512, 8)

    def body(ba_ref, a_ref, d_ref, o_ref):
        o_ref[...] = _gbeta_fn(ba_ref[...], a_ref[...], d_ref[...])

    vec = pl.BlockSpec((1, 128), lambda i: (0, 0))
    return pl.pallas_call(body, name=name, grid=(s // tm,), in_specs=[pl.BlockSpec((tm, 128), lambda i: (i, BA_COL)), vec, vec],
                          out_specs=pl.BlockSpec((tm, 128), lambda i: (i, 0)), out_shape=jax.ShapeDtypeStruct((s, 128), f32),
                          compiler_params=_params("parallel"))(swa_in, alog_row, dt_row)


def gbeta_bwd(swa_in, alog_row, dt_row, dgbeta, *, name):
    s = swa_in.shape[0]
    tm = _tile(s, 512, 8)

    def body(ba_ref, a_ref, d_ref, dgb_ref, dba_ref, da_ref, dd_ref):
        @pl.when(pl.program_id(0) == 0)
        def _():
            da_ref[...] = jnp.zeros_like(da_ref)
            dd_ref[...] = jnp.zeros_like(dd_ref)

        _, pull = jax.vjp(_gbeta_fn, ba_ref[...], a_ref[...], d_ref[...])
        dba, da, dd = pull(dgb_ref[...])
        dba_ref[...] = dba
        da_ref[...] += da
        dd_ref[...] += dd

    vec = pl.BlockSpec((1, 128), lambda i: (0, 0))
    row = pl.BlockSpec((tm, 128), lambda i: (i, 0))
    return pl.pallas_call(body, name=name, grid=(s // tm,),
                          in_specs=[pl.BlockSpec((tm, 128), lambda i: (i, BA_COL)), vec, vec, row], out_specs=[row, vec, vec],
                          out_shape=[jax.ShapeDtypeStruct((s, 128), f32), jax.ShapeDtypeStruct((1, 128), f32),
                                     jax.ShapeDtypeStruct((1, 128), f32)],
                          compiler_params=_params("arbitrary"))(swa_in, alog_row, dt_row, dgbeta)


QKV_W = 3 * GDN_W


def gdn_pre_fwd(gdn_in, convw, *, name):
    s = gdn_in.shape[0]
    tm = _tile(s, 256, 16)
    hb = tm // HALO

    def body(x_ref, xp_ref, w_ref, q_ref, k_ref, v_ref):
        prev = jnp.where(pl.program_id(0) == 0, 0.0, xp_ref[...])
        pre = _causal_conv(jnp.concatenate([prev, x_ref[...]], axis=0), w_ref, GDN_CONV)
        act = pre * _sigmoid(pre)
        for h in range(GDN_HEADS):
            lo = h * GDN_HEAD_DIM
            for off, o_ref in ((0, q_ref), (GDN_W, k_ref)):
                seg = act[:, off + lo:off + lo + GDN_HEAD_DIM]
                o_ref[:, lo:lo + GDN_HEAD_DIM] = seg * lax.rsqrt(jnp.sum(seg * seg, axis=-1, keepdims=True) + 1e-6)
        v_ref[...] = act[:, 2 * GDN_W:]

    out = pl.BlockSpec((tm, GDN_W), lambda i: (i, 0))
    return pl.pallas_call(body, name=name, grid=(s // tm,),
                          in_specs=[pl.BlockSpec((tm, QKV_W), lambda i: (i, 0)),
                                    pl.BlockSpec((HALO, QKV_W), lambda i: (jnp.maximum(i * hb - 1, 0), 0)),
                                    pl.BlockSpec((8, QKV_W), lambda i: (0, 0))],
                          out_specs=[out, out, out], out_shape=[jax.ShapeDtypeStruct((s, GDN_W), f32)] * 3,
                          compiler_params=_params("parallel"))(gdn_in, gdn_in, convw)


def gdn_pre_bwd(gdn_in, convw, dqn, dkn, dv, dgz, *, name):
    s = gdn_in.shape[0]
    tm = _tile(s, 128, 16)
    hb = tm // HALO
    nt = s // tm
    last_hb = s // HALO - 1

    def body(x_ref, xp_ref, xn_ref, w_ref, dq_ref, dqx_ref, dk_ref, dkx_ref, dv_ref, dvx_ref, dz_ref, o_ref, dw_ref):
        i = pl.program_id(0)
        first, last = i == 0, i == nt - 1

        @pl.when(first)
        def _():
            dw_ref[...] = jnp.zeros_like(dw_ref)

        ext = jnp.concatenate([jnp.where(first, 0.0, xp_ref[...]), x_ref[...], xn_ref[...]], axis=0)
        pre = _causal_conv(ext, w_ref, GDN_CONV)
        act, dact_dpre = _silu_and_grad(pre)

        def with_future(t_ref, n_ref):
            return jnp.concatenate([t_ref[...], jnp.where(last, 0.0, n_ref[...])], axis=0)

        dqe, dke, dve = with_future(dq_ref, dqx_ref), with_future(dk_ref, dkx_ref), with_future(dv_ref, dvx_ref)
        parts = []
        for off, dn in ((0, dqe), (GDN_W, dke)):
            for h in range(GDN_HEADS):
                lo = h * GDN_HEAD_DIM
                seg = act[:, off + lo:off + lo + GDN_HEAD_DIM]
                r = lax.rsqrt(jnp.sum(seg * seg, axis=-1, keepdims=True) + 1e-6)
                nrm = seg * r
                dseg = dn[:, lo:lo + GDN_HEAD_DIM]
                parts.append(r * (dseg - nrm * jnp.sum(dseg * nrm, axis=-1, keepdims=True)))
        dpre = jnp.concatenate(parts + [dve], axis=1) * dact_dpre
        o_ref[:, :QKV_W] = _conv_input_grad(dpre, w_ref, GDN_CONV, tm).astype(bf16)
        o_ref[:, QKV_W:] = dz_ref[...].astype(bf16)
        dw_ref[...] += _rows_to_block(_conv_weight_grad(dpre[:tm], ext, GDN_CONV, tm), 8, QKV_W)

    row = pl.BlockSpec((tm, GDN_W), lambda i: (i, 0))
    fut = pl.BlockSpec((HALO, GDN_W), lambda i: (jnp.minimum((i + 1) * hb, last_hb), 0))
    return pl.pallas_call(
        body, name=name, grid=(nt,),
        in_specs=[pl.BlockSpec((tm, QKV_W), lambda i: (i, 0)),
                  pl.BlockSpec((HALO, QKV_W), lambda i: (jnp.maximum(i * hb - 1, 0), 0)),
                  pl.BlockSpec((HALO, QKV_W), lambda i: (jnp.minimum((i + 1) * hb, last_hb), 0)),
                  pl.BlockSpec((8, QKV_W), lambda i: (0, 0)), row, fut, row, fut, row, fut, row],
        out_specs=[pl.BlockSpec((tm, 4 * GDN_W), lambda i: (i, 0)), pl.BlockSpec((8, QKV_W), lambda i: (0, 0))],
        out_shape=[jax.ShapeDtypeStruct((s, 4 * GDN_W), bf16), jax.ShapeDtypeStruct((8, QKV_W), f32)],
        compiler_params=_params("arbitrary"),
    )(gdn_in, gdn_in, gdn_in, convw, dqn, dqn, dkn, dkn, dv, dv, dgz)


def _gdn_post_head(o, z, nw):
    return o * lax.rsqrt(jnp.mean(o * o, axis=-1, keepdims=True) + 1e-6) * nw * (z * _sigmoid(z))


def gdn_post_fwd(o, gdn_in, nw, *, name):
    s = o.shape[0]
    tm = _tile(s, 256, 16)

    def body(o_ref, z_ref, nw_ref, y_ref):
        for h in range(GDN_HEADS):
            sl = slice(h * GDN_HEAD_DIM, (h + 1) * GDN_HEAD_DIM)
            y_ref[:, sl] = _gdn_post_head(o_ref[:, sl], z_ref[:, sl], nw_ref[...]).astype(bf16)

    row = pl.BlockSpec((tm, GDN_W), lambda i: (i, 0))
    return pl.pallas_call(body, name=name, grid=(s // tm,),
                          in_specs=[row, pl.BlockSpec((tm, GDN_W), lambda i: (i, 3)), pl.BlockSpec((1, 128), lambda i: (0, 0))],
                          out_specs=row, out_shape=jax.ShapeDtypeStruct((s, GDN_W), bf16),
                          compiler_params=_params("parallel"))(o, gdn_in, nw)


def gdn_post_bwd(o, gdn_in, nw, dy, *, name):
    s = o.shape[0]
    tm = _tile(s, 256, 16)

    def body(o_ref, z_ref, nw_ref, dy_ref, do_ref, dz_ref, dnw_ref):
        @pl.when(pl.program_id(0) == 0)
        def _():
            dnw_ref[...] = jnp.zeros_like(dnw_ref)

        dnw = jnp.zeros((1, 128), f32)
        for h in range(GDN_HEADS):
            sl = slice(h * GDN_HEAD_DIM, (h + 1) * GDN_HEAD_DIM)
            _, pull = jax.vjp(_gdn_post_head, o_ref[:, sl], z_ref[:, sl], nw_ref[...])
            do, dz, dn = pull(dy_ref[:, sl])
            do_ref[:, sl] = do
            dz_ref[:, sl] = dz
            dnw = dnw + dn
        dnw_ref[...] += dnw

    row = pl.BlockSpec((tm, GDN_W), lambda i: (i, 0))
    vec = pl.BlockSpec((1, 128), lambda i: (0, 0))
    return pl.pallas_call(body, name=name, grid=(s // tm,),
                          in_specs=[row, pl.BlockSpec((tm, GDN_W), lambda i: (i, 3)), vec, row], out_specs=[row, row, vec],
                          out_shape=[jax.ShapeDtypeStruct((s, GDN_W), f32), jax.ShapeDtypeStruct((s, GDN_W), f32),
                                     jax.ShapeDtypeStruct((1, 128), f32)],
                          compiler_params=_params("arbitrary"))(o, gdn_in, nw, dy)


def _gdn_chunk(q, k, v, g, beta, state):
    c = GDN_CHUNK
    r = lax.broadcasted_iota(jnp.int32, (c, c), 0)
    cc = lax.broadcasted_iota(jnp.int32, (c, c), 1)
    tril, strict = r >= cc, r > cc
    eye = (r == cc).astype(f32)

    def dhi(a, b):
        return jnp.dot(a, b, precision=HI, preferred_element_type=f32)

    gi = dhi(tril.astype(f32), jnp.broadcast_to(g, (c, c)))
    gj = dhi(jnp.ones((c, c), f32), eye * gi)
    decay = jnp.where(tril, jnp.exp(jnp.where(tril, gi - gj, 0.0)), 0.0)
    kb, vb = k * beta, v * beta
    a = jnp.where(strict, _d16(kb, k, NT) * decay, 0.0)
    tinv = eye - a
    pw = dhi(a, a)
    for it in range(5):
        tinv = tinv + dhi(tinv, pw)
        if it < 4:
            pw = dhi(pw, pw)
    gc = gi[:, 0:1]
    egc = jnp.exp(gc)
    u = dhi(tinv, vb)
    w = dhi(tinv, kb * egc)
    qs = q * (GDN_HEAD_DIM ** -0.5)
    attn = jnp.where(tril, _d16(qs, k, NT) * decay, 0.0)
    g_last = gi[c - 1:c, 0:1]
    v_new = u - _d16(w, state, NN)
    out = _d16(qs * egc, state, NN) + _d16(attn, v_new, NN)
    new_state = state * jnp.exp(g_last) + _d16(k * jnp.exp(g_last - gc), v_new, TN)
    return out, new_state


def gdn_scan_fwd(qn, kn, v, gbeta, *, name):
    s = qn.shape[0]
    nc = s // GDN_CHUNK

    def body(q_ref, k_ref, v_ref, gb_ref, o_ref, st_ref, state_ref):
        @pl.when(pl.program_id(0) == 0)
        def _():
            state_ref[...] = jnp.zeros_like(state_ref)

        for h in range(GDN_HEADS):
            sl = slice(h * GDN_HEAD_DIM, (h + 1) * GDN_HEAD_DIM)
            st = state_ref[h]
            st_ref[0, h] = st
            out, new = _gdn_chunk(q_ref[:, sl], k_ref[:, sl], v_ref[:, sl], gb_ref[:, GDN_HEADS + h:GDN_HEADS + h + 1],
                                  gb_ref[:, h:h + 1], st)
            o_ref[:, sl] = out
            state_ref[h] = new

    row = pl.BlockSpec((GDN_CHUNK, GDN_W), lambda n: (n, 0))
    return pl.pallas_call(
        body, name=name, grid=(nc,), in_specs=[row, row, row, pl.BlockSpec((GDN_CHUNK, 128), lambda n: (n, 0))],
        out_specs=[row, pl.BlockSpec((1, GDN_HEADS, GDN_HEAD_DIM, GDN_HEAD_DIM), lambda n: (n, 0, 0, 0))],
        out_shape=[jax.ShapeDtypeStruct((s, GDN_W), f32),
                   jax.ShapeDtypeStruct((nc, GDN_HEADS, GDN_HEAD_DIM, GDN_HEAD_DIM), f32)],
        scratch_shapes=[pltpu.VMEM((GDN_HEADS, GDN_HEAD_DIM, GDN_HEAD_DIM), f32)],
        compiler_params=_params("arbitrary"),
    )(qn, kn, v, gbeta)


def gdn_scan_bwd(qn, kn, v, gbeta, states, dout, *, name):
    s = qn.shape[0]
    nc = s // GDN_CHUNK

    def body(q_ref, k_ref, v_ref, gb_ref, st_ref, do_ref, dq_ref, dk_ref, dv_ref, dgb_ref, dstate_ref):
        @pl.when(pl.program_id(0) == 0)
        def _():
            dstate_ref[...] = jnp.zeros_like(dstate_ref)

        col = lax.broadcasted_iota(jnp.int32, (GDN_CHUNK, 128), 1)
        dgb = jnp.zeros((GDN_CHUNK, 128), f32)
        for h in range(GDN_HEADS):
            sl = slice(h * GDN_HEAD_DIM, (h + 1) * GDN_HEAD_DIM)
            _, pull = jax.vjp(_gdn_chunk, q_ref[:, sl], k_ref[:, sl], v_ref[:, sl],
                              gb_ref[:, GDN_HEADS + h:GDN_HEADS + h + 1], gb_ref[:, h:h + 1], st_ref[0, h])
            dq, dk, dv, dg, dbeta, dst = pull((do_ref[:, sl], dstate_ref[h]))
            dq_ref[:, sl] = dq
            dk_ref[:, sl] = dk
            dv_ref[:, sl] = dv
            dstate_ref[h] = dst
            dgb = dgb + jnp.where(col == h, dbeta, 0.0) + jnp.where(col == GDN_HEADS + h, dg, 0.0)
        dgb_ref[...] = dgb

    row = pl.BlockSpec((GDN_CHUNK, GDN_W), lambda n: (nc - 1 - n, 0))
    gb = pl.BlockSpec((GDN_CHUNK, 128), lambda n: (nc - 1 - n, 0))
    return pl.pallas_call(
        body, name=name, grid=(nc,),
        in_specs=[row, row, row, gb, pl.BlockSpec((1, GDN_HEADS, GDN_HEAD_DIM, GDN_HEAD_DIM), lambda n: (nc - 1 - n, 0, 0, 0)), row],
        out_specs=[row, row, row, gb],
        out_shape=[jax.ShapeDtypeStruct((s, GDN_W), f32)] * 3 + [jax.ShapeDtypeStruct((s, 128), f32)],
        scratch_shapes=[pltpu.VMEM((GDN_HEADS, GDN_HEAD_DIM, GDN_HEAD_DIM), f32)],
        compiler_params=_params("arbitrary"),
    )(qn, kn, v, gbeta, states, dout)


def adamw(w, g, m, v, *, name):
    r, c = w.shape
    tr = _tile(r, 256, 8)

    def body(w_ref, g_ref, m_ref, v_ref, d_ref, nm_ref, nv_ref):
        gv = g_ref[...]
        nm = ADAM_B1 * m_ref[...] + (1.0 - ADAM_B1) * gv
        nv = ADAM_B2 * v_ref[...] + (1.0 - ADAM_B2) * (gv * gv)
        m_hat = nm / (1.0 - ADAM_B1 ** ADAM_STEP)
        v_hat = nv / (1.0 - ADAM_B2 ** ADAM_STEP)
        d_ref[...] = -ADAM_LR * (m_hat / (jnp.sqrt(v_hat) + ADAM_EPS) + ADAM_WD * w_ref[...])
        nm_ref[...] = nm
        nv_ref[...] = nv

    spec = pl.BlockSpec((tr, c), lambda i: (i, 0))
    return pl.pallas_call(body, name=name, grid=(r // tr,), in_specs=[spec] * 4, out_specs=[spec] * 3,
                          out_shape=[jax.ShapeDtypeStruct((r, c), f32)] * 3, compiler_params=_params("parallel"))(w, g, m, v)


def _pos():
    return lax.axis_index("x"), lax.axis_index("y"), lax.axis_index("c")


ANY = pl.BlockSpec(memory_space=pl.ANY)


def all_gather_packed(pack, *, name):
    r, c = pack.shape

    def body(x_ref, out_ref, send_sems, recv_sems, local_sem):
        x, y, cc = _pos()
        me, sibling = (x, y, cc), (x, y, 1 - cc)
        chips = [(1 - x, y), (x, 1 - y), (1 - x, 1 - y)]

        def slot(px, py, pc):
            return out_ref.at[4 * px + 2 * py + pc]

        def copy(k, block, to, src=None):
            return pltpu.make_async_remote_copy(src_ref=slot(*block) if src is None else src, dst_ref=slot(*block),
                                                send_sem=send_sems.at[k], recv_sem=recv_sems.at[k], device_id=to,
                                                device_id_type=MESH)

        mine = pltpu.make_async_copy(x_ref, slot(*me), local_sem)
        mine.start()
        first = [copy(0, me, sibling, src=x_ref)] + [copy(1 + j, me, (*chip, cc), src=x_ref) for j, chip in enumerate(chips)]
        for cp in first:
            cp.start()
        passed = [copy(4 + j, (*chip, cc), sibling) for j, chip in enumerate(chips)]
        for j, chip in enumerate(chips):
            copy(1 + j, (*chip, cc), me).wait_recv()
            passed[j].start()
        copy(0, sibling, me).wait_recv()
        for j, chip in enumerate(chips):
            copy(4 + j, (*chip, 1 - cc), me).wait_recv()
        for cp in first + passed:
            cp.wait_send()
        mine.wait()

    return pl.pallas_call(body, name=name, in_specs=[ANY], out_specs=ANY,
                          out_shape=jax.ShapeDtypeStruct((N_DEV, r, c), pack.dtype),
                          scratch_shapes=[pltpu.SemaphoreType.DMA((7,)), pltpu.SemaphoreType.DMA((7,)),
                                          pltpu.SemaphoreType.DMA])(pack)


def exchange_d2d(g2, *, name):
    _, nchip, r, c = g2.shape

    def body(g_ref, out_ref, send_sem, recv_sem):
        x, y, cc = _pos()
        cp = pltpu.make_async_remote_copy(src_ref=g_ref.at[1 - cc], dst_ref=out_ref, send_sem=send_sem, recv_sem=recv_sem,
                                          device_id=(x, y, 1 - cc), device_id_type=MESH)
        cp.start()
        cp.wait()

    return pl.pallas_call(body, name=name, in_specs=[ANY], out_specs=ANY,
                          out_shape=jax.ShapeDtypeStruct((nchip, r, c), g2.dtype),
                          scratch_shapes=[pltpu.SemaphoreType.DMA, pltpu.SemaphoreType.DMA])(g2)


def add_halves(g2, recv, *, name):
    _, nchip, r, c = g2.shape
    tr = _tile(r, 512, 8)
    core = lax.axis_index("c").astype(jnp.int32).reshape(1)

    def body(core_ref, a_ref, b_ref, o_ref):
        o_ref[...] = a_ref[0] + b_ref[...]

    return pl.pallas_call(
        body, name=name,
        grid_spec=pltpu.PrefetchScalarGridSpec(
            num_scalar_prefetch=1, grid=(nchip, r // tr),
            in_specs=[pl.BlockSpec((1, 1, tr, c), lambda q, i, core_ref: (core_ref[0], q, i, 0)),
                      pl.BlockSpec((1, tr, c), lambda q, i, core_ref: (q, i, 0))],
            out_specs=pl.BlockSpec((1, tr, c), lambda q, i, core_ref: (q, i, 0))),
        out_shape=jax.ShapeDtypeStruct((nchip, r, c), f32), compiler_params=_params("parallel", "parallel"),
    )(core, g2, recv)


def exchange_ici(p4, *, name):
    _, r, c = p4.shape

    def body(p_ref, out_ref, send_sems, recv_sems, local_sem):
        x, y, cc = _pos()
        chips = [(1 - x, y), (x, 1 - y), (1 - x, 1 - y)]
        mine = pltpu.make_async_copy(p_ref.at[2 * x + y], out_ref.at[3], local_sem)
        mine.start()
        cps = [pltpu.make_async_remote_copy(src_ref=p_ref.at[2 * px + py], dst_ref=out_ref.at[k], send_sem=send_sems.at[k],
                                            recv_sem=recv_sems.at[k], device_id=(px, py, cc), device_id_type=MESH)
               for k, (px, py) in enumerate(chips)]
        for cp in cps:
            cp.start()
        for cp in cps:
            cp.wait()
        mine.wait()

    return pl.pallas_call(body, name=name, in_specs=[ANY], out_specs=ANY, out_shape=jax.ShapeDtypeStruct((4, r, c), p4.dtype),
                          scratch_shapes=[pltpu.SemaphoreType.DMA((3,)), pltpu.SemaphoreType.DMA((3,)),
                                          pltpu.SemaphoreType.DMA])(p4)


def add_four(r4, *, name):
    _, r, c = r4.shape
    tr = _tile(r, 512, 8)

    def body(a_ref, o_ref):
        o_ref[...] = ((a_ref[3] + a_ref[0]) + a_ref[1]) + a_ref[2]

    return pl.pallas_call(body, name=name, grid=(r // tr,), in_specs=[pl.BlockSpec((4, tr, c), lambda i: (0, i, 0))],
                          out_specs=pl.BlockSpec((tr, c), lambda i: (i, 0)), out_shape=jax.ShapeDtypeStruct((r, c), f32),
                          compiler_params=_params("parallel"))(r4)


def all_reduce_small(vec, *, name):
    r, c = vec.shape

    def body(v_ref, out_ref, buf_ref, send_sems, recv_sems):
        x, y, cc = _pos()
        my_id = 4 * x + 2 * y + cc
        buf_ref[my_id] = v_ref[...]
        flips = [(fx, fy, fc) for fx in (0, 1) for fy in (0, 1) for fc in (0, 1)][1:]
        cps = []
        for k, (fx, fy, fc) in enumerate(flips):
            peer = ((1 - x) if fx else x, (1 - y) if fy else y, (1 - cc) if fc else cc)
            cps.append(pltpu.make_async_remote_copy(src_ref=v_ref, dst_ref=buf_ref.at[my_id], send_sem=send_sems.at[k],
                                                    recv_sem=recv_sems.at[k], device_id=peer, device_id_type=MESH))
        for cp in cps:
            cp.start()
        for cp in cps:
            cp.wait()
        acc = buf_ref[0]
        for d in range(1, N_DEV):
            acc = acc + buf_ref[d]
        out_ref[...] = acc

    vm = pl.BlockSpec(memory_space=pltpu.VMEM)
    return pl.pallas_call(body, name=name, in_specs=[vm], out_specs=vm, out_shape=jax.ShapeDtypeStruct((r, c), f32),
                          scratch_shapes=[pltpu.VMEM((N_DEV, r, c), f32), pltpu.SemaphoreType.DMA((7,)),
                                          pltpu.SemaphoreType.DMA((7,))])(vec)


def _pack(parts, rows, dtype):
    flat = jnp.concatenate([p.reshape(-1).astype(dtype) for p in parts])
    return jnp.pad(flat, (0, rows * PACK_COLS - flat.shape[0])).reshape(rows, PACK_COLS)


def _unpack(flat, shapes):
    out, off = [], 0
    for shp in shapes:
        n = shp[0] * shp[1]
        out.append(flat[..., off:off + n].reshape(flat.shape[:-1] + tuple(shp)))
        off += n
    return out


def _gathered_full(g, axis):
    _, r, c = g.shape
    return g.reshape(N_DEV * r, c) if axis == 0 else jnp.transpose(g, (1, 0, 2)).reshape(r, N_DEV * c)


def _shards_of(full, shard_shape, axis):
    r, c = shard_shape
    g = full.reshape(N_DEV, r, c) if axis == 0 else jnp.transpose(full.reshape(r, N_DEV, c), (1, 0, 2))
    return g.reshape(N_DEV, r * c)


def kernel(x, mem, w_in, rel_bias, swa_sinks, gdn_conv_w, gdn_a_log, gdn_dt_bias, gdn_norm_w, w_br_swa, w_br_gdn, w_mix_o, ln1_g, ln1_b, w_mem_q, w_mem_kv, w_mem_o, ln2_g, ln2_b, w_up, ffn_conv_w, ffn_conv_b, w_down, ln3_g, ln3_b, loss_target, m_w_in, m_rel_bias, m_swa_sinks, m_gdn_conv_w, m_gdn_a_log, m_gdn_dt_bias, m_gdn_norm_w, m_w_br_swa, m_w_br_gdn, m_w_mix_o, m_ln1_g, m_ln1_b, m_w_mem_q, m_w_mem_kv, m_w_mem_o, m_ln2_g, m_ln2_b, m_w_up, m_ffn_conv_w, m_ffn_conv_b, m_w_down, m_ln3_g, m_ln3_b, v_w_in, v_rel_bias, v_swa_sinks, v_gdn_conv_w, v_gdn_a_log, v_gdn_dt_bias, v_gdn_norm_w, v_w_br_swa, v_w_br_gdn, v_w_mix_o, v_ln1_g, v_ln1_b, v_w_mem_q, v_w_mem_kv, v_w_mem_o, v_ln2_g, v_ln2_b, v_w_up, v_ffn_conv_w, v_ffn_conv_b, v_w_down, v_ln3_g, v_ln3_b):
    env = dict(locals())
    w2 = {n: (env[n][0] if env[n].ndim == 3 else env[n]) for n in WEIGHTS}
    m2 = {n: (env["m_" + n][0] if env["m_" + n].ndim == 3 else env["m_" + n]) for n in WEIGHTS}
    v2 = {n: (env["v_" + n][0] if env["v_" + n].ndim == 3 else env["v_" + n]) for n in WEIGHTS}
    xs, mems, target = x[0], mem[0], loss_target[0]
    shard_shapes = [shp for _, shp, _ in SHARDED]

    gathered = all_gather_packed(_pack([w2[n] for n, _, _ in SHARDED], PACK_ROWS, bf16), name="gather_weights")
    wsh = dict(zip([n for n, _, _ in SHARDED], _unpack(gathered.reshape(N_DEV, -1), shard_shapes)))
    full = {n: _gathered_full(wsh[n], ax) for n, _, ax in SHARDED if n not in ("w_up", "w_down", "ffn_conv_w")}
    win = full["w_in"]
    w_gates, w_gdn = win[:, 5392:], win[:, 1280:5376]
    w_swa = jnp.concatenate([win[:, :1280], win[:, 5376:5392], jnp.zeros((D_MODEL, SWA_IN_W - 1296), bf16)], axis=1)
    pad_ff = FF_PAD - FF_SHARD
    w_up_p = jnp.transpose(jnp.pad(wsh["w_up"], ((0, 0), (0, 0), (0, pad_ff))), (1, 0, 2)).reshape(D_MODEL, N_DEV * FF_PAD)
    w_down_p = jnp.pad(wsh["w_down"].reshape(4, FF_SHARD, D_MODEL), ((0, 0), (0, pad_ff), (0, 0))).reshape(4 * FF_PAD, D_MODEL)
    cwb = jnp.concatenate([wsh["ffn_conv_w"].astype(f32), w2["ffn_conv_b"].reshape(N_DEV, 1, FF_SHARD),
                           jnp.zeros((N_DEV, 4, FF_SHARD), f32)], axis=1)
    cwb = jnp.pad(cwb, ((0, 0), (0, 0), (0, pad_ff)))
    convw = jnp.pad(full["gdn_conv_w"].astype(f32), ((0, 4), (0, 0)))
    onehot = _bucket_onehot()
    bias = mm(w2["rel_bias"].T, onehot, "nn", hi=True, tn=4096, name="rel_bias_table").reshape(SWA_HEADS, BLOCK, 2 * BLOCK)
    alog_row = jnp.pad(w2["gdn_a_log"], ((0, 0), (GDN_HEADS, 128 - 2 * GDN_HEADS)))
    dt_row = jnp.pad(w2["gdn_dt_bias"], ((0, 0), (GDN_HEADS, 128 - 2 * GDN_HEADS)))

    xb = cast_bf16(xs, name="cast_x")
    memb = cast_bf16(mems, name="cast_mem")
    gates = mm(xb, w_gates, "nn", name="proj_gates")
    gdn_in = mm(xb, w_gdn, "nn", name="proj_gdn")
    swa_in = mm(xb, w_swa, "nn", tn=SWA_IN_W, name="proj_swa")
    attn = swa_fwd(swa_in, bias, w2["swa_sinks"], name="swa_fwd")
    qn, kn, vv = gdn_pre_fwd(gdn_in, convw, name="gdn_pre_fwd")
    gbeta = gbeta_fwd(swa_in, alog_row, dt_row, name="gbeta_fwd")
    o_gdn, states = gdn_scan_fwd(qn, kn, vv, gbeta, name="gdn_scan_fwd")
    ygd = gdn_post_fwd(o_gdn, gdn_in, w2["gdn_norm_w"], name="gdn_post_fwd")
    y_swa = mm(attn, full["w_br_swa"], "nn", name="br_swa")
    y_gdn = mm(ygd, full["w_br_gdn"], "nn", name="br_gdn")
    mixed = merge_fwd(gates, y_swa, y_gdn, name="merge_fwd")
    z1 = mm(mixed, full["w_mix_o"], "nn", add=xs, add_scale=ALPHA, name="mix_o")
    x1, x1b = ln_fwd(z1, w2["ln1_g"], w2["ln1_b"], name="ln1_fwd")
    qm = mm(x1b, full["w_mem_q"], "nn", name="mem_q")
    kv = mm(memb, full["w_mem_kv"], "nn", name="mem_kv")
    om = memattn_fwd(qm, kv, name="memattn_fwd")
    z2 = mm(om, full["w_mem_o"], "nn", add=x1, add_scale=ALPHA, name="mem_o")
    x2, x2b = ln_fwd(z2, w2["ln2_g"], w2["ln2_b"], name="ln2_fwd")
    hpre = mm(x2b, w_up_p, "nn", tn=FF_PAD, name="ffn_up")
    act = ffn_act_fwd(hpre, cwb, name="ffn_act_fwd")
    z3 = mm(act, w_down_p, "nn", add=x2, add_scale=ALPHA, tk=FF_PAD, name="ffn_down")
    dz3, dz3b, d_ln3g, d_ln3b, loss = ln_loss(z3, target, w2["ln3_g"], w2["ln3_b"], name="ln3_loss")

    dact = mm(dz3b, w_down_p, "nt", tn=FF_PAD, name="d_act")
    d_wdown_p = mm(act, dz3b, "tn", tm=FF_PAD, name="dw_down")
    d_hpre, d_cwb = ffn_act_bwd(hpre, dact, cwb, name="ffn_act_bwd")
    dx2 = mm(d_hpre, w_up_p, "nt", add=dz3, add_scale=ALPHA, tk=FF_PAD, name="d_x2")
    d_wup_p = mm(x2b, d_hpre, "tn", tn=FF_PAD, name="dw_up")
    dz2, dz2b, d_ln2g, d_ln2b = ln_bwd(dx2, z2, w2["ln2_g"], name="ln2_bwd")
    d_om = mm(dz2b, full["w_mem_o"], "nt", name="d_om")
    d_wmemo = mm(om, dz2b, "tn", name="dw_mem_o")
    dqm, dkv = memattn_bwd(qm, kv, d_om, name="memattn_bwd")
    dx1 = mm(dqm, full["w_mem_q"], "nt", add=dz2, add_scale=ALPHA, name="d_x1")
    d_wmemq = mm(x1b, dqm, "tn", name="dw_mem_q")
    d_wmemkv = mm(memb, dkv, "tn", name="dw_mem_kv")
    dz1, dz1b, d_ln1g, d_ln1b = ln_bwd(dx1, z1, w2["ln1_g"], name="ln1_bwd")
    dmix = mm(dz1b, full["w_mix_o"], "nt", name="d_mixed")
    d_wmixo = mm(mixed, dz1b, "tn", name="dw_mix_o")
    dys, dyg, d_gates = merge_bwd(gates, y_swa, y_gdn, dmix, name="merge_bwd")
    d_attn = mm(dys, full["w_br_swa"], "nt", name="d_attn")
    d_wbrswa = mm(attn, dys, "tn", name="dw_br_swa")
    d_ygd = mm(dyg, full["w_br_gdn"], "nt", name="d_ygd")
    d_wbrgdn = mm(ygd, dyg, "tn", name="dw_br_gdn")
    d_o, d_gz, d_normw = gdn_post_bwd(o_gdn, gdn_in, w2["gdn_norm_w"], d_ygd, name="gdn_post_bwd")
    dqn, dkn, dvv, dgbeta = gdn_scan_bwd(qn, kn, vv, gbeta, states, d_o, name="gdn_scan_bwd")
    d_gdn_in, d_convw = gdn_pre_bwd(gdn_in, convw, dqn, dkn, dvv, d_gz, name="gdn_pre_bwd")
    d_ba, d_alog, d_dt = gbeta_bwd(swa_in, alog_row, dt_row, dgbeta, name="gbeta_bwd")
    dq, dkc, dkp, dvc, dvp, d_bias, d_sinks = swa_bwd(swa_in, bias, w2["swa_sinks"], d_attn, name="swa_bwd")
    d_swa_in = swa_in_grad(dq, dkc, dkp, dvc, dvp, d_ba, name="swa_in_grad")
    d_relbias = mm(d_bias.reshape(SWA_HEADS, -1), onehot, "nt", hi=True, tk=4096, name="d_rel_bias").T
    gx = mm(d_gates, w_gates, "nt", add=dz1, add_scale=ALPHA, name="dx_gates")
    gx = mm(d_gdn_in, w_gdn, "nt", add=gx, name="dx_gdn")
    gx = mm(d_swa_in, w_swa, "nt", add=gx, tk=SWA_IN_W, name="dx_swa")
    d_wgates = mm(xb, d_gates, "tn", name="dw_gates")
    d_wgdn = mm(xb, d_gdn_in, "tn", name="dw_gdn")
    d_wswa = mm(xb, d_swa_in, "tn", tn=SWA_IN_W, name="dw_swa")

    gfull = {
        "w_in": jnp.concatenate([d_wswa[:, :1280], d_wgdn, d_wswa[:, 1280:1296], d_wgates], axis=1),
        "w_br_swa": d_wbrswa, "w_br_gdn": d_wbrgdn, "w_mix_o": d_wmixo, "w_mem_q": d_wmemq, "w_mem_kv": d_wmemkv,
        "w_mem_o": d_wmemo,
        "w_up": d_wup_p.reshape(D_MODEL, N_DEV, FF_PAD)[:, :, :FF_SHARD].reshape(D_MODEL, 2 * D_FF),
        "w_down": d_wdown_p.reshape(4, FF_PAD, D_MODEL)[:, :FF_SHARD].reshape(D_FF, D_MODEL),
        "ffn_conv_w": jnp.transpose(d_cwb[:, :3, :FF_SHARD], (1, 0, 2)).reshape(3, 2 * D_FF),
        "gdn_conv_w": d_convw[:GDN_CONV],
    }
    gsmall = {
        "rel_bias": d_relbias, "swa_sinks": d_sinks[:, :SWA_HEADS], "gdn_a_log": d_alog[:, GDN_HEADS:2 * GDN_HEADS],
        "gdn_dt_bias": d_dt[:, GDN_HEADS:2 * GDN_HEADS], "gdn_norm_w": d_normw, "ln1_g": d_ln1g, "ln1_b": d_ln1b,
        "ln2_g": d_ln2g, "ln2_b": d_ln2b, "ln3_g": d_ln3g, "ln3_b": d_ln3b,
        "ffn_conv_b": d_cwb[:, 3, :FF_SHARD].reshape(1, 2 * D_FF),
    }

    per_dev = jnp.concatenate([_shards_of(gfull[n], shp, ax) for n, shp, ax in SHARDED], axis=1)
    per_dev = jnp.pad(per_dev, ((0, 0), (0, PACK_ROWS * PACK_COLS - per_dev.shape[1])))
    g2 = jnp.transpose(per_dev.reshape(4, 2, PACK_ROWS, PACK_COLS), (1, 0, 2, 3))
    from_sibling = exchange_d2d(g2, name="grad_exchange_d2d")
    chip_sum = add_halves(g2, from_sibling, name="grad_add_sibling")
    from_chips = exchange_ici(chip_sum, name="grad_exchange_ici")
    gshard = dict(zip([n for n, _, _ in SHARDED],
                      _unpack(add_four(from_chips, name="grad_add_chips").reshape(-1), shard_shapes)))
    small_sum = all_reduce_small(_pack([gsmall[n] for n, _ in SMALL], SMALL_ROWS, f32), name="all_reduce_small")
    gsm = dict(zip([n for n, _ in SMALL], _unpack(small_sum.reshape(-1), [shp for _, shp in SMALL])))
    grads = {**gshard, **gsm}

    big = [n for n, shp, _ in SHARDED if shp[0] * shp[1] > 8192]
    tiny = [n for n in WEIGHTS if n not in big]
    delta, new_m, new_v = {}, {}, {}
    for n in big:
        delta[n], new_m[n], new_v[n] = adamw(w2[n], grads[n], m2[n], v2[n], name="adamw_" + n)
    tiny_shapes = [w2[n].shape for n in tiny]
    packed = [_pack([src[n] for n in tiny], SMALL_ROWS, f32) for src in (w2, grads, m2, v2)]
    for dst, res in zip((delta, new_m, new_v), adamw(*packed, name="adamw_small")):
        dst.update(zip(tiny, _unpack(res.reshape(-1), tiny_shapes)))

    def shaped(d):
        return [d[n].reshape(env[n].shape) for n in WEIGHTS]

    loss_all = lax.psum(loss[0, 0], ("x", "y", "c"))
    return (loss_all, gx[None], *shaped(grads), *shaped(delta), *shaped(new_m), *shaped(new_v))
```

```python
import functools
import math

import jax
import jax.numpy as jnp
from jax import lax
from jax.experimental import pallas as pl
from jax.experimental.pallas import tpu as pltpu

f32 = jnp.float32
bf16 = jnp.bfloat16
HI = lax.Precision.HIGHEST
MESH = pl.DeviceIdType.MESH

D_MODEL = 2048
N_DEV = 8
SWA_HEADS, SWA_KV_HEADS, SWA_HEAD_DIM, BLOCK = 16, 2, 64, 128
REL_BUCKETS, REL_MAX_DIST = 32, 128
GDN_HEADS, GDN_HEAD_DIM, GDN_CONV, GDN_CHUNK = 8, 128, 4, 64
MEM_HEADS, MEM_HEAD_DIM = 4, 128
D_FF = 5504
FF_SHARD = 2 * D_FF // N_DEV
FF_PAD = 1408
NORM_EPS = 1e-5
ALPHA = 2.0 ** 0.25
NEG_INF = -1e30
SWA_Q, SWA_KV, GDN_W, MEM_W = 1024, 128, 1024, 512
IN_DIM = 9488
HALO = 8

ADAM_LR, ADAM_B1, ADAM_B2, ADAM_EPS, ADAM_WD, ADAM_STEP = 0.001, 0.9, 0.999, 1e-08, 0.01, 10

PACK_COLS = 1024
SMALL_ROWS = 32
AR_ROWS = 72
CONV_ROWS = 48

SHARDED = (
    ("w_in", (2048, 1186), 1), ("w_br_swa", (1024, 256), 1), ("w_br_gdn", (1024, 256), 1),
    ("w_mix_o", (256, 2048), 0), ("w_mem_q", (256, 512), 0), ("w_mem_kv", (256, 1024), 0),
    ("w_mem_o", (512, 256), 1), ("w_up", (2048, 1376), 1), ("w_down", (688, 2048), 0),
    ("ffn_conv_w", (3, 1376), 1), ("gdn_conv_w", (4, 384), 1),
)
SMALL = (
    ("rel_bias", (32, 16)), ("swa_sinks", (1, 16)), ("gdn_a_log", (1, 8)), ("gdn_dt_bias", (1, 8)),
    ("gdn_norm_w", (1, 128)), ("ln1_g", (1, 2048)), ("ln1_b", (1, 2048)), ("ln2_g", (1, 2048)),
    ("ln2_b", (1, 2048)), ("ln3_g", (1, 2048)), ("ln3_b", (1, 2048)), ("ffn_conv_b", (1, 11008)),
)
WEIGHTS = ("w_in", "rel_bias", "swa_sinks", "gdn_conv_w", "gdn_a_log", "gdn_dt_bias", "gdn_norm_w", "w_br_swa",
           "w_br_gdn", "w_mix_o", "ln1_g", "ln1_b", "w_mem_q", "w_mem_kv", "w_mem_o", "ln2_g", "ln2_b", "w_up",
           "ffn_conv_w", "ffn_conv_b", "w_down", "ln3_g", "ln3_b")


def _tile(n, target, align):
    if n <= target:
        return n
    t = (target // align) * align
    while t >= align:
        if n % t == 0:
            return t
        t -= align
    return n


VMEM_LIMIT_BYTES = 56 * 1024 * 1024


def _params(*sem):
    return pltpu.CompilerParams(dimension_semantics=sem, vmem_limit_bytes=VMEM_LIMIT_BYTES)


def _sigmoid(v):
    return jax.nn.sigmoid(v)


def _d16(a, b, dims):
    return lax.dot_general(a.astype(bf16), b.astype(bf16), (dims, ((), ())), preferred_element_type=f32)


NN = ((1,), (0,))
NT = ((1,), (1,))
TN = ((0,), (0,))


def mm(a, b, mode, *, name, add=None, add_scale=1.0, out_dtype=f32, hi=False, tm=1024, tn=1024, tk=512,
       b_blocked=False, out_blocked=False):
    if b_blocked:
        nb, rows, width = b.shape
        if mode == "nn":
            (m, k), n, tn = a.shape, nb * width, width
        else:
            (m, k), n, tk = a.shape, rows, width
    elif mode == "nn":
        (m, k), (_, n) = a.shape, b.shape
    elif mode == "nt":
        (m, k), (n, _) = a.shape, b.shape
    else:
        (k, m), (_, n) = a.shape, b.shape
    if out_blocked:
        tn = n // N_DEV
    tm, tn, tk = _tile(m, tm, 8 if mode != "tn" else 128), _tile(n, tn, 128), _tile(k, tk, 128 if mode != "tn" else 8)
    nk = k // tk
    dims = {"nn": NN, "nt": NT, "tn": TN}[mode]
    a_spec = pl.BlockSpec((tk, tm), lambda i, j, kk: (kk, i)) if mode == "tn" else pl.BlockSpec((tm, tk), lambda i, j, kk: (i, kk))
    if b_blocked:
        b_spec = (pl.BlockSpec((None, tk, tn), lambda i, j, kk: (j, kk, 0)) if mode == "nn"
                  else pl.BlockSpec((None, tn, tk), lambda i, j, kk: (kk, j, 0)))
    else:
        b_spec = pl.BlockSpec((tn, tk), lambda i, j, kk: (j, kk)) if mode == "nt" else pl.BlockSpec((tk, tn), lambda i, j, kk: (kk, j))
    if out_blocked:
        o_spec, o_shape = pl.BlockSpec((None, tm, tn), lambda i, j, kk: (j, i, 0)), (N_DEV, m, tn)
    else:
        o_spec, o_shape = pl.BlockSpec((tm, tn), lambda i, j, kk: (i, j)), (m, n)
    has_add = add is not None

    def body(*refs):
        if has_add:
            a_ref, b_ref, add_ref, o_ref, acc_ref = refs
        else:
            a_ref, b_ref, o_ref, acc_ref = refs
        kk = pl.program_id(2)

        @pl.when(kk == 0)
        def _():
            acc_ref[...] = jnp.zeros_like(acc_ref)

        if hi:
            acc_ref[...] += lax.dot_general(a_ref[...], b_ref[...], (dims, ((), ())), precision=HI,
                                            preferred_element_type=f32)
        else:
            acc_ref[...] += _d16(a_ref[...], b_ref[...], dims)

        @pl.when(kk == nk - 1)
        def _():
            r = acc_ref[...]
            if has_add:
                r = r + add_scale * add_ref[...]
            o_ref[...] = r.astype(out_dtype)

    return pl.pallas_call(
        body, name=name, grid=(m // tm, n // tn, nk),
        in_specs=[a_spec, b_spec] + ([o_spec] if has_add else []), out_specs=o_spec,
        out_shape=jax.ShapeDtypeStruct(o_shape, out_dtype),
        scratch_shapes=[pltpu.VMEM((tm, tn), f32)],
        compiler_params=_params("parallel", "parallel", "arbitrary"),
    )(*((a, b, add) if has_add else (a, b)))


def cast_bf16(a, *, name):
    m, n = a.shape
    tm = _tile(m, 512, 16)

    def body(a_ref, o_ref):
        o_ref[...] = a_ref[...].astype(bf16)

    return pl.pallas_call(body, name=name, grid=(m // tm,), in_specs=[pl.BlockSpec((tm, n), lambda i: (i, 0))],
                          out_specs=pl.BlockSpec((tm, n), lambda i: (i, 0)), out_shape=jax.ShapeDtypeStruct((m, n), bf16),
                          compiler_params=_params("parallel"))(a)


def _ln_stats(z):
    mu = jnp.mean(z, axis=-1, keepdims=True)
    zc = z - mu
    var = jnp.mean(zc * zc, axis=-1, keepdims=True)
    rstd = lax.rsqrt(var + NORM_EPS)
    return zc * rstd, rstd


def ln_fwd(z, g, b, *, name):
    s, d = z.shape
    tm = _tile(s, 256, 16)

    def body(z_ref, g_ref, b_ref, y_ref, yb_ref):
        xhat, _ = _ln_stats(z_ref[...])
        y = xhat * g_ref[...] + b_ref[...]
        y_ref[...] = y
        yb_ref[...] = y.astype(bf16)

    row = pl.BlockSpec((tm, d), lambda i: (i, 0))
    vec = pl.BlockSpec((1, d), lambda i: (0, 0))
    return pl.pallas_call(body, name=name, grid=(s // tm,), in_specs=[row, vec, vec], out_specs=[row, row],
                          out_shape=[jax.ShapeDtypeStruct((s, d), f32), jax.ShapeDtypeStruct((s, d), bf16)],
                          compiler_params=_params("parallel"))(z, g, b)


def _ln_bwd_tile(dy, z, g):
    xhat, rstd = _ln_stats(z)
    dxh = dy * g
    m1 = jnp.mean(dxh, axis=-1, keepdims=True)
    m2 = jnp.mean(dxh * xhat, axis=-1, keepdims=True)
    dz = rstd * (dxh - m1 - xhat * m2)
    return dz, jnp.sum(dy * xhat, axis=0, keepdims=True), jnp.sum(dy, axis=0, keepdims=True)


def ln_bwd(dy, z, g, *, name):
    s, d = z.shape
    tm = _tile(s, 256, 16)

    def body(dy_ref, z_ref, g_ref, dz_ref, dzb_ref, dg_ref, db_ref):
        @pl.when(pl.program_id(0) == 0)
        def _():
            dg_ref[...] = jnp.zeros_like(dg_ref)
            db_ref[...] = jnp.zeros_like(db_ref)

        dz, dg, db = _ln_bwd_tile(dy_ref[...], z_ref[...], g_ref[...])
        dz_ref[...] = dz
        dzb_ref[...] = dz.astype(bf16)
        dg_ref[...] += dg
        db_ref[...] += db

    row = pl.BlockSpec((tm, d), lambda i: (i, 0))
    vec = pl.BlockSpec((1, d), lambda i: (0, 0))
    return pl.pallas_call(body, name=name, grid=(s // tm,), in_specs=[row, row, vec], out_specs=[row, row, vec, vec],
                          out_shape=[jax.ShapeDtypeStruct((s, d), f32), jax.ShapeDtypeStruct((s, d), bf16),
                                     jax.ShapeDtypeStruct((1, d), f32), jax.ShapeDtypeStruct((1, d), f32)],
                          compiler_params=_params("arbitrary"))(dy, z, g)


def ln_loss(z, target, g, b, *, name):
    s, d = z.shape
    tm = _tile(s, 256, 16)
    nt = s // tm

    def body(z_ref, t_ref, g_ref, b_ref, dz_ref, dzb_ref, dg_ref, db_ref, loss_ref, lacc_ref):
        i = pl.program_id(0)

        @pl.when(i == 0)
        def _():
            dg_ref[...] = jnp.zeros_like(dg_ref)
            db_ref[...] = jnp.zeros_like(db_ref)
            lacc_ref[...] = jnp.zeros_like(lacc_ref)

        zv, gv = z_ref[...], g_ref[...]
        xhat, _ = _ln_stats(zv)
        err = xhat * gv + b_ref[...] - t_ref[...]
        lacc_ref[...] += jnp.sum(err * err, axis=0, keepdims=True)
        dz, dg, db = _ln_bwd_tile(err * (1.0 / d), zv, gv)
        dz_ref[...] = dz
        dzb_ref[...] = dz.astype(bf16)
        dg_ref[...] += dg
        db_ref[...] += db

        @pl.when(i == nt - 1)
        def _():
            loss_ref[...] = (0.5 / d) * jnp.sum(lacc_ref[...], axis=1, keepdims=True)

    row = pl.BlockSpec((tm, d), lambda i: (i, 0))
    vec = pl.BlockSpec((1, d), lambda i: (0, 0))
    return pl.pallas_call(body, name=name, grid=(nt,), in_specs=[row, row, vec, vec],
                          out_specs=[row, row, vec, vec, pl.BlockSpec((1, 1), lambda i: (0, 0))],
                          out_shape=[jax.ShapeDtypeStruct((s, d), f32), jax.ShapeDtypeStruct((s, d), bf16),
                                     jax.ShapeDtypeStruct((1, d), f32), jax.ShapeDtypeStruct((1, d), f32),
                                     jax.ShapeDtypeStruct((1, 1), f32)],
                          scratch_shapes=[pltpu.VMEM((1, d), f32)],
                          compiler_params=_params("arbitrary"))(z, target, g, b)


def merge_fwd(gates, ys, yg, *, name):
    s, d = ys.shape
    tm = _tile(s, 256, 16)

    def body(gt_ref, ys_ref, yg_ref, o_ref):
        o_ref[...] = (_sigmoid(gt_ref[:, :d]) * ys_ref[...] + _sigmoid(gt_ref[:, d:]) * yg_ref[...]).astype(bf16)

    row = pl.BlockSpec((tm, d), lambda i: (i, 0))
    return pl.pallas_call(body, name=name, grid=(s // tm,), in_specs=[pl.BlockSpec((tm, 2 * d), lambda i: (i, 0)), row, row],
                          out_specs=row, out_shape=jax.ShapeDtypeStruct((s, d), bf16),
                          compiler_params=_params("parallel"))(gates, ys, yg)


def merge_bwd(gates, ys, yg, dmix, *, name):
    s, d = ys.shape
    tm = _tile(s, 256, 16)

    def body(gt_ref, ys_ref, yg_ref, dm_ref, dys_ref, dyg_ref, dgt_ref):
        dm = dm_ref[...]
        sa, sb = _sigmoid(gt_ref[:, :d]), _sigmoid(gt_ref[:, d:])
        dys_ref[...] = (dm * sa).astype(bf16)
        dyg_ref[...] = (dm * sb).astype(bf16)
        dgt_ref[:, :d] = (dm * ys_ref[...] * sa * (1.0 - sa)).astype(bf16)
        dgt_ref[:, d:] = (dm * yg_ref[...] * sb * (1.0 - sb)).astype(bf16)

    row = pl.BlockSpec((tm, d), lambda i: (i, 0))
    wide = pl.BlockSpec((tm, 2 * d), lambda i: (i, 0))
    return pl.pallas_call(body, name=name, grid=(s // tm,), in_specs=[wide, row, row, row], out_specs=[row, row, wide],
                          out_shape=[jax.ShapeDtypeStruct((s, d), bf16), jax.ShapeDtypeStruct((s, d), bf16),
                                     jax.ShapeDtypeStruct((s, 2 * d), bf16)],
                          compiler_params=_params("parallel"))(gates, ys, yg, dmix)


def _shift_down(ext, j):
    return ext if j == 0 else pltpu.roll(ext, j, 0)


def _shift_up(ext, j):
    return ext if j == 0 else pltpu.roll(ext, ext.shape[0] - j, 0)


def _causal_conv(ext, w_ref, width):
    acc = None
    for j in range(width):
        term = w_ref[j:j + 1, :] * _shift_down(ext, width - 1 - j)
        acc = term if acc is None else acc + term
    return acc[HALO:]


def _conv_input_grad(dy_ext, w_ref, width, rows):
    acc = None
    for j in range(width):
        term = w_ref[j:j + 1, :] * _shift_up(dy_ext, width - 1 - j)
        acc = term if acc is None else acc + term
    return acc[:rows]


def _conv_weight_grad(dy, ext, width, rows):
    return [jnp.sum(dy * _shift_down(ext, width - 1 - j)[HALO:HALO + rows], axis=0, keepdims=True) for j in range(width)]


def _rows_to_block(rows, n_rows, cols):
    r = lax.broadcasted_iota(jnp.int32, (n_rows, cols), 0)
    out = jnp.zeros((n_rows, cols), f32)
    for j, v in enumerate(rows):
        out = out + jnp.where(r == j, v, 0.0)
    return out


def _silu_and_grad(v):
    sg = _sigmoid(v)
    return v * sg, sg * (1.0 + v * (1.0 - sg))


def ffn_act_fwd(hpre, cwb, *, name):
    s = hpre.shape[0]
    tm = _tile(s, 256, 16)
    hb = tm // HALO

    def body(hg_ref, hgp_ref, hu_ref, hup_ref, cg_ref, cu_ref, o_ref):
        first = pl.program_id(1) == 0

        def conv(h_ref, hp_ref, c_ref):
            prev = jnp.where(first, 0.0, hp_ref[...])
            return _causal_conv(jnp.concatenate([prev, h_ref[...]], axis=0), c_ref.at[0], 3) + c_ref[0, 3:4, :]

        g = conv(hg_ref, hgp_ref, cg_ref)
        u = conv(hu_ref, hup_ref, cu_ref)
        o_ref[...] = (g * _sigmoid(g) * u).astype(bf16)

    def tile(off):
        return pl.BlockSpec((tm, FF_PAD), lambda j, i: (i, j + off))

    def halo(off):
        return pl.BlockSpec((HALO, FF_PAD), lambda j, i: (jnp.maximum(i * hb - 1, 0), j + off))

    def taps(off):
        return pl.BlockSpec((1, 8, FF_PAD), lambda j, i: (j + off, 0, 0))

    return pl.pallas_call(body, name=name, grid=(4, s // tm),
                          in_specs=[tile(0), halo(0), tile(4), halo(4), taps(0), taps(4)],
                          out_specs=pl.BlockSpec((tm, FF_PAD), lambda j, i: (i, j)),
                          out_shape=jax.ShapeDtypeStruct((s, 4 * FF_PAD), bf16),
                          compiler_params=_params("parallel", "parallel"))(hpre, hpre, hpre, hpre, cwb, cwb)


def ffn_act_bwd(hpre, dact, cwb, *, name):
    s = hpre.shape[0]
    tm = _tile(s, 256, 16)
    hb = tm // HALO
    nt = s // tm
    last_hb = s // HALO - 1

    def body(hg_ref, hgp_ref, hgn_ref, hu_ref, hup_ref, hun_ref, d_ref, dn_ref, cg_ref, cu_ref,
             dhg_ref, dhu_ref, dcg_ref, dcu_ref):
        i = pl.program_id(1)
        first, last = i == 0, i == nt - 1

        @pl.when(first)
        def _():
            dcg_ref[...] = jnp.zeros_like(dcg_ref)
            dcu_ref[...] = jnp.zeros_like(dcu_ref)

        def ext_of(h_ref, hp_ref, hn_ref):
            return jnp.concatenate([jnp.where(first, 0.0, hp_ref[...]), h_ref[...], hn_ref[...]], axis=0)

        eg, eu = ext_of(hg_ref, hgp_ref, hgn_ref), ext_of(hu_ref, hup_ref, hun_ref)
        g = _causal_conv(eg, cg_ref.at[0], 3) + cg_ref[0, 3:4, :]
        u = _causal_conv(eu, cu_ref.at[0], 3) + cu_ref[0, 3:4, :]
        d = jnp.concatenate([d_ref[...], jnp.where(last, 0.0, dn_ref[...])], axis=0)
        act, dact_dg = _silu_and_grad(g)
        dg = d * u * dact_dg
        du = d * act
        dhg_ref[...] = _conv_input_grad(dg, cg_ref.at[0], 3, tm).astype(bf16)
        dhu_ref[...] = _conv_input_grad(du, cu_ref.at[0], 3, tm).astype(bf16)
        dgt, dut = dg[:tm], du[:tm]
        dcg_ref[0] += _rows_to_block(_conv_weight_grad(dgt, eg, 3, tm) + [jnp.sum(dgt, axis=0, keepdims=True)], 8, FF_PAD)
        dcu_ref[0] += _rows_to_block(_conv_weight_grad(dut, eu, 3, tm) + [jnp.sum(dut, axis=0, keepdims=True)], 8, FF_PAD)

    def tile(off):
        return pl.BlockSpec((tm, FF_PAD), lambda j, i: (i, j + off))

    def prev(off):
        return pl.BlockSpec((HALO, FF_PAD), lambda j, i: (jnp.maximum(i * hb - 1, 0), j + off))

    def nxt(off):
        return pl.BlockSpec((HALO, FF_PAD), lambda j, i: (jnp.minimum((i + 1) * hb, last_hb), j + off))

    def taps(off):
        return pl.BlockSpec((1, 8, FF_PAD), lambda j, i: (j + off, 0, 0))

    dhg, dhu, dcg, dcu = pl.pallas_call(
        body, name=name, grid=(4, nt),
        in_specs=[tile(0), prev(0), nxt(0), tile(4), prev(4), nxt(4), tile(0), nxt(0), taps(0), taps(4)],
        out_specs=[tile(0), tile(0), taps(0), taps(0)],
        out_shape=[jax.ShapeDtypeStruct((s, 4 * FF_PAD), bf16), jax.ShapeDtypeStruct((s, 4 * FF_PAD), bf16),
                   jax.ShapeDtypeStruct((4, 8, FF_PAD), f32), jax.ShapeDtypeStruct((4, 8, FF_PAD), f32)],
        compiler_params=_params("parallel", "arbitrary"),
    )(hpre, hpre, hpre, hpre, hpre, hpre, dact, dact, cwb, cwb)
    return jnp.concatenate([dhg, dhu], axis=1), jnp.concatenate([dcg, dcu], axis=0)


MEM_SCALE = MEM_HEAD_DIM ** -0.5


def _softmax_rows(sc):
    m = jnp.max(sc, axis=-1, keepdims=True)
    e = jnp.exp(sc - m)
    return e / jnp.sum(e, axis=-1, keepdims=True)


def memattn_fwd(qm, kv, *, name):
    s = qm.shape[0]
    mlen = kv.shape[0]
    tm = _tile(s, 512, 16)

    def body(q_ref, kv_ref, o_ref):
        for h in range(MEM_HEADS):
            lo = h * MEM_HEAD_DIM
            q = q_ref[:, lo:lo + MEM_HEAD_DIM]
            k = kv_ref[:, lo:lo + MEM_HEAD_DIM]
            v = kv_ref[:, MEM_W + lo:MEM_W + lo + MEM_HEAD_DIM]
            p = _softmax_rows(_d16(q, k, NT) * MEM_SCALE)
            o_ref[:, lo:lo + MEM_HEAD_DIM] = _d16(p, v, NN).astype(bf16)

    return pl.pallas_call(body, name=name, grid=(s // tm,),
                          in_specs=[pl.BlockSpec((tm, MEM_W), lambda i: (i, 0)), pl.BlockSpec((mlen, 2 * MEM_W), lambda i: (0, 0))],
                          out_specs=pl.BlockSpec((tm, MEM_W), lambda i: (i, 0)),
                          out_shape=jax.ShapeDtypeStruct((s, MEM_W), bf16), compiler_params=_params("parallel"))(qm, kv)


def memattn_bwd(qm, kv, dout, *, name):
    s = qm.shape[0]
    mlen = kv.shape[0]
    tm = _tile(s, 512, 16)

    def body(q_ref, kv_ref, do_ref, dq_ref, dkv_ref):
        @pl.when(pl.program_id(0) == 0)
        def _():
            dkv_ref[...] = jnp.zeros_like(dkv_ref)

        for h in range(MEM_HEADS):
            lo = h * MEM_HEAD_DIM
            q = q_ref[:, lo:lo + MEM_HEAD_DIM]
            k = kv_ref[:, lo:lo + MEM_HEAD_DIM]
            v = kv_ref[:, MEM_W + lo:MEM_W + lo + MEM_HEAD_DIM]
            do = do_ref[:, lo:lo + MEM_HEAD_DIM]
            p = _softmax_rows(_d16(q, k, NT) * MEM_SCALE)
            dp = _d16(do, v, NT)
            ds = p * (dp - jnp.sum(p * dp, axis=-1, keepdims=True)) * MEM_SCALE
            dq_ref[:, lo:lo + MEM_HEAD_DIM] = _d16(ds, k, NN).astype(bf16)
            dkv_ref[:, lo:lo + MEM_HEAD_DIM] += _d16(ds, q, TN)
            dkv_ref[:, MEM_W + lo:MEM_W + lo + MEM_HEAD_DIM] += _d16(p, do, TN)

    row = pl.BlockSpec((tm, MEM_W), lambda i: (i, 0))
    full = pl.BlockSpec((mlen, 2 * MEM_W), lambda i: (0, 0))
    return pl.pallas_call(body, name=name, grid=(s // tm,), in_specs=[row, full, row], out_specs=[row, full],
                          out_shape=[jax.ShapeDtypeStruct((s, MEM_W), bf16), jax.ShapeDtypeStruct((mlen, 2 * MEM_W), f32)],
                          compiler_params=_params("arbitrary"))(qm, kv, dout)


SWA_SCALE = SWA_HEAD_DIM ** -0.5
SWA_GROUP = SWA_HEADS // SWA_KV_HEADS
SWA_IN_W = 1408
K_COL, V_COL, BA_COL = SWA_Q // 128, SWA_Q // 128 + 1, SWA_Q // 128 + 2


def _swa_mask(n):
    qi = lax.broadcasted_iota(jnp.int32, (BLOCK, 2 * BLOCK), 0)
    kj = lax.broadcasted_iota(jnp.int32, (BLOCK, 2 * BLOCK), 1)
    dist = qi + BLOCK - kj
    return (dist >= 0) & (dist < BLOCK) & ((n > 0) | (kj >= BLOCK))


def _swa_probs(q, k, bias, sink, mask):
    sc = jnp.where(mask, _d16(q, k, NT) * SWA_SCALE + bias, NEG_INF)
    m = jnp.maximum(jnp.max(sc, axis=-1, keepdims=True), sink)
    e = jnp.exp(sc - m)
    es = jnp.exp(sink - m)
    inv = 1.0 / (jnp.sum(e, axis=-1, keepdims=True) + es)
    return e * inv, es * inv


def _swa_specs():
    q_spec = pl.BlockSpec((BLOCK, SWA_Q), lambda n: (n, 0))

    def band(col):
        return [pl.BlockSpec((BLOCK, SWA_KV), lambda n: (jnp.maximum(n - 1, 0), col)),
                pl.BlockSpec((BLOCK, SWA_KV), lambda n: (n, col))]

    bias_spec = pl.BlockSpec((SWA_HEADS, BLOCK, 2 * BLOCK), lambda n: (0, 0, 0))
    sink_spec = pl.BlockSpec((1, SWA_HEADS), lambda n: (0, 0))
    return [q_spec] + band(K_COL) + band(V_COL) + [bias_spec, sink_spec]


def swa_fwd(swa_in, bias, sinks, *, name):
    s = swa_in.shape[0]

    def body(q_ref, kp_ref, kc_ref, vp_ref, vc_ref, bias_ref, sink_ref, o_ref):
        mask = _swa_mask(pl.program_id(0))
        kb = jnp.concatenate([kp_ref[...], kc_ref[...]], axis=0)
        vb = jnp.concatenate([vp_ref[...], vc_ref[...]], axis=0)
        for h in range(SWA_HEADS):
            lo, kl = h * SWA_HEAD_DIM, (h // SWA_GROUP) * SWA_HEAD_DIM
            p, _ = _swa_probs(q_ref[:, lo:lo + SWA_HEAD_DIM], kb[:, kl:kl + SWA_HEAD_DIM], bias_ref[h],
                              sink_ref[:, h:h + 1], mask)
            o_ref[:, lo:lo + SWA_HEAD_DIM] = _d16(p, vb[:, kl:kl + SWA_HEAD_DIM], NN).astype(bf16)

    return pl.pallas_call(body, name=name, grid=(s // BLOCK,), in_specs=_swa_specs(),
                          out_specs=pl.BlockSpec((BLOCK, SWA_Q), lambda n: (n, 0)),
                          out_shape=jax.ShapeDtypeStruct((s, SWA_Q), bf16),
                          compiler_params=_params("parallel"))(swa_in, swa_in, swa_in, swa_in, swa_in, bias, sinks)


def swa_bwd(swa_in, bias, sinks, dout, *, name):
    s = swa_in.shape[0]

    def body(q_ref, kp_ref, kc_ref, vp_ref, vc_ref, bias_ref, sink_ref, do_ref,
             dq_ref, dkc_ref, dkp_ref, dvc_ref, dvp_ref, dbias_ref, dsink_ref):
        n = pl.program_id(0)

        @pl.when(n == 0)
        def _():
            dbias_ref[...] = jnp.zeros_like(dbias_ref)
            dsink_ref[...] = jnp.zeros_like(dsink_ref)

        mask = _swa_mask(n)
        kb = jnp.concatenate([kp_ref[...], kc_ref[...]], axis=0)
        vb = jnp.concatenate([vp_ref[...], vc_ref[...]], axis=0)
        lane = lax.broadcasted_iota(jnp.int32, (1, 128), 1)
        dsink = jnp.zeros((1, 128), f32)
        for g in range(SWA_KV_HEADS):
            kl = g * SWA_HEAD_DIM
            k, v = kb[:, kl:kl + SWA_HEAD_DIM], vb[:, kl:kl + SWA_HEAD_DIM]
            dk = jnp.zeros((2 * BLOCK, SWA_HEAD_DIM), f32)
            dv = jnp.zeros((2 * BLOCK, SWA_HEAD_DIM), f32)
            for hh in range(SWA_GROUP):
                h = g * SWA_GROUP + hh
                lo = h * SWA_HEAD_DIM
                q = q_ref[:, lo:lo + SWA_HEAD_DIM]
                do = do_ref[:, lo:lo + SWA_HEAD_DIM]
                p, ps = _swa_probs(q, k, bias_ref[h], sink_ref[:, h:h + 1], mask)
                dp = _d16(do, v, NT)
                delta = jnp.sum(p * dp, axis=-1, keepdims=True)
                ds = p * (dp - delta)
                dbias_ref[h] += ds
                dsink = dsink + jnp.where(lane == h, -jnp.sum(ps * delta, axis=0, keepdims=True), 0.0)
                dss = ds * SWA_SCALE
                dq_ref[:, lo:lo + SWA_HEAD_DIM] = _d16(dss, k, NN)
                dk = dk + _d16(dss, q, TN)
                dv = dv + _d16(p, do, TN)
            dkp_ref[:, kl:kl + SWA_HEAD_DIM] = dk[:BLOCK]
            dkc_ref[:, kl:kl + SWA_HEAD_DIM] = dk[BLOCK:]
            dvp_ref[:, kl:kl + SWA_HEAD_DIM] = dv[:BLOCK]
            dvc_ref[:, kl:kl + SWA_HEAD_DIM] = dv[BLOCK:]
        dsink_ref[...] += dsink

    qs = pl.BlockSpec((BLOCK, SWA_Q), lambda n: (n, 0))
    ks = pl.BlockSpec((BLOCK, SWA_KV), lambda n: (n, 0))
    return pl.pallas_call(
        body, name=name, grid=(s // BLOCK,), in_specs=_swa_specs() + [qs],
        out_specs=[qs, ks, ks, ks, ks, pl.BlockSpec((SWA_HEADS, BLOCK, 2 * BLOCK), lambda n: (0, 0, 0)),
                   pl.BlockSpec((1, 128), lambda n: (0, 0))],
        out_shape=[jax.ShapeDtypeStruct((s, SWA_Q), f32)] + [jax.ShapeDtypeStruct((s, SWA_KV), f32)] * 4
        + [jax.ShapeDtypeStruct((SWA_HEADS, BLOCK, 2 * BLOCK), f32), jax.ShapeDtypeStruct((1, 128), f32)],
        compiler_params=_params("arbitrary"),
    )(swa_in, swa_in, swa_in, swa_in, swa_in, bias, sinks, dout)


def swa_in_grad(dq, dkc, dkp, dvc, dvp, dba, *, name):
    s = dq.shape[0]
    nb = s // BLOCK

    def body(dq_ref, dkc_ref, dkp_ref, dvc_ref, dvp_ref, dba_ref, o_ref):
        has_next = pl.program_id(0) < nb - 1
        o_ref[:, :SWA_Q] = dq_ref[...].astype(bf16)
        o_ref[:, SWA_Q:SWA_Q + SWA_KV] = (dkc_ref[...] + jnp.where(has_next, dkp_ref[...], 0.0)).astype(bf16)
        o_ref[:, SWA_Q + SWA_KV:SWA_Q + 2 * SWA_KV] = (dvc_ref[...] + jnp.where(has_next, dvp_ref[...], 0.0)).astype(bf16)
        o_ref[:, SWA_Q + 2 * SWA_KV:] = dba_ref[...].astype(bf16)

    cur = pl.BlockSpec((BLOCK, SWA_KV), lambda n: (n, 0))
    nxt = pl.BlockSpec((BLOCK, SWA_KV), lambda n: (jnp.minimum(n + 1, nb - 1), 0))
    return pl.pallas_call(body, name=name, grid=(nb,),
                          in_specs=[pl.BlockSpec((BLOCK, SWA_Q), lambda n: (n, 0)), cur, nxt, cur, nxt, cur],
                          out_specs=pl.BlockSpec((BLOCK, SWA_IN_W), lambda n: (n, 0)),
                          out_shape=jax.ShapeDtypeStruct((s, SWA_IN_W), bf16),
                          compiler_params=_params("parallel"))(dq, dkc, dkp, dvc, dvp, dba)


def _bucket_onehot():
    qi = jnp.arange(BLOCK)[:, None]
    kj = jnp.arange(2 * BLOCK)[None, :]
    dist = jnp.maximum(qi + BLOCK - kj, 0)
    max_exact = REL_BUCKETS // 2
    dd = jnp.maximum(dist, 1).astype(f32)
    large = max_exact + (jnp.log(dd / max_exact) / math.log(REL_MAX_DIST / max_exact) * (REL_BUCKETS - max_exact)).astype(jnp.int32)
    bucket = jnp.where(dist < max_exact, dist, jnp.minimum(large, REL_BUCKETS - 1)).reshape(-1)
    return (bucket[None, :] == jnp.arange(REL_BUCKETS)[:, None]).astype(f32)


def _gbeta_fn(ba, alog_row, dt_row):
    col = lax.broadcasted_iota(jnp.int32, ba.shape, 1)
    v = ba + dt_row
    softplus = jnp.maximum(v, 0.0) + jnp.log(1.0 + jnp.exp(-jnp.abs(v)))
    g = -jnp.exp(alog_row) * softplus
    return jnp.where(col < GDN_HEADS, _sigmoid(ba), jnp.where(col < 2 * GDN_HEADS, g, 0.0))


def gbeta_fwd(swa_in, alog_row, dt_row, *, name):
    s = swa_in.shape[0]
    tm = _tile(s, 512, 8)

    def body(ba_ref, a_ref, d_ref, o_ref):
        o_ref[...] = _gbeta_fn(ba_ref[...], a_ref[...], d_ref[...])

    vec = pl.BlockSpec((1, 128), lambda i: (0, 0))
    return pl.pallas_call(body, name=name, grid=(s // tm,), in_specs=[pl.BlockSpec((tm, 128), lambda i: (i, BA_COL)), vec, vec],
                          out_specs=pl.BlockSpec((tm, 128), lambda i: (i, 0)), out_shape=jax.ShapeDtypeStruct((s, 128), f32),
                          compiler_params=_params("parallel"))(swa_in, alog_row, dt_row)


def gbeta_bwd(swa_in, alog_row, dt_row, dgbeta, *, name):
    s = swa_in.shape[0]
    tm = _tile(s, 512, 8)

    def body(ba_ref, a_ref, d_ref, dgb_ref, dba_ref, da_ref, dd_ref):
        @pl.when(pl.program_id(0) == 0)
        def _():
            da_ref[...] = jnp.zeros_like(da_ref)
            dd_ref[...] = jnp.zeros_like(dd_ref)

        _, pull = jax.vjp(_gbeta_fn, ba_ref[...], a_ref[...], d_ref[...])
        dba, da, dd = pull(dgb_ref[...])
        dba_ref[...] = dba
        da_ref[...] += da
        dd_ref[...] += dd

    vec = pl.BlockSpec((1, 128), lambda i: (0, 0))
    row = pl.BlockSpec((tm, 128), lambda i: (i, 0))
    return pl.pallas_call(body, name=name, grid=(s // tm,),
                          in_specs=[pl.BlockSpec((tm, 128), lambda i: (i, BA_COL)), vec, vec, row], out_specs=[row, vec, vec],
                          out_shape=[jax.ShapeDtypeStruct((s, 128), f32), jax.ShapeDtypeStruct((1, 128), f32),
                                     jax.ShapeDtypeStruct((1, 128), f32)],
                          compiler_params=_params("arbitrary"))(swa_in, alog_row, dt_row, dgbeta)


QKV_W = 3 * GDN_W


def gdn_pre_fwd(gdn_in, convw, *, name):
    s = gdn_in.shape[0]
    tm = _tile(s, 256, 16)
    hb = tm // HALO

    def body(x_ref, xp_ref, w_ref, q_ref, k_ref, v_ref):
        prev = jnp.where(pl.program_id(0) == 0, 0.0, xp_ref[...])
        pre = _causal_conv(jnp.concatenate([prev, x_ref[...]], axis=0), w_ref, GDN_CONV)
        act = pre * _sigmoid(pre)
        for h in range(GDN_HEADS):
            lo = h * GDN_HEAD_DIM
            for off, o_ref in ((0, q_ref), (GDN_W, k_ref)):
                seg = act[:, off + lo:off + lo + GDN_HEAD_DIM]
                o_ref[:, lo:lo + GDN_HEAD_DIM] = seg * lax.rsqrt(jnp.sum(seg * seg, axis=-1, keepdims=True) + 1e-6)
        v_ref[...] = act[:, 2 * GDN_W:]

    out = pl.BlockSpec((tm, GDN_W), lambda i: (i, 0))
    return pl.pallas_call(body, name=name, grid=(s // tm,),
                          in_specs=[pl.BlockSpec((tm, QKV_W), lambda i: (i, 0)),
                                    pl.BlockSpec((HALO, QKV_W), lambda i: (jnp.maximum(i * hb - 1, 0), 0)),
                                    pl.BlockSpec((8, QKV_W), lambda i: (0, 0))],
                          out_specs=[out, out, out], out_shape=[jax.ShapeDtypeStruct((s, GDN_W), f32)] * 3,
                          compiler_params=_params("parallel"))(gdn_in, gdn_in, convw)


def gdn_pre_bwd(gdn_in, convw, dqn, dkn, dv, dgz, *, name):
    s = gdn_in.shape[0]
    tm = _tile(s, 128, 16)
    hb = tm // HALO
    nt = s // tm
    last_hb = s // HALO - 1

    def body(x_ref, xp_ref, xn_ref, w_ref, dq_ref, dqx_ref, dk_ref, dkx_ref, dv_ref, dvx_ref, dz_ref, o_ref, dw_ref):
        i = pl.program_id(0)
        first, last = i == 0, i == nt - 1

        @pl.when(first)
        def _():
            dw_ref[...] = jnp.zeros_like(dw_ref)

        ext = jnp.concatenate([jnp.where(first, 0.0, xp_ref[...]), x_ref[...], xn_ref[...]], axis=0)
        pre = _causal_conv(ext, w_ref, GDN_CONV)
        act, dact_dpre = _silu_and_grad(pre)

        def with_future(t_ref, n_ref):
            return jnp.concatenate([t_ref[...], jnp.where(last, 0.0, n_ref[...])], axis=0)

        dqe, dke, dve = with_future(dq_ref, dqx_ref), with_future(dk_ref, dkx_ref), with_future(dv_ref, dvx_ref)
        parts = []
        for off, dn in ((0, dqe), (GDN_W, dke)):
            for h in range(GDN_HEADS):
                lo = h * GDN_HEAD_DIM
                seg = act[:, off + lo:off + lo + GDN_HEAD_DIM]
                r = lax.rsqrt(jnp.sum(seg * seg, axis=-1, keepdims=True) + 1e-6)
                nrm = seg * r
                dseg = dn[:, lo:lo + GDN_HEAD_DIM]
                parts.append(r * (dseg - nrm * jnp.sum(dseg * nrm, axis=-1, keepdims=True)))
        dpre = jnp.concatenate(parts + [dve], axis=1) * dact_dpre
        o_ref[:, :QKV_W] = _conv_input_grad(dpre, w_ref, GDN_CONV, tm).astype(bf16)
        o_ref[:, QKV_W:] = dz_ref[...].astype(bf16)
        dw_ref[...] += _rows_to_block(_conv_weight_grad(dpre[:tm], ext, GDN_CONV, tm), 8, QKV_W)

    row = pl.BlockSpec((tm, GDN_W), lambda i: (i, 0))
    fut = pl.BlockSpec((HALO, GDN_W), lambda i: (jnp.minimum((i + 1) * hb, last_hb), 0))
    return pl.pallas_call(
        body, name=name, grid=(nt,),
        in_specs=[pl.BlockSpec((tm, QKV_W), lambda i: (i, 0)),
                  pl.BlockSpec((HALO, QKV_W), lambda i: (jnp.maximum(i * hb - 1, 0), 0)),
                  pl.BlockSpec((HALO, QKV_W), lambda i: (jnp.minimum((i + 1) * hb, last_hb), 0)),
                  pl.BlockSpec((8, QKV_W), lambda i: (0, 0)), row, fut, row, fut, row, fut, row],
        out_specs=[pl.BlockSpec((tm, 4 * GDN_W), lambda i: (i, 0)), pl.BlockSpec((8, QKV_W), lambda i: (0, 0))],
        out_shape=[jax.ShapeDtypeStruct((s, 4 * GDN_W), bf16), jax.ShapeDtypeStruct((8, QKV_W), f32)],
        compiler_params=_params("arbitrary"),
    )(gdn_in, gdn_in, gdn_in, convw, dqn, dqn, dkn, dkn, dv, dv, dgz)


def _gdn_post_head(o, z, nw):
    return o * lax.rsqrt(jnp.mean(o * o, axis=-1, keepdims=True) + 1e-6) * nw * (z * _sigmoid(z))


def gdn_post_fwd(o, gdn_in, nw, *, name):
    s = o.shape[0]
    tm = _tile(s, 256, 16)

    def body(o_ref, z_ref, nw_ref, y_ref):
        for h in range(GDN_HEADS):
            sl = slice(h * GDN_HEAD_DIM, (h + 1) * GDN_HEAD_DIM)
            y_ref[:, sl] = _gdn_post_head(o_ref[:, sl], z_ref[:, sl], nw_ref[...]).astype(bf16)

    row = pl.BlockSpec((tm, GDN_W), lambda i: (i, 0))
    return pl.pallas_call(body, name=name, grid=(s // tm,),
                          in_specs=[row, pl.BlockSpec((tm, GDN_W), lambda i: (i, 3)), pl.BlockSpec((1, 128), lambda i: (0, 0))],
                          out_specs=row, out_shape=jax.ShapeDtypeStruct((s, GDN_W), bf16),
                          compiler_params=_params("parallel"))(o, gdn_in, nw)


def gdn_post_bwd(o, gdn_in, nw, dy, *, name):
    s = o.shape[0]
    tm = _tile(s, 256, 16)

    def body(o_ref, z_ref, nw_ref, dy_ref, do_ref, dz_ref, dnw_ref):
        @pl.when(pl.program_id(0) == 0)
        def _():
            dnw_ref[...] = jnp.zeros_like(dnw_ref)

        dnw = jnp.zeros((1, 128), f32)
        for h in range(GDN_HEADS):
            sl = slice(h * GDN_HEAD_DIM, (h + 1) * GDN_HEAD_DIM)
            _, pull = jax.vjp(_gdn_post_head, o_ref[:, sl], z_ref[:, sl], nw_ref[...])
            do, dz, dn = pull(dy_ref[:, sl])
            do_ref[:, sl] = do
            dz_ref[:, sl] = dz
            dnw = dnw + dn
        dnw_ref[...] += dnw

    row = pl.BlockSpec((tm, GDN_W), lambda i: (i, 0))
    vec = pl.BlockSpec((1, 128), lambda i: (0, 0))
    return pl.pallas_call(body, name=name, grid=(s // tm,),
                          in_specs=[row, pl.BlockSpec((tm, GDN_W), lambda i: (i, 3)), vec, row], out_specs=[row, row, vec],
                          out_shape=[jax.ShapeDtypeStruct((s, GDN_W), f32), jax.ShapeDtypeStruct((s, GDN_W), f32),
                                     jax.ShapeDtypeStruct((1, 128), f32)],
                          compiler_params=_params("arbitrary"))(o, gdn_in, nw, dy)


def _gdn_chunks(q, k, v, g, beta, state):
    c = GDN_CHUNK
    heads = range(len(q))
    r = lax.broadcasted_iota(jnp.int32, (c, c), 0)
    cc = lax.broadcasted_iota(jnp.int32, (c, c), 1)
    tril, strict = r >= cc, r > cc
    eye = (r == cc).astype(f32)
    trilf, ones = tril.astype(f32), jnp.ones((c, c), f32)

    def dhi(a, b):
        return jnp.dot(a, b, precision=HI, preferred_element_type=f32)

    gi = [dhi(trilf, jnp.broadcast_to(g[h], (c, c))) for h in heads]
    gj = [dhi(ones, eye * gi[h]) for h in heads]
    decay = [jnp.where(tril, jnp.exp(jnp.where(tril, gi[h] - gj[h], 0.0)), 0.0) for h in heads]
    kb = [k[h] * beta[h] for h in heads]
    vb = [v[h] * beta[h] for h in heads]
    a = [jnp.where(strict, _d16(kb[h], k[h], NT) * decay[h], 0.0) for h in heads]
    tinv = [eye - a[h] for h in heads]
    pw = [dhi(a[h], a[h]) for h in heads]
    for it in range(5):
        tinv = [tinv[h] + dhi(tinv[h], pw[h]) for h in heads]
        if it < 4:
            pw = [dhi(pw[h], pw[h]) for h in heads]
    gc = [gi[h][:, 0:1] for h in heads]
    egc = [jnp.exp(gc[h]) for h in heads]
    u = [dhi(tinv[h], vb[h]) for h in heads]
    w = [dhi(tinv[h], kb[h] * egc[h]) for h in heads]
    qs = [q[h] * (GDN_HEAD_DIM ** -0.5) for h in heads]
    attn = [jnp.where(tril, _d16(qs[h], k[h], NT) * decay[h], 0.0) for h in heads]
    g_last = [gi[h][c - 1:c, 0:1] for h in heads]
    v_new = [u[h] - _d16(w[h], state[h], NN) for h in heads]
    out = [_d16(qs[h] * egc[h], state[h], NN) + _d16(attn[h], v_new[h], NN) for h in heads]
    new_state = [state[h] * jnp.exp(g_last[h]) + _d16(k[h] * jnp.exp(g_last[h] - gc[h]), v_new[h], TN) for h in heads]
    return out, new_state


def _head_cols(ref):
    return [ref[:, h * GDN_HEAD_DIM:(h + 1) * GDN_HEAD_DIM] for h in range(GDN_HEADS)]


def _chunk_gates(gb_ref):
    return ([gb_ref[:, GDN_HEADS + h:GDN_HEADS + h + 1] for h in range(GDN_HEADS)],
            [gb_ref[:, h:h + 1] for h in range(GDN_HEADS)])


def gdn_scan_fwd(qn, kn, v, gbeta, *, name):
    s = qn.shape[0]
    nc = s // GDN_CHUNK

    def body(q_ref, k_ref, v_ref, gb_ref, o_ref, st_ref, state_ref):
        @pl.when(pl.program_id(0) == 0)
        def _():
            state_ref[...] = jnp.zeros_like(state_ref)

        g, beta = _chunk_gates(gb_ref)
        states = [state_ref[h] for h in range(GDN_HEADS)]
        outs, new = _gdn_chunks(_head_cols(q_ref), _head_cols(k_ref), _head_cols(v_ref), g, beta, states)
        for h in range(GDN_HEADS):
            st_ref[0, h] = states[h]
            o_ref[:, h * GDN_HEAD_DIM:(h + 1) * GDN_HEAD_DIM] = outs[h]
            state_ref[h] = new[h]

    row = pl.BlockSpec((GDN_CHUNK, GDN_W), lambda n: (n, 0))
    return pl.pallas_call(
        body, name=name, grid=(nc,), in_specs=[row, row, row, pl.BlockSpec((GDN_CHUNK, 128), lambda n: (n, 0))],
        out_specs=[row, pl.BlockSpec((1, GDN_HEADS, GDN_HEAD_DIM, GDN_HEAD_DIM), lambda n: (n, 0, 0, 0))],
        out_shape=[jax.ShapeDtypeStruct((s, GDN_W), f32),
                   jax.ShapeDtypeStruct((nc, GDN_HEADS, GDN_HEAD_DIM, GDN_HEAD_DIM), f32)],
        scratch_shapes=[pltpu.VMEM((GDN_HEADS, GDN_HEAD_DIM, GDN_HEAD_DIM), f32)],
        compiler_params=_params("arbitrary"),
    )(qn, kn, v, gbeta)


def gdn_scan_bwd(qn, kn, v, gbeta, states, dout, *, name):
    s = qn.shape[0]
    nc = s // GDN_CHUNK

    def body(q_ref, k_ref, v_ref, gb_ref, st_ref, do_ref, dq_ref, dk_ref, dv_ref, dgb_ref, dstate_ref):
        @pl.when(pl.program_id(0) == 0)
        def _():
            dstate_ref[...] = jnp.zeros_like(dstate_ref)

        g, beta = _chunk_gates(gb_ref)
        _, pull = jax.vjp(_gdn_chunks, _head_cols(q_ref), _head_cols(k_ref), _head_cols(v_ref), g, beta,
                          [st_ref[0, h] for h in range(GDN_HEADS)])
        dq, dk, dv, dg, dbeta, dst = pull((_head_cols(do_ref), [dstate_ref[h] for h in range(GDN_HEADS)]))
        col = lax.broadcasted_iota(jnp.int32, (GDN_CHUNK, 128), 1)
        dgb = jnp.zeros((GDN_CHUNK, 128), f32)
        for h in range(GDN_HEADS):
            sl = slice(h * GDN_HEAD_DIM, (h + 1) * GDN_HEAD_DIM)
            dq_ref[:, sl] = dq[h]
            dk_ref[:, sl] = dk[h]
            dv_ref[:, sl] = dv[h]
            dstate_ref[h] = dst[h]
            dgb = dgb + jnp.where(col == h, dbeta[h], 0.0) + jnp.where(col == GDN_HEADS + h, dg[h], 0.0)
        dgb_ref[...] = dgb

    row = pl.BlockSpec((GDN_CHUNK, GDN_W), lambda n: (nc - 1 - n, 0))
    gb = pl.BlockSpec((GDN_CHUNK, 128), lambda n: (nc - 1 - n, 0))
    return pl.pallas_call(
        body, name=name, grid=(nc,),
        in_specs=[row, row, row, gb, pl.BlockSpec((1, GDN_HEADS, GDN_HEAD_DIM, GDN_HEAD_DIM), lambda n: (nc - 1 - n, 0, 0, 0)), row],
        out_specs=[row, row, row, gb],
        out_shape=[jax.ShapeDtypeStruct((s, GDN_W), f32)] * 3 + [jax.ShapeDtypeStruct((s, 128), f32)],
        scratch_shapes=[pltpu.VMEM((GDN_HEADS, GDN_HEAD_DIM, GDN_HEAD_DIM), f32)],
        compiler_params=_params("arbitrary"),
    )(qn, kn, v, gbeta, states, dout)


def adamw(w, g, m, v, *, name):
    r, c = w.shape
    tr = _tile(r, 256, 8)

    def body(w_ref, g_ref, m_ref, v_ref, d_ref, nm_ref, nv_ref):
        gv = g_ref[...]
        nm = ADAM_B1 * m_ref[...] + (1.0 - ADAM_B1) * gv
        nv = ADAM_B2 * v_ref[...] + (1.0 - ADAM_B2) * (gv * gv)
        m_hat = nm / (1.0 - ADAM_B1 ** ADAM_STEP)
        v_hat = nv / (1.0 - ADAM_B2 ** ADAM_STEP)
        d_ref[...] = -ADAM_LR * (m_hat / (jnp.sqrt(v_hat) + ADAM_EPS) + ADAM_WD * w_ref[...])
        nm_ref[...] = nm
        nv_ref[...] = nv

    spec = pl.BlockSpec((tr, c), lambda i: (i, 0))
    return pl.pallas_call(body, name=name, grid=(r // tr,), in_specs=[spec] * 4, out_specs=[spec] * 3,
                          out_shape=[jax.ShapeDtypeStruct((r, c), f32)] * 3, compiler_params=_params("parallel"))(w, g, m, v)


def _pos():
    return lax.axis_index("x"), lax.axis_index("y"), lax.axis_index("c")


ANY = pl.BlockSpec(memory_space=pl.ANY)


def all_gather_shards(shards, *, name):
    na = len(shards)

    def body(*refs):
        x_refs, out_refs = refs[:na], refs[na:2 * na]
        send_sems, recv_sems, local_sems = refs[2 * na:]
        x, y, cc = _pos()
        me, sibling = (x, y, cc), (x, y, 1 - cc)
        chips = [(1 - x, y), (x, 1 - y), (1 - x, 1 - y)]

        def copy(a, k, block, to, own=False):
            px, py, pc = block
            dst = out_refs[a].at[4 * px + 2 * py + pc]
            return pltpu.make_async_remote_copy(src_ref=x_refs[a] if own else dst, dst_ref=dst, send_sem=send_sems.at[7 * a + k],
                                                recv_sem=recv_sems.at[7 * a + k], device_id=to, device_id_type=MESH)

        mine = [pltpu.make_async_copy(x_refs[a], out_refs[a].at[4 * x + 2 * y + cc], local_sems.at[a]) for a in range(na)]
        first = [[copy(a, 0, me, sibling, own=True)] + [copy(a, 1 + j, me, (*chip, cc), own=True) for j, chip in enumerate(chips)]
                 for a in range(na)]
        for a in range(na):
            mine[a].start()
            for cp in first[a]:
                cp.start()
        passed = [[copy(a, 4 + j, (*chip, cc), sibling) for j, chip in enumerate(chips)] for a in range(na)]
        for a in range(na):
            for j, chip in enumerate(chips):
                copy(a, 1 + j, (*chip, cc), me).wait_recv()
                passed[a][j].start()
        for a in range(na):
            copy(a, 0, sibling, me).wait_recv()
            for j, chip in enumerate(chips):
                copy(a, 4 + j, (*chip, 1 - cc), me).wait_recv()
            for cp in first[a] + passed[a]:
                cp.wait_send()
            mine[a].wait()

    return pl.pallas_call(body, name=name, in_specs=[ANY] * na, out_specs=[ANY] * na,
                          out_shape=[jax.ShapeDtypeStruct((N_DEV,) + s.shape, s.dtype) for s in shards],
                          scratch_shapes=[pltpu.SemaphoreType.DMA((7 * na,)), pltpu.SemaphoreType.DMA((7 * na,)),
                                          pltpu.SemaphoreType.DMA((na,))])(*shards)


def exchange_d2d(chunks, *, name):
    na = len(chunks)

    def body(*refs):
        g_refs, out_refs = refs[:na], refs[na:2 * na]
        send_sems, recv_sems = refs[2 * na:]
        x, y, cc = _pos()
        for a in range(na):
            for q in range(4):
                pltpu.make_async_remote_copy(src_ref=g_refs[a].at[2 * q + 1 - cc], dst_ref=out_refs[a].at[q],
                                             send_sem=send_sems.at[a], recv_sem=recv_sems.at[a], device_id=(x, y, 1 - cc),
                                             device_id_type=MESH).start()
        for a in range(na):
            pltpu.make_async_remote_copy(src_ref=out_refs[a], dst_ref=out_refs[a], send_sem=send_sems.at[a],
                                         recv_sem=recv_sems.at[a], device_id=(x, y, 1 - cc), device_id_type=MESH).wait()

    return pl.pallas_call(body, name=name, in_specs=[ANY] * na, out_specs=[ANY] * na,
                          out_shape=[jax.ShapeDtypeStruct((4,) + g.shape[1:], g.dtype) for g in chunks],
                          scratch_shapes=[pltpu.SemaphoreType.DMA((na,)), pltpu.SemaphoreType.DMA((na,))])(*chunks)


def add_sibling(chunks, recv, *, name):
    _, r, c = chunks.shape
    tr = r if r <= 1024 else _tile(r, 512, 16)
    core = lax.axis_index("c").astype(jnp.int32).reshape(1)

    def body(core_ref, a_ref, b_ref, o_ref):
        o_ref[...] = (a_ref[...] + b_ref[...]).astype(bf16)

    return pl.pallas_call(
        body, name=name,
        grid_spec=pltpu.PrefetchScalarGridSpec(
            num_scalar_prefetch=1, grid=(4, r // tr),
            in_specs=[pl.BlockSpec((1, tr, c), lambda q, i, core_ref: (2 * q + core_ref[0], i, 0)),
                      pl.BlockSpec((1, tr, c), lambda q, i, core_ref: (q, i, 0))],
            out_specs=pl.BlockSpec((1, tr, c), lambda q, i, core_ref: (q, i, 0))),
        out_shape=jax.ShapeDtypeStruct((4, r, c), bf16), compiler_params=_params("parallel", "parallel"),
    )(core, chunks, recv)


def exchange_ici(parts, *, name):
    na = len(parts)

    def body(*refs):
        p_refs, out_refs = refs[:na], refs[na:2 * na]
        send_sems, recv_sems, local_sems = refs[2 * na:]
        x, y, cc = _pos()
        chips = [(1 - x, y), (x, 1 - y), (1 - x, 1 - y)]
        mine = [pltpu.make_async_copy(p_refs[a].at[2 * x + y], out_refs[a].at[3], local_sems.at[a]) for a in range(na)]
        cps = [pltpu.make_async_remote_copy(src_ref=p_refs[a].at[2 * px + py], dst_ref=out_refs[a].at[k],
                                            send_sem=send_sems.at[3 * a + k], recv_sem=recv_sems.at[3 * a + k],
                                            device_id=(px, py, cc), device_id_type=MESH)
               for a in range(na) for k, (px, py) in enumerate(chips)]
        for cp in mine + cps:
            cp.start()
        for cp in cps:
            cp.wait()
        for cp in mine:
            cp.wait()

    return pl.pallas_call(body, name=name, in_specs=[ANY] * na, out_specs=[ANY] * na,
                          out_shape=[jax.ShapeDtypeStruct(p.shape, p.dtype) for p in parts],
                          scratch_shapes=[pltpu.SemaphoreType.DMA((3 * na,)), pltpu.SemaphoreType.DMA((3 * na,)),
                                          pltpu.SemaphoreType.DMA((na,))])(*parts)


def add_four(r4, *, name):
    _, r, c = r4.shape
    tr = r if r <= 1024 else _tile(r, 512, 16)

    def body(a_ref, o_ref):
        o_ref[...] = ((a_ref[3].astype(f32) + a_ref[0].astype(f32)) + a_ref[1].astype(f32)) + a_ref[2].astype(f32)

    return pl.pallas_call(body, name=name, grid=(r // tr,), in_specs=[pl.BlockSpec((4, tr, c), lambda i: (0, i, 0))],
                          out_specs=pl.BlockSpec((tr, c), lambda i: (i, 0)), out_shape=jax.ShapeDtypeStruct((r, c), f32),
                          compiler_params=_params("parallel"))(r4)


def all_reduce_small(vec, *, name):
    r, c = vec.shape

    def body(v_ref, out_ref, buf_ref, send_sems, recv_sems):
        x, y, cc = _pos()
        my_id = 4 * x + 2 * y + cc
        buf_ref[my_id] = v_ref[...]
        flips = [(fx, fy, fc) for fx in (0, 1) for fy in (0, 1) for fc in (0, 1)][1:]
        cps = []
        for k, (fx, fy, fc) in enumerate(flips):
            peer = ((1 - x) if fx else x, (1 - y) if fy else y, (1 - cc) if fc else cc)
            cps.append(pltpu.make_async_remote_copy(src_ref=v_ref, dst_ref=buf_ref.at[my_id], send_sem=send_sems.at[k],
                                                    recv_sem=recv_sems.at[k], device_id=peer, device_id_type=MESH))
        for cp in cps:
            cp.start()
        for cp in cps:
            cp.wait()
        acc = buf_ref[0]
        for d in range(1, N_DEV):
            acc = acc + buf_ref[d]
        out_ref[...] = acc

    vm = pl.BlockSpec(memory_space=pltpu.VMEM)
    return pl.pallas_call(body, name=name, in_specs=[vm], out_specs=vm, out_shape=jax.ShapeDtypeStruct((r, c), f32),
                          scratch_shapes=[pltpu.VMEM((N_DEV, r, c), f32), pltpu.SemaphoreType.DMA((7,)),
                                          pltpu.SemaphoreType.DMA((7,))])(vec)


def _pack(parts, rows, dtype):
    flat = jnp.concatenate([p.reshape(-1).astype(dtype) for p in parts])
    return jnp.pad(flat, (0, rows * PACK_COLS - flat.shape[0])).reshape(rows, PACK_COLS)


def _unpack(flat, shapes):
    out, off = [], 0
    for shp in shapes:
        n = shp[0] * shp[1]
        out.append(flat[..., off:off + n].reshape(flat.shape[:-1] + tuple(shp)))
        off += n
    return out


W_IN_SHARD = IN_DIM // N_DEV
W_IN_PAD = 1280
W_IN_PARTS = (("swa", 0, 0, 1280), ("swa", 1280, 5376, 5392), ("gdn", 0, 1280, 5376), ("gates", 0, 5392, IN_DIM))
W_IN_WIDTHS = {"swa": SWA_IN_W, "gdn": 4 * GDN_W, "gates": 2 * D_MODEL}


def _w_in_segments():
    segs = []
    for part, p0, g0, g1 in W_IN_PARTS:
        for j in range(N_DEV):
            lo, hi = max(g0, W_IN_SHARD * j), min(g1, W_IN_SHARD * (j + 1))
            if lo < hi:
                segs.append((part, p0 + lo - g0, j, lo - W_IN_SHARD * j, hi - lo))
    return segs


def split_w_in(shards, *, name):
    dt = shards.dtype
    tm = 256

    def body(w_ref, swa_ref, gdn_ref, gates_ref):
        out = {"swa": swa_ref, "gdn": gdn_ref, "gates": gates_ref}
        swa_ref[:, SWA_Q + 2 * SWA_KV + 2 * GDN_HEADS:] = jnp.zeros((tm, SWA_IN_W - SWA_Q - 2 * SWA_KV - 2 * GDN_HEADS), dt)
        for part, p0, j, l0, n in _w_in_segments():
            out[part][:, p0:p0 + n] = w_ref[j, :, l0:l0 + n]

    return pl.pallas_call(body, name=name, grid=(D_MODEL // tm,),
                          in_specs=[pl.BlockSpec((N_DEV, tm, W_IN_PAD), lambda i: (0, i, 0))],
                          out_specs=[pl.BlockSpec((tm, W_IN_WIDTHS[p]), lambda i: (i, 0)) for p in ("swa", "gdn", "gates")],
                          out_shape=[jax.ShapeDtypeStruct((D_MODEL, W_IN_WIDTHS[p]), dt) for p in ("swa", "gdn", "gates")],
                          compiler_params=_params("parallel"))(shards)


def merge_w_in_grad(d_swa, d_gdn, d_gates, *, name):
    tm = 256

    def body(swa_ref, gdn_ref, gates_ref, w_ref):
        src = {"swa": swa_ref, "gdn": gdn_ref, "gates": gates_ref}
        w_ref[:, :, W_IN_SHARD:] = jnp.zeros((N_DEV, tm, W_IN_PAD - W_IN_SHARD), f32)
        for part, p0, j, l0, n in _w_in_segments():
            w_ref[j, :, l0:l0 + n] = src[part][:, p0:p0 + n]

    return pl.pallas_call(body, name=name, grid=(D_MODEL // tm,),
                          in_specs=[pl.BlockSpec((tm, W_IN_WIDTHS[p]), lambda i: (i, 0)) for p in ("swa", "gdn", "gates")],
                          out_specs=pl.BlockSpec((N_DEV, tm, W_IN_PAD), lambda i: (0, i, 0)),
                          out_shape=jax.ShapeDtypeStruct((N_DEV, D_MODEL, W_IN_PAD), f32),
                          compiler_params=_params("parallel"))(d_swa, d_gdn, d_gates)


def kernel(x, mem, w_in, rel_bias, swa_sinks, gdn_conv_w, gdn_a_log, gdn_dt_bias, gdn_norm_w, w_br_swa, w_br_gdn, w_mix_o, ln1_g, ln1_b, w_mem_q, w_mem_kv, w_mem_o, ln2_g, ln2_b, w_up, ffn_conv_w, ffn_conv_b, w_down, ln3_g, ln3_b, loss_target, m_w_in, m_rel_bias, m_swa_sinks, m_gdn_conv_w, m_gdn_a_log, m_gdn_dt_bias, m_gdn_norm_w, m_w_br_swa, m_w_br_gdn, m_w_mix_o, m_ln1_g, m_ln1_b, m_w_mem_q, m_w_mem_kv, m_w_mem_o, m_ln2_g, m_ln2_b, m_w_up, m_ffn_conv_w, m_ffn_conv_b, m_w_down, m_ln3_g, m_ln3_b, v_w_in, v_rel_bias, v_swa_sinks, v_gdn_conv_w, v_gdn_a_log, v_gdn_dt_bias, v_gdn_norm_w, v_w_br_swa, v_w_br_gdn, v_w_mix_o, v_ln1_g, v_ln1_b, v_w_mem_q, v_w_mem_kv, v_w_mem_o, v_ln2_g, v_ln2_b, v_w_up, v_ffn_conv_w, v_ffn_conv_b, v_w_down, v_ln3_g, v_ln3_b):
    env = dict(locals())
    w2 = {n: (env[n][0] if env[n].ndim == 3 else env[n]) for n in WEIGHTS}
    m2 = {n: (env["m_" + n][0] if env["m_" + n].ndim == 3 else env["m_" + n]) for n in WEIGHTS}
    v2 = {n: (env["v_" + n][0] if env["v_" + n].ndim == 3 else env["v_" + n]) for n in WEIGHTS}
    xs, mems, target = x[0], mem[0], loss_target[0]
    my_id = 4 * lax.axis_index("x") + 2 * lax.axis_index("y") + lax.axis_index("c")
    pad_ff = FF_PAD - FF_SHARD

    sent = ("w_in", "w_br_swa", "w_br_gdn", "w_mem_o", "w_mix_o", "w_down", "w_mem_q", "w_mem_kv", "w_up")
    pad_cols = {"w_in": W_IN_PAD - W_IN_SHARD, "w_up": pad_ff}
    got = dict(zip(sent, all_gather_shards(
        [jnp.pad(w2[n], ((0, 0), (0, pad_cols.get(n, 0)))).astype(bf16) for n in sent], name="gather_weights")))
    w_swa, w_gdn, w_gates = split_w_in(got["w_in"], name="split_w_in")
    w_mix_o = got["w_mix_o"].reshape(D_MODEL, D_MODEL)
    w_mem_q = got["w_mem_q"].reshape(D_MODEL, MEM_W)
    w_mem_kv = got["w_mem_kv"].reshape(D_MODEL, 2 * MEM_W)
    w_down_p = jnp.pad(got["w_down"].reshape(4, FF_SHARD, D_MODEL), ((0, 0), (0, pad_ff), (0, 0))).reshape(4 * FF_PAD, D_MODEL)
    n_ffn, n_gdn = 3 * FF_SHARD, GDN_CONV * (QKV_W // N_DEV)
    conv_mine = jnp.concatenate([w2["ffn_conv_w"].reshape(-1), w2["gdn_conv_w"].reshape(-1)])[None]
    conv_rows = lax.dynamic_update_slice(jnp.zeros((N_DEV, n_ffn + n_gdn), f32), conv_mine, (my_id, 0))
    conv_all = all_reduce_small(_pack([conv_rows], CONV_ROWS, f32), name="gather_conv_w")
    conv_all = conv_all.reshape(-1)[:N_DEV * (n_ffn + n_gdn)].reshape(N_DEV, n_ffn + n_gdn)
    cwb = jnp.concatenate([conv_all[:, :n_ffn].reshape(N_DEV, 3, FF_SHARD), w2["ffn_conv_b"].reshape(N_DEV, 1, FF_SHARD),
                           jnp.zeros((N_DEV, 4, FF_SHARD), f32)], axis=1)
    cwb = jnp.pad(cwb, ((0, 0), (0, 0), (0, pad_ff)))
    convw = jnp.transpose(conv_all[:, n_ffn:].reshape(N_DEV, GDN_CONV, QKV_W // N_DEV), (1, 0, 2)).reshape(GDN_CONV, QKV_W)
    convw = jnp.pad(convw, ((0, 4), (0, 0)))
    onehot = _bucket_onehot()
    bias = mm(w2["rel_bias"].T, onehot, "nn", hi=True, tn=4096, name="rel_bias_table").reshape(SWA_HEADS, BLOCK, 2 * BLOCK)
    alog_row = jnp.pad(w2["gdn_a_log"], ((0, 0), (GDN_HEADS, 128 - 2 * GDN_HEADS)))
    dt_row = jnp.pad(w2["gdn_dt_bias"], ((0, 0), (GDN_HEADS, 128 - 2 * GDN_HEADS)))

    xb = cast_bf16(xs, name="cast_x")
    memb = cast_bf16(mems, name="cast_mem")
    gates = mm(xb, w_gates, "nn", name="proj_gates")
    gdn_in = mm(xb, w_gdn, "nn", name="proj_gdn")
    swa_in = mm(xb, w_swa, "nn", tn=SWA_IN_W, name="proj_swa")
    attn = swa_fwd(swa_in, bias, w2["swa_sinks"], name="swa_fwd")
    qn, kn, vv = gdn_pre_fwd(gdn_in, convw, name="gdn_pre_fwd")
    gbeta = gbeta_fwd(swa_in, alog_row, dt_row, name="gbeta_fwd")
    o_gdn, states = gdn_scan_fwd(qn, kn, vv, gbeta, name="gdn_scan_fwd")
    ygd = gdn_post_fwd(o_gdn, gdn_in, w2["gdn_norm_w"], name="gdn_post_fwd")
    y_swa = mm(attn, got["w_br_swa"], "nn", b_blocked=True, name="br_swa")
    y_gdn = mm(ygd, got["w_br_gdn"], "nn", b_blocked=True, name="br_gdn")
    mixed = merge_fwd(gates, y_swa, y_gdn, name="merge_fwd")
    z1 = mm(mixed, w_mix_o, "nn", add=xs, add_scale=ALPHA, name="mix_o")
    x1, x1b = ln_fwd(z1, w2["ln1_g"], w2["ln1_b"], name="ln1_fwd")
    qm = mm(x1b, w_mem_q, "nn", name="mem_q")
    kv = mm(memb, w_mem_kv, "nn", name="mem_kv")
    om = memattn_fwd(qm, kv, name="memattn_fwd")
    z2 = mm(om, got["w_mem_o"], "nn", b_blocked=True, add=x1, add_scale=ALPHA, name="mem_o")
    x2, x2b = ln_fwd(z2, w2["ln2_g"], w2["ln2_b"], name="ln2_fwd")
    hpre = mm(x2b, got["w_up"], "nn", b_blocked=True, name="ffn_up")
    act = ffn_act_fwd(hpre, cwb, name="ffn_act_fwd")
    z3 = mm(act, w_down_p, "nn", add=x2, add_scale=ALPHA, tk=FF_PAD, name="ffn_down")
    dz3, dz3b, d_ln3g, d_ln3b, loss = ln_loss(z3, target, w2["ln3_g"], w2["ln3_b"], name="ln3_loss")

    dact = mm(dz3b, w_down_p, "nt", tn=FF_PAD, name="d_act")
    d_wdown_p = mm(act, dz3b, "tn", tm=FF_PAD, name="dw_down")
    d_hpre, d_cwb = ffn_act_bwd(hpre, dact, cwb, name="ffn_act_bwd")
    dx2 = mm(d_hpre, got["w_up"], "nt", b_blocked=True, add=dz3, add_scale=ALPHA, name="d_x2")
    d_wup = mm(x2b, d_hpre, "tn", out_blocked=True, name="dw_up")
    dz2, dz2b, d_ln2g, d_ln2b = ln_bwd(dx2, z2, w2["ln2_g"], name="ln2_bwd")
    d_om = mm(dz2b, got["w_mem_o"], "nt", b_blocked=True, name="d_om")
    d_wmemo = mm(om, dz2b, "tn", out_blocked=True, name="dw_mem_o")
    dqm, dkv = memattn_bwd(qm, kv, d_om, name="memattn_bwd")
    dx1 = mm(dqm, w_mem_q, "nt", add=dz2, add_scale=ALPHA, name="d_x1")
    d_wmemq = mm(x1b, dqm, "tn", name="dw_mem_q")
    d_wmemkv = mm(memb, dkv, "tn", name="dw_mem_kv")
    dz1, dz1b, d_ln1g, d_ln1b = ln_bwd(dx1, z1, w2["ln1_g"], name="ln1_bwd")
    dmix = mm(dz1b, w_mix_o, "nt", name="d_mixed")
    d_wmixo = mm(mixed, dz1b, "tn", name="dw_mix_o")
    dys, dyg, d_gates = merge_bwd(gates, y_swa, y_gdn, dmix, name="merge_bwd")
    d_attn = mm(dys, got["w_br_swa"], "nt", b_blocked=True, name="d_attn")
    d_wbrswa = mm(attn, dys, "tn", out_blocked=True, name="dw_br_swa")
    d_ygd = mm(dyg, got["w_br_gdn"], "nt", b_blocked=True, name="d_ygd")
    d_wbrgdn = mm(ygd, dyg, "tn", out_blocked=True, name="dw_br_gdn")
    d_o, d_gz, d_normw = gdn_post_bwd(o_gdn, gdn_in, w2["gdn_norm_w"], d_ygd, name="gdn_post_bwd")
    dqn, dkn, dvv, dgbeta = gdn_scan_bwd(qn, kn, vv, gbeta, states, d_o, name="gdn_scan_bwd")
    d_gdn_in, d_convw = gdn_pre_bwd(gdn_in, convw, dqn, dkn, dvv, d_gz, name="gdn_pre_bwd")
    d_ba, d_alog, d_dt = gbeta_bwd(swa_in, alog_row, dt_row, dgbeta, name="gbeta_bwd")
    dq, dkc, dkp, dvc, dvp, d_bias, d_sinks = swa_bwd(swa_in, bias, w2["swa_sinks"], d_attn, name="swa_bwd")
    d_swa_in = swa_in_grad(dq, dkc, dkp, dvc, dvp, d_ba, name="swa_in_grad")
    d_relbias = mm(d_bias.reshape(SWA_HEADS, -1), onehot, "nt", hi=True, tk=4096, name="d_rel_bias").T
    gx = mm(d_gates, w_gates, "nt", add=dz1, add_scale=ALPHA, name="dx_gates")
    gx = mm(d_gdn_in, w_gdn, "nt", add=gx, name="dx_gdn")
    gx = mm(d_swa_in, w_swa, "nt", add=gx, tk=SWA_IN_W, name="dx_swa")
    d_wgates = mm(xb, d_gates, "tn", name="dw_gates")
    d_wgdn = mm(xb, d_gdn_in, "tn", name="dw_gdn")
    d_wswa = mm(xb, d_swa_in, "tn", tn=SWA_IN_W, name="dw_swa")

    chunks = dict(zip(sent, (
        merge_w_in_grad(d_wswa, d_wgdn, d_wgates, name="merge_w_in_grad"), d_wbrswa, d_wbrgdn, d_wmemo,
        d_wmixo.reshape(N_DEV, D_MODEL // N_DEV, D_MODEL),
        d_wdown_p.reshape(4, FF_PAD, D_MODEL)[:, :FF_SHARD].reshape(N_DEV, FF_SHARD // 2, D_MODEL),
        d_wmemq.reshape(N_DEV, D_MODEL // N_DEV, MEM_W), d_wmemkv.reshape(N_DEV, D_MODEL // N_DEV, 2 * MEM_W), d_wup)))
    from_sibling = exchange_d2d([chunks[n] for n in sent], name="grad_exchange_d2d")
    chip_sums = [add_sibling(chunks[n], r, name="grad_add_sibling_" + n) for n, r in zip(sent, from_sibling)]
    from_chips = exchange_ici(chip_sums, name="grad_exchange_ici")
    grads = {n: add_four(r, name="grad_add_chips_" + n) for n, r in zip(sent, from_chips)}
    grads["w_in"] = grads["w_in"][:, :W_IN_SHARD]
    grads["w_up"] = grads["w_up"][:, :FF_SHARD]

    gsmall = {
        "rel_bias": d_relbias, "swa_sinks": d_sinks[:, :SWA_HEADS], "gdn_a_log": d_alog[:, GDN_HEADS:2 * GDN_HEADS],
        "gdn_dt_bias": d_dt[:, GDN_HEADS:2 * GDN_HEADS], "gdn_norm_w": d_normw, "ln1_g": d_ln1g, "ln1_b": d_ln1b,
        "ln2_g": d_ln2g, "ln2_b": d_ln2b, "ln3_g": d_ln3g, "ln3_b": d_ln3b,
        "ffn_conv_b": d_cwb[:, 3, :FF_SHARD].reshape(1, 2 * D_FF),
        "ffn_conv_w": jnp.transpose(d_cwb[:, :3, :FF_SHARD], (1, 0, 2)).reshape(3, 2 * D_FF),
        "gdn_conv_w": d_convw[:GDN_CONV],
    }
    small_shapes = [shp for _, shp in SMALL] + [(3, 2 * D_FF), (GDN_CONV, QKV_W)]
    small_names = [n for n, _ in SMALL] + ["ffn_conv_w", "gdn_conv_w"]
    small_sum = all_reduce_small(_pack([gsmall[n] for n in small_names], AR_ROWS, f32), name="all_reduce_small")
    grads.update(zip(small_names, _unpack(small_sum.reshape(-1), small_shapes)))
    grads["ffn_conv_w"] = lax.dynamic_slice_in_dim(grads["ffn_conv_w"], my_id * FF_SHARD, FF_SHARD, axis=1)
    grads["gdn_conv_w"] = lax.dynamic_slice_in_dim(grads["gdn_conv_w"], my_id * (QKV_W // N_DEV), QKV_W // N_DEV, axis=1)

    big = [n for n, shp, _ in SHARDED if shp[0] * shp[1] > 8192]
    tiny = [n for n in WEIGHTS if n not in big]
    delta, new_m, new_v = {}, {}, {}
    for n in big:
        delta[n], new_m[n], new_v[n] = adamw(w2[n], grads[n], m2[n], v2[n], name="adamw_" + n)
    tiny_shapes = [w2[n].shape for n in tiny]
    packed = [_pack([src[n] for n in tiny], SMALL_ROWS, f32) for src in (w2, grads, m2, v2)]
    for dst, res in zip((delta, new_m, new_v), adamw(*packed, name="adamw_small")):
        dst.update(zip(tiny, _unpack(res.reshape(-1), tiny_shapes)))

    def shaped(d):
        return [d[n].reshape(env[n].shape) for n in WEIGHTS]

    loss_all = lax.psum(loss[0, 0], ("x", "y", "c"))
    return (loss_all, gx[None], *shaped(grads), *shaped(delta), *shaped(new_m), *shaped(new_v))
```

```python
import functools
import math

import jax
import jax.numpy as jnp
from jax import lax
from jax.experimental import pallas as pl
from jax.experimental.pallas import tpu as pltpu

f32 = jnp.float32
bf16 = jnp.bfloat16
HI = lax.Precision.HIGHEST
MESH = pl.DeviceIdType.MESH

D_MODEL = 2048
N_DEV = 8
SWA_HEADS, SWA_KV_HEADS, SWA_HEAD_DIM, BLOCK = 16, 2, 64, 128
REL_BUCKETS, REL_MAX_DIST = 32, 128
GDN_HEADS, GDN_HEAD_DIM, GDN_CONV, GDN_CHUNK = 8, 128, 4, 64
MEM_HEADS, MEM_HEAD_DIM = 4, 128
D_FF = 5504
FF_SHARD = 2 * D_FF // N_DEV
FF_PAD = 1408
NORM_EPS = 1e-5
ALPHA = 2.0 ** 0.25
NEG_INF = -1e30
SWA_Q, SWA_KV, GDN_W, MEM_W = 1024, 128, 1024, 512
IN_DIM = 9488
HALO = 8

ADAM_LR, ADAM_B1, ADAM_B2, ADAM_EPS, ADAM_WD, ADAM_STEP = 0.001, 0.9, 0.999, 1e-08, 0.01, 10

PACK_COLS = 1024
SMALL_ROWS = 32
AR_ROWS = 72
CONV_ROWS = 48

SHARDED = (
    ("w_in", (2048, 1186), 1), ("w_br_swa", (1024, 256), 1), ("w_br_gdn", (1024, 256), 1),
    ("w_mix_o", (256, 2048), 0), ("w_mem_q", (256, 512), 0), ("w_mem_kv", (256, 1024), 0),
    ("w_mem_o", (512, 256), 1), ("w_up", (2048, 1376), 1), ("w_down", (688, 2048), 0),
    ("ffn_conv_w", (3, 1376), 1), ("gdn_conv_w", (4, 384), 1),
)
SMALL = (
    ("rel_bias", (32, 16)), ("swa_sinks", (1, 16)), ("gdn_a_log", (1, 8)), ("gdn_dt_bias", (1, 8)),
    ("gdn_norm_w", (1, 128)), ("ln1_g", (1, 2048)), ("ln1_b", (1, 2048)), ("ln2_g", (1, 2048)),
    ("ln2_b", (1, 2048)), ("ln3_g", (1, 2048)), ("ln3_b", (1, 2048)), ("ffn_conv_b", (1, 11008)),
)
WEIGHTS = ("w_in", "rel_bias", "swa_sinks", "gdn_conv_w", "gdn_a_log", "gdn_dt_bias", "gdn_norm_w", "w_br_swa",
           "w_br_gdn", "w_mix_o", "ln1_g", "ln1_b", "w_mem_q", "w_mem_kv", "w_mem_o", "ln2_g", "ln2_b", "w_up",
           "ffn_conv_w", "ffn_conv_b", "w_down", "ln3_g", "ln3_b")


def _tile(n, target, align):
    if n <= target:
        return n
    t = (target // align) * align
    while t >= align:
        if n % t == 0:
            return t
        t -= align
    return n


VMEM_LIMIT_BYTES = 56 * 1024 * 1024


def _params(*sem):
    return pltpu.CompilerParams(dimension_semantics=sem, vmem_limit_bytes=VMEM_LIMIT_BYTES)


def _sigmoid(v):
    return jax.nn.sigmoid(v)


def _d16(a, b, dims):
    return lax.dot_general(a.astype(bf16), b.astype(bf16), (dims, ((), ())), preferred_element_type=f32)


NN = ((1,), (0,))
NT = ((1,), (1,))
TN = ((0,), (0,))


def mm(a, b, mode, *, name, add=None, add_scale=1.0, out_dtype=f32, hi=False, tm=1024, tn=1024, tk=2048,
       b_blocked=False, out_blocked=False):
    if b_blocked:
        nb, rows, width = b.shape
        if mode == "nn":
            (m, k), n, tn = a.shape, nb * width, width
        else:
            (m, k), n, tk = a.shape, rows, width
    elif mode == "nn":
        (m, k), (_, n) = a.shape, b.shape
    elif mode == "nt":
        (m, k), (n, _) = a.shape, b.shape
    else:
        (k, m), (_, n) = a.shape, b.shape
    if out_blocked:
        tn = n // N_DEV
    tm, tn, tk = _tile(m, tm, 8 if mode != "tn" else 128), _tile(n, tn, 128), _tile(k, tk, 128 if mode != "tn" else 8)
    nk = k // tk
    dims = {"nn": NN, "nt": NT, "tn": TN}[mode]
    a_spec = pl.BlockSpec((tk, tm), lambda i, j, kk: (kk, i)) if mode == "tn" else pl.BlockSpec((tm, tk), lambda i, j, kk: (i, kk))
    if b_blocked:
        b_spec = (pl.BlockSpec((None, tk, tn), lambda i, j, kk: (j, kk, 0)) if mode == "nn"
                  else pl.BlockSpec((None, tn, tk), lambda i, j, kk: (kk, j, 0)))
    else:
        b_spec = pl.BlockSpec((tn, tk), lambda i, j, kk: (j, kk)) if mode == "nt" else pl.BlockSpec((tk, tn), lambda i, j, kk: (kk, j))
    if out_blocked:
        o_spec, o_shape = pl.BlockSpec((None, tm, tn), lambda i, j, kk: (j, i, 0)), (N_DEV, m, tn)
    else:
        o_spec, o_shape = pl.BlockSpec((tm, tn), lambda i, j, kk: (i, j)), (m, n)
    has_add = add is not None

    def product(a_ref, b_ref):
        if hi:
            return lax.dot_general(a_ref[...], b_ref[...], (dims, ((), ())), precision=HI, preferred_element_type=f32)
        return _d16(a_ref[...], b_ref[...], dims)

    def finish(r, add_ref, o_ref):
        if has_add:
            r = r + add_scale * add_ref[...]
        o_ref[...] = r.astype(out_dtype)

    def body_one_step(a_ref, b_ref, *rest):
        finish(product(a_ref, b_ref), rest[0] if has_add else None, rest[-1])

    def body_k_steps(a_ref, b_ref, *rest):
        o_ref, acc_ref = rest[-2:]
        kk = pl.program_id(2)

        @pl.when(kk == 0)
        def _():
            acc_ref[...] = jnp.zeros_like(acc_ref)

        acc_ref[...] += product(a_ref, b_ref)

        @pl.when(kk == nk - 1)
        def _():
            finish(acc_ref[...], rest[0] if has_add else None, o_ref)

    return pl.pallas_call(
        body_one_step if nk == 1 else body_k_steps, name=name, grid=(m // tm, n // tn, nk),
        in_specs=[a_spec, b_spec] + ([o_spec] if has_add else []), out_specs=o_spec,
        out_shape=jax.ShapeDtypeStruct(o_shape, out_dtype),
        scratch_shapes=[] if nk == 1 else [pltpu.VMEM((tm, tn), f32)],
        compiler_params=_params("parallel", "parallel", "arbitrary"),
    )(*((a, b, add) if has_add else (a, b)))


def cast_bf16(a, *, name):
    m, n = a.shape
    tm = _tile(m, 512, 16)

    def body(a_ref, o_ref):
        o_ref[...] = a_ref[...].astype(bf16)

    return pl.pallas_call(body, name=name, grid=(m // tm,), in_specs=[pl.BlockSpec((tm, n), lambda i: (i, 0))],
                          out_specs=pl.BlockSpec((tm, n), lambda i: (i, 0)), out_shape=jax.ShapeDtypeStruct((m, n), bf16),
                          compiler_params=_params("parallel"))(a)


def _ln_stats(z):
    mu = jnp.mean(z, axis=-1, keepdims=True)
    zc = z - mu
    var = jnp.mean(zc * zc, axis=-1, keepdims=True)
    rstd = lax.rsqrt(var + NORM_EPS)
    return zc * rstd, rstd


def ln_fwd(z, g, b, *, name):
    s, d = z.shape
    tm = _tile(s, 256, 16)

    def body(z_ref, g_ref, b_ref, y_ref, yb_ref):
        xhat, _ = _ln_stats(z_ref[...])
        y = xhat * g_ref[...] + b_ref[...]
        y_ref[...] = y
        yb_ref[...] = y.astype(bf16)

    row = pl.BlockSpec((tm, d), lambda i: (i, 0))
    vec = pl.BlockSpec((1, d), lambda i: (0, 0))
    return pl.pallas_call(body, name=name, grid=(s // tm,), in_specs=[row, vec, vec], out_specs=[row, row],
                          out_shape=[jax.ShapeDtypeStruct((s, d), f32), jax.ShapeDtypeStruct((s, d), bf16)],
                          compiler_params=_params("parallel"))(z, g, b)


def _ln_bwd_tile(dy, z, g):
    xhat, rstd = _ln_stats(z)
    dxh = dy * g
    m1 = jnp.mean(dxh, axis=-1, keepdims=True)
    m2 = jnp.mean(dxh * xhat, axis=-1, keepdims=True)
    dz = rstd * (dxh - m1 - xhat * m2)
    return dz, jnp.sum(dy * xhat, axis=0, keepdims=True), jnp.sum(dy, axis=0, keepdims=True)


def ln_bwd(dy, z, g, *, name):
    s, d = z.shape
    tm = _tile(s, 256, 16)

    def body(dy_ref, z_ref, g_ref, dz_ref, dzb_ref, dg_ref, db_ref):
        @pl.when(pl.program_id(0) == 0)
        def _():
            dg_ref[...] = jnp.zeros_like(dg_ref)
            db_ref[...] = jnp.zeros_like(db_ref)

        dz, dg, db = _ln_bwd_tile(dy_ref[...], z_ref[...], g_ref[...])
        dz_ref[...] = dz
        dzb_ref[...] = dz.astype(bf16)
        dg_ref[...] += dg
        db_ref[...] += db

    row = pl.BlockSpec((tm, d), lambda i: (i, 0))
    vec = pl.BlockSpec((1, d), lambda i: (0, 0))
    return pl.pallas_call(body, name=name, grid=(s // tm,), in_specs=[row, row, vec], out_specs=[row, row, vec, vec],
                          out_shape=[jax.ShapeDtypeStruct((s, d), f32), jax.ShapeDtypeStruct((s, d), bf16),
                                     jax.ShapeDtypeStruct((1, d), f32), jax.ShapeDtypeStruct((1, d), f32)],
                          compiler_params=_params("arbitrary"))(dy, z, g)


def ln_loss(z, target, g, b, *, name):
    s, d = z.shape
    tm = _tile(s, 256, 16)
    nt = s // tm

    def body(z_ref, t_ref, g_ref, b_ref, dz_ref, dzb_ref, dg_ref, db_ref, loss_ref, lacc_ref):
        i = pl.program_id(0)

        @pl.when(i == 0)
        def _():
            dg_ref[...] = jnp.zeros_like(dg_ref)
            db_ref[...] = jnp.zeros_like(db_ref)
            lacc_ref[...] = jnp.zeros_like(lacc_ref)

        zv, gv = z_ref[...], g_ref[...]
        xhat, _ = _ln_stats(zv)
        err = xhat * gv + b_ref[...] - t_ref[...]
        lacc_ref[...] += jnp.sum(err * err, axis=0, keepdims=True)
        dz, dg, db = _ln_bwd_tile(err * (1.0 / d), zv, gv)
        dz_ref[...] = dz
        dzb_ref[...] = dz.astype(bf16)
        dg_ref[...] += dg
        db_ref[...] += db

        @pl.when(i == nt - 1)
        def _():
            loss_ref[...] = (0.5 / d) * jnp.sum(lacc_ref[...], axis=1, keepdims=True)

    row = pl.BlockSpec((tm, d), lambda i: (i, 0))
    vec = pl.BlockSpec((1, d), lambda i: (0, 0))
    return pl.pallas_call(body, name=name, grid=(nt,), in_specs=[row, row, vec, vec],
                          out_specs=[row, row, vec, vec, pl.BlockSpec((1, 1), lambda i: (0, 0))],
                          out_shape=[jax.ShapeDtypeStruct((s, d), f32), jax.ShapeDtypeStruct((s, d), bf16),
                                     jax.ShapeDtypeStruct((1, d), f32), jax.ShapeDtypeStruct((1, d), f32),
                                     jax.ShapeDtypeStruct((1, 1), f32)],
                          scratch_shapes=[pltpu.VMEM((1, d), f32)],
                          compiler_params=_params("arbitrary"))(z, target, g, b)


def merge_fwd(gates, ys, yg, *, name):
    s, d = ys.shape
    tm = _tile(s, 256, 16)

    def body(gt_ref, ys_ref, yg_ref, o_ref):
        o_ref[...] = (_sigmoid(gt_ref[:, :d]) * ys_ref[...] + _sigmoid(gt_ref[:, d:]) * yg_ref[...]).astype(bf16)

    row = pl.BlockSpec((tm, d), lambda i: (i, 0))
    return pl.pallas_call(body, name=name, grid=(s // tm,), in_specs=[pl.BlockSpec((tm, 2 * d), lambda i: (i, 0)), row, row],
                          out_specs=row, out_shape=jax.ShapeDtypeStruct((s, d), bf16),
                          compiler_params=_params("parallel"))(gates, ys, yg)


def merge_bwd(gates, ys, yg, dmix, *, name):
    s, d = ys.shape
    tm = _tile(s, 256, 16)

    def body(gt_ref, ys_ref, yg_ref, dm_ref, dys_ref, dyg_ref, dgt_ref):
        dm = dm_ref[...]
        sa, sb = _sigmoid(gt_ref[:, :d]), _sigmoid(gt_ref[:, d:])
        dys_ref[...] = (dm * sa).astype(bf16)
        dyg_ref[...] = (dm * sb).astype(bf16)
        dgt_ref[:, :d] = (dm * ys_ref[...] * sa * (1.0 - sa)).astype(bf16)
        dgt_ref[:, d:] = (dm * yg_ref[...] * sb * (1.0 - sb)).astype(bf16)

    row = pl.BlockSpec((tm, d), lambda i: (i, 0))
    wide = pl.BlockSpec((tm, 2 * d), lambda i: (i, 0))
    return pl.pallas_call(body, name=name, grid=(s // tm,), in_specs=[wide, row, row, row], out_specs=[row, row, wide],
                          out_shape=[jax.ShapeDtypeStruct((s, d), bf16), jax.ShapeDtypeStruct((s, d), bf16),
                                     jax.ShapeDtypeStruct((s, 2 * d), bf16)],
                          compiler_params=_params("parallel"))(gates, ys, yg, dmix)


def _shift_down(ext, j):
    return ext if j == 0 else pltpu.roll(ext, j, 0)


def _shift_up(ext, j):
    return ext if j == 0 else pltpu.roll(ext, ext.shape[0] - j, 0)


def _causal_conv(ext, w_ref, width):
    acc = None
    for j in range(width):
        term = w_ref[j:j + 1, :] * _shift_down(ext, width - 1 - j)
        acc = term if acc is None else acc + term
    return acc[HALO:]


def _conv_input_grad(dy_ext, w_ref, width, rows):
    acc = None
    for j in range(width):
        term = w_ref[j:j + 1, :] * _shift_up(dy_ext, width - 1 - j)
        acc = term if acc is None else acc + term
    return acc[:rows]


def _conv_weight_grad(dy, ext, width, rows):
    return [jnp.sum(dy * _shift_down(ext, width - 1 - j)[HALO:HALO + rows], axis=0, keepdims=True) for j in range(width)]


def _rows_to_block(rows, n_rows, cols):
    r = lax.broadcasted_iota(jnp.int32, (n_rows, cols), 0)
    out = jnp.zeros((n_rows, cols), f32)
    for j, v in enumerate(rows):
        out = out + jnp.where(r == j, v, 0.0)
    return out


def _silu_and_grad(v):
    sg = _sigmoid(v)
    return v * sg, sg * (1.0 + v * (1.0 - sg))


def ffn_act_fwd(hpre, cwb, *, name):
    s = hpre.shape[0]
    tm = _tile(s, 256, 16)
    hb = tm // HALO

    def body(hg_ref, hgp_ref, hu_ref, hup_ref, cg_ref, cu_ref, o_ref):
        first = pl.program_id(1) == 0

        def conv(h_ref, hp_ref, c_ref):
            prev = jnp.where(first, 0.0, hp_ref[...])
            return _causal_conv(jnp.concatenate([prev, h_ref[...]], axis=0), c_ref.at[0], 3) + c_ref[0, 3:4, :]

        g = conv(hg_ref, hgp_ref, cg_ref)
        u = conv(hu_ref, hup_ref, cu_ref)
        o_ref[...] = (g * _sigmoid(g) * u).astype(bf16)

    def tile(off):
        return pl.BlockSpec((tm, FF_PAD), lambda j, i: (i, j + off))

    def halo(off):
        return pl.BlockSpec((HALO, FF_PAD), lambda j, i: (jnp.maximum(i * hb - 1, 0), j + off))

    def taps(off):
        return pl.BlockSpec((1, 8, FF_PAD), lambda j, i: (j + off, 0, 0))

    return pl.pallas_call(body, name=name, grid=(4, s // tm),
                          in_specs=[tile(0), halo(0), tile(4), halo(4), taps(0), taps(4)],
                          out_specs=pl.BlockSpec((tm, FF_PAD), lambda j, i: (i, j)),
                          out_shape=jax.ShapeDtypeStruct((s, 4 * FF_PAD), bf16),
                          compiler_params=_params("parallel", "parallel"))(hpre, hpre, hpre, hpre, cwb, cwb)


def ffn_act_bwd(hpre, dact, cwb, *, name):
    s = hpre.shape[0]
    tm = _tile(s, 256, 16)
    hb = tm // HALO
    nt = s // tm
    last_hb = s // HALO - 1

    def body(hg_ref, hgp_ref, hgn_ref, hu_ref, hup_ref, hun_ref, d_ref, dn_ref, cg_ref, cu_ref,
             dhg_ref, dhu_ref, dcg_ref, dcu_ref):
        i = pl.program_id(1)
        first, last = i == 0, i == nt - 1

        @pl.when(first)
        def _():
            dcg_ref[...] = jnp.zeros_like(dcg_ref)
            dcu_ref[...] = jnp.zeros_like(dcu_ref)

        def ext_of(h_ref, hp_ref, hn_ref):
            return jnp.concatenate([jnp.where(first, 0.0, hp_ref[...]), h_ref[...], hn_ref[...]], axis=0)

        eg, eu = ext_of(hg_ref, hgp_ref, hgn_ref), ext_of(hu_ref, hup_ref, hun_ref)
        g = _causal_conv(eg, cg_ref.at[0], 3) + cg_ref[0, 3:4, :]
        u = _causal_conv(eu, cu_ref.at[0], 3) + cu_ref[0, 3:4, :]
        d = jnp.concatenate([d_ref[...], jnp.where(last, 0.0, dn_ref[...])], axis=0)
        act, dact_dg = _silu_and_grad(g)
        dg = d * u * dact_dg
        du = d * act
        dhg_ref[...] = _conv_input_grad(dg, cg_ref.at[0], 3, tm).astype(bf16)
        dhu_ref[...] = _conv_input_grad(du, cu_ref.at[0], 3, tm).astype(bf16)
        dgt, dut = dg[:tm], du[:tm]
        dcg_ref[0] += _rows_to_block(_conv_weight_grad(dgt, eg, 3, tm) + [jnp.sum(dgt, axis=0, keepdims=True)], 8, FF_PAD)
        dcu_ref[0] += _rows_to_block(_conv_weight_grad(dut, eu, 3, tm) + [jnp.sum(dut, axis=0, keepdims=True)], 8, FF_PAD)

    def tile(off):
        return pl.BlockSpec((tm, FF_PAD), lambda j, i: (i, j + off))

    def prev(off):
        return pl.BlockSpec((HALO, FF_PAD), lambda j, i: (jnp.maximum(i * hb - 1, 0), j + off))

    def nxt(off):
        return pl.BlockSpec((HALO, FF_PAD), lambda j, i: (jnp.minimum((i + 1) * hb, last_hb), j + off))

    def taps(off):
        return pl.BlockSpec((1, 8, FF_PAD), lambda j, i: (j + off, 0, 0))

    dhg, dhu, dcg, dcu = pl.pallas_call(
        body, name=name, grid=(4, nt),
        in_specs=[tile(0), prev(0), nxt(0), tile(4), prev(4), nxt(4), tile(0), nxt(0), taps(0), taps(4)],
        out_specs=[tile(0), tile(0), taps(0), taps(0)],
        out_shape=[jax.ShapeDtypeStruct((s, 4 * FF_PAD), bf16), jax.ShapeDtypeStruct((s, 4 * FF_PAD), bf16),
                   jax.ShapeDtypeStruct((4, 8, FF_PAD), f32), jax.ShapeDtypeStruct((4, 8, FF_PAD), f32)],
        compiler_params=_params("parallel", "arbitrary"),
    )(hpre, hpre, hpre, hpre, hpre, hpre, dact, dact, cwb, cwb)
    return jnp.concatenate([dhg, dhu], axis=1), jnp.concatenate([dcg, dcu], axis=0)


MEM_SCALE = MEM_HEAD_DIM ** -0.5


def _softmax_rows(sc):
    m = jnp.max(sc, axis=-1, keepdims=True)
    e = jnp.exp(sc - m)
    return e / jnp.sum(e, axis=-1, keepdims=True)


def memattn_fwd(qm, kv, *, name):
    s = qm.shape[0]
    mlen = kv.shape[0]
    tm = _tile(s, 512, 16)

    def body(q_ref, kv_ref, o_ref):
        for h in range(MEM_HEADS):
            lo = h * MEM_HEAD_DIM
            q = q_ref[:, lo:lo + MEM_HEAD_DIM]
            k = kv_ref[:, lo:lo + MEM_HEAD_DIM]
            v = kv_ref[:, MEM_W + lo:MEM_W + lo + MEM_HEAD_DIM]
            p = _softmax_rows(_d16(q, k, NT) * MEM_SCALE)
            o_ref[:, lo:lo + MEM_HEAD_DIM] = _d16(p, v, NN).astype(bf16)

    return pl.pallas_call(body, name=name, grid=(s // tm,),
                          in_specs=[pl.BlockSpec((tm, MEM_W), lambda i: (i, 0)), pl.BlockSpec((mlen, 2 * MEM_W), lambda i: (0, 0))],
                          out_specs=pl.BlockSpec((tm, MEM_W), lambda i: (i, 0)),
                          out_shape=jax.ShapeDtypeStruct((s, MEM_W), bf16), compiler_params=_params("parallel"))(qm, kv)


def memattn_bwd(qm, kv, dout, *, name):
    s = qm.shape[0]
    mlen = kv.shape[0]
    tm = _tile(s, 512, 16)

    def body(q_ref, kv_ref, do_ref, dq_ref, dkv_ref):
        @pl.when(pl.program_id(0) == 0)
        def _():
            dkv_ref[...] = jnp.zeros_like(dkv_ref)

        for h in range(MEM_HEADS):
            lo = h * MEM_HEAD_DIM
            q = q_ref[:, lo:lo + MEM_HEAD_DIM]
            k = kv_ref[:, lo:lo + MEM_HEAD_DIM]
            v = kv_ref[:, MEM_W + lo:MEM_W + lo + MEM_HEAD_DIM]
            do = do_ref[:, lo:lo + MEM_HEAD_DIM]
            p = _softmax_rows(_d16(q, k, NT) * MEM_SCALE)
            dp = _d16(do, v, NT)
            ds = p * (dp - jnp.sum(p * dp, axis=-1, keepdims=True)) * MEM_SCALE
            dq_ref[:, lo:lo + MEM_HEAD_DIM] = _d16(ds, k, NN).astype(bf16)
            dkv_ref[:, lo:lo + MEM_HEAD_DIM] += _d16(ds, q, TN)
            dkv_ref[:, MEM_W + lo:MEM_W + lo + MEM_HEAD_DIM] += _d16(p, do, TN)

    row = pl.BlockSpec((tm, MEM_W), lambda i: (i, 0))
    full = pl.BlockSpec((mlen, 2 * MEM_W), lambda i: (0, 0))
    return pl.pallas_call(body, name=name, grid=(s // tm,), in_specs=[row, full, row], out_specs=[row, full],
                          out_shape=[jax.ShapeDtypeStruct((s, MEM_W), bf16), jax.ShapeDtypeStruct((mlen, 2 * MEM_W), f32)],
                          compiler_params=_params("arbitrary"))(qm, kv, dout)


SWA_SCALE = SWA_HEAD_DIM ** -0.5
SWA_GROUP = SWA_HEADS // SWA_KV_HEADS
SWA_IN_W = 1408
K_COL, V_COL, BA_COL = SWA_Q // 128, SWA_Q // 128 + 1, SWA_Q // 128 + 2


def _swa_mask(n):
    qi = lax.broadcasted_iota(jnp.int32, (BLOCK, 2 * BLOCK), 0)
    kj = lax.broadcasted_iota(jnp.int32, (BLOCK, 2 * BLOCK), 1)
    dist = qi + BLOCK - kj
    return (dist >= 0) & (dist < BLOCK) & ((n > 0) | (kj >= BLOCK))


def _swa_group_probs(q, k, bias, sink, mask):
    heads = range(len(q))
    sc = [jnp.where(mask, _d16(q[h], k, NT) * SWA_SCALE + bias[h], NEG_INF) for h in heads]
    m = [jnp.maximum(jnp.max(sc[h], axis=-1, keepdims=True), sink[h]) for h in heads]
    e = [jnp.exp(sc[h] - m[h]) for h in heads]
    es = [jnp.exp(sink[h] - m[h]) for h in heads]
    inv = [1.0 / (jnp.sum(e[h], axis=-1, keepdims=True) + es[h]) for h in heads]
    return e, es, inv


def _swa_group_inputs(g, q_ref, bias_ref, sink_ref):
    hs = range(g * SWA_GROUP, (g + 1) * SWA_GROUP)
    return ([q_ref[:, h * SWA_HEAD_DIM:(h + 1) * SWA_HEAD_DIM] for h in hs], [bias_ref[h] for h in hs],
            [sink_ref[:, h:h + 1] for h in hs])


def _swa_specs():
    q_spec = pl.BlockSpec((BLOCK, SWA_Q), lambda n: (n, 0))

    def band(col):
        return [pl.BlockSpec((BLOCK, SWA_KV), lambda n: (jnp.maximum(n - 1, 0), col)),
                pl.BlockSpec((BLOCK, SWA_KV), lambda n: (n, col))]

    bias_spec = pl.BlockSpec((SWA_HEADS, BLOCK, 2 * BLOCK), lambda n: (0, 0, 0))
    sink_spec = pl.BlockSpec((1, SWA_HEADS), lambda n: (0, 0))
    return [q_spec] + band(K_COL) + band(V_COL) + [bias_spec, sink_spec]


def swa_fwd(swa_in, bias, sinks, *, name):
    s = swa_in.shape[0]

    def body(q_ref, kp_ref, kc_ref, vp_ref, vc_ref, bias_ref, sink_ref, o_ref):
        mask = _swa_mask(pl.program_id(0))
        kb = jnp.concatenate([kp_ref[...], kc_ref[...]], axis=0)
        vb = jnp.concatenate([vp_ref[...], vc_ref[...]], axis=0)
        for g in range(SWA_KV_HEADS):
            kl = g * SWA_HEAD_DIM
            q, bias_g, sink_g = _swa_group_inputs(g, q_ref, bias_ref, sink_ref)
            e, _, inv = _swa_group_probs(q, kb[:, kl:kl + SWA_HEAD_DIM], bias_g, sink_g, mask)
            v = vb[:, kl:kl + SWA_HEAD_DIM]
            outs = [_d16(e[h] * inv[h], v, NN) for h in range(SWA_GROUP)]
            for h in range(SWA_GROUP):
                lo = (g * SWA_GROUP + h) * SWA_HEAD_DIM
                o_ref[:, lo:lo + SWA_HEAD_DIM] = outs[h].astype(bf16)

    return pl.pallas_call(body, name=name, grid=(s // BLOCK,), in_specs=_swa_specs(),
                          out_specs=pl.BlockSpec((BLOCK, SWA_Q), lambda n: (n, 0)),
                          out_shape=jax.ShapeDtypeStruct((s, SWA_Q), bf16),
                          compiler_params=_params("parallel"))(swa_in, swa_in, swa_in, swa_in, swa_in, bias, sinks)


def swa_bwd(swa_in, bias, sinks, dout, *, name):
    s = swa_in.shape[0]

    def body(q_ref, kp_ref, kc_ref, vp_ref, vc_ref, bias_ref, sink_ref, do_ref,
             dq_ref, dkc_ref, dkp_ref, dvc_ref, dvp_ref, dbias_ref, dsink_ref):
        n = pl.program_id(0)

        @pl.when(n == 0)
        def _():
            dbias_ref[...] = jnp.zeros_like(dbias_ref)
            dsink_ref[...] = jnp.zeros_like(dsink_ref)

        mask = _swa_mask(n)
        kb = jnp.concatenate([kp_ref[...], kc_ref[...]], axis=0)
        vb = jnp.concatenate([vp_ref[...], vc_ref[...]], axis=0)
        lane = lax.broadcasted_iota(jnp.int32, (1, 128), 1)
        dsink = jnp.zeros((1, 128), f32)
        for g in range(SWA_KV_HEADS):
            kl = g * SWA_HEAD_DIM
            k, v = kb[:, kl:kl + SWA_HEAD_DIM], vb[:, kl:kl + SWA_HEAD_DIM]
            hs = range(SWA_GROUP)
            q, bias_g, sink_g = _swa_group_inputs(g, q_ref, bias_ref, sink_ref)
            do = [do_ref[:, (g * SWA_GROUP + h) * SWA_HEAD_DIM:(g * SWA_GROUP + h + 1) * SWA_HEAD_DIM] for h in hs]
            e, es, inv = _swa_group_probs(q, k, bias_g, sink_g, mask)
            p = [e[h] * inv[h] for h in hs]
            dp = [_d16(do[h], v, NT) for h in hs]
            delta = [jnp.sum(p[h] * dp[h], axis=-1, keepdims=True) for h in hs]
            ds = [p[h] * (dp[h] - delta[h]) for h in hs]
            dss = [ds[h] * SWA_SCALE for h in hs]
            dq = [_d16(dss[h], k, NN) for h in hs]
            dks = [_d16(dss[h], q[h], TN) for h in hs]
            dvs = [_d16(p[h], do[h], TN) for h in hs]
            dk, dv = sum(dks[1:], dks[0]), sum(dvs[1:], dvs[0])
            for h in hs:
                hh = g * SWA_GROUP + h
                dbias_ref[hh] += ds[h]
                dq_ref[:, hh * SWA_HEAD_DIM:(hh + 1) * SWA_HEAD_DIM] = dq[h]
                dsink = dsink + jnp.where(lane == hh, -jnp.sum(es[h] * inv[h] * delta[h], axis=0, keepdims=True), 0.0)
            dkp_ref[:, kl:kl + SWA_HEAD_DIM] = dk[:BLOCK]
            dkc_ref[:, kl:kl + SWA_HEAD_DIM] = dk[BLOCK:]
            dvp_ref[:, kl:kl + SWA_HEAD_DIM] = dv[:BLOCK]
            dvc_ref[:, kl:kl + SWA_HEAD_DIM] = dv[BLOCK:]
        dsink_ref[...] += dsink

    qs = pl.BlockSpec((BLOCK, SWA_Q), lambda n: (n, 0))
    ks = pl.BlockSpec((BLOCK, SWA_KV), lambda n: (n, 0))
    return pl.pallas_call(
        body, name=name, grid=(s // BLOCK,), in_specs=_swa_specs() + [qs],
        out_specs=[qs, ks, ks, ks, ks, pl.BlockSpec((SWA_HEADS, BLOCK, 2 * BLOCK), lambda n: (0, 0, 0)),
                   pl.BlockSpec((1, 128), lambda n: (0, 0))],
        out_shape=[jax.ShapeDtypeStruct((s, SWA_Q), f32)] + [jax.ShapeDtypeStruct((s, SWA_KV), f32)] * 4
        + [jax.ShapeDtypeStruct((SWA_HEADS, BLOCK, 2 * BLOCK), f32), jax.ShapeDtypeStruct((1, 128), f32)],
        compiler_params=_params("arbitrary"),
    )(swa_in, swa_in, swa_in, swa_in, swa_in, bias, sinks, dout)


def swa_in_grad(dq, dkc, dkp, dvc, dvp, dba, *, name):
    s = dq.shape[0]
    nb = s // BLOCK

    def body(dq_ref, dkc_ref, dkp_ref, dvc_ref, dvp_ref, dba_ref, o_ref):
        has_next = pl.program_id(0) < nb - 1
        o_ref[:, :SWA_Q] = dq_ref[...].astype(bf16)
        o_ref[:, SWA_Q:SWA_Q + SWA_KV] = (dkc_ref[...] + jnp.where(has_next, dkp_ref[...], 0.0)).astype(bf16)
        o_ref[:, SWA_Q + SWA_KV:SWA_Q + 2 * SWA_KV] = (dvc_ref[...] + jnp.where(has_next, dvp_ref[...], 0.0)).astype(bf16)
        o_ref[:, SWA_Q + 2 * SWA_KV:] = dba_ref[...].astype(bf16)

    cur = pl.BlockSpec((BLOCK, SWA_KV), lambda n: (n, 0))
    nxt = pl.BlockSpec((BLOCK, SWA_KV), lambda n: (jnp.minimum(n + 1, nb - 1), 0))
    return pl.pallas_call(body, name=name, grid=(nb,),
                          in_specs=[pl.BlockSpec((BLOCK, SWA_Q), lambda n: (n, 0)), cur, nxt, cur, nxt, cur],
                          out_specs=pl.BlockSpec((BLOCK, SWA_IN_W), lambda n: (n, 0)),
                          out_shape=jax.ShapeDtypeStruct((s, SWA_IN_W), bf16),
                          compiler_params=_params("parallel"))(dq, dkc, dkp, dvc, dvp, dba)


def _bucket_onehot():
    qi = jnp.arange(BLOCK)[:, None]
    kj = jnp.arange(2 * BLOCK)[None, :]
    dist = jnp.maximum(qi + BLOCK - kj, 0)
    max_exact = REL_BUCKETS // 2
    dd = jnp.maximum(dist, 1).astype(f32)
    large = max_exact + (jnp.log(dd / max_exact) / math.log(REL_MAX_DIST / max_exact) * (REL_BUCKETS - max_exact)).astype(jnp.int32)
    bucket = jnp.where(dist < max_exact, dist, jnp.minimum(large, REL_BUCKETS - 1)).reshape(-1)
    return (bucket[None, :] == jnp.arange(REL_BUCKETS)[:, None]).astype(f32)


def _gbeta_fn(ba, alog_row, dt_row):
    col = lax.broadcasted_iota(jnp.int32, ba.shape, 1)
    v = ba + dt_row
    softplus = jnp.maximum(v, 0.0) + jnp.log(1.0 + jnp.exp(-jnp.abs(v)))
    g = -jnp.exp(alog_row) * softplus
    return jnp.where(col < GDN_HEADS, _sigmoid(ba), jnp.where(col < 2 * GDN_HEADS, g, 0.0))


def gbeta_fwd(swa_in, alog_row, dt_row, *, name):
    s = swa_in.shape[0]
    tm = _tile(s, 512, 8)

    def body(ba_ref, a_ref, d_ref, o_ref):
        o_ref[...] = _gbeta_fn(ba_ref[...], a_ref[...], d_ref[...])

    vec = pl.BlockSpec((1, 128), lambda i: (0, 0))
    return pl.pallas_call(body, name=name, grid=(s // tm,), in_specs=[pl.BlockSpec((tm, 128), lambda i: (i, BA_COL)), vec, vec],
                          out_specs=pl.BlockSpec((tm, 128), lambda i: (i, 0)), out_shape=jax.ShapeDtypeStruct((s, 128), f32),
                          compiler_params=_params("parallel"))(swa_in, alog_row, dt_row)


def gbeta_bwd(swa_in, alog_row, dt_row, dgbeta, *, name):
    s = swa_in.shape[0]
    tm = _tile(s, 512, 8)

    def body(ba_ref, a_ref, d_ref, dgb_ref, dba_ref, da_ref, dd_ref):
        @pl.when(pl.program_id(0) == 0)
        def _():
            da_ref[...] = jnp.zeros_like(da_ref)
            dd_ref[...] = jnp.zeros_like(dd_ref)

        _, pull = jax.vjp(_gbeta_fn, ba_ref[...], a_ref[...], d_ref[...])
        dba, da, dd = pull(dgb_ref[...])
        dba_ref[...] = dba
        da_ref[...] += da
        dd_ref[...] += dd

    vec = pl.BlockSpec((1, 128), lambda i: (0, 0))
    row = pl.BlockSpec((tm, 128), lambda i: (i, 0))
    return pl.pallas_call(body, name=name, grid=(s // tm,),
                          in_specs=[pl.BlockSpec((tm, 128), lambda i: (i, BA_COL)), vec, vec, row], out_specs=[row, vec, vec],
                          out_shape=[jax.ShapeDtypeStruct((s, 128), f32), jax.ShapeDtypeStruct((1, 128), f32),
                                     jax.ShapeDtypeStruct((1, 128), f32)],
                          compiler_params=_params("arbitrary"))(swa_in, alog_row, dt_row, dgbeta)


QKV_W = 3 * GDN_W


def gdn_pre_fwd(gdn_in, convw, *, name):
    s = gdn_in.shape[0]
    tm = _tile(s, 256, 16)
    hb = tm // HALO

    def body(x_ref, xp_ref, w_ref, q_ref, k_ref, v_ref):
        prev = jnp.where(pl.program_id(0) == 0, 0.0, xp_ref[...])
        pre = _causal_conv(jnp.concatenate([prev, x_ref[...]], axis=0), w_ref, GDN_CONV)
        act = pre * _sigmoid(pre)
        for h in range(GDN_HEADS):
            lo = h * GDN_HEAD_DIM
            for off, o_ref in ((0, q_ref), (GDN_W, k_ref)):
                seg = act[:, off + lo:off + lo + GDN_HEAD_DIM]
                o_ref[:, lo:lo + GDN_HEAD_DIM] = seg * lax.rsqrt(jnp.sum(seg * seg, axis=-1, keepdims=True) + 1e-6)
        v_ref[...] = act[:, 2 * GDN_W:]

    out = pl.BlockSpec((tm, GDN_W), lambda i: (i, 0))
    return pl.pallas_call(body, name=name, grid=(s // tm,),
                          in_specs=[pl.BlockSpec((tm, QKV_W), lambda i: (i, 0)),
                                    pl.BlockSpec((HALO, QKV_W), lambda i: (jnp.maximum(i * hb - 1, 0), 0)),
                                    pl.BlockSpec((8, QKV_W), lambda i: (0, 0))],
                          out_specs=[out, out, out], out_shape=[jax.ShapeDtypeStruct((s, GDN_W), f32)] * 3,
                          compiler_params=_params("parallel"))(gdn_in, gdn_in, convw)


def gdn_pre_bwd(gdn_in, convw, dqn, dkn, dv, dgz, *, name):
    s = gdn_in.shape[0]
    tm = _tile(s, 128, 16)
    hb = tm // HALO
    nt = s // tm
    last_hb = s // HALO - 1

    def body(x_ref, xp_ref, xn_ref, w_ref, dq_ref, dqx_ref, dk_ref, dkx_ref, dv_ref, dvx_ref, dz_ref, o_ref, dw_ref):
        i = pl.program_id(0)
        first, last = i == 0, i == nt - 1

        @pl.when(first)
        def _():
            dw_ref[...] = jnp.zeros_like(dw_ref)

        ext = jnp.concatenate([jnp.where(first, 0.0, xp_ref[...]), x_ref[...], xn_ref[...]], axis=0)
        pre = _causal_conv(ext, w_ref, GDN_CONV)
        act, dact_dpre = _silu_and_grad(pre)

        def with_future(t_ref, n_ref):
            return jnp.concatenate([t_ref[...], jnp.where(last, 0.0, n_ref[...])], axis=0)

        dqe, dke, dve = with_future(dq_ref, dqx_ref), with_future(dk_ref, dkx_ref), with_future(dv_ref, dvx_ref)
        parts = []
        for off, dn in ((0, dqe), (GDN_W, dke)):
            for h in range(GDN_HEADS):
                lo = h * GDN_HEAD_DIM
                seg = act[:, off + lo:off + lo + GDN_HEAD_DIM]
                r = lax.rsqrt(jnp.sum(seg * seg, axis=-1, keepdims=True) + 1e-6)
                nrm = seg * r
                dseg = dn[:, lo:lo + GDN_HEAD_DIM]
                parts.append(r * (dseg - nrm * jnp.sum(dseg * nrm, axis=-1, keepdims=True)))
        dpre = jnp.concatenate(parts + [dve], axis=1) * dact_dpre
        o_ref[:, :QKV_W] = _conv_input_grad(dpre, w_ref, GDN_CONV, tm).astype(bf16)
        o_ref[:, QKV_W:] = dz_ref[...].astype(bf16)
        dw_ref[...] += _rows_to_block(_conv_weight_grad(dpre[:tm], ext, GDN_CONV, tm), 8, QKV_W)

    row = pl.BlockSpec((tm, GDN_W), lambda i: (i, 0))
    fut = pl.BlockSpec((HALO, GDN_W), lambda i: (jnp.minimum((i + 1) * hb, last_hb), 0))
    return pl.pallas_call(
        body, name=name, grid=(nt,),
        in_specs=[pl.BlockSpec((tm, QKV_W), lambda i: (i, 0)),
                  pl.BlockSpec((HALO, QKV_W), lambda i: (jnp.maximum(i * hb - 1, 0), 0)),
                  pl.BlockSpec((HALO, QKV_W), lambda i: (jnp.minimum((i + 1) * hb, last_hb), 0)),
                  pl.BlockSpec((8, QKV_W), lambda i: (0, 0)), row, fut, row, fut, row, fut, row],
        out_specs=[pl.BlockSpec((tm, 4 * GDN_W), lambda i: (i, 0)), pl.BlockSpec((8, QKV_W), lambda i: (0, 0))],
        out_shape=[jax.ShapeDtypeStruct((s, 4 * GDN_W), bf16), jax.ShapeDtypeStruct((8, QKV_W), f32)],
        compiler_params=_params("arbitrary"),
    )(gdn_in, gdn_in, gdn_in, convw, dqn, dqn, dkn, dkn, dv, dv, dgz)


def _gdn_post_head(o, z, nw):
    return o * lax.rsqrt(jnp.mean(o * o, axis=-1, keepdims=True) + 1e-6) * nw * (z * _sigmoid(z))


def gdn_post_fwd(o, gdn_in, nw, *, name):
    s = o.shape[0]
    tm = _tile(s, 256, 16)

    def body(o_ref, z_ref, nw_ref, y_ref):
        for h in range(GDN_HEADS):
            sl = slice(h * GDN_HEAD_DIM, (h + 1) * GDN_HEAD_DIM)
            y_ref[:, sl] = _gdn_post_head(o_ref[:, sl], z_ref[:, sl], nw_ref[...]).astype(bf16)

    row = pl.BlockSpec((tm, GDN_W), lambda i: (i, 0))
    return pl.pallas_call(body, name=name, grid=(s // tm,),
                          in_specs=[row, pl.BlockSpec((tm, GDN_W), lambda i: (i, 3)), pl.BlockSpec((1, 128), lambda i: (0, 0))],
                          out_specs=row, out_shape=jax.ShapeDtypeStruct((s, GDN_W), bf16),
                          compiler_params=_params("parallel"))(o, gdn_in, nw)


def gdn_post_bwd(o, gdn_in, nw, dy, *, name):
    s = o.shape[0]
    tm = _tile(s, 256, 16)

    def body(o_ref, z_ref, nw_ref, dy_ref, do_ref, dz_ref, dnw_ref):
        @pl.when(pl.program_id(0) == 0)
        def _():
            dnw_ref[...] = jnp.zeros_like(dnw_ref)

        dnw = jnp.zeros((1, 128), f32)
        for h in range(GDN_HEADS):
            sl = slice(h * GDN_HEAD_DIM, (h + 1) * GDN_HEAD_DIM)
            _, pull = jax.vjp(_gdn_post_head, o_ref[:, sl], z_ref[:, sl], nw_ref[...])
            do, dz, dn = pull(dy_ref[:, sl])
            do_ref[:, sl] = do
            dz_ref[:, sl] = dz
            dnw = dnw + dn
        dnw_ref[...] += dnw

    row = pl.BlockSpec((tm, GDN_W), lambda i: (i, 0))
    vec = pl.BlockSpec((1, 128), lambda i: (0, 0))
    return pl.pallas_call(body, name=name, grid=(s // tm,),
                          in_specs=[row, pl.BlockSpec((tm, GDN_W), lambda i: (i, 3)), vec, row], out_specs=[row, row, vec],
                          out_shape=[jax.ShapeDtypeStruct((s, GDN_W), f32), jax.ShapeDtypeStruct((s, GDN_W), f32),
                                     jax.ShapeDtypeStruct((1, 128), f32)],
                          compiler_params=_params("arbitrary"))(o, gdn_in, nw, dy)


def _gdn_chunks(q, k, v, gb, state):
    c = GDN_CHUNK
    heads = range(len(q))
    r = lax.broadcasted_iota(jnp.int32, (c, c), 0)
    cc = lax.broadcasted_iota(jnp.int32, (c, c), 1)
    tril, strict = r >= cc, r > cc
    eye = (r == cc).astype(f32)

    def dhi(a, b):
        return jnp.dot(a, b, precision=lax.Precision.HIGH, preferred_element_type=f32)

    beta = [gb[:, h:h + 1] for h in heads]
    cum_cols = dhi(tril.astype(f32), gb)
    cum_rows = dhi(gb.T, (r <= cc).astype(f32))
    gi = [jnp.broadcast_to(cum_cols[:, GDN_HEADS + h:GDN_HEADS + h + 1], (c, c)) for h in heads]
    gj = [jnp.broadcast_to(cum_rows[GDN_HEADS + h:GDN_HEADS + h + 1, :], (c, c)) for h in heads]
    decay = [jnp.where(tril, jnp.exp(jnp.where(tril, gi[h] - gj[h], 0.0)), 0.0) for h in heads]
    kb = [k[h] * beta[h] for h in heads]
    vb = [v[h] * beta[h] for h in heads]
    a = [jnp.where(strict, _d16(kb[h], k[h], NT) * decay[h], 0.0) for h in heads]
    tinv = [eye - a[h] for h in heads]
    pw = [dhi(a[h], a[h]) for h in heads]
    for it in range(5):
        tinv = [tinv[h] + dhi(tinv[h], pw[h]) for h in heads]
        if it < 4:
            pw = [dhi(pw[h], pw[h]) for h in heads]
    gc = [gi[h][:, 0:1] for h in heads]
    egc = [jnp.exp(gc[h]) for h in heads]
    u = [dhi(tinv[h], vb[h]) for h in heads]
    w = [dhi(tinv[h], kb[h] * egc[h]) for h in heads]
    qs = [q[h] * (GDN_HEAD_DIM ** -0.5) for h in heads]
    attn = [jnp.where(tril, _d16(qs[h], k[h], NT) * decay[h], 0.0) for h in heads]
    g_last = [gi[h][c - 1:c, 0:1] for h in heads]
    v_new = [u[h] - _d16(w[h], state[h], NN) for h in heads]
    out = [_d16(qs[h] * egc[h], state[h], NN) + _d16(attn[h], v_new[h], NN) for h in heads]
    new_state = [state[h] * jnp.exp(g_last[h]) + _d16(k[h] * jnp.exp(g_last[h] - gc[h]), v_new[h], TN) for h in heads]
    return out, new_state


def _head_cols(ref):
    return [ref[:, h * GDN_HEAD_DIM:(h + 1) * GDN_HEAD_DIM] for h in range(GDN_HEADS)]


def gdn_scan_fwd(qn, kn, v, gbeta, *, name):
    s = qn.shape[0]
    nc = s // GDN_CHUNK

    def body(q_ref, k_ref, v_ref, gb_ref, o_ref, st_ref, state_ref):
        @pl.when(pl.program_id(0) == 0)
        def _():
            state_ref[...] = jnp.zeros_like(state_ref)

        states = [state_ref[h] for h in range(GDN_HEADS)]
        outs, new = _gdn_chunks(_head_cols(q_ref), _head_cols(k_ref), _head_cols(v_ref), gb_ref[...], states)
        for h in range(GDN_HEADS):
            st_ref[0, h] = states[h]
            o_ref[:, h * GDN_HEAD_DIM:(h + 1) * GDN_HEAD_DIM] = outs[h]
            state_ref[h] = new[h]

    row = pl.BlockSpec((GDN_CHUNK, GDN_W), lambda n: (n, 0))
    return pl.pallas_call(
        body, name=name, grid=(nc,), in_specs=[row, row, row, pl.BlockSpec((GDN_CHUNK, 128), lambda n: (n, 0))],
        out_specs=[row, pl.BlockSpec((1, GDN_HEADS, GDN_HEAD_DIM, GDN_HEAD_DIM), lambda n: (n, 0, 0, 0))],
        out_shape=[jax.ShapeDtypeStruct((s, GDN_W), f32),
                   jax.ShapeDtypeStruct((nc, GDN_HEADS, GDN_HEAD_DIM, GDN_HEAD_DIM), f32)],
        scratch_shapes=[pltpu.VMEM((GDN_HEADS, GDN_HEAD_DIM, GDN_HEAD_DIM), f32)],
        compiler_params=_params("arbitrary"),
    )(qn, kn, v, gbeta)


def gdn_scan_bwd(qn, kn, v, gbeta, states, dout, *, name):
    s = qn.shape[0]
    nc = s // GDN_CHUNK

    def body(q_ref, k_ref, v_ref, gb_ref, st_ref, do_ref, dq_ref, dk_ref, dv_ref, dgb_ref, dstate_ref):
        @pl.when(pl.program_id(0) == 0)
        def _():
            dstate_ref[...] = jnp.zeros_like(dstate_ref)

        _, pull = jax.vjp(_gdn_chunks, _head_cols(q_ref), _head_cols(k_ref), _head_cols(v_ref), gb_ref[...],
                          [st_ref[0, h] for h in range(GDN_HEADS)])
        dq, dk, dv, dgb, dst = pull((_head_cols(do_ref), [dstate_ref[h] for h in range(GDN_HEADS)]))
        for h in range(GDN_HEADS):
            sl = slice(h * GDN_HEAD_DIM, (h + 1) * GDN_HEAD_DIM)
            dq_ref[:, sl] = dq[h]
            dk_ref[:, sl] = dk[h]
            dv_ref[:, sl] = dv[h]
            dstate_ref[h] = dst[h]
        dgb_ref[...] = dgb

    row = pl.BlockSpec((GDN_CHUNK, GDN_W), lambda n: (nc - 1 - n, 0))
    gb = pl.BlockSpec((GDN_CHUNK, 128), lambda n: (nc - 1 - n, 0))
    return pl.pallas_call(
        body, name=name, grid=(nc,),
        in_specs=[row, row, row, gb, pl.BlockSpec((1, GDN_HEADS, GDN_HEAD_DIM, GDN_HEAD_DIM), lambda n: (nc - 1 - n, 0, 0, 0)), row],
        out_specs=[row, row, row, gb],
        out_shape=[jax.ShapeDtypeStruct((s, GDN_W), f32)] * 3 + [jax.ShapeDtypeStruct((s, 128), f32)],
        scratch_shapes=[pltpu.VMEM((GDN_HEADS, GDN_HEAD_DIM, GDN_HEAD_DIM), f32)],
        compiler_params=_params("arbitrary"),
    )(qn, kn, v, gbeta, states, dout)


def adamw(w, g, m, v, *, name):
    r, c = w.shape
    tr = _tile(r, 256, 8)

    def body(w_ref, g_ref, m_ref, v_ref, d_ref, nm_ref, nv_ref):
        gv = g_ref[...]
        nm = ADAM_B1 * m_ref[...] + (1.0 - ADAM_B1) * gv
        nv = ADAM_B2 * v_ref[...] + (1.0 - ADAM_B2) * (gv * gv)
        m_hat = nm / (1.0 - ADAM_B1 ** ADAM_STEP)
        v_hat = nv / (1.0 - ADAM_B2 ** ADAM_STEP)
        d_ref[...] = -ADAM_LR * (m_hat / (jnp.sqrt(v_hat) + ADAM_EPS) + ADAM_WD * w_ref[...])
        nm_ref[...] = nm
        nv_ref[...] = nv

    spec = pl.BlockSpec((tr, c), lambda i: (i, 0))
    return pl.pallas_call(body, name=name, grid=(r // tr,), in_specs=[spec] * 4, out_specs=[spec] * 3,
                          out_shape=[jax.ShapeDtypeStruct((r, c), f32)] * 3, compiler_params=_params("parallel"))(w, g, m, v)


def _pos():
    return lax.axis_index("x"), lax.axis_index("y"), lax.axis_index("c")


ANY = pl.BlockSpec(memory_space=pl.ANY)


def all_gather_shards(shards, *, name):
    na = len(shards)

    def body(*refs):
        x_refs, out_refs = refs[:na], refs[na:2 * na]
        send_sems, recv_sems, local_sems = refs[2 * na:]
        x, y, cc = _pos()
        me, sibling = (x, y, cc), (x, y, 1 - cc)
        chips = [(1 - x, y), (x, 1 - y), (1 - x, 1 - y)]

        def copy(a, k, block, to, own=False):
            px, py, pc = block
            dst = out_refs[a].at[4 * px + 2 * py + pc]
            return pltpu.make_async_remote_copy(src_ref=x_refs[a] if own else dst, dst_ref=dst, send_sem=send_sems.at[7 * a + k],
                                                recv_sem=recv_sems.at[7 * a + k], device_id=to, device_id_type=MESH)

        mine = [pltpu.make_async_copy(x_refs[a], out_refs[a].at[4 * x + 2 * y + cc], local_sems.at[a]) for a in range(na)]
        first = [[copy(a, 0, me, sibling, own=True)] + [copy(a, 1 + j, me, (*chip, cc), own=True) for j, chip in enumerate(chips)]
                 for a in range(na)]
        for a in range(na):
            mine[a].start()
            for cp in first[a]:
                cp.start()
        passed = [[copy(a, 4 + j, (*chip, cc), sibling) for j, chip in enumerate(chips)] for a in range(na)]
        for a in range(na):
            for j, chip in enumerate(chips):
                copy(a, 1 + j, (*chip, cc), me).wait_recv()
                passed[a][j].start()
        for a in range(na):
            copy(a, 0, sibling, me).wait_recv()
            for j, chip in enumerate(chips):
                copy(a, 4 + j, (*chip, 1 - cc), me).wait_recv()
            for cp in first[a] + passed[a]:
                cp.wait_send()
            mine[a].wait()

    return pl.pallas_call(body, name=name, in_specs=[ANY] * na, out_specs=[ANY] * na,
                          out_shape=[jax.ShapeDtypeStruct((N_DEV,) + s.shape, s.dtype) for s in shards],
                          scratch_shapes=[pltpu.SemaphoreType.DMA((7 * na,)), pltpu.SemaphoreType.DMA((7 * na,)),
                                          pltpu.SemaphoreType.DMA((na,))])(*shards)


def exchange_d2d(chunks, *, name):
    na = len(chunks)

    def body(*refs):
        g_refs, out_refs = refs[:na], refs[na:2 * na]
        send_sems, recv_sems = refs[2 * na:]
        x, y, cc = _pos()
        for a in range(na):
            for q in range(4):
                pltpu.make_async_remote_copy(src_ref=g_refs[a].at[2 * q + 1 - cc], dst_ref=out_refs[a].at[q],
                                             send_sem=send_sems.at[a], recv_sem=recv_sems.at[a], device_id=(x, y, 1 - cc),
                                             device_id_type=MESH).start()
        for a in range(na):
            pltpu.make_async_remote_copy(src_ref=out_refs[a], dst_ref=out_refs[a], send_sem=send_sems.at[a],
                                         recv_sem=recv_sems.at[a], device_id=(x, y, 1 - cc), device_id_type=MESH).wait()

    return pl.pallas_call(body, name=name, in_specs=[ANY] * na, out_specs=[ANY] * na,
                          out_shape=[jax.ShapeDtypeStruct((4,) + g.shape[1:], g.dtype) for g in chunks],
                          scratch_shapes=[pltpu.SemaphoreType.DMA((na,)), pltpu.SemaphoreType.DMA((na,))])(*chunks)


def add_sibling(chunks, recv, *, name):
    _, r, c = chunks.shape
    tr = r if r <= 1024 else _tile(r, 512, 16)
    core = lax.axis_index("c").astype(jnp.int32).reshape(1)

    def body(core_ref, a_ref, b_ref, o_ref):
        o_ref[...] = (a_ref[...] + b_ref[...]).astype(bf16)

    return pl.pallas_call(
        body, name=name,
        grid_spec=pltpu.PrefetchScalarGridSpec(
            num_scalar_prefetch=1, grid=(4, r // tr),
            in_specs=[pl.BlockSpec((1, tr, c), lambda q, i, core_ref: (2 * q + core_ref[0], i, 0)),
                      pl.BlockSpec((1, tr, c), lambda q, i, core_ref: (q, i, 0))],
            out_specs=pl.BlockSpec((1, tr, c), lambda q, i, core_ref: (q, i, 0))),
        out_shape=jax.ShapeDtypeStruct((4, r, c), bf16), compiler_params=_params("parallel", "parallel"),
    )(core, chunks, recv)


def exchange_ici(parts, *, name):
    na = len(parts)

    def body(*refs):
        p_refs, out_refs = refs[:na], refs[na:2 * na]
        send_sems, recv_sems, local_sems = refs[2 * na:]
        x, y, cc = _pos()
        chips = [(1 - x, y), (x, 1 - y), (1 - x, 1 - y)]
        mine = [pltpu.make_async_copy(p_refs[a].at[2 * x + y], out_refs[a].at[3], local_sems.at[a]) for a in range(na)]
        cps = [pltpu.make_async_remote_copy(src_ref=p_refs[a].at[2 * px + py], dst_ref=out_refs[a].at[k],
                                            send_sem=send_sems.at[3 * a + k], recv_sem=recv_sems.at[3 * a + k],
                                            device_id=(px, py, cc), device_id_type=MESH)
               for a in range(na) for k, (px, py) in enumerate(chips)]
        for cp in mine + cps:
            cp.start()
        for cp in cps:
            cp.wait()
        for cp in mine:
            cp.wait()

    return pl.pallas_call(body, name=name, in_specs=[ANY] * na, out_specs=[ANY] * na,
                          out_shape=[jax.ShapeDtypeStruct(p.shape, p.dtype) for p in parts],
                          scratch_shapes=[pltpu.SemaphoreType.DMA((3 * na,)), pltpu.SemaphoreType.DMA((3 * na,)),
                                          pltpu.SemaphoreType.DMA((na,))])(*parts)


def add_four(r4, *, name):
    _, r, c = r4.shape
    tr = r if r <= 1024 else _tile(r, 512, 16)

    def body(a_ref, o_ref):
        o_ref[...] = ((a_ref[3].astype(f32) + a_ref[0].astype(f32)) + a_ref[1].astype(f32)) + a_ref[2].astype(f32)

    return pl.pallas_call(body, name=name, grid=(r // tr,), in_specs=[pl.BlockSpec((4, tr, c), lambda i: (0, i, 0))],
                          out_specs=pl.BlockSpec((tr, c), lambda i: (i, 0)), out_shape=jax.ShapeDtypeStruct((r, c), f32),
                          compiler_params=_params("parallel"))(r4)


def all_reduce_small(vec, *, name):
    r, c = vec.shape

    def body(v_ref, out_ref, buf_ref, send_sems, recv_sems):
        x, y, cc = _pos()
        my_id = 4 * x + 2 * y + cc
        buf_ref[my_id] = v_ref[...]
        flips = [(fx, fy, fc) for fx in (0, 1) for fy in (0, 1) for fc in (0, 1)][1:]
        cps = []
        for k, (fx, fy, fc) in enumerate(flips):
            peer = ((1 - x) if fx else x, (1 - y) if fy else y, (1 - cc) if fc else cc)
            cps.append(pltpu.make_async_remote_copy(src_ref=v_ref, dst_ref=buf_ref.at[my_id], send_sem=send_sems.at[k],
                                                    recv_sem=recv_sems.at[k], device_id=peer, device_id_type=MESH))
        for cp in cps:
            cp.start()
        for cp in cps:
            cp.wait()
        acc = buf_ref[0]
        for d in range(1, N_DEV):
            acc = acc + buf_ref[d]
        out_ref[...] = acc

    vm = pl.BlockSpec(memory_space=pltpu.VMEM)
    return pl.pallas_call(body, name=name, in_specs=[vm], out_specs=vm, out_shape=jax.ShapeDtypeStruct((r, c), f32),
                          scratch_shapes=[pltpu.VMEM((N_DEV, r, c), f32), pltpu.SemaphoreType.DMA((7,)),
                                          pltpu.SemaphoreType.DMA((7,))])(vec)


def _pack(parts, rows, dtype):
    flat = jnp.concatenate([p.reshape(-1).astype(dtype) for p in parts])
    return jnp.pad(flat, (0, rows * PACK_COLS - flat.shape[0])).reshape(rows, PACK_COLS)


def _unpack(flat, shapes):
    out, off = [], 0
    for shp in shapes:
        n = shp[0] * shp[1]
        out.append(flat[..., off:off + n].reshape(flat.shape[:-1] + tuple(shp)))
        off += n
    return out


def _from_column_shards(g):
    _, r, c = g.shape
    return jnp.transpose(g, (1, 0, 2)).reshape(r, N_DEV * c)


def _column_shards(full):
    r, c8 = full.shape
    return jnp.transpose(full.reshape(r, N_DEV, c8 // N_DEV), (1, 0, 2))


W_IN_SHARD = IN_DIM // N_DEV
W_IN_PAD = 1280
W_IN_PARTS = (("swa", 0, 0, 1280), ("swa", 1280, 5376, 5392), ("gdn", 0, 1280, 5376), ("gates", 0, 5392, IN_DIM))
W_IN_WIDTHS = {"swa": SWA_IN_W, "gdn": 4 * GDN_W, "gates": 2 * D_MODEL}


def _w_in_segments():
    segs = []
    for part, p0, g0, g1 in W_IN_PARTS:
        for j in range(N_DEV):
            lo, hi = max(g0, W_IN_SHARD * j), min(g1, W_IN_SHARD * (j + 1))
            if lo < hi:
                segs.append((part, p0 + lo - g0, j, lo - W_IN_SHARD * j, hi - lo))
    return segs


def split_w_in(shards, *, name):
    dt = shards.dtype
    tm = 256

    def body(w_ref, swa_ref, gdn_ref, gates_ref):
        out = {"swa": swa_ref, "gdn": gdn_ref, "gates": gates_ref}
        swa_ref[:, SWA_Q + 2 * SWA_KV + 2 * GDN_HEADS:] = jnp.zeros((tm, SWA_IN_W - SWA_Q - 2 * SWA_KV - 2 * GDN_HEADS), dt)
        for part, p0, j, l0, n in _w_in_segments():
            out[part][:, p0:p0 + n] = w_ref[j, :, l0:l0 + n]

    return pl.pallas_call(body, name=name, grid=(D_MODEL // tm,),
                          in_specs=[pl.BlockSpec((N_DEV, tm, W_IN_PAD), lambda i: (0, i, 0))],
                          out_specs=[pl.BlockSpec((tm, W_IN_WIDTHS[p]), lambda i: (i, 0)) for p in ("swa", "gdn", "gates")],
                          out_shape=[jax.ShapeDtypeStruct((D_MODEL, W_IN_WIDTHS[p]), dt) for p in ("swa", "gdn", "gates")],
                          compiler_params=_params("parallel"))(shards)


def merge_w_in_grad(d_swa, d_gdn, d_gates, *, name):
    tm = 256

    def body(swa_ref, gdn_ref, gates_ref, w_ref):
        src = {"swa": swa_ref, "gdn": gdn_ref, "gates": gates_ref}
        w_ref[:, :, W_IN_SHARD:] = jnp.zeros((N_DEV, tm, W_IN_PAD - W_IN_SHARD), f32)
        for part, p0, j, l0, n in _w_in_segments():
            w_ref[j, :, l0:l0 + n] = src[part][:, p0:p0 + n]

    return pl.pallas_call(body, name=name, grid=(D_MODEL // tm,),
                          in_specs=[pl.BlockSpec((tm, W_IN_WIDTHS[p]), lambda i: (i, 0)) for p in ("swa", "gdn", "gates")],
                          out_specs=pl.BlockSpec((N_DEV, tm, W_IN_PAD), lambda i: (0, i, 0)),
                          out_shape=jax.ShapeDtypeStruct((N_DEV, D_MODEL, W_IN_PAD), f32),
                          compiler_params=_params("parallel"))(d_swa, d_gdn, d_gates)


def kernel(x, mem, w_in, rel_bias, swa_sinks, gdn_conv_w, gdn_a_log, gdn_dt_bias, gdn_norm_w, w_br_swa, w_br_gdn, w_mix_o, ln1_g, ln1_b, w_mem_q, w_mem_kv, w_mem_o, ln2_g, ln2_b, w_up, ffn_conv_w, ffn_conv_b, w_down, ln3_g, ln3_b, loss_target, m_w_in, m_rel_bias, m_swa_sinks, m_gdn_conv_w, m_gdn_a_log, m_gdn_dt_bias, m_gdn_norm_w, m_w_br_swa, m_w_br_gdn, m_w_mix_o, m_ln1_g, m_ln1_b, m_w_mem_q, m_w_mem_kv, m_w_mem_o, m_ln2_g, m_ln2_b, m_w_up, m_ffn_conv_w, m_ffn_conv_b, m_w_down, m_ln3_g, m_ln3_b, v_w_in, v_rel_bias, v_swa_sinks, v_gdn_conv_w, v_gdn_a_log, v_gdn_dt_bias, v_gdn_norm_w, v_w_br_swa, v_w_br_gdn, v_w_mix_o, v_ln1_g, v_ln1_b, v_w_mem_q, v_w_mem_kv, v_w_mem_o, v_ln2_g, v_ln2_b, v_w_up, v_ffn_conv_w, v_ffn_conv_b, v_w_down, v_ln3_g, v_ln3_b):
    env = dict(locals())
    w2 = {n: (env[n][0] if env[n].ndim == 3 else env[n]) for n in WEIGHTS}
    m2 = {n: (env["m_" + n][0] if env["m_" + n].ndim == 3 else env["m_" + n]) for n in WEIGHTS}
    v2 = {n: (env["v_" + n][0] if env["v_" + n].ndim == 3 else env["v_" + n]) for n in WEIGHTS}
    xs, mems, target = x[0], mem[0], loss_target[0]
    my_id = 4 * lax.axis_index("x") + 2 * lax.axis_index("y") + lax.axis_index("c")
    pad_ff = FF_PAD - FF_SHARD

    sent = ("w_in", "w_br_swa", "w_br_gdn", "w_mem_o", "w_mix_o", "w_down", "w_mem_q", "w_mem_kv", "w_up")
    pad_cols = {"w_in": W_IN_PAD - W_IN_SHARD, "w_up": pad_ff}
    got = dict(zip(sent, all_gather_shards(
        [jnp.pad(w2[n], ((0, 0), (0, pad_cols.get(n, 0)))).astype(bf16) for n in sent], name="gather_weights")))
    w_swa, w_gdn, w_gates = split_w_in(got["w_in"], name="split_w_in")
    w_br_swa, w_br_gdn, w_mem_o = (_from_column_shards(got[n]) for n in ("w_br_swa", "w_br_gdn", "w_mem_o"))
    w_mix_o = got["w_mix_o"].reshape(D_MODEL, D_MODEL)
    w_mem_q = got["w_mem_q"].reshape(D_MODEL, MEM_W)
    w_mem_kv = got["w_mem_kv"].reshape(D_MODEL, 2 * MEM_W)
    w_down_p = jnp.pad(got["w_down"].reshape(4, FF_SHARD, D_MODEL), ((0, 0), (0, pad_ff), (0, 0))).reshape(4 * FF_PAD, D_MODEL)
    n_ffn, n_gdn = 3 * FF_SHARD, GDN_CONV * (QKV_W // N_DEV)
    conv_mine = jnp.concatenate([w2["ffn_conv_w"].reshape(-1), w2["gdn_conv_w"].reshape(-1)])[None]
    conv_rows = lax.dynamic_update_slice(jnp.zeros((N_DEV, n_ffn + n_gdn), f32), conv_mine, (my_id, 0))
    conv_all = all_reduce_small(_pack([conv_rows], CONV_ROWS, f32), name="gather_conv_w")
    conv_all = conv_all.reshape(-1)[:N_DEV * (n_ffn + n_gdn)].reshape(N_DEV, n_ffn + n_gdn)
    cwb = jnp.concatenate([conv_all[:, :n_ffn].reshape(N_DEV, 3, FF_SHARD), w2["ffn_conv_b"].reshape(N_DEV, 1, FF_SHARD),
                           jnp.zeros((N_DEV, 4, FF_SHARD), f32)], axis=1)
    cwb = jnp.pad(cwb, ((0, 0), (0, 0), (0, pad_ff)))
    convw = jnp.transpose(conv_all[:, n_ffn:].reshape(N_DEV, GDN_CONV, QKV_W // N_DEV), (1, 0, 2)).reshape(GDN_CONV, QKV_W)
    convw = jnp.pad(convw, ((0, 4), (0, 0)))
    onehot = _bucket_onehot()
    bias = mm(w2["rel_bias"].T, onehot, "nn", hi=True, tn=4096, name="rel_bias_table").reshape(SWA_HEADS, BLOCK, 2 * BLOCK)
    alog_row = jnp.pad(w2["gdn_a_log"], ((0, 0), (GDN_HEADS, 128 - 2 * GDN_HEADS)))
    dt_row = jnp.pad(w2["gdn_dt_bias"], ((0, 0), (GDN_HEADS, 128 - 2 * GDN_HEADS)))

    xb = cast_bf16(xs, name="cast_x")
    memb = cast_bf16(mems, name="cast_mem")
    gates = mm(xb, w_gates, "nn", name="proj_gates")
    gdn_in = mm(xb, w_gdn, "nn", name="proj_gdn")
    swa_in = mm(xb, w_swa, "nn", tn=SWA_IN_W, name="proj_swa")
    attn = swa_fwd(swa_in, bias, w2["swa_sinks"], name="swa_fwd")
    qn, kn, vv = gdn_pre_fwd(gdn_in, convw, name="gdn_pre_fwd")
    gbeta = gbeta_fwd(swa_in, alog_row, dt_row, name="gbeta_fwd")
    o_gdn, states = gdn_scan_fwd(qn, kn, vv, gbeta, name="gdn_scan_fwd")
    ygd = gdn_post_fwd(o_gdn, gdn_in, w2["gdn_norm_w"], name="gdn_post_fwd")
    y_swa = mm(attn, w_br_swa, "nn", name="br_swa")
    y_gdn = mm(ygd, w_br_gdn, "nn", name="br_gdn")
    mixed = merge_fwd(gates, y_swa, y_gdn, name="merge_fwd")
    z1 = mm(mixed, w_mix_o, "nn", add=xs, add_scale=ALPHA, name="mix_o")
    x1, x1b = ln_fwd(z1, w2["ln1_g"], w2["ln1_b"], name="ln1_fwd")
    qm = mm(x1b, w_mem_q, "nn", name="mem_q")
    kv = mm(memb, w_mem_kv, "nn", name="mem_kv")
    om = memattn_fwd(qm, kv, name="memattn_fwd")
    z2 = mm(om, w_mem_o, "nn", add=x1, add_scale=ALPHA, name="mem_o")
    x2, x2b = ln_fwd(z2, w2["ln2_g"], w2["ln2_b"], name="ln2_fwd")
    hpre = mm(x2b, got["w_up"], "nn", b_blocked=True, name="ffn_up")
    act = ffn_act_fwd(hpre, cwb, name="ffn_act_fwd")
    z3 = mm(act, w_down_p, "nn", add=x2, add_scale=ALPHA, tk=2 * FF_PAD, name="ffn_down")
    dz3, dz3b, d_ln3g, d_ln3b, loss = ln_loss(z3, target, w2["ln3_g"], w2["ln3_b"], name="ln3_loss")

    dact = mm(dz3b, w_down_p, "nt", tn=FF_PAD, name="d_act")
    d_wdown_p = mm(act, dz3b, "tn", tm=FF_PAD, name="dw_down")
    d_hpre, d_cwb = ffn_act_bwd(hpre, dact, cwb, name="ffn_act_bwd")
    dx2 = mm(d_hpre, got["w_up"], "nt", b_blocked=True, add=dz3, add_scale=ALPHA, name="d_x2")
    d_wup = mm(x2b, d_hpre, "tn", out_blocked=True, name="dw_up")
    dz2, dz2b, d_ln2g, d_ln2b = ln_bwd(dx2, z2, w2["ln2_g"], name="ln2_bwd")
    d_om = mm(dz2b, w_mem_o, "nt", name="d_om")
    d_wmemo = mm(om, dz2b, "tn", name="dw_mem_o")
    dqm, dkv = memattn_bwd(qm, kv, d_om, name="memattn_bwd")
    dx1 = mm(dqm, w_mem_q, "nt", add=dz2, add_scale=ALPHA, name="d_x1")
    d_wmemq = mm(x1b, dqm, "tn", name="dw_mem_q")
    d_wmemkv = mm(memb, dkv, "tn", name="dw_mem_kv")
    dz1, dz1b, d_ln1g, d_ln1b = ln_bwd(dx1, z1, w2["ln1_g"], name="ln1_bwd")
    dmix = mm(dz1b, w_mix_o, "nt", name="d_mixed")
    d_wmixo = mm(mixed, dz1b, "tn", name="dw_mix_o")
    dys, dyg, d_gates = merge_bwd(gates, y_swa, y_gdn, dmix, name="merge_bwd")
    d_attn = mm(dys, w_br_swa, "nt", name="d_attn")
    d_wbrswa = mm(attn, dys, "tn", name="dw_br_swa")
    d_ygd = mm(dyg, w_br_gdn, "nt", name="d_ygd")
    d_wbrgdn = mm(ygd, dyg, "tn", name="dw_br_gdn")
    d_o, d_gz, d_normw = gdn_post_bwd(o_gdn, gdn_in, w2["gdn_norm_w"], d_ygd, name="gdn_post_bwd")
    dqn, dkn, dvv, dgbeta = gdn_scan_bwd(qn, kn, vv, gbeta, states, d_o, name="gdn_scan_bwd")
    d_gdn_in, d_convw = gdn_pre_bwd(gdn_in, convw, dqn, dkn, dvv, d_gz, name="gdn_pre_bwd")
    d_ba, d_alog, d_dt = gbeta_bwd(swa_in, alog_row, dt_row, dgbeta, name="gbeta_bwd")
    dq, dkc, dkp, dvc, dvp, d_bias, d_sinks = swa_bwd(swa_in, bias, w2["swa_sinks"], d_attn, name="swa_bwd")
    d_swa_in = swa_in_grad(dq, dkc, dkp, dvc, dvp, d_ba, name="swa_in_grad")
    d_relbias = mm(d_bias.reshape(SWA_HEADS, -1), onehot, "nt", hi=True, tk=4096, name="d_rel_bias").T
    gx = mm(d_gates, w_gates, "nt", add=dz1, add_scale=ALPHA, name="dx_gates")
    gx = mm(d_gdn_in, w_gdn, "nt", add=gx, name="dx_gdn")
    gx = mm(d_swa_in, w_swa, "nt", add=gx, tk=SWA_IN_W, name="dx_swa")
    d_wgates = mm(xb, d_gates, "tn", name="dw_gates")
    d_wgdn = mm(xb, d_gdn_in, "tn", name="dw_gdn")
    d_wswa = mm(xb, d_swa_in, "tn", tn=SWA_IN_W, name="dw_swa")

    chunks = dict(zip(sent, (
        merge_w_in_grad(d_wswa, d_wgdn, d_wgates, name="merge_w_in_grad"),
        _column_shards(d_wbrswa), _column_shards(d_wbrgdn), _column_shards(d_wmemo),
        d_wmixo.reshape(N_DEV, D_MODEL // N_DEV, D_MODEL),
        d_wdown_p.reshape(4, FF_PAD, D_MODEL)[:, :FF_SHARD].reshape(N_DEV, FF_SHARD // 2, D_MODEL),
        d_wmemq.reshape(N_DEV, D_MODEL // N_DEV, MEM_W), d_wmemkv.reshape(N_DEV, D_MODEL // N_DEV, 2 * MEM_W), d_wup)))
    from_sibling = exchange_d2d([chunks[n] for n in sent], name="grad_exchange_d2d")
    chip_sums = [add_sibling(chunks[n], r, name="grad_add_sibling_" + n) for n, r in zip(sent, from_sibling)]
    from_chips = exchange_ici(chip_sums, name="grad_exchange_ici")
    grads = {n: add_four(r, name="grad_add_chips_" + n) for n, r in zip(sent, from_chips)}
    grads["w_in"] = grads["w_in"][:, :W_IN_SHARD]
    grads["w_up"] = grads["w_up"][:, :FF_SHARD]

    gsmall = {
        "rel_bias": d_relbias, "swa_sinks": d_sinks[:, :SWA_HEADS], "gdn_a_log": d_alog[:, GDN_HEADS:2 * GDN_HEADS],
        "gdn_dt_bias": d_dt[:, GDN_HEADS:2 * GDN_HEADS], "gdn_norm_w": d_normw, "ln1_g": d_ln1g, "ln1_b": d_ln1b,
        "ln2_g": d_ln2g, "ln2_b": d_ln2b, "ln3_g": d_ln3g, "ln3_b": d_ln3b,
        "ffn_conv_b": d_cwb[:, 3, :FF_SHARD].reshape(1, 2 * D_FF),
        "ffn_conv_w": jnp.transpose(d_cwb[:, :3, :FF_SHARD], (1, 0, 2)).reshape(3, 2 * D_FF),
        "gdn_conv_w": d_convw[:GDN_CONV],
    }
    small_shapes = [shp for _, shp in SMALL] + [(3, 2 * D_FF), (GDN_CONV, QKV_W)]
    small_names = [n for n, _ in SMALL] + ["ffn_conv_w", "gdn_conv_w"]
    small_sum = all_reduce_small(_pack([gsmall[n] for n in small_names], AR_ROWS, f32), name="all_reduce_small")
    grads.update(zip(small_names, _unpack(small_sum.reshape(-1), small_shapes)))
    grads["ffn_conv_w"] = lax.dynamic_slice_in_dim(grads["ffn_conv_w"], my_id * FF_SHARD, FF_SHARD, axis=1)
    grads["gdn_conv_w"] = lax.dynamic_slice_in_dim(grads["gdn_conv_w"], my_id * (QKV_W // N_DEV), QKV_W // N_DEV, axis=1)

    big = [n for n, shp, _ in SHARDED if shp[0] * shp[1] > 8192]
    tiny = [n for n in WEIGHTS if n not in big]
    delta, new_m, new_v = {}, {}, {}
    for n in big:
        delta[n], new_m[n], new_v[n] = adamw(w2[n], grads[n], m2[n], v2[n], name="adamw_" + n)
    tiny_shapes = [w2[n].shape for n in tiny]
    packed = [_pack([src[n] for n in tiny], SMALL_ROWS, f32) for src in (w2, grads, m2, v2)]
    for dst, res in zip((delta, new_m, new_v), adamw(*packed, name="adamw_small")):
        dst.update(zip(tiny, _unpack(res.reshape(-1), tiny_shapes)))

    def shaped(d):
        return [d[n].reshape(env[n].shape) for n in WEIGHTS]

    loss_all = lax.psum(loss[0, 0], ("x", "y", "c"))
    return (loss_all, gx[None], *shaped(grads), *shaped(delta), *shaped(new_m), *shaped(new_v))
```

```python
import functools
import math
from typing import Callable, NamedTuple

import jax
import jax.numpy as jnp
from jax import lax
from jax.experimental import pallas as pl
from jax.experimental.pallas import tpu as pltpu

f32 = jnp.float32
bf16 = jnp.bfloat16
HI = lax.Precision.HIGHEST
MESH = pl.DeviceIdType.MESH

D_MODEL = 2048
N_DEV = 8
SWA_HEADS, SWA_KV_HEADS, SWA_HEAD_DIM, BLOCK = 16, 2, 64, 128
REL_BUCKETS, REL_MAX_DIST = 32, 128
GDN_HEADS, GDN_HEAD_DIM, GDN_CONV, GDN_CHUNK = 8, 128, 4, 64
MEM_HEADS, MEM_HEAD_DIM = 4, 128
D_FF = 5504
FF_SHARD = 2 * D_FF // N_DEV
FF_PAD = 1408
NORM_EPS = 1e-5
ALPHA = 2.0 ** 0.25
NEG_INF = -1e30
SWA_Q, SWA_KV, GDN_W, MEM_W = 1024, 128, 1024, 512
IN_DIM = 9488
HALO = 8

ADAM_LR, ADAM_B1, ADAM_B2, ADAM_EPS, ADAM_WD, ADAM_STEP = 0.001, 0.9, 0.999, 1e-08, 0.01, 10

PACK_COLS = 1024
SMALL_ROWS = 32
AR_ROWS = 72
CONV_ROWS = 48

SHARDED = (
    ("w_in", (2048, 1186), 1), ("w_br_swa", (1024, 256), 1), ("w_br_gdn", (1024, 256), 1),
    ("w_mix_o", (256, 2048), 0), ("w_mem_q", (256, 512), 0), ("w_mem_kv", (256, 1024), 0),
    ("w_mem_o", (512, 256), 1), ("w_up", (2048, 1376), 1), ("w_down", (688, 2048), 0),
    ("ffn_conv_w", (3, 1376), 1), ("gdn_conv_w", (4, 384), 1),
)
SMALL = (
    ("rel_bias", (32, 16)), ("swa_sinks", (1, 16)), ("gdn_a_log", (1, 8)), ("gdn_dt_bias", (1, 8)),
    ("gdn_norm_w", (1, 128)), ("ln1_g", (1, 2048)), ("ln1_b", (1, 2048)), ("ln2_g", (1, 2048)),
    ("ln2_b", (1, 2048)), ("ln3_g", (1, 2048)), ("ln3_b", (1, 2048)), ("ffn_conv_b", (1, 11008)),
)
WEIGHTS = ("w_in", "rel_bias", "swa_sinks", "gdn_conv_w", "gdn_a_log", "gdn_dt_bias", "gdn_norm_w", "w_br_swa",
           "w_br_gdn", "w_mix_o", "ln1_g", "ln1_b", "w_mem_q", "w_mem_kv", "w_mem_o", "ln2_g", "ln2_b", "w_up",
           "ffn_conv_w", "ffn_conv_b", "w_down", "ln3_g", "ln3_b")


def _tile(n, target, align):
    if n <= target:
        return n
    t = (target // align) * align
    while t >= align:
        if n % t == 0:
            return t
        t -= align
    return n


VMEM_LIMIT_BYTES = 56 * 1024 * 1024


def _params(*sem):
    return pltpu.CompilerParams(dimension_semantics=sem, vmem_limit_bytes=VMEM_LIMIT_BYTES)


def _sigmoid(v):
    return jax.nn.sigmoid(v)


def _d16(a, b, dims):
    return lax.dot_general(a.astype(bf16), b.astype(bf16), (dims, ((), ())), preferred_element_type=f32)


NN = ((1,), (0,))
NT = ((1,), (1,))
TN = ((0,), (0,))


def mm(a, b, mode, *, name, add=None, add_scale=1.0, out_dtype=f32, hi=False, tm=1024, tn=1024, tk=2048,
       b_blocked=False, out_blocked=False, side=None):
    if b_blocked:
        nb, rows, width = b.shape
        if mode == "nn":
            (m, k), n, tn = a.shape, nb * width, width
        else:
            (m, k), n, tk = a.shape, rows, width
    elif mode == "nn":
        (m, k), (_, n) = a.shape, b.shape
    elif mode == "nt":
        (m, k), (n, _) = a.shape, b.shape
    else:
        (k, m), (_, n) = a.shape, b.shape
    if out_blocked:
        tn = n // N_DEV
    tm, tn, tk = _tile(m, tm, 8 if mode != "tn" else 128), _tile(n, tn, 128), _tile(k, tk, 128 if mode != "tn" else 8)
    nk = k // tk
    dims = {"nn": NN, "nt": NT, "tn": TN}[mode]
    a_spec = pl.BlockSpec((tk, tm), lambda i, j, kk: (kk, i)) if mode == "tn" else pl.BlockSpec((tm, tk), lambda i, j, kk: (i, kk))
    if b_blocked:
        b_spec = (pl.BlockSpec((None, tk, tn), lambda i, j, kk: (j, kk, 0)) if mode == "nn"
                  else pl.BlockSpec((None, tn, tk), lambda i, j, kk: (kk, j, 0)))
    else:
        b_spec = pl.BlockSpec((tn, tk), lambda i, j, kk: (j, kk)) if mode == "nt" else pl.BlockSpec((tk, tn), lambda i, j, kk: (kk, j))
    if out_blocked:
        o_spec, o_shape = pl.BlockSpec((None, tm, tn), lambda i, j, kk: (j, i, 0)), (N_DEV, m, tn)
    else:
        o_spec, o_shape = pl.BlockSpec((tm, tn), lambda i, j, kk: (i, j)), (m, n)
    has_add = add is not None

    def product(a_ref, b_ref):
        if hi:
            return lax.dot_general(a_ref[...], b_ref[...], (dims, ((), ())), precision=HI, preferred_element_type=f32)
        return _d16(a_ref[...], b_ref[...], dims)

    def finish(r, add_ref, o_ref):
        if has_add:
            r = r + add_scale * add_ref[...]
        o_ref[...] = r.astype(out_dtype)

    def body_one_step(a_ref, b_ref, *rest):
        finish(product(a_ref, b_ref), rest[0] if has_add else None, rest[-1])

    def body_k_steps(a_ref, b_ref, *rest):
        o_ref, acc_ref = rest[-2:]
        kk = pl.program_id(2)

        @pl.when(kk == 0)
        def _():
            acc_ref[...] = jnp.zeros_like(acc_ref)

        acc_ref[...] += product(a_ref, b_ref)

        @pl.when(kk == nk - 1)
        def _():
            finish(acc_ref[...], rest[0] if has_add else None, o_ref)

    return _call(body_one_step if nk == 1 else body_k_steps, (a, b, add) if has_add else (a, b), name=name,
                 grid=(m // tm, n // tn, nk), in_specs=[a_spec, b_spec] + ([o_spec] if has_add else []), out_specs=o_spec,
                 out_shape=jax.ShapeDtypeStruct(o_shape, out_dtype),
                 scratch_shapes=[] if nk == 1 else [pltpu.VMEM((tm, tn), f32)],
                 semantics=("parallel", "parallel", "arbitrary"), side=side)


def cast_bf16(a, *, name):
    m, n = a.shape
    tm = _tile(m, 512, 16)

    def body(a_ref, o_ref):
        o_ref[...] = a_ref[...].astype(bf16)

    return pl.pallas_call(body, name=name, grid=(m // tm,), in_specs=[pl.BlockSpec((tm, n), lambda i: (i, 0))],
                          out_specs=pl.BlockSpec((tm, n), lambda i: (i, 0)), out_shape=jax.ShapeDtypeStruct((m, n), bf16),
                          compiler_params=_params("parallel"))(a)


def _ln_stats(z):
    mu = jnp.mean(z, axis=-1, keepdims=True)
    zc = z - mu
    var = jnp.mean(zc * zc, axis=-1, keepdims=True)
    rstd = lax.rsqrt(var + NORM_EPS)
    return zc * rstd, rstd


def ln_fwd(z, g, b, *, name):
    s, d = z.shape
    tm = _tile(s, 256, 16)

    def body(z_ref, g_ref, b_ref, y_ref, yb_ref):
        xhat, _ = _ln_stats(z_ref[...])
        y = xhat * g_ref[...] + b_ref[...]
        y_ref[...] = y
        yb_ref[...] = y.astype(bf16)

    row = pl.BlockSpec((tm, d), lambda i: (i, 0))
    vec = pl.BlockSpec((1, d), lambda i: (0, 0))
    return pl.pallas_call(body, name=name, grid=(s // tm,), in_specs=[row, vec, vec], out_specs=[row, row],
                          out_shape=[jax.ShapeDtypeStruct((s, d), f32), jax.ShapeDtypeStruct((s, d), bf16)],
                          compiler_params=_params("parallel"))(z, g, b)


def _ln_bwd_tile(dy, z, g):
    xhat, rstd = _ln_stats(z)
    dxh = dy * g
    m1 = jnp.mean(dxh, axis=-1, keepdims=True)
    m2 = jnp.mean(dxh * xhat, axis=-1, keepdims=True)
    dz = rstd * (dxh - m1 - xhat * m2)
    return dz, jnp.sum(dy * xhat, axis=0, keepdims=True), jnp.sum(dy, axis=0, keepdims=True)


def ln_bwd(dy, z, g, *, name):
    s, d = z.shape
    tm = _tile(s, 256, 16)

    def body(dy_ref, z_ref, g_ref, dz_ref, dzb_ref, dg_ref, db_ref):
        @pl.when(pl.program_id(0) == 0)
        def _():
            dg_ref[...] = jnp.zeros_like(dg_ref)
            db_ref[...] = jnp.zeros_like(db_ref)

        dz, dg, db = _ln_bwd_tile(dy_ref[...], z_ref[...], g_ref[...])
        dz_ref[...] = dz
        dzb_ref[...] = dz.astype(bf16)
        dg_ref[...] += dg
        db_ref[...] += db

    row = pl.BlockSpec((tm, d), lambda i: (i, 0))
    vec = pl.BlockSpec((1, d), lambda i: (0, 0))
    return pl.pallas_call(body, name=name, grid=(s // tm,), in_specs=[row, row, vec], out_specs=[row, row, vec, vec],
                          out_shape=[jax.ShapeDtypeStruct((s, d), f32), jax.ShapeDtypeStruct((s, d), bf16),
                                     jax.ShapeDtypeStruct((1, d), f32), jax.ShapeDtypeStruct((1, d), f32)],
                          compiler_params=_params("arbitrary"))(dy, z, g)


def ln_loss(z, target, g, b, *, name):
    s, d = z.shape
    tm = _tile(s, 256, 16)
    nt = s // tm

    def body(z_ref, t_ref, g_ref, b_ref, dz_ref, dzb_ref, dg_ref, db_ref, loss_ref, lacc_ref):
        i = pl.program_id(0)

        @pl.when(i == 0)
        def _():
            dg_ref[...] = jnp.zeros_like(dg_ref)
            db_ref[...] = jnp.zeros_like(db_ref)
            lacc_ref[...] = jnp.zeros_like(lacc_ref)

        zv, gv = z_ref[...], g_ref[...]
        xhat, _ = _ln_stats(zv)
        err = xhat * gv + b_ref[...] - t_ref[...]
        lacc_ref[...] += jnp.sum(err * err, axis=0, keepdims=True)
        dz, dg, db = _ln_bwd_tile(err * (1.0 / d), zv, gv)
        dz_ref[...] = dz
        dzb_ref[...] = dz.astype(bf16)
        dg_ref[...] += dg
        db_ref[...] += db

        @pl.when(i == nt - 1)
        def _():
            loss_ref[...] = (0.5 / d) * jnp.sum(lacc_ref[...], axis=1, keepdims=True)

    row = pl.BlockSpec((tm, d), lambda i: (i, 0))
    vec = pl.BlockSpec((1, d), lambda i: (0, 0))
    return pl.pallas_call(body, name=name, grid=(nt,), in_specs=[row, row, vec, vec],
                          out_specs=[row, row, vec, vec, pl.BlockSpec((1, 1), lambda i: (0, 0))],
                          out_shape=[jax.ShapeDtypeStruct((s, d), f32), jax.ShapeDtypeStruct((s, d), bf16),
                                     jax.ShapeDtypeStruct((1, d), f32), jax.ShapeDtypeStruct((1, d), f32),
                                     jax.ShapeDtypeStruct((1, 1), f32)],
                          scratch_shapes=[pltpu.VMEM((1, d), f32)],
                          compiler_params=_params("arbitrary"))(z, target, g, b)


def merge_fwd(gates, ys, yg, *, name):
    s, d = ys.shape
    tm = _tile(s, 256, 16)

    def body(gt_ref, ys_ref, yg_ref, o_ref):
        o_ref[...] = (_sigmoid(gt_ref[:, :d]) * ys_ref[...] + _sigmoid(gt_ref[:, d:]) * yg_ref[...]).astype(bf16)

    row = pl.BlockSpec((tm, d), lambda i: (i, 0))
    return pl.pallas_call(body, name=name, grid=(s // tm,), in_specs=[pl.BlockSpec((tm, 2 * d), lambda i: (i, 0)), row, row],
                          out_specs=row, out_shape=jax.ShapeDtypeStruct((s, d), bf16),
                          compiler_params=_params("parallel"))(gates, ys, yg)


def merge_bwd(gates, ys, yg, dmix, *, name):
    s, d = ys.shape
    tm = _tile(s, 256, 16)

    def body(gt_ref, ys_ref, yg_ref, dm_ref, dys_ref, dyg_ref, dgt_ref):
        dm = dm_ref[...]
        sa, sb = _sigmoid(gt_ref[:, :d]), _sigmoid(gt_ref[:, d:])
        dys_ref[...] = (dm * sa).astype(bf16)
        dyg_ref[...] = (dm * sb).astype(bf16)
        dgt_ref[:, :d] = (dm * ys_ref[...] * sa * (1.0 - sa)).astype(bf16)
        dgt_ref[:, d:] = (dm * yg_ref[...] * sb * (1.0 - sb)).astype(bf16)

    row = pl.BlockSpec((tm, d), lambda i: (i, 0))
    wide = pl.BlockSpec((tm, 2 * d), lambda i: (i, 0))
    return pl.pallas_call(body, name=name, grid=(s // tm,), in_specs=[wide, row, row, row], out_specs=[row, row, wide],
                          out_shape=[jax.ShapeDtypeStruct((s, d), bf16), jax.ShapeDtypeStruct((s, d), bf16),
                                     jax.ShapeDtypeStruct((s, 2 * d), bf16)],
                          compiler_params=_params("parallel"))(gates, ys, yg, dmix)


def _shift_down(ext, j):
    return ext if j == 0 else pltpu.roll(ext, j, 0)


def _shift_up(ext, j):
    return ext if j == 0 else pltpu.roll(ext, ext.shape[0] - j, 0)


def _causal_conv(ext, w_ref, width):
    acc = None
    for j in range(width):
        term = w_ref[j:j + 1, :] * _shift_down(ext, width - 1 - j)
        acc = term if acc is None else acc + term
    return acc[HALO:]


def _conv_input_grad(dy_ext, w_ref, width, rows):
    acc = None
    for j in range(width):
        term = w_ref[j:j + 1, :] * _shift_up(dy_ext, width - 1 - j)
        acc = term if acc is None else acc + term
    return acc[:rows]


def _conv_weight_grad(dy, ext, width, rows):
    return [jnp.sum(dy * _shift_down(ext, width - 1 - j)[HALO:HALO + rows], axis=0, keepdims=True) for j in range(width)]


def _rows_to_block(rows, n_rows, cols):
    r = lax.broadcasted_iota(jnp.int32, (n_rows, cols), 0)
    out = jnp.zeros((n_rows, cols), f32)
    for j, v in enumerate(rows):
        out = out + jnp.where(r == j, v, 0.0)
    return out


def _silu_and_grad(v):
    sg = _sigmoid(v)
    return v * sg, sg * (1.0 + v * (1.0 - sg))


def ffn_act_fwd(hpre, cwb, *, name):
    s = hpre.shape[0]
    tm = _tile(s, 256, 16)
    hb = tm // HALO

    def body(hg_ref, hgp_ref, hu_ref, hup_ref, cg_ref, cu_ref, o_ref):
        first = pl.program_id(1) == 0

        def conv(h_ref, hp_ref, c_ref):
            prev = jnp.where(first, 0.0, hp_ref[...])
            return _causal_conv(jnp.concatenate([prev, h_ref[...]], axis=0), c_ref.at[0], 3) + c_ref[0, 3:4, :]

        g = conv(hg_ref, hgp_ref, cg_ref)
        u = conv(hu_ref, hup_ref, cu_ref)
        o_ref[...] = (g * _sigmoid(g) * u).astype(bf16)

    def tile(off):
        return pl.BlockSpec((tm, FF_PAD), lambda j, i: (i, j + off))

    def halo(off):
        return pl.BlockSpec((HALO, FF_PAD), lambda j, i: (jnp.maximum(i * hb - 1, 0), j + off))

    def taps(off):
        return pl.BlockSpec((1, 8, FF_PAD), lambda j, i: (j + off, 0, 0))

    return pl.pallas_call(body, name=name, grid=(4, s // tm),
                          in_specs=[tile(0), halo(0), tile(4), halo(4), taps(0), taps(4)],
                          out_specs=pl.BlockSpec((tm, FF_PAD), lambda j, i: (i, j)),
                          out_shape=jax.ShapeDtypeStruct((s, 4 * FF_PAD), bf16),
                          compiler_params=_params("parallel", "parallel"))(hpre, hpre, hpre, hpre, cwb, cwb)


def ffn_act_bwd(hpre, dact, cwb, *, name):
    s = hpre.shape[0]
    tm = _tile(s, 256, 16)
    hb = tm // HALO
    nt = s // tm
    last_hb = s // HALO - 1

    def body(hg_ref, hgp_ref, hgn_ref, hu_ref, hup_ref, hun_ref, d_ref, dn_ref, cg_ref, cu_ref,
             dhg_ref, dhu_ref, dcg_ref, dcu_ref):
        i = pl.program_id(1)
        first, last = i == 0, i == nt - 1

        @pl.when(first)
        def _():
            dcg_ref[...] = jnp.zeros_like(dcg_ref)
            dcu_ref[...] = jnp.zeros_like(dcu_ref)

        def ext_of(h_ref, hp_ref, hn_ref):
            return jnp.concatenate([jnp.where(first, 0.0, hp_ref[...]), h_ref[...], hn_ref[...]], axis=0)

        eg, eu = ext_of(hg_ref, hgp_ref, hgn_ref), ext_of(hu_ref, hup_ref, hun_ref)
        g = _causal_conv(eg, cg_ref.at[0], 3) + cg_ref[0, 3:4, :]
        u = _causal_conv(eu, cu_ref.at[0], 3) + cu_ref[0, 3:4, :]
        d = jnp.concatenate([d_ref[...], jnp.where(last, 0.0, dn_ref[...])], axis=0)
        act, dact_dg = _silu_and_grad(g)
        dg = d * u * dact_dg
        du = d * act
        dhg_ref[...] = _conv_input_grad(dg, cg_ref.at[0], 3, tm).astype(bf16)
        dhu_ref[...] = _conv_input_grad(du, cu_ref.at[0], 3, tm).astype(bf16)
        dgt, dut = dg[:tm], du[:tm]
        dcg_ref[0] += _rows_to_block(_conv_weight_grad(dgt, eg, 3, tm) + [jnp.sum(dgt, axis=0, keepdims=True)], 8, FF_PAD)
        dcu_ref[0] += _rows_to_block(_conv_weight_grad(dut, eu, 3, tm) + [jnp.sum(dut, axis=0, keepdims=True)], 8, FF_PAD)

    def tile(off):
        return pl.BlockSpec((tm, FF_PAD), lambda j, i: (i, j + off))

    def prev(off):
        return pl.BlockSpec((HALO, FF_PAD), lambda j, i: (jnp.maximum(i * hb - 1, 0), j + off))

    def nxt(off):
        return pl.BlockSpec((HALO, FF_PAD), lambda j, i: (jnp.minimum((i + 1) * hb, last_hb), j + off))

    def taps(off):
        return pl.BlockSpec((1, 8, FF_PAD), lambda j, i: (j + off, 0, 0))

    dhg, dhu, dcg, dcu = pl.pallas_call(
        body, name=name, grid=(4, nt),
        in_specs=[tile(0), prev(0), nxt(0), tile(4), prev(4), nxt(4), tile(0), nxt(0), taps(0), taps(4)],
        out_specs=[tile(0), tile(0), taps(0), taps(0)],
        out_shape=[jax.ShapeDtypeStruct((s, 4 * FF_PAD), bf16), jax.ShapeDtypeStruct((s, 4 * FF_PAD), bf16),
                   jax.ShapeDtypeStruct((4, 8, FF_PAD), f32), jax.ShapeDtypeStruct((4, 8, FF_PAD), f32)],
        compiler_params=_params("parallel", "arbitrary"),
    )(hpre, hpre, hpre, hpre, hpre, hpre, dact, dact, cwb, cwb)
    return jnp.concatenate([dhg, dhu], axis=1), jnp.concatenate([dcg, dcu], axis=0)


MEM_SCALE = MEM_HEAD_DIM ** -0.5


def _softmax_rows(sc):
    m = jnp.max(sc, axis=-1, keepdims=True)
    e = jnp.exp(sc - m)
    return e / jnp.sum(e, axis=-1, keepdims=True)


def memattn_fwd(qm, kv, *, name):
    s = qm.shape[0]
    mlen = kv.shape[0]
    tm = _tile(s, 512, 16)

    def body(q_ref, kv_ref, o_ref):
        for h in range(MEM_HEADS):
            lo = h * MEM_HEAD_DIM
            q = q_ref[:, lo:lo + MEM_HEAD_DIM]
            k = kv_ref[:, lo:lo + MEM_HEAD_DIM]
            v = kv_ref[:, MEM_W + lo:MEM_W + lo + MEM_HEAD_DIM]
            p = _softmax_rows(_d16(q, k, NT) * MEM_SCALE)
            o_ref[:, lo:lo + MEM_HEAD_DIM] = _d16(p, v, NN).astype(bf16)

    return pl.pallas_call(body, name=name, grid=(s // tm,),
                          in_specs=[pl.BlockSpec((tm, MEM_W), lambda i: (i, 0)), pl.BlockSpec((mlen, 2 * MEM_W), lambda i: (0, 0))],
                          out_specs=pl.BlockSpec((tm, MEM_W), lambda i: (i, 0)),
                          out_shape=jax.ShapeDtypeStruct((s, MEM_W), bf16), compiler_params=_params("parallel"))(qm, kv)


def memattn_bwd(qm, kv, dout, *, name):
    s = qm.shape[0]
    mlen = kv.shape[0]
    tm = _tile(s, 512, 16)

    def body(q_ref, kv_ref, do_ref, dq_ref, dkv_ref):
        @pl.when(pl.program_id(0) == 0)
        def _():
            dkv_ref[...] = jnp.zeros_like(dkv_ref)

        for h in range(MEM_HEADS):
            lo = h * MEM_HEAD_DIM
            q = q_ref[:, lo:lo + MEM_HEAD_DIM]
            k = kv_ref[:, lo:lo + MEM_HEAD_DIM]
            v = kv_ref[:, MEM_W + lo:MEM_W + lo + MEM_HEAD_DIM]
            do = do_ref[:, lo:lo + MEM_HEAD_DIM]
            p = _softmax_rows(_d16(q, k, NT) * MEM_SCALE)
            dp = _d16(do, v, NT)
            ds = p * (dp - jnp.sum(p * dp, axis=-1, keepdims=True)) * MEM_SCALE
            dq_ref[:, lo:lo + MEM_HEAD_DIM] = _d16(ds, k, NN).astype(bf16)
            dkv_ref[:, lo:lo + MEM_HEAD_DIM] += _d16(ds, q, TN)
            dkv_ref[:, MEM_W + lo:MEM_W + lo + MEM_HEAD_DIM] += _d16(p, do, TN)

    row = pl.BlockSpec((tm, MEM_W), lambda i: (i, 0))
    full = pl.BlockSpec((mlen, 2 * MEM_W), lambda i: (0, 0))
    return pl.pallas_call(body, name=name, grid=(s // tm,), in_specs=[row, full, row], out_specs=[row, full],
                          out_shape=[jax.ShapeDtypeStruct((s, MEM_W), bf16), jax.ShapeDtypeStruct((mlen, 2 * MEM_W), f32)],
                          compiler_params=_params("arbitrary"))(qm, kv, dout)


SWA_SCALE = SWA_HEAD_DIM ** -0.5
SWA_GROUP = SWA_HEADS // SWA_KV_HEADS
SWA_IN_W = 1408
K_COL, V_COL, BA_COL = SWA_Q // 128, SWA_Q // 128 + 1, SWA_Q // 128 + 2


def _swa_mask(n):
    qi = lax.broadcasted_iota(jnp.int32, (BLOCK, 2 * BLOCK), 0)
    kj = lax.broadcasted_iota(jnp.int32, (BLOCK, 2 * BLOCK), 1)
    dist = qi + BLOCK - kj
    return (dist >= 0) & (dist < BLOCK) & ((n > 0) | (kj >= BLOCK))


def _swa_group_probs(q, k, bias, sink, mask):
    heads = range(len(q))
    sc = [jnp.where(mask, _d16(q[h], k, NT) * SWA_SCALE + bias[h], NEG_INF) for h in heads]
    m = [jnp.maximum(jnp.max(sc[h], axis=-1, keepdims=True), sink[h]) for h in heads]
    e = [jnp.exp(sc[h] - m[h]) for h in heads]
    es = [jnp.exp(sink[h] - m[h]) for h in heads]
    inv = [1.0 / (jnp.sum(e[h], axis=-1, keepdims=True) + es[h]) for h in heads]
    return e, es, inv


def _swa_group_inputs(g, q_ref, bias_ref, sink_ref):
    hs = range(g * SWA_GROUP, (g + 1) * SWA_GROUP)
    return ([q_ref[:, h * SWA_HEAD_DIM:(h + 1) * SWA_HEAD_DIM] for h in hs], [bias_ref[h] for h in hs],
            [sink_ref[:, h:h + 1] for h in hs])


def _swa_specs():
    q_spec = pl.BlockSpec((BLOCK, SWA_Q), lambda n: (n, 0))

    def band(col):
        return [pl.BlockSpec((BLOCK, SWA_KV), lambda n: (jnp.maximum(n - 1, 0), col)),
                pl.BlockSpec((BLOCK, SWA_KV), lambda n: (n, col))]

    bias_spec = pl.BlockSpec((SWA_HEADS, BLOCK, 2 * BLOCK), lambda n: (0, 0, 0))
    sink_spec = pl.BlockSpec((1, SWA_HEADS), lambda n: (0, 0))
    return [q_spec] + band(K_COL) + band(V_COL) + [bias_spec, sink_spec]


def swa_fwd(swa_in, bias, sinks, *, name, side=None):
    s = swa_in.shape[0]

    def body(q_ref, kp_ref, kc_ref, vp_ref, vc_ref, bias_ref, sink_ref, o_ref):
        mask = _swa_mask(pl.program_id(0))
        kb = jnp.concatenate([kp_ref[...], kc_ref[...]], axis=0)
        vb = jnp.concatenate([vp_ref[...], vc_ref[...]], axis=0)
        for g in range(SWA_KV_HEADS):
            kl = g * SWA_HEAD_DIM
            q, bias_g, sink_g = _swa_group_inputs(g, q_ref, bias_ref, sink_ref)
            e, _, inv = _swa_group_probs(q, kb[:, kl:kl + SWA_HEAD_DIM], bias_g, sink_g, mask)
            v = vb[:, kl:kl + SWA_HEAD_DIM]
            outs = [_d16(e[h] * inv[h], v, NN) for h in range(SWA_GROUP)]
            for h in range(SWA_GROUP):
                lo = (g * SWA_GROUP + h) * SWA_HEAD_DIM
                o_ref[:, lo:lo + SWA_HEAD_DIM] = outs[h].astype(bf16)

    return _call(body, (swa_in, swa_in, swa_in, swa_in, swa_in, bias, sinks), name=name, grid=(s // BLOCK,),
                 in_specs=_swa_specs(), out_specs=pl.BlockSpec((BLOCK, SWA_Q), lambda n: (n, 0)),
                 out_shape=jax.ShapeDtypeStruct((s, SWA_Q), bf16), semantics=("parallel",), side=side)


def swa_bwd(swa_in, bias, sinks, dout, *, name, side=None):
    s = swa_in.shape[0]

    def body(q_ref, kp_ref, kc_ref, vp_ref, vc_ref, bias_ref, sink_ref, do_ref,
             dq_ref, dkc_ref, dkp_ref, dvc_ref, dvp_ref, dbias_ref, dsink_ref):
        n = pl.program_id(0)

        @pl.when(n == 0)
        def _():
            dbias_ref[...] = jnp.zeros_like(dbias_ref)
            dsink_ref[...] = jnp.zeros_like(dsink_ref)

        mask = _swa_mask(n)
        kb = jnp.concatenate([kp_ref[...], kc_ref[...]], axis=0)
        vb = jnp.concatenate([vp_ref[...], vc_ref[...]], axis=0)
        lane = lax.broadcasted_iota(jnp.int32, (1, 128), 1)
        dsink = jnp.zeros((1, 128), f32)
        for g in range(SWA_KV_HEADS):
            kl = g * SWA_HEAD_DIM
            k, v = kb[:, kl:kl + SWA_HEAD_DIM], vb[:, kl:kl + SWA_HEAD_DIM]
            hs = range(SWA_GROUP)
            q, bias_g, sink_g = _swa_group_inputs(g, q_ref, bias_ref, sink_ref)
            do = [do_ref[:, (g * SWA_GROUP + h) * SWA_HEAD_DIM:(g * SWA_GROUP + h + 1) * SWA_HEAD_DIM] for h in hs]
            e, es, inv = _swa_group_probs(q, k, bias_g, sink_g, mask)
            p = [e[h] * inv[h] for h in hs]
            dp = [_d16(do[h], v, NT) for h in hs]
            delta = [jnp.sum(p[h] * dp[h], axis=-1, keepdims=True) for h in hs]
            ds = [p[h] * (dp[h] - delta[h]) for h in hs]
            dss = [ds[h] * SWA_SCALE for h in hs]
            dq = [_d16(dss[h], k, NN) for h in hs]
            dks = [_d16(dss[h], q[h], TN) for h in hs]
            dvs = [_d16(p[h], do[h], TN) for h in hs]
            dk, dv = sum(dks[1:], dks[0]), sum(dvs[1:], dvs[0])
            for h in hs:
                hh = g * SWA_GROUP + h
                dbias_ref[hh] += ds[h]
                dq_ref[:, hh * SWA_HEAD_DIM:(hh + 1) * SWA_HEAD_DIM] = dq[h]
                dsink = dsink + jnp.where(lane == hh, -jnp.sum(es[h] * inv[h] * delta[h], axis=0, keepdims=True), 0.0)
            dkp_ref[:, kl:kl + SWA_HEAD_DIM] = dk[:BLOCK]
            dkc_ref[:, kl:kl + SWA_HEAD_DIM] = dk[BLOCK:]
            dvp_ref[:, kl:kl + SWA_HEAD_DIM] = dv[:BLOCK]
            dvc_ref[:, kl:kl + SWA_HEAD_DIM] = dv[BLOCK:]
        dsink_ref[...] += dsink

    qs = pl.BlockSpec((BLOCK, SWA_Q), lambda n: (n, 0))
    ks = pl.BlockSpec((BLOCK, SWA_KV), lambda n: (n, 0))
    return _call(
        body, (swa_in, swa_in, swa_in, swa_in, swa_in, bias, sinks, dout), name=name, grid=(s // BLOCK,),
        in_specs=_swa_specs() + [qs],
        out_specs=[qs, ks, ks, ks, ks, pl.BlockSpec((SWA_HEADS, BLOCK, 2 * BLOCK), lambda n: (0, 0, 0)),
                   pl.BlockSpec((1, 128), lambda n: (0, 0))],
        out_shape=[jax.ShapeDtypeStruct((s, SWA_Q), f32)] + [jax.ShapeDtypeStruct((s, SWA_KV), f32)] * 4
        + [jax.ShapeDtypeStruct((SWA_HEADS, BLOCK, 2 * BLOCK), f32), jax.ShapeDtypeStruct((1, 128), f32)],
        semantics=("arbitrary",), side=side)


def swa_in_grad(dq, dkc, dkp, dvc, dvp, dba, *, name):
    s = dq.shape[0]
    nb = s // BLOCK

    def body(dq_ref, dkc_ref, dkp_ref, dvc_ref, dvp_ref, dba_ref, o_ref):
        has_next = pl.program_id(0) < nb - 1
        o_ref[:, :SWA_Q] = dq_ref[...].astype(bf16)
        o_ref[:, SWA_Q:SWA_Q + SWA_KV] = (dkc_ref[...] + jnp.where(has_next, dkp_ref[...], 0.0)).astype(bf16)
        o_ref[:, SWA_Q + SWA_KV:SWA_Q + 2 * SWA_KV] = (dvc_ref[...] + jnp.where(has_next, dvp_ref[...], 0.0)).astype(bf16)
        o_ref[:, SWA_Q + 2 * SWA_KV:] = dba_ref[...].astype(bf16)

    cur = pl.BlockSpec((BLOCK, SWA_KV), lambda n: (n, 0))
    nxt = pl.BlockSpec((BLOCK, SWA_KV), lambda n: (jnp.minimum(n + 1, nb - 1), 0))
    return pl.pallas_call(body, name=name, grid=(nb,),
                          in_specs=[pl.BlockSpec((BLOCK, SWA_Q), lambda n: (n, 0)), cur, nxt, cur, nxt, cur],
                          out_specs=pl.BlockSpec((BLOCK, SWA_IN_W), lambda n: (n, 0)),
                          out_shape=jax.ShapeDtypeStruct((s, SWA_IN_W), bf16),
                          compiler_params=_params("parallel"))(dq, dkc, dkp, dvc, dvp, dba)


def _bucket_onehot():
    qi = jnp.arange(BLOCK)[:, None]
    kj = jnp.arange(2 * BLOCK)[None, :]
    dist = jnp.maximum(qi + BLOCK - kj, 0)
    max_exact = REL_BUCKETS // 2
    dd = jnp.maximum(dist, 1).astype(f32)
    large = max_exact + (jnp.log(dd / max_exact) / math.log(REL_MAX_DIST / max_exact) * (REL_BUCKETS - max_exact)).astype(jnp.int32)
    bucket = jnp.where(dist < max_exact, dist, jnp.minimum(large, REL_BUCKETS - 1)).reshape(-1)
    return (bucket[None, :] == jnp.arange(REL_BUCKETS)[:, None]).astype(f32)


def _gbeta_fn(ba, alog_row, dt_row):
    col = lax.broadcasted_iota(jnp.int32, ba.shape, 1)
    v = ba + dt_row
    softplus = jnp.maximum(v, 0.0) + jnp.log(1.0 + jnp.exp(-jnp.abs(v)))
    g = -jnp.exp(alog_row) * softplus
    return jnp.where(col < GDN_HEADS, _sigmoid(ba), jnp.where(col < 2 * GDN_HEADS, g, 0.0))


def gbeta_fwd(swa_in, alog_row, dt_row, *, name):
    s = swa_in.shape[0]
    tm = _tile(s, 512, 8)

    def body(ba_ref, a_ref, d_ref, o_ref):
        o_ref[...] = _gbeta_fn(ba_ref[...], a_ref[...], d_ref[...])

    vec = pl.BlockSpec((1, 128), lambda i: (0, 0))
    return pl.pallas_call(body, name=name, grid=(s // tm,), in_specs=[pl.BlockSpec((tm, 128), lambda i: (i, BA_COL)), vec, vec],
                          out_specs=pl.BlockSpec((tm, 128), lambda i: (i, 0)), out_shape=jax.ShapeDtypeStruct((s, 128), f32),
                          compiler_params=_params("parallel"))(swa_in, alog_row, dt_row)


def gbeta_bwd(swa_in, alog_row, dt_row, dgbeta, *, name):
    s = swa_in.shape[0]
    tm = _tile(s, 512, 8)

    def body(ba_ref, a_ref, d_ref, dgb_ref, dba_ref, da_ref, dd_ref):
        @pl.when(pl.program_id(0) == 0)
        def _():
            da_ref[...] = jnp.zeros_like(da_ref)
            dd_ref[...] = jnp.zeros_like(dd_ref)

        _, pull = jax.vjp(_gbeta_fn, ba_ref[...], a_ref[...], d_ref[...])
        dba, da, dd = pull(dgb_ref[...])
        dba_ref[...] = dba
        da_ref[...] += da
        dd_ref[...] += dd

    vec = pl.BlockSpec((1, 128), lambda i: (0, 0))
    row = pl.BlockSpec((tm, 128), lambda i: (i, 0))
    return pl.pallas_call(body, name=name, grid=(s // tm,),
                          in_specs=[pl.BlockSpec((tm, 128), lambda i: (i, BA_COL)), vec, vec, row], out_specs=[row, vec, vec],
                          out_shape=[jax.ShapeDtypeStruct((s, 128), f32), jax.ShapeDtypeStruct((1, 128), f32),
                                     jax.ShapeDtypeStruct((1, 128), f32)],
                          compiler_params=_params("arbitrary"))(swa_in, alog_row, dt_row, dgbeta)


QKV_W = 3 * GDN_W


def gdn_pre_fwd(gdn_in, convw, *, name):
    s = gdn_in.shape[0]
    tm = _tile(s, 256, 16)
    hb = tm // HALO

    def body(x_ref, xp_ref, w_ref, q_ref, k_ref, v_ref):
        prev = jnp.where(pl.program_id(0) == 0, 0.0, xp_ref[...])
        pre = _causal_conv(jnp.concatenate([prev, x_ref[...]], axis=0), w_ref, GDN_CONV)
        act = pre * _sigmoid(pre)
        for h in range(GDN_HEADS):
            lo = h * GDN_HEAD_DIM
            for off, o_ref in ((0, q_ref), (GDN_W, k_ref)):
                seg = act[:, off + lo:off + lo + GDN_HEAD_DIM]
                o_ref[:, lo:lo + GDN_HEAD_DIM] = seg * lax.rsqrt(jnp.sum(seg * seg, axis=-1, keepdims=True) + 1e-6)
        v_ref[...] = act[:, 2 * GDN_W:]

    out = pl.BlockSpec((tm, GDN_W), lambda i: (i, 0))
    return pl.pallas_call(body, name=name, grid=(s // tm,),
                          in_specs=[pl.BlockSpec((tm, QKV_W), lambda i: (i, 0)),
                                    pl.BlockSpec((HALO, QKV_W), lambda i: (jnp.maximum(i * hb - 1, 0), 0)),
                                    pl.BlockSpec((8, QKV_W), lambda i: (0, 0))],
                          out_specs=[out, out, out], out_shape=[jax.ShapeDtypeStruct((s, GDN_W), f32)] * 3,
                          compiler_params=_params("parallel"))(gdn_in, gdn_in, convw)


def gdn_pre_bwd(gdn_in, convw, dqn, dkn, dv, dgz, *, name):
    s = gdn_in.shape[0]
    tm = _tile(s, 128, 16)
    hb = tm // HALO
    nt = s // tm
    last_hb = s // HALO - 1

    def body(x_ref, xp_ref, xn_ref, w_ref, dq_ref, dqx_ref, dk_ref, dkx_ref, dv_ref, dvx_ref, dz_ref, o_ref, dw_ref):
        i = pl.program_id(0)
        first, last = i == 0, i == nt - 1

        @pl.when(first)
        def _():
            dw_ref[...] = jnp.zeros_like(dw_ref)

        ext = jnp.concatenate([jnp.where(first, 0.0, xp_ref[...]), x_ref[...], xn_ref[...]], axis=0)
        pre = _causal_conv(ext, w_ref, GDN_CONV)
        act, dact_dpre = _silu_and_grad(pre)

        def with_future(t_ref, n_ref):
            return jnp.concatenate([t_ref[...], jnp.where(last, 0.0, n_ref[...])], axis=0)

        dqe, dke, dve = with_future(dq_ref, dqx_ref), with_future(dk_ref, dkx_ref), with_future(dv_ref, dvx_ref)
        parts = []
        for off, dn in ((0, dqe), (GDN_W, dke)):
            for h in range(GDN_HEADS):
                lo = h * GDN_HEAD_DIM
                seg = act[:, off + lo:off + lo + GDN_HEAD_DIM]
                r = lax.rsqrt(jnp.sum(seg * seg, axis=-1, keepdims=True) + 1e-6)
                nrm = seg * r
                dseg = dn[:, lo:lo + GDN_HEAD_DIM]
                parts.append(r * (dseg - nrm * jnp.sum(dseg * nrm, axis=-1, keepdims=True)))
        dpre = jnp.concatenate(parts + [dve], axis=1) * dact_dpre
        o_ref[:, :QKV_W] = _conv_input_grad(dpre, w_ref, GDN_CONV, tm).astype(bf16)
        o_ref[:, QKV_W:] = dz_ref[...].astype(bf16)
        dw_ref[...] += _rows_to_block(_conv_weight_grad(dpre[:tm], ext, GDN_CONV, tm), 8, QKV_W)

    row = pl.BlockSpec((tm, GDN_W), lambda i: (i, 0))
    fut = pl.BlockSpec((HALO, GDN_W), lambda i: (jnp.minimum((i + 1) * hb, last_hb), 0))
    return pl.pallas_call(
        body, name=name, grid=(nt,),
        in_specs=[pl.BlockSpec((tm, QKV_W), lambda i: (i, 0)),
                  pl.BlockSpec((HALO, QKV_W), lambda i: (jnp.maximum(i * hb - 1, 0), 0)),
                  pl.BlockSpec((HALO, QKV_W), lambda i: (jnp.minimum((i + 1) * hb, last_hb), 0)),
                  pl.BlockSpec((8, QKV_W), lambda i: (0, 0)), row, fut, row, fut, row, fut, row],
        out_specs=[pl.BlockSpec((tm, 4 * GDN_W), lambda i: (i, 0)), pl.BlockSpec((8, QKV_W), lambda i: (0, 0))],
        out_shape=[jax.ShapeDtypeStruct((s, 4 * GDN_W), bf16), jax.ShapeDtypeStruct((8, QKV_W), f32)],
        compiler_params=_params("arbitrary"),
    )(gdn_in, gdn_in, gdn_in, convw, dqn, dqn, dkn, dkn, dv, dv, dgz)


def _gdn_post_head(o, z, nw):
    return o * lax.rsqrt(jnp.mean(o * o, axis=-1, keepdims=True) + 1e-6) * nw * (z * _sigmoid(z))


def gdn_post_fwd(o, gdn_in, nw, *, name):
    s = o.shape[0]
    tm = _tile(s, 256, 16)

    def body(o_ref, z_ref, nw_ref, y_ref):
        for h in range(GDN_HEADS):
            sl = slice(h * GDN_HEAD_DIM, (h + 1) * GDN_HEAD_DIM)
            y_ref[:, sl] = _gdn_post_head(o_ref[:, sl], z_ref[:, sl], nw_ref[...]).astype(bf16)

    row = pl.BlockSpec((tm, GDN_W), lambda i: (i, 0))
    return pl.pallas_call(body, name=name, grid=(s // tm,),
                          in_specs=[row, pl.BlockSpec((tm, GDN_W), lambda i: (i, 3)), pl.BlockSpec((1, 128), lambda i: (0, 0))],
                          out_specs=row, out_shape=jax.ShapeDtypeStruct((s, GDN_W), bf16),
                          compiler_params=_params("parallel"))(o, gdn_in, nw)


def gdn_post_bwd(o, gdn_in, nw, dy, *, name):
    s = o.shape[0]
    tm = _tile(s, 256, 16)

    def body(o_ref, z_ref, nw_ref, dy_ref, do_ref, dz_ref, dnw_ref):
        @pl.when(pl.program_id(0) == 0)
        def _():
            dnw_ref[...] = jnp.zeros_like(dnw_ref)

        dnw = jnp.zeros((1, 128), f32)
        for h in range(GDN_HEADS):
            sl = slice(h * GDN_HEAD_DIM, (h + 1) * GDN_HEAD_DIM)
            _, pull = jax.vjp(_gdn_post_head, o_ref[:, sl], z_ref[:, sl], nw_ref[...])
            do, dz, dn = pull(dy_ref[:, sl])
            do_ref[:, sl] = do
            dz_ref[:, sl] = dz
            dnw = dnw + dn
        dnw_ref[...] += dnw

    row = pl.BlockSpec((tm, GDN_W), lambda i: (i, 0))
    vec = pl.BlockSpec((1, 128), lambda i: (0, 0))
    return pl.pallas_call(body, name=name, grid=(s // tm,),
                          in_specs=[row, pl.BlockSpec((tm, GDN_W), lambda i: (i, 3)), vec, row], out_specs=[row, row, vec],
                          out_shape=[jax.ShapeDtypeStruct((s, GDN_W), f32), jax.ShapeDtypeStruct((s, GDN_W), f32),
                                     jax.ShapeDtypeStruct((1, 128), f32)],
                          compiler_params=_params("arbitrary"))(o, gdn_in, nw, dy)


def _gdn_chunks(q, k, v, gb, state):
    c = GDN_CHUNK
    heads = range(len(q))
    r = lax.broadcasted_iota(jnp.int32, (c, c), 0)
    cc = lax.broadcasted_iota(jnp.int32, (c, c), 1)
    tril, strict = r >= cc, r > cc
    eye = (r == cc).astype(f32)

    def dhi(a, b):
        return jnp.dot(a, b, precision=lax.Precision.HIGH, preferred_element_type=f32)

    beta = [gb[:, h:h + 1] for h in heads]
    cum_cols = dhi(tril.astype(f32), gb)
    cum_rows = dhi(gb.T, (r <= cc).astype(f32))
    gi = [jnp.broadcast_to(cum_cols[:, GDN_HEADS + h:GDN_HEADS + h + 1], (c, c)) for h in heads]
    gj = [jnp.broadcast_to(cum_rows[GDN_HEADS + h:GDN_HEADS + h + 1, :], (c, c)) for h in heads]
    decay = [jnp.where(tril, jnp.exp(jnp.where(tril, gi[h] - gj[h], 0.0)), 0.0) for h in heads]
    kb = [k[h] * beta[h] for h in heads]
    vb = [v[h] * beta[h] for h in heads]
    a = [jnp.where(strict, _d16(kb[h], k[h], NT) * decay[h], 0.0) for h in heads]
    tinv = [eye - a[h] for h in heads]
    pw = [dhi(a[h], a[h]) for h in heads]
    for it in range(5):
        tinv = [tinv[h] + dhi(tinv[h], pw[h]) for h in heads]
        if it < 4:
            pw = [dhi(pw[h], pw[h]) for h in heads]
    gc = [gi[h][:, 0:1] for h in heads]
    egc = [jnp.exp(gc[h]) for h in heads]
    u = [dhi(tinv[h], vb[h]) for h in heads]
    w = [dhi(tinv[h], kb[h] * egc[h]) for h in heads]
    qs = [q[h] * (GDN_HEAD_DIM ** -0.5) for h in heads]
    attn = [jnp.where(tril, _d16(qs[h], k[h], NT) * decay[h], 0.0) for h in heads]
    g_last = [gi[h][c - 1:c, 0:1] for h in heads]
    v_new = [u[h] - _d16(w[h], state[h], NN) for h in heads]
    out = [_d16(qs[h] * egc[h], state[h], NN) + _d16(attn[h], v_new[h], NN) for h in heads]
    new_state = [state[h] * jnp.exp(g_last[h]) + _d16(k[h] * jnp.exp(g_last[h] - gc[h]), v_new[h], TN) for h in heads]
    return out, new_state


def _head_cols(ref):
    return [ref[:, h * GDN_HEAD_DIM:(h + 1) * GDN_HEAD_DIM] for h in range(GDN_HEADS)]


def gdn_scan_fwd(qn, kn, v, gbeta, *, name, side=None):
    s = qn.shape[0]
    nc = s // GDN_CHUNK

    def body(q_ref, k_ref, v_ref, gb_ref, o_ref, st_ref, state_ref):
        @pl.when(pl.program_id(0) == 0)
        def _():
            state_ref[...] = jnp.zeros_like(state_ref)

        states = [state_ref[h] for h in range(GDN_HEADS)]
        outs, new = _gdn_chunks(_head_cols(q_ref), _head_cols(k_ref), _head_cols(v_ref), gb_ref[...], states)
        for h in range(GDN_HEADS):
            st_ref[0, h] = states[h]
            o_ref[:, h * GDN_HEAD_DIM:(h + 1) * GDN_HEAD_DIM] = outs[h]
            state_ref[h] = new[h]

    row = pl.BlockSpec((GDN_CHUNK, GDN_W), lambda n: (n, 0))
    return _call(
        body, (qn, kn, v, gbeta), name=name, grid=(nc,),
        in_specs=[row, row, row, pl.BlockSpec((GDN_CHUNK, 128), lambda n: (n, 0))],
        out_specs=[row, pl.BlockSpec((1, GDN_HEADS, GDN_HEAD_DIM, GDN_HEAD_DIM), lambda n: (n, 0, 0, 0))],
        out_shape=[jax.ShapeDtypeStruct((s, GDN_W), f32),
                   jax.ShapeDtypeStruct((nc, GDN_HEADS, GDN_HEAD_DIM, GDN_HEAD_DIM), f32)],
        scratch_shapes=[pltpu.VMEM((GDN_HEADS, GDN_HEAD_DIM, GDN_HEAD_DIM), f32)], semantics=("arbitrary",), side=side)


def gdn_scan_bwd(qn, kn, v, gbeta, states, dout, *, name, side=None):
    s = qn.shape[0]
    nc = s // GDN_CHUNK

    def body(q_ref, k_ref, v_ref, gb_ref, st_ref, do_ref, dq_ref, dk_ref, dv_ref, dgb_ref, dstate_ref):
        @pl.when(pl.program_id(0) == 0)
        def _():
            dstate_ref[...] = jnp.zeros_like(dstate_ref)

        _, pull = jax.vjp(_gdn_chunks, _head_cols(q_ref), _head_cols(k_ref), _head_cols(v_ref), gb_ref[...],
                          [st_ref[0, h] for h in range(GDN_HEADS)])
        dq, dk, dv, dgb, dst = pull((_head_cols(do_ref), [dstate_ref[h] for h in range(GDN_HEADS)]))
        for h in range(GDN_HEADS):
            sl = slice(h * GDN_HEAD_DIM, (h + 1) * GDN_HEAD_DIM)
            dq_ref[:, sl] = dq[h]
            dk_ref[:, sl] = dk[h]
            dv_ref[:, sl] = dv[h]
            dstate_ref[h] = dst[h]
        dgb_ref[...] = dgb

    row = pl.BlockSpec((GDN_CHUNK, GDN_W), lambda n: (nc - 1 - n, 0))
    gb = pl.BlockSpec((GDN_CHUNK, 128), lambda n: (nc - 1 - n, 0))
    return _call(
        body, (qn, kn, v, gbeta, states, dout), name=name, grid=(nc,),
        in_specs=[row, row, row, gb, pl.BlockSpec((1, GDN_HEADS, GDN_HEAD_DIM, GDN_HEAD_DIM), lambda n: (nc - 1 - n, 0, 0, 0)), row],
        out_specs=[row, row, row, gb],
        out_shape=[jax.ShapeDtypeStruct((s, GDN_W), f32)] * 3 + [jax.ShapeDtypeStruct((s, 128), f32)],
        scratch_shapes=[pltpu.VMEM((GDN_HEADS, GDN_HEAD_DIM, GDN_HEAD_DIM), f32)], semantics=("arbitrary",), side=side)


def adamw(w, g, m, v, *, name):
    r, c = w.shape
    tr = _tile(r, 256, 8)

    def body(w_ref, g_ref, m_ref, v_ref, d_ref, nm_ref, nv_ref):
        gv = g_ref[...]
        nm = ADAM_B1 * m_ref[...] + (1.0 - ADAM_B1) * gv
        nv = ADAM_B2 * v_ref[...] + (1.0 - ADAM_B2) * (gv * gv)
        m_hat = nm / (1.0 - ADAM_B1 ** ADAM_STEP)
        v_hat = nv / (1.0 - ADAM_B2 ** ADAM_STEP)
        d_ref[...] = -ADAM_LR * (m_hat / (jnp.sqrt(v_hat) + ADAM_EPS) + ADAM_WD * w_ref[...])
        nm_ref[...] = nm
        nv_ref[...] = nv

    spec = pl.BlockSpec((tr, c), lambda i: (i, 0))
    return pl.pallas_call(body, name=name, grid=(r // tr,), in_specs=[spec] * 4, out_specs=[spec] * 3,
                          out_shape=[jax.ShapeDtypeStruct((r, c), f32)] * 3, compiler_params=_params("parallel"))(w, g, m, v)


def _pos():
    return lax.axis_index("x"), lax.axis_index("y"), lax.axis_index("c")


ANY = pl.BlockSpec(memory_space=pl.ANY)


class Side(NamedTuple):
    ins: list
    outs: list
    aliases: dict
    sems: list
    start: Callable
    wait: Callable


def join_sides(*sides):
    def spans(key):
        out, off = [], 0
        for sd in sides:
            out.append(slice(off, off + len(getattr(sd, key))))
            off += len(getattr(sd, key))
        return out

    i_sp, o_sp, s_sp = spans("ins"), spans("outs"), spans("sems")
    aliases = {i_sp[n].start + i: o_sp[n].start + o for n, sd in enumerate(sides) for i, o in sd.aliases.items()}

    def each(what):
        def run(ins, outs, sems):
            for n, sd in enumerate(sides):
                getattr(sd, what)(ins[i_sp[n]], outs[o_sp[n]], sems[s_sp[n]])
        return run

    return Side([a for sd in sides for a in sd.ins], [o for sd in sides for o in sd.outs], aliases,
                [s for sd in sides for s in sd.sems], each("start"), each("wait"))


def _side_body(body, side, n_in, n_out, n_scratch, grid):
    ns_in, ns_out = len(side.ins), len(side.outs)

    def wrapped(*refs):
        cut = [n_in, ns_in, n_out, ns_out, n_scratch]
        parts, off = [], 0
        for c in cut:
            parts.append(refs[off:off + c])
            off += c
        ins, s_ins, outs, s_outs, scratch = parts
        sems = refs[off:]
        if grid:
            ids = [pl.program_id(d) for d in range(len(grid))]
            first = functools.reduce(jnp.logical_and, [i == 0 for i in ids])
            last = functools.reduce(jnp.logical_and, [i == g - 1 for i, g in zip(ids, grid)])
            pl.when(first)(lambda: side.start(s_ins, s_outs, sems))
            body(*ins, *outs, *scratch)
            pl.when(last)(lambda: side.wait(s_ins, s_outs, sems))
        else:
            side.start(s_ins, s_outs, sems)
            side.wait(s_ins, s_outs, sems)

    return wrapped


def _call(body, args, *, name, grid, in_specs, out_specs, out_shape, semantics, scratch_shapes=(), side=None):
    if side is None:
        return pl.pallas_call(body, name=name, grid=grid, in_specs=in_specs, out_specs=out_specs, out_shape=out_shape,
                              scratch_shapes=list(scratch_shapes), compiler_params=_params(*semantics))(*args)
    single = not isinstance(out_shape, (list, tuple))
    shapes, specs = ([out_shape], [out_specs]) if single else (list(out_shape), list(out_specs))
    n_in, n_out = len(in_specs), len(shapes)
    res = pl.pallas_call(
        _side_body(body, side, n_in, n_out, len(scratch_shapes), grid), name=name, grid=grid,
        in_specs=list(in_specs) + [ANY] * len(side.ins), out_specs=specs + [ANY] * len(side.outs),
        out_shape=shapes + list(side.outs), scratch_shapes=list(scratch_shapes) + list(side.sems),
        input_output_aliases={n_in + i: n_out + o for i, o in side.aliases.items()},
        compiler_params=_params(*(["arbitrary"] * len(grid))),
    )(*args, *side.ins)
    return (res[0] if single else res[:n_out]), list(res[n_out:])


def run_side(side, *, name):
    return pl.pallas_call(_side_body(None, side, 0, 0, 0, ()), name=name, in_specs=[ANY] * len(side.ins),
                          out_specs=[ANY] * len(side.outs), out_shape=list(side.outs), scratch_shapes=list(side.sems),
                          input_output_aliases=dict(side.aliases))(*side.ins)


def _remote(src, dst, send, recv, k, to):
    return pltpu.make_async_remote_copy(src_ref=src, dst_ref=dst, send_sem=send.at[k], recv_sem=recv.at[k], device_id=to,
                                        device_id_type=MESH)


def gather_first(shards):
    na = len(shards)

    def copies(x_refs, out_refs, sems):
        send, recv, local = sems
        x, y, cc = _pos()
        me = 4 * x + 2 * y + cc
        peers = [(x, y, 1 - cc), (1 - x, y, cc), (x, 1 - y, cc), (1 - x, 1 - y, cc)]
        mine = [pltpu.make_async_copy(x_refs[a], out_refs[a].at[me], local.at[a]) for a in range(na)]
        sent = [_remote(x_refs[a], out_refs[a].at[me], send, recv, 4 * a + k, p) for a in range(na) for k, p in enumerate(peers)]
        landed = [_remote(x_refs[a], out_refs[a].at[4 * p[0] + 2 * p[1] + p[2]], send, recv, 4 * a + k, p)
                  for a in range(na) for k, p in enumerate(peers)]
        return mine, sent, landed

    def start(x_refs, out_refs, sems):
        mine, sent, _ = copies(x_refs, out_refs, sems)
        for cp in mine + sent:
            cp.start()

    def wait(x_refs, out_refs, sems):
        mine, sent, landed = copies(x_refs, out_refs, sems)
        for cp in sent:
            cp.wait_send()
        for cp in landed:
            cp.wait_recv()
        for cp in mine:
            cp.wait()

    return Side(list(shards), [jax.ShapeDtypeStruct((N_DEV,) + s.shape, s.dtype) for s in shards], {},
                [pltpu.SemaphoreType.DMA((4 * na,)), pltpu.SemaphoreType.DMA((4 * na,)), pltpu.SemaphoreType.DMA((na,))],
                start, wait)


def gather_second(slots):
    na = len(slots)

    def copies(out_refs, sems):
        send, recv = sems
        x, y, cc = _pos()
        chips = [(1 - x, y), (x, 1 - y), (1 - x, 1 - y)]
        sent, landed = [], []
        for a in range(na):
            for j, (px, py) in enumerate(chips):
                row = out_refs[a].at[4 * px + 2 * py + cc]
                sent.append(_remote(row, row, send, recv, 3 * a + j, (x, y, 1 - cc)))
                landed.append(_remote(row, out_refs[a].at[4 * px + 2 * py + 1 - cc], send, recv, 3 * a + j, (x, y, 1 - cc)))
        return sent, landed

    def start(_, out_refs, sems):
        for cp in copies(out_refs, sems)[0]:
            cp.start()

    def wait(_, out_refs, sems):
        sent, landed = copies(out_refs, sems)
        for cp in sent:
            cp.wait_send()
        for cp in landed:
            cp.wait_recv()

    return Side(list(slots), [jax.ShapeDtypeStruct(s.shape, s.dtype) for s in slots], {a: a for a in range(na)},
                [pltpu.SemaphoreType.DMA((3 * na,)), pltpu.SemaphoreType.DMA((3 * na,))], start, wait)


def grad_to_sibling(chunks):
    na = len(chunks)

    def start(g_refs, out_refs, sems):
        send, recv = sems
        x, y, cc = _pos()
        for a in range(na):
            for q in range(4):
                _remote(g_refs[a].at[2 * q + 1 - cc], out_refs[a].at[q], send, recv, a, (x, y, 1 - cc)).start()

    def wait(g_refs, out_refs, sems):
        send, recv = sems
        x, y, cc = _pos()
        for a in range(na):
            _remote(out_refs[a], out_refs[a], send, recv, a, (x, y, 1 - cc)).wait()

    return Side(list(chunks), [jax.ShapeDtypeStruct((4,) + g.shape[1:], g.dtype) for g in chunks], {},
                [pltpu.SemaphoreType.DMA((na,)), pltpu.SemaphoreType.DMA((na,))], start, wait)


def grad_to_chips(parts):
    na = len(parts)

    def copies(p_refs, out_refs, sems):
        send, recv, local = sems
        x, y, cc = _pos()
        chips = [(1 - x, y), (x, 1 - y), (1 - x, 1 - y)]
        mine = [pltpu.make_async_copy(p_refs[a].at[2 * x + y], out_refs[a].at[3], local.at[a]) for a in range(na)]
        sent = [_remote(p_refs[a].at[2 * px + py], out_refs[a].at[k], send, recv, 3 * a + k, (px, py, cc))
                for a in range(na) for k, (px, py) in enumerate(chips)]
        return mine, sent

    def start(p_refs, out_refs, sems):
        mine, sent = copies(p_refs, out_refs, sems)
        for cp in mine + sent:
            cp.start()

    def wait(p_refs, out_refs, sems):
        mine, sent = copies(p_refs, out_refs, sems)
        for cp in sent:
            cp.wait()
        for cp in mine:
            cp.wait()

    return Side(list(parts), [jax.ShapeDtypeStruct(p.shape, p.dtype) for p in parts], {},
                [pltpu.SemaphoreType.DMA((3 * na,)), pltpu.SemaphoreType.DMA((3 * na,)), pltpu.SemaphoreType.DMA((na,))],
                start, wait)


def add_sibling(chunks, recv, *, name):
    _, r, c = chunks.shape
    tr = r if r <= 1024 else _tile(r, 512, 16)
    core = lax.axis_index("c").astype(jnp.int32).reshape(1)

    def body(core_ref, a_ref, b_ref, o_ref):
        o_ref[...] = (a_ref[...] + b_ref[...]).astype(bf16)

    return pl.pallas_call(
        body, name=name,
        grid_spec=pltpu.PrefetchScalarGridSpec(
            num_scalar_prefetch=1, grid=(4, r // tr),
            in_specs=[pl.BlockSpec((1, tr, c), lambda q, i, core_ref: (2 * q + core_ref[0], i, 0)),
                      pl.BlockSpec((1, tr, c), lambda q, i, core_ref: (q, i, 0))],
            out_specs=pl.BlockSpec((1, tr, c), lambda q, i, core_ref: (q, i, 0))),
        out_shape=jax.ShapeDtypeStruct((4, r, c), bf16), compiler_params=_params("parallel", "parallel"),
    )(core, chunks, recv)


def add_four(r4, *, name):
    _, r, c = r4.shape
    tr = r if r <= 1024 else _tile(r, 512, 16)

    def body(a_ref, o_ref):
        o_ref[...] = ((a_ref[3].astype(f32) + a_ref[0].astype(f32)) + a_ref[1].astype(f32)) + a_ref[2].astype(f32)

    return pl.pallas_call(body, name=name, grid=(r // tr,), in_specs=[pl.BlockSpec((4, tr, c), lambda i: (0, i, 0))],
                          out_specs=pl.BlockSpec((tr, c), lambda i: (i, 0)), out_shape=jax.ShapeDtypeStruct((r, c), f32),
                          compiler_params=_params("parallel"))(r4)


def all_reduce_small(vec, *, name):
    r, c = vec.shape

    def body(v_ref, out_ref, buf_ref, send_sems, recv_sems):
        x, y, cc = _pos()
        my_id = 4 * x + 2 * y + cc
        buf_ref[my_id] = v_ref[...]
        flips = [(fx, fy, fc) for fx in (0, 1) for fy in (0, 1) for fc in (0, 1)][1:]
        cps = []
        for k, (fx, fy, fc) in enumerate(flips):
            peer = ((1 - x) if fx else x, (1 - y) if fy else y, (1 - cc) if fc else cc)
            cps.append(pltpu.make_async_remote_copy(src_ref=v_ref, dst_ref=buf_ref.at[my_id], send_sem=send_sems.at[k],
                                                    recv_sem=recv_sems.at[k], device_id=peer, device_id_type=MESH))
        for cp in cps:
            cp.start()
        for cp in cps:
            cp.wait()
        acc = buf_ref[0]
        for d in range(1, N_DEV):
            acc = acc + buf_ref[d]
        out_ref[...] = acc

    vm = pl.BlockSpec(memory_space=pltpu.VMEM)
    return pl.pallas_call(body, name=name, in_specs=[vm], out_specs=vm, out_shape=jax.ShapeDtypeStruct((r, c), f32),
                          scratch_shapes=[pltpu.VMEM((N_DEV, r, c), f32), pltpu.SemaphoreType.DMA((7,)),
                                          pltpu.SemaphoreType.DMA((7,))])(vec)


def _pack(parts, rows, dtype):
    flat = jnp.concatenate([p.reshape(-1).astype(dtype) for p in parts])
    return jnp.pad(flat, (0, rows * PACK_COLS - flat.shape[0])).reshape(rows, PACK_COLS)


def _unpack(flat, shapes):
    out, off = [], 0
    for shp in shapes:
        n = shp[0] * shp[1]
        out.append(flat[..., off:off + n].reshape(flat.shape[:-1] + tuple(shp)))
        off += n
    return out


def _from_column_shards(g):
    _, r, c = g.shape
    return jnp.transpose(g, (1, 0, 2)).reshape(r, N_DEV * c)


def _column_shards(full):
    r, c8 = full.shape
    return jnp.transpose(full.reshape(r, N_DEV, c8 // N_DEV), (1, 0, 2))


W_IN_SHARD = IN_DIM // N_DEV
W_IN_PAD = 1280
W_IN_PARTS = (("swa", 0, 0, 1280), ("swa", 1280, 5376, 5392), ("gdn", 0, 1280, 5376), ("gates", 0, 5392, IN_DIM))
W_IN_WIDTHS = {"swa": SWA_IN_W, "gdn": 4 * GDN_W, "gates": 2 * D_MODEL}


def _w_in_segments():
    segs = []
    for part, p0, g0, g1 in W_IN_PARTS:
        for j in range(N_DEV):
            lo, hi = max(g0, W_IN_SHARD * j), min(g1, W_IN_SHARD * (j + 1))
            if lo < hi:
                segs.append((part, p0 + lo - g0, j, lo - W_IN_SHARD * j, hi - lo))
    return segs


def split_w_in(shards, *, name):
    dt = shards.dtype
    tm = 256

    def body(w_ref, swa_ref, gdn_ref, gates_ref):
        out = {"swa": swa_ref, "gdn": gdn_ref, "gates": gates_ref}
        swa_ref[:, SWA_Q + 2 * SWA_KV + 2 * GDN_HEADS:] = jnp.zeros((tm, SWA_IN_W - SWA_Q - 2 * SWA_KV - 2 * GDN_HEADS), dt)
        for part, p0, j, l0, n in _w_in_segments():
            out[part][:, p0:p0 + n] = w_ref[j, :, l0:l0 + n]

    return pl.pallas_call(body, name=name, grid=(D_MODEL // tm,),
                          in_specs=[pl.BlockSpec((N_DEV, tm, W_IN_PAD), lambda i: (0, i, 0))],
                          out_specs=[pl.BlockSpec((tm, W_IN_WIDTHS[p]), lambda i: (i, 0)) for p in ("swa", "gdn", "gates")],
                          out_shape=[jax.ShapeDtypeStruct((D_MODEL, W_IN_WIDTHS[p]), dt) for p in ("swa", "gdn", "gates")],
                          compiler_params=_params("parallel"))(shards)


def merge_w_in_grad(d_swa, d_gdn, d_gates, *, name):
    tm = 256

    def body(swa_ref, gdn_ref, gates_ref, w_ref):
        src = {"swa": swa_ref, "gdn": gdn_ref, "gates": gates_ref}
        w_ref[:, :, W_IN_SHARD:] = jnp.zeros((N_DEV, tm, W_IN_PAD - W_IN_SHARD), f32)
        for part, p0, j, l0, n in _w_in_segments():
            w_ref[j, :, l0:l0 + n] = src[part][:, p0:p0 + n]

    return pl.pallas_call(body, name=name, grid=(D_MODEL // tm,),
                          in_specs=[pl.BlockSpec((tm, W_IN_WIDTHS[p]), lambda i: (i, 0)) for p in ("swa", "gdn", "gates")],
                          out_specs=pl.BlockSpec((N_DEV, tm, W_IN_PAD), lambda i: (0, i, 0)),
                          out_shape=jax.ShapeDtypeStruct((N_DEV, D_MODEL, W_IN_PAD), f32),
                          compiler_params=_params("parallel"))(d_swa, d_gdn, d_gates)


def kernel(x, mem, w_in, rel_bias, swa_sinks, gdn_conv_w, gdn_a_log, gdn_dt_bias, gdn_norm_w, w_br_swa, w_br_gdn, w_mix_o, ln1_g, ln1_b, w_mem_q, w_mem_kv, w_mem_o, ln2_g, ln2_b, w_up, ffn_conv_w, ffn_conv_b, w_down, ln3_g, ln3_b, loss_target, m_w_in, m_rel_bias, m_swa_sinks, m_gdn_conv_w, m_gdn_a_log, m_gdn_dt_bias, m_gdn_norm_w, m_w_br_swa, m_w_br_gdn, m_w_mix_o, m_ln1_g, m_ln1_b, m_w_mem_q, m_w_mem_kv, m_w_mem_o, m_ln2_g, m_ln2_b, m_w_up, m_ffn_conv_w, m_ffn_conv_b, m_w_down, m_ln3_g, m_ln3_b, v_w_in, v_rel_bias, v_swa_sinks, v_gdn_conv_w, v_gdn_a_log, v_gdn_dt_bias, v_gdn_norm_w, v_w_br_swa, v_w_br_gdn, v_w_mix_o, v_ln1_g, v_ln1_b, v_w_mem_q, v_w_mem_kv, v_w_mem_o, v_ln2_g, v_ln2_b, v_w_up, v_ffn_conv_w, v_ffn_conv_b, v_w_down, v_ln3_g, v_ln3_b):
    env = dict(locals())
    w2 = {n: (env[n][0] if env[n].ndim == 3 else env[n]) for n in WEIGHTS}
    m2 = {n: (env["m_" + n][0] if env["m_" + n].ndim == 3 else env["m_" + n]) for n in WEIGHTS}
    v2 = {n: (env["v_" + n][0] if env["v_" + n].ndim == 3 else env["v_" + n]) for n in WEIGHTS}
    xs, mems, target = x[0], mem[0], loss_target[0]
    my_id = 4 * lax.axis_index("x") + 2 * lax.axis_index("y") + lax.axis_index("c")
    pad_ff = FF_PAD - FF_SHARD

    pad_cols = {"w_in": W_IN_PAD - W_IN_SHARD, "w_up": pad_ff}
    mid = ("w_br_swa", "w_br_gdn", "w_mem_o", "w_mix_o", "w_mem_q", "w_mem_kv")
    mine = {n: jnp.pad(w2[n], ((0, 0), (0, pad_cols.get(n, 0)))).astype(bf16) for n in ("w_in", "w_up", "w_down") + mid}
    got_in = run_side(gather_second(run_side(gather_first([mine["w_in"]]), name="gather_w_in")), name="gather_w_in_pass_on")
    w_swa, w_gdn, w_gates = split_w_in(got_in[0], name="split_w_in")
    n_ffn, n_gdn = 3 * FF_SHARD, GDN_CONV * (QKV_W // N_DEV)
    conv_mine = jnp.concatenate([w2["ffn_conv_w"].reshape(-1), w2["gdn_conv_w"].reshape(-1)])[None]
    conv_rows = lax.dynamic_update_slice(jnp.zeros((N_DEV, n_ffn + n_gdn), f32), conv_mine, (my_id, 0))
    conv_all = all_reduce_small(_pack([conv_rows], CONV_ROWS, f32), name="gather_conv_w")
    conv_all = conv_all.reshape(-1)[:N_DEV * (n_ffn + n_gdn)].reshape(N_DEV, n_ffn + n_gdn)
    cwb = jnp.concatenate([conv_all[:, :n_ffn].reshape(N_DEV, 3, FF_SHARD), w2["ffn_conv_b"].reshape(N_DEV, 1, FF_SHARD),
                           jnp.zeros((N_DEV, 4, FF_SHARD), f32)], axis=1)
    cwb = jnp.pad(cwb, ((0, 0), (0, 0), (0, pad_ff)))
    convw = jnp.transpose(conv_all[:, n_ffn:].reshape(N_DEV, GDN_CONV, QKV_W // N_DEV), (1, 0, 2)).reshape(GDN_CONV, QKV_W)
    convw = jnp.pad(convw, ((0, 4), (0, 0)))
    onehot = _bucket_onehot()
    bias = mm(w2["rel_bias"].T, onehot, "nn", hi=True, tn=4096, name="rel_bias_table").reshape(SWA_HEADS, BLOCK, 2 * BLOCK)
    alog_row = jnp.pad(w2["gdn_a_log"], ((0, 0), (GDN_HEADS, 128 - 2 * GDN_HEADS)))
    dt_row = jnp.pad(w2["gdn_dt_bias"], ((0, 0), (GDN_HEADS, 128 - 2 * GDN_HEADS)))

    xb = cast_bf16(xs, name="cast_x")
    memb = cast_bf16(mems, name="cast_mem")
    gates, mid_got = mm(xb, w_gates, "nn", name="proj_gates", side=gather_first([mine[n] for n in mid]))
    gdn_in, mid_got = mm(xb, w_gdn, "nn", name="proj_gdn", side=gather_second(mid_got))
    got = dict(zip(mid, mid_got))
    w_br_swa, w_br_gdn, w_mem_o = (_from_column_shards(got[n]) for n in ("w_br_swa", "w_br_gdn", "w_mem_o"))
    w_mix_o = got["w_mix_o"].reshape(D_MODEL, D_MODEL)
    w_mem_q = got["w_mem_q"].reshape(D_MODEL, MEM_W)
    w_mem_kv = got["w_mem_kv"].reshape(D_MODEL, 2 * MEM_W)
    swa_in = mm(xb, w_swa, "nn", tn=SWA_IN_W, name="proj_swa")
    attn, down_got = swa_fwd(swa_in, bias, w2["swa_sinks"], name="swa_fwd", side=gather_first([mine["w_down"]]))
    qn, kn, vv = gdn_pre_fwd(gdn_in, convw, name="gdn_pre_fwd")
    gbeta = gbeta_fwd(swa_in, alog_row, dt_row, name="gbeta_fwd")
    (o_gdn, states), up_got = gdn_scan_fwd(qn, kn, vv, gbeta, name="gdn_scan_fwd", side=gather_first([mine["w_up"]]))
    ygd = gdn_post_fwd(o_gdn, gdn_in, w2["gdn_norm_w"], name="gdn_post_fwd")
    y_swa, down_got = mm(attn, w_br_swa, "nn", name="br_swa", side=gather_second(down_got))
    y_gdn, up_got = mm(ygd, w_br_gdn, "nn", name="br_gdn", side=gather_second(up_got))
    w_up_blocked = up_got[0]
    w_down_p = jnp.pad(down_got[0].reshape(4, FF_SHARD, D_MODEL), ((0, 0), (0, pad_ff), (0, 0))).reshape(4 * FF_PAD, D_MODEL)
    mixed = merge_fwd(gates, y_swa, y_gdn, name="merge_fwd")
    z1 = mm(mixed, w_mix_o, "nn", add=xs, add_scale=ALPHA, name="mix_o")
    x1, x1b = ln_fwd(z1, w2["ln1_g"], w2["ln1_b"], name="ln1_fwd")
    qm = mm(x1b, w_mem_q, "nn", name="mem_q")
    kv = mm(memb, w_mem_kv, "nn", name="mem_kv")
    om = memattn_fwd(qm, kv, name="memattn_fwd")
    z2 = mm(om, w_mem_o, "nn", add=x1, add_scale=ALPHA, name="mem_o")
    x2, x2b = ln_fwd(z2, w2["ln2_g"], w2["ln2_b"], name="ln2_fwd")
    hpre = mm(x2b, w_up_blocked, "nn", b_blocked=True, name="ffn_up")
    act = ffn_act_fwd(hpre, cwb, name="ffn_act_fwd")
    z3 = mm(act, w_down_p, "nn", add=x2, add_scale=ALPHA, tk=2 * FF_PAD, name="ffn_down")
    dz3, dz3b, d_ln3g, d_ln3b, loss = ln_loss(z3, target, w2["ln3_g"], w2["ln3_b"], name="ln3_loss")

    dact = mm(dz3b, w_down_p, "nt", tn=FF_PAD, name="d_act")
    d_wdown_p = mm(act, dz3b, "tn", tm=FF_PAD, name="dw_down")
    d_hpre, d_cwb = ffn_act_bwd(hpre, dact, cwb, name="ffn_act_bwd")
    def sibling_sums(names, chunks, received):
        return [add_sibling(c, r, name="grad_add_sibling_" + n) for n, c, r in zip(names, chunks, received)]

    def chip_sums(names, received):
        return [add_four(r, name="grad_add_chips_" + n) for n, r in zip(names, received)]

    dx2 = mm(d_hpre, w_up_blocked, "nt", b_blocked=True, add=dz3, add_scale=ALPHA, name="d_x2")
    d_wup = mm(x2b, d_hpre, "tn", out_blocked=True, name="dw_up")
    ffn = ("w_up", "w_down")
    ffn_chunks = [d_wup, d_wdown_p.reshape(4, FF_PAD, D_MODEL)[:, :FF_SHARD].reshape(N_DEV, FF_SHARD // 2, D_MODEL)]
    dz2, dz2b, d_ln2g, d_ln2b = ln_bwd(dx2, z2, w2["ln2_g"], name="ln2_bwd")
    d_om = mm(dz2b, w_mem_o, "nt", name="d_om")
    d_wmemo = mm(om, dz2b, "tn", name="dw_mem_o")
    dqm, dkv = memattn_bwd(qm, kv, d_om, name="memattn_bwd")
    dx1 = mm(dqm, w_mem_q, "nt", add=dz2, add_scale=ALPHA, name="d_x1")
    d_wmemq = mm(x1b, dqm, "tn", name="dw_mem_q")
    d_wmemkv = mm(memb, dkv, "tn", name="dw_mem_kv")
    dz1, dz1b, d_ln1g, d_ln1b = ln_bwd(dx1, z1, w2["ln1_g"], name="ln1_bwd")
    dmix, ffn_received = mm(dz1b, w_mix_o, "nt", name="d_mixed", side=grad_to_sibling(ffn_chunks))
    ffn_sums = sibling_sums(ffn, ffn_chunks, ffn_received)
    d_wmixo = mm(mixed, dz1b, "tn", name="dw_mix_o")
    dys, dyg, d_gates = merge_bwd(gates, y_swa, y_gdn, dmix, name="merge_bwd")
    d_attn = mm(dys, w_br_swa, "nt", name="d_attn")
    d_wbrswa = mm(attn, dys, "tn", name="dw_br_swa")
    d_ygd = mm(dyg, w_br_gdn, "nt", name="d_ygd")
    d_wbrgdn = mm(ygd, dyg, "tn", name="dw_br_gdn")
    mid_chunks = [_column_shards(d_wbrswa), _column_shards(d_wbrgdn), _column_shards(d_wmemo),
                  d_wmixo.reshape(N_DEV, D_MODEL // N_DEV, D_MODEL), d_wmemq.reshape(N_DEV, D_MODEL // N_DEV, MEM_W),
                  d_wmemkv.reshape(N_DEV, D_MODEL // N_DEV, 2 * MEM_W)]
    d_o, d_gz, d_normw = gdn_post_bwd(o_gdn, gdn_in, w2["gdn_norm_w"], d_ygd, name="gdn_post_bwd")
    (dqn, dkn, dvv, dgbeta), received = gdn_scan_bwd(
        qn, kn, vv, gbeta, states, d_o, name="gdn_scan_bwd", side=join_sides(grad_to_chips(ffn_sums), grad_to_sibling(mid_chunks)))
    grads = dict(zip(ffn, chip_sums(ffn, received[:2])))
    mid_sums = sibling_sums(mid, mid_chunks, received[2:])
    d_gdn_in, d_convw = gdn_pre_bwd(gdn_in, convw, dqn, dkn, dvv, d_gz, name="gdn_pre_bwd")
    d_ba, d_alog, d_dt = gbeta_bwd(swa_in, alog_row, dt_row, dgbeta, name="gbeta_bwd")
    (dq, dkc, dkp, dvc, dvp, d_bias, d_sinks), received = swa_bwd(swa_in, bias, w2["swa_sinks"], d_attn, name="swa_bwd",
                                                                  side=grad_to_chips(mid_sums))
    grads.update(zip(mid, chip_sums(mid, received)))
    d_swa_in = swa_in_grad(dq, dkc, dkp, dvc, dvp, d_ba, name="swa_in_grad")
    d_relbias = mm(d_bias.reshape(SWA_HEADS, -1), onehot, "nt", hi=True, tk=4096, name="d_rel_bias").T
    d_wgates = mm(xb, d_gates, "tn", name="dw_gates")
    d_wgdn = mm(xb, d_gdn_in, "tn", name="dw_gdn")
    d_wswa = mm(xb, d_swa_in, "tn", tn=SWA_IN_W, name="dw_swa")
    in_chunks = [merge_w_in_grad(d_wswa, d_wgdn, d_wgates, name="merge_w_in_grad")]
    gx, received = mm(d_gates, w_gates, "nt", add=dz1, add_scale=ALPHA, name="dx_gates", side=grad_to_sibling(in_chunks))
    in_sums = sibling_sums(("w_in",), in_chunks, received)
    gx, received = mm(d_gdn_in, w_gdn, "nt", add=gx, name="dx_gdn", side=grad_to_chips(in_sums))
    gx = mm(d_swa_in, w_swa, "nt", add=gx, tk=SWA_IN_W, name="dx_swa")
    grads["w_in"] = chip_sums(("w_in",), received)[0][:, :W_IN_SHARD]
    grads["w_up"] = grads["w_up"][:, :FF_SHARD]

    gsmall = {
        "rel_bias": d_relbias, "swa_sinks": d_sinks[:, :SWA_HEADS], "gdn_a_log": d_alog[:, GDN_HEADS:2 * GDN_HEADS],
        "gdn_dt_bias": d_dt[:, GDN_HEADS:2 * GDN_HEADS], "gdn_norm_w": d_normw, "ln1_g": d_ln1g, "ln1_b": d_ln1b,
        "ln2_g": d_ln2g, "ln2_b": d_ln2b, "ln3_g": d_ln3g, "ln3_b": d_ln3b,
        "ffn_conv_b": d_cwb[:, 3, :FF_SHARD].reshape(1, 2 * D_FF),
        "ffn_conv_w": jnp.transpose(d_cwb[:, :3, :FF_SHARD], (1, 0, 2)).reshape(3, 2 * D_FF),
        "gdn_conv_w": d_convw[:GDN_CONV],
    }
    small_shapes = [shp for _, shp in SMALL] + [(3, 2 * D_FF), (GDN_CONV, QKV_W)]
    small_names = [n for n, _ in SMALL] + ["ffn_conv_w", "gdn_conv_w"]
    small_sum = all_reduce_small(_pack([gsmall[n] for n in small_names], AR_ROWS, f32), name="all_reduce_small")
    grads.update(zip(small_names, _unpack(small_sum.reshape(-1), small_shapes)))
    grads["ffn_conv_w"] = lax.dynamic_slice_in_dim(grads["ffn_conv_w"], my_id * FF_SHARD, FF_SHARD, axis=1)
    grads["gdn_conv_w"] = lax.dynamic_slice_in_dim(grads["gdn_conv_w"], my_id * (QKV_W // N_DEV), QKV_W // N_DEV, axis=1)

    big = [n for n, shp, _ in SHARDED if shp[0] * shp[1] > 8192]
    tiny = [n for n in WEIGHTS if n not in big]
    delta, new_m, new_v = {}, {}, {}
    for n in big:
        delta[n], new_m[n], new_v[n] = adamw(w2[n], grads[n], m2[n], v2[n], name="adamw_" + n)
    tiny_shapes = [w2[n].shape for n in tiny]
    packed = [_pack([src[n] for n in tiny], SMALL_ROWS, f32) for src in (w2, grads, m2, v2)]
    for dst, res in zip((delta, new_m, new_v), adamw(*packed, name="adamw_small")):
        dst.update(zip(tiny, _unpack(res.reshape(-1), tiny_shapes)))

    def shaped(d):
        return [d[n].reshape(env[n].shape) for n in WEIGHTS]

    loss_all = lax.psum(loss[0, 0], ("x", "y", "c"))
    return (loss_all, gx[None], *shaped(grads), *shaped(delta), *shaped(new_m), *shaped(new_v))
```

```python
import functools
import math
from typing import Callable, NamedTuple

import jax
import jax.numpy as jnp
from jax import lax
from jax.experimental import pallas as pl
from jax.experimental.pallas import tpu as pltpu

f32 = jnp.float32
bf16 = jnp.bfloat16
HI = lax.Precision.HIGHEST
MESH = pl.DeviceIdType.MESH

D_MODEL = 2048
N_DEV = 8
SWA_HEADS, SWA_KV_HEADS, SWA_HEAD_DIM, BLOCK = 16, 2, 64, 128
REL_BUCKETS, REL_MAX_DIST = 32, 128
GDN_HEADS, GDN_HEAD_DIM, GDN_CONV, GDN_CHUNK = 8, 128, 4, 64
MEM_HEADS, MEM_HEAD_DIM = 4, 128
D_FF = 5504
FF_SHARD = 2 * D_FF // N_DEV
FF_PAD = 1408
NORM_EPS = 1e-5
ALPHA = 2.0 ** 0.25
NEG_INF = -1e30
SWA_Q, SWA_KV, GDN_W, MEM_W = 1024, 128, 1024, 512
IN_DIM = 9488
HALO = 8

ADAM_LR, ADAM_B1, ADAM_B2, ADAM_EPS, ADAM_WD, ADAM_STEP = 0.001, 0.9, 0.999, 1e-08, 0.01, 10

PACK_COLS = 1024
SMALL_ROWS = 32
AR_ROWS = 72
CONV_ROWS = 48

SHARDED = (
    ("w_in", (2048, 1186), 1), ("w_br_swa", (1024, 256), 1), ("w_br_gdn", (1024, 256), 1),
    ("w_mix_o", (256, 2048), 0), ("w_mem_q", (256, 512), 0), ("w_mem_kv", (256, 1024), 0),
    ("w_mem_o", (512, 256), 1), ("w_up", (2048, 1376), 1), ("w_down", (688, 2048), 0),
    ("ffn_conv_w", (3, 1376), 1), ("gdn_conv_w", (4, 384), 1),
)
SMALL = (
    ("rel_bias", (32, 16)), ("swa_sinks", (1, 16)), ("gdn_a_log", (1, 8)), ("gdn_dt_bias", (1, 8)),
    ("gdn_norm_w", (1, 128)), ("ln1_g", (1, 2048)), ("ln1_b", (1, 2048)), ("ln2_g", (1, 2048)),
    ("ln2_b", (1, 2048)), ("ln3_g", (1, 2048)), ("ln3_b", (1, 2048)), ("ffn_conv_b", (1, 11008)),
)
WEIGHTS = ("w_in", "rel_bias", "swa_sinks", "gdn_conv_w", "gdn_a_log", "gdn_dt_bias", "gdn_norm_w", "w_br_swa",
           "w_br_gdn", "w_mix_o", "ln1_g", "ln1_b", "w_mem_q", "w_mem_kv", "w_mem_o", "ln2_g", "ln2_b", "w_up",
           "ffn_conv_w", "ffn_conv_b", "w_down", "ln3_g", "ln3_b")


def _tile(n, target, align):
    if n <= target:
        return n
    t = (target // align) * align
    while t >= align:
        if n % t == 0:
            return t
        t -= align
    return n


VMEM_LIMIT_BYTES = 56 * 1024 * 1024


def _params(*sem):
    return pltpu.CompilerParams(dimension_semantics=sem, vmem_limit_bytes=VMEM_LIMIT_BYTES)


def _sigmoid(v):
    return jax.nn.sigmoid(v)


def _d16(a, b, dims):
    return lax.dot_general(a.astype(bf16), b.astype(bf16), (dims, ((), ())), preferred_element_type=f32)


NN = ((1,), (0,))
NT = ((1,), (1,))
TN = ((0,), (0,))


def mm(a, b, mode, *, name, add=None, add_scale=1.0, out_dtype=f32, hi=False, tm=1024, tn=1024, tk=2048,
       b_blocked=False, out_blocked=False, side=None):
    if b_blocked:
        nb, rows, width = b.shape
        if mode == "nn":
            (m, k), n, tn = a.shape, nb * width, width
        else:
            (m, k), n, tk = a.shape, rows, width
    elif mode == "nn":
        (m, k), (_, n) = a.shape, b.shape
    elif mode == "nt":
        (m, k), (n, _) = a.shape, b.shape
    else:
        (k, m), (_, n) = a.shape, b.shape
    if out_blocked:
        tn = n // N_DEV
    tm, tn, tk = _tile(m, tm, 8 if mode != "tn" else 128), _tile(n, tn, 128), _tile(k, tk, 128 if mode != "tn" else 8)
    nk = k // tk
    dims = {"nn": NN, "nt": NT, "tn": TN}[mode]
    a_spec = pl.BlockSpec((tk, tm), lambda i, j, kk: (kk, i)) if mode == "tn" else pl.BlockSpec((tm, tk), lambda i, j, kk: (i, kk))
    if b_blocked:
        b_spec = (pl.BlockSpec((None, tk, tn), lambda i, j, kk: (j, kk, 0)) if mode == "nn"
                  else pl.BlockSpec((None, tn, tk), lambda i, j, kk: (kk, j, 0)))
    else:
        b_spec = pl.BlockSpec((tn, tk), lambda i, j, kk: (j, kk)) if mode == "nt" else pl.BlockSpec((tk, tn), lambda i, j, kk: (kk, j))
    if out_blocked:
        o_spec, o_shape = pl.BlockSpec((None, tm, tn), lambda i, j, kk: (j, i, 0)), (N_DEV, m, tn)
    else:
        o_spec, o_shape = pl.BlockSpec((tm, tn), lambda i, j, kk: (i, j)), (m, n)
    has_add = add is not None

    def product(a_ref, b_ref):
        if hi:
            return lax.dot_general(a_ref[...], b_ref[...], (dims, ((), ())), precision=HI, preferred_element_type=f32)
        return _d16(a_ref[...], b_ref[...], dims)

    def finish(r, add_ref, o_ref):
        if has_add:
            r = r + add_scale * add_ref[...]
        o_ref[...] = r.astype(out_dtype)

    def body_one_step(a_ref, b_ref, *rest):
        finish(product(a_ref, b_ref), rest[0] if has_add else None, rest[-1])

    def body_k_steps(a_ref, b_ref, *rest):
        o_ref, acc_ref = rest[-2:]
        kk = pl.program_id(2)

        @pl.when(kk == 0)
        def _():
            acc_ref[...] = jnp.zeros_like(acc_ref)

        acc_ref[...] += product(a_ref, b_ref)

        @pl.when(kk == nk - 1)
        def _():
            finish(acc_ref[...], rest[0] if has_add else None, o_ref)

    return _call(body_one_step if nk == 1 else body_k_steps, (a, b, add) if has_add else (a, b), name=name,
                 grid=(m // tm, n // tn, nk), in_specs=[a_spec, b_spec] + ([o_spec] if has_add else []), out_specs=o_spec,
                 out_shape=jax.ShapeDtypeStruct(o_shape, out_dtype),
                 scratch_shapes=[] if nk == 1 else [pltpu.VMEM((tm, tn), f32)],
                 semantics=("parallel", "parallel", "arbitrary"), side=side)


def cast_bf16(a, *, name):
    m, n = a.shape
    tm = _tile(m, 512, 16)

    def body(a_ref, o_ref):
        o_ref[...] = a_ref[...].astype(bf16)

    return pl.pallas_call(body, name=name, grid=(m // tm,), in_specs=[pl.BlockSpec((tm, n), lambda i: (i, 0))],
                          out_specs=pl.BlockSpec((tm, n), lambda i: (i, 0)), out_shape=jax.ShapeDtypeStruct((m, n), bf16),
                          compiler_params=_params("parallel"))(a)


def _ln_stats(z):
    mu = jnp.mean(z, axis=-1, keepdims=True)
    zc = z - mu
    var = jnp.mean(zc * zc, axis=-1, keepdims=True)
    rstd = lax.rsqrt(var + NORM_EPS)
    return zc * rstd, rstd


def ln_fwd(z, g, b, *, name):
    s, d = z.shape
    tm = _tile(s, 256, 16)

    def body(z_ref, g_ref, b_ref, y_ref, yb_ref):
        xhat, _ = _ln_stats(z_ref[...])
        y = xhat * g_ref[...] + b_ref[...]
        y_ref[...] = y
        yb_ref[...] = y.astype(bf16)

    row = pl.BlockSpec((tm, d), lambda i: (i, 0))
    vec = pl.BlockSpec((1, d), lambda i: (0, 0))
    return pl.pallas_call(body, name=name, grid=(s // tm,), in_specs=[row, vec, vec], out_specs=[row, row],
                          out_shape=[jax.ShapeDtypeStruct((s, d), f32), jax.ShapeDtypeStruct((s, d), bf16)],
                          compiler_params=_params("parallel"))(z, g, b)


def _ln_bwd_tile(dy, z, g):
    xhat, rstd = _ln_stats(z)
    dxh = dy * g
    m1 = jnp.mean(dxh, axis=-1, keepdims=True)
    m2 = jnp.mean(dxh * xhat, axis=-1, keepdims=True)
    dz = rstd * (dxh - m1 - xhat * m2)
    return dz, jnp.sum(dy * xhat, axis=0, keepdims=True), jnp.sum(dy, axis=0, keepdims=True)


def ln_bwd(dy, z, g, *, name):
    s, d = z.shape
    tm = _tile(s, 256, 16)

    def body(dy_ref, z_ref, g_ref, dz_ref, dzb_ref, dg_ref, db_ref):
        @pl.when(pl.program_id(0) == 0)
        def _():
            dg_ref[...] = jnp.zeros_like(dg_ref)
            db_ref[...] = jnp.zeros_like(db_ref)

        dz, dg, db = _ln_bwd_tile(dy_ref[...], z_ref[...], g_ref[...])
        dz_ref[...] = dz
        dzb_ref[...] = dz.astype(bf16)
        dg_ref[...] += dg
        db_ref[...] += db

    row = pl.BlockSpec((tm, d), lambda i: (i, 0))
    vec = pl.BlockSpec((1, d), lambda i: (0, 0))
    return pl.pallas_call(body, name=name, grid=(s // tm,), in_specs=[row, row, vec], out_specs=[row, row, vec, vec],
                          out_shape=[jax.ShapeDtypeStruct((s, d), f32), jax.ShapeDtypeStruct((s, d), bf16),
                                     jax.ShapeDtypeStruct((1, d), f32), jax.ShapeDtypeStruct((1, d), f32)],
                          compiler_params=_params("arbitrary"))(dy, z, g)


def ln_loss(z, target, g, b, *, name):
    s, d = z.shape
    tm = _tile(s, 256, 16)
    nt = s // tm

    def body(z_ref, t_ref, g_ref, b_ref, dz_ref, dzb_ref, dg_ref, db_ref, loss_ref, lacc_ref):
        i = pl.program_id(0)

        @pl.when(i == 0)
        def _():
            dg_ref[...] = jnp.zeros_like(dg_ref)
            db_ref[...] = jnp.zeros_like(db_ref)
            lacc_ref[...] = jnp.zeros_like(lacc_ref)

        zv, gv = z_ref[...], g_ref[...]
        xhat, _ = _ln_stats(zv)
        err = xhat * gv + b_ref[...] - t_ref[...]
        lacc_ref[...] += jnp.sum(err * err, axis=0, keepdims=True)
        dz, dg, db = _ln_bwd_tile(err * (1.0 / d), zv, gv)
        dz_ref[...] = dz
        dzb_ref[...] = dz.astype(bf16)
        dg_ref[...] += dg
        db_ref[...] += db

        @pl.when(i == nt - 1)
        def _():
            loss_ref[...] = (0.5 / d) * jnp.sum(lacc_ref[...], axis=1, keepdims=True)

    row = pl.BlockSpec((tm, d), lambda i: (i, 0))
    vec = pl.BlockSpec((1, d), lambda i: (0, 0))
    return pl.pallas_call(body, name=name, grid=(nt,), in_specs=[row, row, vec, vec],
                          out_specs=[row, row, vec, vec, pl.BlockSpec((1, 1), lambda i: (0, 0))],
                          out_shape=[jax.ShapeDtypeStruct((s, d), f32), jax.ShapeDtypeStruct((s, d), bf16),
                                     jax.ShapeDtypeStruct((1, d), f32), jax.ShapeDtypeStruct((1, d), f32),
                                     jax.ShapeDtypeStruct((1, 1), f32)],
                          scratch_shapes=[pltpu.VMEM((1, d), f32)],
                          compiler_params=_params("arbitrary"))(z, target, g, b)


def merge_fwd(gates, ys, yg, *, name):
    s, d = ys.shape
    tm = _tile(s, 256, 16)

    def body(gt_ref, ys_ref, yg_ref, o_ref):
        o_ref[...] = (_sigmoid(gt_ref[:, :d]) * ys_ref[...] + _sigmoid(gt_ref[:, d:]) * yg_ref[...]).astype(bf16)

    row = pl.BlockSpec((tm, d), lambda i: (i, 0))
    return pl.pallas_call(body, name=name, grid=(s // tm,), in_specs=[pl.BlockSpec((tm, 2 * d), lambda i: (i, 0)), row, row],
                          out_specs=row, out_shape=jax.ShapeDtypeStruct((s, d), bf16),
                          compiler_params=_params("parallel"))(gates, ys, yg)


def merge_bwd(gates, ys, yg, dmix, *, name):
    s, d = ys.shape
    tm = _tile(s, 256, 16)

    def body(gt_ref, ys_ref, yg_ref, dm_ref, dys_ref, dyg_ref, dgt_ref):
        dm = dm_ref[...]
        sa, sb = _sigmoid(gt_ref[:, :d]), _sigmoid(gt_ref[:, d:])
        dys_ref[...] = (dm * sa).astype(bf16)
        dyg_ref[...] = (dm * sb).astype(bf16)
        dgt_ref[:, :d] = (dm * ys_ref[...] * sa * (1.0 - sa)).astype(bf16)
        dgt_ref[:, d:] = (dm * yg_ref[...] * sb * (1.0 - sb)).astype(bf16)

    row = pl.BlockSpec((tm, d), lambda i: (i, 0))
    wide = pl.BlockSpec((tm, 2 * d), lambda i: (i, 0))
    return pl.pallas_call(body, name=name, grid=(s // tm,), in_specs=[wide, row, row, row], out_specs=[row, row, wide],
                          out_shape=[jax.ShapeDtypeStruct((s, d), bf16), jax.ShapeDtypeStruct((s, d), bf16),
                                     jax.ShapeDtypeStruct((s, 2 * d), bf16)],
                          compiler_params=_params("parallel"))(gates, ys, yg, dmix)


def _shift_down(ext, j):
    return ext if j == 0 else pltpu.roll(ext, j, 0)


def _shift_up(ext, j):
    return ext if j == 0 else pltpu.roll(ext, ext.shape[0] - j, 0)


def _conv_taps(ext, width):
    return [_shift_down(ext, width - 1 - j)[HALO:] for j in range(width)]


def _causal_conv(taps, w_ref):
    acc = None
    for j, tap in enumerate(taps):
        term = w_ref[j:j + 1, :] * tap
        acc = term if acc is None else acc + term
    return acc


def _conv_input_grad(dy_ext, w_ref, width, rows):
    acc = None
    for j in range(width):
        term = w_ref[j:j + 1, :] * _shift_up(dy_ext, width - 1 - j)
        acc = term if acc is None else acc + term
    return acc[:rows]


def _conv_weight_grad(dy, taps, rows):
    return [jnp.sum(dy * tap[:rows], axis=0, keepdims=True) for tap in taps]


def _rows_to_block(rows, n_rows, cols):
    r = lax.broadcasted_iota(jnp.int32, (n_rows, cols), 0)
    out = jnp.zeros((n_rows, cols), f32)
    for j, v in enumerate(rows):
        out = out + jnp.where(r == j, v, 0.0)
    return out


def _silu_and_grad(v):
    sg = _sigmoid(v)
    return v * sg, sg * (1.0 + v * (1.0 - sg))


def ffn_act_fwd(hpre, cwb, *, name):
    s = hpre.shape[0]
    tm = _tile(s, 256, 16)
    hb = tm // HALO

    def body(hg_ref, hgp_ref, hu_ref, hup_ref, cg_ref, cu_ref, o_ref):
        first = pl.program_id(1) == 0

        def conv(h_ref, hp_ref, c_ref):
            prev = jnp.where(first, 0.0, hp_ref[...])
            return _causal_conv(_conv_taps(jnp.concatenate([prev, h_ref[...]], axis=0), 3), c_ref.at[0]) + c_ref[0, 3:4, :]

        g = conv(hg_ref, hgp_ref, cg_ref)
        u = conv(hu_ref, hup_ref, cu_ref)
        o_ref[...] = (g * _sigmoid(g) * u).astype(bf16)

    def tile(off):
        return pl.BlockSpec((tm, FF_PAD), lambda j, i: (i, j + off))

    def halo(off):
        return pl.BlockSpec((HALO, FF_PAD), lambda j, i: (jnp.maximum(i * hb - 1, 0), j + off))

    def taps(off):
        return pl.BlockSpec((1, 8, FF_PAD), lambda j, i: (j + off, 0, 0))

    return pl.pallas_call(body, name=name, grid=(4, s // tm),
                          in_specs=[tile(0), halo(0), tile(4), halo(4), taps(0), taps(4)],
                          out_specs=pl.BlockSpec((tm, FF_PAD), lambda j, i: (i, j)),
                          out_shape=jax.ShapeDtypeStruct((s, 4 * FF_PAD), bf16),
                          compiler_params=_params("parallel", "parallel"))(hpre, hpre, hpre, hpre, cwb, cwb)


def ffn_act_bwd(hpre, dact, cwb, *, name):
    s = hpre.shape[0]
    tm = _tile(s, 256, 16)
    hb = tm // HALO
    nt = s // tm
    last_hb = s // HALO - 1

    def body(hg_ref, hgp_ref, hgn_ref, hu_ref, hup_ref, hun_ref, d_ref, dn_ref, cg_ref, cu_ref,
             dhg_ref, dhu_ref, dcg_ref, dcu_ref):
        i = pl.program_id(1)
        first, last = i == 0, i == nt - 1

        @pl.when(first)
        def _():
            dcg_ref[...] = jnp.zeros_like(dcg_ref)
            dcu_ref[...] = jnp.zeros_like(dcu_ref)

        def ext_of(h_ref, hp_ref, hn_ref):
            return jnp.concatenate([jnp.where(first, 0.0, hp_ref[...]), h_ref[...], hn_ref[...]], axis=0)

        eg, eu = ext_of(hg_ref, hgp_ref, hgn_ref), ext_of(hu_ref, hup_ref, hun_ref)
        g_taps, u_taps = _conv_taps(eg, 3), _conv_taps(eu, 3)
        g = _causal_conv(g_taps, cg_ref.at[0]) + cg_ref[0, 3:4, :]
        u = _causal_conv(u_taps, cu_ref.at[0]) + cu_ref[0, 3:4, :]
        d = jnp.concatenate([d_ref[...], jnp.where(last, 0.0, dn_ref[...])], axis=0)
        act, dact_dg = _silu_and_grad(g)
        dg = d * u * dact_dg
        du = d * act
        dhg_ref[...] = _conv_input_grad(dg, cg_ref.at[0], 3, tm).astype(bf16)
        dhu_ref[...] = _conv_input_grad(du, cu_ref.at[0], 3, tm).astype(bf16)
        dgt, dut = dg[:tm], du[:tm]
        dcg_ref[0] += _rows_to_block(_conv_weight_grad(dgt, g_taps, tm) + [jnp.sum(dgt, axis=0, keepdims=True)], 8, FF_PAD)
        dcu_ref[0] += _rows_to_block(_conv_weight_grad(dut, u_taps, tm) + [jnp.sum(dut, axis=0, keepdims=True)], 8, FF_PAD)

    def tile(off):
        return pl.BlockSpec((tm, FF_PAD), lambda j, i: (i, j + off))

    def prev(off):
        return pl.BlockSpec((HALO, FF_PAD), lambda j, i: (jnp.maximum(i * hb - 1, 0), j + off))

    def nxt(off):
        return pl.BlockSpec((HALO, FF_PAD), lambda j, i: (jnp.minimum((i + 1) * hb, last_hb), j + off))

    def taps(off):
        return pl.BlockSpec((1, 8, FF_PAD), lambda j, i: (j + off, 0, 0))

    dhg, dhu, dcg, dcu = pl.pallas_call(
        body, name=name, grid=(4, nt),
        in_specs=[tile(0), prev(0), nxt(0), tile(4), prev(4), nxt(4), tile(0), nxt(0), taps(0), taps(4)],
        out_specs=[tile(0), tile(0), taps(0), taps(0)],
        out_shape=[jax.ShapeDtypeStruct((s, 4 * FF_PAD), bf16), jax.ShapeDtypeStruct((s, 4 * FF_PAD), bf16),
                   jax.ShapeDtypeStruct((4, 8, FF_PAD), f32), jax.ShapeDtypeStruct((4, 8, FF_PAD), f32)],
        compiler_params=_params("parallel", "arbitrary"),
    )(hpre, hpre, hpre, hpre, hpre, hpre, dact, dact, cwb, cwb)
    return jnp.concatenate([dhg, dhu], axis=1), jnp.concatenate([dcg, dcu], axis=0)


MEM_SCALE = MEM_HEAD_DIM ** -0.5


def _softmax_rows(sc):
    m = jnp.max(sc, axis=-1, keepdims=True)
    e = jnp.exp(sc - m)
    return e / jnp.sum(e, axis=-1, keepdims=True)


def memattn_fwd(qm, kv, *, name):
    s = qm.shape[0]
    mlen = kv.shape[0]
    tm = _tile(s, 512, 16)

    def body(q_ref, kv_ref, o_ref):
        for h in range(MEM_HEADS):
            lo = h * MEM_HEAD_DIM
            q = q_ref[:, lo:lo + MEM_HEAD_DIM]
            k = kv_ref[:, lo:lo + MEM_HEAD_DIM]
            v = kv_ref[:, MEM_W + lo:MEM_W + lo + MEM_HEAD_DIM]
            p = _softmax_rows(_d16(q, k, NT) * MEM_SCALE)
            o_ref[:, lo:lo + MEM_HEAD_DIM] = _d16(p, v, NN).astype(bf16)

    return pl.pallas_call(body, name=name, grid=(s // tm,),
                          in_specs=[pl.BlockSpec((tm, MEM_W), lambda i: (i, 0)), pl.BlockSpec((mlen, 2 * MEM_W), lambda i: (0, 0))],
                          out_specs=pl.BlockSpec((tm, MEM_W), lambda i: (i, 0)),
                          out_shape=jax.ShapeDtypeStruct((s, MEM_W), bf16), compiler_params=_params("parallel"))(qm, kv)


def memattn_bwd(qm, kv, dout, *, name):
    s = qm.shape[0]
    mlen = kv.shape[0]
    tm = _tile(s, 512, 16)

    def body(q_ref, kv_ref, do_ref, dq_ref, dkv_ref):
        @pl.when(pl.program_id(0) == 0)
        def _():
            dkv_ref[...] = jnp.zeros_like(dkv_ref)

        for h in range(MEM_HEADS):
            lo = h * MEM_HEAD_DIM
            q = q_ref[:, lo:lo + MEM_HEAD_DIM]
            k = kv_ref[:, lo:lo + MEM_HEAD_DIM]
            v = kv_ref[:, MEM_W + lo:MEM_W + lo + MEM_HEAD_DIM]
            do = do_ref[:, lo:lo + MEM_HEAD_DIM]
            p = _softmax_rows(_d16(q, k, NT) * MEM_SCALE)
            dp = _d16(do, v, NT)
            ds = p * (dp - jnp.sum(p * dp, axis=-1, keepdims=True)) * MEM_SCALE
            dq_ref[:, lo:lo + MEM_HEAD_DIM] = _d16(ds, k, NN).astype(bf16)
            dkv_ref[:, lo:lo + MEM_HEAD_DIM] += _d16(ds, q, TN)
            dkv_ref[:, MEM_W + lo:MEM_W + lo + MEM_HEAD_DIM] += _d16(p, do, TN)

    row = pl.BlockSpec((tm, MEM_W), lambda i: (i, 0))
    full = pl.BlockSpec((mlen, 2 * MEM_W), lambda i: (0, 0))
    return pl.pallas_call(body, name=name, grid=(s // tm,), in_specs=[row, full, row], out_specs=[row, full],
                          out_shape=[jax.ShapeDtypeStruct((s, MEM_W), bf16), jax.ShapeDtypeStruct((mlen, 2 * MEM_W), f32)],
                          compiler_params=_params("arbitrary"))(qm, kv, dout)


SWA_SCALE = SWA_HEAD_DIM ** -0.5
SWA_GROUP = SWA_HEADS // SWA_KV_HEADS
SWA_IN_W = 1408
K_COL, V_COL, BA_COL = SWA_Q // 128, SWA_Q // 128 + 1, SWA_Q // 128 + 2


def _swa_mask(n):
    qi = lax.broadcasted_iota(jnp.int32, (BLOCK, 2 * BLOCK), 0)
    kj = lax.broadcasted_iota(jnp.int32, (BLOCK, 2 * BLOCK), 1)
    dist = qi + BLOCK - kj
    return (dist >= 0) & (dist < BLOCK) & ((n > 0) | (kj >= BLOCK))


def _swa_group_probs(q, k, bias, sink, mask):
    heads = range(len(q))
    sc = [jnp.where(mask, _d16(q[h], k, NT) * SWA_SCALE + bias[h], NEG_INF) for h in heads]
    m = [jnp.maximum(jnp.max(sc[h], axis=-1, keepdims=True), sink[h]) for h in heads]
    e = [jnp.exp(sc[h] - m[h]) for h in heads]
    es = [jnp.exp(sink[h] - m[h]) for h in heads]
    inv = [1.0 / (jnp.sum(e[h], axis=-1, keepdims=True) + es[h]) for h in heads]
    return e, es, inv


def _swa_group_inputs(g, q_ref, bias_ref, sink_ref):
    hs = range(g * SWA_GROUP, (g + 1) * SWA_GROUP)
    return ([q_ref[:, h * SWA_HEAD_DIM:(h + 1) * SWA_HEAD_DIM] for h in hs], [bias_ref[h] for h in hs],
            [sink_ref[:, h:h + 1] for h in hs])


def _swa_specs():
    q_spec = pl.BlockSpec((BLOCK, SWA_Q), lambda n: (n, 0))

    def band(col):
        return [pl.BlockSpec((BLOCK, SWA_KV), lambda n: (jnp.maximum(n - 1, 0), col)),
                pl.BlockSpec((BLOCK, SWA_KV), lambda n: (n, col))]

    bias_spec = pl.BlockSpec((SWA_HEADS, BLOCK, 2 * BLOCK), lambda n: (0, 0, 0))
    sink_spec = pl.BlockSpec((1, SWA_HEADS), lambda n: (0, 0))
    return [q_spec] + band(K_COL) + band(V_COL) + [bias_spec, sink_spec]


def swa_fwd(swa_in, bias, sinks, *, name, side=None):
    s = swa_in.shape[0]

    def body(q_ref, kp_ref, kc_ref, vp_ref, vc_ref, bias_ref, sink_ref, o_ref):
        mask = _swa_mask(pl.program_id(0))
        kb = jnp.concatenate([kp_ref[...], kc_ref[...]], axis=0)
        vb = jnp.concatenate([vp_ref[...], vc_ref[...]], axis=0)
        for g in range(SWA_KV_HEADS):
            kl = g * SWA_HEAD_DIM
            q, bias_g, sink_g = _swa_group_inputs(g, q_ref, bias_ref, sink_ref)
            e, _, inv = _swa_group_probs(q, kb[:, kl:kl + SWA_HEAD_DIM], bias_g, sink_g, mask)
            v = vb[:, kl:kl + SWA_HEAD_DIM]
            outs = [_d16(e[h] * inv[h], v, NN) for h in range(SWA_GROUP)]
            for h in range(SWA_GROUP):
                lo = (g * SWA_GROUP + h) * SWA_HEAD_DIM
                o_ref[:, lo:lo + SWA_HEAD_DIM] = outs[h].astype(bf16)

    return _call(body, (swa_in, swa_in, swa_in, swa_in, swa_in, bias, sinks), name=name, grid=(s // BLOCK,),
                 in_specs=_swa_specs(), out_specs=pl.BlockSpec((BLOCK, SWA_Q), lambda n: (n, 0)),
                 out_shape=jax.ShapeDtypeStruct((s, SWA_Q), bf16), semantics=("parallel",), side=side)


def swa_bwd(swa_in, bias, sinks, dout, *, name, side=None):
    s = swa_in.shape[0]

    def body(q_ref, kp_ref, kc_ref, vp_ref, vc_ref, bias_ref, sink_ref, do_ref,
             dq_ref, dkc_ref, dkp_ref, dvc_ref, dvp_ref, dbias_ref, dsink_ref):
        n = pl.program_id(0)

        @pl.when(n == 0)
        def _():
            dbias_ref[...] = jnp.zeros_like(dbias_ref)
            dsink_ref[...] = jnp.zeros_like(dsink_ref)

        mask = _swa_mask(n)
        kb = jnp.concatenate([kp_ref[...], kc_ref[...]], axis=0)
        vb = jnp.concatenate([vp_ref[...], vc_ref[...]], axis=0)
        lane = lax.broadcasted_iota(jnp.int32, (1, 128), 1)
        dsink = jnp.zeros((1, 128), f32)
        for g in range(SWA_KV_HEADS):
            kl = g * SWA_HEAD_DIM
            k, v = kb[:, kl:kl + SWA_HEAD_DIM], vb[:, kl:kl + SWA_HEAD_DIM]
            hs = range(SWA_GROUP)
            q, bias_g, sink_g = _swa_group_inputs(g, q_ref, bias_ref, sink_ref)
            do = [do_ref[:, (g * SWA_GROUP + h) * SWA_HEAD_DIM:(g * SWA_GROUP + h + 1) * SWA_HEAD_DIM] for h in hs]
            e, es, inv = _swa_group_probs(q, k, bias_g, sink_g, mask)
            p = [e[h] * inv[h] for h in hs]
            dp = [_d16(do[h], v, NT) for h in hs]
            delta = [jnp.sum(p[h] * dp[h], axis=-1, keepdims=True) for h in hs]
            ds = [p[h] * (dp[h] - delta[h]) for h in hs]
            dss = [ds[h] * SWA_SCALE for h in hs]
            dq = [_d16(dss[h], k, NN) for h in hs]
            dks = [_d16(dss[h], q[h], TN) for h in hs]
            dvs = [_d16(p[h], do[h], TN) for h in hs]
            dk, dv = sum(dks[1:], dks[0]), sum(dvs[1:], dvs[0])
            for h in hs:
                hh = g * SWA_GROUP + h
                dbias_ref[hh] += ds[h]
                dq_ref[:, hh * SWA_HEAD_DIM:(hh + 1) * SWA_HEAD_DIM] = dq[h]
                dsink = dsink + jnp.where(lane == hh, -jnp.sum(es[h] * inv[h] * delta[h], axis=0, keepdims=True), 0.0)
            dkp_ref[:, kl:kl + SWA_HEAD_DIM] = dk[:BLOCK]
            dkc_ref[:, kl:kl + SWA_HEAD_DIM] = dk[BLOCK:]
            dvp_ref[:, kl:kl + SWA_HEAD_DIM] = dv[:BLOCK]
            dvc_ref[:, kl:kl + SWA_HEAD_DIM] = dv[BLOCK:]
        dsink_ref[...] += dsink

    qs = pl.BlockSpec((BLOCK, SWA_Q), lambda n: (n, 0))
    ks = pl.BlockSpec((BLOCK, SWA_KV), lambda n: (n, 0))
    return _call(
        body, (swa_in, swa_in, swa_in, swa_in, swa_in, bias, sinks, dout), name=name, grid=(s // BLOCK,),
        in_specs=_swa_specs() + [qs],
        out_specs=[qs, ks, ks, ks, ks, pl.BlockSpec((SWA_HEADS, BLOCK, 2 * BLOCK), lambda n: (0, 0, 0)),
                   pl.BlockSpec((1, 128), lambda n: (0, 0))],
        out_shape=[jax.ShapeDtypeStruct((s, SWA_Q), f32)] + [jax.ShapeDtypeStruct((s, SWA_KV), f32)] * 4
        + [jax.ShapeDtypeStruct((SWA_HEADS, BLOCK, 2 * BLOCK), f32), jax.ShapeDtypeStruct((1, 128), f32)],
        semantics=("arbitrary",), side=side)


def swa_in_grad(dq, dkc, dkp, dvc, dvp, dba, *, name):
    s = dq.shape[0]
    nb = s // BLOCK

    def body(dq_ref, dkc_ref, dkp_ref, dvc_ref, dvp_ref, dba_ref, o_ref):
        has_next = pl.program_id(0) < nb - 1
        o_ref[:, :SWA_Q] = dq_ref[...].astype(bf16)
        o_ref[:, SWA_Q:SWA_Q + SWA_KV] = (dkc_ref[...] + jnp.where(has_next, dkp_ref[...], 0.0)).astype(bf16)
        o_ref[:, SWA_Q + SWA_KV:SWA_Q + 2 * SWA_KV] = (dvc_ref[...] + jnp.where(has_next, dvp_ref[...], 0.0)).astype(bf16)
        o_ref[:, SWA_Q + 2 * SWA_KV:] = dba_ref[...].astype(bf16)

    cur = pl.BlockSpec((BLOCK, SWA_KV), lambda n: (n, 0))
    nxt = pl.BlockSpec((BLOCK, SWA_KV), lambda n: (jnp.minimum(n + 1, nb - 1), 0))
    return pl.pallas_call(body, name=name, grid=(nb,),
                          in_specs=[pl.BlockSpec((BLOCK, SWA_Q), lambda n: (n, 0)), cur, nxt, cur, nxt, cur],
                          out_specs=pl.BlockSpec((BLOCK, SWA_IN_W), lambda n: (n, 0)),
                          out_shape=jax.ShapeDtypeStruct((s, SWA_IN_W), bf16),
                          compiler_params=_params("parallel"))(dq, dkc, dkp, dvc, dvp, dba)


def _bucket_onehot():
    qi = jnp.arange(BLOCK)[:, None]
    kj = jnp.arange(2 * BLOCK)[None, :]
    dist = jnp.maximum(qi + BLOCK - kj, 0)
    max_exact = REL_BUCKETS // 2
    dd = jnp.maximum(dist, 1).astype(f32)
    large = max_exact + (jnp.log(dd / max_exact) / math.log(REL_MAX_DIST / max_exact) * (REL_BUCKETS - max_exact)).astype(jnp.int32)
    bucket = jnp.where(dist < max_exact, dist, jnp.minimum(large, REL_BUCKETS - 1)).reshape(-1)
    return (bucket[None, :] == jnp.arange(REL_BUCKETS)[:, None]).astype(f32)


def _gbeta_fn(ba, alog_row, dt_row):
    col = lax.broadcasted_iota(jnp.int32, ba.shape, 1)
    v = ba + dt_row
    softplus = jnp.maximum(v, 0.0) + jnp.log(1.0 + jnp.exp(-jnp.abs(v)))
    g = -jnp.exp(alog_row) * softplus
    return jnp.where(col < GDN_HEADS, _sigmoid(ba), jnp.where(col < 2 * GDN_HEADS, g, 0.0))


def gbeta_fwd(swa_in, alog_row, dt_row, *, name):
    s = swa_in.shape[0]
    tm = _tile(s, 512, 8)

    def body(ba_ref, a_ref, d_ref, o_ref):
        o_ref[...] = _gbeta_fn(ba_ref[...], a_ref[...], d_ref[...])

    vec = pl.BlockSpec((1, 128), lambda i: (0, 0))
    return pl.pallas_call(body, name=name, grid=(s // tm,), in_specs=[pl.BlockSpec((tm, 128), lambda i: (i, BA_COL)), vec, vec],
                          out_specs=pl.BlockSpec((tm, 128), lambda i: (i, 0)), out_shape=jax.ShapeDtypeStruct((s, 128), f32),
                          compiler_params=_params("parallel"))(swa_in, alog_row, dt_row)


def gbeta_bwd(swa_in, alog_row, dt_row, dgbeta, *, name):
    s = swa_in.shape[0]
    tm = _tile(s, 512, 8)

    def body(ba_ref, a_ref, d_ref, dgb_ref, dba_ref, da_ref, dd_ref):
        @pl.when(pl.program_id(0) == 0)
        def _():
            da_ref[...] = jnp.zeros_like(da_ref)
            dd_ref[...] = jnp.zeros_like(dd_ref)

        _, pull = jax.vjp(_gbeta_fn, ba_ref[...], a_ref[...], d_ref[...])
        dba, da, dd = pull(dgb_ref[...])
        dba_ref[...] = dba
        da_ref[...] += da
        dd_ref[...] += dd

    vec = pl.BlockSpec((1, 128), lambda i: (0, 0))
    row = pl.BlockSpec((tm, 128), lambda i: (i, 0))
    return pl.pallas_call(body, name=name, grid=(s // tm,),
                          in_specs=[pl.BlockSpec((tm, 128), lambda i: (i, BA_COL)), vec, vec, row], out_specs=[row, vec, vec],
                          out_shape=[jax.ShapeDtypeStruct((s, 128), f32), jax.ShapeDtypeStruct((1, 128), f32),
                                     jax.ShapeDtypeStruct((1, 128), f32)],
                          compiler_params=_params("arbitrary"))(swa_in, alog_row, dt_row, dgbeta)


QKV_W = 3 * GDN_W


def gdn_pre_fwd(gdn_in, convw, *, name):
    s = gdn_in.shape[0]
    tm = _tile(s, 256, 16)
    hb = tm // HALO

    def body(x_ref, xp_ref, w_ref, q_ref, k_ref, v_ref):
        prev = jnp.where(pl.program_id(0) == 0, 0.0, xp_ref[...])
        pre = _causal_conv(_conv_taps(jnp.concatenate([prev, x_ref[...]], axis=0), GDN_CONV), w_ref)
        act = pre * _sigmoid(pre)
        for h in range(GDN_HEADS):
            lo = h * GDN_HEAD_DIM
            for off, o_ref in ((0, q_ref), (GDN_W, k_ref)):
                seg = act[:, off + lo:off + lo + GDN_HEAD_DIM]
                o_ref[:, lo:lo + GDN_HEAD_DIM] = seg * lax.rsqrt(jnp.sum(seg * seg, axis=-1, keepdims=True) + 1e-6)
        v_ref[...] = act[:, 2 * GDN_W:]

    out = pl.BlockSpec((tm, GDN_W), lambda i: (i, 0))
    return pl.pallas_call(body, name=name, grid=(s // tm,),
                          in_specs=[pl.BlockSpec((tm, QKV_W), lambda i: (i, 0)),
                                    pl.BlockSpec((HALO, QKV_W), lambda i: (jnp.maximum(i * hb - 1, 0), 0)),
                                    pl.BlockSpec((8, QKV_W), lambda i: (0, 0))],
                          out_specs=[out, out, out], out_shape=[jax.ShapeDtypeStruct((s, GDN_W), f32)] * 3,
                          compiler_params=_params("parallel"))(gdn_in, gdn_in, convw)


def gdn_pre_bwd(gdn_in, convw, dqn, dkn, dv, dgz, *, name):
    s = gdn_in.shape[0]
    tm = _tile(s, 128, 16)
    hb = tm // HALO
    nt = s // tm
    last_hb = s // HALO - 1

    def body(x_ref, xp_ref, xn_ref, w_ref, dq_ref, dqx_ref, dk_ref, dkx_ref, dv_ref, dvx_ref, dz_ref, o_ref, dw_ref):
        i = pl.program_id(0)
        first, last = i == 0, i == nt - 1

        @pl.when(first)
        def _():
            dw_ref[...] = jnp.zeros_like(dw_ref)

        ext = jnp.concatenate([jnp.where(first, 0.0, xp_ref[...]), x_ref[...], xn_ref[...]], axis=0)
        taps = _conv_taps(ext, GDN_CONV)
        pre = _causal_conv(taps, w_ref)
        act, dact_dpre = _silu_and_grad(pre)

        def with_future(t_ref, n_ref):
            return jnp.concatenate([t_ref[...], jnp.where(last, 0.0, n_ref[...])], axis=0)

        dqe, dke, dve = with_future(dq_ref, dqx_ref), with_future(dk_ref, dkx_ref), with_future(dv_ref, dvx_ref)
        parts = []
        for off, dn in ((0, dqe), (GDN_W, dke)):
            for h in range(GDN_HEADS):
                lo = h * GDN_HEAD_DIM
                seg = act[:, off + lo:off + lo + GDN_HEAD_DIM]
                r = lax.rsqrt(jnp.sum(seg * seg, axis=-1, keepdims=True) + 1e-6)
                nrm = seg * r
                dseg = dn[:, lo:lo + GDN_HEAD_DIM]
                parts.append(r * (dseg - nrm * jnp.sum(dseg * nrm, axis=-1, keepdims=True)))
        dpre = jnp.concatenate(parts + [dve], axis=1) * dact_dpre
        o_ref[:, :QKV_W] = _conv_input_grad(dpre, w_ref, GDN_CONV, tm).astype(bf16)
        o_ref[:, QKV_W:] = dz_ref[...].astype(bf16)
        dw_ref[...] += _rows_to_block(_conv_weight_grad(dpre[:tm], taps, tm), 8, QKV_W)

    row = pl.BlockSpec((tm, GDN_W), lambda i: (i, 0))
    fut = pl.BlockSpec((HALO, GDN_W), lambda i: (jnp.minimum((i + 1) * hb, last_hb), 0))
    return pl.pallas_call(
        body, name=name, grid=(nt,),
        in_specs=[pl.BlockSpec((tm, QKV_W), lambda i: (i, 0)),
                  pl.BlockSpec((HALO, QKV_W), lambda i: (jnp.maximum(i * hb - 1, 0), 0)),
                  pl.BlockSpec((HALO, QKV_W), lambda i: (jnp.minimum((i + 1) * hb, last_hb), 0)),
                  pl.BlockSpec((8, QKV_W), lambda i: (0, 0)), row, fut, row, fut, row, fut, row],
        out_specs=[pl.BlockSpec((tm, 4 * GDN_W), lambda i: (i, 0)), pl.BlockSpec((8, QKV_W), lambda i: (0, 0))],
        out_shape=[jax.ShapeDtypeStruct((s, 4 * GDN_W), bf16), jax.ShapeDtypeStruct((8, QKV_W), f32)],
        compiler_params=_params("arbitrary"),
    )(gdn_in, gdn_in, gdn_in, convw, dqn, dqn, dkn, dkn, dv, dv, dgz)


def _gdn_post_head(o, z, nw):
    return o * lax.rsqrt(jnp.mean(o * o, axis=-1, keepdims=True) + 1e-6) * nw * (z * _sigmoid(z))


def gdn_post_fwd(o, gdn_in, nw, *, name):
    s = o.shape[0]
    tm = _tile(s, 256, 16)

    def body(o_ref, z_ref, nw_ref, y_ref):
        for h in range(GDN_HEADS):
            sl = slice(h * GDN_HEAD_DIM, (h + 1) * GDN_HEAD_DIM)
            y_ref[:, sl] = _gdn_post_head(o_ref[:, sl], z_ref[:, sl], nw_ref[...]).astype(bf16)

    row = pl.BlockSpec((tm, GDN_W), lambda i: (i, 0))
    return pl.pallas_call(body, name=name, grid=(s // tm,),
                          in_specs=[row, pl.BlockSpec((tm, GDN_W), lambda i: (i, 3)), pl.BlockSpec((1, 128), lambda i: (0, 0))],
                          out_specs=row, out_shape=jax.ShapeDtypeStruct((s, GDN_W), bf16),
                          compiler_params=_params("parallel"))(o, gdn_in, nw)


def gdn_post_bwd(o, gdn_in, nw, dy, *, name):
    s = o.shape[0]
    tm = _tile(s, 256, 16)

    def body(o_ref, z_ref, nw_ref, dy_ref, do_ref, dz_ref, dnw_ref):
        @pl.when(pl.program_id(0) == 0)
        def _():
            dnw_ref[...] = jnp.zeros_like(dnw_ref)

        dnw = jnp.zeros((1, 128), f32)
        for h in range(GDN_HEADS):
            sl = slice(h * GDN_HEAD_DIM, (h + 1) * GDN_HEAD_DIM)
            _, pull = jax.vjp(_gdn_post_head, o_ref[:, sl], z_ref[:, sl], nw_ref[...])
            do, dz, dn = pull(dy_ref[:, sl])
            do_ref[:, sl] = do
            dz_ref[:, sl] = dz
            dnw = dnw + dn
        dnw_ref[...] += dnw

    row = pl.BlockSpec((tm, GDN_W), lambda i: (i, 0))
    vec = pl.BlockSpec((1, 128), lambda i: (0, 0))
    return pl.pallas_call(body, name=name, grid=(s // tm,),
                          in_specs=[row, pl.BlockSpec((tm, GDN_W), lambda i: (i, 3)), vec, row], out_specs=[row, row, vec],
                          out_shape=[jax.ShapeDtypeStruct((s, GDN_W), f32), jax.ShapeDtypeStruct((s, GDN_W), f32),
                                     jax.ShapeDtypeStruct((1, 128), f32)],
                          compiler_params=_params("arbitrary"))(o, gdn_in, nw, dy)


def _dot_high(a, b, dims=NN):
    return lax.dot_general(a, b, (dims, ((), ())), precision=lax.Precision.HIGH, preferred_element_type=f32)


@jax.custom_vjp
def _unit_lower_inverses(a):
    c = a[0].shape[0]
    n = range(len(a))
    eye = (lax.broadcasted_iota(jnp.int32, (c, c), 0) == lax.broadcasted_iota(jnp.int32, (c, c), 1)).astype(f32)
    inv = [eye - a[i] for i in n]
    pw = [_dot_high(a[i], a[i]) for i in n]
    width = 2
    while width < c:
        inv = [inv[i] + _dot_high(inv[i], pw[i]) for i in n]
        width *= 2
        if width < c:
            pw = [_dot_high(pw[i], pw[i]) for i in n]
    return inv


def _unit_lower_inverses_fwd(a):
    inv = _unit_lower_inverses(a)
    return inv, inv


def _unit_lower_inverses_bwd(inv, g):
    return ([-_dot_high(_dot_high(x, gx, TN), x, NT) for x, gx in zip(inv, g)],)


_unit_lower_inverses.defvjp(_unit_lower_inverses_fwd, _unit_lower_inverses_bwd)


def _gdn_chunks(q, k, v, gb, state):
    c = GDN_CHUNK
    heads = range(len(q))
    r = lax.broadcasted_iota(jnp.int32, (c, c), 0)
    cc = lax.broadcasted_iota(jnp.int32, (c, c), 1)
    tril, strict = r >= cc, r > cc
    eye = (r == cc).astype(f32)

    def dhi(a, b):
        return jnp.dot(a, b, precision=lax.Precision.HIGH, preferred_element_type=f32)

    beta = [gb[:, h:h + 1] for h in heads]
    cum_cols = dhi(tril.astype(f32), gb)
    cum_rows = dhi(gb.T, (r <= cc).astype(f32))
    gi = [jnp.broadcast_to(cum_cols[:, GDN_HEADS + h:GDN_HEADS + h + 1], (c, c)) for h in heads]
    gj = [jnp.broadcast_to(cum_rows[GDN_HEADS + h:GDN_HEADS + h + 1, :], (c, c)) for h in heads]
    decay = [jnp.where(tril, jnp.exp(jnp.where(tril, gi[h] - gj[h], 0.0)), 0.0) for h in heads]
    kb = [k[h] * beta[h] for h in heads]
    vb = [v[h] * beta[h] for h in heads]
    a = [jnp.where(strict, _d16(kb[h], k[h], NT) * decay[h], 0.0) for h in heads]
    tinv = _unit_lower_inverses(a)
    gc = [gi[h][:, 0:1] for h in heads]
    egc = [jnp.exp(gc[h]) for h in heads]
    u = [dhi(tinv[h], vb[h]) for h in heads]
    w = [dhi(tinv[h], kb[h] * egc[h]) for h in heads]
    qs = [q[h] * (GDN_HEAD_DIM ** -0.5) for h in heads]
    attn = [jnp.where(tril, _d16(qs[h], k[h], NT) * decay[h], 0.0) for h in heads]
    g_last = [gi[h][c - 1:c, 0:1] for h in heads]
    v_new = [u[h] - _d16(w[h], state[h], NN) for h in heads]
    out = [_d16(qs[h] * egc[h], state[h], NN) + _d16(attn[h], v_new[h], NN) for h in heads]
    new_state = [state[h] * jnp.exp(g_last[h]) + _d16(k[h] * jnp.exp(g_last[h] - gc[h]), v_new[h], TN) for h in heads]
    return out, new_state


def _head_cols(ref):
    return [ref[:, h * GDN_HEAD_DIM:(h + 1) * GDN_HEAD_DIM] for h in range(GDN_HEADS)]


def gdn_scan_fwd(qn, kn, v, gbeta, *, name, side=None):
    s = qn.shape[0]
    nc = s // GDN_CHUNK

    def body(q_ref, k_ref, v_ref, gb_ref, o_ref, st_ref, state_ref):
        @pl.when(pl.program_id(0) == 0)
        def _():
            state_ref[...] = jnp.zeros_like(state_ref)

        states = [state_ref[h] for h in range(GDN_HEADS)]
        outs, new = _gdn_chunks(_head_cols(q_ref), _head_cols(k_ref), _head_cols(v_ref), gb_ref[...], states)
        for h in range(GDN_HEADS):
            st_ref[0, h] = states[h]
            o_ref[:, h * GDN_HEAD_DIM:(h + 1) * GDN_HEAD_DIM] = outs[h]
            state_ref[h] = new[h]

    row = pl.BlockSpec((GDN_CHUNK, GDN_W), lambda n: (n, 0))
    return _call(
        body, (qn, kn, v, gbeta), name=name, grid=(nc,),
        in_specs=[row, row, row, pl.BlockSpec((GDN_CHUNK, 128), lambda n: (n, 0))],
        out_specs=[row, pl.BlockSpec((1, GDN_HEADS, GDN_HEAD_DIM, GDN_HEAD_DIM), lambda n: (n, 0, 0, 0))],
        out_shape=[jax.ShapeDtypeStruct((s, GDN_W), f32),
                   jax.ShapeDtypeStruct((nc, GDN_HEADS, GDN_HEAD_DIM, GDN_HEAD_DIM), f32)],
        scratch_shapes=[pltpu.VMEM((GDN_HEADS, GDN_HEAD_DIM, GDN_HEAD_DIM), f32)], semantics=("arbitrary",), side=side)


def gdn_scan_bwd(qn, kn, v, gbeta, states, dout, *, name, side=None):
    s = qn.shape[0]
    nc = s // GDN_CHUNK

    def body(q_ref, k_ref, v_ref, gb_ref, st_ref, do_ref, dq_ref, dk_ref, dv_ref, dgb_ref, dstate_ref):
        @pl.when(pl.program_id(0) == 0)
        def _():
            dstate_ref[...] = jnp.zeros_like(dstate_ref)

        _, pull = jax.vjp(_gdn_chunks, _head_cols(q_ref), _head_cols(k_ref), _head_cols(v_ref), gb_ref[...],
                          [st_ref[0, h] for h in range(GDN_HEADS)])
        dq, dk, dv, dgb, dst = pull((_head_cols(do_ref), [dstate_ref[h] for h in range(GDN_HEADS)]))
        for h in range(GDN_HEADS):
            sl = slice(h * GDN_HEAD_DIM, (h + 1) * GDN_HEAD_DIM)
            dq_ref[:, sl] = dq[h]
            dk_ref[:, sl] = dk[h]
            dv_ref[:, sl] = dv[h]
            dstate_ref[h] = dst[h]
        dgb_ref[...] = dgb

    row = pl.BlockSpec((GDN_CHUNK, GDN_W), lambda n: (nc - 1 - n, 0))
    gb = pl.BlockSpec((GDN_CHUNK, 128), lambda n: (nc - 1 - n, 0))
    return _call(
        body, (qn, kn, v, gbeta, states, dout), name=name, grid=(nc,),
        in_specs=[row, row, row, gb, pl.BlockSpec((1, GDN_HEADS, GDN_HEAD_DIM, GDN_HEAD_DIM), lambda n: (nc - 1 - n, 0, 0, 0)), row],
        out_specs=[row, row, row, gb],
        out_shape=[jax.ShapeDtypeStruct((s, GDN_W), f32)] * 3 + [jax.ShapeDtypeStruct((s, 128), f32)],
        scratch_shapes=[pltpu.VMEM((GDN_HEADS, GDN_HEAD_DIM, GDN_HEAD_DIM), f32)], semantics=("arbitrary",), side=side)


def adamw(w, g, m, v, *, name):
    r, c = w.shape
    tr = _tile(r, 256, 8)

    def body(w_ref, g_ref, m_ref, v_ref, d_ref, nm_ref, nv_ref):
        gv = g_ref[...]
        nm = ADAM_B1 * m_ref[...] + (1.0 - ADAM_B1) * gv
        nv = ADAM_B2 * v_ref[...] + (1.0 - ADAM_B2) * (gv * gv)
        m_hat = nm / (1.0 - ADAM_B1 ** ADAM_STEP)
        v_hat = nv / (1.0 - ADAM_B2 ** ADAM_STEP)
        d_ref[...] = -ADAM_LR * (m_hat / (jnp.sqrt(v_hat) + ADAM_EPS) + ADAM_WD * w_ref[...])
        nm_ref[...] = nm
        nv_ref[...] = nv

    spec = pl.BlockSpec((tr, c), lambda i: (i, 0))
    return pl.pallas_call(body, name=name, grid=(r // tr,), in_specs=[spec] * 4, out_specs=[spec] * 3,
                          out_shape=[jax.ShapeDtypeStruct((r, c), f32)] * 3, compiler_params=_params("parallel"))(w, g, m, v)


def _pos():
    return lax.axis_index("x"), lax.axis_index("y"), lax.axis_index("c")


ANY = pl.BlockSpec(memory_space=pl.ANY)


class Side(NamedTuple):
    ins: list
    outs: list
    aliases: dict
    sems: list
    start: Callable
    wait: Callable


def join_sides(*sides):
    def spans(key):
        out, off = [], 0
        for sd in sides:
            out.append(slice(off, off + len(getattr(sd, key))))
            off += len(getattr(sd, key))
        return out

    i_sp, o_sp, s_sp = spans("ins"), spans("outs"), spans("sems")
    aliases = {i_sp[n].start + i: o_sp[n].start + o for n, sd in enumerate(sides) for i, o in sd.aliases.items()}

    def each(what):
        def run(ins, outs, sems):
            for n, sd in enumerate(sides):
                getattr(sd, what)(ins[i_sp[n]], outs[o_sp[n]], sems[s_sp[n]])
        return run

    return Side([a for sd in sides for a in sd.ins], [o for sd in sides for o in sd.outs], aliases,
                [s for sd in sides for s in sd.sems], each("start"), each("wait"))


def _side_body(body, side, n_in, n_out, n_scratch, grid):
    ns_in, ns_out = len(side.ins), len(side.outs)

    def wrapped(*refs):
        cut = [n_in, ns_in, n_out, ns_out, n_scratch]
        parts, off = [], 0
        for c in cut:
            parts.append(refs[off:off + c])
            off += c
        ins, s_ins, outs, s_outs, scratch = parts
        sems = refs[off:]
        if grid:
            ids = [pl.program_id(d) for d in range(len(grid))]
            first = functools.reduce(jnp.logical_and, [i == 0 for i in ids])
            last = functools.reduce(jnp.logical_and, [i == g - 1 for i, g in zip(ids, grid)])
            pl.when(first)(lambda: side.start(s_ins, s_outs, sems))
            body(*ins, *outs, *scratch)
            pl.when(last)(lambda: side.wait(s_ins, s_outs, sems))
        else:
            side.start(s_ins, s_outs, sems)
            side.wait(s_ins, s_outs, sems)

    return wrapped


def _call(body, args, *, name, grid, in_specs, out_specs, out_shape, semantics, scratch_shapes=(), side=None):
    if side is None:
        return pl.pallas_call(body, name=name, grid=grid, in_specs=in_specs, out_specs=out_specs, out_shape=out_shape,
                              scratch_shapes=list(scratch_shapes), compiler_params=_params(*semantics))(*args)
    single = not isinstance(out_shape, (list, tuple))
    shapes, specs = ([out_shape], [out_specs]) if single else (list(out_shape), list(out_specs))
    n_in, n_out = len(in_specs), len(shapes)
    res = pl.pallas_call(
        _side_body(body, side, n_in, n_out, len(scratch_shapes), grid), name=name, grid=grid,
        in_specs=list(in_specs) + [ANY] * len(side.ins), out_specs=specs + [ANY] * len(side.outs),
        out_shape=shapes + list(side.outs), scratch_shapes=list(scratch_shapes) + list(side.sems),
        input_output_aliases={n_in + i: n_out + o for i, o in side.aliases.items()},
        compiler_params=_params(*(["arbitrary"] * len(grid))),
    )(*args, *side.ins)
    return (res[0] if single else res[:n_out]), list(res[n_out:])


def run_side(side, *, name):
    return pl.pallas_call(_side_body(None, side, 0, 0, 0, ()), name=name, in_specs=[ANY] * len(side.ins),
                          out_specs=[ANY] * len(side.outs), out_shape=list(side.outs), scratch_shapes=list(side.sems),
                          input_output_aliases=dict(side.aliases))(*side.ins)


def _remote(src, dst, send, recv, k, to):
    return pltpu.make_async_remote_copy(src_ref=src, dst_ref=dst, send_sem=send.at[k], recv_sem=recv.at[k], device_id=to,
                                        device_id_type=MESH)


def gather_first(shards):
    na = len(shards)

    def copies(x_refs, out_refs, sems):
        send, recv, local = sems
        x, y, cc = _pos()
        me = 4 * x + 2 * y + cc
        peers = [(x, y, 1 - cc), (1 - x, y, cc), (x, 1 - y, cc), (1 - x, 1 - y, cc)]
        mine = [pltpu.make_async_copy(x_refs[a], out_refs[a].at[me], local.at[a]) for a in range(na)]
        sent = [_remote(x_refs[a], out_refs[a].at[me], send, recv, 4 * a + k, p) for a in range(na) for k, p in enumerate(peers)]
        landed = [_remote(x_refs[a], out_refs[a].at[4 * p[0] + 2 * p[1] + p[2]], send, recv, 4 * a + k, p)
                  for a in range(na) for k, p in enumerate(peers)]
        return mine, sent, landed

    def start(x_refs, out_refs, sems):
        mine, sent, _ = copies(x_refs, out_refs, sems)
        for cp in mine + sent:
            cp.start()

    def wait(x_refs, out_refs, sems):
        mine, sent, landed = copies(x_refs, out_refs, sems)
        for cp in sent:
            cp.wait_send()
        for cp in landed:
            cp.wait_recv()
        for cp in mine:
            cp.wait()

    return Side(list(shards), [jax.ShapeDtypeStruct((N_DEV,) + s.shape, s.dtype) for s in shards], {},
                [pltpu.SemaphoreType.DMA((4 * na,)), pltpu.SemaphoreType.DMA((4 * na,)), pltpu.SemaphoreType.DMA((na,))],
                start, wait)


def gather_second(slots):
    na = len(slots)

    def copies(out_refs, sems):
        send, recv = sems
        x, y, cc = _pos()
        chips = [(1 - x, y), (x, 1 - y), (1 - x, 1 - y)]
        sent, landed = [], []
        for a in range(na):
            for j, (px, py) in enumerate(chips):
                row = out_refs[a].at[4 * px + 2 * py + cc]
                sent.append(_remote(row, row, send, recv, 3 * a + j, (x, y, 1 - cc)))
                landed.append(_remote(row, out_refs[a].at[4 * px + 2 * py + 1 - cc], send, recv, 3 * a + j, (x, y, 1 - cc)))
        return sent, landed

    def start(_, out_refs, sems):
        for cp in copies(out_refs, sems)[0]:
            cp.start()

    def wait(_, out_refs, sems):
        sent, landed = copies(out_refs, sems)
        for cp in sent:
            cp.wait_send()
        for cp in landed:
            cp.wait_recv()

    return Side(list(slots), [jax.ShapeDtypeStruct(s.shape, s.dtype) for s in slots], {a: a for a in range(na)},
                [pltpu.SemaphoreType.DMA((3 * na,)), pltpu.SemaphoreType.DMA((3 * na,))], start, wait)


def grad_to_sibling(chunks):
    na = len(chunks)

    def start(g_refs, out_refs, sems):
        send, recv = sems
        x, y, cc = _pos()
        for a in range(na):
            for q in range(4):
                _remote(g_refs[a].at[2 * q + 1 - cc], out_refs[a].at[q], send, recv, a, (x, y, 1 - cc)).start()

    def wait(g_refs, out_refs, sems):
        send, recv = sems
        x, y, cc = _pos()
        for a in range(na):
            _remote(out_refs[a], out_refs[a], send, recv, a, (x, y, 1 - cc)).wait()

    return Side(list(chunks), [jax.ShapeDtypeStruct((4,) + g.shape[1:], g.dtype) for g in chunks], {},
                [pltpu.SemaphoreType.DMA((na,)), pltpu.SemaphoreType.DMA((na,))], start, wait)


def grad_to_chips(parts):
    na = len(parts)

    def copies(p_refs, out_refs, sems):
        send, recv, local = sems
        x, y, cc = _pos()
        chips = [(1 - x, y), (x, 1 - y), (1 - x, 1 - y)]
        mine = [pltpu.make_async_copy(p_refs[a].at[2 * x + y], out_refs[a].at[3], local.at[a]) for a in range(na)]
        sent = [_remote(p_refs[a].at[2 * px + py], out_refs[a].at[k], send, recv, 3 * a + k, (px, py, cc))
                for a in range(na) for k, (px, py) in enumerate(chips)]
        return mine, sent

    def start(p_refs, out_refs, sems):
        mine, sent = copies(p_refs, out_refs, sems)
        for cp in mine + sent:
            cp.start()

    def wait(p_refs, out_refs, sems):
        mine, sent = copies(p_refs, out_refs, sems)
        for cp in sent:
            cp.wait()
        for cp in mine:
            cp.wait()

    return Side(list(parts), [jax.ShapeDtypeStruct(p.shape, p.dtype) for p in parts], {},
                [pltpu.SemaphoreType.DMA((3 * na,)), pltpu.SemaphoreType.DMA((3 * na,)), pltpu.SemaphoreType.DMA((na,))],
                start, wait)


def add_sibling(chunks, recv, *, name):
    _, r, c = chunks.shape
    tr = r if r <= 1024 else _tile(r, 512, 16)
    core = lax.axis_index("c").astype(jnp.int32).reshape(1)

    def body(core_ref, a_ref, b_ref, o_ref):
        o_ref[...] = (a_ref[...] + b_ref[...]).astype(bf16)

    return pl.pallas_call(
        body, name=name,
        grid_spec=pltpu.PrefetchScalarGridSpec(
            num_scalar_prefetch=1, grid=(4, r // tr),
            in_specs=[pl.BlockSpec((1, tr, c), lambda q, i, core_ref: (2 * q + core_ref[0], i, 0)),
                      pl.BlockSpec((1, tr, c), lambda q, i, core_ref: (q, i, 0))],
            out_specs=pl.BlockSpec((1, tr, c), lambda q, i, core_ref: (q, i, 0))),
        out_shape=jax.ShapeDtypeStruct((4, r, c), bf16), compiler_params=_params("parallel", "parallel"),
    )(core, chunks, recv)


def add_four(r4, *, name):
    _, r, c = r4.shape
    tr = r if r <= 1024 else _tile(r, 512, 16)

    def body(a_ref, o_ref):
        o_ref[...] = ((a_ref[3].astype(f32) + a_ref[0].astype(f32)) + a_ref[1].astype(f32)) + a_ref[2].astype(f32)

    return pl.pallas_call(body, name=name, grid=(r // tr,), in_specs=[pl.BlockSpec((4, tr, c), lambda i: (0, i, 0))],
                          out_specs=pl.BlockSpec((tr, c), lambda i: (i, 0)), out_shape=jax.ShapeDtypeStruct((r, c), f32),
                          compiler_params=_params("parallel"))(r4)


def all_reduce_small(vec, *, name):
    r, c = vec.shape

    def body(v_ref, out_ref, buf_ref, send_sems, recv_sems):
        x, y, cc = _pos()
        my_id = 4 * x + 2 * y + cc
        buf_ref[my_id] = v_ref[...]
        flips = [(fx, fy, fc) for fx in (0, 1) for fy in (0, 1) for fc in (0, 1)][1:]
        cps = []
        for k, (fx, fy, fc) in enumerate(flips):
            peer = ((1 - x) if fx else x, (1 - y) if fy else y, (1 - cc) if fc else cc)
            cps.append(pltpu.make_async_remote_copy(src_ref=v_ref, dst_ref=buf_ref.at[my_id], send_sem=send_sems.at[k],
                                                    recv_sem=recv_sems.at[k], device_id=peer, device_id_type=MESH))
        for cp in cps:
            cp.start()
        for cp in cps:
            cp.wait()
        acc = buf_ref[0]
        for d in range(1, N_DEV):
            acc = acc + buf_ref[d]
        out_ref[...] = acc

    vm = pl.BlockSpec(memory_space=pltpu.VMEM)
    return pl.pallas_call(body, name=name, in_specs=[vm], out_specs=vm, out_shape=jax.ShapeDtypeStruct((r, c), f32),
                          scratch_shapes=[pltpu.VMEM((N_DEV, r, c), f32), pltpu.SemaphoreType.DMA((7,)),
                                          pltpu.SemaphoreType.DMA((7,))])(vec)


def _pack(parts, rows, dtype):
    flat = jnp.concatenate([p.reshape(-1).astype(dtype) for p in parts])
    return jnp.pad(flat, (0, rows * PACK_COLS - flat.shape[0])).reshape(rows, PACK_COLS)


def _unpack(flat, shapes):
    out, off = [], 0
    for shp in shapes:
        n = shp[0] * shp[1]
        out.append(flat[..., off:off + n].reshape(flat.shape[:-1] + tuple(shp)))
        off += n
    return out


def _from_column_shards(g):
    _, r, c = g.shape
    return jnp.transpose(g, (1, 0, 2)).reshape(r, N_DEV * c)


def _column_shards(full):
    r, c8 = full.shape
    return jnp.transpose(full.reshape(r, N_DEV, c8 // N_DEV), (1, 0, 2))


W_IN_SHARD = IN_DIM // N_DEV
W_IN_PAD = 1280
W_IN_PARTS = (("swa", 0, 0, 1280), ("swa", 1280, 5376, 5392), ("gdn", 0, 1280, 5376), ("gates", 0, 5392, IN_DIM))
W_IN_WIDTHS = {"swa": SWA_IN_W, "gdn": 4 * GDN_W, "gates": 2 * D_MODEL}


def _w_in_segments():
    segs = []
    for part, p0, g0, g1 in W_IN_PARTS:
        for j in range(N_DEV):
            lo, hi = max(g0, W_IN_SHARD * j), min(g1, W_IN_SHARD * (j + 1))
            if lo < hi:
                segs.append((part, p0 + lo - g0, j, lo - W_IN_SHARD * j, hi - lo))
    return segs


def split_w_in(shards, *, name):
    dt = shards.dtype
    tm = 256

    def body(w_ref, swa_ref, gdn_ref, gates_ref):
        out = {"swa": swa_ref, "gdn": gdn_ref, "gates": gates_ref}
        swa_ref[:, SWA_Q + 2 * SWA_KV + 2 * GDN_HEADS:] = jnp.zeros((tm, SWA_IN_W - SWA_Q - 2 * SWA_KV - 2 * GDN_HEADS), dt)
        for part, p0, j, l0, n in _w_in_segments():
            out[part][:, p0:p0 + n] = w_ref[j, :, l0:l0 + n]

    return pl.pallas_call(body, name=name, grid=(D_MODEL // tm,),
                          in_specs=[pl.BlockSpec((N_DEV, tm, W_IN_PAD), lambda i: (0, i, 0))],
                          out_specs=[pl.BlockSpec((tm, W_IN_WIDTHS[p]), lambda i: (i, 0)) for p in ("swa", "gdn", "gates")],
                          out_shape=[jax.ShapeDtypeStruct((D_MODEL, W_IN_WIDTHS[p]), dt) for p in ("swa", "gdn", "gates")],
                          compiler_params=_params("parallel"))(shards)


def merge_w_in_grad(d_swa, d_gdn, d_gates, *, name):
    tm = 256

    def body(swa_ref, gdn_ref, gates_ref, w_ref):
        src = {"swa": swa_ref, "gdn": gdn_ref, "gates": gates_ref}
        w_ref[:, :, W_IN_SHARD:] = jnp.zeros((N_DEV, tm, W_IN_PAD - W_IN_SHARD), f32)
        for part, p0, j, l0, n in _w_in_segments():
            w_ref[j, :, l0:l0 + n] = src[part][:, p0:p0 + n]

    return pl.pallas_call(body, name=name, grid=(D_MODEL // tm,),
                          in_specs=[pl.BlockSpec((tm, W_IN_WIDTHS[p]), lambda i: (i, 0)) for p in ("swa", "gdn", "gates")],
                          out_specs=pl.BlockSpec((N_DEV, tm, W_IN_PAD), lambda i: (0, i, 0)),
                          out_shape=jax.ShapeDtypeStruct((N_DEV, D_MODEL, W_IN_PAD), f32),
                          compiler_params=_params("parallel"))(d_swa, d_gdn, d_gates)


def kernel(x, mem, w_in, rel_bias, swa_sinks, gdn_conv_w, gdn_a_log, gdn_dt_bias, gdn_norm_w, w_br_swa, w_br_gdn, w_mix_o, ln1_g, ln1_b, w_mem_q, w_mem_kv, w_mem_o, ln2_g, ln2_b, w_up, ffn_conv_w, ffn_conv_b, w_down, ln3_g, ln3_b, loss_target, m_w_in, m_rel_bias, m_swa_sinks, m_gdn_conv_w, m_gdn_a_log, m_gdn_dt_bias, m_gdn_norm_w, m_w_br_swa, m_w_br_gdn, m_w_mix_o, m_ln1_g, m_ln1_b, m_w_mem_q, m_w_mem_kv, m_w_mem_o, m_ln2_g, m_ln2_b, m_w_up, m_ffn_conv_w, m_ffn_conv_b, m_w_down, m_ln3_g, m_ln3_b, v_w_in, v_rel_bias, v_swa_sinks, v_gdn_conv_w, v_gdn_a_log, v_gdn_dt_bias, v_gdn_norm_w, v_w_br_swa, v_w_br_gdn, v_w_mix_o, v_ln1_g, v_ln1_b, v_w_mem_q, v_w_mem_kv, v_w_mem_o, v_ln2_g, v_ln2_b, v_w_up, v_ffn_conv_w, v_ffn_conv_b, v_w_down, v_ln3_g, v_ln3_b):
    env = dict(locals())
    w2 = {n: (env[n][0] if env[n].ndim == 3 else env[n]) for n in WEIGHTS}
    m2 = {n: (env["m_" + n][0] if env["m_" + n].ndim == 3 else env["m_" + n]) for n in WEIGHTS}
    v2 = {n: (env["v_" + n][0] if env["v_" + n].ndim == 3 else env["v_" + n]) for n in WEIGHTS}
    xs, mems, target = x[0], mem[0], loss_target[0]
    my_id = 4 * lax.axis_index("x") + 2 * lax.axis_index("y") + lax.axis_index("c")
    pad_ff = FF_PAD - FF_SHARD

    pad_cols = {"w_in": W_IN_PAD - W_IN_SHARD, "w_up": pad_ff}
    mid = ("w_br_swa", "w_br_gdn", "w_mem_o", "w_mix_o", "w_mem_q", "w_mem_kv")
    mine = {n: jnp.pad(w2[n], ((0, 0), (0, pad_cols.get(n, 0)))).astype(bf16) for n in ("w_in", "w_up", "w_down") + mid}
    got_in = run_side(gather_second(run_side(gather_first([mine["w_in"]]), name="gather_w_in")), name="gather_w_in_pass_on")
    w_swa, w_gdn, w_gates = split_w_in(got_in[0], name="split_w_in")
    n_ffn, n_gdn = 3 * FF_SHARD, GDN_CONV * (QKV_W // N_DEV)
    conv_mine = jnp.concatenate([w2["ffn_conv_w"].reshape(-1), w2["gdn_conv_w"].reshape(-1)])[None]
    conv_rows = lax.dynamic_update_slice(jnp.zeros((N_DEV, n_ffn + n_gdn), f32), conv_mine, (my_id, 0))
    conv_all = all_reduce_small(_pack([conv_rows], CONV_ROWS, f32), name="gather_conv_w")
    conv_all = conv_all.reshape(-1)[:N_DEV * (n_ffn + n_gdn)].reshape(N_DEV, n_ffn + n_gdn)
    cwb = jnp.concatenate([conv_all[:, :n_ffn].reshape(N_DEV, 3, FF_SHARD), w2["ffn_conv_b"].reshape(N_DEV, 1, FF_SHARD),
                           jnp.zeros((N_DEV, 4, FF_SHARD), f32)], axis=1)
    cwb = jnp.pad(cwb, ((0, 0), (0, 0), (0, pad_ff)))
    convw = jnp.transpose(conv_all[:, n_ffn:].reshape(N_DEV, GDN_CONV, QKV_W // N_DEV), (1, 0, 2)).reshape(GDN_CONV, QKV_W)
    convw = jnp.pad(convw, ((0, 4), (0, 0)))
    onehot = _bucket_onehot()
    bias = mm(w2["rel_bias"].T, onehot, "nn", hi=True, tn=4096, name="rel_bias_table").reshape(SWA_HEADS, BLOCK, 2 * BLOCK)
    alog_row = jnp.pad(w2["gdn_a_log"], ((0, 0), (GDN_HEADS, 128 - 2 * GDN_HEADS)))
    dt_row = jnp.pad(w2["gdn_dt_bias"], ((0, 0), (GDN_HEADS, 128 - 2 * GDN_HEADS)))

    xb = cast_bf16(xs, name="cast_x")
    memb = cast_bf16(mems, name="cast_mem")
    gates, mid_got = mm(xb, w_gates, "nn", name="proj_gates", side=gather_first([mine[n] for n in mid]))
    gdn_in, mid_got = mm(xb, w_gdn, "nn", name="proj_gdn", side=gather_second(mid_got))
    got = dict(zip(mid, mid_got))
    w_br_swa, w_br_gdn, w_mem_o = (_from_column_shards(got[n]) for n in ("w_br_swa", "w_br_gdn", "w_mem_o"))
    w_mix_o = got["w_mix_o"].reshape(D_MODEL, D_MODEL)
    w_mem_q = got["w_mem_q"].reshape(D_MODEL, MEM_W)
    w_mem_kv = got["w_mem_kv"].reshape(D_MODEL, 2 * MEM_W)
    swa_in = mm(xb, w_swa, "nn", tn=SWA_IN_W, name="proj_swa")
    attn, down_got = swa_fwd(swa_in, bias, w2["swa_sinks"], name="swa_fwd", side=gather_first([mine["w_down"]]))
    qn, kn, vv = gdn_pre_fwd(gdn_in, convw, name="gdn_pre_fwd")
    gbeta = gbeta_fwd(swa_in, alog_row, dt_row, name="gbeta_fwd")
    (o_gdn, states), up_got = gdn_scan_fwd(qn, kn, vv, gbeta, name="gdn_scan_fwd", side=gather_first([mine["w_up"]]))
    ygd = gdn_post_fwd(o_gdn, gdn_in, w2["gdn_norm_w"], name="gdn_post_fwd")
    y_swa, down_got = mm(attn, w_br_swa, "nn", name="br_swa", side=gather_second(down_got))
    y_gdn, up_got = mm(ygd, w_br_gdn, "nn", name="br_gdn", side=gather_second(up_got))
    w_up_blocked = up_got[0]
    w_down_p = jnp.pad(down_got[0].reshape(4, FF_SHARD, D_MODEL), ((0, 0), (0, pad_ff), (0, 0))).reshape(4 * FF_PAD, D_MODEL)
    mixed = merge_fwd(gates, y_swa, y_gdn, name="merge_fwd")
    z1 = mm(mixed, w_mix_o, "nn", add=xs, add_scale=ALPHA, name="mix_o")
    x1, x1b = ln_fwd(z1, w2["ln1_g"], w2["ln1_b"], name="ln1_fwd")
    qm = mm(x1b, w_mem_q, "nn", name="mem_q")
    kv = mm(memb, w_mem_kv, "nn", name="mem_kv")
    om = memattn_fwd(qm, kv, name="memattn_fwd")
    z2 = mm(om, w_mem_o, "nn", add=x1, add_scale=ALPHA, name="mem_o")
    x2, x2b = ln_fwd(z2, w2["ln2_g"], w2["ln2_b"], name="ln2_fwd")
    hpre = mm(x2b, w_up_blocked, "nn", b_blocked=True, name="ffn_up")
    act = ffn_act_fwd(hpre, cwb, name="ffn_act_fwd")
    z3 = mm(act, w_down_p, "nn", add=x2, add_scale=ALPHA, tk=2 * FF_PAD, name="ffn_down")
    dz3, dz3b, d_ln3g, d_ln3b, loss = ln_loss(z3, target, w2["ln3_g"], w2["ln3_b"], name="ln3_loss")

    dact = mm(dz3b, w_down_p, "nt", tn=FF_PAD, name="d_act")
    d_wdown_p = mm(act, dz3b, "tn", tm=FF_PAD, name="dw_down")
    d_hpre, d_cwb = ffn_act_bwd(hpre, dact, cwb, name="ffn_act_bwd")
    def sibling_sums(names, chunks, received):
        return [add_sibling(c, r, name="grad_add_sibling_" + n) for n, c, r in zip(names, chunks, received)]

    def chip_sums(names, received):
        return [add_four(r, name="grad_add_chips_" + n) for n, r in zip(names, received)]

    dx2 = mm(d_hpre, w_up_blocked, "nt", b_blocked=True, add=dz3, add_scale=ALPHA, name="d_x2")
    d_wup = mm(x2b, d_hpre, "tn", out_blocked=True, name="dw_up")
    ffn = ("w_up", "w_down")
    ffn_chunks = [d_wup, d_wdown_p.reshape(4, FF_PAD, D_MODEL)[:, :FF_SHARD].reshape(N_DEV, FF_SHARD // 2, D_MODEL)]
    dz2, dz2b, d_ln2g, d_ln2b = ln_bwd(dx2, z2, w2["ln2_g"], name="ln2_bwd")
    d_om, down_received = mm(dz2b, w_mem_o, "nt", name="d_om", side=grad_to_sibling(ffn_chunks[1:]))
    d_wmemo = mm(om, dz2b, "tn", name="dw_mem_o")
    dqm, dkv = memattn_bwd(qm, kv, d_om, name="memattn_bwd")
    dx1 = mm(dqm, w_mem_q, "nt", add=dz2, add_scale=ALPHA, name="d_x1")
    d_wmemq = mm(x1b, dqm, "tn", name="dw_mem_q")
    d_wmemkv = mm(memb, dkv, "tn", name="dw_mem_kv")
    dz1, dz1b, d_ln1g, d_ln1b = ln_bwd(dx1, z1, w2["ln1_g"], name="ln1_bwd")
    dmix, up_received = mm(dz1b, w_mix_o, "nt", name="d_mixed", side=grad_to_sibling(ffn_chunks[:1]))
    ffn_sums = sibling_sums(ffn, ffn_chunks, up_received + down_received)
    d_wmixo = mm(mixed, dz1b, "tn", name="dw_mix_o")
    dys, dyg, d_gates = merge_bwd(gates, y_swa, y_gdn, dmix, name="merge_bwd")
    d_attn = mm(dys, w_br_swa, "nt", name="d_attn")
    d_wbrswa = mm(attn, dys, "tn", name="dw_br_swa")
    d_ygd = mm(dyg, w_br_gdn, "nt", name="d_ygd")
    d_wbrgdn = mm(ygd, dyg, "tn", name="dw_br_gdn")
    mid_chunks = [_column_shards(d_wbrswa), _column_shards(d_wbrgdn), _column_shards(d_wmemo),
                  d_wmixo.reshape(N_DEV, D_MODEL // N_DEV, D_MODEL), d_wmemq.reshape(N_DEV, D_MODEL // N_DEV, MEM_W),
                  d_wmemkv.reshape(N_DEV, D_MODEL // N_DEV, 2 * MEM_W)]
    d_o, d_gz, d_normw = gdn_post_bwd(o_gdn, gdn_in, w2["gdn_norm_w"], d_ygd, name="gdn_post_bwd")
    (dqn, dkn, dvv, dgbeta), received = gdn_scan_bwd(
        qn, kn, vv, gbeta, states, d_o, name="gdn_scan_bwd", side=join_sides(grad_to_chips(ffn_sums), grad_to_sibling(mid_chunks)))
    grads = dict(zip(ffn, chip_sums(ffn, received[:2])))
    mid_sums = sibling_sums(mid, mid_chunks, received[2:])
    d_gdn_in, d_convw = gdn_pre_bwd(gdn_in, convw, dqn, dkn, dvv, d_gz, name="gdn_pre_bwd")
    d_ba, d_alog, d_dt = gbeta_bwd(swa_in, alog_row, dt_row, dgbeta, name="gbeta_bwd")
    (dq, dkc, dkp, dvc, dvp, d_bias, d_sinks), received = swa_bwd(swa_in, bias, w2["swa_sinks"], d_attn, name="swa_bwd",
                                                                  side=grad_to_chips(mid_sums))
    grads.update(zip(mid, chip_sums(mid, received)))
    d_swa_in = swa_in_grad(dq, dkc, dkp, dvc, dvp, d_ba, name="swa_in_grad")
    d_relbias = mm(d_bias.reshape(SWA_HEADS, -1), onehot, "nt", hi=True, tk=4096, name="d_rel_bias").T
    d_wgates = mm(xb, d_gates, "tn", name="dw_gates")
    d_wgdn = mm(xb, d_gdn_in, "tn", name="dw_gdn")
    d_wswa = mm(xb, d_swa_in, "tn", tn=SWA_IN_W, name="dw_swa")
    in_chunks = [merge_w_in_grad(d_wswa, d_wgdn, d_wgates, name="merge_w_in_grad")]
    gx, received = mm(d_gates, w_gates, "nt", add=dz1, add_scale=ALPHA, name="dx_gates", side=grad_to_sibling(in_chunks))
    in_sums = sibling_sums(("w_in",), in_chunks, received)
    gx, received = mm(d_gdn_in, w_gdn, "nt", add=gx, name="dx_gdn", side=grad_to_chips(in_sums))
    gx = mm(d_swa_in, w_swa, "nt", add=gx, tk=SWA_IN_W, name="dx_swa")
    grads["w_in"] = chip_sums(("w_in",), received)[0][:, :W_IN_SHARD]
    grads["w_up"] = grads["w_up"][:, :FF_SHARD]

    gsmall = {
        "rel_bias": d_relbias, "swa_sinks": d_sinks[:, :SWA_HEADS], "gdn_a_log": d_alog[:, GDN_HEADS:2 * GDN_HEADS],
        "gdn_dt_bias": d_dt[:, GDN_HEADS:2 * GDN_HEADS], "gdn_norm_w": d_normw, "ln1_g": d_ln1g, "ln1_b": d_ln1b,
        "ln2_g": d_ln2g, "ln2_b": d_ln2b, "ln3_g": d_ln3g, "ln3_b": d_ln3b,
        "ffn_conv_b": d_cwb[:, 3, :FF_SHARD].reshape(1, 2 * D_FF),
        "ffn_conv_w": jnp.transpose(d_cwb[:, :3, :FF_SHARD], (1, 0, 2)).reshape(3, 2 * D_FF),
        "gdn_conv_w": d_convw[:GDN_CONV],
    }
    small_shapes = [shp for _, shp in SMALL] + [(3, 2 * D_FF), (GDN_CONV, QKV_W)]
    small_names = [n for n, _ in SMALL] + ["ffn_conv_w", "gdn_conv_w"]
    small_sum = all_reduce_small(_pack([gsmall[n] for n in small_names], AR_ROWS, f32), name="all_reduce_small")
    grads.update(zip(small_names, _unpack(small_sum.reshape(-1), small_shapes)))
    grads["ffn_conv_w"] = lax.dynamic_slice_in_dim(grads["ffn_conv_w"], my_id * FF_SHARD, FF_SHARD, axis=1)
    grads["gdn_conv_w"] = lax.dynamic_slice_in_dim(grads["gdn_conv_w"], my_id * (QKV_W // N_DEV), QKV_W // N_DEV, axis=1)

    big = [n for n, shp, _ in SHARDED if shp[0] * shp[1] > 8192]
    tiny = [n for n in WEIGHTS if n not in big]
    delta, new_m, new_v = {}, {}, {}
    for n in big:
        delta[n], new_m[n], new_v[n] = adamw(w2[n], grads[n], m2[n], v2[n], name="adamw_" + n)
    tiny_shapes = [w2[n].shape for n in tiny]
    packed = [_pack([src[n] for n in tiny], SMALL_ROWS, f32) for src in (w2, grads, m2, v2)]
    for dst, res in zip((delta, new_m, new_v), adamw(*packed, name="adamw_small")):
        dst.update(zip(tiny, _unpack(res.reshape(-1), tiny_shapes)))

    def shaped(d):
        return [d[n].reshape(env[n].shape) for n in WEIGHTS]

    loss_all = lax.psum(loss[0, 0], ("x", "y", "c"))
    return (loss_all, gx[None], *shaped(grads), *shaped(delta), *shaped(new_m), *shaped(new_v))
```

```python
import functools
import math
from typing import Callable, NamedTuple

import jax
import jax.numpy as jnp
from jax import lax
from jax.experimental import pallas as pl
from jax.experimental.pallas import tpu as pltpu

f32 = jnp.float32
bf16 = jnp.bfloat16
HI = lax.Precision.HIGHEST
MESH = pl.DeviceIdType.MESH

D_MODEL = 2048
N_DEV = 8
SWA_HEADS, SWA_KV_HEADS, SWA_HEAD_DIM, BLOCK = 16, 2, 64, 128
REL_BUCKETS, REL_MAX_DIST = 32, 128
GDN_HEADS, GDN_HEAD_DIM, GDN_CONV, GDN_CHUNK = 8, 128, 4, 64
MEM_HEADS, MEM_HEAD_DIM = 4, 128
D_FF = 5504
FF_SHARD = 2 * D_FF // N_DEV
FF_PAD = 1408
NORM_EPS = 1e-5
ALPHA = 2.0 ** 0.25
NEG_INF = -1e30
SWA_Q, SWA_KV, GDN_W, MEM_W = 1024, 128, 1024, 512
IN_DIM = 9488
HALO = 8

ADAM_LR, ADAM_B1, ADAM_B2, ADAM_EPS, ADAM_WD, ADAM_STEP = 0.001, 0.9, 0.999, 1e-08, 0.01, 10

PACK_COLS = 1024
SMALL_ROWS = 32
AR_ROWS = 72
CONV_ROWS = 48

SHARDED = (
    ("w_in", (2048, 1186), 1), ("w_br_swa", (1024, 256), 1), ("w_br_gdn", (1024, 256), 1),
    ("w_mix_o", (256, 2048), 0), ("w_mem_q", (256, 512), 0), ("w_mem_kv", (256, 1024), 0),
    ("w_mem_o", (512, 256), 1), ("w_up", (2048, 1376), 1), ("w_down", (688, 2048), 0),
    ("ffn_conv_w", (3, 1376), 1), ("gdn_conv_w", (4, 384), 1),
)
SMALL = (
    ("rel_bias", (32, 16)), ("swa_sinks", (1, 16)), ("gdn_a_log", (1, 8)), ("gdn_dt_bias", (1, 8)),
    ("gdn_norm_w", (1, 128)), ("ln1_g", (1, 2048)), ("ln1_b", (1, 2048)), ("ln2_g", (1, 2048)),
    ("ln2_b", (1, 2048)), ("ln3_g", (1, 2048)), ("ln3_b", (1, 2048)), ("ffn_conv_b", (1, 11008)),
)
WEIGHTS = ("w_in", "rel_bias", "swa_sinks", "gdn_conv_w", "gdn_a_log", "gdn_dt_bias", "gdn_norm_w", "w_br_swa",
           "w_br_gdn", "w_mix_o", "ln1_g", "ln1_b", "w_mem_q", "w_mem_kv", "w_mem_o", "ln2_g", "ln2_b", "w_up",
           "ffn_conv_w", "ffn_conv_b", "w_down", "ln3_g", "ln3_b")


def _tile(n, target, align):
    if n <= target:
        return n
    t = (target // align) * align
    while t >= align:
        if n % t == 0:
            return t
        t -= align
    return n


VMEM_LIMIT_BYTES = 56 * 1024 * 1024


def _params(*sem):
    return pltpu.CompilerParams(dimension_semantics=sem, vmem_limit_bytes=VMEM_LIMIT_BYTES)


def _sigmoid(v):
    return jax.nn.sigmoid(v)


def _d16(a, b, dims):
    return lax.dot_general(a.astype(bf16), b.astype(bf16), (dims, ((), ())), preferred_element_type=f32)


NN = ((1,), (0,))
NT = ((1,), (1,))
TN = ((0,), (0,))


def mm(a, b, mode, *, name, add=None, add_scale=1.0, out_dtype=f32, hi=False, tm=1024, tn=1024, tk=2048,
       b_blocked=False, out_blocked=False, side=None):
    if b_blocked:
        nb, rows, width = b.shape
        if mode == "nn":
            (m, k), n, tn = a.shape, nb * width, width
        else:
            (m, k), n, tk = a.shape, rows, width
    elif mode == "nn":
        (m, k), (_, n) = a.shape, b.shape
    elif mode == "nt":
        (m, k), (n, _) = a.shape, b.shape
    else:
        (k, m), (_, n) = a.shape, b.shape
    if out_blocked:
        tn = n // N_DEV
    tm, tn, tk = _tile(m, tm, 8 if mode != "tn" else 128), _tile(n, tn, 128), _tile(k, tk, 128 if mode != "tn" else 8)
    nk = k // tk
    dims = {"nn": NN, "nt": NT, "tn": TN}[mode]
    a_spec = pl.BlockSpec((tk, tm), lambda i, j, kk: (kk, i)) if mode == "tn" else pl.BlockSpec((tm, tk), lambda i, j, kk: (i, kk))
    if b_blocked:
        b_spec = (pl.BlockSpec((None, tk, tn), lambda i, j, kk: (j, kk, 0)) if mode == "nn"
                  else pl.BlockSpec((None, tn, tk), lambda i, j, kk: (kk, j, 0)))
    else:
        b_spec = pl.BlockSpec((tn, tk), lambda i, j, kk: (j, kk)) if mode == "nt" else pl.BlockSpec((tk, tn), lambda i, j, kk: (kk, j))
    if out_blocked:
        o_spec, o_shape = pl.BlockSpec((None, tm, tn), lambda i, j, kk: (j, i, 0)), (N_DEV, m, tn)
    else:
        o_spec, o_shape = pl.BlockSpec((tm, tn), lambda i, j, kk: (i, j)), (m, n)
    has_add = add is not None

    def product(a_ref, b_ref):
        if hi:
            return lax.dot_general(a_ref[...], b_ref[...], (dims, ((), ())), precision=HI, preferred_element_type=f32)
        return _d16(a_ref[...], b_ref[...], dims)

    def finish(r, add_ref, o_ref):
        if has_add:
            r = r + add_scale * add_ref[...]
        o_ref[...] = r.astype(out_dtype)

    def body_one_step(a_ref, b_ref, *rest):
        finish(product(a_ref, b_ref), rest[0] if has_add else None, rest[-1])

    def body_k_steps(a_ref, b_ref, *rest):
        o_ref, acc_ref = rest[-2:]
        kk = pl.program_id(2)

        @pl.when(kk == 0)
        def _():
            acc_ref[...] = jnp.zeros_like(acc_ref)

        acc_ref[...] += product(a_ref, b_ref)

        @pl.when(kk == nk - 1)
        def _():
            finish(acc_ref[...], rest[0] if has_add else None, o_ref)

    return _call(body_one_step if nk == 1 else body_k_steps, (a, b, add) if has_add else (a, b), name=name,
                 grid=(m // tm, n // tn, nk), in_specs=[a_spec, b_spec] + ([o_spec] if has_add else []), out_specs=o_spec,
                 out_shape=jax.ShapeDtypeStruct(o_shape, out_dtype),
                 scratch_shapes=[] if nk == 1 else [pltpu.VMEM((tm, tn), f32)],
                 semantics=("parallel", "parallel", "arbitrary"), side=side)


def cast_bf16(a, *, name):
    m, n = a.shape
    tm = _tile(m, 512, 16)

    def body(a_ref, o_ref):
        o_ref[...] = a_ref[...].astype(bf16)

    return pl.pallas_call(body, name=name, grid=(m // tm,), in_specs=[pl.BlockSpec((tm, n), lambda i: (i, 0))],
                          out_specs=pl.BlockSpec((tm, n), lambda i: (i, 0)), out_shape=jax.ShapeDtypeStruct((m, n), bf16),
                          compiler_params=_params("parallel"))(a)


def _ln_stats(z):
    mu = jnp.mean(z, axis=-1, keepdims=True)
    zc = z - mu
    var = jnp.mean(zc * zc, axis=-1, keepdims=True)
    rstd = lax.rsqrt(var + NORM_EPS)
    return zc * rstd, rstd


def ln_fwd(z, g, b, *, name):
    s, d = z.shape
    tm = _tile(s, 256, 16)

    def body(z_ref, g_ref, b_ref, y_ref, yb_ref):
        xhat, _ = _ln_stats(z_ref[...])
        y = xhat * g_ref[...] + b_ref[...]
        y_ref[...] = y
        yb_ref[...] = y.astype(bf16)

    row = pl.BlockSpec((tm, d), lambda i: (i, 0))
    vec = pl.BlockSpec((1, d), lambda i: (0, 0))
    return pl.pallas_call(body, name=name, grid=(s // tm,), in_specs=[row, vec, vec], out_specs=[row, row],
                          out_shape=[jax.ShapeDtypeStruct((s, d), f32), jax.ShapeDtypeStruct((s, d), bf16)],
                          compiler_params=_params("parallel"))(z, g, b)


def _ln_bwd_tile(dy, z, g):
    xhat, rstd = _ln_stats(z)
    dxh = dy * g
    m1 = jnp.mean(dxh, axis=-1, keepdims=True)
    m2 = jnp.mean(dxh * xhat, axis=-1, keepdims=True)
    dz = rstd * (dxh - m1 - xhat * m2)
    return dz, jnp.sum(dy * xhat, axis=0, keepdims=True), jnp.sum(dy, axis=0, keepdims=True)


def ln_bwd(dy, z, g, *, name):
    s, d = z.shape
    tm = _tile(s, 256, 16)

    def body(dy_ref, z_ref, g_ref, dz_ref, dzb_ref, dg_ref, db_ref):
        @pl.when(pl.program_id(0) == 0)
        def _():
            dg_ref[...] = jnp.zeros_like(dg_ref)
            db_ref[...] = jnp.zeros_like(db_ref)

        dz, dg, db = _ln_bwd_tile(dy_ref[...], z_ref[...], g_ref[...])
        dz_ref[...] = dz
        dzb_ref[...] = dz.astype(bf16)
        dg_ref[...] += dg
        db_ref[...] += db

    row = pl.BlockSpec((tm, d), lambda i: (i, 0))
    vec = pl.BlockSpec((1, d), lambda i: (0, 0))
    return pl.pallas_call(body, name=name, grid=(s // tm,), in_specs=[row, row, vec], out_specs=[row, row, vec, vec],
                          out_shape=[jax.ShapeDtypeStruct((s, d), f32), jax.ShapeDtypeStruct((s, d), bf16),
                                     jax.ShapeDtypeStruct((1, d), f32), jax.ShapeDtypeStruct((1, d), f32)],
                          compiler_params=_params("arbitrary"))(dy, z, g)


def ln_loss(z, target, g, b, *, name):
    s, d = z.shape
    tm = _tile(s, 256, 16)
    nt = s // tm

    def body(z_ref, t_ref, g_ref, b_ref, dz_ref, dzb_ref, dg_ref, db_ref, loss_ref, lacc_ref):
        i = pl.program_id(0)

        @pl.when(i == 0)
        def _():
            dg_ref[...] = jnp.zeros_like(dg_ref)
            db_ref[...] = jnp.zeros_like(db_ref)
            lacc_ref[...] = jnp.zeros_like(lacc_ref)

        zv, gv = z_ref[...], g_ref[...]
        xhat, _ = _ln_stats(zv)
        err = xhat * gv + b_ref[...] - t_ref[...]
        lacc_ref[...] += jnp.sum(err * err, axis=0, keepdims=True)
        dz, dg, db = _ln_bwd_tile(err * (1.0 / d), zv, gv)
        dz_ref[...] = dz
        dzb_ref[...] = dz.astype(bf16)
        dg_ref[...] += dg
        db_ref[...] += db

        @pl.when(i == nt - 1)
        def _():
            loss_ref[...] = (0.5 / d) * jnp.sum(lacc_ref[...], axis=1, keepdims=True)

    row = pl.BlockSpec((tm, d), lambda i: (i, 0))
    vec = pl.BlockSpec((1, d), lambda i: (0, 0))
    return pl.pallas_call(body, name=name, grid=(nt,), in_specs=[row, row, vec, vec],
                          out_specs=[row, row, vec, vec, pl.BlockSpec((1, 1), lambda i: (0, 0))],
                          out_shape=[jax.ShapeDtypeStruct((s, d), f32), jax.ShapeDtypeStruct((s, d), bf16),
                                     jax.ShapeDtypeStruct((1, d), f32), jax.ShapeDtypeStruct((1, d), f32),
                                     jax.ShapeDtypeStruct((1, 1), f32)],
                          scratch_shapes=[pltpu.VMEM((1, d), f32)],
                          compiler_params=_params("arbitrary"))(z, target, g, b)


def merge_fwd(gates, ys, yg, *, name):
    s, d = ys.shape
    tm = _tile(s, 256, 16)

    def body(gt_ref, ys_ref, yg_ref, o_ref):
        o_ref[...] = (_sigmoid(gt_ref[:, :d]) * ys_ref[...] + _sigmoid(gt_ref[:, d:]) * yg_ref[...]).astype(bf16)

    row = pl.BlockSpec((tm, d), lambda i: (i, 0))
    return pl.pallas_call(body, name=name, grid=(s // tm,), in_specs=[pl.BlockSpec((tm, 2 * d), lambda i: (i, 0)), row, row],
                          out_specs=row, out_shape=jax.ShapeDtypeStruct((s, d), bf16),
                          compiler_params=_params("parallel"))(gates, ys, yg)


def merge_bwd(gates, ys, yg, dmix, *, name):
    s, d = ys.shape
    tm = _tile(s, 256, 16)

    def body(gt_ref, ys_ref, yg_ref, dm_ref, dys_ref, dyg_ref, dgt_ref):
        dm = dm_ref[...]
        sa, sb = _sigmoid(gt_ref[:, :d]), _sigmoid(gt_ref[:, d:])
        dys_ref[...] = (dm * sa).astype(bf16)
        dyg_ref[...] = (dm * sb).astype(bf16)
        dgt_ref[:, :d] = (dm * ys_ref[...] * sa * (1.0 - sa)).astype(bf16)
        dgt_ref[:, d:] = (dm * yg_ref[...] * sb * (1.0 - sb)).astype(bf16)

    row = pl.BlockSpec((tm, d), lambda i: (i, 0))
    wide = pl.BlockSpec((tm, 2 * d), lambda i: (i, 0))
    return pl.pallas_call(body, name=name, grid=(s // tm,), in_specs=[wide, row, row, row], out_specs=[row, row, wide],
                          out_shape=[jax.ShapeDtypeStruct((s, d), bf16), jax.ShapeDtypeStruct((s, d), bf16),
                                     jax.ShapeDtypeStruct((s, 2 * d), bf16)],
                          compiler_params=_params("parallel"))(gates, ys, yg, dmix)


def _shift_down(ext, j):
    return ext if j == 0 else pltpu.roll(ext, j, 0)


def _shift_up(ext, j):
    return ext if j == 0 else pltpu.roll(ext, ext.shape[0] - j, 0)


def _conv_taps(ext, width):
    return [_shift_down(ext, width - 1 - j)[HALO:] for j in range(width)]


def _causal_conv(taps, w_ref):
    acc = None
    for j, tap in enumerate(taps):
        term = w_ref[j:j + 1, :] * tap
        acc = term if acc is None else acc + term
    return acc


def _conv_grads(dy_ext, x, w_ref, width, rows):
    ahead = [_shift_up(dy_ext, width - 1 - j)[:rows] for j in range(width)]
    dx = None
    for j in range(width):
        term = w_ref[j:j + 1, :] * ahead[j]
        dx = term if dx is None else dx + term
    return dx, [jnp.sum(x * ahead[j], axis=0, keepdims=True) for j in range(width)]


def _rows_to_block(rows, n_rows, cols):
    r = lax.broadcasted_iota(jnp.int32, (n_rows, cols), 0)
    out = jnp.zeros((n_rows, cols), f32)
    for j, v in enumerate(rows):
        out = out + jnp.where(r == j, v, 0.0)
    return out


def _silu_and_grad(v):
    sg = _sigmoid(v)
    return v * sg, sg * (1.0 + v * (1.0 - sg))


def ffn_act_fwd(hpre, cwb, *, name):
    s = hpre.shape[0]
    tm = _tile(s, 256, 16)
    hb = tm // HALO

    def body(hg_ref, hgp_ref, hu_ref, hup_ref, cg_ref, cu_ref, o_ref, g_ref, u_ref):
        first = pl.program_id(1) == 0

        def conv(h_ref, hp_ref, c_ref):
            prev = jnp.where(first, 0.0, hp_ref[...])
            return _causal_conv(_conv_taps(jnp.concatenate([prev, h_ref[...]], axis=0), 3), c_ref.at[0]) + c_ref[0, 3:4, :]

        g = conv(hg_ref, hgp_ref, cg_ref)
        u = conv(hu_ref, hup_ref, cu_ref)
        g_ref[...] = g
        u_ref[...] = u
        o_ref[...] = (g * _sigmoid(g) * u).astype(bf16)

    def tile(off):
        return pl.BlockSpec((tm, FF_PAD), lambda j, i: (i, j + off))

    def halo(off):
        return pl.BlockSpec((HALO, FF_PAD), lambda j, i: (jnp.maximum(i * hb - 1, 0), j + off))

    def taps(off):
        return pl.BlockSpec((1, 8, FF_PAD), lambda j, i: (j + off, 0, 0))

    out = pl.BlockSpec((tm, FF_PAD), lambda j, i: (i, j))
    return pl.pallas_call(body, name=name, grid=(4, s // tm),
                          in_specs=[tile(0), halo(0), tile(4), halo(4), taps(0), taps(4)], out_specs=[out, out, out],
                          out_shape=[jax.ShapeDtypeStruct((s, 4 * FF_PAD), bf16)] + [jax.ShapeDtypeStruct((s, 4 * FF_PAD), f32)] * 2,
                          compiler_params=_params("parallel", "parallel"))(hpre, hpre, hpre, hpre, cwb, cwb)


def ffn_act_bwd(hpre, conv_g, conv_u, dact, cwb, *, name):
    s = hpre.shape[0]
    tm = _tile(s, 256, 16)
    hb = tm // HALO
    nt = s // tm
    last_hb = s // HALO - 1

    def body(hg_ref, hu_ref, g_ref, gn_ref, u_ref, un_ref, d_ref, dn_ref, cg_ref, cu_ref, dhg_ref, dhu_ref, dcg_ref, dcu_ref):
        i = pl.program_id(1)

        @pl.when(i == 0)
        def _():
            dcg_ref[...] = jnp.zeros_like(dcg_ref)
            dcu_ref[...] = jnp.zeros_like(dcu_ref)

        g = jnp.concatenate([g_ref[...], gn_ref[...]], axis=0)
        u = jnp.concatenate([u_ref[...], un_ref[...]], axis=0)
        d = jnp.concatenate([d_ref[...], jnp.where(i == nt - 1, 0.0, dn_ref[...])], axis=0)
        act, dact_dg = _silu_and_grad(g)
        dg = d * u * dact_dg
        du = d * act
        dhg, dwg = _conv_grads(dg, hg_ref[...], cg_ref.at[0], 3, tm)
        dhu, dwu = _conv_grads(du, hu_ref[...], cu_ref.at[0], 3, tm)
        dhg_ref[...] = dhg.astype(bf16)
        dhu_ref[...] = dhu.astype(bf16)
        dcg_ref[0] += _rows_to_block(dwg + [jnp.sum(dg[:tm], axis=0, keepdims=True)], 8, FF_PAD)
        dcu_ref[0] += _rows_to_block(dwu + [jnp.sum(du[:tm], axis=0, keepdims=True)], 8, FF_PAD)

    def tile(off):
        return pl.BlockSpec((tm, FF_PAD), lambda j, i: (i, j + off))

    nxt = pl.BlockSpec((HALO, FF_PAD), lambda j, i: (jnp.minimum((i + 1) * hb, last_hb), j))
    taps = [pl.BlockSpec((1, 8, FF_PAD), lambda j, i, off=off: (j + off, 0, 0)) for off in (0, 4)]
    dhg, dhu, dcg, dcu = pl.pallas_call(
        body, name=name, grid=(4, nt),
        in_specs=[tile(0), tile(4), tile(0), nxt, tile(0), nxt, tile(0), nxt] + taps,
        out_specs=[tile(0), tile(0), taps[0], taps[0]],
        out_shape=[jax.ShapeDtypeStruct((s, 4 * FF_PAD), bf16), jax.ShapeDtypeStruct((s, 4 * FF_PAD), bf16),
                   jax.ShapeDtypeStruct((4, 8, FF_PAD), f32), jax.ShapeDtypeStruct((4, 8, FF_PAD), f32)],
        compiler_params=_params("parallel", "arbitrary"),
    )(hpre, hpre, conv_g, conv_g, conv_u, conv_u, dact, dact, cwb, cwb)
    return jnp.concatenate([dhg, dhu], axis=1), jnp.concatenate([dcg, dcu], axis=0)


MEM_SCALE = MEM_HEAD_DIM ** -0.5


def _softmax_rows(sc):
    m = jnp.max(sc, axis=-1, keepdims=True)
    e = jnp.exp(sc - m)
    return e / jnp.sum(e, axis=-1, keepdims=True)


def memattn_fwd(qm, kv, *, name):
    s = qm.shape[0]
    mlen = kv.shape[0]
    tm = _tile(s, 512, 16)

    def body(q_ref, kv_ref, o_ref):
        for h in range(MEM_HEADS):
            lo = h * MEM_HEAD_DIM
            q = q_ref[:, lo:lo + MEM_HEAD_DIM]
            k = kv_ref[:, lo:lo + MEM_HEAD_DIM]
            v = kv_ref[:, MEM_W + lo:MEM_W + lo + MEM_HEAD_DIM]
            p = _softmax_rows(_d16(q, k, NT) * MEM_SCALE)
            o_ref[:, lo:lo + MEM_HEAD_DIM] = _d16(p, v, NN).astype(bf16)

    return pl.pallas_call(body, name=name, grid=(s // tm,),
                          in_specs=[pl.BlockSpec((tm, MEM_W), lambda i: (i, 0)), pl.BlockSpec((mlen, 2 * MEM_W), lambda i: (0, 0))],
                          out_specs=pl.BlockSpec((tm, MEM_W), lambda i: (i, 0)),
                          out_shape=jax.ShapeDtypeStruct((s, MEM_W), bf16), compiler_params=_params("parallel"))(qm, kv)


def memattn_bwd(qm, kv, dout, *, name):
    s = qm.shape[0]
    mlen = kv.shape[0]
    tm = _tile(s, 512, 16)

    def body(q_ref, kv_ref, do_ref, dq_ref, dkv_ref):
        @pl.when(pl.program_id(0) == 0)
        def _():
            dkv_ref[...] = jnp.zeros_like(dkv_ref)

        for h in range(MEM_HEADS):
            lo = h * MEM_HEAD_DIM
            q = q_ref[:, lo:lo + MEM_HEAD_DIM]
            k = kv_ref[:, lo:lo + MEM_HEAD_DIM]
            v = kv_ref[:, MEM_W + lo:MEM_W + lo + MEM_HEAD_DIM]
            do = do_ref[:, lo:lo + MEM_HEAD_DIM]
            p = _softmax_rows(_d16(q, k, NT) * MEM_SCALE)
            dp = _d16(do, v, NT)
            ds = p * (dp - jnp.sum(p * dp, axis=-1, keepdims=True)) * MEM_SCALE
            dq_ref[:, lo:lo + MEM_HEAD_DIM] = _d16(ds, k, NN).astype(bf16)
            dkv_ref[:, lo:lo + MEM_HEAD_DIM] += _d16(ds, q, TN)
            dkv_ref[:, MEM_W + lo:MEM_W + lo + MEM_HEAD_DIM] += _d16(p, do, TN)

    row = pl.BlockSpec((tm, MEM_W), lambda i: (i, 0))
    full = pl.BlockSpec((mlen, 2 * MEM_W), lambda i: (0, 0))
    return pl.pallas_call(body, name=name, grid=(s // tm,), in_specs=[row, full, row], out_specs=[row, full],
                          out_shape=[jax.ShapeDtypeStruct((s, MEM_W), bf16), jax.ShapeDtypeStruct((mlen, 2 * MEM_W), f32)],
                          compiler_params=_params("arbitrary"))(qm, kv, dout)


SWA_SCALE = SWA_HEAD_DIM ** -0.5
SWA_GROUP = SWA_HEADS // SWA_KV_HEADS
SWA_IN_W = 1408
K_COL, V_COL, BA_COL = SWA_Q // 128, SWA_Q // 128 + 1, SWA_Q // 128 + 2


def _swa_mask(n):
    qi = lax.broadcasted_iota(jnp.int32, (BLOCK, 2 * BLOCK), 0)
    kj = lax.broadcasted_iota(jnp.int32, (BLOCK, 2 * BLOCK), 1)
    dist = qi + BLOCK - kj
    return (dist >= 0) & (dist < BLOCK) & ((n > 0) | (kj >= BLOCK))


def _swa_group_probs(q, k, bias, sink, mask):
    heads = range(len(q))
    sc = [jnp.where(mask, _d16(q[h], k, NT) * SWA_SCALE + bias[h], NEG_INF) for h in heads]
    m = [jnp.maximum(jnp.max(sc[h], axis=-1, keepdims=True), sink[h]) for h in heads]
    e = [jnp.exp(sc[h] - m[h]) for h in heads]
    es = [jnp.exp(sink[h] - m[h]) for h in heads]
    inv = [1.0 / (jnp.sum(e[h], axis=-1, keepdims=True) + es[h]) for h in heads]
    return e, es, inv


def _swa_group_inputs(g, q_ref, bias_ref, sink_ref):
    hs = range(g * SWA_GROUP, (g + 1) * SWA_GROUP)
    return ([q_ref[:, h * SWA_HEAD_DIM:(h + 1) * SWA_HEAD_DIM] for h in hs], [bias_ref[h] for h in hs],
            [sink_ref[:, h:h + 1] for h in hs])


def _swa_specs():
    q_spec = pl.BlockSpec((BLOCK, SWA_Q), lambda n: (n, 0))

    def band(col):
        return [pl.BlockSpec((BLOCK, SWA_KV), lambda n: (jnp.maximum(n - 1, 0), col)),
                pl.BlockSpec((BLOCK, SWA_KV), lambda n: (n, col))]

    bias_spec = pl.BlockSpec((SWA_HEADS, BLOCK, 2 * BLOCK), lambda n: (0, 0, 0))
    sink_spec = pl.BlockSpec((1, SWA_HEADS), lambda n: (0, 0))
    return [q_spec] + band(K_COL) + band(V_COL) + [bias_spec, sink_spec]


def swa_fwd(swa_in, bias, sinks, *, name, side=None):
    s = swa_in.shape[0]

    def body(q_ref, kp_ref, kc_ref, vp_ref, vc_ref, bias_ref, sink_ref, o_ref):
        mask = _swa_mask(pl.program_id(0))
        kb = jnp.concatenate([kp_ref[...], kc_ref[...]], axis=0)
        vb = jnp.concatenate([vp_ref[...], vc_ref[...]], axis=0)
        for g in range(SWA_KV_HEADS):
            kl = g * SWA_HEAD_DIM
            q, bias_g, sink_g = _swa_group_inputs(g, q_ref, bias_ref, sink_ref)
            e, _, inv = _swa_group_probs(q, kb[:, kl:kl + SWA_HEAD_DIM], bias_g, sink_g, mask)
            v = vb[:, kl:kl + SWA_HEAD_DIM]
            outs = [_d16(e[h] * inv[h], v, NN) for h in range(SWA_GROUP)]
            for h in range(SWA_GROUP):
                lo = (g * SWA_GROUP + h) * SWA_HEAD_DIM
                o_ref[:, lo:lo + SWA_HEAD_DIM] = outs[h].astype(bf16)

    return _call(body, (swa_in, swa_in, swa_in, swa_in, swa_in, bias, sinks), name=name, grid=(s // BLOCK,),
                 in_specs=_swa_specs(), out_specs=pl.BlockSpec((BLOCK, SWA_Q), lambda n: (n, 0)),
                 out_shape=jax.ShapeDtypeStruct((s, SWA_Q), bf16), semantics=("parallel",), side=side)


def swa_bwd(swa_in, bias, sinks, dout, *, name, side=None):
    s = swa_in.shape[0]

    def body(q_ref, kp_ref, kc_ref, vp_ref, vc_ref, bias_ref, sink_ref, do_ref,
             dq_ref, dkc_ref, dkp_ref, dvc_ref, dvp_ref, dbias_ref, dsink_ref):
        n = pl.program_id(0)

        @pl.when(n == 0)
        def _():
            dbias_ref[...] = jnp.zeros_like(dbias_ref)
            dsink_ref[...] = jnp.zeros_like(dsink_ref)

        mask = _swa_mask(n)
        kb = jnp.concatenate([kp_ref[...], kc_ref[...]], axis=0)
        vb = jnp.concatenate([vp_ref[...], vc_ref[...]], axis=0)
        lane = lax.broadcasted_iota(jnp.int32, (1, 128), 1)
        dsink = jnp.zeros((1, 128), f32)
        for g in range(SWA_KV_HEADS):
            kl = g * SWA_HEAD_DIM
            k, v = kb[:, kl:kl + SWA_HEAD_DIM], vb[:, kl:kl + SWA_HEAD_DIM]
            hs = range(SWA_GROUP)
            q, bias_g, sink_g = _swa_group_inputs(g, q_ref, bias_ref, sink_ref)
            do = [do_ref[:, (g * SWA_GROUP + h) * SWA_HEAD_DIM:(g * SWA_GROUP + h + 1) * SWA_HEAD_DIM] for h in hs]
            e, es, inv = _swa_group_probs(q, k, bias_g, sink_g, mask)
            p = [e[h] * inv[h] for h in hs]
            dp = [_d16(do[h], v, NT) for h in hs]
            delta = [jnp.sum(p[h] * dp[h], axis=-1, keepdims=True) for h in hs]
            ds = [p[h] * (dp[h] - delta[h]) for h in hs]
            dss = [ds[h] * SWA_SCALE for h in hs]
            dq = [_d16(dss[h], k, NN) for h in hs]
            dks = [_d16(dss[h], q[h], TN) for h in hs]
            dvs = [_d16(p[h], do[h], TN) for h in hs]
            dk, dv = sum(dks[1:], dks[0]), sum(dvs[1:], dvs[0])
            for h in hs:
                hh = g * SWA_GROUP + h
                dbias_ref[hh] += ds[h]
                dq_ref[:, hh * SWA_HEAD_DIM:(hh + 1) * SWA_HEAD_DIM] = dq[h]
                dsink = dsink + jnp.where(lane == hh, -jnp.sum(es[h] * inv[h] * delta[h], axis=0, keepdims=True), 0.0)
            dkp_ref[:, kl:kl + SWA_HEAD_DIM] = dk[:BLOCK]
            dkc_ref[:, kl:kl + SWA_HEAD_DIM] = dk[BLOCK:]
            dvp_ref[:, kl:kl + SWA_HEAD_DIM] = dv[:BLOCK]
            dvc_ref[:, kl:kl + SWA_HEAD_DIM] = dv[BLOCK:]
        dsink_ref[...] += dsink

    qs = pl.BlockSpec((BLOCK, SWA_Q), lambda n: (n, 0))
    ks = pl.BlockSpec((BLOCK, SWA_KV), lambda n: (n, 0))
    return _call(
        body, (swa_in, swa_in, swa_in, swa_in, swa_in, bias, sinks, dout), name=name, grid=(s // BLOCK,),
        in_specs=_swa_specs() + [qs],
        out_specs=[qs, ks, ks, ks, ks, pl.BlockSpec((SWA_HEADS, BLOCK, 2 * BLOCK), lambda n: (0, 0, 0)),
                   pl.BlockSpec((1, 128), lambda n: (0, 0))],
        out_shape=[jax.ShapeDtypeStruct((s, SWA_Q), f32)] + [jax.ShapeDtypeStruct((s, SWA_KV), f32)] * 4
        + [jax.ShapeDtypeStruct((SWA_HEADS, BLOCK, 2 * BLOCK), f32), jax.ShapeDtypeStruct((1, 128), f32)],
        semantics=("arbitrary",), side=side)


def swa_in_grad(dq, dkc, dkp, dvc, dvp, dba, *, name):
    s = dq.shape[0]
    nb = s // BLOCK

    def body(dq_ref, dkc_ref, dkp_ref, dvc_ref, dvp_ref, dba_ref, o_ref):
        has_next = pl.program_id(0) < nb - 1
        o_ref[:, :SWA_Q] = dq_ref[...].astype(bf16)
        o_ref[:, SWA_Q:SWA_Q + SWA_KV] = (dkc_ref[...] + jnp.where(has_next, dkp_ref[...], 0.0)).astype(bf16)
        o_ref[:, SWA_Q + SWA_KV:SWA_Q + 2 * SWA_KV] = (dvc_ref[...] + jnp.where(has_next, dvp_ref[...], 0.0)).astype(bf16)
        o_ref[:, SWA_Q + 2 * SWA_KV:] = dba_ref[...].astype(bf16)

    cur = pl.BlockSpec((BLOCK, SWA_KV), lambda n: (n, 0))
    nxt = pl.BlockSpec((BLOCK, SWA_KV), lambda n: (jnp.minimum(n + 1, nb - 1), 0))
    return pl.pallas_call(body, name=name, grid=(nb,),
                          in_specs=[pl.BlockSpec((BLOCK, SWA_Q), lambda n: (n, 0)), cur, nxt, cur, nxt, cur],
                          out_specs=pl.BlockSpec((BLOCK, SWA_IN_W), lambda n: (n, 0)),
                          out_shape=jax.ShapeDtypeStruct((s, SWA_IN_W), bf16),
                          compiler_params=_params("parallel"))(dq, dkc, dkp, dvc, dvp, dba)


def _bucket_onehot():
    qi = jnp.arange(BLOCK)[:, None]
    kj = jnp.arange(2 * BLOCK)[None, :]
    dist = jnp.maximum(qi + BLOCK - kj, 0)
    max_exact = REL_BUCKETS // 2
    dd = jnp.maximum(dist, 1).astype(f32)
    large = max_exact + (jnp.log(dd / max_exact) / math.log(REL_MAX_DIST / max_exact) * (REL_BUCKETS - max_exact)).astype(jnp.int32)
    bucket = jnp.where(dist < max_exact, dist, jnp.minimum(large, REL_BUCKETS - 1)).reshape(-1)
    return (bucket[None, :] == jnp.arange(REL_BUCKETS)[:, None]).astype(f32)


def _gbeta_fn(ba, alog_row, dt_row):
    col = lax.broadcasted_iota(jnp.int32, ba.shape, 1)
    v = ba + dt_row
    softplus = jnp.maximum(v, 0.0) + jnp.log(1.0 + jnp.exp(-jnp.abs(v)))
    g = -jnp.exp(alog_row) * softplus
    return jnp.where(col < GDN_HEADS, _sigmoid(ba), jnp.where(col < 2 * GDN_HEADS, g, 0.0))


def gbeta_fwd(swa_in, alog_row, dt_row, *, name):
    s = swa_in.shape[0]
    tm = _tile(s, 512, 8)

    def body(ba_ref, a_ref, d_ref, o_ref):
        o_ref[...] = _gbeta_fn(ba_ref[...], a_ref[...], d_ref[...])

    vec = pl.BlockSpec((1, 128), lambda i: (0, 0))
    return pl.pallas_call(body, name=name, grid=(s // tm,), in_specs=[pl.BlockSpec((tm, 128), lambda i: (i, BA_COL)), vec, vec],
                          out_specs=pl.BlockSpec((tm, 128), lambda i: (i, 0)), out_shape=jax.ShapeDtypeStruct((s, 128), f32),
                          compiler_params=_params("parallel"))(swa_in, alog_row, dt_row)


def gbeta_bwd(swa_in, alog_row, dt_row, dgbeta, *, name):
    s = swa_in.shape[0]
    tm = _tile(s, 512, 8)

    def body(ba_ref, a_ref, d_ref, dgb_ref, dba_ref, da_ref, dd_ref):
        @pl.when(pl.program_id(0) == 0)
        def _():
            da_ref[...] = jnp.zeros_like(da_ref)
            dd_ref[...] = jnp.zeros_like(dd_ref)

        _, pull = jax.vjp(_gbeta_fn, ba_ref[...], a_ref[...], d_ref[...])
        dba, da, dd = pull(dgb_ref[...])
        dba_ref[...] = dba
        da_ref[...] += da
        dd_ref[...] += dd

    vec = pl.BlockSpec((1, 128), lambda i: (0, 0))
    row = pl.BlockSpec((tm, 128), lambda i: (i, 0))
    return pl.pallas_call(body, name=name, grid=(s // tm,),
                          in_specs=[pl.BlockSpec((tm, 128), lambda i: (i, BA_COL)), vec, vec, row], out_specs=[row, vec, vec],
                          out_shape=[jax.ShapeDtypeStruct((s, 128), f32), jax.ShapeDtypeStruct((1, 128), f32),
                                     jax.ShapeDtypeStruct((1, 128), f32)],
                          compiler_params=_params("arbitrary"))(swa_in, alog_row, dt_row, dgbeta)


QKV_W = 3 * GDN_W


def gdn_pre_fwd(gdn_in, convw, *, name):
    s = gdn_in.shape[0]
    tm = _tile(s, 256, 16)
    hb = tm // HALO

    def body(x_ref, xp_ref, w_ref, q_ref, k_ref, v_ref, pre_ref):
        prev = jnp.where(pl.program_id(0) == 0, 0.0, xp_ref[...])
        pre = _causal_conv(_conv_taps(jnp.concatenate([prev, x_ref[...]], axis=0), GDN_CONV), w_ref)
        pre_ref[...] = pre
        act = pre * _sigmoid(pre)
        for h in range(GDN_HEADS):
            lo = h * GDN_HEAD_DIM
            for off, o_ref in ((0, q_ref), (GDN_W, k_ref)):
                seg = act[:, off + lo:off + lo + GDN_HEAD_DIM]
                o_ref[:, lo:lo + GDN_HEAD_DIM] = seg * lax.rsqrt(jnp.sum(seg * seg, axis=-1, keepdims=True) + 1e-6)
        v_ref[...] = act[:, 2 * GDN_W:]

    out = pl.BlockSpec((tm, GDN_W), lambda i: (i, 0))
    return pl.pallas_call(body, name=name, grid=(s // tm,),
                          in_specs=[pl.BlockSpec((tm, QKV_W), lambda i: (i, 0)),
                                    pl.BlockSpec((HALO, QKV_W), lambda i: (jnp.maximum(i * hb - 1, 0), 0)),
                                    pl.BlockSpec((8, QKV_W), lambda i: (0, 0))],
                          out_specs=[out, out, out, pl.BlockSpec((tm, QKV_W), lambda i: (i, 0))],
                          out_shape=[jax.ShapeDtypeStruct((s, GDN_W), f32)] * 3 + [jax.ShapeDtypeStruct((s, QKV_W), f32)],
                          compiler_params=_params("parallel"))(gdn_in, gdn_in, convw)


def gdn_pre_bwd(gdn_in, conv_out, convw, dqn, dkn, dv, dgz, *, name):
    s = gdn_in.shape[0]
    tm = _tile(s, 128, 16)
    hb = tm // HALO
    nt = s // tm
    last_hb = s // HALO - 1

    def body(x_ref, pre_ref, pren_ref, w_ref, dq_ref, dqx_ref, dk_ref, dkx_ref, dv_ref, dvx_ref, dz_ref, o_ref, dw_ref):
        i = pl.program_id(0)
        last = i == nt - 1

        @pl.when(i == 0)
        def _():
            dw_ref[...] = jnp.zeros_like(dw_ref)

        pre = jnp.concatenate([pre_ref[...], pren_ref[...]], axis=0)
        act, dact_dpre = _silu_and_grad(pre)

        def with_future(t_ref, n_ref):
            return jnp.concatenate([t_ref[...], jnp.where(last, 0.0, n_ref[...])], axis=0)

        dqe, dke, dve = with_future(dq_ref, dqx_ref), with_future(dk_ref, dkx_ref), with_future(dv_ref, dvx_ref)
        parts = []
        for off, dn in ((0, dqe), (GDN_W, dke)):
            for h in range(GDN_HEADS):
                lo = h * GDN_HEAD_DIM
                seg = act[:, off + lo:off + lo + GDN_HEAD_DIM]
                r = lax.rsqrt(jnp.sum(seg * seg, axis=-1, keepdims=True) + 1e-6)
                nrm = seg * r
                dseg = dn[:, lo:lo + GDN_HEAD_DIM]
                parts.append(r * (dseg - nrm * jnp.sum(dseg * nrm, axis=-1, keepdims=True)))
        dpre = jnp.concatenate(parts + [dve], axis=1) * dact_dpre
        dx, dw = _conv_grads(dpre, x_ref[...], w_ref, GDN_CONV, tm)
        o_ref[:, :QKV_W] = dx.astype(bf16)
        o_ref[:, QKV_W:] = dz_ref[...].astype(bf16)
        dw_ref[...] += _rows_to_block(dw, 8, QKV_W)

    row = pl.BlockSpec((tm, GDN_W), lambda i: (i, 0))
    fut = pl.BlockSpec((HALO, GDN_W), lambda i: (jnp.minimum((i + 1) * hb, last_hb), 0))
    wide = pl.BlockSpec((tm, QKV_W), lambda i: (i, 0))
    return pl.pallas_call(
        body, name=name, grid=(nt,),
        in_specs=[wide, wide, pl.BlockSpec((HALO, QKV_W), lambda i: (jnp.minimum((i + 1) * hb, last_hb), 0)),
                  pl.BlockSpec((8, QKV_W), lambda i: (0, 0)), row, fut, row, fut, row, fut, row],
        out_specs=[pl.BlockSpec((tm, 4 * GDN_W), lambda i: (i, 0)), pl.BlockSpec((8, QKV_W), lambda i: (0, 0))],
        out_shape=[jax.ShapeDtypeStruct((s, 4 * GDN_W), bf16), jax.ShapeDtypeStruct((8, QKV_W), f32)],
        compiler_params=_params("arbitrary"),
    )(gdn_in, conv_out, conv_out, convw, dqn, dqn, dkn, dkn, dv, dv, dgz)


def _gdn_post_head(o, z, nw):
    return o * lax.rsqrt(jnp.mean(o * o, axis=-1, keepdims=True) + 1e-6) * nw * (z * _sigmoid(z))


def gdn_post_fwd(o, gdn_in, nw, *, name):
    s = o.shape[0]
    tm = _tile(s, 256, 16)

    def body(o_ref, z_ref, nw_ref, y_ref):
        for h in range(GDN_HEADS):
            sl = slice(h * GDN_HEAD_DIM, (h + 1) * GDN_HEAD_DIM)
            y_ref[:, sl] = _gdn_post_head(o_ref[:, sl], z_ref[:, sl], nw_ref[...]).astype(bf16)

    row = pl.BlockSpec((tm, GDN_W), lambda i: (i, 0))
    return pl.pallas_call(body, name=name, grid=(s // tm,),
                          in_specs=[row, pl.BlockSpec((tm, GDN_W), lambda i: (i, 3)), pl.BlockSpec((1, 128), lambda i: (0, 0))],
                          out_specs=row, out_shape=jax.ShapeDtypeStruct((s, GDN_W), bf16),
                          compiler_params=_params("parallel"))(o, gdn_in, nw)


def gdn_post_bwd(o, gdn_in, nw, dy, *, name):
    s = o.shape[0]
    tm = _tile(s, 256, 16)

    def body(o_ref, z_ref, nw_ref, dy_ref, do_ref, dz_ref, dnw_ref):
        @pl.when(pl.program_id(0) == 0)
        def _():
            dnw_ref[...] = jnp.zeros_like(dnw_ref)

        dnw = jnp.zeros((1, 128), f32)
        for h in range(GDN_HEADS):
            sl = slice(h * GDN_HEAD_DIM, (h + 1) * GDN_HEAD_DIM)
            _, pull = jax.vjp(_gdn_post_head, o_ref[:, sl], z_ref[:, sl], nw_ref[...])
            do, dz, dn = pull(dy_ref[:, sl])
            do_ref[:, sl] = do
            dz_ref[:, sl] = dz
            dnw = dnw + dn
        dnw_ref[...] += dnw

    row = pl.BlockSpec((tm, GDN_W), lambda i: (i, 0))
    vec = pl.BlockSpec((1, 128), lambda i: (0, 0))
    return pl.pallas_call(body, name=name, grid=(s // tm,),
                          in_specs=[row, pl.BlockSpec((tm, GDN_W), lambda i: (i, 3)), vec, row], out_specs=[row, row, vec],
                          out_shape=[jax.ShapeDtypeStruct((s, GDN_W), f32), jax.ShapeDtypeStruct((s, GDN_W), f32),
                                     jax.ShapeDtypeStruct((1, 128), f32)],
                          compiler_params=_params("arbitrary"))(o, gdn_in, nw, dy)


def _dot_high(a, b, dims=NN):
    return lax.dot_general(a, b, (dims, ((), ())), precision=lax.Precision.HIGH, preferred_element_type=f32)


@jax.custom_vjp
def _unit_lower_inverses(a):
    c = a[0].shape[0]
    n = range(len(a))
    eye = (lax.broadcasted_iota(jnp.int32, (c, c), 0) == lax.broadcasted_iota(jnp.int32, (c, c), 1)).astype(f32)
    inv = [eye - a[i] for i in n]
    pw = [_dot_high(a[i], a[i]) for i in n]
    width = 2
    while width < c:
        inv = [inv[i] + _dot_high(inv[i], pw[i]) for i in n]
        width *= 2
        if width < c:
            pw = [_dot_high(pw[i], pw[i]) for i in n]
    return inv


def _unit_lower_inverses_fwd(a):
    inv = _unit_lower_inverses(a)
    return inv, inv


def _unit_lower_inverses_bwd(inv, g):
    return ([-_dot_high(_dot_high(x, gx, TN), x, NT) for x, gx in zip(inv, g)],)


_unit_lower_inverses.defvjp(_unit_lower_inverses_fwd, _unit_lower_inverses_bwd)


@jax.custom_vjp
def _known_inverses(a, inv):
    return inv


_known_inverses.defvjp(lambda a, inv: (inv, inv),
                       lambda inv, g: (_unit_lower_inverses_bwd(inv, g)[0], [jnp.zeros_like(x) for x in inv]))


def _gdn_chunks(q, k, v, gb, state, kept_inverses=None):
    c = GDN_CHUNK
    heads = range(len(q))
    r = lax.broadcasted_iota(jnp.int32, (c, c), 0)
    cc = lax.broadcasted_iota(jnp.int32, (c, c), 1)
    tril, strict = r >= cc, r > cc
    eye = (r == cc).astype(f32)

    def dhi(a, b):
        return jnp.dot(a, b, precision=lax.Precision.HIGH, preferred_element_type=f32)

    beta = [gb[:, h:h + 1] for h in heads]
    cum_cols = dhi(tril.astype(f32), gb)
    cum_rows = dhi(gb.T, (r <= cc).astype(f32))
    gi = [jnp.broadcast_to(cum_cols[:, GDN_HEADS + h:GDN_HEADS + h + 1], (c, c)) for h in heads]
    gj = [jnp.broadcast_to(cum_rows[GDN_HEADS + h:GDN_HEADS + h + 1, :], (c, c)) for h in heads]
    decay = [jnp.where(tril, jnp.exp(jnp.where(tril, gi[h] - gj[h], 0.0)), 0.0) for h in heads]
    kb = [k[h] * beta[h] for h in heads]
    vb = [v[h] * beta[h] for h in heads]
    a = [jnp.where(strict, _d16(kb[h], k[h], NT) * decay[h], 0.0) for h in heads]
    tinv = _unit_lower_inverses(a) if kept_inverses is None else _known_inverses(a, kept_inverses)
    gc = [gi[h][:, 0:1] for h in heads]
    egc = [jnp.exp(gc[h]) for h in heads]
    u = [dhi(tinv[h], vb[h]) for h in heads]
    w = [dhi(tinv[h], kb[h] * egc[h]) for h in heads]
    qs = [q[h] * (GDN_HEAD_DIM ** -0.5) for h in heads]
    attn = [jnp.where(tril, _d16(qs[h], k[h], NT) * decay[h], 0.0) for h in heads]
    g_last = [gi[h][c - 1:c, 0:1] for h in heads]
    v_new = [u[h] - _d16(w[h], state[h], NN) for h in heads]
    out = [_d16(qs[h] * egc[h], state[h], NN) + _d16(attn[h], v_new[h], NN) for h in heads]
    new_state = [state[h] * jnp.exp(g_last[h]) + _d16(k[h] * jnp.exp(g_last[h] - gc[h]), v_new[h], TN) for h in heads]
    return out, new_state, tinv


def _head_cols(ref):
    return [ref[:, h * GDN_HEAD_DIM:(h + 1) * GDN_HEAD_DIM] for h in range(GDN_HEADS)]


def gdn_scan_fwd(qn, kn, v, gbeta, *, name, side=None):
    s = qn.shape[0]
    nc = s // GDN_CHUNK

    def body(q_ref, k_ref, v_ref, gb_ref, o_ref, st_ref, inv_ref, state_ref):
        @pl.when(pl.program_id(0) == 0)
        def _():
            state_ref[...] = jnp.zeros_like(state_ref)

        states = [state_ref[h] for h in range(GDN_HEADS)]
        outs, new, inverses = _gdn_chunks(_head_cols(q_ref), _head_cols(k_ref), _head_cols(v_ref), gb_ref[...], states)
        for h in range(GDN_HEADS):
            st_ref[0, h] = states[h]
            inv_ref[0, h] = inverses[h]
            o_ref[:, h * GDN_HEAD_DIM:(h + 1) * GDN_HEAD_DIM] = outs[h]
            state_ref[h] = new[h]

    row = pl.BlockSpec((GDN_CHUNK, GDN_W), lambda n: (n, 0))
    return _call(
        body, (qn, kn, v, gbeta), name=name, grid=(nc,),
        in_specs=[row, row, row, pl.BlockSpec((GDN_CHUNK, 128), lambda n: (n, 0))],
        out_specs=[row, pl.BlockSpec((1, GDN_HEADS, GDN_HEAD_DIM, GDN_HEAD_DIM), lambda n: (n, 0, 0, 0)),
                   pl.BlockSpec((1, GDN_HEADS, GDN_CHUNK, GDN_CHUNK), lambda n: (n, 0, 0, 0))],
        out_shape=[jax.ShapeDtypeStruct((s, GDN_W), f32),
                   jax.ShapeDtypeStruct((nc, GDN_HEADS, GDN_HEAD_DIM, GDN_HEAD_DIM), f32),
                   jax.ShapeDtypeStruct((nc, GDN_HEADS, GDN_CHUNK, GDN_CHUNK), f32)],
        scratch_shapes=[pltpu.VMEM((GDN_HEADS, GDN_HEAD_DIM, GDN_HEAD_DIM), f32)], semantics=("arbitrary",), side=side)


def gdn_scan_bwd(qn, kn, v, gbeta, states, inverses, dout, *, name, side=None):
    s = qn.shape[0]
    nc = s // GDN_CHUNK

    def body(q_ref, k_ref, v_ref, gb_ref, st_ref, inv_ref, do_ref, dq_ref, dk_ref, dv_ref, dgb_ref, dstate_ref):
        @pl.when(pl.program_id(0) == 0)
        def _():
            dstate_ref[...] = jnp.zeros_like(dstate_ref)

        kept = [inv_ref[0, h] for h in range(GDN_HEADS)]
        _, pull = jax.vjp(lambda *args: _gdn_chunks(*args, kept_inverses=kept)[:2],
                          _head_cols(q_ref), _head_cols(k_ref), _head_cols(v_ref), gb_ref[...],
                          [st_ref[0, h] for h in range(GDN_HEADS)])
        dq, dk, dv, dgb, dst = pull((_head_cols(do_ref), [dstate_ref[h] for h in range(GDN_HEADS)]))
        for h in range(GDN_HEADS):
            sl = slice(h * GDN_HEAD_DIM, (h + 1) * GDN_HEAD_DIM)
            dq_ref[:, sl] = dq[h]
            dk_ref[:, sl] = dk[h]
            dv_ref[:, sl] = dv[h]
            dstate_ref[h] = dst[h]
        dgb_ref[...] = dgb

    row = pl.BlockSpec((GDN_CHUNK, GDN_W), lambda n: (nc - 1 - n, 0))
    gb = pl.BlockSpec((GDN_CHUNK, 128), lambda n: (nc - 1 - n, 0))
    return _call(
        body, (qn, kn, v, gbeta, states, inverses, dout), name=name, grid=(nc,),
        in_specs=[row, row, row, gb, pl.BlockSpec((1, GDN_HEADS, GDN_HEAD_DIM, GDN_HEAD_DIM), lambda n: (nc - 1 - n, 0, 0, 0)),
                  pl.BlockSpec((1, GDN_HEADS, GDN_CHUNK, GDN_CHUNK), lambda n: (nc - 1 - n, 0, 0, 0)), row],
        out_specs=[row, row, row, gb],
        out_shape=[jax.ShapeDtypeStruct((s, GDN_W), f32)] * 3 + [jax.ShapeDtypeStruct((s, 128), f32)],
        scratch_shapes=[pltpu.VMEM((GDN_HEADS, GDN_HEAD_DIM, GDN_HEAD_DIM), f32)], semantics=("arbitrary",), side=side)


def adamw(w, g, m, v, *, name):
    r, c = w.shape
    tr = _tile(r, 256, 8)

    def body(w_ref, g_ref, m_ref, v_ref, d_ref, nm_ref, nv_ref):
        gv = g_ref[...]
        nm = ADAM_B1 * m_ref[...] + (1.0 - ADAM_B1) * gv
        nv = ADAM_B2 * v_ref[...] + (1.0 - ADAM_B2) * (gv * gv)
        m_hat = nm / (1.0 - ADAM_B1 ** ADAM_STEP)
        v_hat = nv / (1.0 - ADAM_B2 ** ADAM_STEP)
        d_ref[...] = -ADAM_LR * (m_hat / (jnp.sqrt(v_hat) + ADAM_EPS) + ADAM_WD * w_ref[...])
        nm_ref[...] = nm
        nv_ref[...] = nv

    spec = pl.BlockSpec((tr, c), lambda i: (i, 0))
    return pl.pallas_call(body, name=name, grid=(r // tr,), in_specs=[spec] * 4, out_specs=[spec] * 3,
                          out_shape=[jax.ShapeDtypeStruct((r, c), f32)] * 3, compiler_params=_params("parallel"))(w, g, m, v)


def _pos():
    return lax.axis_index("x"), lax.axis_index("y"), lax.axis_index("c")


ANY = pl.BlockSpec(memory_space=pl.ANY)


class Side(NamedTuple):
    ins: list
    outs: list
    aliases: dict
    sems: list
    start: Callable
    wait: Callable


def join_sides(*sides):
    def spans(key):
        out, off = [], 0
        for sd in sides:
            out.append(slice(off, off + len(getattr(sd, key))))
            off += len(getattr(sd, key))
        return out

    i_sp, o_sp, s_sp = spans("ins"), spans("outs"), spans("sems")
    aliases = {i_sp[n].start + i: o_sp[n].start + o for n, sd in enumerate(sides) for i, o in sd.aliases.items()}

    def each(what):
        def run(ins, outs, sems):
            for n, sd in enumerate(sides):
                getattr(sd, what)(ins[i_sp[n]], outs[o_sp[n]], sems[s_sp[n]])
        return run

    return Side([a for sd in sides for a in sd.ins], [o for sd in sides for o in sd.outs], aliases,
                [s for sd in sides for s in sd.sems], each("start"), each("wait"))


def _side_body(body, side, n_in, n_out, n_scratch, grid):
    ns_in, ns_out = len(side.ins), len(side.outs)

    def wrapped(*refs):
        cut = [n_in, ns_in, n_out, ns_out, n_scratch]
        parts, off = [], 0
        for c in cut:
            parts.append(refs[off:off + c])
            off += c
        ins, s_ins, outs, s_outs, scratch = parts
        sems = refs[off:]
        if grid:
            ids = [pl.program_id(d) for d in range(len(grid))]
            first = functools.reduce(jnp.logical_and, [i == 0 for i in ids])
            last = functools.reduce(jnp.logical_and, [i == g - 1 for i, g in zip(ids, grid)])
            pl.when(first)(lambda: side.start(s_ins, s_outs, sems))
            body(*ins, *outs, *scratch)
            pl.when(last)(lambda: side.wait(s_ins, s_outs, sems))
        else:
            side.start(s_ins, s_outs, sems)
            side.wait(s_ins, s_outs, sems)

    return wrapped


def _call(body, args, *, name, grid, in_specs, out_specs, out_shape, semantics, scratch_shapes=(), side=None):
    if side is None:
        return pl.pallas_call(body, name=name, grid=grid, in_specs=in_specs, out_specs=out_specs, out_shape=out_shape,
                              scratch_shapes=list(scratch_shapes), compiler_params=_params(*semantics))(*args)
    single = not isinstance(out_shape, (list, tuple))
    shapes, specs = ([out_shape], [out_specs]) if single else (list(out_shape), list(out_specs))
    n_in, n_out = len(in_specs), len(shapes)
    res = pl.pallas_call(
        _side_body(body, side, n_in, n_out, len(scratch_shapes), grid), name=name, grid=grid,
        in_specs=list(in_specs) + [ANY] * len(side.ins), out_specs=specs + [ANY] * len(side.outs),
        out_shape=shapes + list(side.outs), scratch_shapes=list(scratch_shapes) + list(side.sems),
        input_output_aliases={n_in + i: n_out + o for i, o in side.aliases.items()},
        compiler_params=_params(*(["arbitrary"] * len(grid))),
    )(*args, *side.ins)
    return (res[0] if single else res[:n_out]), list(res[n_out:])


def run_side(side, *, name):
    return pl.pallas_call(_side_body(None, side, 0, 0, 0, ()), name=name, in_specs=[ANY] * len(side.ins),
                          out_specs=[ANY] * len(side.outs), out_shape=list(side.outs), scratch_shapes=list(side.sems),
                          input_output_aliases=dict(side.aliases))(*side.ins)


def _remote(src, dst, send, recv, k, to):
    return pltpu.make_async_remote_copy(src_ref=src, dst_ref=dst, send_sem=send.at[k], recv_sem=recv.at[k], device_id=to,
                                        device_id_type=MESH)


def gather_first(shards):
    na = len(shards)

    def copies(x_refs, out_refs, sems):
        send, recv, local = sems
        x, y, cc = _pos()
        me = 4 * x + 2 * y + cc
        peers = [(x, y, 1 - cc), (1 - x, y, cc), (x, 1 - y, cc), (1 - x, 1 - y, cc)]
        mine = [pltpu.make_async_copy(x_refs[a], out_refs[a].at[me], local.at[a]) for a in range(na)]
        sent = [_remote(x_refs[a], out_refs[a].at[me], send, recv, 4 * a + k, p) for a in range(na) for k, p in enumerate(peers)]
        landed = [_remote(x_refs[a], out_refs[a].at[4 * p[0] + 2 * p[1] + p[2]], send, recv, 4 * a + k, p)
                  for a in range(na) for k, p in enumerate(peers)]
        return mine, sent, landed

    def start(x_refs, out_refs, sems):
        mine, sent, _ = copies(x_refs, out_refs, sems)
        for cp in mine + sent:
            cp.start()

    def wait(x_refs, out_refs, sems):
        mine, sent, landed = copies(x_refs, out_refs, sems)
        for cp in sent:
            cp.wait_send()
        for cp in landed:
            cp.wait_recv()
        for cp in mine:
            cp.wait()

    return Side(list(shards), [jax.ShapeDtypeStruct((N_DEV,) + s.shape, s.dtype) for s in shards], {},
                [pltpu.SemaphoreType.DMA((4 * na,)), pltpu.SemaphoreType.DMA((4 * na,)), pltpu.SemaphoreType.DMA((na,))],
                start, wait)


def gather_second(slots):
    na = len(slots)

    def copies(out_refs, sems):
        send, recv = sems
        x, y, cc = _pos()
        chips = [(1 - x, y), (x, 1 - y), (1 - x, 1 - y)]
        sent, landed = [], []
        for a in range(na):
            for j, (px, py) in enumerate(chips):
                row = out_refs[a].at[4 * px + 2 * py + cc]
                sent.append(_remote(row, row, send, recv, 3 * a + j, (x, y, 1 - cc)))
                landed.append(_remote(row, out_refs[a].at[4 * px + 2 * py + 1 - cc], send, recv, 3 * a + j, (x, y, 1 - cc)))
        return sent, landed

    def start(_, out_refs, sems):
        for cp in copies(out_refs, sems)[0]:
            cp.start()

    def wait(_, out_refs, sems):
        sent, landed = copies(out_refs, sems)
        for cp in sent:
            cp.wait_send()
        for cp in landed:
            cp.wait_recv()

    return Side(list(slots), [jax.ShapeDtypeStruct(s.shape, s.dtype) for s in slots], {a: a for a in range(na)},
                [pltpu.SemaphoreType.DMA((3 * na,)), pltpu.SemaphoreType.DMA((3 * na,))], start, wait)


def grad_to_sibling(chunks):
    na = len(chunks)

    def start(g_refs, out_refs, sems):
        send, recv = sems
        x, y, cc = _pos()
        for a in range(na):
            for q in range(4):
                _remote(g_refs[a].at[2 * q + 1 - cc], out_refs[a].at[q], send, recv, a, (x, y, 1 - cc)).start()

    def wait(g_refs, out_refs, sems):
        send, recv = sems
        x, y, cc = _pos()
        for a in range(na):
            _remote(out_refs[a], out_refs[a], send, recv, a, (x, y, 1 - cc)).wait()

    return Side(list(chunks), [jax.ShapeDtypeStruct((4,) + g.shape[1:], g.dtype) for g in chunks], {},
                [pltpu.SemaphoreType.DMA((na,)), pltpu.SemaphoreType.DMA((na,))], start, wait)


def grad_to_chips(parts):
    na = len(parts)

    def copies(p_refs, out_refs, sems):
        send, recv, local = sems
        x, y, cc = _pos()
        chips = [(1 - x, y), (x, 1 - y), (1 - x, 1 - y)]
        mine = [pltpu.make_async_copy(p_refs[a].at[2 * x + y], out_refs[a].at[3], local.at[a]) for a in range(na)]
        sent = [_remote(p_refs[a].at[2 * px + py], out_refs[a].at[k], send, recv, 3 * a + k, (px, py, cc))
                for a in range(na) for k, (px, py) in enumerate(chips)]
        return mine, sent

    def start(p_refs, out_refs, sems):
        mine, sent = copies(p_refs, out_refs, sems)
        for cp in mine + sent:
            cp.start()

    def wait(p_refs, out_refs, sems):
        mine, sent = copies(p_refs, out_refs, sems)
        for cp in sent:
            cp.wait()
        for cp in mine:
            cp.wait()

    return Side(list(parts), [jax.ShapeDtypeStruct(p.shape, p.dtype) for p in parts], {},
                [pltpu.SemaphoreType.DMA((3 * na,)), pltpu.SemaphoreType.DMA((3 * na,)), pltpu.SemaphoreType.DMA((na,))],
                start, wait)


def add_sibling(chunks, recv, *, name):
    _, r, c = chunks.shape
    tr = r if r <= 1024 else _tile(r, 512, 16)
    core = lax.axis_index("c").astype(jnp.int32).reshape(1)

    def body(core_ref, a_ref, b_ref, o_ref):
        o_ref[...] = (a_ref[...] + b_ref[...]).astype(bf16)

    return pl.pallas_call(
        body, name=name,
        grid_spec=pltpu.PrefetchScalarGridSpec(
            num_scalar_prefetch=1, grid=(4, r // tr),
            in_specs=[pl.BlockSpec((1, tr, c), lambda q, i, core_ref: (2 * q + core_ref[0], i, 0)),
                      pl.BlockSpec((1, tr, c), lambda q, i, core_ref: (q, i, 0))],
            out_specs=pl.BlockSpec((1, tr, c), lambda q, i, core_ref: (q, i, 0))),
        out_shape=jax.ShapeDtypeStruct((4, r, c), bf16), compiler_params=_params("parallel", "parallel"),
    )(core, chunks, recv)


def add_four(r4, *, name):
    _, r, c = r4.shape
    tr = r if r <= 1024 else _tile(r, 512, 16)

    def body(a_ref, o_ref):
        o_ref[...] = ((a_ref[3].astype(f32) + a_ref[0].astype(f32)) + a_ref[1].astype(f32)) + a_ref[2].astype(f32)

    return pl.pallas_call(body, name=name, grid=(r // tr,), in_specs=[pl.BlockSpec((4, tr, c), lambda i: (0, i, 0))],
                          out_specs=pl.BlockSpec((tr, c), lambda i: (i, 0)), out_shape=jax.ShapeDtypeStruct((r, c), f32),
                          compiler_params=_params("parallel"))(r4)


def all_reduce_small(vec, *, name):
    r, c = vec.shape

    def body(v_ref, out_ref, buf_ref, send_sems, recv_sems):
        x, y, cc = _pos()
        my_id = 4 * x + 2 * y + cc
        buf_ref[my_id] = v_ref[...]
        flips = [(fx, fy, fc) for fx in (0, 1) for fy in (0, 1) for fc in (0, 1)][1:]
        cps = []
        for k, (fx, fy, fc) in enumerate(flips):
            peer = ((1 - x) if fx else x, (1 - y) if fy else y, (1 - cc) if fc else cc)
            cps.append(pltpu.make_async_remote_copy(src_ref=v_ref, dst_ref=buf_ref.at[my_id], send_sem=send_sems.at[k],
                                                    recv_sem=recv_sems.at[k], device_id=peer, device_id_type=MESH))
        for cp in cps:
            cp.start()
        for cp in cps:
            cp.wait()
        acc = buf_ref[0]
        for d in range(1, N_DEV):
            acc = acc + buf_ref[d]
        out_ref[...] = acc

    vm = pl.BlockSpec(memory_space=pltpu.VMEM)
    return pl.pallas_call(body, name=name, in_specs=[vm], out_specs=vm, out_shape=jax.ShapeDtypeStruct((r, c), f32),
                          scratch_shapes=[pltpu.VMEM((N_DEV, r, c), f32), pltpu.SemaphoreType.DMA((7,)),
                                          pltpu.SemaphoreType.DMA((7,))])(vec)


def _pack(parts, rows, dtype):
    flat = jnp.concatenate([p.reshape(-1).astype(dtype) for p in parts])
    return jnp.pad(flat, (0, rows * PACK_COLS - flat.shape[0])).reshape(rows, PACK_COLS)


def _unpack(flat, shapes):
    out, off = [], 0
    for shp in shapes:
        n = shp[0] * shp[1]
        out.append(flat[..., off:off + n].reshape(flat.shape[:-1] + tuple(shp)))
        off += n
    return out


def _from_column_shards(g):
    _, r, c = g.shape
    return jnp.transpose(g, (1, 0, 2)).reshape(r, N_DEV * c)


def _column_shards(full):
    r, c8 = full.shape
    return jnp.transpose(full.reshape(r, N_DEV, c8 // N_DEV), (1, 0, 2))


W_IN_SHARD = IN_DIM // N_DEV
W_IN_PAD = 1280
W_IN_PARTS = (("swa", 0, 0, 1280), ("swa", 1280, 5376, 5392), ("gdn", 0, 1280, 5376), ("gates", 0, 5392, IN_DIM))
W_IN_WIDTHS = {"swa": SWA_IN_W, "gdn": 4 * GDN_W, "gates": 2 * D_MODEL}


def _w_in_segments():
    segs = []
    for part, p0, g0, g1 in W_IN_PARTS:
        for j in range(N_DEV):
            lo, hi = max(g0, W_IN_SHARD * j), min(g1, W_IN_SHARD * (j + 1))
            if lo < hi:
                segs.append((part, p0 + lo - g0, j, lo - W_IN_SHARD * j, hi - lo))
    return segs


def split_w_in(shards, *, name):
    dt = shards.dtype
    tm = 256

    def body(w_ref, swa_ref, gdn_ref, gates_ref):
        out = {"swa": swa_ref, "gdn": gdn_ref, "gates": gates_ref}
        swa_ref[:, SWA_Q + 2 * SWA_KV + 2 * GDN_HEADS:] = jnp.zeros((tm, SWA_IN_W - SWA_Q - 2 * SWA_KV - 2 * GDN_HEADS), dt)
        for part, p0, j, l0, n in _w_in_segments():
            out[part][:, p0:p0 + n] = w_ref[j, :, l0:l0 + n]

    return pl.pallas_call(body, name=name, grid=(D_MODEL // tm,),
                          in_specs=[pl.BlockSpec((N_DEV, tm, W_IN_PAD), lambda i: (0, i, 0))],
                          out_specs=[pl.BlockSpec((tm, W_IN_WIDTHS[p]), lambda i: (i, 0)) for p in ("swa", "gdn", "gates")],
                          out_shape=[jax.ShapeDtypeStruct((D_MODEL, W_IN_WIDTHS[p]), dt) for p in ("swa", "gdn", "gates")],
                          compiler_params=_params("parallel"))(shards)


def merge_w_in_grad(d_swa, d_gdn, d_gates, *, name):
    tm = 256

    def body(swa_ref, gdn_ref, gates_ref, w_ref):
        src = {"swa": swa_ref, "gdn": gdn_ref, "gates": gates_ref}
        w_ref[:, :, W_IN_SHARD:] = jnp.zeros((N_DEV, tm, W_IN_PAD - W_IN_SHARD), f32)
        for part, p0, j, l0, n in _w_in_segments():
            w_ref[j, :, l0:l0 + n] = src[part][:, p0:p0 + n]

    return pl.pallas_call(body, name=name, grid=(D_MODEL // tm,),
                          in_specs=[pl.BlockSpec((tm, W_IN_WIDTHS[p]), lambda i: (i, 0)) for p in ("swa", "gdn", "gates")],
                          out_specs=pl.BlockSpec((N_DEV, tm, W_IN_PAD), lambda i: (0, i, 0)),
                          out_shape=jax.ShapeDtypeStruct((N_DEV, D_MODEL, W_IN_PAD), f32),
                          compiler_params=_params("parallel"))(d_swa, d_gdn, d_gates)


def kernel(x, mem, w_in, rel_bias, swa_sinks, gdn_conv_w, gdn_a_log, gdn_dt_bias, gdn_norm_w, w_br_swa, w_br_gdn, w_mix_o, ln1_g, ln1_b, w_mem_q, w_mem_kv, w_mem_o, ln2_g, ln2_b, w_up, ffn_conv_w, ffn_conv_b, w_down, ln3_g, ln3_b, loss_target, m_w_in, m_rel_bias, m_swa_sinks, m_gdn_conv_w, m_gdn_a_log, m_gdn_dt_bias, m_gdn_norm_w, m_w_br_swa, m_w_br_gdn, m_w_mix_o, m_ln1_g, m_ln1_b, m_w_mem_q, m_w_mem_kv, m_w_mem_o, m_ln2_g, m_ln2_b, m_w_up, m_ffn_conv_w, m_ffn_conv_b, m_w_down, m_ln3_g, m_ln3_b, v_w_in, v_rel_bias, v_swa_sinks, v_gdn_conv_w, v_gdn_a_log, v_gdn_dt_bias, v_gdn_norm_w, v_w_br_swa, v_w_br_gdn, v_w_mix_o, v_ln1_g, v_ln1_b, v_w_mem_q, v_w_mem_kv, v_w_mem_o, v_ln2_g, v_ln2_b, v_w_up, v_ffn_conv_w, v_ffn_conv_b, v_w_down, v_ln3_g, v_ln3_b):
    env = dict(locals())
    w2 = {n: (env[n][0] if env[n].ndim == 3 else env[n]) for n in WEIGHTS}
    m2 = {n: (env["m_" + n][0] if env["m_" + n].ndim == 3 else env["m_" + n]) for n in WEIGHTS}
    v2 = {n: (env["v_" + n][0] if env["v_" + n].ndim == 3 else env["v_" + n]) for n in WEIGHTS}
    xs, mems, target = x[0], mem[0], loss_target[0]
    my_id = 4 * lax.axis_index("x") + 2 * lax.axis_index("y") + lax.axis_index("c")
    pad_ff = FF_PAD - FF_SHARD

    pad_cols = {"w_in": W_IN_PAD - W_IN_SHARD, "w_up": pad_ff}
    mid = ("w_br_swa", "w_br_gdn", "w_mem_o", "w_mix_o", "w_mem_q", "w_mem_kv")
    mine = {n: jnp.pad(w2[n], ((0, 0), (0, pad_cols.get(n, 0)))).astype(bf16) for n in ("w_in", "w_up", "w_down") + mid}
    got_in = run_side(gather_second(run_side(gather_first([mine["w_in"]]), name="gather_w_in")), name="gather_w_in_pass_on")
    w_swa, w_gdn, w_gates = split_w_in(got_in[0], name="split_w_in")
    n_ffn, n_gdn = 3 * FF_SHARD, GDN_CONV * (QKV_W // N_DEV)
    conv_mine = jnp.concatenate([w2["ffn_conv_w"].reshape(-1), w2["gdn_conv_w"].reshape(-1)])[None]
    conv_rows = lax.dynamic_update_slice(jnp.zeros((N_DEV, n_ffn + n_gdn), f32), conv_mine, (my_id, 0))
    conv_all = all_reduce_small(_pack([conv_rows], CONV_ROWS, f32), name="gather_conv_w")
    conv_all = conv_all.reshape(-1)[:N_DEV * (n_ffn + n_gdn)].reshape(N_DEV, n_ffn + n_gdn)
    cwb = jnp.concatenate([conv_all[:, :n_ffn].reshape(N_DEV, 3, FF_SHARD), w2["ffn_conv_b"].reshape(N_DEV, 1, FF_SHARD),
                           jnp.zeros((N_DEV, 4, FF_SHARD), f32)], axis=1)
    cwb = jnp.pad(cwb, ((0, 0), (0, 0), (0, pad_ff)))
    convw = jnp.transpose(conv_all[:, n_ffn:].reshape(N_DEV, GDN_CONV, QKV_W // N_DEV), (1, 0, 2)).reshape(GDN_CONV, QKV_W)
    convw = jnp.pad(convw, ((0, 4), (0, 0)))
    onehot = _bucket_onehot()
    bias = mm(w2["rel_bias"].T, onehot, "nn", hi=True, tn=4096, name="rel_bias_table").reshape(SWA_HEADS, BLOCK, 2 * BLOCK)
    alog_row = jnp.pad(w2["gdn_a_log"], ((0, 0), (GDN_HEADS, 128 - 2 * GDN_HEADS)))
    dt_row = jnp.pad(w2["gdn_dt_bias"], ((0, 0), (GDN_HEADS, 128 - 2 * GDN_HEADS)))

    xb = cast_bf16(xs, name="cast_x")
    memb = cast_bf16(mems, name="cast_mem")
    gates, mid_got = mm(xb, w_gates, "nn", name="proj_gates", side=gather_first([mine[n] for n in mid]))
    gdn_in, mid_got = mm(xb, w_gdn, "nn", name="proj_gdn", side=gather_second(mid_got))
    got = dict(zip(mid, mid_got))
    w_br_swa, w_br_gdn, w_mem_o = (_from_column_shards(got[n]) for n in ("w_br_swa", "w_br_gdn", "w_mem_o"))
    w_mix_o = got["w_mix_o"].reshape(D_MODEL, D_MODEL)
    w_mem_q = got["w_mem_q"].reshape(D_MODEL, MEM_W)
    w_mem_kv = got["w_mem_kv"].reshape(D_MODEL, 2 * MEM_W)
    swa_in = mm(xb, w_swa, "nn", tn=SWA_IN_W, name="proj_swa")
    attn, down_got = swa_fwd(swa_in, bias, w2["swa_sinks"], name="swa_fwd", side=gather_first([mine["w_down"]]))
    qn, kn, vv, gdn_conv = gdn_pre_fwd(gdn_in, convw, name="gdn_pre_fwd")
    gbeta = gbeta_fwd(swa_in, alog_row, dt_row, name="gbeta_fwd")
    (o_gdn, states, inverses), up_got = gdn_scan_fwd(qn, kn, vv, gbeta, name="gdn_scan_fwd",
                                                     side=gather_first([mine["w_up"]]))
    ygd = gdn_post_fwd(o_gdn, gdn_in, w2["gdn_norm_w"], name="gdn_post_fwd")
    y_swa, down_got = mm(attn, w_br_swa, "nn", name="br_swa", side=gather_second(down_got))
    y_gdn, up_got = mm(ygd, w_br_gdn, "nn", name="br_gdn", side=gather_second(up_got))
    w_up_blocked = up_got[0]
    w_down_p = jnp.pad(down_got[0].reshape(4, FF_SHARD, D_MODEL), ((0, 0), (0, pad_ff), (0, 0))).reshape(4 * FF_PAD, D_MODEL)
    mixed = merge_fwd(gates, y_swa, y_gdn, name="merge_fwd")
    z1 = mm(mixed, w_mix_o, "nn", add=xs, add_scale=ALPHA, name="mix_o")
    x1, x1b = ln_fwd(z1, w2["ln1_g"], w2["ln1_b"], name="ln1_fwd")
    qm = mm(x1b, w_mem_q, "nn", name="mem_q")
    kv = mm(memb, w_mem_kv, "nn", name="mem_kv")
    om = memattn_fwd(qm, kv, name="memattn_fwd")
    z2 = mm(om, w_mem_o, "nn", add=x1, add_scale=ALPHA, name="mem_o")
    x2, x2b = ln_fwd(z2, w2["ln2_g"], w2["ln2_b"], name="ln2_fwd")
    hpre = mm(x2b, w_up_blocked, "nn", b_blocked=True, name="ffn_up")
    act, conv_g, conv_u = ffn_act_fwd(hpre, cwb, name="ffn_act_fwd")
    z3 = mm(act, w_down_p, "nn", add=x2, add_scale=ALPHA, tk=2 * FF_PAD, name="ffn_down")
    dz3, dz3b, d_ln3g, d_ln3b, loss = ln_loss(z3, target, w2["ln3_g"], w2["ln3_b"], name="ln3_loss")

    dact = mm(dz3b, w_down_p, "nt", tn=FF_PAD, name="d_act")
    d_wdown_p = mm(act, dz3b, "tn", tm=FF_PAD, name="dw_down")
    d_hpre, d_cwb = ffn_act_bwd(hpre, conv_g, conv_u, dact, cwb, name="ffn_act_bwd")
    def sibling_sums(names, chunks, received):
        return [add_sibling(c, r, name="grad_add_sibling_" + n) for n, c, r in zip(names, chunks, received)]

    def chip_sums(names, received):
        return [add_four(r, name="grad_add_chips_" + n) for n, r in zip(names, received)]

    dx2 = mm(d_hpre, w_up_blocked, "nt", b_blocked=True, add=dz3, add_scale=ALPHA, name="d_x2")
    d_wup = mm(x2b, d_hpre, "tn", out_blocked=True, name="dw_up")
    ffn = ("w_up", "w_down")
    ffn_chunks = [d_wup, d_wdown_p.reshape(4, FF_PAD, D_MODEL)[:, :FF_SHARD].reshape(N_DEV, FF_SHARD // 2, D_MODEL)]
    dz2, dz2b, d_ln2g, d_ln2b = ln_bwd(dx2, z2, w2["ln2_g"], name="ln2_bwd")
    d_om, down_received = mm(dz2b, w_mem_o, "nt", name="d_om", side=grad_to_sibling(ffn_chunks[1:]))
    d_wmemo = mm(om, dz2b, "tn", name="dw_mem_o")
    dqm, dkv = memattn_bwd(qm, kv, d_om, name="memattn_bwd")
    dx1 = mm(dqm, w_mem_q, "nt", add=dz2, add_scale=ALPHA, name="d_x1")
    d_wmemq = mm(x1b, dqm, "tn", name="dw_mem_q")
    d_wmemkv = mm(memb, dkv, "tn", name="dw_mem_kv")
    dz1, dz1b, d_ln1g, d_ln1b = ln_bwd(dx1, z1, w2["ln1_g"], name="ln1_bwd")
    dmix, up_received = mm(dz1b, w_mix_o, "nt", name="d_mixed", side=grad_to_sibling(ffn_chunks[:1]))
    ffn_sums = sibling_sums(ffn, ffn_chunks, up_received + down_received)
    d_wmixo = mm(mixed, dz1b, "tn", name="dw_mix_o")
    dys, dyg, d_gates = merge_bwd(gates, y_swa, y_gdn, dmix, name="merge_bwd")
    d_attn = mm(dys, w_br_swa, "nt", name="d_attn")
    d_wbrswa = mm(attn, dys, "tn", name="dw_br_swa")
    d_ygd = mm(dyg, w_br_gdn, "nt", name="d_ygd")
    d_wbrgdn = mm(ygd, dyg, "tn", name="dw_br_gdn")
    mid_chunks = [_column_shards(d_wbrswa), _column_shards(d_wbrgdn), _column_shards(d_wmemo),
                  d_wmixo.reshape(N_DEV, D_MODEL // N_DEV, D_MODEL), d_wmemq.reshape(N_DEV, D_MODEL // N_DEV, MEM_W),
                  d_wmemkv.reshape(N_DEV, D_MODEL // N_DEV, 2 * MEM_W)]
    d_o, d_gz, d_normw = gdn_post_bwd(o_gdn, gdn_in, w2["gdn_norm_w"], d_ygd, name="gdn_post_bwd")
    (dqn, dkn, dvv, dgbeta), received = gdn_scan_bwd(
        qn, kn, vv, gbeta, states, inverses, d_o, name="gdn_scan_bwd",
        side=join_sides(grad_to_chips(ffn_sums), grad_to_sibling(mid_chunks)))
    grads = dict(zip(ffn, chip_sums(ffn, received[:2])))
    mid_sums = sibling_sums(mid, mid_chunks, received[2:])
    d_gdn_in, d_convw = gdn_pre_bwd(gdn_in, gdn_conv, convw, dqn, dkn, dvv, d_gz, name="gdn_pre_bwd")
    d_ba, d_alog, d_dt = gbeta_bwd(swa_in, alog_row, dt_row, dgbeta, name="gbeta_bwd")
    (dq, dkc, dkp, dvc, dvp, d_bias, d_sinks), received = swa_bwd(swa_in, bias, w2["swa_sinks"], d_attn, name="swa_bwd",
                                                                  side=grad_to_chips(mid_sums))
    grads.update(zip(mid, chip_sums(mid, received)))
    d_swa_in = swa_in_grad(dq, dkc, dkp, dvc, dvp, d_ba, name="swa_in_grad")
    d_relbias = mm(d_bias.reshape(SWA_HEADS, -1), onehot, "nt", hi=True, tk=4096, name="d_rel_bias").T
    d_wgates = mm(xb, d_gates, "tn", name="dw_gates")
    d_wgdn = mm(xb, d_gdn_in, "tn", name="dw_gdn")
    d_wswa = mm(xb, d_swa_in, "tn", tn=SWA_IN_W, name="dw_swa")
    in_chunks = [merge_w_in_grad(d_wswa, d_wgdn, d_wgates, name="merge_w_in_grad")]
    gx, received = mm(d_gates, w_gates, "nt", add=dz1, add_scale=ALPHA, name="dx_gates", side=grad_to_sibling(in_chunks))
    in_sums = sibling_sums(("w_in",), in_chunks, received)
    gx, received = mm(d_gdn_in, w_gdn, "nt", add=gx, name="dx_gdn", side=grad_to_chips(in_sums))
    gx = mm(d_swa_in, w_swa, "nt", add=gx, tk=SWA_IN_W, name="dx_swa")
    grads["w_in"] = chip_sums(("w_in",), received)[0][:, :W_IN_SHARD]
    grads["w_up"] = grads["w_up"][:, :FF_SHARD]

    gsmall = {
        "rel_bias": d_relbias, "swa_sinks": d_sinks[:, :SWA_HEADS], "gdn_a_log": d_alog[:, GDN_HEADS:2 * GDN_HEADS],
        "gdn_dt_bias": d_dt[:, GDN_HEADS:2 * GDN_HEADS], "gdn_norm_w": d_normw, "ln1_g": d_ln1g, "ln1_b": d_ln1b,
        "ln2_g": d_ln2g, "ln2_b": d_ln2b, "ln3_g": d_ln3g, "ln3_b": d_ln3b,
        "ffn_conv_b": d_cwb[:, 3, :FF_SHARD].reshape(1, 2 * D_FF),
        "ffn_conv_w": jnp.transpose(d_cwb[:, :3, :FF_SHARD], (1, 0, 2)).reshape(3, 2 * D_FF),
        "gdn_conv_w": d_convw[:GDN_CONV],
    }
    small_shapes = [shp for _, shp in SMALL] + [(3, 2 * D_FF), (GDN_CONV, QKV_W)]
    small_names = [n for n, _ in SMALL] + ["ffn_conv_w", "gdn_conv_w"]
    small_sum = all_reduce_small(_pack([gsmall[n] for n in small_names], AR_ROWS, f32), name="all_reduce_small")
    grads.update(zip(small_names, _unpack(small_sum.reshape(-1), small_shapes)))
    grads["ffn_conv_w"] = lax.dynamic_slice_in_dim(grads["ffn_conv_w"], my_id * FF_SHARD, FF_SHARD, axis=1)
    grads["gdn_conv_w"] = lax.dynamic_slice_in_dim(grads["gdn_conv_w"], my_id * (QKV_W // N_DEV), QKV_W // N_DEV, axis=1)

    big = [n for n, shp, _ in SHARDED if shp[0] * shp[1] > 8192]
    tiny = [n for n in WEIGHTS if n not in big]
    delta, new_m, new_v = {}, {}, {}
    for n in big:
        delta[n], new_m[n], new_v[n] = adamw(w2[n], grads[n], m2[n], v2[n], name="adamw_" + n)
    tiny_shapes = [w2[n].shape for n in tiny]
    packed = [_pack([src[n] for n in tiny], SMALL_ROWS, f32) for src in (w2, grads, m2, v2)]
    for dst, res in zip((delta, new_m, new_v), adamw(*packed, name="adamw_small")):
        dst.update(zip(tiny, _unpack(res.reshape(-1), tiny_shapes)))

    def shaped(d):
        return [d[n].reshape(env[n].shape) for n in WEIGHTS]

    loss_all = lax.psum(loss[0, 0], ("x", "y", "c"))
    return (loss_all, gx[None], *shaped(grads), *shaped(delta), *shaped(new_m), *shaped(new_v))
```

```python
import functools
import math
from typing import Callable, NamedTuple

import jax
import jax.numpy as jnp
from jax import lax
from jax.experimental import pallas as pl
from jax.experimental.pallas import tpu as pltpu

f32 = jnp.float32
bf16 = jnp.bfloat16
HI = lax.Precision.HIGHEST
MESH = pl.DeviceIdType.MESH

D_MODEL = 2048
N_DEV = 8
SWA_HEADS, SWA_KV_HEADS, SWA_HEAD_DIM, BLOCK = 16, 2, 64, 128
REL_BUCKETS, REL_MAX_DIST = 32, 128
GDN_HEADS, GDN_HEAD_DIM, GDN_CONV, GDN_CHUNK = 8, 128, 4, 64
MEM_HEADS, MEM_HEAD_DIM = 4, 128
D_FF = 5504
FF_SHARD = 2 * D_FF // N_DEV
FF_PAD = 1408
NORM_EPS = 1e-5
ALPHA = 2.0 ** 0.25
NEG_INF = -1e30
SWA_Q, SWA_KV, GDN_W, MEM_W = 1024, 128, 1024, 512
IN_DIM = 9488
HALO = 8

ADAM_LR, ADAM_B1, ADAM_B2, ADAM_EPS, ADAM_WD, ADAM_STEP = 0.001, 0.9, 0.999, 1e-08, 0.01, 10

PACK_COLS = 1024
SMALL_ROWS = 32
AR_ROWS = 72
CONV_ROWS = 48

SHARDED = (
    ("w_in", (2048, 1186), 1), ("w_br_swa", (1024, 256), 1), ("w_br_gdn", (1024, 256), 1),
    ("w_mix_o", (256, 2048), 0), ("w_mem_q", (256, 512), 0), ("w_mem_kv", (256, 1024), 0),
    ("w_mem_o", (512, 256), 1), ("w_up", (2048, 1376), 1), ("w_down", (688, 2048), 0),
    ("ffn_conv_w", (3, 1376), 1), ("gdn_conv_w", (4, 384), 1),
)
SMALL = (
    ("rel_bias", (32, 16)), ("swa_sinks", (1, 16)), ("gdn_a_log", (1, 8)), ("gdn_dt_bias", (1, 8)),
    ("gdn_norm_w", (1, 128)), ("ln1_g", (1, 2048)), ("ln1_b", (1, 2048)), ("ln2_g", (1, 2048)),
    ("ln2_b", (1, 2048)), ("ln3_g", (1, 2048)), ("ln3_b", (1, 2048)), ("ffn_conv_b", (1, 11008)),
)
WEIGHTS = ("w_in", "rel_bias", "swa_sinks", "gdn_conv_w", "gdn_a_log", "gdn_dt_bias", "gdn_norm_w", "w_br_swa",
           "w_br_gdn", "w_mix_o", "ln1_g", "ln1_b", "w_mem_q", "w_mem_kv", "w_mem_o", "ln2_g", "ln2_b", "w_up",
           "ffn_conv_w", "ffn_conv_b", "w_down", "ln3_g", "ln3_b")


def _tile(n, target, align):
    if n <= target:
        return n
    t = (target // align) * align
    while t >= align:
        if n % t == 0:
            return t
        t -= align
    return n


VMEM_LIMIT_BYTES = 56 * 1024 * 1024


def _params(*sem):
    return pltpu.CompilerParams(dimension_semantics=sem, vmem_limit_bytes=VMEM_LIMIT_BYTES)


def _sigmoid(v):
    return jax.nn.sigmoid(v)


def _d16(a, b, dims):
    return lax.dot_general(a.astype(bf16), b.astype(bf16), (dims, ((), ())), preferred_element_type=f32)


NN = ((1,), (0,))
NT = ((1,), (1,))
TN = ((0,), (0,))


def mm(a, b, mode, *, name, add=None, add_scale=1.0, out_dtype=f32, hi=False, tm=1024, tn=1024, tk=2048,
       b_blocked=False, out_blocked=False, k_shards=1, side=None):
    if b_blocked:
        nb, rows, width = b.shape
        if mode == "nn":
            (m, k), n, tn = a.shape, nb * width, width
        else:
            (m, k), n, tk = a.shape, rows, k_shards * width
    elif mode == "nn":
        (m, k), (_, n) = a.shape, b.shape
    elif mode == "nt":
        (m, k), (n, _) = a.shape, b.shape
    else:
        (k, m), (_, n) = a.shape, b.shape
    if out_blocked:
        tn = n // N_DEV
    tm, tn, tk = _tile(m, tm, 8 if mode != "tn" else 128), _tile(n, tn, 128), _tile(k, tk, 128 if mode != "tn" else 8)
    nk = k // tk
    dims = {"nn": NN, "nt": NT, "tn": TN}[mode]
    a_spec = pl.BlockSpec((tk, tm), lambda i, j, kk: (kk, i)) if mode == "tn" else pl.BlockSpec((tm, tk), lambda i, j, kk: (i, kk))
    if b_blocked:
        b_spec = (pl.BlockSpec((None, tk, tn), lambda i, j, kk: (j, kk, 0)) if mode == "nn"
                  else pl.BlockSpec((k_shards, tn, tk // k_shards), lambda i, j, kk: (kk, j, 0)))
    else:
        b_spec = pl.BlockSpec((tn, tk), lambda i, j, kk: (j, kk)) if mode == "nt" else pl.BlockSpec((tk, tn), lambda i, j, kk: (kk, j))
    if out_blocked:
        o_spec, o_shape = pl.BlockSpec((None, tm, tn), lambda i, j, kk: (j, i, 0)), (N_DEV, m, tn)
    else:
        o_spec, o_shape = pl.BlockSpec((tm, tn), lambda i, j, kk: (i, j)), (m, n)
    has_add = add is not None

    def product(a_ref, b_ref):
        if hi:
            return lax.dot_general(a_ref[...], b_ref[...], (dims, ((), ())), precision=HI, preferred_element_type=f32)
        if b_blocked and mode == "nt":
            width = tk // k_shards
            parts = [_d16(a_ref[:, s * width:(s + 1) * width], b_ref[s], dims) for s in range(k_shards)]
            return functools.reduce(lambda p, q: p + q, parts)
        return _d16(a_ref[...], b_ref[...], dims)

    def finish(r, add_ref, o_ref):
        if has_add:
            r = r + add_scale * add_ref[...]
        o_ref[...] = r.astype(out_dtype)

    def body_one_step(a_ref, b_ref, *rest):
        finish(product(a_ref, b_ref), rest[0] if has_add else None, rest[-1])

    def body_k_steps(a_ref, b_ref, *rest):
        o_ref, acc_ref = rest[-2:]
        kk = pl.program_id(2)

        @pl.when(kk == 0)
        def _():
            acc_ref[...] = jnp.zeros_like(acc_ref)

        acc_ref[...] += product(a_ref, b_ref)

        @pl.when(kk == nk - 1)
        def _():
            finish(acc_ref[...], rest[0] if has_add else None, o_ref)

    return _call(body_one_step if nk == 1 else body_k_steps, (a, b, add) if has_add else (a, b), name=name,
                 grid=(m // tm, n // tn, nk), in_specs=[a_spec, b_spec] + ([o_spec] if has_add else []), out_specs=o_spec,
                 out_shape=jax.ShapeDtypeStruct(o_shape, out_dtype),
                 scratch_shapes=[] if nk == 1 else [pltpu.VMEM((tm, tn), f32)],
                 semantics=("parallel", "parallel", "arbitrary"), side=side)


def cast_bf16(a, *, name):
    m, n = a.shape
    tm = _tile(m, 512, 16)

    def body(a_ref, o_ref):
        o_ref[...] = a_ref[...].astype(bf16)

    return pl.pallas_call(body, name=name, grid=(m // tm,), in_specs=[pl.BlockSpec((tm, n), lambda i: (i, 0))],
                          out_specs=pl.BlockSpec((tm, n), lambda i: (i, 0)), out_shape=jax.ShapeDtypeStruct((m, n), bf16),
                          compiler_params=_params("parallel"))(a)


def _ln_stats(z):
    mu = jnp.mean(z, axis=-1, keepdims=True)
    zc = z - mu
    var = jnp.mean(zc * zc, axis=-1, keepdims=True)
    rstd = lax.rsqrt(var + NORM_EPS)
    return zc * rstd, rstd


def ln_fwd(z, g, b, *, name):
    s, d = z.shape
    tm = _tile(s, 256, 16)

    def body(z_ref, g_ref, b_ref, y_ref, yb_ref):
        xhat, _ = _ln_stats(z_ref[...])
        y = xhat * g_ref[...] + b_ref[...]
        y_ref[...] = y
        yb_ref[...] = y.astype(bf16)

    row = pl.BlockSpec((tm, d), lambda i: (i, 0))
    vec = pl.BlockSpec((1, d), lambda i: (0, 0))
    return pl.pallas_call(body, name=name, grid=(s // tm,), in_specs=[row, vec, vec], out_specs=[row, row],
                          out_shape=[jax.ShapeDtypeStruct((s, d), f32), jax.ShapeDtypeStruct((s, d), bf16)],
                          compiler_params=_params("parallel"))(z, g, b)


def _ln_bwd_tile(dy, z, g):
    xhat, rstd = _ln_stats(z)
    dxh = dy * g
    m1 = jnp.mean(dxh, axis=-1, keepdims=True)
    m2 = jnp.mean(dxh * xhat, axis=-1, keepdims=True)
    dz = rstd * (dxh - m1 - xhat * m2)
    return dz, jnp.sum(dy * xhat, axis=0, keepdims=True), jnp.sum(dy, axis=0, keepdims=True)


def ln_bwd(dy, z, g, *, name):
    s, d = z.shape
    tm = _tile(s, 256, 16)

    def body(dy_ref, z_ref, g_ref, dz_ref, dzb_ref, dg_ref, db_ref):
        @pl.when(pl.program_id(0) == 0)
        def _():
            dg_ref[...] = jnp.zeros_like(dg_ref)
            db_ref[...] = jnp.zeros_like(db_ref)

        dz, dg, db = _ln_bwd_tile(dy_ref[...], z_ref[...], g_ref[...])
        dz_ref[...] = dz
        dzb_ref[...] = dz.astype(bf16)
        dg_ref[...] += dg
        db_ref[...] += db

    row = pl.BlockSpec((tm, d), lambda i: (i, 0))
    vec = pl.BlockSpec((1, d), lambda i: (0, 0))
    return pl.pallas_call(body, name=name, grid=(s // tm,), in_specs=[row, row, vec], out_specs=[row, row, vec, vec],
                          out_shape=[jax.ShapeDtypeStruct((s, d), f32), jax.ShapeDtypeStruct((s, d), bf16),
                                     jax.ShapeDtypeStruct((1, d), f32), jax.ShapeDtypeStruct((1, d), f32)],
                          compiler_params=_params("arbitrary"))(dy, z, g)


def ln_loss(z, target, g, b, *, name):
    s, d = z.shape
    tm = _tile(s, 256, 16)
    nt = s // tm

    def body(z_ref, t_ref, g_ref, b_ref, dz_ref, dzb_ref, dg_ref, db_ref, loss_ref, lacc_ref):
        i = pl.program_id(0)

        @pl.when(i == 0)
        def _():
            dg_ref[...] = jnp.zeros_like(dg_ref)
            db_ref[...] = jnp.zeros_like(db_ref)
            lacc_ref[...] = jnp.zeros_like(lacc_ref)

        zv, gv = z_ref[...], g_ref[...]
        xhat, _ = _ln_stats(zv)
        err = xhat * gv + b_ref[...] - t_ref[...]
        lacc_ref[...] += jnp.sum(err * err, axis=0, keepdims=True)
        dz, dg, db = _ln_bwd_tile(err * (1.0 / d), zv, gv)
        dz_ref[...] = dz
        dzb_ref[...] = dz.astype(bf16)
        dg_ref[...] += dg
        db_ref[...] += db

        @pl.when(i == nt - 1)
        def _():
            loss_ref[...] = (0.5 / d) * jnp.sum(lacc_ref[...], axis=1, keepdims=True)

    row = pl.BlockSpec((tm, d), lambda i: (i, 0))
    vec = pl.BlockSpec((1, d), lambda i: (0, 0))
    return pl.pallas_call(body, name=name, grid=(nt,), in_specs=[row, row, vec, vec],
                          out_specs=[row, row, vec, vec, pl.BlockSpec((1, 1), lambda i: (0, 0))],
                          out_shape=[jax.ShapeDtypeStruct((s, d), f32), jax.ShapeDtypeStruct((s, d), bf16),
                                     jax.ShapeDtypeStruct((1, d), f32), jax.ShapeDtypeStruct((1, d), f32),
                                     jax.ShapeDtypeStruct((1, 1), f32)],
                          scratch_shapes=[pltpu.VMEM((1, d), f32)],
                          compiler_params=_params("arbitrary"))(z, target, g, b)


def merge_fwd(gates, ys, yg, *, name):
    s, d = ys.shape
    tm = _tile(s, 256, 16)

    def body(gt_ref, ys_ref, yg_ref, o_ref):
        o_ref[...] = (_sigmoid(gt_ref[:, :d]) * ys_ref[...] + _sigmoid(gt_ref[:, d:]) * yg_ref[...]).astype(bf16)

    row = pl.BlockSpec((tm, d), lambda i: (i, 0))
    return pl.pallas_call(body, name=name, grid=(s // tm,), in_specs=[pl.BlockSpec((tm, 2 * d), lambda i: (i, 0)), row, row],
                          out_specs=row, out_shape=jax.ShapeDtypeStruct((s, d), bf16),
                          compiler_params=_params("parallel"))(gates, ys, yg)


def merge_bwd(gates, ys, yg, dmix, *, name):
    s, d = ys.shape
    tm = _tile(s, 256, 16)

    def body(gt_ref, ys_ref, yg_ref, dm_ref, dys_ref, dyg_ref, dgt_ref):
        dm = dm_ref[...]
        sa, sb = _sigmoid(gt_ref[:, :d]), _sigmoid(gt_ref[:, d:])
        dys_ref[...] = (dm * sa).astype(bf16)
        dyg_ref[...] = (dm * sb).astype(bf16)
        dgt_ref[:, :d] = (dm * ys_ref[...] * sa * (1.0 - sa)).astype(bf16)
        dgt_ref[:, d:] = (dm * yg_ref[...] * sb * (1.0 - sb)).astype(bf16)

    row = pl.BlockSpec((tm, d), lambda i: (i, 0))
    wide = pl.BlockSpec((tm, 2 * d), lambda i: (i, 0))
    return pl.pallas_call(body, name=name, grid=(s // tm,), in_specs=[wide, row, row, row], out_specs=[row, row, wide],
                          out_shape=[jax.ShapeDtypeStruct((s, d), bf16), jax.ShapeDtypeStruct((s, d), bf16),
                                     jax.ShapeDtypeStruct((s, 2 * d), bf16)],
                          compiler_params=_params("parallel"))(gates, ys, yg, dmix)


def _shift_down(ext, j):
    return ext if j == 0 else pltpu.roll(ext, j, 0)


def _shift_up(ext, j):
    return ext if j == 0 else pltpu.roll(ext, ext.shape[0] - j, 0)


def _conv_taps(ext, width):
    return [_shift_down(ext, width - 1 - j)[HALO:] for j in range(width)]


def _causal_conv(taps, w_ref):
    acc = None
    for j, tap in enumerate(taps):
        term = w_ref[j:j + 1, :] * tap
        acc = term if acc is None else acc + term
    return acc


def _conv_grads(dy_ext, x, w_ref, width, rows):
    ahead = [_shift_up(dy_ext, width - 1 - j)[:rows] for j in range(width)]
    dx = None
    for j in range(width):
        term = w_ref[j:j + 1, :] * ahead[j]
        dx = term if dx is None else dx + term
    return dx, [jnp.sum(x * ahead[j], axis=0, keepdims=True) for j in range(width)]


def _rows_to_block(rows, n_rows, cols):
    r = lax.broadcasted_iota(jnp.int32, (n_rows, cols), 0)
    out = jnp.zeros((n_rows, cols), f32)
    for j, v in enumerate(rows):
        out = out + jnp.where(r == j, v, 0.0)
    return out


def _silu_and_grad(v):
    sg = _sigmoid(v)
    return v * sg, sg * (1.0 + v * (1.0 - sg))


HALO_BF16 = 16


def ffn_act_fwd(hpre, cwb, *, name):
    s = hpre.shape[0]
    tm = _tile(s, 256, 16)
    hb = tm // HALO_BF16

    def body(hg_ref, hgp_ref, hu_ref, hup_ref, cg_ref, cu_ref, o_ref, g_ref, u_ref):
        first = pl.program_id(1) == 0

        def conv(h_ref, hp_ref, c_ref):
            prev = jnp.where(first, 0.0, hp_ref[...].astype(f32)[HALO_BF16 - HALO:])
            ext = jnp.concatenate([prev, h_ref[...].astype(f32)], axis=0)
            return _causal_conv(_conv_taps(ext, 3), c_ref.at[0]) + c_ref[0, 3:4, :]

        g = conv(hg_ref, hgp_ref, cg_ref)
        u = conv(hu_ref, hup_ref, cu_ref)
        g_ref[...] = g.astype(bf16)
        u_ref[...] = u.astype(bf16)
        o_ref[...] = (g * _sigmoid(g) * u).astype(bf16)

    def tile(off):
        return pl.BlockSpec((tm, FF_PAD), lambda j, i: (i, j + off))

    def halo(off):
        return pl.BlockSpec((HALO_BF16, FF_PAD), lambda j, i: (jnp.maximum(i * hb - 1, 0), j + off))

    def taps(off):
        return pl.BlockSpec((1, 8, FF_PAD), lambda j, i: (j + off, 0, 0))

    out = pl.BlockSpec((tm, FF_PAD), lambda j, i: (i, j))
    return pl.pallas_call(body, name=name, grid=(4, s // tm),
                          in_specs=[tile(0), halo(0), tile(4), halo(4), taps(0), taps(4)], out_specs=[out, out, out],
                          out_shape=[jax.ShapeDtypeStruct((s, 4 * FF_PAD), bf16)] * 3,
                          compiler_params=_params("parallel", "parallel"))(hpre, hpre, hpre, hpre, cwb, cwb)


def ffn_act_bwd(hpre, conv_g, conv_u, dact, cwb, *, name):
    s = hpre.shape[0]
    tm = _tile(s, 256, 16)
    hb = tm // HALO_BF16
    nt = s // tm
    last_hb = s // HALO_BF16 - 1

    def body(hg_ref, hu_ref, g_ref, gn_ref, u_ref, un_ref, d_ref, dn_ref, cg_ref, cu_ref, dhg_ref, dhu_ref, dcg_ref, dcu_ref):
        i = pl.program_id(1)

        @pl.when(i == 0)
        def _():
            dcg_ref[...] = jnp.zeros_like(dcg_ref)
            dcu_ref[...] = jnp.zeros_like(dcu_ref)

        def with_future(t_ref, n_ref):
            return jnp.concatenate([t_ref[...].astype(f32), n_ref[...].astype(f32)[:HALO]], axis=0)

        g, u = with_future(g_ref, gn_ref), with_future(u_ref, un_ref)
        d = jnp.concatenate([d_ref[...].astype(f32), jnp.where(i == nt - 1, 0.0, dn_ref[...].astype(f32)[:HALO])], axis=0)
        act, dact_dg = _silu_and_grad(g)
        dg = d * u * dact_dg
        du = d * act
        dhg, dwg = _conv_grads(dg, hg_ref[...].astype(f32), cg_ref.at[0], 3, tm)
        dhu, dwu = _conv_grads(du, hu_ref[...].astype(f32), cu_ref.at[0], 3, tm)
        dhg_ref[...] = dhg.astype(bf16)
        dhu_ref[...] = dhu.astype(bf16)
        dcg_ref[0] += _rows_to_block(dwg + [jnp.sum(dg[:tm], axis=0, keepdims=True)], 8, FF_PAD)
        dcu_ref[0] += _rows_to_block(dwu + [jnp.sum(du[:tm], axis=0, keepdims=True)], 8, FF_PAD)

    def tile(off):
        return pl.BlockSpec((tm, FF_PAD), lambda j, i: (i, j + off))

    nxt = pl.BlockSpec((HALO_BF16, FF_PAD), lambda j, i: (jnp.minimum((i + 1) * hb, last_hb), j))
    taps = [pl.BlockSpec((1, 8, FF_PAD), lambda j, i, off=off: (j + off, 0, 0)) for off in (0, 4)]
    dhg, dhu, dcg, dcu = pl.pallas_call(
        body, name=name, grid=(4, nt),
        in_specs=[tile(0), tile(4), tile(0), nxt, tile(0), nxt, tile(0), nxt] + taps,
        out_specs=[tile(0), tile(0), taps[0], taps[0]],
        out_shape=[jax.ShapeDtypeStruct((s, 4 * FF_PAD), bf16), jax.ShapeDtypeStruct((s, 4 * FF_PAD), bf16),
                   jax.ShapeDtypeStruct((4, 8, FF_PAD), f32), jax.ShapeDtypeStruct((4, 8, FF_PAD), f32)],
        compiler_params=_params("parallel", "arbitrary"),
    )(hpre, hpre, conv_g, conv_g, conv_u, conv_u, dact, dact, cwb, cwb)
    return jnp.concatenate([dhg, dhu], axis=1), jnp.concatenate([dcg, dcu], axis=0)


MEM_SCALE = MEM_HEAD_DIM ** -0.5


def _softmax_rows(sc):
    m = jnp.max(sc, axis=-1, keepdims=True)
    e = jnp.exp(sc - m)
    return e / jnp.sum(e, axis=-1, keepdims=True)


def memattn_fwd(qm, kv, *, name):
    s = qm.shape[0]
    mlen = kv.shape[0]
    tm = _tile(s, 512, 16)

    def body(q_ref, kv_ref, o_ref):
        for h in range(MEM_HEADS):
            lo = h * MEM_HEAD_DIM
            q = q_ref[:, lo:lo + MEM_HEAD_DIM]
            k = kv_ref[:, lo:lo + MEM_HEAD_DIM]
            v = kv_ref[:, MEM_W + lo:MEM_W + lo + MEM_HEAD_DIM]
            p = _softmax_rows(_d16(q, k, NT) * MEM_SCALE)
            o_ref[:, lo:lo + MEM_HEAD_DIM] = _d16(p, v, NN).astype(bf16)

    return pl.pallas_call(body, name=name, grid=(s // tm,),
                          in_specs=[pl.BlockSpec((tm, MEM_W), lambda i: (i, 0)), pl.BlockSpec((mlen, 2 * MEM_W), lambda i: (0, 0))],
                          out_specs=pl.BlockSpec((tm, MEM_W), lambda i: (i, 0)),
                          out_shape=jax.ShapeDtypeStruct((s, MEM_W), bf16), compiler_params=_params("parallel"))(qm, kv)


def memattn_bwd(qm, kv, dout, *, name):
    s = qm.shape[0]
    mlen = kv.shape[0]
    tm = _tile(s, 512, 16)

    def body(q_ref, kv_ref, do_ref, dq_ref, dkv_ref):
        @pl.when(pl.program_id(0) == 0)
        def _():
            dkv_ref[...] = jnp.zeros_like(dkv_ref)

        for h in range(MEM_HEADS):
            lo = h * MEM_HEAD_DIM
            q = q_ref[:, lo:lo + MEM_HEAD_DIM]
            k = kv_ref[:, lo:lo + MEM_HEAD_DIM]
            v = kv_ref[:, MEM_W + lo:MEM_W + lo + MEM_HEAD_DIM]
            do = do_ref[:, lo:lo + MEM_HEAD_DIM]
            p = _softmax_rows(_d16(q, k, NT) * MEM_SCALE)
            dp = _d16(do, v, NT)
            ds = p * (dp - jnp.sum(p * dp, axis=-1, keepdims=True)) * MEM_SCALE
            dq_ref[:, lo:lo + MEM_HEAD_DIM] = _d16(ds, k, NN).astype(bf16)
            dkv_ref[:, lo:lo + MEM_HEAD_DIM] += _d16(ds, q, TN)
            dkv_ref[:, MEM_W + lo:MEM_W + lo + MEM_HEAD_DIM] += _d16(p, do, TN)

    row = pl.BlockSpec((tm, MEM_W), lambda i: (i, 0))
    full = pl.BlockSpec((mlen, 2 * MEM_W), lambda i: (0, 0))
    return pl.pallas_call(body, name=name, grid=(s // tm,), in_specs=[row, full, row], out_specs=[row, full],
                          out_shape=[jax.ShapeDtypeStruct((s, MEM_W), bf16), jax.ShapeDtypeStruct((mlen, 2 * MEM_W), f32)],
                          compiler_params=_params("arbitrary"))(qm, kv, dout)


SWA_SCALE = SWA_HEAD_DIM ** -0.5
SWA_GROUP = SWA_HEADS // SWA_KV_HEADS
SWA_IN_W = 1408
K_COL, V_COL, BA_COL = SWA_Q // 128, SWA_Q // 128 + 1, SWA_Q // 128 + 2


def _swa_mask(n):
    qi = lax.broadcasted_iota(jnp.int32, (BLOCK, 2 * BLOCK), 0)
    kj = lax.broadcasted_iota(jnp.int32, (BLOCK, 2 * BLOCK), 1)
    dist = qi + BLOCK - kj
    return (dist >= 0) & (dist < BLOCK) & ((n > 0) | (kj >= BLOCK))


def _swa_group_probs(q, k, bias, sink, mask):
    heads = range(len(q))
    sc = [jnp.where(mask, _d16(q[h], k, NT) * SWA_SCALE + bias[h], NEG_INF) for h in heads]
    m = [jnp.maximum(jnp.max(sc[h], axis=-1, keepdims=True), sink[h]) for h in heads]
    e = [jnp.exp(sc[h] - m[h]) for h in heads]
    es = [jnp.exp(sink[h] - m[h]) for h in heads]
    inv = [1.0 / (jnp.sum(e[h], axis=-1, keepdims=True) + es[h]) for h in heads]
    return e, es, inv


def _swa_group_inputs(g, q_ref, bias_ref, sink_ref):
    hs = range(g * SWA_GROUP, (g + 1) * SWA_GROUP)
    return ([q_ref[:, h * SWA_HEAD_DIM:(h + 1) * SWA_HEAD_DIM] for h in hs], [bias_ref[h] for h in hs],
            [sink_ref[:, h:h + 1] for h in hs])


def _swa_specs():
    q_spec = pl.BlockSpec((BLOCK, SWA_Q), lambda n: (n, 0))

    def band(col):
        return [pl.BlockSpec((BLOCK, SWA_KV), lambda n: (jnp.maximum(n - 1, 0), col)),
                pl.BlockSpec((BLOCK, SWA_KV), lambda n: (n, col))]

    bias_spec = pl.BlockSpec((SWA_HEADS, BLOCK, 2 * BLOCK), lambda n: (0, 0, 0))
    sink_spec = pl.BlockSpec((1, SWA_HEADS), lambda n: (0, 0))
    return [q_spec] + band(K_COL) + band(V_COL) + [bias_spec, sink_spec]


def swa_fwd(swa_in, bias, sinks, *, name, side=None):
    s = swa_in.shape[0]

    def body(q_ref, kp_ref, kc_ref, vp_ref, vc_ref, bias_ref, sink_ref, o_ref):
        mask = _swa_mask(pl.program_id(0))
        kb = jnp.concatenate([kp_ref[...], kc_ref[...]], axis=0)
        vb = jnp.concatenate([vp_ref[...], vc_ref[...]], axis=0)
        for g in range(SWA_KV_HEADS):
            kl = g * SWA_HEAD_DIM
            q, bias_g, sink_g = _swa_group_inputs(g, q_ref, bias_ref, sink_ref)
            e, _, inv = _swa_group_probs(q, kb[:, kl:kl + SWA_HEAD_DIM], bias_g, sink_g, mask)
            v = vb[:, kl:kl + SWA_HEAD_DIM]
            outs = [_d16(e[h] * inv[h], v, NN) for h in range(SWA_GROUP)]
            for h in range(SWA_GROUP):
                lo = (g * SWA_GROUP + h) * SWA_HEAD_DIM
                o_ref[:, lo:lo + SWA_HEAD_DIM] = outs[h].astype(bf16)

    return _call(body, (swa_in, swa_in, swa_in, swa_in, swa_in, bias, sinks), name=name, grid=(s // BLOCK,),
                 in_specs=_swa_specs(), out_specs=pl.BlockSpec((BLOCK, SWA_Q), lambda n: (n, 0)),
                 out_shape=jax.ShapeDtypeStruct((s, SWA_Q), bf16), semantics=("parallel",), side=side)


def swa_bwd(swa_in, bias, sinks, dout, *, name, side=None):
    s = swa_in.shape[0]

    def body(q_ref, kp_ref, kc_ref, vp_ref, vc_ref, bias_ref, sink_ref, do_ref,
             dq_ref, dkc_ref, dkp_ref, dvc_ref, dvp_ref, dbias_ref, dsink_ref):
        n = pl.program_id(0)

        @pl.when(n == 0)
        def _():
            dbias_ref[...] = jnp.zeros_like(dbias_ref)
            dsink_ref[...] = jnp.zeros_like(dsink_ref)

        mask = _swa_mask(n)
        kb = jnp.concatenate([kp_ref[...], kc_ref[...]], axis=0)
        vb = jnp.concatenate([vp_ref[...], vc_ref[...]], axis=0)
        lane = lax.broadcasted_iota(jnp.int32, (1, 128), 1)
        dsink = jnp.zeros((1, 128), f32)
        for g in range(SWA_KV_HEADS):
            kl = g * SWA_HEAD_DIM
            k, v = kb[:, kl:kl + SWA_HEAD_DIM], vb[:, kl:kl + SWA_HEAD_DIM]
            hs = range(SWA_GROUP)
            q, bias_g, sink_g = _swa_group_inputs(g, q_ref, bias_ref, sink_ref)
            do = [do_ref[:, (g * SWA_GROUP + h) * SWA_HEAD_DIM:(g * SWA_GROUP + h + 1) * SWA_HEAD_DIM] for h in hs]
            e, es, inv = _swa_group_probs(q, k, bias_g, sink_g, mask)
            p = [e[h] * inv[h] for h in hs]
            dp = [_d16(do[h], v, NT) for h in hs]
            delta = [jnp.sum(p[h] * dp[h], axis=-1, keepdims=True) for h in hs]
            ds = [p[h] * (dp[h] - delta[h]) for h in hs]
            dss = [ds[h] * SWA_SCALE for h in hs]
            dq = [_d16(dss[h], k, NN) for h in hs]
            dks = [_d16(dss[h], q[h], TN) for h in hs]
            dvs = [_d16(p[h], do[h], TN) for h in hs]
            dk, dv = sum(dks[1:], dks[0]), sum(dvs[1:], dvs[0])
            for h in hs:
                hh = g * SWA_GROUP + h
                dbias_ref[hh] += ds[h]
                dq_ref[:, hh * SWA_HEAD_DIM:(hh + 1) * SWA_HEAD_DIM] = dq[h]
                dsink = dsink + jnp.where(lane == hh, -jnp.sum(es[h] * inv[h] * delta[h], axis=0, keepdims=True), 0.0)
            dkp_ref[:, kl:kl + SWA_HEAD_DIM] = dk[:BLOCK]
            dkc_ref[:, kl:kl + SWA_HEAD_DIM] = dk[BLOCK:]
            dvp_ref[:, kl:kl + SWA_HEAD_DIM] = dv[:BLOCK]
            dvc_ref[:, kl:kl + SWA_HEAD_DIM] = dv[BLOCK:]
        dsink_ref[...] += dsink

    qs = pl.BlockSpec((BLOCK, SWA_Q), lambda n: (n, 0))
    ks = pl.BlockSpec((BLOCK, SWA_KV), lambda n: (n, 0))
    return _call(
        body, (swa_in, swa_in, swa_in, swa_in, swa_in, bias, sinks, dout), name=name, grid=(s // BLOCK,),
        in_specs=_swa_specs() + [qs],
        out_specs=[qs, ks, ks, ks, ks, pl.BlockSpec((SWA_HEADS, BLOCK, 2 * BLOCK), lambda n: (0, 0, 0)),
                   pl.BlockSpec((1, 128), lambda n: (0, 0))],
        out_shape=[jax.ShapeDtypeStruct((s, SWA_Q), f32)] + [jax.ShapeDtypeStruct((s, SWA_KV), f32)] * 4
        + [jax.ShapeDtypeStruct((SWA_HEADS, BLOCK, 2 * BLOCK), f32), jax.ShapeDtypeStruct((1, 128), f32)],
        semantics=("arbitrary",), side=side)


def swa_in_grad(dq, dkc, dkp, dvc, dvp, dba, *, name):
    s = dq.shape[0]
    nb = s // BLOCK

    def body(dq_ref, dkc_ref, dkp_ref, dvc_ref, dvp_ref, dba_ref, o_ref):
        has_next = pl.program_id(0) < nb - 1
        o_ref[:, :SWA_Q] = dq_ref[...].astype(bf16)
        o_ref[:, SWA_Q:SWA_Q + SWA_KV] = (dkc_ref[...] + jnp.where(has_next, dkp_ref[...], 0.0)).astype(bf16)
        o_ref[:, SWA_Q + SWA_KV:SWA_Q + 2 * SWA_KV] = (dvc_ref[...] + jnp.where(has_next, dvp_ref[...], 0.0)).astype(bf16)
        o_ref[:, SWA_Q + 2 * SWA_KV:] = dba_ref[...].astype(bf16)

    cur = pl.BlockSpec((BLOCK, SWA_KV), lambda n: (n, 0))
    nxt = pl.BlockSpec((BLOCK, SWA_KV), lambda n: (jnp.minimum(n + 1, nb - 1), 0))
    return pl.pallas_call(body, name=name, grid=(nb,),
                          in_specs=[pl.BlockSpec((BLOCK, SWA_Q), lambda n: (n, 0)), cur, nxt, cur, nxt, cur],
                          out_specs=pl.BlockSpec((BLOCK, SWA_IN_W), lambda n: (n, 0)),
                          out_shape=jax.ShapeDtypeStruct((s, SWA_IN_W), bf16),
                          compiler_params=_params("parallel"))(dq, dkc, dkp, dvc, dvp, dba)


def _bucket_onehot():
    qi = jnp.arange(BLOCK)[:, None]
    kj = jnp.arange(2 * BLOCK)[None, :]
    dist = jnp.maximum(qi + BLOCK - kj, 0)
    max_exact = REL_BUCKETS // 2
    dd = jnp.maximum(dist, 1).astype(f32)
    large = max_exact + (jnp.log(dd / max_exact) / math.log(REL_MAX_DIST / max_exact) * (REL_BUCKETS - max_exact)).astype(jnp.int32)
    bucket = jnp.where(dist < max_exact, dist, jnp.minimum(large, REL_BUCKETS - 1)).reshape(-1)
    return (bucket[None, :] == jnp.arange(REL_BUCKETS)[:, None]).astype(f32)


def _gbeta_fn(ba, alog_row, dt_row):
    col = lax.broadcasted_iota(jnp.int32, ba.shape, 1)
    v = ba + dt_row
    softplus = jnp.maximum(v, 0.0) + jnp.log(1.0 + jnp.exp(-jnp.abs(v)))
    g = -jnp.exp(alog_row) * softplus
    return jnp.where(col < GDN_HEADS, _sigmoid(ba), jnp.where(col < 2 * GDN_HEADS, g, 0.0))


def gbeta_fwd(swa_in, alog_row, dt_row, *, name):
    s = swa_in.shape[0]
    tm = _tile(s, 512, 8)

    def body(ba_ref, a_ref, d_ref, o_ref):
        o_ref[...] = _gbeta_fn(ba_ref[...], a_ref[...], d_ref[...])

    vec = pl.BlockSpec((1, 128), lambda i: (0, 0))
    return pl.pallas_call(body, name=name, grid=(s // tm,), in_specs=[pl.BlockSpec((tm, 128), lambda i: (i, BA_COL)), vec, vec],
                          out_specs=pl.BlockSpec((tm, 128), lambda i: (i, 0)), out_shape=jax.ShapeDtypeStruct((s, 128), f32),
                          compiler_params=_params("parallel"))(swa_in, alog_row, dt_row)


def gbeta_bwd(swa_in, alog_row, dt_row, dgbeta, *, name):
    s = swa_in.shape[0]
    tm = _tile(s, 512, 8)

    def body(ba_ref, a_ref, d_ref, dgb_ref, dba_ref, da_ref, dd_ref):
        @pl.when(pl.program_id(0) == 0)
        def _():
            da_ref[...] = jnp.zeros_like(da_ref)
            dd_ref[...] = jnp.zeros_like(dd_ref)

        _, pull = jax.vjp(_gbeta_fn, ba_ref[...], a_ref[...], d_ref[...])
        dba, da, dd = pull(dgb_ref[...])
        dba_ref[...] = dba
        da_ref[...] += da
        dd_ref[...] += dd

    vec = pl.BlockSpec((1, 128), lambda i: (0, 0))
    row = pl.BlockSpec((tm, 128), lambda i: (i, 0))
    return pl.pallas_call(body, name=name, grid=(s // tm,),
                          in_specs=[pl.BlockSpec((tm, 128), lambda i: (i, BA_COL)), vec, vec, row], out_specs=[row, vec, vec],
                          out_shape=[jax.ShapeDtypeStruct((s, 128), f32), jax.ShapeDtypeStruct((1, 128), f32),
                                     jax.ShapeDtypeStruct((1, 128), f32)],
                          compiler_params=_params("arbitrary"))(swa_in, alog_row, dt_row, dgbeta)


QKV_W = 3 * GDN_W


def gdn_pre_fwd(gdn_in, convw, *, name):
    s = gdn_in.shape[0]
    tm = _tile(s, 256, 16)
    hb = tm // HALO

    def body(x_ref, xp_ref, w_ref, q_ref, k_ref, v_ref, pre_ref):
        prev = jnp.where(pl.program_id(0) == 0, 0.0, xp_ref[...])
        pre = _causal_conv(_conv_taps(jnp.concatenate([prev, x_ref[...]], axis=0), GDN_CONV), w_ref)
        pre_ref[...] = pre
        act = pre * _sigmoid(pre)
        for h in range(GDN_HEADS):
            lo = h * GDN_HEAD_DIM
            for off, o_ref in ((0, q_ref), (GDN_W, k_ref)):
                seg = act[:, off + lo:off + lo + GDN_HEAD_DIM]
                o_ref[:, lo:lo + GDN_HEAD_DIM] = seg * lax.rsqrt(jnp.sum(seg * seg, axis=-1, keepdims=True) + 1e-6)
        v_ref[...] = act[:, 2 * GDN_W:]

    out = pl.BlockSpec((tm, GDN_W), lambda i: (i, 0))
    return pl.pallas_call(body, name=name, grid=(s // tm,),
                          in_specs=[pl.BlockSpec((tm, QKV_W), lambda i: (i, 0)),
                                    pl.BlockSpec((HALO, QKV_W), lambda i: (jnp.maximum(i * hb - 1, 0), 0)),
                                    pl.BlockSpec((8, QKV_W), lambda i: (0, 0))],
                          out_specs=[out, out, out, pl.BlockSpec((tm, QKV_W), lambda i: (i, 0))],
                          out_shape=[jax.ShapeDtypeStruct((s, GDN_W), f32)] * 3 + [jax.ShapeDtypeStruct((s, QKV_W), f32)],
                          compiler_params=_params("parallel"))(gdn_in, gdn_in, convw)


def gdn_pre_bwd(gdn_in, conv_out, convw, dqn, dkn, dv, dgz, *, name):
    s = gdn_in.shape[0]
    tm = _tile(s, 128, 16)
    hb = tm // HALO
    nt = s // tm
    last_hb = s // HALO - 1

    def body(x_ref, pre_ref, pren_ref, w_ref, dq_ref, dqx_ref, dk_ref, dkx_ref, dv_ref, dvx_ref, dz_ref, o_ref, dw_ref):
        i = pl.program_id(0)
        last = i == nt - 1

        @pl.when(i == 0)
        def _():
            dw_ref[...] = jnp.zeros_like(dw_ref)

        pre = jnp.concatenate([pre_ref[...], pren_ref[...]], axis=0)
        act, dact_dpre = _silu_and_grad(pre)

        def with_future(t_ref, n_ref):
            return jnp.concatenate([t_ref[...], jnp.where(last, 0.0, n_ref[...])], axis=0)

        dqe, dke, dve = with_future(dq_ref, dqx_ref), with_future(dk_ref, dkx_ref), with_future(dv_ref, dvx_ref)
        parts = []
        for off, dn in ((0, dqe), (GDN_W, dke)):
            for h in range(GDN_HEADS):
                lo = h * GDN_HEAD_DIM
                seg = act[:, off + lo:off + lo + GDN_HEAD_DIM]
                r = lax.rsqrt(jnp.sum(seg * seg, axis=-1, keepdims=True) + 1e-6)
                nrm = seg * r
                dseg = dn[:, lo:lo + GDN_HEAD_DIM]
                parts.append(r * (dseg - nrm * jnp.sum(dseg * nrm, axis=-1, keepdims=True)))
        dpre = jnp.concatenate(parts + [dve], axis=1) * dact_dpre
        dx, dw = _conv_grads(dpre, x_ref[...], w_ref, GDN_CONV, tm)
        o_ref[:, :QKV_W] = dx.astype(bf16)
        o_ref[:, QKV_W:] = dz_ref[...].astype(bf16)
        dw_ref[...] += _rows_to_block(dw, 8, QKV_W)

    row = pl.BlockSpec((tm, GDN_W), lambda i: (i, 0))
    fut = pl.BlockSpec((HALO, GDN_W), lambda i: (jnp.minimum((i + 1) * hb, last_hb), 0))
    wide = pl.BlockSpec((tm, QKV_W), lambda i: (i, 0))
    return pl.pallas_call(
        body, name=name, grid=(nt,),
        in_specs=[wide, wide, pl.BlockSpec((HALO, QKV_W), lambda i: (jnp.minimum((i + 1) * hb, last_hb), 0)),
                  pl.BlockSpec((8, QKV_W), lambda i: (0, 0)), row, fut, row, fut, row, fut, row],
        out_specs=[pl.BlockSpec((tm, 4 * GDN_W), lambda i: (i, 0)), pl.BlockSpec((8, QKV_W), lambda i: (0, 0))],
        out_shape=[jax.ShapeDtypeStruct((s, 4 * GDN_W), bf16), jax.ShapeDtypeStruct((8, QKV_W), f32)],
        compiler_params=_params("arbitrary"),
    )(gdn_in, conv_out, conv_out, convw, dqn, dqn, dkn, dkn, dv, dv, dgz)


def _gdn_post_head(o, z, nw):
    return o * lax.rsqrt(jnp.mean(o * o, axis=-1, keepdims=True) + 1e-6) * nw * (z * _sigmoid(z))


def gdn_post_fwd(o, gdn_in, nw, *, name):
    s = o.shape[0]
    tm = _tile(s, 256, 16)

    def body(o_ref, z_ref, nw_ref, y_ref):
        for h in range(GDN_HEADS):
            sl = slice(h * GDN_HEAD_DIM, (h + 1) * GDN_HEAD_DIM)
            y_ref[:, sl] = _gdn_post_head(o_ref[:, sl], z_ref[:, sl], nw_ref[...]).astype(bf16)

    row = pl.BlockSpec((tm, GDN_W), lambda i: (i, 0))
    return pl.pallas_call(body, name=name, grid=(s // tm,),
                          in_specs=[row, pl.BlockSpec((tm, GDN_W), lambda i: (i, 3)), pl.BlockSpec((1, 128), lambda i: (0, 0))],
                          out_specs=row, out_shape=jax.ShapeDtypeStruct((s, GDN_W), bf16),
                          compiler_params=_params("parallel"))(o, gdn_in, nw)


def gdn_post_bwd(o, gdn_in, nw, dy, *, name):
    s = o.shape[0]
    tm = _tile(s, 256, 16)

    def body(o_ref, z_ref, nw_ref, dy_ref, do_ref, dz_ref, dnw_ref):
        @pl.when(pl.program_id(0) == 0)
        def _():
            dnw_ref[...] = jnp.zeros_like(dnw_ref)

        dnw = jnp.zeros((1, 128), f32)
        for h in range(GDN_HEADS):
            sl = slice(h * GDN_HEAD_DIM, (h + 1) * GDN_HEAD_DIM)
            _, pull = jax.vjp(_gdn_post_head, o_ref[:, sl], z_ref[:, sl], nw_ref[...])
            do, dz, dn = pull(dy_ref[:, sl])
            do_ref[:, sl] = do
            dz_ref[:, sl] = dz
            dnw = dnw + dn
        dnw_ref[...] += dnw

    row = pl.BlockSpec((tm, GDN_W), lambda i: (i, 0))
    vec = pl.BlockSpec((1, 128), lambda i: (0, 0))
    return pl.pallas_call(body, name=name, grid=(s // tm,),
                          in_specs=[row, pl.BlockSpec((tm, GDN_W), lambda i: (i, 3)), vec, row], out_specs=[row, row, vec],
                          out_shape=[jax.ShapeDtypeStruct((s, GDN_W), f32), jax.ShapeDtypeStruct((s, GDN_W), f32),
                                     jax.ShapeDtypeStruct((1, 128), f32)],
                          compiler_params=_params("arbitrary"))(o, gdn_in, nw, dy)


def _dot_high(a, b, dims=NN):
    return lax.dot_general(a, b, (dims, ((), ())), precision=lax.Precision.HIGH, preferred_element_type=f32)


@jax.custom_vjp
def _unit_lower_inverses(a):
    c = a[0].shape[0]
    n = range(len(a))
    eye = (lax.broadcasted_iota(jnp.int32, (c, c), 0) == lax.broadcasted_iota(jnp.int32, (c, c), 1)).astype(f32)
    inv = [eye - a[i] for i in n]
    pw = [_dot_high(a[i], a[i]) for i in n]
    width = 2
    while width < c:
        inv = [inv[i] + _dot_high(inv[i], pw[i]) for i in n]
        width *= 2
        if width < c:
            pw = [_dot_high(pw[i], pw[i]) for i in n]
    return inv


def _unit_lower_inverses_fwd(a):
    inv = _unit_lower_inverses(a)
    return inv, inv


def _unit_lower_inverses_bwd(inv, g):
    return ([-_dot_high(_dot_high(x, gx, TN), x, NT) for x, gx in zip(inv, g)],)


_unit_lower_inverses.defvjp(_unit_lower_inverses_fwd, _unit_lower_inverses_bwd)


@jax.custom_vjp
def _known_inverses(a, inv):
    return inv


_known_inverses.defvjp(lambda a, inv: (inv, inv),
                       lambda inv, g: (_unit_lower_inverses_bwd(inv, g)[0], [jnp.zeros_like(x) for x in inv]))


def _gdn_chunks(q, k, v, gb, state, kept_inverses=None):
    c = GDN_CHUNK
    heads = range(len(q))
    r = lax.broadcasted_iota(jnp.int32, (c, c), 0)
    cc = lax.broadcasted_iota(jnp.int32, (c, c), 1)
    tril, strict = r >= cc, r > cc
    eye = (r == cc).astype(f32)

    def dhi(a, b):
        return jnp.dot(a, b, precision=lax.Precision.HIGH, preferred_element_type=f32)

    beta = [gb[:, h:h + 1] for h in heads]
    cum_cols = dhi(tril.astype(f32), gb)
    cum_rows = dhi(gb.T, (r <= cc).astype(f32))
    gi = [jnp.broadcast_to(cum_cols[:, GDN_HEADS + h:GDN_HEADS + h + 1], (c, c)) for h in heads]
    gj = [jnp.broadcast_to(cum_rows[GDN_HEADS + h:GDN_HEADS + h + 1, :], (c, c)) for h in heads]
    decay = [jnp.where(tril, jnp.exp(jnp.where(tril, gi[h] - gj[h], 0.0)), 0.0) for h in heads]
    kb = [k[h] * beta[h] for h in heads]
    vb = [v[h] * beta[h] for h in heads]
    a = [jnp.where(strict, _d16(kb[h], k[h], NT) * decay[h], 0.0) for h in heads]
    tinv = _unit_lower_inverses(a) if kept_inverses is None else _known_inverses(a, kept_inverses)
    gc = [gi[h][:, 0:1] for h in heads]
    egc = [jnp.exp(gc[h]) for h in heads]
    u = [dhi(tinv[h], vb[h]) for h in heads]
    w = [dhi(tinv[h], kb[h] * egc[h]) for h in heads]
    qs = [q[h] * (GDN_HEAD_DIM ** -0.5) for h in heads]
    attn = [jnp.where(tril, _d16(qs[h], k[h], NT) * decay[h], 0.0) for h in heads]
    g_last = [gi[h][c - 1:c, 0:1] for h in heads]
    v_new = [u[h] - _d16(w[h], state[h], NN) for h in heads]
    out = [_d16(qs[h] * egc[h], state[h], NN) + _d16(attn[h], v_new[h], NN) for h in heads]
    new_state = [state[h] * jnp.exp(g_last[h]) + _d16(k[h] * jnp.exp(g_last[h] - gc[h]), v_new[h], TN) for h in heads]
    return out, new_state, tinv


def _head_cols(ref):
    return [ref[:, h * GDN_HEAD_DIM:(h + 1) * GDN_HEAD_DIM] for h in range(GDN_HEADS)]


def gdn_scan_fwd(qn, kn, v, gbeta, *, name, side=None):
    s = qn.shape[0]
    nc = s // GDN_CHUNK

    def body(q_ref, k_ref, v_ref, gb_ref, o_ref, st_ref, inv_ref, state_ref):
        @pl.when(pl.program_id(0) == 0)
        def _():
            state_ref[...] = jnp.zeros_like(state_ref)

        states = [state_ref[h] for h in range(GDN_HEADS)]
        outs, new, inverses = _gdn_chunks(_head_cols(q_ref), _head_cols(k_ref), _head_cols(v_ref), gb_ref[...], states)
        for h in range(GDN_HEADS):
            st_ref[0, h] = states[h]
            inv_ref[0, h] = inverses[h]
            o_ref[:, h * GDN_HEAD_DIM:(h + 1) * GDN_HEAD_DIM] = outs[h]
            state_ref[h] = new[h]

    row = pl.BlockSpec((GDN_CHUNK, GDN_W), lambda n: (n, 0))
    return _call(
        body, (qn, kn, v, gbeta), name=name, grid=(nc,),
        in_specs=[row, row, row, pl.BlockSpec((GDN_CHUNK, 128), lambda n: (n, 0))],
        out_specs=[row, pl.BlockSpec((1, GDN_HEADS, GDN_HEAD_DIM, GDN_HEAD_DIM), lambda n: (n, 0, 0, 0)),
                   pl.BlockSpec((1, GDN_HEADS, GDN_CHUNK, GDN_CHUNK), lambda n: (n, 0, 0, 0))],
        out_shape=[jax.ShapeDtypeStruct((s, GDN_W), f32),
                   jax.ShapeDtypeStruct((nc, GDN_HEADS, GDN_HEAD_DIM, GDN_HEAD_DIM), f32),
                   jax.ShapeDtypeStruct((nc, GDN_HEADS, GDN_CHUNK, GDN_CHUNK), f32)],
        scratch_shapes=[pltpu.VMEM((GDN_HEADS, GDN_HEAD_DIM, GDN_HEAD_DIM), f32)], semantics=("arbitrary",), side=side)


def gdn_scan_bwd(qn, kn, v, gbeta, states, inverses, dout, *, name, side=None):
    s = qn.shape[0]
    nc = s // GDN_CHUNK

    def body(q_ref, k_ref, v_ref, gb_ref, st_ref, inv_ref, do_ref, dq_ref, dk_ref, dv_ref, dgb_ref, dstate_ref):
        @pl.when(pl.program_id(0) == 0)
        def _():
            dstate_ref[...] = jnp.zeros_like(dstate_ref)

        kept = [inv_ref[0, h] for h in range(GDN_HEADS)]
        _, pull = jax.vjp(lambda *args: _gdn_chunks(*args, kept_inverses=kept)[:2],
                          _head_cols(q_ref), _head_cols(k_ref), _head_cols(v_ref), gb_ref[...],
                          [st_ref[0, h] for h in range(GDN_HEADS)])
        dq, dk, dv, dgb, dst = pull((_head_cols(do_ref), [dstate_ref[h] for h in range(GDN_HEADS)]))
        for h in range(GDN_HEADS):
            sl = slice(h * GDN_HEAD_DIM, (h + 1) * GDN_HEAD_DIM)
            dq_ref[:, sl] = dq[h]
            dk_ref[:, sl] = dk[h]
            dv_ref[:, sl] = dv[h]
            dstate_ref[h] = dst[h]
        dgb_ref[...] = dgb

    row = pl.BlockSpec((GDN_CHUNK, GDN_W), lambda n: (nc - 1 - n, 0))
    gb = pl.BlockSpec((GDN_CHUNK, 128), lambda n: (nc - 1 - n, 0))
    return _call(
        body, (qn, kn, v, gbeta, states, inverses, dout), name=name, grid=(nc,),
        in_specs=[row, row, row, gb, pl.BlockSpec((1, GDN_HEADS, GDN_HEAD_DIM, GDN_HEAD_DIM), lambda n: (nc - 1 - n, 0, 0, 0)),
                  pl.BlockSpec((1, GDN_HEADS, GDN_CHUNK, GDN_CHUNK), lambda n: (nc - 1 - n, 0, 0, 0)), row],
        out_specs=[row, row, row, gb],
        out_shape=[jax.ShapeDtypeStruct((s, GDN_W), f32)] * 3 + [jax.ShapeDtypeStruct((s, 128), f32)],
        scratch_shapes=[pltpu.VMEM((GDN_HEADS, GDN_HEAD_DIM, GDN_HEAD_DIM), f32)], semantics=("arbitrary",), side=side)


def adamw(w, g, m, v, *, name):
    r, c = w.shape
    tr = _tile(r, 256, 8)

    def body(w_ref, g_ref, m_ref, v_ref, d_ref, nm_ref, nv_ref):
        gv = g_ref[...]
        nm = ADAM_B1 * m_ref[...] + (1.0 - ADAM_B1) * gv
        nv = ADAM_B2 * v_ref[...] + (1.0 - ADAM_B2) * (gv * gv)
        m_hat = nm / (1.0 - ADAM_B1 ** ADAM_STEP)
        v_hat = nv / (1.0 - ADAM_B2 ** ADAM_STEP)
        d_ref[...] = -ADAM_LR * (m_hat / (jnp.sqrt(v_hat) + ADAM_EPS) + ADAM_WD * w_ref[...])
        nm_ref[...] = nm
        nv_ref[...] = nv

    spec = pl.BlockSpec((tr, c), lambda i: (i, 0))
    return pl.pallas_call(body, name=name, grid=(r // tr,), in_specs=[spec] * 4, out_specs=[spec] * 3,
                          out_shape=[jax.ShapeDtypeStruct((r, c), f32)] * 3, compiler_params=_params("parallel"))(w, g, m, v)


def _pos():
    return lax.axis_index("x"), lax.axis_index("y"), lax.axis_index("c")


ANY = pl.BlockSpec(memory_space=pl.ANY)


class Side(NamedTuple):
    ins: list
    outs: list
    aliases: dict
    sems: list
    start: Callable
    wait: Callable


def join_sides(*sides):
    def spans(key):
        out, off = [], 0
        for sd in sides:
            out.append(slice(off, off + len(getattr(sd, key))))
            off += len(getattr(sd, key))
        return out

    i_sp, o_sp, s_sp = spans("ins"), spans("outs"), spans("sems")
    aliases = {i_sp[n].start + i: o_sp[n].start + o for n, sd in enumerate(sides) for i, o in sd.aliases.items()}

    def each(what):
        def run(ins, outs, sems):
            for n, sd in enumerate(sides):
                getattr(sd, what)(ins[i_sp[n]], outs[o_sp[n]], sems[s_sp[n]])
        return run

    return Side([a for sd in sides for a in sd.ins], [o for sd in sides for o in sd.outs], aliases,
                [s for sd in sides for s in sd.sems], each("start"), each("wait"))


def _side_body(body, side, n_in, n_out, n_scratch, grid):
    ns_in, ns_out = len(side.ins), len(side.outs)

    def wrapped(*refs):
        cut = [n_in, ns_in, n_out, ns_out, n_scratch]
        parts, off = [], 0
        for c in cut:
            parts.append(refs[off:off + c])
            off += c
        ins, s_ins, outs, s_outs, scratch = parts
        sems = refs[off:]
        if grid:
            ids = [pl.program_id(d) for d in range(len(grid))]
            first = functools.reduce(jnp.logical_and, [i == 0 for i in ids])
            last = functools.reduce(jnp.logical_and, [i == g - 1 for i, g in zip(ids, grid)])
            pl.when(first)(lambda: side.start(s_ins, s_outs, sems))
            body(*ins, *outs, *scratch)
            pl.when(last)(lambda: side.wait(s_ins, s_outs, sems))
        else:
            side.start(s_ins, s_outs, sems)
            side.wait(s_ins, s_outs, sems)

    return wrapped


def _call(body, args, *, name, grid, in_specs, out_specs, out_shape, semantics, scratch_shapes=(), side=None):
    if side is None:
        return pl.pallas_call(body, name=name, grid=grid, in_specs=in_specs, out_specs=out_specs, out_shape=out_shape,
                              scratch_shapes=list(scratch_shapes), compiler_params=_params(*semantics))(*args)
    single = not isinstance(out_shape, (list, tuple))
    shapes, specs = ([out_shape], [out_specs]) if single else (list(out_shape), list(out_specs))
    n_in, n_out = len(in_specs), len(shapes)
    res = pl.pallas_call(
        _side_body(body, side, n_in, n_out, len(scratch_shapes), grid), name=name, grid=grid,
        in_specs=list(in_specs) + [ANY] * len(side.ins), out_specs=specs + [ANY] * len(side.outs),
        out_shape=shapes + list(side.outs), scratch_shapes=list(scratch_shapes) + list(side.sems),
        input_output_aliases={n_in + i: n_out + o for i, o in side.aliases.items()},
        compiler_params=_params(*(["arbitrary"] * len(grid))),
    )(*args, *side.ins)
    return (res[0] if single else res[:n_out]), list(res[n_out:])


def run_side(side, *, name):
    return pl.pallas_call(_side_body(None, side, 0, 0, 0, ()), name=name, in_specs=[ANY] * len(side.ins),
                          out_specs=[ANY] * len(side.outs), out_shape=list(side.outs), scratch_shapes=list(side.sems),
                          input_output_aliases=dict(side.aliases))(*side.ins)


def _remote(src, dst, send, recv, k, to):
    return pltpu.make_async_remote_copy(src_ref=src, dst_ref=dst, send_sem=send.at[k], recv_sem=recv.at[k], device_id=to,
                                        device_id_type=MESH)


def gather_first(shards):
    na = len(shards)

    def copies(x_refs, out_refs, sems):
        send, recv, local = sems
        x, y, cc = _pos()
        me = 4 * x + 2 * y + cc
        peers = [(x, y, 1 - cc), (1 - x, y, cc), (x, 1 - y, cc), (1 - x, 1 - y, cc)]
        mine = [pltpu.make_async_copy(x_refs[a], out_refs[a].at[me], local.at[a]) for a in range(na)]
        sent = [_remote(x_refs[a], out_refs[a].at[me], send, recv, 4 * a + k, p) for a in range(na) for k, p in enumerate(peers)]
        landed = [_remote(x_refs[a], out_refs[a].at[4 * p[0] + 2 * p[1] + p[2]], send, recv, 4 * a + k, p)
                  for a in range(na) for k, p in enumerate(peers)]
        return mine, sent, landed

    def start(x_refs, out_refs, sems):
        mine, sent, _ = copies(x_refs, out_refs, sems)
        for cp in mine + sent:
            cp.start()

    def wait(x_refs, out_refs, sems):
        mine, sent, landed = copies(x_refs, out_refs, sems)
        for cp in sent:
            cp.wait_send()
        for cp in landed:
            cp.wait_recv()
        for cp in mine:
            cp.wait()

    return Side(list(shards), [jax.ShapeDtypeStruct((N_DEV,) + s.shape, s.dtype) for s in shards], {},
                [pltpu.SemaphoreType.DMA((4 * na,)), pltpu.SemaphoreType.DMA((4 * na,)), pltpu.SemaphoreType.DMA((na,))],
                start, wait)


def gather_second(slots):
    na = len(slots)

    def copies(out_refs, sems):
        send, recv = sems
        x, y, cc = _pos()
        chips = [(1 - x, y), (x, 1 - y), (1 - x, 1 - y)]
        sent, landed = [], []
        for a in range(na):
            for j, (px, py) in enumerate(chips):
                row = out_refs[a].at[4 * px + 2 * py + cc]
                sent.append(_remote(row, row, send, recv, 3 * a + j, (x, y, 1 - cc)))
                landed.append(_remote(row, out_refs[a].at[4 * px + 2 * py + 1 - cc], send, recv, 3 * a + j, (x, y, 1 - cc)))
        return sent, landed

    def start(_, out_refs, sems):
        for cp in copies(out_refs, sems)[0]:
            cp.start()

    def wait(_, out_refs, sems):
        sent, landed = copies(out_refs, sems)
        for cp in sent:
            cp.wait_send()
        for cp in landed:
            cp.wait_recv()

    return Side(list(slots), [jax.ShapeDtypeStruct(s.shape, s.dtype) for s in slots], {a: a for a in range(na)},
                [pltpu.SemaphoreType.DMA((3 * na,)), pltpu.SemaphoreType.DMA((3 * na,))], start, wait)


def grad_to_sibling(chunks):
    na = len(chunks)

    def start(g_refs, out_refs, sems):
        send, recv = sems
        x, y, cc = _pos()
        for a in range(na):
            for q in range(4):
                _remote(g_refs[a].at[2 * q + 1 - cc], out_refs[a].at[q], send, recv, a, (x, y, 1 - cc)).start()

    def wait(g_refs, out_refs, sems):
        send, recv = sems
        x, y, cc = _pos()
        for a in range(na):
            _remote(out_refs[a], out_refs[a], send, recv, a, (x, y, 1 - cc)).wait()

    return Side(list(chunks), [jax.ShapeDtypeStruct((4,) + g.shape[1:], g.dtype) for g in chunks], {},
                [pltpu.SemaphoreType.DMA((na,)), pltpu.SemaphoreType.DMA((na,))], start, wait)


def grad_to_chips(parts):
    na = len(parts)

    def copies(p_refs, out_refs, sems):
        send, recv, local = sems
        x, y, cc = _pos()
        chips = [(1 - x, y), (x, 1 - y), (1 - x, 1 - y)]
        mine = [pltpu.make_async_copy(p_refs[a].at[2 * x + y], out_refs[a].at[3], local.at[a]) for a in range(na)]
        sent = [_remote(p_refs[a].at[2 * px + py], out_refs[a].at[k], send, recv, 3 * a + k, (px, py, cc))
                for a in range(na) for k, (px, py) in enumerate(chips)]
        return mine, sent

    def start(p_refs, out_refs, sems):
        mine, sent = copies(p_refs, out_refs, sems)
        for cp in mine + sent:
            cp.start()

    def wait(p_refs, out_refs, sems):
        mine, sent = copies(p_refs, out_refs, sems)
        for cp in sent:
            cp.wait()
        for cp in mine:
            cp.wait()

    return Side(list(parts), [jax.ShapeDtypeStruct(p.shape, p.dtype) for p in parts], {},
                [pltpu.SemaphoreType.DMA((3 * na,)), pltpu.SemaphoreType.DMA((3 * na,)), pltpu.SemaphoreType.DMA((na,))],
                start, wait)


def add_sibling(chunks, recv, *, name):
    _, r, c = chunks.shape
    tr = r if r <= 1024 else _tile(r, 512, 16)
    core = lax.axis_index("c").astype(jnp.int32).reshape(1)

    def body(core_ref, a_ref, b_ref, o_ref):
        o_ref[...] = (a_ref[...] + b_ref[...]).astype(bf16)

    return pl.pallas_call(
        body, name=name,
        grid_spec=pltpu.PrefetchScalarGridSpec(
            num_scalar_prefetch=1, grid=(4, r // tr),
            in_specs=[pl.BlockSpec((1, tr, c), lambda q, i, core_ref: (2 * q + core_ref[0], i, 0)),
                      pl.BlockSpec((1, tr, c), lambda q, i, core_ref: (q, i, 0))],
            out_specs=pl.BlockSpec((1, tr, c), lambda q, i, core_ref: (q, i, 0))),
        out_shape=jax.ShapeDtypeStruct((4, r, c), bf16), compiler_params=_params("parallel", "parallel"),
    )(core, chunks, recv)


def add_four(r4, *, name):
    _, r, c = r4.shape
    tr = r if r <= 1024 else _tile(r, 512, 16)

    def body(a_ref, o_ref):
        o_ref[...] = ((a_ref[3].astype(f32) + a_ref[0].astype(f32)) + a_ref[1].astype(f32)) + a_ref[2].astype(f32)

    return pl.pallas_call(body, name=name, grid=(r // tr,), in_specs=[pl.BlockSpec((4, tr, c), lambda i: (0, i, 0))],
                          out_specs=pl.BlockSpec((tr, c), lambda i: (i, 0)), out_shape=jax.ShapeDtypeStruct((r, c), f32),
                          compiler_params=_params("parallel"))(r4)


def all_reduce_small(vec, *, name):
    r, c = vec.shape

    def body(v_ref, out_ref, buf_ref, send_sems, recv_sems):
        x, y, cc = _pos()
        my_id = 4 * x + 2 * y + cc
        buf_ref[my_id] = v_ref[...]
        flips = [(fx, fy, fc) for fx in (0, 1) for fy in (0, 1) for fc in (0, 1)][1:]
        cps = []
        for k, (fx, fy, fc) in enumerate(flips):
            peer = ((1 - x) if fx else x, (1 - y) if fy else y, (1 - cc) if fc else cc)
            cps.append(pltpu.make_async_remote_copy(src_ref=v_ref, dst_ref=buf_ref.at[my_id], send_sem=send_sems.at[k],
                                                    recv_sem=recv_sems.at[k], device_id=peer, device_id_type=MESH))
        for cp in cps:
            cp.start()
        for cp in cps:
            cp.wait()
        acc = buf_ref[0]
        for d in range(1, N_DEV):
            acc = acc + buf_ref[d]
        out_ref[...] = acc

    vm = pl.BlockSpec(memory_space=pltpu.VMEM)
    return pl.pallas_call(body, name=name, in_specs=[vm], out_specs=vm, out_shape=jax.ShapeDtypeStruct((r, c), f32),
                          scratch_shapes=[pltpu.VMEM((N_DEV, r, c), f32), pltpu.SemaphoreType.DMA((7,)),
                                          pltpu.SemaphoreType.DMA((7,))])(vec)


def _pack(parts, rows, dtype):
    flat = jnp.concatenate([p.reshape(-1).astype(dtype) for p in parts])
    return jnp.pad(flat, (0, rows * PACK_COLS - flat.shape[0])).reshape(rows, PACK_COLS)


def _unpack(flat, shapes):
    out, off = [], 0
    for shp in shapes:
        n = shp[0] * shp[1]
        out.append(flat[..., off:off + n].reshape(flat.shape[:-1] + tuple(shp)))
        off += n
    return out


def _from_column_shards(g):
    _, r, c = g.shape
    return jnp.transpose(g, (1, 0, 2)).reshape(r, N_DEV * c)


def _column_shards(full):
    r, c8 = full.shape
    return jnp.transpose(full.reshape(r, N_DEV, c8 // N_DEV), (1, 0, 2))


W_IN_SHARD = IN_DIM // N_DEV
W_IN_PAD = 1280
W_IN_PARTS = (("swa", 0, 0, 1280), ("swa", 1280, 5376, 5392), ("gdn", 0, 1280, 5376), ("gates", 0, 5392, IN_DIM))
W_IN_WIDTHS = {"swa": SWA_IN_W, "gdn": 4 * GDN_W, "gates": 2 * D_MODEL}


def _w_in_segments():
    segs = []
    for part, p0, g0, g1 in W_IN_PARTS:
        for j in range(N_DEV):
            lo, hi = max(g0, W_IN_SHARD * j), min(g1, W_IN_SHARD * (j + 1))
            if lo < hi:
                segs.append((part, p0 + lo - g0, j, lo - W_IN_SHARD * j, hi - lo))
    return segs


def split_w_in(shards, *, name):
    dt = shards.dtype
    tm = 256

    def body(w_ref, swa_ref, gdn_ref, gates_ref):
        out = {"swa": swa_ref, "gdn": gdn_ref, "gates": gates_ref}
        swa_ref[:, SWA_Q + 2 * SWA_KV + 2 * GDN_HEADS:] = jnp.zeros((tm, SWA_IN_W - SWA_Q - 2 * SWA_KV - 2 * GDN_HEADS), dt)
        for part, p0, j, l0, n in _w_in_segments():
            out[part][:, p0:p0 + n] = w_ref[j, :, l0:l0 + n]

    return pl.pallas_call(body, name=name, grid=(D_MODEL // tm,),
                          in_specs=[pl.BlockSpec((N_DEV, tm, W_IN_PAD), lambda i: (0, i, 0))],
                          out_specs=[pl.BlockSpec((tm, W_IN_WIDTHS[p]), lambda i: (i, 0)) for p in ("swa", "gdn", "gates")],
                          out_shape=[jax.ShapeDtypeStruct((D_MODEL, W_IN_WIDTHS[p]), dt) for p in ("swa", "gdn", "gates")],
                          compiler_params=_params("parallel"))(shards)


def merge_w_in_grad(d_swa, d_gdn, d_gates, *, name):
    tm = 256

    def body(swa_ref, gdn_ref, gates_ref, w_ref):
        src = {"swa": swa_ref, "gdn": gdn_ref, "gates": gates_ref}
        w_ref[:, :, W_IN_SHARD:] = jnp.zeros((N_DEV, tm, W_IN_PAD - W_IN_SHARD), f32)
        for part, p0, j, l0, n in _w_in_segments():
            w_ref[j, :, l0:l0 + n] = src[part][:, p0:p0 + n]

    return pl.pallas_call(body, name=name, grid=(D_MODEL // tm,),
                          in_specs=[pl.BlockSpec((tm, W_IN_WIDTHS[p]), lambda i: (i, 0)) for p in ("swa", "gdn", "gates")],
                          out_specs=pl.BlockSpec((N_DEV, tm, W_IN_PAD), lambda i: (0, i, 0)),
                          out_shape=jax.ShapeDtypeStruct((N_DEV, D_MODEL, W_IN_PAD), f32),
                          compiler_params=_params("parallel"))(d_swa, d_gdn, d_gates)


def kernel(x, mem, w_in, rel_bias, swa_sinks, gdn_conv_w, gdn_a_log, gdn_dt_bias, gdn_norm_w, w_br_swa, w_br_gdn, w_mix_o, ln1_g, ln1_b, w_mem_q, w_mem_kv, w_mem_o, ln2_g, ln2_b, w_up, ffn_conv_w, ffn_conv_b, w_down, ln3_g, ln3_b, loss_target, m_w_in, m_rel_bias, m_swa_sinks, m_gdn_conv_w, m_gdn_a_log, m_gdn_dt_bias, m_gdn_norm_w, m_w_br_swa, m_w_br_gdn, m_w_mix_o, m_ln1_g, m_ln1_b, m_w_mem_q, m_w_mem_kv, m_w_mem_o, m_ln2_g, m_ln2_b, m_w_up, m_ffn_conv_w, m_ffn_conv_b, m_w_down, m_ln3_g, m_ln3_b, v_w_in, v_rel_bias, v_swa_sinks, v_gdn_conv_w, v_gdn_a_log, v_gdn_dt_bias, v_gdn_norm_w, v_w_br_swa, v_w_br_gdn, v_w_mix_o, v_ln1_g, v_ln1_b, v_w_mem_q, v_w_mem_kv, v_w_mem_o, v_ln2_g, v_ln2_b, v_w_up, v_ffn_conv_w, v_ffn_conv_b, v_w_down, v_ln3_g, v_ln3_b):
    env = dict(locals())
    w2 = {n: (env[n][0] if env[n].ndim == 3 else env[n]) for n in WEIGHTS}
    m2 = {n: (env["m_" + n][0] if env["m_" + n].ndim == 3 else env["m_" + n]) for n in WEIGHTS}
    v2 = {n: (env["v_" + n][0] if env["v_" + n].ndim == 3 else env["v_" + n]) for n in WEIGHTS}
    xs, mems, target = x[0], mem[0], loss_target[0]
    my_id = 4 * lax.axis_index("x") + 2 * lax.axis_index("y") + lax.axis_index("c")
    pad_ff = FF_PAD - FF_SHARD

    pad_cols = {"w_in": W_IN_PAD - W_IN_SHARD, "w_up": pad_ff}
    mid = ("w_br_swa", "w_br_gdn", "w_mem_o", "w_mix_o", "w_mem_q", "w_mem_kv")
    mine = {n: jnp.pad(w2[n], ((0, 0), (0, pad_cols.get(n, 0)))).astype(bf16) for n in ("w_in", "w_up", "w_down") + mid}
    got_in = run_side(gather_second(run_side(gather_first([mine["w_in"]]), name="gather_w_in")), name="gather_w_in_pass_on")
    w_swa, w_gdn, w_gates = split_w_in(got_in[0], name="split_w_in")
    n_ffn, n_gdn = 3 * FF_SHARD, GDN_CONV * (QKV_W // N_DEV)
    conv_mine = jnp.concatenate([w2["ffn_conv_w"].reshape(-1), w2["gdn_conv_w"].reshape(-1)])[None]
    conv_rows = lax.dynamic_update_slice(jnp.zeros((N_DEV, n_ffn + n_gdn), f32), conv_mine, (my_id, 0))
    conv_all = all_reduce_small(_pack([conv_rows], CONV_ROWS, f32), name="gather_conv_w")
    conv_all = conv_all.reshape(-1)[:N_DEV * (n_ffn + n_gdn)].reshape(N_DEV, n_ffn + n_gdn)
    cwb = jnp.concatenate([conv_all[:, :n_ffn].reshape(N_DEV, 3, FF_SHARD), w2["ffn_conv_b"].reshape(N_DEV, 1, FF_SHARD),
                           jnp.zeros((N_DEV, 4, FF_SHARD), f32)], axis=1)
    cwb = jnp.pad(cwb, ((0, 0), (0, 0), (0, pad_ff)))
    convw = jnp.transpose(conv_all[:, n_ffn:].reshape(N_DEV, GDN_CONV, QKV_W // N_DEV), (1, 0, 2)).reshape(GDN_CONV, QKV_W)
    convw = jnp.pad(convw, ((0, 4), (0, 0)))
    onehot = _bucket_onehot()
    bias = mm(w2["rel_bias"].T, onehot, "nn", hi=True, tn=4096, name="rel_bias_table").reshape(SWA_HEADS, BLOCK, 2 * BLOCK)
    alog_row = jnp.pad(w2["gdn_a_log"], ((0, 0), (GDN_HEADS, 128 - 2 * GDN_HEADS)))
    dt_row = jnp.pad(w2["gdn_dt_bias"], ((0, 0), (GDN_HEADS, 128 - 2 * GDN_HEADS)))

    xb = cast_bf16(xs, name="cast_x")
    memb = cast_bf16(mems, name="cast_mem")
    gates, mid_got = mm(xb, w_gates, "nn", name="proj_gates", side=gather_first([mine[n] for n in mid]))
    gdn_in, mid_got = mm(xb, w_gdn, "nn", name="proj_gdn", side=gather_second(mid_got))
    got = dict(zip(mid, mid_got))
    w_br_swa, w_br_gdn, w_mem_o = (_from_column_shards(got[n]) for n in ("w_br_swa", "w_br_gdn", "w_mem_o"))
    w_mix_o = got["w_mix_o"].reshape(D_MODEL, D_MODEL)
    w_mem_q = got["w_mem_q"].reshape(D_MODEL, MEM_W)
    w_mem_kv = got["w_mem_kv"].reshape(D_MODEL, 2 * MEM_W)
    swa_in = mm(xb, w_swa, "nn", tn=SWA_IN_W, name="proj_swa")
    attn, down_got = swa_fwd(swa_in, bias, w2["swa_sinks"], name="swa_fwd", side=gather_first([mine["w_down"]]))
    qn, kn, vv, gdn_conv = gdn_pre_fwd(gdn_in, convw, name="gdn_pre_fwd")
    gbeta = gbeta_fwd(swa_in, alog_row, dt_row, name="gbeta_fwd")
    (o_gdn, states, inverses), up_got = gdn_scan_fwd(qn, kn, vv, gbeta, name="gdn_scan_fwd",
                                                     side=gather_first([mine["w_up"]]))
    ygd = gdn_post_fwd(o_gdn, gdn_in, w2["gdn_norm_w"], name="gdn_post_fwd")
    y_swa, down_got = mm(attn, w_br_swa, "nn", name="br_swa", side=gather_second(down_got))
    y_gdn, up_got = mm(ygd, w_br_gdn, "nn", name="br_gdn", side=gather_second(up_got))
    w_up_blocked = up_got[0]
    w_down_p = jnp.pad(down_got[0].reshape(4, FF_SHARD, D_MODEL), ((0, 0), (0, pad_ff), (0, 0))).reshape(4 * FF_PAD, D_MODEL)
    mixed = merge_fwd(gates, y_swa, y_gdn, name="merge_fwd")
    z1 = mm(mixed, w_mix_o, "nn", add=xs, add_scale=ALPHA, name="mix_o")
    x1, x1b = ln_fwd(z1, w2["ln1_g"], w2["ln1_b"], name="ln1_fwd")
    qm = mm(x1b, w_mem_q, "nn", name="mem_q")
    kv = mm(memb, w_mem_kv, "nn", name="mem_kv")
    om = memattn_fwd(qm, kv, name="memattn_fwd")
    z2 = mm(om, w_mem_o, "nn", add=x1, add_scale=ALPHA, name="mem_o")
    x2, x2b = ln_fwd(z2, w2["ln2_g"], w2["ln2_b"], name="ln2_fwd")
    hpre = mm(x2b, w_up_blocked, "nn", b_blocked=True, out_dtype=bf16, name="ffn_up")
    act, conv_g, conv_u = ffn_act_fwd(hpre, cwb, name="ffn_act_fwd")
    z3 = mm(act, w_down_p, "nn", add=x2, add_scale=ALPHA, tk=2 * FF_PAD, name="ffn_down")
    dz3, dz3b, d_ln3g, d_ln3b, loss = ln_loss(z3, target, w2["ln3_g"], w2["ln3_b"], name="ln3_loss")

    dact = mm(dz3b, w_down_p, "nt", tn=FF_PAD, out_dtype=bf16, name="d_act")
    d_wdown_p = mm(act, dz3b, "tn", tm=FF_PAD, name="dw_down")
    d_hpre, d_cwb = ffn_act_bwd(hpre, conv_g, conv_u, dact, cwb, name="ffn_act_bwd")
    def sibling_sums(names, chunks, received):
        return [add_sibling(c, r, name="grad_add_sibling_" + n) for n, c, r in zip(names, chunks, received)]

    def chip_sums(names, received):
        return [add_four(r, name="grad_add_chips_" + n) for n, r in zip(names, received)]

    dx2 = mm(d_hpre, w_up_blocked, "nt", b_blocked=True, k_shards=2, add=dz3, add_scale=ALPHA, name="d_x2")
    d_wup = mm(x2b, d_hpre, "tn", out_blocked=True, name="dw_up")
    ffn = ("w_up", "w_down")
    ffn_chunks = [d_wup, d_wdown_p.reshape(4, FF_PAD, D_MODEL)[:, :FF_SHARD].reshape(N_DEV, FF_SHARD // 2, D_MODEL)]
    dz2, dz2b, d_ln2g, d_ln2b = ln_bwd(dx2, z2, w2["ln2_g"], name="ln2_bwd")
    d_om, down_received = mm(dz2b, w_mem_o, "nt", name="d_om", side=grad_to_sibling(ffn_chunks[1:]))
    d_wmemo = mm(om, dz2b, "tn", name="dw_mem_o")
    dqm, dkv = memattn_bwd(qm, kv, d_om, name="memattn_bwd")
    dx1 = mm(dqm, w_mem_q, "nt", add=dz2, add_scale=ALPHA, name="d_x1")
    d_wmemq = mm(x1b, dqm, "tn", name="dw_mem_q")
    d_wmemkv = mm(memb, dkv, "tn", name="dw_mem_kv")
    dz1, dz1b, d_ln1g, d_ln1b = ln_bwd(dx1, z1, w2["ln1_g"], name="ln1_bwd")
    dmix, up_received = mm(dz1b, w_mix_o, "nt", name="d_mixed", side=grad_to_sibling(ffn_chunks[:1]))
    ffn_sums = sibling_sums(ffn, ffn_chunks, up_received + down_received)
    d_wmixo = mm(mixed, dz1b, "tn", name="dw_mix_o")
    dys, dyg, d_gates = merge_bwd(gates, y_swa, y_gdn, dmix, name="merge_bwd")
    d_attn = mm(dys, w_br_swa, "nt", name="d_attn")
    d_wbrswa = mm(attn, dys, "tn", name="dw_br_swa")
    d_ygd = mm(dyg, w_br_gdn, "nt", name="d_ygd")
    d_wbrgdn = mm(ygd, dyg, "tn", name="dw_br_gdn")
    mid_chunks = [_column_shards(d_wbrswa), _column_shards(d_wbrgdn), _column_shards(d_wmemo),
                  d_wmixo.reshape(N_DEV, D_MODEL // N_DEV, D_MODEL), d_wmemq.reshape(N_DEV, D_MODEL // N_DEV, MEM_W),
                  d_wmemkv.reshape(N_DEV, D_MODEL // N_DEV, 2 * MEM_W)]
    d_o, d_gz, d_normw = gdn_post_bwd(o_gdn, gdn_in, w2["gdn_norm_w"], d_ygd, name="gdn_post_bwd")
    (dqn, dkn, dvv, dgbeta), received = gdn_scan_bwd(
        qn, kn, vv, gbeta, states, inverses, d_o, name="gdn_scan_bwd",
        side=join_sides(grad_to_chips(ffn_sums), grad_to_sibling(mid_chunks)))
    grads = dict(zip(ffn, chip_sums(ffn, received[:2])))
    mid_sums = sibling_sums(mid, mid_chunks, received[2:])
    d_gdn_in, d_convw = gdn_pre_bwd(gdn_in, gdn_conv, convw, dqn, dkn, dvv, d_gz, name="gdn_pre_bwd")
    d_ba, d_alog, d_dt = gbeta_bwd(swa_in, alog_row, dt_row, dgbeta, name="gbeta_bwd")
    (dq, dkc, dkp, dvc, dvp, d_bias, d_sinks), received = swa_bwd(swa_in, bias, w2["swa_sinks"], d_attn, name="swa_bwd",
                                                                  side=grad_to_chips(mid_sums))
    grads.update(zip(mid, chip_sums(mid, received)))
    d_swa_in = swa_in_grad(dq, dkc, dkp, dvc, dvp, d_ba, name="swa_in_grad")
    d_relbias = mm(d_bias.reshape(SWA_HEADS, -1), onehot, "nt", hi=True, tk=4096, name="d_rel_bias").T
    d_wgates = mm(xb, d_gates, "tn", name="dw_gates")
    d_wgdn = mm(xb, d_gdn_in, "tn", name="dw_gdn")
    d_wswa = mm(xb, d_swa_in, "tn", tn=SWA_IN_W, name="dw_swa")
    in_chunks = [merge_w_in_grad(d_wswa, d_wgdn, d_wgates, name="merge_w_in_grad")]
    gx, received = mm(d_gates, w_gates, "nt", add=dz1, add_scale=ALPHA, name="dx_gates", side=grad_to_sibling(in_chunks))
    in_sums = sibling_sums(("w_in",), in_chunks, received)
    gx, received = mm(d_gdn_in, w_gdn, "nt", add=gx, name="dx_gdn", side=grad_to_chips(in_sums))
    gx = mm(d_swa_in, w_swa, "nt", add=gx, tk=SWA_IN_W, name="dx_swa")
    grads["w_in"] = chip_sums(("w_in",), received)[0][:, :W_IN_SHARD]
    grads["w_up"] = grads["w_up"][:, :FF_SHARD]

    gsmall = {
        "rel_bias": d_relbias, "swa_sinks": d_sinks[:, :SWA_HEADS], "gdn_a_log": d_alog[:, GDN_HEADS:2 * GDN_HEADS],
        "gdn_dt_bias": d_dt[:, GDN_HEADS:2 * GDN_HEADS], "gdn_norm_w": d_normw, "ln1_g": d_ln1g, "ln1_b": d_ln1b,
        "ln2_g": d_ln2g, "ln2_b": d_ln2b, "ln3_g": d_ln3g, "ln3_b": d_ln3b,
        "ffn_conv_b": d_cwb[:, 3, :FF_SHARD].reshape(1, 2 * D_FF),
        "ffn_conv_w": jnp.transpose(d_cwb[:, :3, :FF_SHARD], (1, 0, 2)).reshape(3, 2 * D_FF),
        "gdn_conv_w": d_convw[:GDN_CONV],
    }
    small_shapes = [shp for _, shp in SMALL] + [(3, 2 * D_FF), (GDN_CONV, QKV_W)]
    small_names = [n for n, _ in SMALL] + ["ffn_conv_w", "gdn_conv_w"]
    small_sum = all_reduce_small(_pack([gsmall[n] for n in small_names], AR_ROWS, f32), name="all_reduce_small")
    grads.update(zip(small_names, _unpack(small_sum.reshape(-1), small_shapes)))
    grads["ffn_conv_w"] = lax.dynamic_slice_in_dim(grads["ffn_conv_w"], my_id * FF_SHARD, FF_SHARD, axis=1)
    grads["gdn_conv_w"] = lax.dynamic_slice_in_dim(grads["gdn_conv_w"], my_id * (QKV_W // N_DEV), QKV_W // N_DEV, axis=1)

    big = [n for n, shp, _ in SHARDED if shp[0] * shp[1] > 8192]
    tiny = [n for n in WEIGHTS if n not in big]
    delta, new_m, new_v = {}, {}, {}
    for n in big:
        delta[n], new_m[n], new_v[n] = adamw(w2[n], grads[n], m2[n], v2[n], name="adamw_" + n)
    tiny_shapes = [w2[n].shape for n in tiny]
    packed = [_pack([src[n] for n in tiny], SMALL_ROWS, f32) for src in (w2, grads, m2, v2)]
    for dst, res in zip((delta, new_m, new_v), adamw(*packed, name="adamw_small")):
        dst.update(zip(tiny, _unpack(res.reshape(-1), tiny_shapes)))

    def shaped(d):
        return [d[n].reshape(env[n].shape) for n in WEIGHTS]

    loss_all = lax.psum(loss[0, 0], ("x", "y", "c"))
    return (loss_all, gx[None], *shaped(grads), *shaped(delta), *shaped(new_m), *shaped(new_v))
```

```python
import functools
import math
from typing import Callable, NamedTuple

import jax
import jax.numpy as jnp
from jax import lax
from jax.experimental import pallas as pl
from jax.experimental.pallas import tpu as pltpu

f32 = jnp.float32
bf16 = jnp.bfloat16
HI = lax.Precision.HIGHEST
MESH = pl.DeviceIdType.MESH

D_MODEL = 2048
N_DEV = 8
SWA_HEADS, SWA_KV_HEADS, SWA_HEAD_DIM, BLOCK = 16, 2, 64, 128
REL_BUCKETS, REL_MAX_DIST = 32, 128
GDN_HEADS, GDN_HEAD_DIM, GDN_CONV, GDN_CHUNK = 8, 128, 4, 64
MEM_HEADS, MEM_HEAD_DIM = 4, 128
D_FF = 5504
FF_SHARD = 2 * D_FF // N_DEV
FF_PAD = 1408
NORM_EPS = 1e-5
ALPHA = 2.0 ** 0.25
NEG_INF = -1e30
SWA_Q, SWA_KV, GDN_W, MEM_W = 1024, 128, 1024, 512
IN_DIM = 9488
HALO = 8

ADAM_LR, ADAM_B1, ADAM_B2, ADAM_EPS, ADAM_WD, ADAM_STEP = 0.001, 0.9, 0.999, 1e-08, 0.01, 10

PACK_COLS = 1024
SMALL_ROWS = 32
AR_ROWS = 72
CONV_ROWS = 48

SHARDED = (
    ("w_in", (2048, 1186), 1), ("w_br_swa", (1024, 256), 1), ("w_br_gdn", (1024, 256), 1),
    ("w_mix_o", (256, 2048), 0), ("w_mem_q", (256, 512), 0), ("w_mem_kv", (256, 1024), 0),
    ("w_mem_o", (512, 256), 1), ("w_up", (2048, 1376), 1), ("w_down", (688, 2048), 0),
    ("ffn_conv_w", (3, 1376), 1), ("gdn_conv_w", (4, 384), 1),
)
SMALL = (
    ("rel_bias", (32, 16)), ("swa_sinks", (1, 16)), ("gdn_a_log", (1, 8)), ("gdn_dt_bias", (1, 8)),
    ("gdn_norm_w", (1, 128)), ("ln1_g", (1, 2048)), ("ln1_b", (1, 2048)), ("ln2_g", (1, 2048)),
    ("ln2_b", (1, 2048)), ("ln3_g", (1, 2048)), ("ln3_b", (1, 2048)), ("ffn_conv_b", (1, 11008)),
)
WEIGHTS = ("w_in", "rel_bias", "swa_sinks", "gdn_conv_w", "gdn_a_log", "gdn_dt_bias", "gdn_norm_w", "w_br_swa",
           "w_br_gdn", "w_mix_o", "ln1_g", "ln1_b", "w_mem_q", "w_mem_kv", "w_mem_o", "ln2_g", "ln2_b", "w_up",
           "ffn_conv_w", "ffn_conv_b", "w_down", "ln3_g", "ln3_b")


def _tile(n, target, align):
    if n <= target:
        return n
    t = (target // align) * align
    while t >= align:
        if n % t == 0:
            return t
        t -= align
    return n


VMEM_LIMIT_BYTES = 56 * 1024 * 1024


def _params(*sem):
    return pltpu.CompilerParams(dimension_semantics=sem, vmem_limit_bytes=VMEM_LIMIT_BYTES)


def _sigmoid(v):
    return jax.nn.sigmoid(v)


def _d16(a, b, dims):
    return lax.dot_general(a.astype(bf16), b.astype(bf16), (dims, ((), ())), preferred_element_type=f32)


NN = ((1,), (0,))
NT = ((1,), (1,))
TN = ((0,), (0,))


def mm(a, b, mode, *, name, add=None, add_scale=1.0, out_dtype=f32, hi=False, tm=1024, tn=1024, tk=2048,
       b_blocked=False, out_blocked=False, k_shards=1, side=None):
    if b_blocked:
        nb, rows, width = b.shape
        if mode == "nn":
            (m, k), n, tn = a.shape, nb * width, width
        else:
            (m, k), n, tk = a.shape, rows, k_shards * width
    elif mode == "nn":
        (m, k), (_, n) = a.shape, b.shape
    elif mode == "nt":
        (m, k), (n, _) = a.shape, b.shape
    else:
        (k, m), (_, n) = a.shape, b.shape
    if out_blocked:
        tn = n // N_DEV
    tm, tn, tk = _tile(m, tm, 8 if mode != "tn" else 128), _tile(n, tn, 128), _tile(k, tk, 128 if mode != "tn" else 8)
    nk = k // tk
    dims = {"nn": NN, "nt": NT, "tn": TN}[mode]
    a_spec = pl.BlockSpec((tk, tm), lambda i, j, kk: (kk, i)) if mode == "tn" else pl.BlockSpec((tm, tk), lambda i, j, kk: (i, kk))
    if b_blocked:
        b_spec = (pl.BlockSpec((None, tk, tn), lambda i, j, kk: (j, kk, 0)) if mode == "nn"
                  else pl.BlockSpec((k_shards, tn, tk // k_shards), lambda i, j, kk: (kk, j, 0)))
    else:
        b_spec = pl.BlockSpec((tn, tk), lambda i, j, kk: (j, kk)) if mode == "nt" else pl.BlockSpec((tk, tn), lambda i, j, kk: (kk, j))
    if out_blocked:
        o_spec, o_shape = pl.BlockSpec((None, tm, tn), lambda i, j, kk: (j, i, 0)), (N_DEV, m, tn)
    else:
        o_spec, o_shape = pl.BlockSpec((tm, tn), lambda i, j, kk: (i, j)), (m, n)
    has_add = add is not None

    def product(a_ref, b_ref):
        if hi:
            return lax.dot_general(a_ref[...], b_ref[...], (dims, ((), ())), precision=HI, preferred_element_type=f32)
        if b_blocked and mode == "nt":
            width = tk // k_shards
            parts = [_d16(a_ref[:, s * width:(s + 1) * width], b_ref[s], dims) for s in range(k_shards)]
            return functools.reduce(lambda p, q: p + q, parts)
        return _d16(a_ref[...], b_ref[...], dims)

    def finish(r, add_ref, o_ref):
        if has_add:
            r = r + add_scale * add_ref[...]
        o_ref[...] = r.astype(out_dtype)

    def body_one_step(a_ref, b_ref, *rest):
        finish(product(a_ref, b_ref), rest[0] if has_add else None, rest[-1])

    def body_k_steps(a_ref, b_ref, *rest):
        o_ref, acc_ref = rest[-2:]
        kk = pl.program_id(2)

        @pl.when(kk == 0)
        def _():
            acc_ref[...] = jnp.zeros_like(acc_ref)

        acc_ref[...] += product(a_ref, b_ref)

        @pl.when(kk == nk - 1)
        def _():
            finish(acc_ref[...], rest[0] if has_add else None, o_ref)

    return _call(body_one_step if nk == 1 else body_k_steps, (a, b, add) if has_add else (a, b), name=name,
                 grid=(m // tm, n // tn, nk), in_specs=[a_spec, b_spec] + ([o_spec] if has_add else []), out_specs=o_spec,
                 out_shape=jax.ShapeDtypeStruct(o_shape, out_dtype),
                 scratch_shapes=[] if nk == 1 else [pltpu.VMEM((tm, tn), f32)],
                 semantics=("parallel", "parallel", "arbitrary"), side=side)


def cast_bf16(a, *, name):
    m, n = a.shape
    tm = _tile(m, 512, 16)

    def body(a_ref, o_ref):
        o_ref[...] = a_ref[...].astype(bf16)

    return pl.pallas_call(body, name=name, grid=(m // tm,), in_specs=[pl.BlockSpec((tm, n), lambda i: (i, 0))],
                          out_specs=pl.BlockSpec((tm, n), lambda i: (i, 0)), out_shape=jax.ShapeDtypeStruct((m, n), bf16),
                          compiler_params=_params("parallel"))(a)


def _ln_stats(z):
    mu = jnp.mean(z, axis=-1, keepdims=True)
    zc = z - mu
    var = jnp.mean(zc * zc, axis=-1, keepdims=True)
    rstd = lax.rsqrt(var + NORM_EPS)
    return zc * rstd, rstd


def ln_fwd(z, g, b, *, name):
    s, d = z.shape
    tm = _tile(s, 256, 16)

    def body(z_ref, g_ref, b_ref, y_ref, yb_ref):
        xhat, _ = _ln_stats(z_ref[...])
        y = xhat * g_ref[...] + b_ref[...]
        y_ref[...] = y
        yb_ref[...] = y.astype(bf16)

    row = pl.BlockSpec((tm, d), lambda i: (i, 0))
    vec = pl.BlockSpec((1, d), lambda i: (0, 0))
    return pl.pallas_call(body, name=name, grid=(s // tm,), in_specs=[row, vec, vec], out_specs=[row, row],
                          out_shape=[jax.ShapeDtypeStruct((s, d), f32), jax.ShapeDtypeStruct((s, d), bf16)],
                          compiler_params=_params("parallel"))(z, g, b)


def _ln_bwd_tile(dy, z, g):
    xhat, rstd = _ln_stats(z)
    dxh = dy * g
    m1 = jnp.mean(dxh, axis=-1, keepdims=True)
    m2 = jnp.mean(dxh * xhat, axis=-1, keepdims=True)
    dz = rstd * (dxh - m1 - xhat * m2)
    return dz, jnp.sum(dy * xhat, axis=0, keepdims=True), jnp.sum(dy, axis=0, keepdims=True)


def ln_bwd(dy, z, g, *, name):
    s, d = z.shape
    tm = _tile(s, 256, 16)

    def body(dy_ref, z_ref, g_ref, dz_ref, dzb_ref, dg_ref, db_ref):
        @pl.when(pl.program_id(0) == 0)
        def _():
            dg_ref[...] = jnp.zeros_like(dg_ref)
            db_ref[...] = jnp.zeros_like(db_ref)

        dz, dg, db = _ln_bwd_tile(dy_ref[...], z_ref[...], g_ref[...])
        dz_ref[...] = dz
        dzb_ref[...] = dz.astype(bf16)
        dg_ref[...] += dg
        db_ref[...] += db

    row = pl.BlockSpec((tm, d), lambda i: (i, 0))
    vec = pl.BlockSpec((1, d), lambda i: (0, 0))
    return pl.pallas_call(body, name=name, grid=(s // tm,), in_specs=[row, row, vec], out_specs=[row, row, vec, vec],
                          out_shape=[jax.ShapeDtypeStruct((s, d), f32), jax.ShapeDtypeStruct((s, d), bf16),
                                     jax.ShapeDtypeStruct((1, d), f32), jax.ShapeDtypeStruct((1, d), f32)],
                          compiler_params=_params("arbitrary"))(dy, z, g)


def ln_loss(z, target, g, b, *, name):
    s, d = z.shape
    tm = _tile(s, 256, 16)
    nt = s // tm

    def body(z_ref, t_ref, g_ref, b_ref, dz_ref, dzb_ref, dg_ref, db_ref, loss_ref, lacc_ref):
        i = pl.program_id(0)

        @pl.when(i == 0)
        def _():
            dg_ref[...] = jnp.zeros_like(dg_ref)
            db_ref[...] = jnp.zeros_like(db_ref)
            lacc_ref[...] = jnp.zeros_like(lacc_ref)

        zv, gv = z_ref[...], g_ref[...]
        xhat, _ = _ln_stats(zv)
        err = xhat * gv + b_ref[...] - t_ref[...]
        lacc_ref[...] += jnp.sum(err * err, axis=0, keepdims=True)
        dz, dg, db = _ln_bwd_tile(err * (1.0 / d), zv, gv)
        dz_ref[...] = dz
        dzb_ref[...] = dz.astype(bf16)
        dg_ref[...] += dg
        db_ref[...] += db

        @pl.when(i == nt - 1)
        def _():
            loss_ref[...] = (0.5 / d) * jnp.sum(lacc_ref[...], axis=1, keepdims=True)

    row = pl.BlockSpec((tm, d), lambda i: (i, 0))
    vec = pl.BlockSpec((1, d), lambda i: (0, 0))
    return pl.pallas_call(body, name=name, grid=(nt,), in_specs=[row, row, vec, vec],
                          out_specs=[row, row, vec, vec, pl.BlockSpec((1, 1), lambda i: (0, 0))],
                          out_shape=[jax.ShapeDtypeStruct((s, d), f32), jax.ShapeDtypeStruct((s, d), bf16),
                                     jax.ShapeDtypeStruct((1, d), f32), jax.ShapeDtypeStruct((1, d), f32),
                                     jax.ShapeDtypeStruct((1, 1), f32)],
                          scratch_shapes=[pltpu.VMEM((1, d), f32)],
                          compiler_params=_params("arbitrary"))(z, target, g, b)


def merge_fwd(gates, ys, yg, *, name):
    s, d = ys.shape
    tm = _tile(s, 256, 16)

    def body(gt_ref, ys_ref, yg_ref, o_ref):
        o_ref[...] = (_sigmoid(gt_ref[:, :d].astype(f32)) * ys_ref[...].astype(f32)
                      + _sigmoid(gt_ref[:, d:].astype(f32)) * yg_ref[...].astype(f32)).astype(bf16)

    row = pl.BlockSpec((tm, d), lambda i: (i, 0))
    return pl.pallas_call(body, name=name, grid=(s // tm,), in_specs=[pl.BlockSpec((tm, 2 * d), lambda i: (i, 0)), row, row],
                          out_specs=row, out_shape=jax.ShapeDtypeStruct((s, d), bf16),
                          compiler_params=_params("parallel"))(gates, ys, yg)


def merge_bwd(gates, ys, yg, dmix, *, name):
    s, d = ys.shape
    tm = _tile(s, 256, 16)

    def body(gt_ref, ys_ref, yg_ref, dm_ref, dys_ref, dyg_ref, dgt_ref):
        dm = dm_ref[...]
        sa, sb = _sigmoid(gt_ref[:, :d].astype(f32)), _sigmoid(gt_ref[:, d:].astype(f32))
        dys_ref[...] = (dm * sa).astype(bf16)
        dyg_ref[...] = (dm * sb).astype(bf16)
        dgt_ref[:, :d] = (dm * ys_ref[...].astype(f32) * sa * (1.0 - sa)).astype(bf16)
        dgt_ref[:, d:] = (dm * yg_ref[...].astype(f32) * sb * (1.0 - sb)).astype(bf16)

    row = pl.BlockSpec((tm, d), lambda i: (i, 0))
    wide = pl.BlockSpec((tm, 2 * d), lambda i: (i, 0))
    return pl.pallas_call(body, name=name, grid=(s // tm,), in_specs=[wide, row, row, row], out_specs=[row, row, wide],
                          out_shape=[jax.ShapeDtypeStruct((s, d), bf16), jax.ShapeDtypeStruct((s, d), bf16),
                                     jax.ShapeDtypeStruct((s, 2 * d), bf16)],
                          compiler_params=_params("parallel"))(gates, ys, yg, dmix)


def _shift_down(ext, j):
    return ext if j == 0 else pltpu.roll(ext, j, 0)


def _shift_up(ext, j):
    return ext if j == 0 else pltpu.roll(ext, ext.shape[0] - j, 0)


def _conv_taps(ext, width):
    return [_shift_down(ext, width - 1 - j)[HALO:] for j in range(width)]


def _causal_conv(taps, w_ref):
    acc = None
    for j, tap in enumerate(taps):
        term = w_ref[j:j + 1, :] * tap
        acc = term if acc is None else acc + term
    return acc


def _conv_grads(dy_ext, x, w_ref, width, rows):
    ahead = [_shift_up(dy_ext, width - 1 - j)[:rows] for j in range(width)]
    dx = None
    for j in range(width):
        term = w_ref[j:j + 1, :] * ahead[j]
        dx = term if dx is None else dx + term
    return dx, [jnp.sum(x * ahead[j], axis=0, keepdims=True) for j in range(width)]


def _rows_to_block(rows, n_rows, cols):
    r = lax.broadcasted_iota(jnp.int32, (n_rows, cols), 0)
    out = jnp.zeros((n_rows, cols), f32)
    for j, v in enumerate(rows):
        out = out + jnp.where(r == j, v, 0.0)
    return out


def _silu_and_grad(v):
    sg = _sigmoid(v)
    return v * sg, sg * (1.0 + v * (1.0 - sg))


HALO_BF16 = 16


def ffn_act_fwd(hpre, cwb, *, name):
    s = hpre.shape[0]
    tm = _tile(s, 256, 16)
    hb = tm // HALO_BF16

    def body(hg_ref, hgp_ref, hu_ref, hup_ref, cg_ref, cu_ref, o_ref, g_ref, u_ref):
        first = pl.program_id(1) == 0

        def conv(h_ref, hp_ref, c_ref):
            prev = jnp.where(first, 0.0, hp_ref[...].astype(f32)[HALO_BF16 - HALO:])
            ext = jnp.concatenate([prev, h_ref[...].astype(f32)], axis=0)
            return _causal_conv(_conv_taps(ext, 3), c_ref.at[0]) + c_ref[0, 3:4, :]

        g = conv(hg_ref, hgp_ref, cg_ref)
        u = conv(hu_ref, hup_ref, cu_ref)
        g_ref[...] = g.astype(bf16)
        u_ref[...] = u.astype(bf16)
        o_ref[...] = (g * _sigmoid(g) * u).astype(bf16)

    def tile(off):
        return pl.BlockSpec((tm, FF_PAD), lambda j, i: (i, j + off))

    def halo(off):
        return pl.BlockSpec((HALO_BF16, FF_PAD), lambda j, i: (jnp.maximum(i * hb - 1, 0), j + off))

    def taps(off):
        return pl.BlockSpec((1, 8, FF_PAD), lambda j, i: (j + off, 0, 0))

    out = pl.BlockSpec((tm, FF_PAD), lambda j, i: (i, j))
    return pl.pallas_call(body, name=name, grid=(4, s // tm),
                          in_specs=[tile(0), halo(0), tile(4), halo(4), taps(0), taps(4)], out_specs=[out, out, out],
                          out_shape=[jax.ShapeDtypeStruct((s, 4 * FF_PAD), bf16)] * 3,
                          compiler_params=_params("parallel", "parallel"))(hpre, hpre, hpre, hpre, cwb, cwb)


def ffn_act_bwd(hpre, conv_g, conv_u, dact, cwb, *, name):
    s = hpre.shape[0]
    tm = _tile(s, 256, 16)
    hb = tm // HALO_BF16
    nt = s // tm
    last_hb = s // HALO_BF16 - 1

    def body(hg_ref, hu_ref, g_ref, gn_ref, u_ref, un_ref, d_ref, dn_ref, cg_ref, cu_ref, dh_ref, dcg_ref, dcu_ref,
             buf_ref, sems):
        j, i = pl.program_id(0), pl.program_id(1)
        step = j * nt + i
        slot = step % 2

        def writes(from_slot):
            rows = pl.ds(pl.multiple_of(i * tm, tm), tm)
            return [pltpu.make_async_copy(buf_ref.at[from_slot, half],
                                          dh_ref.at[rows, pl.ds(pl.multiple_of((j + 4 * half) * FF_PAD, 128), FF_PAD)],
                                          sems.at[from_slot, half]) for half in (0, 1)]

        @pl.when(i == 0)
        def _():
            dcg_ref[...] = jnp.zeros_like(dcg_ref)
            dcu_ref[...] = jnp.zeros_like(dcu_ref)

        @pl.when(step >= 2)
        def _():
            for cp in writes(slot):
                cp.wait()

        def with_future(t_ref, n_ref):
            return jnp.concatenate([t_ref[...].astype(f32), n_ref[...].astype(f32)[:HALO]], axis=0)

        g, u = with_future(g_ref, gn_ref), with_future(u_ref, un_ref)
        d = jnp.concatenate([d_ref[...].astype(f32), jnp.where(i == nt - 1, 0.0, dn_ref[...].astype(f32)[:HALO])], axis=0)
        act, dact_dg = _silu_and_grad(g)
        dg = d * u * dact_dg
        du = d * act
        dhg, dwg = _conv_grads(dg, hg_ref[...].astype(f32), cg_ref.at[0], 3, tm)
        dhu, dwu = _conv_grads(du, hu_ref[...].astype(f32), cu_ref.at[0], 3, tm)
        buf_ref[slot, 0] = dhg.astype(bf16)
        buf_ref[slot, 1] = dhu.astype(bf16)
        for cp in writes(slot):
            cp.start()
        dcg_ref[0] += _rows_to_block(dwg + [jnp.sum(dg[:tm], axis=0, keepdims=True)], 8, FF_PAD)
        dcu_ref[0] += _rows_to_block(dwu + [jnp.sum(du[:tm], axis=0, keepdims=True)], 8, FF_PAD)

        @pl.when(step == 4 * nt - 1)
        def _():
            for cp in writes(slot) + writes(1 - slot):
                cp.wait()

    def tile(off):
        return pl.BlockSpec((tm, FF_PAD), lambda j, i: (i, j + off))

    nxt = pl.BlockSpec((HALO_BF16, FF_PAD), lambda j, i: (jnp.minimum((i + 1) * hb, last_hb), j))
    taps = [pl.BlockSpec((1, 8, FF_PAD), lambda j, i, off=off: (j + off, 0, 0)) for off in (0, 4)]
    dh, dcg, dcu = pl.pallas_call(
        body, name=name, grid=(4, nt),
        in_specs=[tile(0), tile(4), tile(0), nxt, tile(0), nxt, tile(0), nxt] + taps,
        out_specs=[ANY, taps[0], taps[0]],
        out_shape=[jax.ShapeDtypeStruct((s, 8 * FF_PAD), bf16),
                   jax.ShapeDtypeStruct((4, 8, FF_PAD), f32), jax.ShapeDtypeStruct((4, 8, FF_PAD), f32)],
        scratch_shapes=[pltpu.VMEM((2, 2, tm, FF_PAD), bf16), pltpu.SemaphoreType.DMA((2, 2))],
        compiler_params=_params("arbitrary", "arbitrary"),
    )(hpre, hpre, conv_g, conv_g, conv_u, conv_u, dact, dact, cwb, cwb)
    return dh, jnp.concatenate([dcg, dcu], axis=0)


MEM_SCALE = MEM_HEAD_DIM ** -0.5


def _softmax_rows(sc):
    m = jnp.max(sc, axis=-1, keepdims=True)
    e = jnp.exp(sc - m)
    return e / jnp.sum(e, axis=-1, keepdims=True)


def memattn_fwd(qm, kv, *, name):
    s = qm.shape[0]
    mlen = kv.shape[0]
    tm = _tile(s, 512, 16)

    def body(q_ref, kv_ref, o_ref):
        for h in range(MEM_HEADS):
            lo = h * MEM_HEAD_DIM
            q = q_ref[:, lo:lo + MEM_HEAD_DIM]
            k = kv_ref[:, lo:lo + MEM_HEAD_DIM]
            v = kv_ref[:, MEM_W + lo:MEM_W + lo + MEM_HEAD_DIM]
            p = _softmax_rows(_d16(q, k, NT) * MEM_SCALE)
            o_ref[:, lo:lo + MEM_HEAD_DIM] = _d16(p, v, NN).astype(bf16)

    return pl.pallas_call(body, name=name, grid=(s // tm,),
                          in_specs=[pl.BlockSpec((tm, MEM_W), lambda i: (i, 0)), pl.BlockSpec((mlen, 2 * MEM_W), lambda i: (0, 0))],
                          out_specs=pl.BlockSpec((tm, MEM_W), lambda i: (i, 0)),
                          out_shape=jax.ShapeDtypeStruct((s, MEM_W), bf16), compiler_params=_params("parallel"))(qm, kv)


def memattn_bwd(qm, kv, dout, *, name):
    s = qm.shape[0]
    mlen = kv.shape[0]
    tm = _tile(s, 512, 16)

    def body(q_ref, kv_ref, do_ref, dq_ref, dkv_ref):
        @pl.when(pl.program_id(0) == 0)
        def _():
            dkv_ref[...] = jnp.zeros_like(dkv_ref)

        for h in range(MEM_HEADS):
            lo = h * MEM_HEAD_DIM
            q = q_ref[:, lo:lo + MEM_HEAD_DIM]
            k = kv_ref[:, lo:lo + MEM_HEAD_DIM]
            v = kv_ref[:, MEM_W + lo:MEM_W + lo + MEM_HEAD_DIM]
            do = do_ref[:, lo:lo + MEM_HEAD_DIM]
            p = _softmax_rows(_d16(q, k, NT) * MEM_SCALE)
            dp = _d16(do, v, NT)
            ds = p * (dp - jnp.sum(p * dp, axis=-1, keepdims=True)) * MEM_SCALE
            dq_ref[:, lo:lo + MEM_HEAD_DIM] = _d16(ds, k, NN).astype(bf16)
            dkv_ref[:, lo:lo + MEM_HEAD_DIM] += _d16(ds, q, TN)
            dkv_ref[:, MEM_W + lo:MEM_W + lo + MEM_HEAD_DIM] += _d16(p, do, TN)

    row = pl.BlockSpec((tm, MEM_W), lambda i: (i, 0))
    full = pl.BlockSpec((mlen, 2 * MEM_W), lambda i: (0, 0))
    return pl.pallas_call(body, name=name, grid=(s // tm,), in_specs=[row, full, row], out_specs=[row, full],
                          out_shape=[jax.ShapeDtypeStruct((s, MEM_W), bf16), jax.ShapeDtypeStruct((mlen, 2 * MEM_W), f32)],
                          compiler_params=_params("arbitrary"))(qm, kv, dout)


SWA_SCALE = SWA_HEAD_DIM ** -0.5
SWA_GROUP = SWA_HEADS // SWA_KV_HEADS
SWA_IN_W = 1408
K_COL, V_COL, BA_COL = SWA_Q // 128, SWA_Q // 128 + 1, SWA_Q // 128 + 2


def _swa_mask(n):
    qi = lax.broadcasted_iota(jnp.int32, (BLOCK, 2 * BLOCK), 0)
    kj = lax.broadcasted_iota(jnp.int32, (BLOCK, 2 * BLOCK), 1)
    dist = qi + BLOCK - kj
    return (dist >= 0) & (dist < BLOCK) & ((n > 0) | (kj >= BLOCK))


def _swa_probs(q, k, bias, sink, mask):
    heads = range(len(q))
    sc = [jnp.where(mask, _d16(q[h], k[h], NT) * SWA_SCALE + bias[h], NEG_INF) for h in heads]
    m = [jnp.maximum(jnp.max(sc[h], axis=-1, keepdims=True), sink[h]) for h in heads]
    e = [jnp.exp(sc[h] - m[h]) for h in heads]
    es = [jnp.exp(sink[h] - m[h]) for h in heads]
    inv = [1.0 / (jnp.sum(e[h], axis=-1, keepdims=True) + es[h]) for h in heads]
    return e, es, inv


def _swa_heads(ref):
    return [ref[:, h * SWA_HEAD_DIM:(h + 1) * SWA_HEAD_DIM] for h in range(SWA_HEADS)]


def _swa_kv_of_heads(band):
    kv = [band[:, g * SWA_HEAD_DIM:(g + 1) * SWA_HEAD_DIM] for g in range(SWA_KV_HEADS)]
    return [kv[h // SWA_GROUP] for h in range(SWA_HEADS)]


def _swa_specs():
    q_spec = pl.BlockSpec((BLOCK, SWA_Q), lambda n: (n, 0))

    def band(col):
        return [pl.BlockSpec((BLOCK, SWA_KV), lambda n: (jnp.maximum(n - 1, 0), col)),
                pl.BlockSpec((BLOCK, SWA_KV), lambda n: (n, col))]

    bias_spec = pl.BlockSpec((SWA_HEADS, BLOCK, 2 * BLOCK), lambda n: (0, 0, 0))
    sink_spec = pl.BlockSpec((1, SWA_HEADS), lambda n: (0, 0))
    return [q_spec] + band(K_COL) + band(V_COL) + [bias_spec, sink_spec]


def swa_fwd(swa_in, bias, sinks, *, name, side=None):
    s = swa_in.shape[0]

    def body(q_ref, kp_ref, kc_ref, vp_ref, vc_ref, bias_ref, sink_ref, o_ref):
        mask = _swa_mask(pl.program_id(0))
        kb = jnp.concatenate([kp_ref[...], kc_ref[...]], axis=0)
        vb = jnp.concatenate([vp_ref[...], vc_ref[...]], axis=0)
        heads = range(SWA_HEADS)
        k, v = _swa_kv_of_heads(kb), _swa_kv_of_heads(vb)
        e, _, inv = _swa_probs(_swa_heads(q_ref), k, [bias_ref[h] for h in heads], [sink_ref[:, h:h + 1] for h in heads], mask)
        outs = [_d16(e[h] * inv[h], v[h], NN) for h in heads]
        for h in heads:
            o_ref[:, h * SWA_HEAD_DIM:(h + 1) * SWA_HEAD_DIM] = outs[h].astype(bf16)

    return _call(body, (swa_in, swa_in, swa_in, swa_in, swa_in, bias, sinks), name=name, grid=(s // BLOCK,),
                 in_specs=_swa_specs(), out_specs=pl.BlockSpec((BLOCK, SWA_Q), lambda n: (n, 0)),
                 out_shape=jax.ShapeDtypeStruct((s, SWA_Q), bf16), semantics=("parallel",), side=side)


def swa_bwd(swa_in, bias, sinks, dout, *, name, side=None):
    s = swa_in.shape[0]

    def body(q_ref, kp_ref, kc_ref, vp_ref, vc_ref, bias_ref, sink_ref, do_ref,
             dq_ref, dkc_ref, dkp_ref, dvc_ref, dvp_ref, dbias_ref, dsink_ref):
        n = pl.program_id(0)

        @pl.when(n == 0)
        def _():
            dbias_ref[...] = jnp.zeros_like(dbias_ref)
            dsink_ref[...] = jnp.zeros_like(dsink_ref)

        mask = _swa_mask(n)
        kb = jnp.concatenate([kp_ref[...], kc_ref[...]], axis=0)
        vb = jnp.concatenate([vp_ref[...], vc_ref[...]], axis=0)
        lane = lax.broadcasted_iota(jnp.int32, (1, 128), 1)
        hs = range(SWA_HEADS)
        q, do = _swa_heads(q_ref), _swa_heads(do_ref)
        k, v = _swa_kv_of_heads(kb), _swa_kv_of_heads(vb)
        e, es, inv = _swa_probs(q, k, [bias_ref[h] for h in hs], [sink_ref[:, h:h + 1] for h in hs], mask)
        p = [e[h] * inv[h] for h in hs]
        dp = [_d16(do[h], v[h], NT) for h in hs]
        delta = [jnp.sum(p[h] * dp[h], axis=-1, keepdims=True) for h in hs]
        ds = [p[h] * (dp[h] - delta[h]) for h in hs]
        dss = [ds[h] * SWA_SCALE for h in hs]
        dq = [_d16(dss[h], k[h], NN) for h in hs]
        dks = [_d16(dss[h], q[h], TN) for h in hs]
        dvs = [_d16(p[h], do[h], TN) for h in hs]
        dsink = jnp.zeros((1, 128), f32)
        for h in hs:
            dbias_ref[h] += ds[h]
            dq_ref[:, h * SWA_HEAD_DIM:(h + 1) * SWA_HEAD_DIM] = dq[h]
            dsink = dsink + jnp.where(lane == h, -jnp.sum(es[h] * inv[h] * delta[h], axis=0, keepdims=True), 0.0)
        for g in range(SWA_KV_HEADS):
            kl = g * SWA_HEAD_DIM
            group = range(g * SWA_GROUP, (g + 1) * SWA_GROUP)
            dk = functools.reduce(lambda a, b: a + b, [dks[h] for h in group])
            dv = functools.reduce(lambda a, b: a + b, [dvs[h] for h in group])
            dkp_ref[:, kl:kl + SWA_HEAD_DIM] = dk[:BLOCK]
            dkc_ref[:, kl:kl + SWA_HEAD_DIM] = dk[BLOCK:]
            dvp_ref[:, kl:kl + SWA_HEAD_DIM] = dv[:BLOCK]
            dvc_ref[:, kl:kl + SWA_HEAD_DIM] = dv[BLOCK:]
        dsink_ref[...] += dsink

    qs = pl.BlockSpec((BLOCK, SWA_Q), lambda n: (n, 0))
    ks = pl.BlockSpec((BLOCK, SWA_KV), lambda n: (n, 0))
    return _call(
        body, (swa_in, swa_in, swa_in, swa_in, swa_in, bias, sinks, dout), name=name, grid=(s // BLOCK,),
        in_specs=_swa_specs() + [qs],
        out_specs=[qs, ks, ks, ks, ks, pl.BlockSpec((SWA_HEADS, BLOCK, 2 * BLOCK), lambda n: (0, 0, 0)),
                   pl.BlockSpec((1, 128), lambda n: (0, 0))],
        out_shape=[jax.ShapeDtypeStruct((s, SWA_Q), f32)] + [jax.ShapeDtypeStruct((s, SWA_KV), f32)] * 4
        + [jax.ShapeDtypeStruct((SWA_HEADS, BLOCK, 2 * BLOCK), f32), jax.ShapeDtypeStruct((1, 128), f32)],
        semantics=("arbitrary",), side=side)


def swa_in_grad(dq, dkc, dkp, dvc, dvp, dba, *, name):
    s = dq.shape[0]
    nb = s // BLOCK

    def body(dq_ref, dkc_ref, dkp_ref, dvc_ref, dvp_ref, dba_ref, o_ref):
        has_next = pl.program_id(0) < nb - 1
        o_ref[:, :SWA_Q] = dq_ref[...].astype(bf16)
        o_ref[:, SWA_Q:SWA_Q + SWA_KV] = (dkc_ref[...] + jnp.where(has_next, dkp_ref[...], 0.0)).astype(bf16)
        o_ref[:, SWA_Q + SWA_KV:SWA_Q + 2 * SWA_KV] = (dvc_ref[...] + jnp.where(has_next, dvp_ref[...], 0.0)).astype(bf16)
        o_ref[:, SWA_Q + 2 * SWA_KV:] = dba_ref[...].astype(bf16)

    cur = pl.BlockSpec((BLOCK, SWA_KV), lambda n: (n, 0))
    nxt = pl.BlockSpec((BLOCK, SWA_KV), lambda n: (jnp.minimum(n + 1, nb - 1), 0))
    return pl.pallas_call(body, name=name, grid=(nb,),
                          in_specs=[pl.BlockSpec((BLOCK, SWA_Q), lambda n: (n, 0)), cur, nxt, cur, nxt, cur],
                          out_specs=pl.BlockSpec((BLOCK, SWA_IN_W), lambda n: (n, 0)),
                          out_shape=jax.ShapeDtypeStruct((s, SWA_IN_W), bf16),
                          compiler_params=_params("parallel"))(dq, dkc, dkp, dvc, dvp, dba)


def _bucket_onehot():
    qi = jnp.arange(BLOCK)[:, None]
    kj = jnp.arange(2 * BLOCK)[None, :]
    dist = jnp.maximum(qi + BLOCK - kj, 0)
    max_exact = REL_BUCKETS // 2
    dd = jnp.maximum(dist, 1).astype(f32)
    large = max_exact + (jnp.log(dd / max_exact) / math.log(REL_MAX_DIST / max_exact) * (REL_BUCKETS - max_exact)).astype(jnp.int32)
    bucket = jnp.where(dist < max_exact, dist, jnp.minimum(large, REL_BUCKETS - 1)).reshape(-1)
    return (bucket[None, :] == jnp.arange(REL_BUCKETS)[:, None]).astype(f32)


def _gbeta_fn(ba, alog_row, dt_row):
    col = lax.broadcasted_iota(jnp.int32, ba.shape, 1)
    v = ba + dt_row
    softplus = jnp.maximum(v, 0.0) + jnp.log(1.0 + jnp.exp(-jnp.abs(v)))
    g = -jnp.exp(alog_row) * softplus
    return jnp.where(col < GDN_HEADS, _sigmoid(ba), jnp.where(col < 2 * GDN_HEADS, g, 0.0))


def gbeta_fwd(swa_in, alog_row, dt_row, *, name):
    s = swa_in.shape[0]
    tm = _tile(s, 512, 8)

    def body(ba_ref, a_ref, d_ref, o_ref):
        o_ref[...] = _gbeta_fn(ba_ref[...], a_ref[...], d_ref[...])

    vec = pl.BlockSpec((1, 128), lambda i: (0, 0))
    return pl.pallas_call(body, name=name, grid=(s // tm,), in_specs=[pl.BlockSpec((tm, 128), lambda i: (i, BA_COL)), vec, vec],
                          out_specs=pl.BlockSpec((tm, 128), lambda i: (i, 0)), out_shape=jax.ShapeDtypeStruct((s, 128), f32),
                          compiler_params=_params("parallel"))(swa_in, alog_row, dt_row)


def gbeta_bwd(swa_in, alog_row, dt_row, dgbeta, *, name):
    s = swa_in.shape[0]
    tm = _tile(s, 512, 8)

    def body(ba_ref, a_ref, d_ref, dgb_ref, dba_ref, da_ref, dd_ref):
        @pl.when(pl.program_id(0) == 0)
        def _():
            da_ref[...] = jnp.zeros_like(da_ref)
            dd_ref[...] = jnp.zeros_like(dd_ref)

        _, pull = jax.vjp(_gbeta_fn, ba_ref[...], a_ref[...], d_ref[...])
        dba, da, dd = pull(dgb_ref[...])
        dba_ref[...] = dba
        da_ref[...] += da
        dd_ref[...] += dd

    vec = pl.BlockSpec((1, 128), lambda i: (0, 0))
    row = pl.BlockSpec((tm, 128), lambda i: (i, 0))
    return pl.pallas_call(body, name=name, grid=(s // tm,),
                          in_specs=[pl.BlockSpec((tm, 128), lambda i: (i, BA_COL)), vec, vec, row], out_specs=[row, vec, vec],
                          out_shape=[jax.ShapeDtypeStruct((s, 128), f32), jax.ShapeDtypeStruct((1, 128), f32),
                                     jax.ShapeDtypeStruct((1, 128), f32)],
                          compiler_params=_params("arbitrary"))(swa_in, alog_row, dt_row, dgbeta)


QKV_W = 3 * GDN_W


def gdn_pre_fwd(gdn_in, convw, *, name):
    s = gdn_in.shape[0]
    tm = _tile(s, 256, 16)
    hb = tm // HALO

    def body(x_ref, xp_ref, w_ref, q_ref, k_ref, v_ref, pre_ref):
        prev = jnp.where(pl.program_id(0) == 0, 0.0, xp_ref[...])
        pre = _causal_conv(_conv_taps(jnp.concatenate([prev, x_ref[...]], axis=0), GDN_CONV), w_ref)
        pre_ref[...] = pre
        act = pre * _sigmoid(pre)
        for h in range(GDN_HEADS):
            lo = h * GDN_HEAD_DIM
            for off, o_ref in ((0, q_ref), (GDN_W, k_ref)):
                seg = act[:, off + lo:off + lo + GDN_HEAD_DIM]
                o_ref[:, lo:lo + GDN_HEAD_DIM] = seg * lax.rsqrt(jnp.sum(seg * seg, axis=-1, keepdims=True) + 1e-6)
        v_ref[...] = act[:, 2 * GDN_W:]

    out = pl.BlockSpec((tm, GDN_W), lambda i: (i, 0))
    return pl.pallas_call(body, name=name, grid=(s // tm,),
                          in_specs=[pl.BlockSpec((tm, QKV_W), lambda i: (i, 0)),
                                    pl.BlockSpec((HALO, QKV_W), lambda i: (jnp.maximum(i * hb - 1, 0), 0)),
                                    pl.BlockSpec((8, QKV_W), lambda i: (0, 0))],
                          out_specs=[out, out, out, pl.BlockSpec((tm, QKV_W), lambda i: (i, 0))],
                          out_shape=[jax.ShapeDtypeStruct((s, GDN_W), f32)] * 3 + [jax.ShapeDtypeStruct((s, QKV_W), f32)],
                          compiler_params=_params("parallel"))(gdn_in, gdn_in, convw)


def gdn_pre_bwd(gdn_in, conv_out, convw, dqn, dkn, dv, dgz, *, name):
    s = gdn_in.shape[0]
    tm = _tile(s, 128, 16)
    hb = tm // HALO
    nt = s // tm
    last_hb = s // HALO - 1

    def body(x_ref, pre_ref, pren_ref, w_ref, dq_ref, dqx_ref, dk_ref, dkx_ref, dv_ref, dvx_ref, dz_ref, o_ref, dw_ref):
        i = pl.program_id(0)
        last = i == nt - 1

        @pl.when(i == 0)
        def _():
            dw_ref[...] = jnp.zeros_like(dw_ref)

        pre = jnp.concatenate([pre_ref[...], pren_ref[...]], axis=0)
        act, dact_dpre = _silu_and_grad(pre)

        def with_future(t_ref, n_ref):
            return jnp.concatenate([t_ref[...], jnp.where(last, 0.0, n_ref[...])], axis=0)

        dqe, dke, dve = with_future(dq_ref, dqx_ref), with_future(dk_ref, dkx_ref), with_future(dv_ref, dvx_ref)
        parts = []
        for off, dn in ((0, dqe), (GDN_W, dke)):
            for h in range(GDN_HEADS):
                lo = h * GDN_HEAD_DIM
                seg = act[:, off + lo:off + lo + GDN_HEAD_DIM]
                r = lax.rsqrt(jnp.sum(seg * seg, axis=-1, keepdims=True) + 1e-6)
                nrm = seg * r
                dseg = dn[:, lo:lo + GDN_HEAD_DIM]
                parts.append(r * (dseg - nrm * jnp.sum(dseg * nrm, axis=-1, keepdims=True)))
        dpre = jnp.concatenate(parts + [dve], axis=1) * dact_dpre
        dx, dw = _conv_grads(dpre, x_ref[...], w_ref, GDN_CONV, tm)
        o_ref[:, :QKV_W] = dx.astype(bf16)
        o_ref[:, QKV_W:] = dz_ref[...].astype(bf16)
        dw_ref[...] += _rows_to_block(dw, 8, QKV_W)

    row = pl.BlockSpec((tm, GDN_W), lambda i: (i, 0))
    fut = pl.BlockSpec((HALO, GDN_W), lambda i: (jnp.minimum((i + 1) * hb, last_hb), 0))
    wide = pl.BlockSpec((tm, QKV_W), lambda i: (i, 0))
    return pl.pallas_call(
        body, name=name, grid=(nt,),
        in_specs=[wide, wide, pl.BlockSpec((HALO, QKV_W), lambda i: (jnp.minimum((i + 1) * hb, last_hb), 0)),
                  pl.BlockSpec((8, QKV_W), lambda i: (0, 0)), row, fut, row, fut, row, fut, row],
        out_specs=[pl.BlockSpec((tm, 4 * GDN_W), lambda i: (i, 0)), pl.BlockSpec((8, QKV_W), lambda i: (0, 0))],
        out_shape=[jax.ShapeDtypeStruct((s, 4 * GDN_W), bf16), jax.ShapeDtypeStruct((8, QKV_W), f32)],
        compiler_params=_params("arbitrary"),
    )(gdn_in, conv_out, conv_out, convw, dqn, dqn, dkn, dkn, dv, dv, dgz)


def _gdn_post_head(o, z, nw):
    return o * lax.rsqrt(jnp.mean(o * o, axis=-1, keepdims=True) + 1e-6) * nw * (z * _sigmoid(z))


def gdn_post_fwd(o, gdn_in, nw, *, name):
    s = o.shape[0]
    tm = _tile(s, 256, 16)

    def body(o_ref, z_ref, nw_ref, y_ref):
        for h in range(GDN_HEADS):
            sl = slice(h * GDN_HEAD_DIM, (h + 1) * GDN_HEAD_DIM)
            y_ref[:, sl] = _gdn_post_head(o_ref[:, sl], z_ref[:, sl], nw_ref[...]).astype(bf16)

    row = pl.BlockSpec((tm, GDN_W), lambda i: (i, 0))
    return pl.pallas_call(body, name=name, grid=(s // tm,),
                          in_specs=[row, pl.BlockSpec((tm, GDN_W), lambda i: (i, 3)), pl.BlockSpec((1, 128), lambda i: (0, 0))],
                          out_specs=row, out_shape=jax.ShapeDtypeStruct((s, GDN_W), bf16),
                          compiler_params=_params("parallel"))(o, gdn_in, nw)


def gdn_post_bwd(o, gdn_in, nw, dy, *, name):
    s = o.shape[0]
    tm = _tile(s, 256, 16)

    def body(o_ref, z_ref, nw_ref, dy_ref, do_ref, dz_ref, dnw_ref):
        @pl.when(pl.program_id(0) == 0)
        def _():
            dnw_ref[...] = jnp.zeros_like(dnw_ref)

        dnw = jnp.zeros((1, 128), f32)
        for h in range(GDN_HEADS):
            sl = slice(h * GDN_HEAD_DIM, (h + 1) * GDN_HEAD_DIM)
            _, pull = jax.vjp(_gdn_post_head, o_ref[:, sl], z_ref[:, sl], nw_ref[...])
            do, dz, dn = pull(dy_ref[:, sl])
            do_ref[:, sl] = do
            dz_ref[:, sl] = dz
            dnw = dnw + dn
        dnw_ref[...] += dnw

    row = pl.BlockSpec((tm, GDN_W), lambda i: (i, 0))
    vec = pl.BlockSpec((1, 128), lambda i: (0, 0))
    return pl.pallas_call(body, name=name, grid=(s // tm,),
                          in_specs=[row, pl.BlockSpec((tm, GDN_W), lambda i: (i, 3)), vec, row], out_specs=[row, row, vec],
                          out_shape=[jax.ShapeDtypeStruct((s, GDN_W), f32), jax.ShapeDtypeStruct((s, GDN_W), f32),
                                     jax.ShapeDtypeStruct((1, 128), f32)],
                          compiler_params=_params("arbitrary"))(o, gdn_in, nw, dy)


def _dot_high(a, b, dims=NN):
    return lax.dot_general(a, b, (dims, ((), ())), precision=lax.Precision.HIGH, preferred_element_type=f32)


@jax.custom_vjp
def _unit_lower_inverses(a):
    c = a[0].shape[0]
    n = range(len(a))
    eye = (lax.broadcasted_iota(jnp.int32, (c, c), 0) == lax.broadcasted_iota(jnp.int32, (c, c), 1)).astype(f32)
    inv = [eye - a[i] for i in n]
    pw = [_dot_high(a[i], a[i]) for i in n]
    width = 2
    while width < c:
        inv = [inv[i] + _dot_high(inv[i], pw[i]) for i in n]
        width *= 2
        if width < c:
            pw = [_dot_high(pw[i], pw[i]) for i in n]
    return inv


def _unit_lower_inverses_fwd(a):
    inv = _unit_lower_inverses(a)
    return inv, inv


def _unit_lower_inverses_bwd(inv, g):
    return ([-_dot_high(_dot_high(x, gx, TN), x, NT) for x, gx in zip(inv, g)],)


_unit_lower_inverses.defvjp(_unit_lower_inverses_fwd, _unit_lower_inverses_bwd)


@jax.custom_vjp
def _known_inverses(a, inv):
    return inv


_known_inverses.defvjp(lambda a, inv: (inv, inv),
                       lambda inv, g: (_unit_lower_inverses_bwd(inv, g)[0], [jnp.zeros_like(x) for x in inv]))


def _gdn_chunks(q, k, v, gb, state, kept_inverses=None):
    c = GDN_CHUNK
    heads = range(len(q))
    r = lax.broadcasted_iota(jnp.int32, (c, c), 0)
    cc = lax.broadcasted_iota(jnp.int32, (c, c), 1)
    tril, strict = r >= cc, r > cc
    eye = (r == cc).astype(f32)

    def dhi(a, b):
        return jnp.dot(a, b, precision=lax.Precision.HIGH, preferred_element_type=f32)

    beta = [gb[:, h:h + 1] for h in heads]
    cum_cols = dhi(tril.astype(f32), gb)
    cum_rows = dhi(gb.T, (r <= cc).astype(f32))
    gi = [jnp.broadcast_to(cum_cols[:, GDN_HEADS + h:GDN_HEADS + h + 1], (c, c)) for h in heads]
    gj = [jnp.broadcast_to(cum_rows[GDN_HEADS + h:GDN_HEADS + h + 1, :], (c, c)) for h in heads]
    decay = [jnp.where(tril, jnp.exp(jnp.where(tril, gi[h] - gj[h], 0.0)), 0.0) for h in heads]
    kb = [k[h] * beta[h] for h in heads]
    vb = [v[h] * beta[h] for h in heads]
    a = [jnp.where(strict, _d16(kb[h], k[h], NT) * decay[h], 0.0) for h in heads]
    tinv = _unit_lower_inverses(a) if kept_inverses is None else _known_inverses(a, kept_inverses)
    gc = [gi[h][:, 0:1] for h in heads]
    egc = [jnp.exp(gc[h]) for h in heads]
    u = [dhi(tinv[h], vb[h]) for h in heads]
    w = [dhi(tinv[h], kb[h] * egc[h]) for h in heads]
    qs = [q[h] * (GDN_HEAD_DIM ** -0.5) for h in heads]
    attn = [jnp.where(tril, _d16(qs[h], k[h], NT) * decay[h], 0.0) for h in heads]
    g_last = [gi[h][c - 1:c, 0:1] for h in heads]
    v_new = [u[h] - _d16(w[h], state[h], NN) for h in heads]
    out = [_d16(qs[h] * egc[h], state[h], NN) + _d16(attn[h], v_new[h], NN) for h in heads]
    new_state = [state[h] * jnp.exp(g_last[h]) + _d16(k[h] * jnp.exp(g_last[h] - gc[h]), v_new[h], TN) for h in heads]
    return out, new_state, tinv


def _head_cols(ref):
    return [ref[:, h * GDN_HEAD_DIM:(h + 1) * GDN_HEAD_DIM] for h in range(GDN_HEADS)]


def gdn_scan_fwd(qn, kn, v, gbeta, *, name, side=None):
    s = qn.shape[0]
    nc = s // GDN_CHUNK

    def body(q_ref, k_ref, v_ref, gb_ref, o_ref, st_ref, inv_ref, state_ref):
        @pl.when(pl.program_id(0) == 0)
        def _():
            state_ref[...] = jnp.zeros_like(state_ref)

        states = [state_ref[h] for h in range(GDN_HEADS)]
        outs, new, inverses = _gdn_chunks(_head_cols(q_ref), _head_cols(k_ref), _head_cols(v_ref), gb_ref[...], states)
        for h in range(GDN_HEADS):
            st_ref[0, h] = states[h]
            inv_ref[0, h] = inverses[h]
            o_ref[:, h * GDN_HEAD_DIM:(h + 1) * GDN_HEAD_DIM] = outs[h]
            state_ref[h] = new[h]

    row = pl.BlockSpec((GDN_CHUNK, GDN_W), lambda n: (n, 0))
    return _call(
        body, (qn, kn, v, gbeta), name=name, grid=(nc,),
        in_specs=[row, row, row, pl.BlockSpec((GDN_CHUNK, 128), lambda n: (n, 0))],
        out_specs=[row, pl.BlockSpec((1, GDN_HEADS, GDN_HEAD_DIM, GDN_HEAD_DIM), lambda n: (n, 0, 0, 0)),
                   pl.BlockSpec((1, GDN_HEADS, GDN_CHUNK, GDN_CHUNK), lambda n: (n, 0, 0, 0))],
        out_shape=[jax.ShapeDtypeStruct((s, GDN_W), f32),
                   jax.ShapeDtypeStruct((nc, GDN_HEADS, GDN_HEAD_DIM, GDN_HEAD_DIM), f32),
                   jax.ShapeDtypeStruct((nc, GDN_HEADS, GDN_CHUNK, GDN_CHUNK), f32)],
        scratch_shapes=[pltpu.VMEM((GDN_HEADS, GDN_HEAD_DIM, GDN_HEAD_DIM), f32)], semantics=("arbitrary",), side=side)


def gdn_scan_bwd(qn, kn, v, gbeta, states, inverses, dout, *, name, side=None):
    s = qn.shape[0]
    nc = s // GDN_CHUNK

    def body(q_ref, k_ref, v_ref, gb_ref, st_ref, inv_ref, do_ref, dq_ref, dk_ref, dv_ref, dgb_ref, dstate_ref):
        @pl.when(pl.program_id(0) == 0)
        def _():
            dstate_ref[...] = jnp.zeros_like(dstate_ref)

        kept = [inv_ref[0, h] for h in range(GDN_HEADS)]
        _, pull = jax.vjp(lambda *args: _gdn_chunks(*args, kept_inverses=kept)[:2],
                          _head_cols(q_ref), _head_cols(k_ref), _head_cols(v_ref), gb_ref[...],
                          [st_ref[0, h] for h in range(GDN_HEADS)])
        dq, dk, dv, dgb, dst = pull((_head_cols(do_ref), [dstate_ref[h] for h in range(GDN_HEADS)]))
        for h in range(GDN_HEADS):
            sl = slice(h * GDN_HEAD_DIM, (h + 1) * GDN_HEAD_DIM)
            dq_ref[:, sl] = dq[h]
            dk_ref[:, sl] = dk[h]
            dv_ref[:, sl] = dv[h]
            dstate_ref[h] = dst[h]
        dgb_ref[...] = dgb

    row = pl.BlockSpec((GDN_CHUNK, GDN_W), lambda n: (nc - 1 - n, 0))
    gb = pl.BlockSpec((GDN_CHUNK, 128), lambda n: (nc - 1 - n, 0))
    return _call(
        body, (qn, kn, v, gbeta, states, inverses, dout), name=name, grid=(nc,),
        in_specs=[row, row, row, gb, pl.BlockSpec((1, GDN_HEADS, GDN_HEAD_DIM, GDN_HEAD_DIM), lambda n: (nc - 1 - n, 0, 0, 0)),
                  pl.BlockSpec((1, GDN_HEADS, GDN_CHUNK, GDN_CHUNK), lambda n: (nc - 1 - n, 0, 0, 0)), row],
        out_specs=[row, row, row, gb],
        out_shape=[jax.ShapeDtypeStruct((s, GDN_W), f32)] * 3 + [jax.ShapeDtypeStruct((s, 128), f32)],
        scratch_shapes=[pltpu.VMEM((GDN_HEADS, GDN_HEAD_DIM, GDN_HEAD_DIM), f32)], semantics=("arbitrary",), side=side)


def adamw(w, g, m, v, *, name):
    r, c = w.shape
    tr = _tile(r, 256, 8)

    def body(w_ref, g_ref, m_ref, v_ref, d_ref, nm_ref, nv_ref):
        gv = g_ref[...]
        nm = ADAM_B1 * m_ref[...] + (1.0 - ADAM_B1) * gv
        nv = ADAM_B2 * v_ref[...] + (1.0 - ADAM_B2) * (gv * gv)
        m_hat = nm / (1.0 - ADAM_B1 ** ADAM_STEP)
        v_hat = nv / (1.0 - ADAM_B2 ** ADAM_STEP)
        d_ref[...] = -ADAM_LR * (m_hat / (jnp.sqrt(v_hat) + ADAM_EPS) + ADAM_WD * w_ref[...])
        nm_ref[...] = nm
        nv_ref[...] = nv

    spec = pl.BlockSpec((tr, c), lambda i: (i, 0))
    return pl.pallas_call(body, name=name, grid=(r // tr,), in_specs=[spec] * 4, out_specs=[spec] * 3,
                          out_shape=[jax.ShapeDtypeStruct((r, c), f32)] * 3, compiler_params=_params("parallel"))(w, g, m, v)


def _pos():
    return lax.axis_index("x"), lax.axis_index("y"), lax.axis_index("c")


ANY = pl.BlockSpec(memory_space=pl.ANY)


class Side(NamedTuple):
    ins: list
    outs: list
    aliases: dict
    sems: list
    start: Callable
    wait: Callable


def join_sides(*sides):
    def spans(key):
        out, off = [], 0
        for sd in sides:
            out.append(slice(off, off + len(getattr(sd, key))))
            off += len(getattr(sd, key))
        return out

    i_sp, o_sp, s_sp = spans("ins"), spans("outs"), spans("sems")
    aliases = {i_sp[n].start + i: o_sp[n].start + o for n, sd in enumerate(sides) for i, o in sd.aliases.items()}

    def each(what):
        def run(ins, outs, sems):
            for n, sd in enumerate(sides):
                getattr(sd, what)(ins[i_sp[n]], outs[o_sp[n]], sems[s_sp[n]])
        return run

    return Side([a for sd in sides for a in sd.ins], [o for sd in sides for o in sd.outs], aliases,
                [s for sd in sides for s in sd.sems], each("start"), each("wait"))


def _side_body(body, side, n_in, n_out, n_scratch, grid):
    ns_in, ns_out = len(side.ins), len(side.outs)

    def wrapped(*refs):
        cut = [n_in, ns_in, n_out, ns_out, n_scratch]
        parts, off = [], 0
        for c in cut:
            parts.append(refs[off:off + c])
            off += c
        ins, s_ins, outs, s_outs, scratch = parts
        sems = refs[off:]
        if grid:
            ids = [pl.program_id(d) for d in range(len(grid))]
            first = functools.reduce(jnp.logical_and, [i == 0 for i in ids])
            last = functools.reduce(jnp.logical_and, [i == g - 1 for i, g in zip(ids, grid)])
            pl.when(first)(lambda: side.start(s_ins, s_outs, sems))
            body(*ins, *outs, *scratch)
            pl.when(last)(lambda: side.wait(s_ins, s_outs, sems))
        else:
            side.start(s_ins, s_outs, sems)
            side.wait(s_ins, s_outs, sems)

    return wrapped


def _call(body, args, *, name, grid, in_specs, out_specs, out_shape, semantics, scratch_shapes=(), side=None):
    if side is None:
        return pl.pallas_call(body, name=name, grid=grid, in_specs=in_specs, out_specs=out_specs, out_shape=out_shape,
                              scratch_shapes=list(scratch_shapes), compiler_params=_params(*semantics))(*args)
    single = not isinstance(out_shape, (list, tuple))
    shapes, specs = ([out_shape], [out_specs]) if single else (list(out_shape), list(out_specs))
    n_in, n_out = len(in_specs), len(shapes)
    res = pl.pallas_call(
        _side_body(body, side, n_in, n_out, len(scratch_shapes), grid), name=name, grid=grid,
        in_specs=list(in_specs) + [ANY] * len(side.ins), out_specs=specs + [ANY] * len(side.outs),
        out_shape=shapes + list(side.outs), scratch_shapes=list(scratch_shapes) + list(side.sems),
        input_output_aliases={n_in + i: n_out + o for i, o in side.aliases.items()},
        compiler_params=_params(*(["arbitrary"] * len(grid))),
    )(*args, *side.ins)
    return (res[0] if single else res[:n_out]), list(res[n_out:])


def run_side(side, *, name):
    return pl.pallas_call(_side_body(None, side, 0, 0, 0, ()), name=name, in_specs=[ANY] * len(side.ins),
                          out_specs=[ANY] * len(side.outs), out_shape=list(side.outs), scratch_shapes=list(side.sems),
                          input_output_aliases=dict(side.aliases))(*side.ins)


def _remote(src, dst, send, recv, k, to):
    return pltpu.make_async_remote_copy(src_ref=src, dst_ref=dst, send_sem=send.at[k], recv_sem=recv.at[k], device_id=to,
                                        device_id_type=MESH)


def gather_first(shards):
    na = len(shards)

    def copies(x_refs, out_refs, sems):
        send, recv, local = sems
        x, y, cc = _pos()
        me = 4 * x + 2 * y + cc
        peers = [(x, y, 1 - cc), (1 - x, y, cc), (x, 1 - y, cc), (1 - x, 1 - y, cc)]
        mine = [pltpu.make_async_copy(x_refs[a], out_refs[a].at[me], local.at[a]) for a in range(na)]
        sent = [_remote(x_refs[a], out_refs[a].at[me], send, recv, 4 * a + k, p) for a in range(na) for k, p in enumerate(peers)]
        landed = [_remote(x_refs[a], out_refs[a].at[4 * p[0] + 2 * p[1] + p[2]], send, recv, 4 * a + k, p)
                  for a in range(na) for k, p in enumerate(peers)]
        return mine, sent, landed

    def start(x_refs, out_refs, sems):
        mine, sent, _ = copies(x_refs, out_refs, sems)
        for cp in mine + sent:
            cp.start()

    def wait(x_refs, out_refs, sems):
        mine, sent, landed = copies(x_refs, out_refs, sems)
        for cp in sent:
            cp.wait_send()
        for cp in landed:
            cp.wait_recv()
        for cp in mine:
            cp.wait()

    return Side(list(shards), [jax.ShapeDtypeStruct((N_DEV,) + s.shape, s.dtype) for s in shards], {},
                [pltpu.SemaphoreType.DMA((4 * na,)), pltpu.SemaphoreType.DMA((4 * na,)), pltpu.SemaphoreType.DMA((na,))],
                start, wait)


def gather_second(slots):
    na = len(slots)

    def copies(out_refs, sems):
        send, recv = sems
        x, y, cc = _pos()
        chips = [(1 - x, y), (x, 1 - y), (1 - x, 1 - y)]
        sent, landed = [], []
        for a in range(na):
            for j, (px, py) in enumerate(chips):
                row = out_refs[a].at[4 * px + 2 * py + cc]
                sent.append(_remote(row, row, send, recv, 3 * a + j, (x, y, 1 - cc)))
                landed.append(_remote(row, out_refs[a].at[4 * px + 2 * py + 1 - cc], send, recv, 3 * a + j, (x, y, 1 - cc)))
        return sent, landed

    def start(_, out_refs, sems):
        for cp in copies(out_refs, sems)[0]:
            cp.start()

    def wait(_, out_refs, sems):
        sent, landed = copies(out_refs, sems)
        for cp in sent:
            cp.wait_send()
        for cp in landed:
            cp.wait_recv()

    return Side(list(slots), [jax.ShapeDtypeStruct(s.shape, s.dtype) for s in slots], {a: a for a in range(na)},
                [pltpu.SemaphoreType.DMA((3 * na,)), pltpu.SemaphoreType.DMA((3 * na,))], start, wait)


def grad_to_sibling(chunks):
    na = len(chunks)

    def start(g_refs, out_refs, sems):
        send, recv = sems
        x, y, cc = _pos()
        for a in range(na):
            for q in range(4):
                _remote(g_refs[a].at[2 * q + 1 - cc], out_refs[a].at[q], send, recv, a, (x, y, 1 - cc)).start()

    def wait(g_refs, out_refs, sems):
        send, recv = sems
        x, y, cc = _pos()
        for a in range(na):
            _remote(out_refs[a], out_refs[a], send, recv, a, (x, y, 1 - cc)).wait()

    return Side(list(chunks), [jax.ShapeDtypeStruct((4,) + g.shape[1:], g.dtype) for g in chunks], {},
                [pltpu.SemaphoreType.DMA((na,)), pltpu.SemaphoreType.DMA((na,))], start, wait)


def grad_to_chips(parts):
    na = len(parts)

    def copies(p_refs, out_refs, sems):
        send, recv, local = sems
        x, y, cc = _pos()
        chips = [(1 - x, y), (x, 1 - y), (1 - x, 1 - y)]
        mine = [pltpu.make_async_copy(p_refs[a].at[2 * x + y], out_refs[a].at[3], local.at[a]) for a in range(na)]
        sent = [_remote(p_refs[a].at[2 * px + py], out_refs[a].at[k], send, recv, 3 * a + k, (px, py, cc))
                for a in range(na) for k, (px, py) in enumerate(chips)]
        return mine, sent

    def start(p_refs, out_refs, sems):
        mine, sent = copies(p_refs, out_refs, sems)
        for cp in mine + sent:
            cp.start()

    def wait(p_refs, out_refs, sems):
        mine, sent = copies(p_refs, out_refs, sems)
        for cp in sent:
            cp.wait()
        for cp in mine:
            cp.wait()

    return Side(list(parts), [jax.ShapeDtypeStruct(p.shape, p.dtype) for p in parts], {},
                [pltpu.SemaphoreType.DMA((3 * na,)), pltpu.SemaphoreType.DMA((3 * na,)), pltpu.SemaphoreType.DMA((na,))],
                start, wait)


def add_sibling(chunks, recv, *, name):
    _, r, c = chunks.shape
    tr = r if r <= 1024 else _tile(r, 512, 16)
    core = lax.axis_index("c").astype(jnp.int32).reshape(1)

    def body(core_ref, a_ref, b_ref, o_ref):
        o_ref[...] = (a_ref[...] + b_ref[...]).astype(bf16)

    return pl.pallas_call(
        body, name=name,
        grid_spec=pltpu.PrefetchScalarGridSpec(
            num_scalar_prefetch=1, grid=(4, r // tr),
            in_specs=[pl.BlockSpec((1, tr, c), lambda q, i, core_ref: (2 * q + core_ref[0], i, 0)),
                      pl.BlockSpec((1, tr, c), lambda q, i, core_ref: (q, i, 0))],
            out_specs=pl.BlockSpec((1, tr, c), lambda q, i, core_ref: (q, i, 0))),
        out_shape=jax.ShapeDtypeStruct((4, r, c), bf16), compiler_params=_params("parallel", "parallel"),
    )(core, chunks, recv)


def add_four(r4, *, name):
    _, r, c = r4.shape
    tr = r if r <= 1024 else _tile(r, 512, 16)

    def body(a_ref, o_ref):
        o_ref[...] = ((a_ref[3].astype(f32) + a_ref[0].astype(f32)) + a_ref[1].astype(f32)) + a_ref[2].astype(f32)

    return pl.pallas_call(body, name=name, grid=(r // tr,), in_specs=[pl.BlockSpec((4, tr, c), lambda i: (0, i, 0))],
                          out_specs=pl.BlockSpec((tr, c), lambda i: (i, 0)), out_shape=jax.ShapeDtypeStruct((r, c), f32),
                          compiler_params=_params("parallel"))(r4)


def all_reduce_small(vec, *, name):
    r, c = vec.shape

    def body(v_ref, out_ref, buf_ref, send_sems, recv_sems):
        x, y, cc = _pos()
        my_id = 4 * x + 2 * y + cc
        buf_ref[my_id] = v_ref[...]
        flips = [(fx, fy, fc) for fx in (0, 1) for fy in (0, 1) for fc in (0, 1)][1:]
        cps = []
        for k, (fx, fy, fc) in enumerate(flips):
            peer = ((1 - x) if fx else x, (1 - y) if fy else y, (1 - cc) if fc else cc)
            cps.append(pltpu.make_async_remote_copy(src_ref=v_ref, dst_ref=buf_ref.at[my_id], send_sem=send_sems.at[k],
                                                    recv_sem=recv_sems.at[k], device_id=peer, device_id_type=MESH))
        for cp in cps:
            cp.start()
        for cp in cps:
            cp.wait()
        acc = buf_ref[0]
        for d in range(1, N_DEV):
            acc = acc + buf_ref[d]
        out_ref[...] = acc

    vm = pl.BlockSpec(memory_space=pltpu.VMEM)
    return pl.pallas_call(body, name=name, in_specs=[vm], out_specs=vm, out_shape=jax.ShapeDtypeStruct((r, c), f32),
                          scratch_shapes=[pltpu.VMEM((N_DEV, r, c), f32), pltpu.SemaphoreType.DMA((7,)),
                                          pltpu.SemaphoreType.DMA((7,))])(vec)


def _pack(parts, rows, dtype):
    flat = jnp.concatenate([p.reshape(-1).astype(dtype) for p in parts])
    return jnp.pad(flat, (0, rows * PACK_COLS - flat.shape[0])).reshape(rows, PACK_COLS)


def _unpack(flat, shapes):
    out, off = [], 0
    for shp in shapes:
        n = shp[0] * shp[1]
        out.append(flat[..., off:off + n].reshape(flat.shape[:-1] + tuple(shp)))
        off += n
    return out


def _from_column_shards(g):
    _, r, c = g.shape
    return jnp.transpose(g, (1, 0, 2)).reshape(r, N_DEV * c)


def _column_shards(full):
    r, c8 = full.shape
    return jnp.transpose(full.reshape(r, N_DEV, c8 // N_DEV), (1, 0, 2))


W_IN_SHARD = IN_DIM // N_DEV
W_IN_PAD = 1280
W_IN_PARTS = (("swa", 0, 0, 1280), ("swa", 1280, 5376, 5392), ("gdn", 0, 1280, 5376), ("gates", 0, 5392, IN_DIM))
W_IN_WIDTHS = {"swa": SWA_IN_W, "gdn": 4 * GDN_W, "gates": 2 * D_MODEL}


def _w_in_segments():
    segs = []
    for part, p0, g0, g1 in W_IN_PARTS:
        for j in range(N_DEV):
            lo, hi = max(g0, W_IN_SHARD * j), min(g1, W_IN_SHARD * (j + 1))
            if lo < hi:
                segs.append((part, p0 + lo - g0, j, lo - W_IN_SHARD * j, hi - lo))
    return segs


def split_w_in(shards, *, name):
    dt = shards.dtype
    tm = 256

    def body(w_ref, swa_ref, gdn_ref, gates_ref):
        out = {"swa": swa_ref, "gdn": gdn_ref, "gates": gates_ref}
        swa_ref[:, SWA_Q + 2 * SWA_KV + 2 * GDN_HEADS:] = jnp.zeros((tm, SWA_IN_W - SWA_Q - 2 * SWA_KV - 2 * GDN_HEADS), dt)
        for part, p0, j, l0, n in _w_in_segments():
            out[part][:, p0:p0 + n] = w_ref[j, :, l0:l0 + n]

    return pl.pallas_call(body, name=name, grid=(D_MODEL // tm,),
                          in_specs=[pl.BlockSpec((N_DEV, tm, W_IN_PAD), lambda i: (0, i, 0))],
                          out_specs=[pl.BlockSpec((tm, W_IN_WIDTHS[p]), lambda i: (i, 0)) for p in ("swa", "gdn", "gates")],
                          out_shape=[jax.ShapeDtypeStruct((D_MODEL, W_IN_WIDTHS[p]), dt) for p in ("swa", "gdn", "gates")],
                          compiler_params=_params("parallel"))(shards)


def merge_w_in_grad(d_swa, d_gdn, d_gates, *, name):
    tm = 256

    def body(swa_ref, gdn_ref, gates_ref, w_ref):
        src = {"swa": swa_ref, "gdn": gdn_ref, "gates": gates_ref}
        w_ref[:, :, W_IN_SHARD:] = jnp.zeros((N_DEV, tm, W_IN_PAD - W_IN_SHARD), f32)
        for part, p0, j, l0, n in _w_in_segments():
            w_ref[j, :, l0:l0 + n] = src[part][:, p0:p0 + n]

    return pl.pallas_call(body, name=name, grid=(D_MODEL // tm,),
                          in_specs=[pl.BlockSpec((tm, W_IN_WIDTHS[p]), lambda i: (i, 0)) for p in ("swa", "gdn", "gates")],
                          out_specs=pl.BlockSpec((N_DEV, tm, W_IN_PAD), lambda i: (0, i, 0)),
                          out_shape=jax.ShapeDtypeStruct((N_DEV, D_MODEL, W_IN_PAD), f32),
                          compiler_params=_params("parallel"))(d_swa, d_gdn, d_gates)


def kernel(x, mem, w_in, rel_bias, swa_sinks, gdn_conv_w, gdn_a_log, gdn_dt_bias, gdn_norm_w, w_br_swa, w_br_gdn, w_mix_o, ln1_g, ln1_b, w_mem_q, w_mem_kv, w_mem_o, ln2_g, ln2_b, w_up, ffn_conv_w, ffn_conv_b, w_down, ln3_g, ln3_b, loss_target, m_w_in, m_rel_bias, m_swa_sinks, m_gdn_conv_w, m_gdn_a_log, m_gdn_dt_bias, m_gdn_norm_w, m_w_br_swa, m_w_br_gdn, m_w_mix_o, m_ln1_g, m_ln1_b, m_w_mem_q, m_w_mem_kv, m_w_mem_o, m_ln2_g, m_ln2_b, m_w_up, m_ffn_conv_w, m_ffn_conv_b, m_w_down, m_ln3_g, m_ln3_b, v_w_in, v_rel_bias, v_swa_sinks, v_gdn_conv_w, v_gdn_a_log, v_gdn_dt_bias, v_gdn_norm_w, v_w_br_swa, v_w_br_gdn, v_w_mix_o, v_ln1_g, v_ln1_b, v_w_mem_q, v_w_mem_kv, v_w_mem_o, v_ln2_g, v_ln2_b, v_w_up, v_ffn_conv_w, v_ffn_conv_b, v_w_down, v_ln3_g, v_ln3_b):
    env = dict(locals())
    w2 = {n: (env[n][0] if env[n].ndim == 3 else env[n]) for n in WEIGHTS}
    m2 = {n: (env["m_" + n][0] if env["m_" + n].ndim == 3 else env["m_" + n]) for n in WEIGHTS}
    v2 = {n: (env["v_" + n][0] if env["v_" + n].ndim == 3 else env["v_" + n]) for n in WEIGHTS}
    xs, mems, target = x[0], mem[0], loss_target[0]
    my_id = 4 * lax.axis_index("x") + 2 * lax.axis_index("y") + lax.axis_index("c")
    pad_ff = FF_PAD - FF_SHARD

    pad_cols = {"w_in": W_IN_PAD - W_IN_SHARD, "w_up": pad_ff}
    mid = ("w_br_swa", "w_br_gdn", "w_mem_o", "w_mix_o", "w_mem_q", "w_mem_kv")
    mine = {n: jnp.pad(w2[n], ((0, 0), (0, pad_cols.get(n, 0)))).astype(bf16) for n in ("w_in", "w_up", "w_down") + mid}
    got_in = run_side(gather_second(run_side(gather_first([mine["w_in"]]), name="gather_w_in")), name="gather_w_in_pass_on")
    w_swa, w_gdn, w_gates = split_w_in(got_in[0], name="split_w_in")
    n_ffn, n_gdn = 3 * FF_SHARD, GDN_CONV * (QKV_W // N_DEV)
    conv_mine = jnp.concatenate([w2["ffn_conv_w"].reshape(-1), w2["gdn_conv_w"].reshape(-1)])[None]
    conv_rows = lax.dynamic_update_slice(jnp.zeros((N_DEV, n_ffn + n_gdn), f32), conv_mine, (my_id, 0))
    conv_all = all_reduce_small(_pack([conv_rows], CONV_ROWS, f32), name="gather_conv_w")
    conv_all = conv_all.reshape(-1)[:N_DEV * (n_ffn + n_gdn)].reshape(N_DEV, n_ffn + n_gdn)
    cwb = jnp.concatenate([conv_all[:, :n_ffn].reshape(N_DEV, 3, FF_SHARD), w2["ffn_conv_b"].reshape(N_DEV, 1, FF_SHARD),
                           jnp.zeros((N_DEV, 4, FF_SHARD), f32)], axis=1)
    cwb = jnp.pad(cwb, ((0, 0), (0, 0), (0, pad_ff)))
    convw = jnp.transpose(conv_all[:, n_ffn:].reshape(N_DEV, GDN_CONV, QKV_W // N_DEV), (1, 0, 2)).reshape(GDN_CONV, QKV_W)
    convw = jnp.pad(convw, ((0, 4), (0, 0)))
    onehot = _bucket_onehot()
    bias = mm(w2["rel_bias"].T, onehot, "nn", hi=True, tn=4096, name="rel_bias_table").reshape(SWA_HEADS, BLOCK, 2 * BLOCK)
    alog_row = jnp.pad(w2["gdn_a_log"], ((0, 0), (GDN_HEADS, 128 - 2 * GDN_HEADS)))
    dt_row = jnp.pad(w2["gdn_dt_bias"], ((0, 0), (GDN_HEADS, 128 - 2 * GDN_HEADS)))

    xb = cast_bf16(xs, name="cast_x")
    memb = cast_bf16(mems, name="cast_mem")
    gates, mid_got = mm(xb, w_gates, "nn", out_dtype=bf16, name="proj_gates", side=gather_first([mine[n] for n in mid]))
    gdn_in, mid_got = mm(xb, w_gdn, "nn", name="proj_gdn", side=gather_second(mid_got))
    got = dict(zip(mid, mid_got))
    w_br_swa, w_br_gdn, w_mem_o = (_from_column_shards(got[n]) for n in ("w_br_swa", "w_br_gdn", "w_mem_o"))
    w_mix_o = got["w_mix_o"].reshape(D_MODEL, D_MODEL)
    w_mem_q = got["w_mem_q"].reshape(D_MODEL, MEM_W)
    w_mem_kv = got["w_mem_kv"].reshape(D_MODEL, 2 * MEM_W)
    swa_in = mm(xb, w_swa, "nn", tn=SWA_IN_W, name="proj_swa")
    attn, down_got = swa_fwd(swa_in, bias, w2["swa_sinks"], name="swa_fwd", side=gather_first([mine["w_down"]]))
    qn, kn, vv, gdn_conv = gdn_pre_fwd(gdn_in, convw, name="gdn_pre_fwd")
    gbeta = gbeta_fwd(swa_in, alog_row, dt_row, name="gbeta_fwd")
    (o_gdn, states, inverses), up_got = gdn_scan_fwd(qn, kn, vv, gbeta, name="gdn_scan_fwd",
                                                     side=gather_first([mine["w_up"]]))
    ygd = gdn_post_fwd(o_gdn, gdn_in, w2["gdn_norm_w"], name="gdn_post_fwd")
    y_swa, down_got = mm(attn, w_br_swa, "nn", out_dtype=bf16, name="br_swa", side=gather_second(down_got))
    y_gdn, up_got = mm(ygd, w_br_gdn, "nn", out_dtype=bf16, name="br_gdn", side=gather_second(up_got))
    w_up_blocked = up_got[0]
    w_down_p = jnp.pad(down_got[0].reshape(4, FF_SHARD, D_MODEL), ((0, 0), (0, pad_ff), (0, 0))).reshape(4 * FF_PAD, D_MODEL)
    mixed = merge_fwd(gates, y_swa, y_gdn, name="merge_fwd")
    z1 = mm(mixed, w_mix_o, "nn", add=xs, add_scale=ALPHA, name="mix_o")
    x1, x1b = ln_fwd(z1, w2["ln1_g"], w2["ln1_b"], name="ln1_fwd")
    qm = mm(x1b, w_mem_q, "nn", name="mem_q")
    kv = mm(memb, w_mem_kv, "nn", name="mem_kv")
    om = memattn_fwd(qm, kv, name="memattn_fwd")
    z2 = mm(om, w_mem_o, "nn", add=x1, add_scale=ALPHA, name="mem_o")
    x2, x2b = ln_fwd(z2, w2["ln2_g"], w2["ln2_b"], name="ln2_fwd")
    hpre = mm(x2b, w_up_blocked, "nn", b_blocked=True, out_dtype=bf16, name="ffn_up")
    act, conv_g, conv_u = ffn_act_fwd(hpre, cwb, name="ffn_act_fwd")
    z3 = mm(act, w_down_p, "nn", add=x2, add_scale=ALPHA, tk=2 * FF_PAD, name="ffn_down")
    dz3, dz3b, d_ln3g, d_ln3b, loss = ln_loss(z3, target, w2["ln3_g"], w2["ln3_b"], name="ln3_loss")

    dact = mm(dz3b, w_down_p, "nt", tn=FF_PAD, out_dtype=bf16, name="d_act")
    d_wdown_p = mm(act, dz3b, "tn", tm=FF_PAD, name="dw_down")
    d_hpre, d_cwb = ffn_act_bwd(hpre, conv_g, conv_u, dact, cwb, name="ffn_act_bwd")
    def sibling_sums(names, chunks, received):
        return [add_sibling(c, r, name="grad_add_sibling_" + n) for n, c, r in zip(names, chunks, received)]

    def chip_sums(names, received):
        return [add_four(r, name="grad_add_chips_" + n) for n, r in zip(names, received)]

    dx2 = mm(d_hpre, w_up_blocked, "nt", b_blocked=True, k_shards=2, add=dz3, add_scale=ALPHA, name="d_x2")
    d_wup = mm(x2b, d_hpre, "tn", out_blocked=True, name="dw_up")
    ffn = ("w_up", "w_down")
    ffn_chunks = [d_wup, d_wdown_p.reshape(4, FF_PAD, D_MODEL)[:, :FF_SHARD].reshape(N_DEV, FF_SHARD // 2, D_MODEL)]
    dz2, dz2b, d_ln2g, d_ln2b = ln_bwd(dx2, z2, w2["ln2_g"], name="ln2_bwd")
    d_om, down_received = mm(dz2b, w_mem_o, "nt", name="d_om", side=grad_to_sibling(ffn_chunks[1:]))
    d_wmemo = mm(om, dz2b, "tn", name="dw_mem_o")
    dqm, dkv = memattn_bwd(qm, kv, d_om, name="memattn_bwd")
    dx1 = mm(dqm, w_mem_q, "nt", add=dz2, add_scale=ALPHA, name="d_x1")
    d_wmemq = mm(x1b, dqm, "tn", name="dw_mem_q")
    d_wmemkv = mm(memb, dkv, "tn", name="dw_mem_kv")
    dz1, dz1b, d_ln1g, d_ln1b = ln_bwd(dx1, z1, w2["ln1_g"], name="ln1_bwd")
    dmix, up_received = mm(dz1b, w_mix_o, "nt", name="d_mixed", side=grad_to_sibling(ffn_chunks[:1]))
    ffn_sums = sibling_sums(ffn, ffn_chunks, up_received + down_received)
    d_wmixo = mm(mixed, dz1b, "tn", name="dw_mix_o")
    dys, dyg, d_gates = merge_bwd(gates, y_swa, y_gdn, dmix, name="merge_bwd")
    d_attn = mm(dys, w_br_swa, "nt", name="d_attn")
    d_wbrswa = mm(attn, dys, "tn", name="dw_br_swa")
    d_ygd = mm(dyg, w_br_gdn, "nt", name="d_ygd")
    d_wbrgdn = mm(ygd, dyg, "tn", name="dw_br_gdn")
    mid_chunks = [_column_shards(d_wbrswa), _column_shards(d_wbrgdn), _column_shards(d_wmemo),
                  d_wmixo.reshape(N_DEV, D_MODEL // N_DEV, D_MODEL), d_wmemq.reshape(N_DEV, D_MODEL // N_DEV, MEM_W),
                  d_wmemkv.reshape(N_DEV, D_MODEL // N_DEV, 2 * MEM_W)]
    d_o, d_gz, d_normw = gdn_post_bwd(o_gdn, gdn_in, w2["gdn_norm_w"], d_ygd, name="gdn_post_bwd")
    (dqn, dkn, dvv, dgbeta), received = gdn_scan_bwd(
        qn, kn, vv, gbeta, states, inverses, d_o, name="gdn_scan_bwd",
        side=join_sides(grad_to_chips(ffn_sums), grad_to_sibling(mid_chunks)))
    grads = dict(zip(ffn, chip_sums(ffn, received[:2])))
    mid_sums = sibling_sums(mid, mid_chunks, received[2:])
    d_gdn_in, d_convw = gdn_pre_bwd(gdn_in, gdn_conv, convw, dqn, dkn, dvv, d_gz, name="gdn_pre_bwd")
    d_ba, d_alog, d_dt = gbeta_bwd(swa_in, alog_row, dt_row, dgbeta, name="gbeta_bwd")
    (dq, dkc, dkp, dvc, dvp, d_bias, d_sinks), received = swa_bwd(swa_in, bias, w2["swa_sinks"], d_attn, name="swa_bwd",
                                                                  side=grad_to_chips(mid_sums))
    grads.update(zip(mid, chip_sums(mid, received)))
    d_swa_in = swa_in_grad(dq, dkc, dkp, dvc, dvp, d_ba, name="swa_in_grad")
    d_relbias = mm(d_bias.reshape(SWA_HEADS, -1), onehot, "nt", hi=True, tk=4096, name="d_rel_bias").T
    d_wgates = mm(xb, d_gates, "tn", name="dw_gates")
    d_wgdn = mm(xb, d_gdn_in, "tn", name="dw_gdn")
    d_wswa = mm(xb, d_swa_in, "tn", tn=SWA_IN_W, name="dw_swa")
    in_chunks = [merge_w_in_grad(d_wswa, d_wgdn, d_wgates, name="merge_w_in_grad")]
    gx, received = mm(d_gates, w_gates, "nt", add=dz1, add_scale=ALPHA, name="dx_gates", side=grad_to_sibling(in_chunks))
    in_sums = sibling_sums(("w_in",), in_chunks, received)
    gx, received = mm(d_gdn_in, w_gdn, "nt", add=gx, name="dx_gdn", side=grad_to_chips(in_sums))
    gx = mm(d_swa_in, w_swa, "nt", add=gx, tk=SWA_IN_W, name="dx_swa")
    grads["w_in"] = chip_sums(("w_in",), received)[0][:, :W_IN_SHARD]
    grads["w_up"] = grads["w_up"][:, :FF_SHARD]

    gsmall = {
        "rel_bias": d_relbias, "swa_sinks": d_sinks[:, :SWA_HEADS], "gdn_a_log": d_alog[:, GDN_HEADS:2 * GDN_HEADS],
        "gdn_dt_bias": d_dt[:, GDN_HEADS:2 * GDN_HEADS], "gdn_norm_w": d_normw, "ln1_g": d_ln1g, "ln1_b": d_ln1b,
        "ln2_g": d_ln2g, "ln2_b": d_ln2b, "ln3_g": d_ln3g, "ln3_b": d_ln3b,
        "ffn_conv_b": d_cwb[:, 3, :FF_SHARD].reshape(1, 2 * D_FF),
        "ffn_conv_w": jnp.transpose(d_cwb[:, :3, :FF_SHARD], (1, 0, 2)).reshape(3, 2 * D_FF),
        "gdn_conv_w": d_convw[:GDN_CONV],
    }
    small_shapes = [shp for _, shp in SMALL] + [(3, 2 * D_FF), (GDN_CONV, QKV_W)]
    small_names = [n for n, _ in SMALL] + ["ffn_conv_w", "gdn_conv_w"]
    small_sum = all_reduce_small(_pack([gsmall[n] for n in small_names], AR_ROWS, f32), name="all_reduce_small")
    grads.update(zip(small_names, _unpack(small_sum.reshape(-1), small_shapes)))
    grads["ffn_conv_w"] = lax.dynamic_slice_in_dim(grads["ffn_conv_w"], my_id * FF_SHARD, FF_SHARD, axis=1)
    grads["gdn_conv_w"] = lax.dynamic_slice_in_dim(grads["gdn_conv_w"], my_id * (QKV_W // N_DEV), QKV_W // N_DEV, axis=1)

    big = [n for n, shp, _ in SHARDED if shp[0] * shp[1] > 8192]
    tiny = [n for n in WEIGHTS if n not in big]
    delta, new_m, new_v = {}, {}, {}
    for n in big:
        delta[n], new_m[n], new_v[n] = adamw(w2[n], grads[n], m2[n], v2[n], name="adamw_" + n)
    tiny_shapes = [w2[n].shape for n in tiny]
    packed = [_pack([src[n] for n in tiny], SMALL_ROWS, f32) for src in (w2, grads, m2, v2)]
    for dst, res in zip((delta, new_m, new_v), adamw(*packed, name="adamw_small")):
        dst.update(zip(tiny, _unpack(res.reshape(-1), tiny_shapes)))

    def shaped(d):
        return [d[n].reshape(env[n].shape) for n in WEIGHTS]

    loss_all = lax.psum(loss[0, 0], ("x", "y", "c"))
    return (loss_all, gx[None], *shaped(grads), *shaped(delta), *shaped(new_m), *shaped(new_v))
```

```python
import functools
import math
from typing import Callable, NamedTuple

import jax
import jax.numpy as jnp
from jax import lax
from jax.experimental import pallas as pl
from jax.experimental.pallas import tpu as pltpu

f32 = jnp.float32
bf16 = jnp.bfloat16
HI = lax.Precision.HIGHEST
MESH = pl.DeviceIdType.MESH

D_MODEL = 2048
N_DEV = 8
SWA_HEADS, SWA_KV_HEADS, SWA_HEAD_DIM, BLOCK = 16, 2, 64, 128
REL_BUCKETS, REL_MAX_DIST = 32, 128
GDN_HEADS, GDN_HEAD_DIM, GDN_CONV, GDN_CHUNK = 8, 128, 4, 64
MEM_HEADS, MEM_HEAD_DIM = 4, 128
D_FF = 5504
FF_SHARD = 2 * D_FF // N_DEV
FF_PAD = 1408
NORM_EPS = 1e-5
ALPHA = 2.0 ** 0.25
NEG_INF = -1e30
SWA_Q, SWA_KV, GDN_W, MEM_W = 1024, 128, 1024, 512
IN_DIM = 9488
HALO = 8

ADAM_LR, ADAM_B1, ADAM_B2, ADAM_EPS, ADAM_WD, ADAM_STEP = 0.001, 0.9, 0.999, 1e-08, 0.01, 10

PACK_COLS = 1024
SMALL_ROWS = 32
AR_ROWS = 72
CONV_ROWS = 48

SHARDED = (
    ("w_in", (2048, 1186), 1), ("w_br_swa", (1024, 256), 1), ("w_br_gdn", (1024, 256), 1),
    ("w_mix_o", (256, 2048), 0), ("w_mem_q", (256, 512), 0), ("w_mem_kv", (256, 1024), 0),
    ("w_mem_o", (512, 256), 1), ("w_up", (2048, 1376), 1), ("w_down", (688, 2048), 0),
    ("ffn_conv_w", (3, 1376), 1), ("gdn_conv_w", (4, 384), 1),
)
SMALL = (
    ("rel_bias", (32, 16)), ("swa_sinks", (1, 16)), ("gdn_a_log", (1, 8)), ("gdn_dt_bias", (1, 8)),
    ("gdn_norm_w", (1, 128)), ("ln1_g", (1, 2048)), ("ln1_b", (1, 2048)), ("ln2_g", (1, 2048)),
    ("ln2_b", (1, 2048)), ("ln3_g", (1, 2048)), ("ln3_b", (1, 2048)), ("ffn_conv_b", (1, 11008)),
)
WEIGHTS = ("w_in", "rel_bias", "swa_sinks", "gdn_conv_w", "gdn_a_log", "gdn_dt_bias", "gdn_norm_w", "w_br_swa",
           "w_br_gdn", "w_mix_o", "ln1_g", "ln1_b", "w_mem_q", "w_mem_kv", "w_mem_o", "ln2_g", "ln2_b", "w_up",
           "ffn_conv_w", "ffn_conv_b", "w_down", "ln3_g", "ln3_b")


def _tile(n, target, align):
    if n <= target:
        return n
    t = (target // align) * align
    while t >= align:
        if n % t == 0:
            return t
        t -= align
    return n


VMEM_LIMIT_BYTES = 56 * 1024 * 1024


def _params(*sem):
    return pltpu.CompilerParams(dimension_semantics=sem, vmem_limit_bytes=VMEM_LIMIT_BYTES)


def _sigmoid(v):
    return jax.nn.sigmoid(v)


def _d16(a, b, dims):
    return lax.dot_general(a.astype(bf16), b.astype(bf16), (dims, ((), ())), preferred_element_type=f32)


NN = ((1,), (0,))
NT = ((1,), (1,))
TN = ((0,), (0,))


def mm(a, b, mode, *, name, add=None, add_scale=1.0, out_dtype=f32, hi=False, tm=1024, tn=1024, tk=2048,
       b_blocked=False, out_blocked=False, k_shards=1, side=None):
    if b_blocked:
        nb, rows, width = b.shape
        if mode == "nn":
            (m, k), n, tn = a.shape, nb * width, width
        else:
            (m, k), n, tk = a.shape, rows, k_shards * width
    elif mode == "nn":
        (m, k), (_, n) = a.shape, b.shape
    elif mode == "nt":
        (m, k), (n, _) = a.shape, b.shape
    else:
        (k, m), (_, n) = a.shape, b.shape
    if out_blocked:
        tn = n // N_DEV
    tm, tn, tk = _tile(m, tm, 8 if mode != "tn" else 128), _tile(n, tn, 128), _tile(k, tk, 128 if mode != "tn" else 8)
    nk = k // tk
    dims = {"nn": NN, "nt": NT, "tn": TN}[mode]
    a_spec = pl.BlockSpec((tk, tm), lambda i, j, kk: (kk, i)) if mode == "tn" else pl.BlockSpec((tm, tk), lambda i, j, kk: (i, kk))
    if b_blocked:
        b_spec = (pl.BlockSpec((None, tk, tn), lambda i, j, kk: (j, kk, 0)) if mode == "nn"
                  else pl.BlockSpec((k_shards, tn, tk // k_shards), lambda i, j, kk: (kk, j, 0)))
    else:
        b_spec = pl.BlockSpec((tn, tk), lambda i, j, kk: (j, kk)) if mode == "nt" else pl.BlockSpec((tk, tn), lambda i, j, kk: (kk, j))
    if out_blocked:
        o_spec, o_shape = pl.BlockSpec((None, tm, tn), lambda i, j, kk: (j, i, 0)), (N_DEV, m, tn)
    else:
        o_spec, o_shape = pl.BlockSpec((tm, tn), lambda i, j, kk: (i, j)), (m, n)
    has_add = add is not None

    def product(a_ref, b_ref):
        if hi:
            return lax.dot_general(a_ref[...], b_ref[...], (dims, ((), ())), precision=HI, preferred_element_type=f32)
        if b_blocked and mode == "nt":
            width = tk // k_shards
            parts = [_d16(a_ref[:, s * width:(s + 1) * width], b_ref[s], dims) for s in range(k_shards)]
            return functools.reduce(lambda p, q: p + q, parts)
        return _d16(a_ref[...], b_ref[...], dims)

    def finish(r, add_ref, o_ref):
        if has_add:
            r = r + add_scale * add_ref[...]
        o_ref[...] = r.astype(out_dtype)

    def body_one_step(a_ref, b_ref, *rest):
        finish(product(a_ref, b_ref), rest[0] if has_add else None, rest[-1])

    def body_k_steps(a_ref, b_ref, *rest):
        o_ref, acc_ref = rest[-2:]
        kk = pl.program_id(2)

        @pl.when(kk == 0)
        def _():
            acc_ref[...] = jnp.zeros_like(acc_ref)

        acc_ref[...] += product(a_ref, b_ref)

        @pl.when(kk == nk - 1)
        def _():
            finish(acc_ref[...], rest[0] if has_add else None, o_ref)

    return _call(body_one_step if nk == 1 else body_k_steps, (a, b, add) if has_add else (a, b), name=name,
                 grid=(m // tm, n // tn, nk), in_specs=[a_spec, b_spec] + ([o_spec] if has_add else []), out_specs=o_spec,
                 out_shape=jax.ShapeDtypeStruct(o_shape, out_dtype),
                 scratch_shapes=[] if nk == 1 else [pltpu.VMEM((tm, tn), f32)],
                 semantics=("parallel", "parallel", "arbitrary"), side=side)


def cast_bf16(a, *, name, side=None):
    m, n = a.shape
    tm = _tile(m, 512, 16)

    def body(a_ref, o_ref):
        o_ref[...] = a_ref[...].astype(bf16)

    return _call(body, (a,), name=name, grid=(m // tm,), in_specs=[pl.BlockSpec((tm, n), lambda i: (i, 0))],
                 out_specs=pl.BlockSpec((tm, n), lambda i: (i, 0)), out_shape=jax.ShapeDtypeStruct((m, n), bf16),
                 semantics=("parallel",), side=side)


def _ln_stats(z):
    mu = jnp.mean(z, axis=-1, keepdims=True)
    zc = z - mu
    var = jnp.mean(zc * zc, axis=-1, keepdims=True)
    rstd = lax.rsqrt(var + NORM_EPS)
    return zc * rstd, rstd


def ln_fwd(z, g, b, *, name):
    s, d = z.shape
    tm = _tile(s, 256, 16)

    def body(z_ref, g_ref, b_ref, y_ref, yb_ref):
        xhat, _ = _ln_stats(z_ref[...])
        y = xhat * g_ref[...] + b_ref[...]
        y_ref[...] = y
        yb_ref[...] = y.astype(bf16)

    row = pl.BlockSpec((tm, d), lambda i: (i, 0))
    vec = pl.BlockSpec((1, d), lambda i: (0, 0))
    return pl.pallas_call(body, name=name, grid=(s // tm,), in_specs=[row, vec, vec], out_specs=[row, row],
                          out_shape=[jax.ShapeDtypeStruct((s, d), f32), jax.ShapeDtypeStruct((s, d), bf16)],
                          compiler_params=_params("parallel"))(z, g, b)


def _ln_bwd_tile(dy, z, g):
    xhat, rstd = _ln_stats(z)
    dxh = dy * g
    m1 = jnp.mean(dxh, axis=-1, keepdims=True)
    m2 = jnp.mean(dxh * xhat, axis=-1, keepdims=True)
    dz = rstd * (dxh - m1 - xhat * m2)
    return dz, jnp.sum(dy * xhat, axis=0, keepdims=True), jnp.sum(dy, axis=0, keepdims=True)


def ln_bwd(dy, z, g, *, name):
    s, d = z.shape
    tm = _tile(s, 256, 16)

    def body(dy_ref, z_ref, g_ref, dz_ref, dzb_ref, dg_ref, db_ref):
        @pl.when(pl.program_id(0) == 0)
        def _():
            dg_ref[...] = jnp.zeros_like(dg_ref)
            db_ref[...] = jnp.zeros_like(db_ref)

        dz, dg, db = _ln_bwd_tile(dy_ref[...], z_ref[...], g_ref[...])
        dz_ref[...] = dz
        dzb_ref[...] = dz.astype(bf16)
        dg_ref[...] += dg
        db_ref[...] += db

    row = pl.BlockSpec((tm, d), lambda i: (i, 0))
    vec = pl.BlockSpec((1, d), lambda i: (0, 0))
    return pl.pallas_call(body, name=name, grid=(s // tm,), in_specs=[row, row, vec], out_specs=[row, row, vec, vec],
                          out_shape=[jax.ShapeDtypeStruct((s, d), f32), jax.ShapeDtypeStruct((s, d), bf16),
                                     jax.ShapeDtypeStruct((1, d), f32), jax.ShapeDtypeStruct((1, d), f32)],
                          compiler_params=_params("arbitrary"))(dy, z, g)


def ln_loss(z, target, g, b, *, name):
    s, d = z.shape
    tm = _tile(s, 256, 16)
    nt = s // tm

    def body(z_ref, t_ref, g_ref, b_ref, dz_ref, dzb_ref, dg_ref, db_ref, loss_ref, lacc_ref):
        i = pl.program_id(0)

        @pl.when(i == 0)
        def _():
            dg_ref[...] = jnp.zeros_like(dg_ref)
            db_ref[...] = jnp.zeros_like(db_ref)
            lacc_ref[...] = jnp.zeros_like(lacc_ref)

        zv, gv = z_ref[...], g_ref[...]
        xhat, _ = _ln_stats(zv)
        err = xhat * gv + b_ref[...] - t_ref[...]
        lacc_ref[...] += jnp.sum(err * err, axis=0, keepdims=True)
        dz, dg, db = _ln_bwd_tile(err * (1.0 / d), zv, gv)
        dz_ref[...] = dz
        dzb_ref[...] = dz.astype(bf16)
        dg_ref[...] += dg
        db_ref[...] += db

        @pl.when(i == nt - 1)
        def _():
            loss_ref[...] = (0.5 / d) * jnp.sum(lacc_ref[...], axis=1, keepdims=True)

    row = pl.BlockSpec((tm, d), lambda i: (i, 0))
    vec = pl.BlockSpec((1, d), lambda i: (0, 0))
    return pl.pallas_call(body, name=name, grid=(nt,), in_specs=[row, row, vec, vec],
                          out_specs=[row, row, vec, vec, pl.BlockSpec((1, 1), lambda i: (0, 0))],
                          out_shape=[jax.ShapeDtypeStruct((s, d), f32), jax.ShapeDtypeStruct((s, d), bf16),
                                     jax.ShapeDtypeStruct((1, d), f32), jax.ShapeDtypeStruct((1, d), f32),
                                     jax.ShapeDtypeStruct((1, 1), f32)],
                          scratch_shapes=[pltpu.VMEM((1, d), f32)],
                          compiler_params=_params("arbitrary"))(z, target, g, b)


def merge_fwd(gates, ys, yg, *, name):
    s, d = ys.shape
    tm = _tile(s, 256, 16)

    def body(gt_ref, ys_ref, yg_ref, o_ref):
        o_ref[...] = (_sigmoid(gt_ref[:, :d].astype(f32)) * ys_ref[...].astype(f32)
                      + _sigmoid(gt_ref[:, d:].astype(f32)) * yg_ref[...].astype(f32)).astype(bf16)

    row = pl.BlockSpec((tm, d), lambda i: (i, 0))
    return pl.pallas_call(body, name=name, grid=(s // tm,), in_specs=[pl.BlockSpec((tm, 2 * d), lambda i: (i, 0)), row, row],
                          out_specs=row, out_shape=jax.ShapeDtypeStruct((s, d), bf16),
                          compiler_params=_params("parallel"))(gates, ys, yg)


def merge_bwd(gates, ys, yg, dmix, *, name):
    s, d = ys.shape
    tm = _tile(s, 256, 16)

    def body(gt_ref, ys_ref, yg_ref, dm_ref, dys_ref, dyg_ref, dgt_ref):
        dm = dm_ref[...]
        sa, sb = _sigmoid(gt_ref[:, :d].astype(f32)), _sigmoid(gt_ref[:, d:].astype(f32))
        dys_ref[...] = (dm * sa).astype(bf16)
        dyg_ref[...] = (dm * sb).astype(bf16)
        dgt_ref[:, :d] = (dm * ys_ref[...].astype(f32) * sa * (1.0 - sa)).astype(bf16)
        dgt_ref[:, d:] = (dm * yg_ref[...].astype(f32) * sb * (1.0 - sb)).astype(bf16)

    row = pl.BlockSpec((tm, d), lambda i: (i, 0))
    wide = pl.BlockSpec((tm, 2 * d), lambda i: (i, 0))
    return pl.pallas_call(body, name=name, grid=(s // tm,), in_specs=[wide, row, row, row], out_specs=[row, row, wide],
                          out_shape=[jax.ShapeDtypeStruct((s, d), bf16), jax.ShapeDtypeStruct((s, d), bf16),
                                     jax.ShapeDtypeStruct((s, 2 * d), bf16)],
                          compiler_params=_params("parallel"))(gates, ys, yg, dmix)


def _shift_down(ext, j):
    return ext if j == 0 else pltpu.roll(ext, j, 0)


def _shift_up(ext, j):
    return ext if j == 0 else pltpu.roll(ext, ext.shape[0] - j, 0)


def _conv_taps(ext, width):
    return [_shift_down(ext, width - 1 - j)[HALO:] for j in range(width)]


def _causal_conv(taps, w_ref):
    acc = None
    for j, tap in enumerate(taps):
        term = w_ref[j:j + 1, :] * tap
        acc = term if acc is None else acc + term
    return acc


def _conv_grads(dy_ext, x, w_ref, width, rows):
    ahead = [_shift_up(dy_ext, width - 1 - j)[:rows] for j in range(width)]
    dx = None
    for j in range(width):
        term = w_ref[j:j + 1, :] * ahead[j]
        dx = term if dx is None else dx + term
    return dx, [jnp.sum(x * ahead[j], axis=0, keepdims=True) for j in range(width)]


def _rows_to_block(rows, n_rows, cols):
    r = lax.broadcasted_iota(jnp.int32, (n_rows, cols), 0)
    out = jnp.zeros((n_rows, cols), f32)
    for j, v in enumerate(rows):
        out = out + jnp.where(r == j, v, 0.0)
    return out


def _silu_and_grad(v):
    sg = _sigmoid(v)
    return v * sg, sg * (1.0 + v * (1.0 - sg))


HALO_BF16 = 16


def ffn_act_fwd(hpre, cwb, *, name):
    s = hpre.shape[0]
    tm = _tile(s, 256, 16)
    hb = tm // HALO_BF16

    def body(hg_ref, hgp_ref, hu_ref, hup_ref, cg_ref, cu_ref, o_ref, g_ref, u_ref):
        first = pl.program_id(1) == 0

        def conv(h_ref, hp_ref, c_ref):
            prev = jnp.where(first, 0.0, hp_ref[...].astype(f32)[HALO_BF16 - HALO:])
            ext = jnp.concatenate([prev, h_ref[...].astype(f32)], axis=0)
            return _causal_conv(_conv_taps(ext, 3), c_ref.at[0]) + c_ref[0, 3:4, :]

        g = conv(hg_ref, hgp_ref, cg_ref)
        u = conv(hu_ref, hup_ref, cu_ref)
        g_ref[...] = g.astype(bf16)
        u_ref[...] = u.astype(bf16)
        o_ref[...] = (g * _sigmoid(g) * u).astype(bf16)

    def tile(off):
        return pl.BlockSpec((tm, FF_PAD), lambda j, i: (i, j + off))

    def halo(off):
        return pl.BlockSpec((HALO_BF16, FF_PAD), lambda j, i: (jnp.maximum(i * hb - 1, 0), j + off))

    def taps(off):
        return pl.BlockSpec((1, 8, FF_PAD), lambda j, i: (j + off, 0, 0))

    out = pl.BlockSpec((tm, FF_PAD), lambda j, i: (i, j))
    return pl.pallas_call(body, name=name, grid=(4, s // tm),
                          in_specs=[tile(0), halo(0), tile(4), halo(4), taps(0), taps(4)], out_specs=[out, out, out],
                          out_shape=[jax.ShapeDtypeStruct((s, 4 * FF_PAD), bf16)] * 3,
                          compiler_params=_params("parallel", "parallel"))(hpre, hpre, hpre, hpre, cwb, cwb)


def ffn_act_bwd(hpre, conv_g, conv_u, dact, cwb, *, name):
    s = hpre.shape[0]
    tm = _tile(s, 256, 16)
    hb = tm // HALO_BF16
    nt = s // tm
    last_hb = s // HALO_BF16 - 1

    def body(hg_ref, hu_ref, g_ref, gn_ref, u_ref, un_ref, d_ref, dn_ref, cg_ref, cu_ref, dh_ref, dcg_ref, dcu_ref,
             buf_ref, sems):
        j, i = pl.program_id(0), pl.program_id(1)
        step = j * nt + i
        slot = step % 2

        def writes(from_slot):
            rows = pl.ds(pl.multiple_of(i * tm, tm), tm)
            return [pltpu.make_async_copy(buf_ref.at[from_slot, half],
                                          dh_ref.at[rows, pl.ds(pl.multiple_of((j + 4 * half) * FF_PAD, 128), FF_PAD)],
                                          sems.at[from_slot, half]) for half in (0, 1)]

        @pl.when(i == 0)
        def _():
            dcg_ref[...] = jnp.zeros_like(dcg_ref)
            dcu_ref[...] = jnp.zeros_like(dcu_ref)

        @pl.when(step >= 2)
        def _():
            for cp in writes(slot):
                cp.wait()

        def with_future(t_ref, n_ref):
            return jnp.concatenate([t_ref[...].astype(f32), n_ref[...].astype(f32)[:HALO]], axis=0)

        g, u = with_future(g_ref, gn_ref), with_future(u_ref, un_ref)
        d = jnp.concatenate([d_ref[...].astype(f32), jnp.where(i == nt - 1, 0.0, dn_ref[...].astype(f32)[:HALO])], axis=0)
        act, dact_dg = _silu_and_grad(g)
        dg = d * u * dact_dg
        du = d * act
        dhg, dwg = _conv_grads(dg, hg_ref[...].astype(f32), cg_ref.at[0], 3, tm)
        dhu, dwu = _conv_grads(du, hu_ref[...].astype(f32), cu_ref.at[0], 3, tm)
        buf_ref[slot, 0] = dhg.astype(bf16)
        buf_ref[slot, 1] = dhu.astype(bf16)
        for cp in writes(slot):
            cp.start()
        dcg_ref[0] += _rows_to_block(dwg + [jnp.sum(dg[:tm], axis=0, keepdims=True)], 8, FF_PAD)
        dcu_ref[0] += _rows_to_block(dwu + [jnp.sum(du[:tm], axis=0, keepdims=True)], 8, FF_PAD)

        @pl.when(step == 4 * nt - 1)
        def _():
            for cp in writes(slot) + writes(1 - slot):
                cp.wait()

    def tile(off):
        return pl.BlockSpec((tm, FF_PAD), lambda j, i: (i, j + off))

    nxt = pl.BlockSpec((HALO_BF16, FF_PAD), lambda j, i: (jnp.minimum((i + 1) * hb, last_hb), j))
    taps = [pl.BlockSpec((1, 8, FF_PAD), lambda j, i, off=off: (j + off, 0, 0)) for off in (0, 4)]
    dh, dcg, dcu = pl.pallas_call(
        body, name=name, grid=(4, nt),
        in_specs=[tile(0), tile(4), tile(0), nxt, tile(0), nxt, tile(0), nxt] + taps,
        out_specs=[ANY, taps[0], taps[0]],
        out_shape=[jax.ShapeDtypeStruct((s, 8 * FF_PAD), bf16),
                   jax.ShapeDtypeStruct((4, 8, FF_PAD), f32), jax.ShapeDtypeStruct((4, 8, FF_PAD), f32)],
        scratch_shapes=[pltpu.VMEM((2, 2, tm, FF_PAD), bf16), pltpu.SemaphoreType.DMA((2, 2))],
        compiler_params=_params("arbitrary", "arbitrary"),
    )(hpre, hpre, conv_g, conv_g, conv_u, conv_u, dact, dact, cwb, cwb)
    return dh, jnp.concatenate([dcg, dcu], axis=0)


MEM_SCALE = MEM_HEAD_DIM ** -0.5


def _softmax_rows(sc):
    m = jnp.max(sc, axis=-1, keepdims=True)
    e = jnp.exp(sc - m)
    return e / jnp.sum(e, axis=-1, keepdims=True)


def memattn_fwd(qm, kv, *, name):
    s = qm.shape[0]
    mlen = kv.shape[0]
    tm = _tile(s, 512, 16)

    def body(q_ref, kv_ref, o_ref):
        for h in range(MEM_HEADS):
            lo = h * MEM_HEAD_DIM
            q = q_ref[:, lo:lo + MEM_HEAD_DIM]
            k = kv_ref[:, lo:lo + MEM_HEAD_DIM]
            v = kv_ref[:, MEM_W + lo:MEM_W + lo + MEM_HEAD_DIM]
            p = _softmax_rows(_d16(q, k, NT) * MEM_SCALE)
            o_ref[:, lo:lo + MEM_HEAD_DIM] = _d16(p, v, NN).astype(bf16)

    return pl.pallas_call(body, name=name, grid=(s // tm,),
                          in_specs=[pl.BlockSpec((tm, MEM_W), lambda i: (i, 0)), pl.BlockSpec((mlen, 2 * MEM_W), lambda i: (0, 0))],
                          out_specs=pl.BlockSpec((tm, MEM_W), lambda i: (i, 0)),
                          out_shape=jax.ShapeDtypeStruct((s, MEM_W), bf16), compiler_params=_params("parallel"))(qm, kv)


def memattn_bwd(qm, kv, dout, *, name):
    s = qm.shape[0]
    mlen = kv.shape[0]
    tm = _tile(s, 512, 16)

    def body(q_ref, kv_ref, do_ref, dq_ref, dkv_ref):
        @pl.when(pl.program_id(0) == 0)
        def _():
            dkv_ref[...] = jnp.zeros_like(dkv_ref)

        for h in range(MEM_HEADS):
            lo = h * MEM_HEAD_DIM
            q = q_ref[:, lo:lo + MEM_HEAD_DIM]
            k = kv_ref[:, lo:lo + MEM_HEAD_DIM]
            v = kv_ref[:, MEM_W + lo:MEM_W + lo + MEM_HEAD_DIM]
            do = do_ref[:, lo:lo + MEM_HEAD_DIM]
            p = _softmax_rows(_d16(q, k, NT) * MEM_SCALE)
            dp = _d16(do, v, NT)
            ds = p * (dp - jnp.sum(p * dp, axis=-1, keepdims=True)) * MEM_SCALE
            dq_ref[:, lo:lo + MEM_HEAD_DIM] = _d16(ds, k, NN).astype(bf16)
            dkv_ref[:, lo:lo + MEM_HEAD_DIM] += _d16(ds, q, TN)
            dkv_ref[:, MEM_W + lo:MEM_W + lo + MEM_HEAD_DIM] += _d16(p, do, TN)

    row = pl.BlockSpec((tm, MEM_W), lambda i: (i, 0))
    full = pl.BlockSpec((mlen, 2 * MEM_W), lambda i: (0, 0))
    return pl.pallas_call(body, name=name, grid=(s // tm,), in_specs=[row, full, row], out_specs=[row, full],
                          out_shape=[jax.ShapeDtypeStruct((s, MEM_W), bf16), jax.ShapeDtypeStruct((mlen, 2 * MEM_W), f32)],
                          compiler_params=_params("arbitrary"))(qm, kv, dout)


SWA_SCALE = SWA_HEAD_DIM ** -0.5
SWA_GROUP = SWA_HEADS // SWA_KV_HEADS
SWA_IN_W = 1408
K_COL, V_COL, BA_COL = SWA_Q // 128, SWA_Q // 128 + 1, SWA_Q // 128 + 2


def _swa_mask(n):
    qi = lax.broadcasted_iota(jnp.int32, (BLOCK, 2 * BLOCK), 0)
    kj = lax.broadcasted_iota(jnp.int32, (BLOCK, 2 * BLOCK), 1)
    dist = qi + BLOCK - kj
    return (dist >= 0) & (dist < BLOCK) & ((n > 0) | (kj >= BLOCK))


def _swa_probs(q, k, bias, sink, mask):
    heads = range(len(q))
    sc = [jnp.where(mask, _d16(q[h], k[h], NT) * SWA_SCALE + bias[h], NEG_INF) for h in heads]
    m = [jnp.maximum(jnp.max(sc[h], axis=-1, keepdims=True), sink[h]) for h in heads]
    e = [jnp.exp(sc[h] - m[h]) for h in heads]
    es = [jnp.exp(sink[h] - m[h]) for h in heads]
    inv = [1.0 / (jnp.sum(e[h], axis=-1, keepdims=True) + es[h]) for h in heads]
    return e, es, inv


def _swa_heads(ref):
    return [ref[:, h * SWA_HEAD_DIM:(h + 1) * SWA_HEAD_DIM] for h in range(SWA_HEADS)]


def _swa_kv_of_heads(band):
    kv = [band[:, g * SWA_HEAD_DIM:(g + 1) * SWA_HEAD_DIM] for g in range(SWA_KV_HEADS)]
    return [kv[h // SWA_GROUP] for h in range(SWA_HEADS)]


def _swa_specs():
    q_spec = pl.BlockSpec((BLOCK, SWA_Q), lambda n: (n, 0))

    def band(col):
        return [pl.BlockSpec((BLOCK, SWA_KV), lambda n: (jnp.maximum(n - 1, 0), col)),
                pl.BlockSpec((BLOCK, SWA_KV), lambda n: (n, col))]

    bias_spec = pl.BlockSpec((SWA_HEADS, BLOCK, 2 * BLOCK), lambda n: (0, 0, 0))
    sink_spec = pl.BlockSpec((1, SWA_HEADS), lambda n: (0, 0))
    return [q_spec] + band(K_COL) + band(V_COL) + [bias_spec, sink_spec]


def swa_fwd(swa_in, bias, sinks, *, name, side=None):
    s = swa_in.shape[0]

    def body(q_ref, kp_ref, kc_ref, vp_ref, vc_ref, bias_ref, sink_ref, o_ref):
        mask = _swa_mask(pl.program_id(0))
        kb = jnp.concatenate([kp_ref[...], kc_ref[...]], axis=0)
        vb = jnp.concatenate([vp_ref[...], vc_ref[...]], axis=0)
        heads = range(SWA_HEADS)
        k, v = _swa_kv_of_heads(kb), _swa_kv_of_heads(vb)
        e, _, inv = _swa_probs(_swa_heads(q_ref), k, [bias_ref[h] for h in heads], [sink_ref[:, h:h + 1] for h in heads], mask)
        outs = [_d16(e[h] * inv[h], v[h], NN) for h in heads]
        for h in heads:
            o_ref[:, h * SWA_HEAD_DIM:(h + 1) * SWA_HEAD_DIM] = outs[h].astype(bf16)

    return _call(body, (swa_in, swa_in, swa_in, swa_in, swa_in, bias, sinks), name=name, grid=(s // BLOCK,),
                 in_specs=_swa_specs(), out_specs=pl.BlockSpec((BLOCK, SWA_Q), lambda n: (n, 0)),
                 out_shape=jax.ShapeDtypeStruct((s, SWA_Q), bf16), semantics=("parallel",), side=side)


def swa_bwd(swa_in, bias, sinks, dout, *, name, side=None):
    s = swa_in.shape[0]

    def body(q_ref, kp_ref, kc_ref, vp_ref, vc_ref, bias_ref, sink_ref, do_ref,
             dq_ref, dkc_ref, dkp_ref, dvc_ref, dvp_ref, dbias_ref, dsink_ref):
        n = pl.program_id(0)

        @pl.when(n == 0)
        def _():
            dbias_ref[...] = jnp.zeros_like(dbias_ref)
            dsink_ref[...] = jnp.zeros_like(dsink_ref)

        mask = _swa_mask(n)
        kb = jnp.concatenate([kp_ref[...], kc_ref[...]], axis=0)
        vb = jnp.concatenate([vp_ref[...], vc_ref[...]], axis=0)
        lane = lax.broadcasted_iota(jnp.int32, (1, 128), 1)
        hs = range(SWA_HEADS)
        q, do = _swa_heads(q_ref), _swa_heads(do_ref)
        k, v = _swa_kv_of_heads(kb), _swa_kv_of_heads(vb)
        e, es, inv = _swa_probs(q, k, [bias_ref[h] for h in hs], [sink_ref[:, h:h + 1] for h in hs], mask)
        p = [e[h] * inv[h] for h in hs]
        dp = [_d16(do[h], v[h], NT) for h in hs]
        delta = [jnp.sum(p[h] * dp[h], axis=-1, keepdims=True) for h in hs]
        ds = [p[h] * (dp[h] - delta[h]) for h in hs]
        dss = [ds[h] * SWA_SCALE for h in hs]
        dq = [_d16(dss[h], k[h], NN) for h in hs]
        dks = [_d16(dss[h], q[h], TN) for h in hs]
        dvs = [_d16(p[h], do[h], TN) for h in hs]
        dsink = jnp.zeros((1, 128), f32)
        for h in hs:
            dbias_ref[h] += ds[h]
            dq_ref[:, h * SWA_HEAD_DIM:(h + 1) * SWA_HEAD_DIM] = dq[h]
            dsink = dsink + jnp.where(lane == h, -jnp.sum(es[h] * inv[h] * delta[h], axis=0, keepdims=True), 0.0)
        for g in range(SWA_KV_HEADS):
            kl = g * SWA_HEAD_DIM
            group = range(g * SWA_GROUP, (g + 1) * SWA_GROUP)
            dk = functools.reduce(lambda a, b: a + b, [dks[h] for h in group])
            dv = functools.reduce(lambda a, b: a + b, [dvs[h] for h in group])
            dkp_ref[:, kl:kl + SWA_HEAD_DIM] = dk[:BLOCK]
            dkc_ref[:, kl:kl + SWA_HEAD_DIM] = dk[BLOCK:]
            dvp_ref[:, kl:kl + SWA_HEAD_DIM] = dv[:BLOCK]
            dvc_ref[:, kl:kl + SWA_HEAD_DIM] = dv[BLOCK:]
        dsink_ref[...] += dsink

    qs = pl.BlockSpec((BLOCK, SWA_Q), lambda n: (n, 0))
    ks = pl.BlockSpec((BLOCK, SWA_KV), lambda n: (n, 0))
    return _call(
        body, (swa_in, swa_in, swa_in, swa_in, swa_in, bias, sinks, dout), name=name, grid=(s // BLOCK,),
        in_specs=_swa_specs() + [qs],
        out_specs=[qs, ks, ks, ks, ks, pl.BlockSpec((SWA_HEADS, BLOCK, 2 * BLOCK), lambda n: (0, 0, 0)),
                   pl.BlockSpec((1, 128), lambda n: (0, 0))],
        out_shape=[jax.ShapeDtypeStruct((s, SWA_Q), f32)] + [jax.ShapeDtypeStruct((s, SWA_KV), f32)] * 4
        + [jax.ShapeDtypeStruct((SWA_HEADS, BLOCK, 2 * BLOCK), f32), jax.ShapeDtypeStruct((1, 128), f32)],
        semantics=("arbitrary",), side=side)


def swa_in_grad(dq, dkc, dkp, dvc, dvp, dba, *, name):
    s = dq.shape[0]
    nb = s // BLOCK

    def body(dq_ref, dkc_ref, dkp_ref, dvc_ref, dvp_ref, dba_ref, o_ref):
        has_next = pl.program_id(0) < nb - 1
        o_ref[:, :SWA_Q] = dq_ref[...].astype(bf16)
        o_ref[:, SWA_Q:SWA_Q + SWA_KV] = (dkc_ref[...] + jnp.where(has_next, dkp_ref[...], 0.0)).astype(bf16)
        o_ref[:, SWA_Q + SWA_KV:SWA_Q + 2 * SWA_KV] = (dvc_ref[...] + jnp.where(has_next, dvp_ref[...], 0.0)).astype(bf16)
        o_ref[:, SWA_Q + 2 * SWA_KV:] = dba_ref[...].astype(bf16)

    cur = pl.BlockSpec((BLOCK, SWA_KV), lambda n: (n, 0))
    nxt = pl.BlockSpec((BLOCK, SWA_KV), lambda n: (jnp.minimum(n + 1, nb - 1), 0))
    return pl.pallas_call(body, name=name, grid=(nb,),
                          in_specs=[pl.BlockSpec((BLOCK, SWA_Q), lambda n: (n, 0)), cur, nxt, cur, nxt, cur],
                          out_specs=pl.BlockSpec((BLOCK, SWA_IN_W), lambda n: (n, 0)),
                          out_shape=jax.ShapeDtypeStruct((s, SWA_IN_W), bf16),
                          compiler_params=_params("parallel"))(dq, dkc, dkp, dvc, dvp, dba)


def _bucket_onehot():
    qi = jnp.arange(BLOCK)[:, None]
    kj = jnp.arange(2 * BLOCK)[None, :]
    dist = jnp.maximum(qi + BLOCK - kj, 0)
    max_exact = REL_BUCKETS // 2
    dd = jnp.maximum(dist, 1).astype(f32)
    large = max_exact + (jnp.log(dd / max_exact) / math.log(REL_MAX_DIST / max_exact) * (REL_BUCKETS - max_exact)).astype(jnp.int32)
    bucket = jnp.where(dist < max_exact, dist, jnp.minimum(large, REL_BUCKETS - 1)).reshape(-1)
    return (bucket[None, :] == jnp.arange(REL_BUCKETS)[:, None]).astype(f32)


def _gbeta_fn(ba, alog_row, dt_row):
    col = lax.broadcasted_iota(jnp.int32, ba.shape, 1)
    v = ba + dt_row
    softplus = jnp.maximum(v, 0.0) + jnp.log(1.0 + jnp.exp(-jnp.abs(v)))
    g = -jnp.exp(alog_row) * softplus
    return jnp.where(col < GDN_HEADS, _sigmoid(ba), jnp.where(col < 2 * GDN_HEADS, g, 0.0))


def gbeta_fwd(swa_in, alog_row, dt_row, *, name):
    s = swa_in.shape[0]
    tm = _tile(s, 512, 8)

    def body(ba_ref, a_ref, d_ref, o_ref):
        o_ref[...] = _gbeta_fn(ba_ref[...], a_ref[...], d_ref[...])

    vec = pl.BlockSpec((1, 128), lambda i: (0, 0))
    return pl.pallas_call(body, name=name, grid=(s // tm,), in_specs=[pl.BlockSpec((tm, 128), lambda i: (i, BA_COL)), vec, vec],
                          out_specs=pl.BlockSpec((tm, 128), lambda i: (i, 0)), out_shape=jax.ShapeDtypeStruct((s, 128), f32),
                          compiler_params=_params("parallel"))(swa_in, alog_row, dt_row)


def gbeta_bwd(swa_in, alog_row, dt_row, dgbeta, *, name):
    s = swa_in.shape[0]
    tm = _tile(s, 512, 8)

    def body(ba_ref, a_ref, d_ref, dgb_ref, dba_ref, da_ref, dd_ref):
        @pl.when(pl.program_id(0) == 0)
        def _():
            da_ref[...] = jnp.zeros_like(da_ref)
            dd_ref[...] = jnp.zeros_like(dd_ref)

        _, pull = jax.vjp(_gbeta_fn, ba_ref[...], a_ref[...], d_ref[...])
        dba, da, dd = pull(dgb_ref[...])
        dba_ref[...] = dba
        da_ref[...] += da
        dd_ref[...] += dd

    vec = pl.BlockSpec((1, 128), lambda i: (0, 0))
    row = pl.BlockSpec((tm, 128), lambda i: (i, 0))
    return pl.pallas_call(body, name=name, grid=(s // tm,),
                          in_specs=[pl.BlockSpec((tm, 128), lambda i: (i, BA_COL)), vec, vec, row], out_specs=[row, vec, vec],
                          out_shape=[jax.ShapeDtypeStruct((s, 128), f32), jax.ShapeDtypeStruct((1, 128), f32),
                                     jax.ShapeDtypeStruct((1, 128), f32)],
                          compiler_params=_params("arbitrary"))(swa_in, alog_row, dt_row, dgbeta)


QKV_W = 3 * GDN_W


def gdn_pre_fwd(gdn_in, convw, *, name):
    s = gdn_in.shape[0]
    tm = _tile(s, 256, 16)
    hb = tm // HALO

    def body(x_ref, xp_ref, w_ref, q_ref, k_ref, v_ref, pre_ref):
        prev = jnp.where(pl.program_id(0) == 0, 0.0, xp_ref[...])
        pre = _causal_conv(_conv_taps(jnp.concatenate([prev, x_ref[...]], axis=0), GDN_CONV), w_ref)
        pre_ref[...] = pre
        act = pre * _sigmoid(pre)
        for h in range(GDN_HEADS):
            lo = h * GDN_HEAD_DIM
            for off, o_ref in ((0, q_ref), (GDN_W, k_ref)):
                seg = act[:, off + lo:off + lo + GDN_HEAD_DIM]
                o_ref[:, lo:lo + GDN_HEAD_DIM] = seg * lax.rsqrt(jnp.sum(seg * seg, axis=-1, keepdims=True) + 1e-6)
        v_ref[...] = act[:, 2 * GDN_W:]

    out = pl.BlockSpec((tm, GDN_W), lambda i: (i, 0))
    return pl.pallas_call(body, name=name, grid=(s // tm,),
                          in_specs=[pl.BlockSpec((tm, QKV_W), lambda i: (i, 0)),
                                    pl.BlockSpec((HALO, QKV_W), lambda i: (jnp.maximum(i * hb - 1, 0), 0)),
                                    pl.BlockSpec((8, QKV_W), lambda i: (0, 0))],
                          out_specs=[out, out, out, pl.BlockSpec((tm, QKV_W), lambda i: (i, 0))],
                          out_shape=[jax.ShapeDtypeStruct((s, GDN_W), f32)] * 3 + [jax.ShapeDtypeStruct((s, QKV_W), f32)],
                          compiler_params=_params("parallel"))(gdn_in, gdn_in, convw)


def gdn_pre_bwd(gdn_in, conv_out, convw, dqn, dkn, dv, dgz, *, name):
    s = gdn_in.shape[0]
    tm = _tile(s, 128, 16)
    hb = tm // HALO
    nt = s // tm
    last_hb = s // HALO - 1

    def body(x_ref, pre_ref, pren_ref, w_ref, dq_ref, dqx_ref, dk_ref, dkx_ref, dv_ref, dvx_ref, dz_ref, o_ref, dw_ref):
        i = pl.program_id(0)
        last = i == nt - 1

        @pl.when(i == 0)
        def _():
            dw_ref[...] = jnp.zeros_like(dw_ref)

        pre = jnp.concatenate([pre_ref[...], pren_ref[...]], axis=0)
        act, dact_dpre = _silu_and_grad(pre)

        def with_future(t_ref, n_ref):
            return jnp.concatenate([t_ref[...], jnp.where(last, 0.0, n_ref[...])], axis=0)

        dqe, dke, dve = with_future(dq_ref, dqx_ref), with_future(dk_ref, dkx_ref), with_future(dv_ref, dvx_ref)
        parts = []
        for off, dn in ((0, dqe), (GDN_W, dke)):
            for h in range(GDN_HEADS):
                lo = h * GDN_HEAD_DIM
                seg = act[:, off + lo:off + lo + GDN_HEAD_DIM]
                r = lax.rsqrt(jnp.sum(seg * seg, axis=-1, keepdims=True) + 1e-6)
                nrm = seg * r
                dseg = dn[:, lo:lo + GDN_HEAD_DIM]
                parts.append(r * (dseg - nrm * jnp.sum(dseg * nrm, axis=-1, keepdims=True)))
        dpre = jnp.concatenate(parts + [dve], axis=1) * dact_dpre
        dx, dw = _conv_grads(dpre, x_ref[...], w_ref, GDN_CONV, tm)
        o_ref[:, :QKV_W] = dx.astype(bf16)
        o_ref[:, QKV_W:] = dz_ref[...].astype(bf16)
        dw_ref[...] += _rows_to_block(dw, 8, QKV_W)

    row = pl.BlockSpec((tm, GDN_W), lambda i: (i, 0))
    fut = pl.BlockSpec((HALO, GDN_W), lambda i: (jnp.minimum((i + 1) * hb, last_hb), 0))
    wide = pl.BlockSpec((tm, QKV_W), lambda i: (i, 0))
    return pl.pallas_call(
        body, name=name, grid=(nt,),
        in_specs=[wide, wide, pl.BlockSpec((HALO, QKV_W), lambda i: (jnp.minimum((i + 1) * hb, last_hb), 0)),
                  pl.BlockSpec((8, QKV_W), lambda i: (0, 0)), row, fut, row, fut, row, fut, row],
        out_specs=[pl.BlockSpec((tm, 4 * GDN_W), lambda i: (i, 0)), pl.BlockSpec((8, QKV_W), lambda i: (0, 0))],
        out_shape=[jax.ShapeDtypeStruct((s, 4 * GDN_W), bf16), jax.ShapeDtypeStruct((8, QKV_W), f32)],
        compiler_params=_params("arbitrary"),
    )(gdn_in, conv_out, conv_out, convw, dqn, dqn, dkn, dkn, dv, dv, dgz)


def _gdn_post_head(o, z, nw):
    return o * lax.rsqrt(jnp.mean(o * o, axis=-1, keepdims=True) + 1e-6) * nw * (z * _sigmoid(z))


def gdn_post_fwd(o, gdn_in, nw, *, name):
    s = o.shape[0]
    tm = _tile(s, 256, 16)

    def body(o_ref, z_ref, nw_ref, y_ref):
        for h in range(GDN_HEADS):
            sl = slice(h * GDN_HEAD_DIM, (h + 1) * GDN_HEAD_DIM)
            y_ref[:, sl] = _gdn_post_head(o_ref[:, sl], z_ref[:, sl], nw_ref[...]).astype(bf16)

    row = pl.BlockSpec((tm, GDN_W), lambda i: (i, 0))
    return pl.pallas_call(body, name=name, grid=(s // tm,),
                          in_specs=[row, pl.BlockSpec((tm, GDN_W), lambda i: (i, 3)), pl.BlockSpec((1, 128), lambda i: (0, 0))],
                          out_specs=row, out_shape=jax.ShapeDtypeStruct((s, GDN_W), bf16),
                          compiler_params=_params("parallel"))(o, gdn_in, nw)


def gdn_post_bwd(o, gdn_in, nw, dy, *, name):
    s = o.shape[0]
    tm = _tile(s, 256, 16)

    def body(o_ref, z_ref, nw_ref, dy_ref, do_ref, dz_ref, dnw_ref):
        @pl.when(pl.program_id(0) == 0)
        def _():
            dnw_ref[...] = jnp.zeros_like(dnw_ref)

        dnw = jnp.zeros((1, 128), f32)
        for h in range(GDN_HEADS):
            sl = slice(h * GDN_HEAD_DIM, (h + 1) * GDN_HEAD_DIM)
            _, pull = jax.vjp(_gdn_post_head, o_ref[:, sl], z_ref[:, sl], nw_ref[...])
            do, dz, dn = pull(dy_ref[:, sl])
            do_ref[:, sl] = do
            dz_ref[:, sl] = dz
            dnw = dnw + dn
        dnw_ref[...] += dnw

    row = pl.BlockSpec((tm, GDN_W), lambda i: (i, 0))
    vec = pl.BlockSpec((1, 128), lambda i: (0, 0))
    return pl.pallas_call(body, name=name, grid=(s // tm,),
                          in_specs=[row, pl.BlockSpec((tm, GDN_W), lambda i: (i, 3)), vec, row], out_specs=[row, row, vec],
                          out_shape=[jax.ShapeDtypeStruct((s, GDN_W), f32), jax.ShapeDtypeStruct((s, GDN_W), f32),
                                     jax.ShapeDtypeStruct((1, 128), f32)],
                          compiler_params=_params("arbitrary"))(o, gdn_in, nw, dy)


def _dot_high(a, b, dims=NN):
    return lax.dot_general(a, b, (dims, ((), ())), precision=lax.Precision.HIGH, preferred_element_type=f32)


@jax.custom_vjp
def _unit_lower_inverses(a):
    c = a[0].shape[0]
    n = range(len(a))
    eye = (lax.broadcasted_iota(jnp.int32, (c, c), 0) == lax.broadcasted_iota(jnp.int32, (c, c), 1)).astype(f32)
    inv = [eye - a[i] for i in n]
    pw = [_dot_high(a[i], a[i]) for i in n]
    width = 2
    while width < c:
        inv = [inv[i] + _dot_high(inv[i], pw[i]) for i in n]
        width *= 2
        if width < c:
            pw = [_dot_high(pw[i], pw[i]) for i in n]
    return inv


def _unit_lower_inverses_fwd(a):
    inv = _unit_lower_inverses(a)
    return inv, inv


def _unit_lower_inverses_bwd(inv, g):
    return ([-_dot_high(_dot_high(x, gx, TN), x, NT) for x, gx in zip(inv, g)],)


_unit_lower_inverses.defvjp(_unit_lower_inverses_fwd, _unit_lower_inverses_bwd)


@jax.custom_vjp
def _known_inverses(a, inv):
    return inv


_known_inverses.defvjp(lambda a, inv: (inv, inv),
                       lambda inv, g: (_unit_lower_inverses_bwd(inv, g)[0], [jnp.zeros_like(x) for x in inv]))


def _gdn_chunks(q, k, v, gb, state, kept_inverses=None):
    c = GDN_CHUNK
    heads = range(len(q))
    r = lax.broadcasted_iota(jnp.int32, (c, c), 0)
    cc = lax.broadcasted_iota(jnp.int32, (c, c), 1)
    tril, strict = r >= cc, r > cc
    eye = (r == cc).astype(f32)

    def dhi(a, b):
        return jnp.dot(a, b, precision=lax.Precision.HIGH, preferred_element_type=f32)

    beta = [gb[:, h:h + 1] for h in heads]
    cum_cols = dhi(tril.astype(f32), gb)
    cum_rows = dhi(gb.T, (r <= cc).astype(f32))
    gi = [jnp.broadcast_to(cum_cols[:, GDN_HEADS + h:GDN_HEADS + h + 1], (c, c)) for h in heads]
    gj = [jnp.broadcast_to(cum_rows[GDN_HEADS + h:GDN_HEADS + h + 1, :], (c, c)) for h in heads]
    decay = [jnp.where(tril, jnp.exp(jnp.where(tril, gi[h] - gj[h], 0.0)), 0.0) for h in heads]
    kb = [k[h] * beta[h] for h in heads]
    vb = [v[h] * beta[h] for h in heads]
    a = [jnp.where(strict, _d16(kb[h], k[h], NT) * decay[h], 0.0) for h in heads]
    tinv = _unit_lower_inverses(a) if kept_inverses is None else _known_inverses(a, kept_inverses)
    gc = [gi[h][:, 0:1] for h in heads]
    egc = [jnp.exp(gc[h]) for h in heads]
    u = [dhi(tinv[h], vb[h]) for h in heads]
    w = [dhi(tinv[h], kb[h] * egc[h]) for h in heads]
    qs = [q[h] * (GDN_HEAD_DIM ** -0.5) for h in heads]
    attn = [jnp.where(tril, _d16(qs[h], k[h], NT) * decay[h], 0.0) for h in heads]
    g_last = [gi[h][c - 1:c, 0:1] for h in heads]
    v_new = [u[h] - _d16(w[h], state[h], NN) for h in heads]
    out = [_d16(qs[h] * egc[h], state[h], NN) + _d16(attn[h], v_new[h], NN) for h in heads]
    new_state = [state[h] * jnp.exp(g_last[h]) + _d16(k[h] * jnp.exp(g_last[h] - gc[h]), v_new[h], TN) for h in heads]
    return out, new_state, tinv


def _head_cols(ref):
    return [ref[:, h * GDN_HEAD_DIM:(h + 1) * GDN_HEAD_DIM] for h in range(GDN_HEADS)]


def gdn_scan_fwd(qn, kn, v, gbeta, *, name, side=None):
    s = qn.shape[0]
    nc = s // GDN_CHUNK

    def body(q_ref, k_ref, v_ref, gb_ref, o_ref, st_ref, inv_ref, state_ref):
        @pl.when(pl.program_id(0) == 0)
        def _():
            state_ref[...] = jnp.zeros_like(state_ref)

        states = [state_ref[h] for h in range(GDN_HEADS)]
        outs, new, inverses = _gdn_chunks(_head_cols(q_ref), _head_cols(k_ref), _head_cols(v_ref), gb_ref[...], states)
        for h in range(GDN_HEADS):
            st_ref[0, h] = states[h]
            inv_ref[0, h] = inverses[h]
            o_ref[:, h * GDN_HEAD_DIM:(h + 1) * GDN_HEAD_DIM] = outs[h]
            state_ref[h] = new[h]

    row = pl.BlockSpec((GDN_CHUNK, GDN_W), lambda n: (n, 0))
    return _call(
        body, (qn, kn, v, gbeta), name=name, grid=(nc,),
        in_specs=[row, row, row, pl.BlockSpec((GDN_CHUNK, 128), lambda n: (n, 0))],
        out_specs=[row, pl.BlockSpec((1, GDN_HEADS, GDN_HEAD_DIM, GDN_HEAD_DIM), lambda n: (n, 0, 0, 0)),
                   pl.BlockSpec((1, GDN_HEADS, GDN_CHUNK, GDN_CHUNK), lambda n: (n, 0, 0, 0))],
        out_shape=[jax.ShapeDtypeStruct((s, GDN_W), f32),
                   jax.ShapeDtypeStruct((nc, GDN_HEADS, GDN_HEAD_DIM, GDN_HEAD_DIM), f32),
                   jax.ShapeDtypeStruct((nc, GDN_HEADS, GDN_CHUNK, GDN_CHUNK), f32)],
        scratch_shapes=[pltpu.VMEM((GDN_HEADS, GDN_HEAD_DIM, GDN_HEAD_DIM), f32)], semantics=("arbitrary",), side=side)


def gdn_scan_bwd(qn, kn, v, gbeta, states, inverses, dout, *, name, side=None):
    s = qn.shape[0]
    nc = s // GDN_CHUNK

    def body(q_ref, k_ref, v_ref, gb_ref, st_ref, inv_ref, do_ref, dq_ref, dk_ref, dv_ref, dgb_ref, dstate_ref):
        @pl.when(pl.program_id(0) == 0)
        def _():
            dstate_ref[...] = jnp.zeros_like(dstate_ref)

        kept = [inv_ref[0, h] for h in range(GDN_HEADS)]
        _, pull = jax.vjp(lambda *args: _gdn_chunks(*args, kept_inverses=kept)[:2],
                          _head_cols(q_ref), _head_cols(k_ref), _head_cols(v_ref), gb_ref[...],
                          [st_ref[0, h] for h in range(GDN_HEADS)])
        dq, dk, dv, dgb, dst = pull((_head_cols(do_ref), [dstate_ref[h] for h in range(GDN_HEADS)]))
        for h in range(GDN_HEADS):
            sl = slice(h * GDN_HEAD_DIM, (h + 1) * GDN_HEAD_DIM)
            dq_ref[:, sl] = dq[h]
            dk_ref[:, sl] = dk[h]
            dv_ref[:, sl] = dv[h]
            dstate_ref[h] = dst[h]
        dgb_ref[...] = dgb

    row = pl.BlockSpec((GDN_CHUNK, GDN_W), lambda n: (nc - 1 - n, 0))
    gb = pl.BlockSpec((GDN_CHUNK, 128), lambda n: (nc - 1 - n, 0))
    return _call(
        body, (qn, kn, v, gbeta, states, inverses, dout), name=name, grid=(nc,),
        in_specs=[row, row, row, gb, pl.BlockSpec((1, GDN_HEADS, GDN_HEAD_DIM, GDN_HEAD_DIM), lambda n: (nc - 1 - n, 0, 0, 0)),
                  pl.BlockSpec((1, GDN_HEADS, GDN_CHUNK, GDN_CHUNK), lambda n: (nc - 1 - n, 0, 0, 0)), row],
        out_specs=[row, row, row, gb],
        out_shape=[jax.ShapeDtypeStruct((s, GDN_W), f32)] * 3 + [jax.ShapeDtypeStruct((s, 128), f32)],
        scratch_shapes=[pltpu.VMEM((GDN_HEADS, GDN_HEAD_DIM, GDN_HEAD_DIM), f32)], semantics=("arbitrary",), side=side)


def adamw(w, g, m, v, *, name):
    r, c = w.shape
    tr = _tile(r, 256, 8)

    def body(w_ref, g_ref, m_ref, v_ref, d_ref, nm_ref, nv_ref):
        gv = g_ref[...]
        nm = ADAM_B1 * m_ref[...] + (1.0 - ADAM_B1) * gv
        nv = ADAM_B2 * v_ref[...] + (1.0 - ADAM_B2) * (gv * gv)
        m_hat = nm / (1.0 - ADAM_B1 ** ADAM_STEP)
        v_hat = nv / (1.0 - ADAM_B2 ** ADAM_STEP)
        d_ref[...] = -ADAM_LR * (m_hat / (jnp.sqrt(v_hat) + ADAM_EPS) + ADAM_WD * w_ref[...])
        nm_ref[...] = nm
        nv_ref[...] = nv

    spec = pl.BlockSpec((tr, c), lambda i: (i, 0))
    return pl.pallas_call(body, name=name, grid=(r // tr,), in_specs=[spec] * 4, out_specs=[spec] * 3,
                          out_shape=[jax.ShapeDtypeStruct((r, c), f32)] * 3, compiler_params=_params("parallel"))(w, g, m, v)


def _pos():
    return lax.axis_index("x"), lax.axis_index("y"), lax.axis_index("c")


ANY = pl.BlockSpec(memory_space=pl.ANY)


class Side(NamedTuple):
    ins: list
    outs: list
    aliases: dict
    sems: list
    start: Callable
    wait: Callable


def join_sides(*sides):
    def spans(key):
        out, off = [], 0
        for sd in sides:
            out.append(slice(off, off + len(getattr(sd, key))))
            off += len(getattr(sd, key))
        return out

    i_sp, o_sp, s_sp = spans("ins"), spans("outs"), spans("sems")
    aliases = {i_sp[n].start + i: o_sp[n].start + o for n, sd in enumerate(sides) for i, o in sd.aliases.items()}

    def each(what):
        def run(ins, outs, sems):
            for n, sd in enumerate(sides):
                getattr(sd, what)(ins[i_sp[n]], outs[o_sp[n]], sems[s_sp[n]])
        return run

    return Side([a for sd in sides for a in sd.ins], [o for sd in sides for o in sd.outs], aliases,
                [s for sd in sides for s in sd.sems], each("start"), each("wait"))


def _side_body(body, side, n_in, n_out, n_scratch, grid):
    ns_in, ns_out = len(side.ins), len(side.outs)

    def wrapped(*refs):
        cut = [n_in, ns_in, n_out, ns_out, n_scratch]
        parts, off = [], 0
        for c in cut:
            parts.append(refs[off:off + c])
            off += c
        ins, s_ins, outs, s_outs, scratch = parts
        sems = refs[off:]
        if grid:
            ids = [pl.program_id(d) for d in range(len(grid))]
            first = functools.reduce(jnp.logical_and, [i == 0 for i in ids])
            last = functools.reduce(jnp.logical_and, [i == g - 1 for i, g in zip(ids, grid)])
            pl.when(first)(lambda: side.start(s_ins, s_outs, sems))
            body(*ins, *outs, *scratch)
            pl.when(last)(lambda: side.wait(s_ins, s_outs, sems))
        else:
            side.start(s_ins, s_outs, sems)
            side.wait(s_ins, s_outs, sems)

    return wrapped


def _call(body, args, *, name, grid, in_specs, out_specs, out_shape, semantics, scratch_shapes=(), side=None):
    if side is None:
        return pl.pallas_call(body, name=name, grid=grid, in_specs=in_specs, out_specs=out_specs, out_shape=out_shape,
                              scratch_shapes=list(scratch_shapes), compiler_params=_params(*semantics))(*args)
    single = not isinstance(out_shape, (list, tuple))
    shapes, specs = ([out_shape], [out_specs]) if single else (list(out_shape), list(out_specs))
    n_in, n_out = len(in_specs), len(shapes)
    res = pl.pallas_call(
        _side_body(body, side, n_in, n_out, len(scratch_shapes), grid), name=name, grid=grid,
        in_specs=list(in_specs) + [ANY] * len(side.ins), out_specs=specs + [ANY] * len(side.outs),
        out_shape=shapes + list(side.outs), scratch_shapes=list(scratch_shapes) + list(side.sems),
        input_output_aliases={n_in + i: n_out + o for i, o in side.aliases.items()},
        compiler_params=_params(*(["arbitrary"] * len(grid))),
    )(*args, *side.ins)
    return (res[0] if single else res[:n_out]), list(res[n_out:])


def run_side(side, *, name):
    return pl.pallas_call(_side_body(None, side, 0, 0, 0, ()), name=name, in_specs=[ANY] * len(side.ins),
                          out_specs=[ANY] * len(side.outs), out_shape=list(side.outs), scratch_shapes=list(side.sems),
                          input_output_aliases=dict(side.aliases))(*side.ins)


def _remote(src, dst, send, recv, k, to):
    return pltpu.make_async_remote_copy(src_ref=src, dst_ref=dst, send_sem=send.at[k], recv_sem=recv.at[k], device_id=to,
                                        device_id_type=MESH)


def gather_first(shards):
    na = len(shards)

    def copies(x_refs, out_refs, sems):
        send, recv, local = sems
        x, y, cc = _pos()
        me = 4 * x + 2 * y + cc
        peers = [(x, y, 1 - cc), (1 - x, y, cc), (x, 1 - y, cc), (1 - x, 1 - y, cc)]
        mine = [pltpu.make_async_copy(x_refs[a], out_refs[a].at[me], local.at[a]) for a in range(na)]
        sent = [_remote(x_refs[a], out_refs[a].at[me], send, recv, 4 * a + k, p) for a in range(na) for k, p in enumerate(peers)]
        landed = [_remote(x_refs[a], out_refs[a].at[4 * p[0] + 2 * p[1] + p[2]], send, recv, 4 * a + k, p)
                  for a in range(na) for k, p in enumerate(peers)]
        return mine, sent, landed

    def start(x_refs, out_refs, sems):
        mine, sent, _ = copies(x_refs, out_refs, sems)
        for cp in mine + sent:
            cp.start()

    def wait(x_refs, out_refs, sems):
        mine, sent, landed = copies(x_refs, out_refs, sems)
        for cp in sent:
            cp.wait_send()
        for cp in landed:
            cp.wait_recv()
        for cp in mine:
            cp.wait()

    return Side(list(shards), [jax.ShapeDtypeStruct((N_DEV,) + s.shape, s.dtype) for s in shards], {},
                [pltpu.SemaphoreType.DMA((4 * na,)), pltpu.SemaphoreType.DMA((4 * na,)), pltpu.SemaphoreType.DMA((na,))],
                start, wait)


def gather_second(slots):
    na = len(slots)

    def copies(out_refs, sems):
        send, recv = sems
        x, y, cc = _pos()
        chips = [(1 - x, y), (x, 1 - y), (1 - x, 1 - y)]
        sent, landed = [], []
        for a in range(na):
            for j, (px, py) in enumerate(chips):
                row = out_refs[a].at[4 * px + 2 * py + cc]
                sent.append(_remote(row, row, send, recv, 3 * a + j, (x, y, 1 - cc)))
                landed.append(_remote(row, out_refs[a].at[4 * px + 2 * py + 1 - cc], send, recv, 3 * a + j, (x, y, 1 - cc)))
        return sent, landed

    def start(_, out_refs, sems):
        for cp in copies(out_refs, sems)[0]:
            cp.start()

    def wait(_, out_refs, sems):
        sent, landed = copies(out_refs, sems)
        for cp in sent:
            cp.wait_send()
        for cp in landed:
            cp.wait_recv()

    return Side(list(slots), [jax.ShapeDtypeStruct(s.shape, s.dtype) for s in slots], {a: a for a in range(na)},
                [pltpu.SemaphoreType.DMA((3 * na,)), pltpu.SemaphoreType.DMA((3 * na,))], start, wait)


def _rows_of(ref, lead, rows):
    if rows is None:
        return ref if lead is None else ref.at[lead]
    cut = pl.ds(rows[0], rows[1])
    return ref.at[:, cut] if lead is None else ref.at[lead, cut]


def _side_into(arrays, out_shapes, into):
    na = len(arrays)
    if into is None:
        return list(arrays), out_shapes, {}
    return list(arrays) + list(into), out_shapes, {na + a: a for a in range(na)}


def grad_to_sibling(chunks, rows=None, into=None):
    na = len(chunks)

    def start(g_refs, out_refs, sems):
        send, recv = sems
        x, y, cc = _pos()
        for a in range(na):
            for q in range(4):
                _remote(_rows_of(g_refs[a], 2 * q + 1 - cc, rows), _rows_of(out_refs[a], q, rows), send, recv, a,
                        (x, y, 1 - cc)).start()

    def wait(g_refs, out_refs, sems):
        send, recv = sems
        x, y, cc = _pos()
        for a in range(na):
            whole = _rows_of(out_refs[a], None, rows)
            _remote(whole, whole, send, recv, a, (x, y, 1 - cc)).wait()

    ins, outs, aliases = _side_into(chunks, [jax.ShapeDtypeStruct((4,) + g.shape[1:], g.dtype) for g in chunks], into)
    return Side(ins, outs, aliases, [pltpu.SemaphoreType.DMA((na,)), pltpu.SemaphoreType.DMA((na,))], start, wait)


def grad_to_chips(parts, rows=None, into=None):
    na = len(parts)

    def copies(p_refs, out_refs, sems):
        send, recv, local = sems
        x, y, cc = _pos()
        chips = [(1 - x, y), (x, 1 - y), (1 - x, 1 - y)]
        mine = [pltpu.make_async_copy(_rows_of(p_refs[a], 2 * x + y, rows), _rows_of(out_refs[a], 3, rows), local.at[a])
                for a in range(na)]
        sent = [_remote(_rows_of(p_refs[a], 2 * px + py, rows), _rows_of(out_refs[a], k, rows), send, recv, 3 * a + k,
                        (px, py, cc)) for a in range(na) for k, (px, py) in enumerate(chips)]
        return mine, sent

    def start(p_refs, out_refs, sems):
        mine, sent = copies(p_refs, out_refs, sems)
        for cp in mine + sent:
            cp.start()

    def wait(p_refs, out_refs, sems):
        mine, sent = copies(p_refs, out_refs, sems)
        for cp in sent:
            cp.wait()
        for cp in mine:
            cp.wait()

    ins, outs, aliases = _side_into(parts, [jax.ShapeDtypeStruct(p.shape, p.dtype) for p in parts], into)
    return Side(ins, outs, aliases,
                [pltpu.SemaphoreType.DMA((3 * na,)), pltpu.SemaphoreType.DMA((3 * na,)), pltpu.SemaphoreType.DMA((na,))],
                start, wait)


def add_sibling(chunks, recv, *, name):
    _, r, c = chunks.shape
    tr = r if r <= 1024 else _tile(r, 512, 16)
    core = lax.axis_index("c").astype(jnp.int32).reshape(1)

    def body(core_ref, a_ref, b_ref, o_ref):
        o_ref[...] = (a_ref[...] + b_ref[...]).astype(bf16)

    return pl.pallas_call(
        body, name=name,
        grid_spec=pltpu.PrefetchScalarGridSpec(
            num_scalar_prefetch=1, grid=(4, r // tr),
            in_specs=[pl.BlockSpec((1, tr, c), lambda q, i, core_ref: (2 * q + core_ref[0], i, 0)),
                      pl.BlockSpec((1, tr, c), lambda q, i, core_ref: (q, i, 0))],
            out_specs=pl.BlockSpec((1, tr, c), lambda q, i, core_ref: (q, i, 0))),
        out_shape=jax.ShapeDtypeStruct((4, r, c), bf16), compiler_params=_params("parallel", "parallel"),
    )(core, chunks, recv)


def add_four(r4, *, name):
    _, r, c = r4.shape
    tr = r if r <= 1024 else _tile(r, 512, 16)

    def body(a_ref, o_ref):
        o_ref[...] = ((a_ref[3].astype(f32) + a_ref[0].astype(f32)) + a_ref[1].astype(f32)) + a_ref[2].astype(f32)

    return pl.pallas_call(body, name=name, grid=(r // tr,), in_specs=[pl.BlockSpec((4, tr, c), lambda i: (0, i, 0))],
                          out_specs=pl.BlockSpec((tr, c), lambda i: (i, 0)), out_shape=jax.ShapeDtypeStruct((r, c), f32),
                          compiler_params=_params("parallel"))(r4)


def all_reduce_small(vec, *, name):
    r, c = vec.shape

    def body(v_ref, out_ref, buf_ref, send_sems, recv_sems):
        x, y, cc = _pos()
        my_id = 4 * x + 2 * y + cc
        buf_ref[my_id] = v_ref[...]
        flips = [(fx, fy, fc) for fx in (0, 1) for fy in (0, 1) for fc in (0, 1)][1:]
        cps = []
        for k, (fx, fy, fc) in enumerate(flips):
            peer = ((1 - x) if fx else x, (1 - y) if fy else y, (1 - cc) if fc else cc)
            cps.append(pltpu.make_async_remote_copy(src_ref=v_ref, dst_ref=buf_ref.at[my_id], send_sem=send_sems.at[k],
                                                    recv_sem=recv_sems.at[k], device_id=peer, device_id_type=MESH))
        for cp in cps:
            cp.start()
        for cp in cps:
            cp.wait()
        acc = buf_ref[0]
        for d in range(1, N_DEV):
            acc = acc + buf_ref[d]
        out_ref[...] = acc

    vm = pl.BlockSpec(memory_space=pltpu.VMEM)
    return pl.pallas_call(body, name=name, in_specs=[vm], out_specs=vm, out_shape=jax.ShapeDtypeStruct((r, c), f32),
                          scratch_shapes=[pltpu.VMEM((N_DEV, r, c), f32), pltpu.SemaphoreType.DMA((7,)),
                                          pltpu.SemaphoreType.DMA((7,))])(vec)


def _pack(parts, rows, dtype):
    flat = jnp.concatenate([p.reshape(-1).astype(dtype) for p in parts])
    return jnp.pad(flat, (0, rows * PACK_COLS - flat.shape[0])).reshape(rows, PACK_COLS)


def _unpack(flat, shapes):
    out, off = [], 0
    for shp in shapes:
        n = shp[0] * shp[1]
        out.append(flat[..., off:off + n].reshape(flat.shape[:-1] + tuple(shp)))
        off += n
    return out


def _from_column_shards(g):
    _, r, c = g.shape
    return jnp.transpose(g, (1, 0, 2)).reshape(r, N_DEV * c)


def _column_shards(full):
    r, c8 = full.shape
    return jnp.transpose(full.reshape(r, N_DEV, c8 // N_DEV), (1, 0, 2))


W_IN_SHARD = IN_DIM // N_DEV
W_IN_PAD = 1280
W_IN_PARTS = (("swa", 0, 0, 1280), ("swa", 1280, 5376, 5392), ("gdn", 0, 1280, 5376), ("gates", 0, 5392, IN_DIM))
W_IN_WIDTHS = {"swa": SWA_IN_W, "gdn": 4 * GDN_W, "gates": 2 * D_MODEL}


def _w_in_segments():
    segs = []
    for part, p0, g0, g1 in W_IN_PARTS:
        for j in range(N_DEV):
            lo, hi = max(g0, W_IN_SHARD * j), min(g1, W_IN_SHARD * (j + 1))
            if lo < hi:
                segs.append((part, p0 + lo - g0, j, lo - W_IN_SHARD * j, hi - lo))
    return segs


def split_w_in(shards, *, name):
    dt = shards.dtype
    tm = 256

    def body(w_ref, swa_ref, gdn_ref, gates_ref):
        out = {"swa": swa_ref, "gdn": gdn_ref, "gates": gates_ref}
        swa_ref[:, SWA_Q + 2 * SWA_KV + 2 * GDN_HEADS:] = jnp.zeros((tm, SWA_IN_W - SWA_Q - 2 * SWA_KV - 2 * GDN_HEADS), dt)
        for part, p0, j, l0, n in _w_in_segments():
            out[part][:, p0:p0 + n] = w_ref[j, :, l0:l0 + n]

    return pl.pallas_call(body, name=name, grid=(D_MODEL // tm,),
                          in_specs=[pl.BlockSpec((N_DEV, tm, W_IN_PAD), lambda i: (0, i, 0))],
                          out_specs=[pl.BlockSpec((tm, W_IN_WIDTHS[p]), lambda i: (i, 0)) for p in ("swa", "gdn", "gates")],
                          out_shape=[jax.ShapeDtypeStruct((D_MODEL, W_IN_WIDTHS[p]), dt) for p in ("swa", "gdn", "gates")],
                          compiler_params=_params("parallel"))(shards)


def merge_w_in_grad(d_swa, d_gdn, d_gates, *, name):
    tm = 256

    def body(swa_ref, gdn_ref, gates_ref, w_ref):
        src = {"swa": swa_ref, "gdn": gdn_ref, "gates": gates_ref}
        w_ref[:, :, W_IN_SHARD:] = jnp.zeros((N_DEV, tm, W_IN_PAD - W_IN_SHARD), f32)
        for part, p0, j, l0, n in _w_in_segments():
            w_ref[j, :, l0:l0 + n] = src[part][:, p0:p0 + n]

    return pl.pallas_call(body, name=name, grid=(D_MODEL // tm,),
                          in_specs=[pl.BlockSpec((tm, W_IN_WIDTHS[p]), lambda i: (i, 0)) for p in ("swa", "gdn", "gates")],
                          out_specs=pl.BlockSpec((N_DEV, tm, W_IN_PAD), lambda i: (0, i, 0)),
                          out_shape=jax.ShapeDtypeStruct((N_DEV, D_MODEL, W_IN_PAD), f32),
                          compiler_params=_params("parallel"))(d_swa, d_gdn, d_gates)


def kernel(x, mem, w_in, rel_bias, swa_sinks, gdn_conv_w, gdn_a_log, gdn_dt_bias, gdn_norm_w, w_br_swa, w_br_gdn, w_mix_o, ln1_g, ln1_b, w_mem_q, w_mem_kv, w_mem_o, ln2_g, ln2_b, w_up, ffn_conv_w, ffn_conv_b, w_down, ln3_g, ln3_b, loss_target, m_w_in, m_rel_bias, m_swa_sinks, m_gdn_conv_w, m_gdn_a_log, m_gdn_dt_bias, m_gdn_norm_w, m_w_br_swa, m_w_br_gdn, m_w_mix_o, m_ln1_g, m_ln1_b, m_w_mem_q, m_w_mem_kv, m_w_mem_o, m_ln2_g, m_ln2_b, m_w_up, m_ffn_conv_w, m_ffn_conv_b, m_w_down, m_ln3_g, m_ln3_b, v_w_in, v_rel_bias, v_swa_sinks, v_gdn_conv_w, v_gdn_a_log, v_gdn_dt_bias, v_gdn_norm_w, v_w_br_swa, v_w_br_gdn, v_w_mix_o, v_ln1_g, v_ln1_b, v_w_mem_q, v_w_mem_kv, v_w_mem_o, v_ln2_g, v_ln2_b, v_w_up, v_ffn_conv_w, v_ffn_conv_b, v_w_down, v_ln3_g, v_ln3_b):
    env = dict(locals())
    w2 = {n: (env[n][0] if env[n].ndim == 3 else env[n]) for n in WEIGHTS}
    m2 = {n: (env["m_" + n][0] if env["m_" + n].ndim == 3 else env["m_" + n]) for n in WEIGHTS}
    v2 = {n: (env["v_" + n][0] if env["v_" + n].ndim == 3 else env["v_" + n]) for n in WEIGHTS}
    xs, mems, target = x[0], mem[0], loss_target[0]
    my_id = 4 * lax.axis_index("x") + 2 * lax.axis_index("y") + lax.axis_index("c")
    pad_ff = FF_PAD - FF_SHARD

    pad_cols = {"w_in": W_IN_PAD - W_IN_SHARD, "w_up": pad_ff}
    mid = ("w_br_swa", "w_br_gdn", "w_mem_o", "w_mix_o", "w_mem_q", "w_mem_kv")
    mine = {n: jnp.pad(w2[n], ((0, 0), (0, pad_cols.get(n, 0)))).astype(bf16) for n in ("w_in", "w_up", "w_down") + mid}
    xb, got_in = cast_bf16(xs, name="cast_x", side=gather_first([mine["w_in"]]))
    got_in = run_side(gather_second(got_in), name="gather_w_in_pass_on")
    w_swa, w_gdn, w_gates = split_w_in(got_in[0], name="split_w_in")
    n_ffn, n_gdn = 3 * FF_SHARD, GDN_CONV * (QKV_W // N_DEV)
    conv_mine = jnp.concatenate([w2["ffn_conv_w"].reshape(-1), w2["gdn_conv_w"].reshape(-1)])[None]
    conv_rows = lax.dynamic_update_slice(jnp.zeros((N_DEV, n_ffn + n_gdn), f32), conv_mine, (my_id, 0))
    conv_all = all_reduce_small(_pack([conv_rows], CONV_ROWS, f32), name="gather_conv_w")
    conv_all = conv_all.reshape(-1)[:N_DEV * (n_ffn + n_gdn)].reshape(N_DEV, n_ffn + n_gdn)
    cwb = jnp.concatenate([conv_all[:, :n_ffn].reshape(N_DEV, 3, FF_SHARD), w2["ffn_conv_b"].reshape(N_DEV, 1, FF_SHARD),
                           jnp.zeros((N_DEV, 4, FF_SHARD), f32)], axis=1)
    cwb = jnp.pad(cwb, ((0, 0), (0, 0), (0, pad_ff)))
    convw = jnp.transpose(conv_all[:, n_ffn:].reshape(N_DEV, GDN_CONV, QKV_W // N_DEV), (1, 0, 2)).reshape(GDN_CONV, QKV_W)
    convw = jnp.pad(convw, ((0, 4), (0, 0)))
    onehot = _bucket_onehot()
    bias = mm(w2["rel_bias"].T, onehot, "nn", hi=True, tn=4096, name="rel_bias_table").reshape(SWA_HEADS, BLOCK, 2 * BLOCK)
    alog_row = jnp.pad(w2["gdn_a_log"], ((0, 0), (GDN_HEADS, 128 - 2 * GDN_HEADS)))
    dt_row = jnp.pad(w2["gdn_dt_bias"], ((0, 0), (GDN_HEADS, 128 - 2 * GDN_HEADS)))

    memb = cast_bf16(mems, name="cast_mem")
    gates, mid_got = mm(xb, w_gates, "nn", out_dtype=bf16, name="proj_gates", side=gather_first([mine[n] for n in mid]))
    gdn_in, mid_got = mm(xb, w_gdn, "nn", name="proj_gdn", side=gather_second(mid_got))
    got = dict(zip(mid, mid_got))
    w_br_swa, w_br_gdn, w_mem_o = (_from_column_shards(got[n]) for n in ("w_br_swa", "w_br_gdn", "w_mem_o"))
    w_mix_o = got["w_mix_o"].reshape(D_MODEL, D_MODEL)
    w_mem_q = got["w_mem_q"].reshape(D_MODEL, MEM_W)
    w_mem_kv = got["w_mem_kv"].reshape(D_MODEL, 2 * MEM_W)
    swa_in = mm(xb, w_swa, "nn", tn=SWA_IN_W, name="proj_swa")
    attn, down_got = swa_fwd(swa_in, bias, w2["swa_sinks"], name="swa_fwd", side=gather_first([mine["w_down"]]))
    qn, kn, vv, gdn_conv = gdn_pre_fwd(gdn_in, convw, name="gdn_pre_fwd")
    gbeta = gbeta_fwd(swa_in, alog_row, dt_row, name="gbeta_fwd")
    (o_gdn, states, inverses), up_got = gdn_scan_fwd(qn, kn, vv, gbeta, name="gdn_scan_fwd",
                                                     side=gather_first([mine["w_up"]]))
    ygd = gdn_post_fwd(o_gdn, gdn_in, w2["gdn_norm_w"], name="gdn_post_fwd")
    y_swa, down_got = mm(attn, w_br_swa, "nn", out_dtype=bf16, name="br_swa", side=gather_second(down_got))
    y_gdn, up_got = mm(ygd, w_br_gdn, "nn", out_dtype=bf16, name="br_gdn", side=gather_second(up_got))
    w_up_blocked = up_got[0]
    w_down_p = jnp.pad(down_got[0].reshape(4, FF_SHARD, D_MODEL), ((0, 0), (0, pad_ff), (0, 0))).reshape(4 * FF_PAD, D_MODEL)
    mixed = merge_fwd(gates, y_swa, y_gdn, name="merge_fwd")
    z1 = mm(mixed, w_mix_o, "nn", add=xs, add_scale=ALPHA, name="mix_o")
    x1, x1b = ln_fwd(z1, w2["ln1_g"], w2["ln1_b"], name="ln1_fwd")
    qm = mm(x1b, w_mem_q, "nn", name="mem_q")
    kv = mm(memb, w_mem_kv, "nn", name="mem_kv")
    om = memattn_fwd(qm, kv, name="memattn_fwd")
    z2 = mm(om, w_mem_o, "nn", add=x1, add_scale=ALPHA, name="mem_o")
    x2, x2b = ln_fwd(z2, w2["ln2_g"], w2["ln2_b"], name="ln2_fwd")
    hpre = mm(x2b, w_up_blocked, "nn", b_blocked=True, out_dtype=bf16, name="ffn_up")
    act, conv_g, conv_u = ffn_act_fwd(hpre, cwb, name="ffn_act_fwd")
    z3 = mm(act, w_down_p, "nn", add=x2, add_scale=ALPHA, tk=2 * FF_PAD, name="ffn_down")
    dz3, dz3b, d_ln3g, d_ln3b, loss = ln_loss(z3, target, w2["ln3_g"], w2["ln3_b"], name="ln3_loss")

    dact = mm(dz3b, w_down_p, "nt", tn=FF_PAD, out_dtype=bf16, name="d_act")
    d_wdown_p = mm(act, dz3b, "tn", tm=FF_PAD, name="dw_down")
    d_hpre, d_cwb = ffn_act_bwd(hpre, conv_g, conv_u, dact, cwb, name="ffn_act_bwd")
    def sibling_sums(names, chunks, received):
        return [add_sibling(c, r, name="grad_add_sibling_" + n) for n, c, r in zip(names, chunks, received)]

    def chip_sums(names, received):
        return [add_four(r, name="grad_add_chips_" + n) for n, r in zip(names, received)]

    dx2 = mm(d_hpre, w_up_blocked, "nt", b_blocked=True, k_shards=2, add=dz3, add_scale=ALPHA, name="d_x2")
    d_wup = mm(x2b, d_hpre, "tn", out_blocked=True, name="dw_up")
    ffn = ("w_up", "w_down")
    ffn_chunks = [d_wup, d_wdown_p.reshape(4, FF_PAD, D_MODEL)[:, :FF_SHARD].reshape(N_DEV, FF_SHARD // 2, D_MODEL)]
    dz2, dz2b, d_ln2g, d_ln2b = ln_bwd(dx2, z2, w2["ln2_g"], name="ln2_bwd")
    d_om, down_received = mm(dz2b, w_mem_o, "nt", name="d_om", side=grad_to_sibling(ffn_chunks[1:]))
    d_wmemo = mm(om, dz2b, "tn", name="dw_mem_o")
    dqm, dkv = memattn_bwd(qm, kv, d_om, name="memattn_bwd")
    dx1 = mm(dqm, w_mem_q, "nt", add=dz2, add_scale=ALPHA, name="d_x1")
    d_wmemq = mm(x1b, dqm, "tn", name="dw_mem_q")
    d_wmemkv = mm(memb, dkv, "tn", name="dw_mem_kv")
    dz1, dz1b, d_ln1g, d_ln1b = ln_bwd(dx1, z1, w2["ln1_g"], name="ln1_bwd")
    half = D_MODEL // 2
    dmix, up_received = mm(dz1b, w_mix_o, "nt", name="d_mixed", side=grad_to_sibling(ffn_chunks[:1], rows=(0, half)))
    d_wmixo, up_received = mm(mixed, dz1b, "tn", name="dw_mix_o",
                              side=grad_to_sibling(ffn_chunks[:1], rows=(half, half), into=up_received))
    ffn_sums = sibling_sums(ffn, ffn_chunks, up_received + down_received)
    dys, dyg, d_gates = merge_bwd(gates, y_swa, y_gdn, dmix, name="merge_bwd")
    d_attn = mm(dys, w_br_swa, "nt", name="d_attn")
    d_wbrswa = mm(attn, dys, "tn", name="dw_br_swa")
    d_ygd = mm(dyg, w_br_gdn, "nt", name="d_ygd")
    d_wbrgdn = mm(ygd, dyg, "tn", name="dw_br_gdn")
    mid_chunks = [_column_shards(d_wbrswa), _column_shards(d_wbrgdn), _column_shards(d_wmemo),
                  d_wmixo.reshape(N_DEV, D_MODEL // N_DEV, D_MODEL), d_wmemq.reshape(N_DEV, D_MODEL // N_DEV, MEM_W),
                  d_wmemkv.reshape(N_DEV, D_MODEL // N_DEV, 2 * MEM_W)]
    d_o, d_gz, d_normw = gdn_post_bwd(o_gdn, gdn_in, w2["gdn_norm_w"], d_ygd, name="gdn_post_bwd")
    (dqn, dkn, dvv, dgbeta), received = gdn_scan_bwd(
        qn, kn, vv, gbeta, states, inverses, d_o, name="gdn_scan_bwd",
        side=join_sides(grad_to_chips(ffn_sums), grad_to_sibling(mid_chunks)))
    grads = dict(zip(ffn, chip_sums(ffn, received[:2])))
    mid_sums = sibling_sums(mid, mid_chunks, received[2:])
    d_gdn_in, d_convw = gdn_pre_bwd(gdn_in, gdn_conv, convw, dqn, dkn, dvv, d_gz, name="gdn_pre_bwd")
    d_ba, d_alog, d_dt = gbeta_bwd(swa_in, alog_row, dt_row, dgbeta, name="gbeta_bwd")
    (dq, dkc, dkp, dvc, dvp, d_bias, d_sinks), received = swa_bwd(swa_in, bias, w2["swa_sinks"], d_attn, name="swa_bwd",
                                                                  side=grad_to_chips(mid_sums))
    grads.update(zip(mid, chip_sums(mid, received)))
    d_swa_in = swa_in_grad(dq, dkc, dkp, dvc, dvp, d_ba, name="swa_in_grad")
    d_relbias = mm(d_bias.reshape(SWA_HEADS, -1), onehot, "nt", hi=True, tk=4096, name="d_rel_bias").T
    d_wgates = mm(xb, d_gates, "tn", name="dw_gates")
    d_wgdn = mm(xb, d_gdn_in, "tn", name="dw_gdn")
    d_wswa = mm(xb, d_swa_in, "tn", tn=SWA_IN_W, name="dw_swa")
    in_chunks = [merge_w_in_grad(d_wswa, d_wgdn, d_wgates, name="merge_w_in_grad")]
    gx, received = mm(d_swa_in, w_swa, "nt", add=dz1, add_scale=ALPHA, tk=SWA_IN_W, name="dx_swa",
                      side=grad_to_sibling(in_chunks))
    in_sums = sibling_sums(("w_in",), in_chunks, received)
    gx, received = mm(d_gates, w_gates, "nt", add=gx, name="dx_gates", side=grad_to_chips(in_sums, rows=(0, half)))
    gx, received = mm(d_gdn_in, w_gdn, "nt", add=gx, name="dx_gdn",
                      side=grad_to_chips(in_sums, rows=(half, half), into=received))
    grads["w_in"] = chip_sums(("w_in",), received)[0][:, :W_IN_SHARD]
    grads["w_up"] = grads["w_up"][:, :FF_SHARD]

    gsmall = {
        "rel_bias": d_relbias, "swa_sinks": d_sinks[:, :SWA_HEADS], "gdn_a_log": d_alog[:, GDN_HEADS:2 * GDN_HEADS],
        "gdn_dt_bias": d_dt[:, GDN_HEADS:2 * GDN_HEADS], "gdn_norm_w": d_normw, "ln1_g": d_ln1g, "ln1_b": d_ln1b,
        "ln2_g": d_ln2g, "ln2_b": d_ln2b, "ln3_g": d_ln3g, "ln3_b": d_ln3b,
        "ffn_conv_b": d_cwb[:, 3, :FF_SHARD].reshape(1, 2 * D_FF),
        "ffn_conv_w": jnp.transpose(d_cwb[:, :3, :FF_SHARD], (1, 0, 2)).reshape(3, 2 * D_FF),
        "gdn_conv_w": d_convw[:GDN_CONV],
    }
    small_shapes = [shp for _, shp in SMALL] + [(3, 2 * D_FF), (GDN_CONV, QKV_W)]
    small_names = [n for n, _ in SMALL] + ["ffn_conv_w", "gdn_conv_w"]
    small_sum = all_reduce_small(_pack([gsmall[n] for n in small_names], AR_ROWS, f32), name="all_reduce_small")
    grads.update(zip(small_names, _unpack(small_sum.reshape(-1), small_shapes)))
    grads["ffn_conv_w"] = lax.dynamic_slice_in_dim(grads["ffn_conv_w"], my_id * FF_SHARD, FF_SHARD, axis=1)
    grads["gdn_conv_w"] = lax.dynamic_slice_in_dim(grads["gdn_conv_w"], my_id * (QKV_W // N_DEV), QKV_W // N_DEV, axis=1)

    big = [n for n, shp, _ in SHARDED if shp[0] * shp[1] > 8192]
    tiny = [n for n in WEIGHTS if n not in big]
    delta, new_m, new_v = {}, {}, {}
    for n in big:
        delta[n], new_m[n], new_v[n] = adamw(w2[n], grads[n], m2[n], v2[n], name="adamw_" + n)
    tiny_shapes = [w2[n].shape for n in tiny]
    packed = [_pack([src[n] for n in tiny], SMALL_ROWS, f32) for src in (w2, grads, m2, v2)]
    for dst, res in zip((delta, new_m, new_v), adamw(*packed, name="adamw_small")):
        dst.update(zip(tiny, _unpack(res.reshape(-1), tiny_shapes)))

    def shaped(d):
        return [d[n].reshape(env[n].shape) for n in WEIGHTS]

    loss_all = lax.psum(loss[0, 0], ("x", "y", "c"))
    return (loss_all, gx[None], *shaped(grads), *shaped(delta), *shaped(new_m), *shaped(new_v))
```

```python
import functools
import math
from typing import Callable, NamedTuple

import jax
import jax.numpy as jnp
from jax import lax
from jax.experimental import pallas as pl
from jax.experimental.pallas import tpu as pltpu

f32 = jnp.float32
bf16 = jnp.bfloat16
HI = lax.Precision.HIGHEST
MESH = pl.DeviceIdType.MESH

D_MODEL = 2048
N_DEV = 8
SWA_HEADS, SWA_KV_HEADS, SWA_HEAD_DIM, BLOCK = 16, 2, 64, 128
REL_BUCKETS, REL_MAX_DIST = 32, 128
GDN_HEADS, GDN_HEAD_DIM, GDN_CONV, GDN_CHUNK = 8, 128, 4, 64
MEM_HEADS, MEM_HEAD_DIM = 4, 128
D_FF = 5504
FF_SHARD = 2 * D_FF // N_DEV
FF_PAD = 1408
NORM_EPS = 1e-5
ALPHA = 2.0 ** 0.25
NEG_INF = -1e30
SWA_Q, SWA_KV, GDN_W, MEM_W = 1024, 128, 1024, 512
IN_DIM = 9488
HALO = 8

ADAM_LR, ADAM_B1, ADAM_B2, ADAM_EPS, ADAM_WD, ADAM_STEP = 0.001, 0.9, 0.999, 1e-08, 0.01, 10

PACK_COLS = 1024
SMALL_ROWS = 32
AR_ROWS = 72
CONV_ROWS = 48

SHARDED = (
    ("w_in", (2048, 1186), 1), ("w_br_swa", (1024, 256), 1), ("w_br_gdn", (1024, 256), 1),
    ("w_mix_o", (256, 2048), 0), ("w_mem_q", (256, 512), 0), ("w_mem_kv", (256, 1024), 0),
    ("w_mem_o", (512, 256), 1), ("w_up", (2048, 1376), 1), ("w_down", (688, 2048), 0),
    ("ffn_conv_w", (3, 1376), 1), ("gdn_conv_w", (4, 384), 1),
)
SMALL = (
    ("rel_bias", (32, 16)), ("swa_sinks", (1, 16)), ("gdn_a_log", (1, 8)), ("gdn_dt_bias", (1, 8)),
    ("gdn_norm_w", (1, 128)), ("ln1_g", (1, 2048)), ("ln1_b", (1, 2048)), ("ln2_g", (1, 2048)),
    ("ln2_b", (1, 2048)), ("ln3_g", (1, 2048)), ("ln3_b", (1, 2048)), ("ffn_conv_b", (1, 11008)),
)
WEIGHTS = ("w_in", "rel_bias", "swa_sinks", "gdn_conv_w", "gdn_a_log", "gdn_dt_bias", "gdn_norm_w", "w_br_swa",
           "w_br_gdn", "w_mix_o", "ln1_g", "ln1_b", "w_mem_q", "w_mem_kv", "w_mem_o", "ln2_g", "ln2_b", "w_up",
           "ffn_conv_w", "ffn_conv_b", "w_down", "ln3_g", "ln3_b")


def _tile(n, target, align):
    if n <= target:
        return n
    t = (target // align) * align
    while t >= align:
        if n % t == 0:
            return t
        t -= align
    return n


VMEM_LIMIT_BYTES = 56 * 1024 * 1024


def _params(*sem):
    return pltpu.CompilerParams(dimension_semantics=sem, vmem_limit_bytes=VMEM_LIMIT_BYTES)


def _sigmoid(v):
    return jax.nn.sigmoid(v)


def _d16(a, b, dims):
    return lax.dot_general(a.astype(bf16), b.astype(bf16), (dims, ((), ())), preferred_element_type=f32)


NN = ((1,), (0,))
NT = ((1,), (1,))
TN = ((0,), (0,))


def mm(a, b, mode, *, name, add=None, add_scale=1.0, out_dtype=f32, hi=False, tm=1024, tn=1024, tk=2048,
       b_blocked=False, out_blocked=False, k_shards=1, side=None):
    if b_blocked:
        nb, rows, width = b.shape
        if mode == "nn":
            (m, k), n, tn = a.shape, nb * width, width
        else:
            (m, k), n, tk = a.shape, rows, k_shards * width
    elif mode == "nn":
        (m, k), (_, n) = a.shape, b.shape
    elif mode == "nt":
        (m, k), (n, _) = a.shape, b.shape
    else:
        (k, m), (_, n) = a.shape, b.shape
    if out_blocked:
        tn = n // N_DEV
    tm, tn, tk = _tile(m, tm, 8 if mode != "tn" else 128), _tile(n, tn, 128), _tile(k, tk, 128 if mode != "tn" else 8)
    nk = k // tk
    dims = {"nn": NN, "nt": NT, "tn": TN}[mode]
    a_spec = pl.BlockSpec((tk, tm), lambda i, j, kk: (kk, i)) if mode == "tn" else pl.BlockSpec((tm, tk), lambda i, j, kk: (i, kk))
    if b_blocked:
        b_spec = (pl.BlockSpec((None, tk, tn), lambda i, j, kk: (j, kk, 0)) if mode == "nn"
                  else pl.BlockSpec((k_shards, tn, tk // k_shards), lambda i, j, kk: (kk, j, 0)))
    else:
        b_spec = pl.BlockSpec((tn, tk), lambda i, j, kk: (j, kk)) if mode == "nt" else pl.BlockSpec((tk, tn), lambda i, j, kk: (kk, j))
    if out_blocked:
        o_spec, o_shape = pl.BlockSpec((None, tm, tn), lambda i, j, kk: (j, i, 0)), (N_DEV, m, tn)
    else:
        o_spec, o_shape = pl.BlockSpec((tm, tn), lambda i, j, kk: (i, j)), (m, n)
    has_add = add is not None

    def product(a_ref, b_ref):
        if hi:
            return lax.dot_general(a_ref[...], b_ref[...], (dims, ((), ())), precision=HI, preferred_element_type=f32)
        if b_blocked and mode == "nt":
            width = tk // k_shards
            parts = [_d16(a_ref[:, s * width:(s + 1) * width], b_ref[s], dims) for s in range(k_shards)]
            return functools.reduce(lambda p, q: p + q, parts)
        return _d16(a_ref[...], b_ref[...], dims)

    def finish(r, add_ref, o_ref):
        if has_add:
            r = r + add_scale * add_ref[...]
        o_ref[...] = r.astype(out_dtype)

    def body_one_step(a_ref, b_ref, *rest):
        finish(product(a_ref, b_ref), rest[0] if has_add else None, rest[-1])

    def body_k_steps(a_ref, b_ref, *rest):
        o_ref, acc_ref = rest[-2:]
        kk = pl.program_id(2)

        @pl.when(kk == 0)
        def _():
            acc_ref[...] = jnp.zeros_like(acc_ref)

        acc_ref[...] += product(a_ref, b_ref)

        @pl.when(kk == nk - 1)
        def _():
            finish(acc_ref[...], rest[0] if has_add else None, o_ref)

    return _call(body_one_step if nk == 1 else body_k_steps, (a, b, add) if has_add else (a, b), name=name,
                 grid=(m // tm, n // tn, nk), in_specs=[a_spec, b_spec] + ([o_spec] if has_add else []), out_specs=o_spec,
                 out_shape=jax.ShapeDtypeStruct(o_shape, out_dtype),
                 scratch_shapes=[] if nk == 1 else [pltpu.VMEM((tm, tn), f32)],
                 semantics=("parallel", "parallel", "arbitrary"), side=side)


def cast_bf16(a, *, name, side=None):
    m, n = a.shape
    tm = _tile(m, 512, 16)

    def body(a_ref, o_ref):
        o_ref[...] = a_ref[...].astype(bf16)

    return _call(body, (a,), name=name, grid=(m // tm,), in_specs=[pl.BlockSpec((tm, n), lambda i: (i, 0))],
                 out_specs=pl.BlockSpec((tm, n), lambda i: (i, 0)), out_shape=jax.ShapeDtypeStruct((m, n), bf16),
                 semantics=("parallel",), side=side)


def _ln_stats(z):
    mu = jnp.mean(z, axis=-1, keepdims=True)
    zc = z - mu
    var = jnp.mean(zc * zc, axis=-1, keepdims=True)
    rstd = lax.rsqrt(var + NORM_EPS)
    return zc * rstd, rstd


def ln_fwd(z, g, b, *, name):
    s, d = z.shape
    tm = _tile(s, 256, 16)

    def body(z_ref, g_ref, b_ref, y_ref, yb_ref):
        xhat, _ = _ln_stats(z_ref[...])
        y = xhat * g_ref[...] + b_ref[...]
        y_ref[...] = y
        yb_ref[...] = y.astype(bf16)

    row = pl.BlockSpec((tm, d), lambda i: (i, 0))
    vec = pl.BlockSpec((1, d), lambda i: (0, 0))
    return pl.pallas_call(body, name=name, grid=(s // tm,), in_specs=[row, vec, vec], out_specs=[row, row],
                          out_shape=[jax.ShapeDtypeStruct((s, d), f32), jax.ShapeDtypeStruct((s, d), bf16)],
                          compiler_params=_params("parallel"))(z, g, b)


def _ln_bwd_tile(dy, z, g):
    xhat, rstd = _ln_stats(z)
    dxh = dy * g
    m1 = jnp.mean(dxh, axis=-1, keepdims=True)
    m2 = jnp.mean(dxh * xhat, axis=-1, keepdims=True)
    dz = rstd * (dxh - m1 - xhat * m2)
    return dz, jnp.sum(dy * xhat, axis=0, keepdims=True), jnp.sum(dy, axis=0, keepdims=True)


def ln_bwd(dy, z, g, *, name):
    s, d = z.shape
    tm = _tile(s, 256, 16)

    def body(dy_ref, z_ref, g_ref, dz_ref, dzb_ref, dg_ref, db_ref):
        @pl.when(pl.program_id(0) == 0)
        def _():
            dg_ref[...] = jnp.zeros_like(dg_ref)
            db_ref[...] = jnp.zeros_like(db_ref)

        dz, dg, db = _ln_bwd_tile(dy_ref[...], z_ref[...], g_ref[...])
        dz_ref[...] = dz
        dzb_ref[...] = dz.astype(bf16)
        dg_ref[...] += dg
        db_ref[...] += db

    row = pl.BlockSpec((tm, d), lambda i: (i, 0))
    vec = pl.BlockSpec((1, d), lambda i: (0, 0))
    return pl.pallas_call(body, name=name, grid=(s // tm,), in_specs=[row, row, vec], out_specs=[row, row, vec, vec],
                          out_shape=[jax.ShapeDtypeStruct((s, d), f32), jax.ShapeDtypeStruct((s, d), bf16),
                                     jax.ShapeDtypeStruct((1, d), f32), jax.ShapeDtypeStruct((1, d), f32)],
                          compiler_params=_params("arbitrary"))(dy, z, g)


def ln_loss(z, target, g, b, *, name):
    s, d = z.shape
    tm = _tile(s, 256, 16)
    nt = s // tm

    def body(z_ref, t_ref, g_ref, b_ref, dz_ref, dzb_ref, dg_ref, db_ref, loss_ref, lacc_ref):
        i = pl.program_id(0)

        @pl.when(i == 0)
        def _():
            dg_ref[...] = jnp.zeros_like(dg_ref)
            db_ref[...] = jnp.zeros_like(db_ref)
            lacc_ref[...] = jnp.zeros_like(lacc_ref)

        zv, gv = z_ref[...], g_ref[...]
        xhat, _ = _ln_stats(zv)
        err = xhat * gv + b_ref[...] - t_ref[...]
        lacc_ref[...] += jnp.sum(err * err, axis=0, keepdims=True)
        dz, dg, db = _ln_bwd_tile(err * (1.0 / d), zv, gv)
        dz_ref[...] = dz
        dzb_ref[...] = dz.astype(bf16)
        dg_ref[...] += dg
        db_ref[...] += db

        @pl.when(i == nt - 1)
        def _():
            loss_ref[...] = (0.5 / d) * jnp.sum(lacc_ref[...], axis=1, keepdims=True)

    row = pl.BlockSpec((tm, d), lambda i: (i, 0))
    vec = pl.BlockSpec((1, d), lambda i: (0, 0))
    return pl.pallas_call(body, name=name, grid=(nt,), in_specs=[row, row, vec, vec],
                          out_specs=[row, row, vec, vec, pl.BlockSpec((1, 1), lambda i: (0, 0))],
                          out_shape=[jax.ShapeDtypeStruct((s, d), f32), jax.ShapeDtypeStruct((s, d), bf16),
                                     jax.ShapeDtypeStruct((1, d), f32), jax.ShapeDtypeStruct((1, d), f32),
                                     jax.ShapeDtypeStruct((1, 1), f32)],
                          scratch_shapes=[pltpu.VMEM((1, d), f32)],
                          compiler_params=_params("arbitrary"))(z, target, g, b)


def merge_fwd(gates, ys, yg, *, name):
    s, d = ys.shape
    tm = _tile(s, 256, 16)

    def body(gt_ref, ys_ref, yg_ref, o_ref):
        o_ref[...] = (_sigmoid(gt_ref[:, :d].astype(f32)) * ys_ref[...].astype(f32)
                      + _sigmoid(gt_ref[:, d:].astype(f32)) * yg_ref[...].astype(f32)).astype(bf16)

    row = pl.BlockSpec((tm, d), lambda i: (i, 0))
    return pl.pallas_call(body, name=name, grid=(s // tm,), in_specs=[pl.BlockSpec((tm, 2 * d), lambda i: (i, 0)), row, row],
                          out_specs=row, out_shape=jax.ShapeDtypeStruct((s, d), bf16),
                          compiler_params=_params("parallel"))(gates, ys, yg)


def merge_bwd(gates, ys, yg, dmix, *, name):
    s, d = ys.shape
    tm = _tile(s, 256, 16)

    def body(gt_ref, ys_ref, yg_ref, dm_ref, dys_ref, dyg_ref, dgt_ref):
        dm = dm_ref[...]
        sa, sb = _sigmoid(gt_ref[:, :d].astype(f32)), _sigmoid(gt_ref[:, d:].astype(f32))
        dys_ref[...] = (dm * sa).astype(bf16)
        dyg_ref[...] = (dm * sb).astype(bf16)
        dgt_ref[:, :d] = (dm * ys_ref[...].astype(f32) * sa * (1.0 - sa)).astype(bf16)
        dgt_ref[:, d:] = (dm * yg_ref[...].astype(f32) * sb * (1.0 - sb)).astype(bf16)

    row = pl.BlockSpec((tm, d), lambda i: (i, 0))
    wide = pl.BlockSpec((tm, 2 * d), lambda i: (i, 0))
    return pl.pallas_call(body, name=name, grid=(s // tm,), in_specs=[wide, row, row, row], out_specs=[row, row, wide],
                          out_shape=[jax.ShapeDtypeStruct((s, d), bf16), jax.ShapeDtypeStruct((s, d), bf16),
                                     jax.ShapeDtypeStruct((s, 2 * d), bf16)],
                          compiler_params=_params("parallel"))(gates, ys, yg, dmix)


def _shift_down(ext, j):
    return ext if j == 0 else pltpu.roll(ext, j, 0)


def _shift_up(ext, j):
    return ext if j == 0 else pltpu.roll(ext, ext.shape[0] - j, 0)


def _conv_taps(ext, width):
    return [_shift_down(ext, width - 1 - j)[HALO:] for j in range(width)]


def _causal_conv(taps, w_ref):
    acc = None
    for j, tap in enumerate(taps):
        term = w_ref[j:j + 1, :] * tap
        acc = term if acc is None else acc + term
    return acc


def _conv_grads(dy_ext, x, w_ref, width, rows):
    ahead = [_shift_up(dy_ext, width - 1 - j)[:rows] for j in range(width)]
    dx = None
    for j in range(width):
        term = w_ref[j:j + 1, :] * ahead[j]
        dx = term if dx is None else dx + term
    return dx, [jnp.sum(x * ahead[j], axis=0, keepdims=True) for j in range(width)]


def _rows_to_block(rows, n_rows, cols):
    r = lax.broadcasted_iota(jnp.int32, (n_rows, cols), 0)
    out = jnp.zeros((n_rows, cols), f32)
    for j, v in enumerate(rows):
        out = out + jnp.where(r == j, v, 0.0)
    return out


def _silu_and_grad(v):
    sg = _sigmoid(v)
    return v * sg, sg * (1.0 + v * (1.0 - sg))


HALO_BF16 = 16


def ffn_act_fwd(hpre, cwb, *, name):
    s = hpre.shape[0]
    tm = _tile(s, 256, 16)
    hb = tm // HALO_BF16

    def body(hg_ref, hgp_ref, hu_ref, hup_ref, cg_ref, cu_ref, o_ref, g_ref, u_ref):
        first = pl.program_id(1) == 0

        def conv(h_ref, hp_ref, c_ref):
            prev = jnp.where(first, 0.0, hp_ref[...].astype(f32)[HALO_BF16 - HALO:])
            ext = jnp.concatenate([prev, h_ref[...].astype(f32)], axis=0)
            return _causal_conv(_conv_taps(ext, 3), c_ref.at[0]) + c_ref[0, 3:4, :]

        g = conv(hg_ref, hgp_ref, cg_ref)
        u = conv(hu_ref, hup_ref, cu_ref)
        g_ref[...] = g.astype(bf16)
        u_ref[...] = u.astype(bf16)
        o_ref[...] = (g * _sigmoid(g) * u).astype(bf16)

    def tile(off):
        return pl.BlockSpec((tm, FF_PAD), lambda j, i: (i, j + off))

    def halo(off):
        return pl.BlockSpec((HALO_BF16, FF_PAD), lambda j, i: (jnp.maximum(i * hb - 1, 0), j + off))

    def taps(off):
        return pl.BlockSpec((1, 8, FF_PAD), lambda j, i: (j + off, 0, 0))

    out = pl.BlockSpec((tm, FF_PAD), lambda j, i: (i, j))
    return pl.pallas_call(body, name=name, grid=(4, s // tm),
                          in_specs=[tile(0), halo(0), tile(4), halo(4), taps(0), taps(4)], out_specs=[out, out, out],
                          out_shape=[jax.ShapeDtypeStruct((s, 4 * FF_PAD), bf16)] * 3,
                          compiler_params=_params("parallel", "parallel"))(hpre, hpre, hpre, hpre, cwb, cwb)


def ffn_act_bwd(hpre, conv_g, conv_u, dact, cwb, *, name):
    s = hpre.shape[0]
    tm = _tile(s, 256, 16)
    hb = tm // HALO_BF16
    nt = s // tm
    last_hb = s // HALO_BF16 - 1

    def body(hg_ref, hu_ref, g_ref, gn_ref, u_ref, un_ref, d_ref, dn_ref, cg_ref, cu_ref, dh_ref, dcg_ref, dcu_ref,
             buf_ref, sems):
        j, i = pl.program_id(0), pl.program_id(1)
        step = j * nt + i
        slot = step % 2

        def writes(from_slot):
            rows = pl.ds(pl.multiple_of(i * tm, tm), tm)
            return [pltpu.make_async_copy(buf_ref.at[from_slot, half],
                                          dh_ref.at[rows, pl.ds(pl.multiple_of((j + 4 * half) * FF_PAD, 128), FF_PAD)],
                                          sems.at[from_slot, half]) for half in (0, 1)]

        @pl.when(i == 0)
        def _():
            dcg_ref[...] = jnp.zeros_like(dcg_ref)
            dcu_ref[...] = jnp.zeros_like(dcu_ref)

        @pl.when(step >= 2)
        def _():
            for cp in writes(slot):
                cp.wait()

        def with_future(t_ref, n_ref):
            return jnp.concatenate([t_ref[...].astype(f32), n_ref[...].astype(f32)[:HALO]], axis=0)

        g, u = with_future(g_ref, gn_ref), with_future(u_ref, un_ref)
        d = jnp.concatenate([d_ref[...].astype(f32), jnp.where(i == nt - 1, 0.0, dn_ref[...].astype(f32)[:HALO])], axis=0)
        act, dact_dg = _silu_and_grad(g)
        dg = d * u * dact_dg
        du = d * act
        dhg, dwg = _conv_grads(dg, hg_ref[...].astype(f32), cg_ref.at[0], 3, tm)
        dhu, dwu = _conv_grads(du, hu_ref[...].astype(f32), cu_ref.at[0], 3, tm)
        buf_ref[slot, 0] = dhg.astype(bf16)
        buf_ref[slot, 1] = dhu.astype(bf16)
        for cp in writes(slot):
            cp.start()
        dcg_ref[0] += _rows_to_block(dwg + [jnp.sum(dg[:tm], axis=0, keepdims=True)], 8, FF_PAD)
        dcu_ref[0] += _rows_to_block(dwu + [jnp.sum(du[:tm], axis=0, keepdims=True)], 8, FF_PAD)

        @pl.when(step == 4 * nt - 1)
        def _():
            for cp in writes(slot) + writes(1 - slot):
                cp.wait()

    def tile(off):
        return pl.BlockSpec((tm, FF_PAD), lambda j, i: (i, j + off))

    nxt = pl.BlockSpec((HALO_BF16, FF_PAD), lambda j, i: (jnp.minimum((i + 1) * hb, last_hb), j))
    taps = [pl.BlockSpec((1, 8, FF_PAD), lambda j, i, off=off: (j + off, 0, 0)) for off in (0, 4)]
    dh, dcg, dcu = pl.pallas_call(
        body, name=name, grid=(4, nt),
        in_specs=[tile(0), tile(4), tile(0), nxt, tile(0), nxt, tile(0), nxt] + taps,
        out_specs=[ANY, taps[0], taps[0]],
        out_shape=[jax.ShapeDtypeStruct((s, 8 * FF_PAD), bf16),
                   jax.ShapeDtypeStruct((4, 8, FF_PAD), f32), jax.ShapeDtypeStruct((4, 8, FF_PAD), f32)],
        scratch_shapes=[pltpu.VMEM((2, 2, tm, FF_PAD), bf16), pltpu.SemaphoreType.DMA((2, 2))],
        compiler_params=_params("arbitrary", "arbitrary"),
    )(hpre, hpre, conv_g, conv_g, conv_u, conv_u, dact, dact, cwb, cwb)
    return dh, jnp.concatenate([dcg, dcu], axis=0)


MEM_SCALE = MEM_HEAD_DIM ** -0.5


def _softmax_rows(sc):
    m = jnp.max(sc, axis=-1, keepdims=True)
    e = jnp.exp(sc - m)
    return e / jnp.sum(e, axis=-1, keepdims=True)


def memattn_fwd(qm, kv, *, name):
    s = qm.shape[0]
    mlen = kv.shape[0]
    tm = _tile(s, 512, 16)

    def body(q_ref, kv_ref, o_ref):
        for h in range(MEM_HEADS):
            lo = h * MEM_HEAD_DIM
            q = q_ref[:, lo:lo + MEM_HEAD_DIM]
            k = kv_ref[:, lo:lo + MEM_HEAD_DIM]
            v = kv_ref[:, MEM_W + lo:MEM_W + lo + MEM_HEAD_DIM]
            p = _softmax_rows(_d16(q, k, NT) * MEM_SCALE)
            o_ref[:, lo:lo + MEM_HEAD_DIM] = _d16(p, v, NN).astype(bf16)

    return pl.pallas_call(body, name=name, grid=(s // tm,),
                          in_specs=[pl.BlockSpec((tm, MEM_W), lambda i: (i, 0)), pl.BlockSpec((mlen, 2 * MEM_W), lambda i: (0, 0))],
                          out_specs=pl.BlockSpec((tm, MEM_W), lambda i: (i, 0)),
                          out_shape=jax.ShapeDtypeStruct((s, MEM_W), bf16), compiler_params=_params("parallel"))(qm, kv)


def memattn_bwd(qm, kv, dout, *, name):
    s = qm.shape[0]
    mlen = kv.shape[0]
    tm = _tile(s, 512, 16)

    def body(q_ref, kv_ref, do_ref, dq_ref, dkv_ref):
        @pl.when(pl.program_id(0) == 0)
        def _():
            dkv_ref[...] = jnp.zeros_like(dkv_ref)

        for h in range(MEM_HEADS):
            lo = h * MEM_HEAD_DIM
            q = q_ref[:, lo:lo + MEM_HEAD_DIM]
            k = kv_ref[:, lo:lo + MEM_HEAD_DIM]
            v = kv_ref[:, MEM_W + lo:MEM_W + lo + MEM_HEAD_DIM]
            do = do_ref[:, lo:lo + MEM_HEAD_DIM]
            p = _softmax_rows(_d16(q, k, NT) * MEM_SCALE)
            dp = _d16(do, v, NT)
            ds = p * (dp - jnp.sum(p * dp, axis=-1, keepdims=True)) * MEM_SCALE
            dq_ref[:, lo:lo + MEM_HEAD_DIM] = _d16(ds, k, NN).astype(bf16)
            dkv_ref[:, lo:lo + MEM_HEAD_DIM] += _d16(ds, q, TN)
            dkv_ref[:, MEM_W + lo:MEM_W + lo + MEM_HEAD_DIM] += _d16(p, do, TN)

    row = pl.BlockSpec((tm, MEM_W), lambda i: (i, 0))
    full = pl.BlockSpec((mlen, 2 * MEM_W), lambda i: (0, 0))
    return pl.pallas_call(body, name=name, grid=(s // tm,), in_specs=[row, full, row], out_specs=[row, full],
                          out_shape=[jax.ShapeDtypeStruct((s, MEM_W), bf16), jax.ShapeDtypeStruct((mlen, 2 * MEM_W), f32)],
                          compiler_params=_params("arbitrary"))(qm, kv, dout)


SWA_SCALE = SWA_HEAD_DIM ** -0.5
SWA_GROUP = SWA_HEADS // SWA_KV_HEADS
SWA_IN_W = 1408
K_COL, V_COL, BA_COL = SWA_Q // 128, SWA_Q // 128 + 1, SWA_Q // 128 + 2


def _swa_mask(n):
    qi = lax.broadcasted_iota(jnp.int32, (BLOCK, 2 * BLOCK), 0)
    kj = lax.broadcasted_iota(jnp.int32, (BLOCK, 2 * BLOCK), 1)
    dist = qi + BLOCK - kj
    return (dist >= 0) & (dist < BLOCK) & ((n > 0) | (kj >= BLOCK))


def _swa_probs(q, k, bias, sink, mask):
    heads = range(len(q))
    sc = [jnp.where(mask, _d16(q[h], k[h], NT) * SWA_SCALE + bias[h], NEG_INF) for h in heads]
    m = [jnp.maximum(jnp.max(sc[h], axis=-1, keepdims=True), sink[h]) for h in heads]
    e = [jnp.exp(sc[h] - m[h]) for h in heads]
    es = [jnp.exp(sink[h] - m[h]) for h in heads]
    inv = [1.0 / (jnp.sum(e[h], axis=-1, keepdims=True) + es[h]) for h in heads]
    return e, es, inv


def _swa_heads(ref):
    return [ref[:, h * SWA_HEAD_DIM:(h + 1) * SWA_HEAD_DIM] for h in range(SWA_HEADS)]


def _swa_kv_of_heads(band):
    kv = [band[:, g * SWA_HEAD_DIM:(g + 1) * SWA_HEAD_DIM] for g in range(SWA_KV_HEADS)]
    return [kv[h // SWA_GROUP] for h in range(SWA_HEADS)]


def _swa_specs():
    q_spec = pl.BlockSpec((BLOCK, SWA_Q), lambda n: (n, 0))

    def band(col):
        return [pl.BlockSpec((BLOCK, SWA_KV), lambda n: (jnp.maximum(n - 1, 0), col)),
                pl.BlockSpec((BLOCK, SWA_KV), lambda n: (n, col))]

    bias_spec = pl.BlockSpec((SWA_HEADS, BLOCK, 2 * BLOCK), lambda n: (0, 0, 0))
    sink_spec = pl.BlockSpec((1, SWA_HEADS), lambda n: (0, 0))
    return [q_spec] + band(K_COL) + band(V_COL) + [bias_spec, sink_spec]


def swa_fwd(swa_in, bias, sinks, *, name, side=None):
    s = swa_in.shape[0]

    def body(q_ref, kp_ref, kc_ref, vp_ref, vc_ref, bias_ref, sink_ref, o_ref):
        mask = _swa_mask(pl.program_id(0))
        kb = jnp.concatenate([kp_ref[...], kc_ref[...]], axis=0)
        vb = jnp.concatenate([vp_ref[...], vc_ref[...]], axis=0)
        heads = range(SWA_HEADS)
        k, v = _swa_kv_of_heads(kb), _swa_kv_of_heads(vb)
        e, _, inv = _swa_probs(_swa_heads(q_ref), k, [bias_ref[h] for h in heads], [sink_ref[:, h:h + 1] for h in heads], mask)
        outs = [_d16(e[h] * inv[h], v[h], NN) for h in heads]
        for h in heads:
            o_ref[:, h * SWA_HEAD_DIM:(h + 1) * SWA_HEAD_DIM] = outs[h].astype(bf16)

    return _call(body, (swa_in, swa_in, swa_in, swa_in, swa_in, bias, sinks), name=name, grid=(s // BLOCK,),
                 in_specs=_swa_specs(), out_specs=pl.BlockSpec((BLOCK, SWA_Q), lambda n: (n, 0)),
                 out_shape=jax.ShapeDtypeStruct((s, SWA_Q), bf16), semantics=("parallel",), side=side)


def swa_bwd(swa_in, bias, sinks, dout, *, name, side=None):
    s = swa_in.shape[0]

    def body(q_ref, kp_ref, kc_ref, vp_ref, vc_ref, bias_ref, sink_ref, do_ref,
             dq_ref, dkc_ref, dkp_ref, dvc_ref, dvp_ref, dbias_ref, dsink_ref):
        n = pl.program_id(0)

        @pl.when(n == 0)
        def _():
            dbias_ref[...] = jnp.zeros_like(dbias_ref)
            dsink_ref[...] = jnp.zeros_like(dsink_ref)

        mask = _swa_mask(n)
        kb = jnp.concatenate([kp_ref[...], kc_ref[...]], axis=0)
        vb = jnp.concatenate([vp_ref[...], vc_ref[...]], axis=0)
        lane = lax.broadcasted_iota(jnp.int32, (1, 128), 1)
        hs = range(SWA_HEADS)
        q, do = _swa_heads(q_ref), _swa_heads(do_ref)
        k, v = _swa_kv_of_heads(kb), _swa_kv_of_heads(vb)
        e, es, inv = _swa_probs(q, k, [bias_ref[h] for h in hs], [sink_ref[:, h:h + 1] for h in hs], mask)
        p = [e[h] * inv[h] for h in hs]
        dp = [_d16(do[h], v[h], NT) for h in hs]
        delta = [jnp.sum(p[h] * dp[h], axis=-1, keepdims=True) for h in hs]
        ds = [p[h] * (dp[h] - delta[h]) for h in hs]
        dss = [ds[h] * SWA_SCALE for h in hs]
        dq = [_d16(dss[h], k[h], NN) for h in hs]
        dks = [_d16(dss[h], q[h], TN) for h in hs]
        dvs = [_d16(p[h], do[h], TN) for h in hs]
        dsink = jnp.zeros((1, 128), f32)
        for h in hs:
            dbias_ref[h] += ds[h]
            dq_ref[:, h * SWA_HEAD_DIM:(h + 1) * SWA_HEAD_DIM] = dq[h]
            dsink = dsink + jnp.where(lane == h, -jnp.sum(es[h] * inv[h] * delta[h], axis=0, keepdims=True), 0.0)
        for g in range(SWA_KV_HEADS):
            kl = g * SWA_HEAD_DIM
            group = range(g * SWA_GROUP, (g + 1) * SWA_GROUP)
            dk = functools.reduce(lambda a, b: a + b, [dks[h] for h in group])
            dv = functools.reduce(lambda a, b: a + b, [dvs[h] for h in group])
            dkp_ref[:, kl:kl + SWA_HEAD_DIM] = dk[:BLOCK]
            dkc_ref[:, kl:kl + SWA_HEAD_DIM] = dk[BLOCK:]
            dvp_ref[:, kl:kl + SWA_HEAD_DIM] = dv[:BLOCK]
            dvc_ref[:, kl:kl + SWA_HEAD_DIM] = dv[BLOCK:]
        dsink_ref[...] += dsink

    qs = pl.BlockSpec((BLOCK, SWA_Q), lambda n: (n, 0))
    ks = pl.BlockSpec((BLOCK, SWA_KV), lambda n: (n, 0))
    return _call(
        body, (swa_in, swa_in, swa_in, swa_in, swa_in, bias, sinks, dout), name=name, grid=(s // BLOCK,),
        in_specs=_swa_specs() + [qs],
        out_specs=[qs, ks, ks, ks, ks, pl.BlockSpec((SWA_HEADS, BLOCK, 2 * BLOCK), lambda n: (0, 0, 0)),
                   pl.BlockSpec((1, 128), lambda n: (0, 0))],
        out_shape=[jax.ShapeDtypeStruct((s, SWA_Q), f32)] + [jax.ShapeDtypeStruct((s, SWA_KV), f32)] * 4
        + [jax.ShapeDtypeStruct((SWA_HEADS, BLOCK, 2 * BLOCK), f32), jax.ShapeDtypeStruct((1, 128), f32)],
        semantics=("arbitrary",), side=side)


def swa_in_grad(dq, dkc, dkp, dvc, dvp, dba, *, name):
    s = dq.shape[0]
    nb = s // BLOCK

    def body(dq_ref, dkc_ref, dkp_ref, dvc_ref, dvp_ref, dba_ref, o_ref):
        has_next = pl.program_id(0) < nb - 1
        o_ref[:, :SWA_Q] = dq_ref[...].astype(bf16)
        o_ref[:, SWA_Q:SWA_Q + SWA_KV] = (dkc_ref[...] + jnp.where(has_next, dkp_ref[...], 0.0)).astype(bf16)
        o_ref[:, SWA_Q + SWA_KV:SWA_Q + 2 * SWA_KV] = (dvc_ref[...] + jnp.where(has_next, dvp_ref[...], 0.0)).astype(bf16)
        o_ref[:, SWA_Q + 2 * SWA_KV:] = dba_ref[...].astype(bf16)

    cur = pl.BlockSpec((BLOCK, SWA_KV), lambda n: (n, 0))
    nxt = pl.BlockSpec((BLOCK, SWA_KV), lambda n: (jnp.minimum(n + 1, nb - 1), 0))
    return pl.pallas_call(body, name=name, grid=(nb,),
                          in_specs=[pl.BlockSpec((BLOCK, SWA_Q), lambda n: (n, 0)), cur, nxt, cur, nxt, cur],
                          out_specs=pl.BlockSpec((BLOCK, SWA_IN_W), lambda n: (n, 0)),
                          out_shape=jax.ShapeDtypeStruct((s, SWA_IN_W), bf16),
                          compiler_params=_params("parallel"))(dq, dkc, dkp, dvc, dvp, dba)


def _bucket_onehot():
    qi = jnp.arange(BLOCK)[:, None]
    kj = jnp.arange(2 * BLOCK)[None, :]
    dist = jnp.maximum(qi + BLOCK - kj, 0)
    max_exact = REL_BUCKETS // 2
    dd = jnp.maximum(dist, 1).astype(f32)
    large = max_exact + (jnp.log(dd / max_exact) / math.log(REL_MAX_DIST / max_exact) * (REL_BUCKETS - max_exact)).astype(jnp.int32)
    bucket = jnp.where(dist < max_exact, dist, jnp.minimum(large, REL_BUCKETS - 1)).reshape(-1)
    return (bucket[None, :] == jnp.arange(REL_BUCKETS)[:, None]).astype(f32)


def _gbeta_fn(ba, alog_row, dt_row):
    col = lax.broadcasted_iota(jnp.int32, ba.shape, 1)
    v = ba + dt_row
    softplus = jnp.maximum(v, 0.0) + jnp.log(1.0 + jnp.exp(-jnp.abs(v)))
    g = -jnp.exp(alog_row) * softplus
    return jnp.where(col < GDN_HEADS, _sigmoid(ba), jnp.where(col < 2 * GDN_HEADS, g, 0.0))


def gbeta_fwd(swa_in, alog_row, dt_row, *, name):
    s = swa_in.shape[0]
    tm = _tile(s, 512, 8)

    def body(ba_ref, a_ref, d_ref, o_ref):
        o_ref[...] = _gbeta_fn(ba_ref[...], a_ref[...], d_ref[...])

    vec = pl.BlockSpec((1, 128), lambda i: (0, 0))
    return pl.pallas_call(body, name=name, grid=(s // tm,), in_specs=[pl.BlockSpec((tm, 128), lambda i: (i, BA_COL)), vec, vec],
                          out_specs=pl.BlockSpec((tm, 128), lambda i: (i, 0)), out_shape=jax.ShapeDtypeStruct((s, 128), f32),
                          compiler_params=_params("parallel"))(swa_in, alog_row, dt_row)


def gbeta_bwd(swa_in, alog_row, dt_row, dgbeta, *, name):
    s = swa_in.shape[0]
    tm = _tile(s, 512, 8)

    def body(ba_ref, a_ref, d_ref, dgb_ref, dba_ref, da_ref, dd_ref):
        @pl.when(pl.program_id(0) == 0)
        def _():
            da_ref[...] = jnp.zeros_like(da_ref)
            dd_ref[...] = jnp.zeros_like(dd_ref)

        _, pull = jax.vjp(_gbeta_fn, ba_ref[...], a_ref[...], d_ref[...])
        dba, da, dd = pull(dgb_ref[...])
        dba_ref[...] = dba
        da_ref[...] += da
        dd_ref[...] += dd

    vec = pl.BlockSpec((1, 128), lambda i: (0, 0))
    row = pl.BlockSpec((tm, 128), lambda i: (i, 0))
    return pl.pallas_call(body, name=name, grid=(s // tm,),
                          in_specs=[pl.BlockSpec((tm, 128), lambda i: (i, BA_COL)), vec, vec, row], out_specs=[row, vec, vec],
                          out_shape=[jax.ShapeDtypeStruct((s, 128), f32), jax.ShapeDtypeStruct((1, 128), f32),
                                     jax.ShapeDtypeStruct((1, 128), f32)],
                          compiler_params=_params("arbitrary"))(swa_in, alog_row, dt_row, dgbeta)


QKV_W = 3 * GDN_W


def gdn_pre_fwd(gdn_in, convw, *, name):
    s = gdn_in.shape[0]
    tm = _tile(s, 256, 16)
    hb = tm // HALO

    def body(x_ref, xp_ref, w_ref, q_ref, k_ref, v_ref, pre_ref):
        prev = jnp.where(pl.program_id(0) == 0, 0.0, xp_ref[...])
        pre = _causal_conv(_conv_taps(jnp.concatenate([prev, x_ref[...]], axis=0), GDN_CONV), w_ref)
        pre_ref[...] = pre
        act = pre * _sigmoid(pre)
        for h in range(GDN_HEADS):
            lo = h * GDN_HEAD_DIM
            for off, o_ref in ((0, q_ref), (GDN_W, k_ref)):
                seg = act[:, off + lo:off + lo + GDN_HEAD_DIM]
                o_ref[:, lo:lo + GDN_HEAD_DIM] = seg * lax.rsqrt(jnp.sum(seg * seg, axis=-1, keepdims=True) + 1e-6)
        v_ref[...] = act[:, 2 * GDN_W:]

    out = pl.BlockSpec((tm, GDN_W), lambda i: (i, 0))
    return pl.pallas_call(body, name=name, grid=(s // tm,),
                          in_specs=[pl.BlockSpec((tm, QKV_W), lambda i: (i, 0)),
                                    pl.BlockSpec((HALO, QKV_W), lambda i: (jnp.maximum(i * hb - 1, 0), 0)),
                                    pl.BlockSpec((8, QKV_W), lambda i: (0, 0))],
                          out_specs=[out, out, out, pl.BlockSpec((tm, QKV_W), lambda i: (i, 0))],
                          out_shape=[jax.ShapeDtypeStruct((s, GDN_W), f32)] * 3 + [jax.ShapeDtypeStruct((s, QKV_W), f32)],
                          compiler_params=_params("parallel"))(gdn_in, gdn_in, convw)


def gdn_pre_bwd(gdn_in, conv_out, convw, dqn, dkn, dv, dgz, *, name):
    s = gdn_in.shape[0]
    tm = _tile(s, 128, 16)
    hb = tm // HALO
    nt = s // tm
    last_hb = s // HALO - 1

    def body(x_ref, pre_ref, pren_ref, w_ref, dq_ref, dqx_ref, dk_ref, dkx_ref, dv_ref, dvx_ref, dz_ref, o_ref, dw_ref):
        i = pl.program_id(0)
        last = i == nt - 1

        @pl.when(i == 0)
        def _():
            dw_ref[...] = jnp.zeros_like(dw_ref)

        pre = jnp.concatenate([pre_ref[...], pren_ref[...]], axis=0)
        act, dact_dpre = _silu_and_grad(pre)

        def with_future(t_ref, n_ref):
            return jnp.concatenate([t_ref[...], jnp.where(last, 0.0, n_ref[...])], axis=0)

        dqe, dke, dve = with_future(dq_ref, dqx_ref), with_future(dk_ref, dkx_ref), with_future(dv_ref, dvx_ref)
        parts = []
        for off, dn in ((0, dqe), (GDN_W, dke)):
            for h in range(GDN_HEADS):
                lo = h * GDN_HEAD_DIM
                seg = act[:, off + lo:off + lo + GDN_HEAD_DIM]
                r = lax.rsqrt(jnp.sum(seg * seg, axis=-1, keepdims=True) + 1e-6)
                nrm = seg * r
                dseg = dn[:, lo:lo + GDN_HEAD_DIM]
                parts.append(r * (dseg - nrm * jnp.sum(dseg * nrm, axis=-1, keepdims=True)))
        dpre = jnp.concatenate(parts + [dve], axis=1) * dact_dpre
        dx, dw = _conv_grads(dpre, x_ref[...], w_ref, GDN_CONV, tm)
        o_ref[:, :QKV_W] = dx.astype(bf16)
        o_ref[:, QKV_W:] = dz_ref[...].astype(bf16)
        dw_ref[...] += _rows_to_block(dw, 8, QKV_W)

    row = pl.BlockSpec((tm, GDN_W), lambda i: (i, 0))
    fut = pl.BlockSpec((HALO, GDN_W), lambda i: (jnp.minimum((i + 1) * hb, last_hb), 0))
    wide = pl.BlockSpec((tm, QKV_W), lambda i: (i, 0))
    return pl.pallas_call(
        body, name=name, grid=(nt,),
        in_specs=[wide, wide, pl.BlockSpec((HALO, QKV_W), lambda i: (jnp.minimum((i + 1) * hb, last_hb), 0)),
                  pl.BlockSpec((8, QKV_W), lambda i: (0, 0)), row, fut, row, fut, row, fut, row],
        out_specs=[pl.BlockSpec((tm, 4 * GDN_W), lambda i: (i, 0)), pl.BlockSpec((8, QKV_W), lambda i: (0, 0))],
        out_shape=[jax.ShapeDtypeStruct((s, 4 * GDN_W), bf16), jax.ShapeDtypeStruct((8, QKV_W), f32)],
        compiler_params=_params("arbitrary"),
    )(gdn_in, conv_out, conv_out, convw, dqn, dqn, dkn, dkn, dv, dv, dgz)


def _gdn_post_head(o, z, nw):
    return o * lax.rsqrt(jnp.mean(o * o, axis=-1, keepdims=True) + 1e-6) * nw * (z * _sigmoid(z))


def gdn_post_fwd(o, gdn_in, nw, *, name):
    s = o.shape[0]
    tm = _tile(s, 256, 16)

    def body(o_ref, z_ref, nw_ref, y_ref):
        for h in range(GDN_HEADS):
            sl = slice(h * GDN_HEAD_DIM, (h + 1) * GDN_HEAD_DIM)
            y_ref[:, sl] = _gdn_post_head(o_ref[:, sl], z_ref[:, sl], nw_ref[...]).astype(bf16)

    row = pl.BlockSpec((tm, GDN_W), lambda i: (i, 0))
    return pl.pallas_call(body, name=name, grid=(s // tm,),
                          in_specs=[row, pl.BlockSpec((tm, GDN_W), lambda i: (i, 3)), pl.BlockSpec((1, 128), lambda i: (0, 0))],
                          out_specs=row, out_shape=jax.ShapeDtypeStruct((s, GDN_W), bf16),
                          compiler_params=_params("parallel"))(o, gdn_in, nw)


def gdn_post_bwd(o, gdn_in, nw, dy, *, name):
    s = o.shape[0]
    tm = _tile(s, 256, 16)

    def body(o_ref, z_ref, nw_ref, dy_ref, do_ref, dz_ref, dnw_ref):
        @pl.when(pl.program_id(0) == 0)
        def _():
            dnw_ref[...] = jnp.zeros_like(dnw_ref)

        dnw = jnp.zeros((1, 128), f32)
        for h in range(GDN_HEADS):
            sl = slice(h * GDN_HEAD_DIM, (h + 1) * GDN_HEAD_DIM)
            _, pull = jax.vjp(_gdn_post_head, o_ref[:, sl], z_ref[:, sl], nw_ref[...])
            do, dz, dn = pull(dy_ref[:, sl])
            do_ref[:, sl] = do
            dz_ref[:, sl] = dz
            dnw = dnw + dn
        dnw_ref[...] += dnw

    row = pl.BlockSpec((tm, GDN_W), lambda i: (i, 0))
    vec = pl.BlockSpec((1, 128), lambda i: (0, 0))
    return pl.pallas_call(body, name=name, grid=(s // tm,),
                          in_specs=[row, pl.BlockSpec((tm, GDN_W), lambda i: (i, 3)), vec, row], out_specs=[row, row, vec],
                          out_shape=[jax.ShapeDtypeStruct((s, GDN_W), f32), jax.ShapeDtypeStruct((s, GDN_W), f32),
                                     jax.ShapeDtypeStruct((1, 128), f32)],
                          compiler_params=_params("arbitrary"))(o, gdn_in, nw, dy)


def _dot_high(a, b, dims=NN):
    return lax.dot_general(a, b, (dims, ((), ())), precision=lax.Precision.HIGH, preferred_element_type=f32)


@jax.custom_vjp
def _unit_lower_inverses(a):
    c = a[0].shape[0]
    n = range(len(a))
    eye = (lax.broadcasted_iota(jnp.int32, (c, c), 0) == lax.broadcasted_iota(jnp.int32, (c, c), 1)).astype(f32)
    inv = [eye - a[i] for i in n]
    pw = [_dot_high(a[i], a[i]) for i in n]
    width = 2
    while width < c:
        inv = [inv[i] + _dot_high(inv[i], pw[i]) for i in n]
        width *= 2
        if width < c:
            pw = [_dot_high(pw[i], pw[i]) for i in n]
    return inv


def _unit_lower_inverses_fwd(a):
    inv = _unit_lower_inverses(a)
    return inv, inv


def _unit_lower_inverses_bwd(inv, g):
    return ([-_dot_high(_dot_high(x, gx, TN), x, NT) for x, gx in zip(inv, g)],)


_unit_lower_inverses.defvjp(_unit_lower_inverses_fwd, _unit_lower_inverses_bwd)


@jax.custom_vjp
def _known_inverses(a, inv):
    return inv


_known_inverses.defvjp(lambda a, inv: (inv, inv),
                       lambda inv, g: (_unit_lower_inverses_bwd(inv, g)[0], [jnp.zeros_like(x) for x in inv]))


def _gdn_chunks(q, k, v, gb, state, kept_inverses=None):
    c = GDN_CHUNK
    heads = range(len(q))
    r = lax.broadcasted_iota(jnp.int32, (c, c), 0)
    cc = lax.broadcasted_iota(jnp.int32, (c, c), 1)
    tril, strict = r >= cc, r > cc
    eye = (r == cc).astype(f32)

    def dhi(a, b):
        return jnp.dot(a, b, precision=lax.Precision.HIGH, preferred_element_type=f32)

    beta = [gb[:, h:h + 1] for h in heads]
    cum_cols = dhi(tril.astype(f32), gb)
    cum_rows = dhi(gb.T, (r <= cc).astype(f32))
    gi = [jnp.broadcast_to(cum_cols[:, GDN_HEADS + h:GDN_HEADS + h + 1], (c, c)) for h in heads]
    gj = [jnp.broadcast_to(cum_rows[GDN_HEADS + h:GDN_HEADS + h + 1, :], (c, c)) for h in heads]
    decay = [jnp.where(tril, jnp.exp(jnp.where(tril, gi[h] - gj[h], 0.0)), 0.0) for h in heads]
    kb = [k[h] * beta[h] for h in heads]
    vb = [v[h] * beta[h] for h in heads]
    a = [jnp.where(strict, _d16(kb[h], k[h], NT) * decay[h], 0.0) for h in heads]
    tinv = _unit_lower_inverses(a) if kept_inverses is None else _known_inverses(a, kept_inverses)
    gc = [gi[h][:, 0:1] for h in heads]
    egc = [jnp.exp(gc[h]) for h in heads]
    u = [dhi(tinv[h], vb[h]) for h in heads]
    w = [dhi(tinv[h], kb[h] * egc[h]) for h in heads]
    qs = [q[h] * (GDN_HEAD_DIM ** -0.5) for h in heads]
    attn = [jnp.where(tril, _d16(qs[h], k[h], NT) * decay[h], 0.0) for h in heads]
    g_last = [gi[h][c - 1:c, 0:1] for h in heads]
    v_new = [u[h] - _d16(w[h], state[h], NN) for h in heads]
    out = [_d16(qs[h] * egc[h], state[h], NN) + _d16(attn[h], v_new[h], NN) for h in heads]
    new_state = [state[h] * jnp.exp(g_last[h]) + _d16(k[h] * jnp.exp(g_last[h] - gc[h]), v_new[h], TN) for h in heads]
    return out, new_state, tinv


def _head_cols(ref):
    return [ref[:, h * GDN_HEAD_DIM:(h + 1) * GDN_HEAD_DIM] for h in range(GDN_HEADS)]


def gdn_scan_fwd(qn, kn, v, gbeta, *, name, side=None):
    s = qn.shape[0]
    nc = s // GDN_CHUNK

    def body(q_ref, k_ref, v_ref, gb_ref, o_ref, st_ref, inv_ref, state_ref):
        @pl.when(pl.program_id(0) == 0)
        def _():
            state_ref[...] = jnp.zeros_like(state_ref)

        states = [state_ref[h] for h in range(GDN_HEADS)]
        outs, new, inverses = _gdn_chunks(_head_cols(q_ref), _head_cols(k_ref), _head_cols(v_ref), gb_ref[...], states)
        for h in range(GDN_HEADS):
            st_ref[0, h] = states[h]
            inv_ref[0, h] = inverses[h]
            o_ref[:, h * GDN_HEAD_DIM:(h + 1) * GDN_HEAD_DIM] = outs[h]
            state_ref[h] = new[h]

    row = pl.BlockSpec((GDN_CHUNK, GDN_W), lambda n: (n, 0))
    return _call(
        body, (qn, kn, v, gbeta), name=name, grid=(nc,),
        in_specs=[row, row, row, pl.BlockSpec((GDN_CHUNK, 128), lambda n: (n, 0))],
        out_specs=[row, pl.BlockSpec((1, GDN_HEADS, GDN_HEAD_DIM, GDN_HEAD_DIM), lambda n: (n, 0, 0, 0)),
                   pl.BlockSpec((1, GDN_HEADS, GDN_CHUNK, GDN_CHUNK), lambda n: (n, 0, 0, 0))],
        out_shape=[jax.ShapeDtypeStruct((s, GDN_W), f32),
                   jax.ShapeDtypeStruct((nc, GDN_HEADS, GDN_HEAD_DIM, GDN_HEAD_DIM), f32),
                   jax.ShapeDtypeStruct((nc, GDN_HEADS, GDN_CHUNK, GDN_CHUNK), f32)],
        scratch_shapes=[pltpu.VMEM((GDN_HEADS, GDN_HEAD_DIM, GDN_HEAD_DIM), f32)], semantics=("arbitrary",), side=side)


def gdn_scan_bwd(qn, kn, v, gbeta, states, inverses, dout, *, name, side=None):
    s = qn.shape[0]
    nc = s // GDN_CHUNK

    def body(q_ref, k_ref, v_ref, gb_ref, st_ref, inv_ref, do_ref, dq_ref, dk_ref, dv_ref, dgb_ref, dstate_ref):
        @pl.when(pl.program_id(0) == 0)
        def _():
            dstate_ref[...] = jnp.zeros_like(dstate_ref)

        kept = [inv_ref[0, h] for h in range(GDN_HEADS)]
        _, pull = jax.vjp(lambda *args: _gdn_chunks(*args, kept_inverses=kept)[:2],
                          _head_cols(q_ref), _head_cols(k_ref), _head_cols(v_ref), gb_ref[...],
                          [st_ref[0, h] for h in range(GDN_HEADS)])
        dq, dk, dv, dgb, dst = pull((_head_cols(do_ref), [dstate_ref[h] for h in range(GDN_HEADS)]))
        for h in range(GDN_HEADS):
            sl = slice(h * GDN_HEAD_DIM, (h + 1) * GDN_HEAD_DIM)
            dq_ref[:, sl] = dq[h]
            dk_ref[:, sl] = dk[h]
            dv_ref[:, sl] = dv[h]
            dstate_ref[h] = dst[h]
        dgb_ref[...] = dgb

    row = pl.BlockSpec((GDN_CHUNK, GDN_W), lambda n: (nc - 1 - n, 0))
    gb = pl.BlockSpec((GDN_CHUNK, 128), lambda n: (nc - 1 - n, 0))
    return _call(
        body, (qn, kn, v, gbeta, states, inverses, dout), name=name, grid=(nc,),
        in_specs=[row, row, row, gb, pl.BlockSpec((1, GDN_HEADS, GDN_HEAD_DIM, GDN_HEAD_DIM), lambda n: (nc - 1 - n, 0, 0, 0)),
                  pl.BlockSpec((1, GDN_HEADS, GDN_CHUNK, GDN_CHUNK), lambda n: (nc - 1 - n, 0, 0, 0)), row],
        out_specs=[row, row, row, gb],
        out_shape=[jax.ShapeDtypeStruct((s, GDN_W), f32)] * 3 + [jax.ShapeDtypeStruct((s, 128), f32)],
        scratch_shapes=[pltpu.VMEM((GDN_HEADS, GDN_HEAD_DIM, GDN_HEAD_DIM), f32)], semantics=("arbitrary",), side=side)


def _adamw_update(w, g, m, v):
    nm = ADAM_B1 * m + (1.0 - ADAM_B1) * g
    nv = ADAM_B2 * v + (1.0 - ADAM_B2) * (g * g)
    m_hat = nm / (1.0 - ADAM_B1 ** ADAM_STEP)
    v_hat = nv / (1.0 - ADAM_B2 ** ADAM_STEP)
    return -ADAM_LR * (m_hat / (jnp.sqrt(v_hat) + ADAM_EPS) + ADAM_WD * w), nm, nv


def adamw(w, g, m, v, *, name):
    r, c = w.shape
    tr = _tile(r, 256, 8)

    def body(w_ref, g_ref, m_ref, v_ref, d_ref, nm_ref, nv_ref):
        d_ref[...], nm_ref[...], nv_ref[...] = _adamw_update(w_ref[...], g_ref[...], m_ref[...], v_ref[...])

    spec = pl.BlockSpec((tr, c), lambda i: (i, 0))
    return pl.pallas_call(body, name=name, grid=(r // tr,), in_specs=[spec] * 4, out_specs=[spec] * 3,
                          out_shape=[jax.ShapeDtypeStruct((r, c), f32)] * 3, compiler_params=_params("parallel"))(w, g, m, v)


def adamw_of_partial_sums(w, parts, m, v, *, name):
    r, c = w.shape
    cp = parts.shape[2]
    tr = _tile(r, 256, 16)

    def body(w_ref, p_ref, m_ref, v_ref, g_ref, d_ref, nm_ref, nv_ref):
        part = [p_ref[k, :, :c].astype(f32) for k in range(4)]
        g = ((part[3] + part[0]) + part[1]) + part[2]
        g_ref[...] = g
        d_ref[...], nm_ref[...], nv_ref[...] = _adamw_update(w_ref[...], g, m_ref[...], v_ref[...])

    spec = pl.BlockSpec((tr, c), lambda i: (i, 0))
    return pl.pallas_call(body, name=name, grid=(r // tr,),
                          in_specs=[spec, pl.BlockSpec((4, tr, cp), lambda i: (0, i, 0)), spec, spec], out_specs=[spec] * 4,
                          out_shape=[jax.ShapeDtypeStruct((r, c), f32)] * 4,
                          compiler_params=_params("parallel"))(w, parts, m, v)


def _pos():
    return lax.axis_index("x"), lax.axis_index("y"), lax.axis_index("c")


ANY = pl.BlockSpec(memory_space=pl.ANY)


class Side(NamedTuple):
    ins: list
    outs: list
    aliases: dict
    sems: list
    start: Callable
    wait: Callable


def join_sides(*sides):
    def spans(key):
        out, off = [], 0
        for sd in sides:
            out.append(slice(off, off + len(getattr(sd, key))))
            off += len(getattr(sd, key))
        return out

    i_sp, o_sp, s_sp = spans("ins"), spans("outs"), spans("sems")
    aliases = {i_sp[n].start + i: o_sp[n].start + o for n, sd in enumerate(sides) for i, o in sd.aliases.items()}

    def each(what):
        def run(ins, outs, sems):
            for n, sd in enumerate(sides):
                getattr(sd, what)(ins[i_sp[n]], outs[o_sp[n]], sems[s_sp[n]])
        return run

    return Side([a for sd in sides for a in sd.ins], [o for sd in sides for o in sd.outs], aliases,
                [s for sd in sides for s in sd.sems], each("start"), each("wait"))


def _side_body(body, side, n_in, n_out, n_scratch, grid):
    ns_in, ns_out = len(side.ins), len(side.outs)

    def wrapped(*refs):
        cut = [n_in, ns_in, n_out, ns_out, n_scratch]
        parts, off = [], 0
        for c in cut:
            parts.append(refs[off:off + c])
            off += c
        ins, s_ins, outs, s_outs, scratch = parts
        sems = refs[off:]
        if grid:
            ids = [pl.program_id(d) for d in range(len(grid))]
            first = functools.reduce(jnp.logical_and, [i == 0 for i in ids])
            last = functools.reduce(jnp.logical_and, [i == g - 1 for i, g in zip(ids, grid)])
            pl.when(first)(lambda: side.start(s_ins, s_outs, sems))
            body(*ins, *outs, *scratch)
            pl.when(last)(lambda: side.wait(s_ins, s_outs, sems))
        else:
            side.start(s_ins, s_outs, sems)
            side.wait(s_ins, s_outs, sems)

    return wrapped


def _call(body, args, *, name, grid, in_specs, out_specs, out_shape, semantics, scratch_shapes=(), side=None):
    if side is None:
        return pl.pallas_call(body, name=name, grid=grid, in_specs=in_specs, out_specs=out_specs, out_shape=out_shape,
                              scratch_shapes=list(scratch_shapes), compiler_params=_params(*semantics))(*args)
    single = not isinstance(out_shape, (list, tuple))
    shapes, specs = ([out_shape], [out_specs]) if single else (list(out_shape), list(out_specs))
    n_in, n_out = len(in_specs), len(shapes)
    res = pl.pallas_call(
        _side_body(body, side, n_in, n_out, len(scratch_shapes), grid), name=name, grid=grid,
        in_specs=list(in_specs) + [ANY] * len(side.ins), out_specs=specs + [ANY] * len(side.outs),
        out_shape=shapes + list(side.outs), scratch_shapes=list(scratch_shapes) + list(side.sems),
        input_output_aliases={n_in + i: n_out + o for i, o in side.aliases.items()},
        compiler_params=_params(*(["arbitrary"] * len(grid))),
    )(*args, *side.ins)
    return (res[0] if single else res[:n_out]), list(res[n_out:])


def run_side(side, *, name):
    return pl.pallas_call(_side_body(None, side, 0, 0, 0, ()), name=name, in_specs=[ANY] * len(side.ins),
                          out_specs=[ANY] * len(side.outs), out_shape=list(side.outs), scratch_shapes=list(side.sems),
                          input_output_aliases=dict(side.aliases))(*side.ins)


def _remote(src, dst, send, recv, k, to):
    return pltpu.make_async_remote_copy(src_ref=src, dst_ref=dst, send_sem=send.at[k], recv_sem=recv.at[k], device_id=to,
                                        device_id_type=MESH)


def gather_first(shards):
    na = len(shards)

    def copies(x_refs, out_refs, sems):
        send, recv, local = sems
        x, y, cc = _pos()
        me = 4 * x + 2 * y + cc
        peers = [(x, y, 1 - cc), (1 - x, y, cc), (x, 1 - y, cc), (1 - x, 1 - y, cc)]
        mine = [pltpu.make_async_copy(x_refs[a], out_refs[a].at[me], local.at[a]) for a in range(na)]
        sent = [_remote(x_refs[a], out_refs[a].at[me], send, recv, 4 * a + k, p) for a in range(na) for k, p in enumerate(peers)]
        landed = [_remote(x_refs[a], out_refs[a].at[4 * p[0] + 2 * p[1] + p[2]], send, recv, 4 * a + k, p)
                  for a in range(na) for k, p in enumerate(peers)]
        return mine, sent, landed

    def start(x_refs, out_refs, sems):
        mine, sent, _ = copies(x_refs, out_refs, sems)
        for cp in mine + sent:
            cp.start()

    def wait(x_refs, out_refs, sems):
        mine, sent, landed = copies(x_refs, out_refs, sems)
        for cp in sent:
            cp.wait_send()
        for cp in landed:
            cp.wait_recv()
        for cp in mine:
            cp.wait()

    return Side(list(shards), [jax.ShapeDtypeStruct((N_DEV,) + s.shape, s.dtype) for s in shards], {},
                [pltpu.SemaphoreType.DMA((4 * na,)), pltpu.SemaphoreType.DMA((4 * na,)), pltpu.SemaphoreType.DMA((na,))],
                start, wait)


def gather_second(slots):
    na = len(slots)

    def copies(out_refs, sems):
        send, recv = sems
        x, y, cc = _pos()
        chips = [(1 - x, y), (x, 1 - y), (1 - x, 1 - y)]
        sent, landed = [], []
        for a in range(na):
            for j, (px, py) in enumerate(chips):
                row = out_refs[a].at[4 * px + 2 * py + cc]
                sent.append(_remote(row, row, send, recv, 3 * a + j, (x, y, 1 - cc)))
                landed.append(_remote(row, out_refs[a].at[4 * px + 2 * py + 1 - cc], send, recv, 3 * a + j, (x, y, 1 - cc)))
        return sent, landed

    def start(_, out_refs, sems):
        for cp in copies(out_refs, sems)[0]:
            cp.start()

    def wait(_, out_refs, sems):
        sent, landed = copies(out_refs, sems)
        for cp in sent:
            cp.wait_send()
        for cp in landed:
            cp.wait_recv()

    return Side(list(slots), [jax.ShapeDtypeStruct(s.shape, s.dtype) for s in slots], {a: a for a in range(na)},
                [pltpu.SemaphoreType.DMA((3 * na,)), pltpu.SemaphoreType.DMA((3 * na,))], start, wait)


def _rows_of(ref, lead, rows):
    if rows is None:
        return ref if lead is None else ref.at[lead]
    cut = pl.ds(rows[0], rows[1])
    return ref.at[:, cut] if lead is None else ref.at[lead, cut]


def _side_into(arrays, out_shapes, into):
    na = len(arrays)
    if into is None:
        return list(arrays), out_shapes, {}
    return list(arrays) + list(into), out_shapes, {na + a: a for a in range(na)}


def grad_to_sibling(chunks, rows=None, into=None):
    na = len(chunks)

    def start(g_refs, out_refs, sems):
        send, recv = sems
        x, y, cc = _pos()
        for a in range(na):
            for q in range(4):
                _remote(_rows_of(g_refs[a], 2 * q + 1 - cc, rows), _rows_of(out_refs[a], q, rows), send, recv, a,
                        (x, y, 1 - cc)).start()

    def wait(g_refs, out_refs, sems):
        send, recv = sems
        x, y, cc = _pos()
        for a in range(na):
            whole = _rows_of(out_refs[a], None, rows)
            _remote(whole, whole, send, recv, a, (x, y, 1 - cc)).wait()

    ins, outs, aliases = _side_into(chunks, [jax.ShapeDtypeStruct((4,) + g.shape[1:], g.dtype) for g in chunks], into)
    return Side(ins, outs, aliases, [pltpu.SemaphoreType.DMA((na,)), pltpu.SemaphoreType.DMA((na,))], start, wait)


def grad_to_chips(parts, rows=None, into=None):
    na = len(parts)

    def copies(p_refs, out_refs, sems):
        send, recv, local = sems
        x, y, cc = _pos()
        chips = [(1 - x, y), (x, 1 - y), (1 - x, 1 - y)]
        mine = [pltpu.make_async_copy(_rows_of(p_refs[a], 2 * x + y, rows), _rows_of(out_refs[a], 3, rows), local.at[a])
                for a in range(na)]
        sent = [_remote(_rows_of(p_refs[a], 2 * px + py, rows), _rows_of(out_refs[a], k, rows), send, recv, 3 * a + k,
                        (px, py, cc)) for a in range(na) for k, (px, py) in enumerate(chips)]
        return mine, sent

    def start(p_refs, out_refs, sems):
        mine, sent = copies(p_refs, out_refs, sems)
        for cp in mine + sent:
            cp.start()

    def wait(p_refs, out_refs, sems):
        mine, sent = copies(p_refs, out_refs, sems)
        for cp in sent:
            cp.wait()
        for cp in mine:
            cp.wait()

    ins, outs, aliases = _side_into(parts, [jax.ShapeDtypeStruct(p.shape, p.dtype) for p in parts], into)
    return Side(ins, outs, aliases,
                [pltpu.SemaphoreType.DMA((3 * na,)), pltpu.SemaphoreType.DMA((3 * na,)), pltpu.SemaphoreType.DMA((na,))],
                start, wait)


def add_sibling(chunks, recv, *, name):
    _, r, c = chunks.shape
    tr = r if r <= 1024 else _tile(r, 512, 16)
    core = lax.axis_index("c").astype(jnp.int32).reshape(1)

    def body(core_ref, a_ref, b_ref, o_ref):
        o_ref[...] = (a_ref[...] + b_ref[...]).astype(bf16)

    return pl.pallas_call(
        body, name=name,
        grid_spec=pltpu.PrefetchScalarGridSpec(
            num_scalar_prefetch=1, grid=(4, r // tr),
            in_specs=[pl.BlockSpec((1, tr, c), lambda q, i, core_ref: (2 * q + core_ref[0], i, 0)),
                      pl.BlockSpec((1, tr, c), lambda q, i, core_ref: (q, i, 0))],
            out_specs=pl.BlockSpec((1, tr, c), lambda q, i, core_ref: (q, i, 0))),
        out_shape=jax.ShapeDtypeStruct((4, r, c), bf16), compiler_params=_params("parallel", "parallel"),
    )(core, chunks, recv)


def all_reduce_small(vec, *, name):
    r, c = vec.shape

    def body(v_ref, out_ref, buf_ref, send_sems, recv_sems):
        x, y, cc = _pos()
        my_id = 4 * x + 2 * y + cc
        buf_ref[my_id] = v_ref[...]
        flips = [(fx, fy, fc) for fx in (0, 1) for fy in (0, 1) for fc in (0, 1)][1:]
        cps = []
        for k, (fx, fy, fc) in enumerate(flips):
            peer = ((1 - x) if fx else x, (1 - y) if fy else y, (1 - cc) if fc else cc)
            cps.append(pltpu.make_async_remote_copy(src_ref=v_ref, dst_ref=buf_ref.at[my_id], send_sem=send_sems.at[k],
                                                    recv_sem=recv_sems.at[k], device_id=peer, device_id_type=MESH))
        for cp in cps:
            cp.start()
        for cp in cps:
            cp.wait()
        acc = buf_ref[0]
        for d in range(1, N_DEV):
            acc = acc + buf_ref[d]
        out_ref[...] = acc

    vm = pl.BlockSpec(memory_space=pltpu.VMEM)
    return pl.pallas_call(body, name=name, in_specs=[vm], out_specs=vm, out_shape=jax.ShapeDtypeStruct((r, c), f32),
                          scratch_shapes=[pltpu.VMEM((N_DEV, r, c), f32), pltpu.SemaphoreType.DMA((7,)),
                                          pltpu.SemaphoreType.DMA((7,))])(vec)


def _pack(parts, rows, dtype):
    flat = jnp.concatenate([p.reshape(-1).astype(dtype) for p in parts])
    return jnp.pad(flat, (0, rows * PACK_COLS - flat.shape[0])).reshape(rows, PACK_COLS)


def _unpack(flat, shapes):
    out, off = [], 0
    for shp in shapes:
        n = shp[0] * shp[1]
        out.append(flat[..., off:off + n].reshape(flat.shape[:-1] + tuple(shp)))
        off += n
    return out


def _from_column_shards(g):
    _, r, c = g.shape
    return jnp.transpose(g, (1, 0, 2)).reshape(r, N_DEV * c)


def _column_shards(full):
    r, c8 = full.shape
    return jnp.transpose(full.reshape(r, N_DEV, c8 // N_DEV), (1, 0, 2))


W_IN_SHARD = IN_DIM // N_DEV
W_IN_PAD = 1280
W_IN_PARTS = (("swa", 0, 0, 1280), ("swa", 1280, 5376, 5392), ("gdn", 0, 1280, 5376), ("gates", 0, 5392, IN_DIM))
W_IN_WIDTHS = {"swa": SWA_IN_W, "gdn": 4 * GDN_W, "gates": 2 * D_MODEL}


def _w_in_segments():
    segs = []
    for part, p0, g0, g1 in W_IN_PARTS:
        for j in range(N_DEV):
            lo, hi = max(g0, W_IN_SHARD * j), min(g1, W_IN_SHARD * (j + 1))
            if lo < hi:
                segs.append((part, p0 + lo - g0, j, lo - W_IN_SHARD * j, hi - lo))
    return segs


def split_w_in(shards, *, name):
    dt = shards.dtype
    tm = 256

    def body(w_ref, swa_ref, gdn_ref, gates_ref):
        out = {"swa": swa_ref, "gdn": gdn_ref, "gates": gates_ref}
        swa_ref[:, SWA_Q + 2 * SWA_KV + 2 * GDN_HEADS:] = jnp.zeros((tm, SWA_IN_W - SWA_Q - 2 * SWA_KV - 2 * GDN_HEADS), dt)
        for part, p0, j, l0, n in _w_in_segments():
            out[part][:, p0:p0 + n] = w_ref[j, :, l0:l0 + n]

    return pl.pallas_call(body, name=name, grid=(D_MODEL // tm,),
                          in_specs=[pl.BlockSpec((N_DEV, tm, W_IN_PAD), lambda i: (0, i, 0))],
                          out_specs=[pl.BlockSpec((tm, W_IN_WIDTHS[p]), lambda i: (i, 0)) for p in ("swa", "gdn", "gates")],
                          out_shape=[jax.ShapeDtypeStruct((D_MODEL, W_IN_WIDTHS[p]), dt) for p in ("swa", "gdn", "gates")],
                          compiler_params=_params("parallel"))(shards)


def merge_w_in_grad(d_swa, d_gdn, d_gates, *, name):
    tm = 256

    def body(swa_ref, gdn_ref, gates_ref, w_ref):
        src = {"swa": swa_ref, "gdn": gdn_ref, "gates": gates_ref}
        w_ref[:, :, W_IN_SHARD:] = jnp.zeros((N_DEV, tm, W_IN_PAD - W_IN_SHARD), f32)
        for part, p0, j, l0, n in _w_in_segments():
            w_ref[j, :, l0:l0 + n] = src[part][:, p0:p0 + n]

    return pl.pallas_call(body, name=name, grid=(D_MODEL // tm,),
                          in_specs=[pl.BlockSpec((tm, W_IN_WIDTHS[p]), lambda i: (i, 0)) for p in ("swa", "gdn", "gates")],
                          out_specs=pl.BlockSpec((N_DEV, tm, W_IN_PAD), lambda i: (0, i, 0)),
                          out_shape=jax.ShapeDtypeStruct((N_DEV, D_MODEL, W_IN_PAD), f32),
                          compiler_params=_params("parallel"))(d_swa, d_gdn, d_gates)


def kernel(x, mem, w_in, rel_bias, swa_sinks, gdn_conv_w, gdn_a_log, gdn_dt_bias, gdn_norm_w, w_br_swa, w_br_gdn, w_mix_o, ln1_g, ln1_b, w_mem_q, w_mem_kv, w_mem_o, ln2_g, ln2_b, w_up, ffn_conv_w, ffn_conv_b, w_down, ln3_g, ln3_b, loss_target, m_w_in, m_rel_bias, m_swa_sinks, m_gdn_conv_w, m_gdn_a_log, m_gdn_dt_bias, m_gdn_norm_w, m_w_br_swa, m_w_br_gdn, m_w_mix_o, m_ln1_g, m_ln1_b, m_w_mem_q, m_w_mem_kv, m_w_mem_o, m_ln2_g, m_ln2_b, m_w_up, m_ffn_conv_w, m_ffn_conv_b, m_w_down, m_ln3_g, m_ln3_b, v_w_in, v_rel_bias, v_swa_sinks, v_gdn_conv_w, v_gdn_a_log, v_gdn_dt_bias, v_gdn_norm_w, v_w_br_swa, v_w_br_gdn, v_w_mix_o, v_ln1_g, v_ln1_b, v_w_mem_q, v_w_mem_kv, v_w_mem_o, v_ln2_g, v_ln2_b, v_w_up, v_ffn_conv_w, v_ffn_conv_b, v_w_down, v_ln3_g, v_ln3_b):
    env = dict(locals())
    w2 = {n: (env[n][0] if env[n].ndim == 3 else env[n]) for n in WEIGHTS}
    m2 = {n: (env["m_" + n][0] if env["m_" + n].ndim == 3 else env["m_" + n]) for n in WEIGHTS}
    v2 = {n: (env["v_" + n][0] if env["v_" + n].ndim == 3 else env["v_" + n]) for n in WEIGHTS}
    xs, mems, target = x[0], mem[0], loss_target[0]
    my_id = 4 * lax.axis_index("x") + 2 * lax.axis_index("y") + lax.axis_index("c")
    pad_ff = FF_PAD - FF_SHARD

    pad_cols = {"w_in": W_IN_PAD - W_IN_SHARD, "w_up": pad_ff}
    mid = ("w_br_swa", "w_br_gdn", "w_mem_o", "w_mix_o", "w_mem_q", "w_mem_kv")
    mine = {n: jnp.pad(w2[n], ((0, 0), (0, pad_cols.get(n, 0)))).astype(bf16) for n in ("w_in", "w_up", "w_down") + mid}
    xb, got_in = cast_bf16(xs, name="cast_x", side=gather_first([mine["w_in"]]))
    got_in = run_side(gather_second(got_in), name="gather_w_in_pass_on")
    w_swa, w_gdn, w_gates = split_w_in(got_in[0], name="split_w_in")
    n_ffn, n_gdn = 3 * FF_SHARD, GDN_CONV * (QKV_W // N_DEV)
    conv_mine = jnp.concatenate([w2["ffn_conv_w"].reshape(-1), w2["gdn_conv_w"].reshape(-1)])[None]
    conv_rows = lax.dynamic_update_slice(jnp.zeros((N_DEV, n_ffn + n_gdn), f32), conv_mine, (my_id, 0))
    conv_all = all_reduce_small(_pack([conv_rows], CONV_ROWS, f32), name="gather_conv_w")
    conv_all = conv_all.reshape(-1)[:N_DEV * (n_ffn + n_gdn)].reshape(N_DEV, n_ffn + n_gdn)
    cwb = jnp.concatenate([conv_all[:, :n_ffn].reshape(N_DEV, 3, FF_SHARD), w2["ffn_conv_b"].reshape(N_DEV, 1, FF_SHARD),
                           jnp.zeros((N_DEV, 4, FF_SHARD), f32)], axis=1)
    cwb = jnp.pad(cwb, ((0, 0), (0, 0), (0, pad_ff)))
    convw = jnp.transpose(conv_all[:, n_ffn:].reshape(N_DEV, GDN_CONV, QKV_W // N_DEV), (1, 0, 2)).reshape(GDN_CONV, QKV_W)
    convw = jnp.pad(convw, ((0, 4), (0, 0)))
    onehot = _bucket_onehot()
    bias = mm(w2["rel_bias"].T, onehot, "nn", hi=True, tn=4096, name="rel_bias_table").reshape(SWA_HEADS, BLOCK, 2 * BLOCK)
    alog_row = jnp.pad(w2["gdn_a_log"], ((0, 0), (GDN_HEADS, 128 - 2 * GDN_HEADS)))
    dt_row = jnp.pad(w2["gdn_dt_bias"], ((0, 0), (GDN_HEADS, 128 - 2 * GDN_HEADS)))

    memb = cast_bf16(mems, name="cast_mem")
    gates, mid_got = mm(xb, w_gates, "nn", out_dtype=bf16, name="proj_gates", side=gather_first([mine[n] for n in mid]))
    gdn_in, mid_got = mm(xb, w_gdn, "nn", name="proj_gdn", side=gather_second(mid_got))
    got = dict(zip(mid, mid_got))
    w_br_swa, w_br_gdn, w_mem_o = (_from_column_shards(got[n]) for n in ("w_br_swa", "w_br_gdn", "w_mem_o"))
    w_mix_o = got["w_mix_o"].reshape(D_MODEL, D_MODEL)
    w_mem_q = got["w_mem_q"].reshape(D_MODEL, MEM_W)
    w_mem_kv = got["w_mem_kv"].reshape(D_MODEL, 2 * MEM_W)
    swa_in = mm(xb, w_swa, "nn", tn=SWA_IN_W, name="proj_swa")
    attn, down_got = swa_fwd(swa_in, bias, w2["swa_sinks"], name="swa_fwd", side=gather_first([mine["w_down"]]))
    qn, kn, vv, gdn_conv = gdn_pre_fwd(gdn_in, convw, name="gdn_pre_fwd")
    gbeta = gbeta_fwd(swa_in, alog_row, dt_row, name="gbeta_fwd")
    (o_gdn, states, inverses), up_got = gdn_scan_fwd(qn, kn, vv, gbeta, name="gdn_scan_fwd",
                                                     side=gather_first([mine["w_up"]]))
    ygd = gdn_post_fwd(o_gdn, gdn_in, w2["gdn_norm_w"], name="gdn_post_fwd")
    y_swa, down_got = mm(attn, w_br_swa, "nn", out_dtype=bf16, name="br_swa", side=gather_second(down_got))
    y_gdn, up_got = mm(ygd, w_br_gdn, "nn", out_dtype=bf16, name="br_gdn", side=gather_second(up_got))
    w_up_blocked = up_got[0]
    w_down_p = jnp.pad(down_got[0].reshape(4, FF_SHARD, D_MODEL), ((0, 0), (0, pad_ff), (0, 0))).reshape(4 * FF_PAD, D_MODEL)
    mixed = merge_fwd(gates, y_swa, y_gdn, name="merge_fwd")
    z1 = mm(mixed, w_mix_o, "nn", add=xs, add_scale=ALPHA, name="mix_o")
    x1, x1b = ln_fwd(z1, w2["ln1_g"], w2["ln1_b"], name="ln1_fwd")
    qm = mm(x1b, w_mem_q, "nn", name="mem_q")
    kv = mm(memb, w_mem_kv, "nn", name="mem_kv")
    om = memattn_fwd(qm, kv, name="memattn_fwd")
    z2 = mm(om, w_mem_o, "nn", add=x1, add_scale=ALPHA, name="mem_o")
    x2, x2b = ln_fwd(z2, w2["ln2_g"], w2["ln2_b"], name="ln2_fwd")
    hpre = mm(x2b, w_up_blocked, "nn", b_blocked=True, out_dtype=bf16, name="ffn_up")
    act, conv_g, conv_u = ffn_act_fwd(hpre, cwb, name="ffn_act_fwd")
    z3 = mm(act, w_down_p, "nn", add=x2, add_scale=ALPHA, tk=2 * FF_PAD, name="ffn_down")
    dz3, dz3b, d_ln3g, d_ln3b, loss = ln_loss(z3, target, w2["ln3_g"], w2["ln3_b"], name="ln3_loss")

    dact = mm(dz3b, w_down_p, "nt", tn=FF_PAD, out_dtype=bf16, name="d_act")
    d_wdown_p = mm(act, dz3b, "tn", tm=FF_PAD, name="dw_down")
    d_hpre, d_cwb = ffn_act_bwd(hpre, conv_g, conv_u, dact, cwb, name="ffn_act_bwd")
    def sibling_sums(names, chunks, received):
        return [add_sibling(c, r, name="grad_add_sibling_" + n) for n, c, r in zip(names, chunks, received)]

    dx2 = mm(d_hpre, w_up_blocked, "nt", b_blocked=True, k_shards=2, add=dz3, add_scale=ALPHA, name="d_x2")
    d_wup = mm(x2b, d_hpre, "tn", out_blocked=True, name="dw_up")
    ffn = ("w_up", "w_down")
    ffn_chunks = [d_wup, d_wdown_p.reshape(4, FF_PAD, D_MODEL)[:, :FF_SHARD].reshape(N_DEV, FF_SHARD // 2, D_MODEL)]
    dz2, dz2b, d_ln2g, d_ln2b = ln_bwd(dx2, z2, w2["ln2_g"], name="ln2_bwd")
    d_om, down_received = mm(dz2b, w_mem_o, "nt", name="d_om", side=grad_to_sibling(ffn_chunks[1:]))
    d_wmemo = mm(om, dz2b, "tn", name="dw_mem_o")
    dqm, dkv = memattn_bwd(qm, kv, d_om, name="memattn_bwd")
    dx1 = mm(dqm, w_mem_q, "nt", add=dz2, add_scale=ALPHA, name="d_x1")
    d_wmemq = mm(x1b, dqm, "tn", name="dw_mem_q")
    d_wmemkv = mm(memb, dkv, "tn", name="dw_mem_kv")
    dz1, dz1b, d_ln1g, d_ln1b = ln_bwd(dx1, z1, w2["ln1_g"], name="ln1_bwd")
    half = D_MODEL // 2
    dmix, up_received = mm(dz1b, w_mix_o, "nt", name="d_mixed", side=grad_to_sibling(ffn_chunks[:1], rows=(0, half)))
    d_wmixo, up_received = mm(mixed, dz1b, "tn", tk=4096, name="dw_mix_o",
                              side=grad_to_sibling(ffn_chunks[:1], rows=(half, half), into=up_received))
    ffn_sums = sibling_sums(ffn, ffn_chunks, up_received + down_received)
    dys, dyg, d_gates = merge_bwd(gates, y_swa, y_gdn, dmix, name="merge_bwd")
    d_attn = mm(dys, w_br_swa, "nt", name="d_attn")
    d_wbrswa = mm(attn, dys, "tn", tk=4096, name="dw_br_swa")
    d_ygd = mm(dyg, w_br_gdn, "nt", name="d_ygd")
    d_wbrgdn = mm(ygd, dyg, "tn", tk=4096, name="dw_br_gdn")
    mid_chunks = [_column_shards(d_wbrswa), _column_shards(d_wbrgdn), _column_shards(d_wmemo),
                  d_wmixo.reshape(N_DEV, D_MODEL // N_DEV, D_MODEL), d_wmemq.reshape(N_DEV, D_MODEL // N_DEV, MEM_W),
                  d_wmemkv.reshape(N_DEV, D_MODEL // N_DEV, 2 * MEM_W)]
    d_o, d_gz, d_normw = gdn_post_bwd(o_gdn, gdn_in, w2["gdn_norm_w"], d_ygd, name="gdn_post_bwd")
    (dqn, dkn, dvv, dgbeta), received = gdn_scan_bwd(
        qn, kn, vv, gbeta, states, inverses, d_o, name="gdn_scan_bwd",
        side=join_sides(grad_to_chips(ffn_sums), grad_to_sibling(mid_chunks)))
    chip_parts = dict(zip(ffn, received[:2]))
    mid_sums = sibling_sums(mid, mid_chunks, received[2:])
    d_gdn_in, d_convw = gdn_pre_bwd(gdn_in, gdn_conv, convw, dqn, dkn, dvv, d_gz, name="gdn_pre_bwd")
    d_ba, d_alog, d_dt = gbeta_bwd(swa_in, alog_row, dt_row, dgbeta, name="gbeta_bwd")
    (dq, dkc, dkp, dvc, dvp, d_bias, d_sinks), received = swa_bwd(swa_in, bias, w2["swa_sinks"], d_attn, name="swa_bwd",
                                                                  side=grad_to_chips(mid_sums))
    chip_parts.update(zip(mid, received))
    d_swa_in = swa_in_grad(dq, dkc, dkp, dvc, dvp, d_ba, name="swa_in_grad")
    d_relbias = mm(d_bias.reshape(SWA_HEADS, -1), onehot, "nt", hi=True, tk=4096, name="d_rel_bias").T
    d_wgates = mm(xb, d_gates, "tn", tk=4096, name="dw_gates")
    d_wgdn = mm(xb, d_gdn_in, "tn", tk=4096, name="dw_gdn")
    d_wswa = mm(xb, d_swa_in, "tn", tn=SWA_IN_W, name="dw_swa")
    in_chunks = [merge_w_in_grad(d_wswa, d_wgdn, d_wgates, name="merge_w_in_grad")]
    gx, received = mm(d_swa_in, w_swa, "nt", add=dz1, add_scale=ALPHA, tk=SWA_IN_W, name="dx_swa",
                      side=grad_to_sibling(in_chunks))
    in_sums = sibling_sums(("w_in",), in_chunks, received)
    gx, received = mm(d_gates, w_gates, "nt", add=gx, name="dx_gates", side=grad_to_chips(in_sums, rows=(0, half)))
    gx, received = mm(d_gdn_in, w_gdn, "nt", add=gx, name="dx_gdn",
                      side=grad_to_chips(in_sums, rows=(half, half), into=received))
    chip_parts["w_in"] = received[0]
    grads = {}

    gsmall = {
        "rel_bias": d_relbias, "swa_sinks": d_sinks[:, :SWA_HEADS], "gdn_a_log": d_alog[:, GDN_HEADS:2 * GDN_HEADS],
        "gdn_dt_bias": d_dt[:, GDN_HEADS:2 * GDN_HEADS], "gdn_norm_w": d_normw, "ln1_g": d_ln1g, "ln1_b": d_ln1b,
        "ln2_g": d_ln2g, "ln2_b": d_ln2b, "ln3_g": d_ln3g, "ln3_b": d_ln3b,
        "ffn_conv_b": d_cwb[:, 3, :FF_SHARD].reshape(1, 2 * D_FF),
        "ffn_conv_w": jnp.transpose(d_cwb[:, :3, :FF_SHARD], (1, 0, 2)).reshape(3, 2 * D_FF),
        "gdn_conv_w": d_convw[:GDN_CONV],
    }
    small_shapes = [shp for _, shp in SMALL] + [(3, 2 * D_FF), (GDN_CONV, QKV_W)]
    small_names = [n for n, _ in SMALL] + ["ffn_conv_w", "gdn_conv_w"]
    small_sum = all_reduce_small(_pack([gsmall[n] for n in small_names], AR_ROWS, f32), name="all_reduce_small")
    grads.update(zip(small_names, _unpack(small_sum.reshape(-1), small_shapes)))
    grads["ffn_conv_w"] = lax.dynamic_slice_in_dim(grads["ffn_conv_w"], my_id * FF_SHARD, FF_SHARD, axis=1)
    grads["gdn_conv_w"] = lax.dynamic_slice_in_dim(grads["gdn_conv_w"], my_id * (QKV_W // N_DEV), QKV_W // N_DEV, axis=1)

    big = [n for n, shp, _ in SHARDED if shp[0] * shp[1] > 8192]
    tiny = [n for n in WEIGHTS if n not in big]
    delta, new_m, new_v = {}, {}, {}
    for n in big:
        grads[n], delta[n], new_m[n], new_v[n] = adamw_of_partial_sums(w2[n], chip_parts[n], m2[n], v2[n], name="adamw_" + n)
    tiny_shapes = [w2[n].shape for n in tiny]
    packed = [_pack([src[n] for n in tiny], SMALL_ROWS, f32) for src in (w2, grads, m2, v2)]
    for dst, res in zip((delta, new_m, new_v), adamw(*packed, name="adamw_small")):
        dst.update(zip(tiny, _unpack(res.reshape(-1), tiny_shapes)))

    def shaped(d):
        return [d[n].reshape(env[n].shape) for n in WEIGHTS]

    loss_all = lax.psum(loss[0, 0], ("x", "y", "c"))
    return (loss_all, gx[None], *shaped(grads), *shaped(delta), *shaped(new_m), *shaped(new_v))
```

```python
import functools
import math
from typing import Callable, NamedTuple

import jax
import jax.numpy as jnp
from jax import lax
from jax.experimental import pallas as pl
from jax.experimental.pallas import tpu as pltpu

f32 = jnp.float32
bf16 = jnp.bfloat16
HI = lax.Precision.HIGHEST
MESH = pl.DeviceIdType.MESH

D_MODEL = 2048
N_DEV = 8
SWA_HEADS, SWA_KV_HEADS, SWA_HEAD_DIM, BLOCK = 16, 2, 64, 128
REL_BUCKETS, REL_MAX_DIST = 32, 128
GDN_HEADS, GDN_HEAD_DIM, GDN_CONV, GDN_CHUNK = 8, 128, 4, 64
MEM_HEADS, MEM_HEAD_DIM = 4, 128
D_FF = 5504
FF_SHARD = 2 * D_FF // N_DEV
FF_PAD = 1408
NORM_EPS = 1e-5
ALPHA = 2.0 ** 0.25
NEG_INF = -1e30
SWA_Q, SWA_KV, GDN_W, MEM_W = 1024, 128, 1024, 512
IN_DIM = 9488
HALO = 8

ADAM_LR, ADAM_B1, ADAM_B2, ADAM_EPS, ADAM_WD, ADAM_STEP = 0.001, 0.9, 0.999, 1e-08, 0.01, 10

PACK_COLS = 1024
SMALL_ROWS = 32
AR_ROWS = 72
CONV_ROWS = 48

SHARDED = (
    ("w_in", (2048, 1186), 1), ("w_br_swa", (1024, 256), 1), ("w_br_gdn", (1024, 256), 1),
    ("w_mix_o", (256, 2048), 0), ("w_mem_q", (256, 512), 0), ("w_mem_kv", (256, 1024), 0),
    ("w_mem_o", (512, 256), 1), ("w_up", (2048, 1376), 1), ("w_down", (688, 2048), 0),
    ("ffn_conv_w", (3, 1376), 1), ("gdn_conv_w", (4, 384), 1),
)
SMALL = (
    ("rel_bias", (32, 16)), ("swa_sinks", (1, 16)), ("gdn_a_log", (1, 8)), ("gdn_dt_bias", (1, 8)),
    ("gdn_norm_w", (1, 128)), ("ln1_g", (1, 2048)), ("ln1_b", (1, 2048)), ("ln2_g", (1, 2048)),
    ("ln2_b", (1, 2048)), ("ln3_g", (1, 2048)), ("ln3_b", (1, 2048)), ("ffn_conv_b", (1, 11008)),
)
WEIGHTS = ("w_in", "rel_bias", "swa_sinks", "gdn_conv_w", "gdn_a_log", "gdn_dt_bias", "gdn_norm_w", "w_br_swa",
           "w_br_gdn", "w_mix_o", "ln1_g", "ln1_b", "w_mem_q", "w_mem_kv", "w_mem_o", "ln2_g", "ln2_b", "w_up",
           "ffn_conv_w", "ffn_conv_b", "w_down", "ln3_g", "ln3_b")


def _tile(n, target, align):
    if n <= target:
        return n
    t = (target // align) * align
    while t >= align:
        if n % t == 0:
            return t
        t -= align
    return n


VMEM_LIMIT_BYTES = 56 * 1024 * 1024


def _params(*sem):
    return pltpu.CompilerParams(dimension_semantics=sem, vmem_limit_bytes=VMEM_LIMIT_BYTES)


def _sigmoid(v):
    return jax.nn.sigmoid(v)


def _d16(a, b, dims):
    return lax.dot_general(a.astype(bf16), b.astype(bf16), (dims, ((), ())), preferred_element_type=f32)


NN = ((1,), (0,))
NT = ((1,), (1,))
TN = ((0,), (0,))


def mm(a, b, mode, *, name, add=None, add_scale=1.0, out_dtype=f32, hi=False, tm=1024, tn=1024, tk=2048,
       b_blocked=False, out_blocked=False, k_shards=1, side=None):
    if b_blocked:
        nb, rows, width = b.shape
        if mode == "nn":
            (m, k), n, tn = a.shape, nb * width, width
        else:
            (m, k), n, tk = a.shape, rows, k_shards * width
    elif mode == "nn":
        (m, k), (_, n) = a.shape, b.shape
    elif mode == "nt":
        (m, k), (n, _) = a.shape, b.shape
    else:
        (k, m), (_, n) = a.shape, b.shape
    if out_blocked:
        tn = n // N_DEV
    tm, tn, tk = _tile(m, tm, 8 if mode != "tn" else 128), _tile(n, tn, 128), _tile(k, tk, 128 if mode != "tn" else 8)
    nk = k // tk
    dims = {"nn": NN, "nt": NT, "tn": TN}[mode]
    a_spec = pl.BlockSpec((tk, tm), lambda i, j, kk: (kk, i)) if mode == "tn" else pl.BlockSpec((tm, tk), lambda i, j, kk: (i, kk))
    if b_blocked:
        b_spec = (pl.BlockSpec((None, tk, tn), lambda i, j, kk: (j, kk, 0)) if mode == "nn"
                  else pl.BlockSpec((k_shards, tn, tk // k_shards), lambda i, j, kk: (kk, j, 0)))
    else:
        b_spec = pl.BlockSpec((tn, tk), lambda i, j, kk: (j, kk)) if mode == "nt" else pl.BlockSpec((tk, tn), lambda i, j, kk: (kk, j))
    if out_blocked:
        o_spec, o_shape = pl.BlockSpec((None, tm, tn), lambda i, j, kk: (j, i, 0)), (N_DEV, m, tn)
    else:
        o_spec, o_shape = pl.BlockSpec((tm, tn), lambda i, j, kk: (i, j)), (m, n)
    has_add = add is not None

    def product(a_ref, b_ref):
        if hi:
            return lax.dot_general(a_ref[...], b_ref[...], (dims, ((), ())), precision=HI, preferred_element_type=f32)
        if b_blocked and mode == "nt":
            width = tk // k_shards
            parts = [_d16(a_ref[:, s * width:(s + 1) * width], b_ref[s], dims) for s in range(k_shards)]
            return functools.reduce(lambda p, q: p + q, parts)
        return _d16(a_ref[...], b_ref[...], dims)

    def finish(r, add_ref, o_ref):
        if has_add:
            r = r + add_scale * add_ref[...]
        o_ref[...] = r.astype(out_dtype)

    def body_one_step(a_ref, b_ref, *rest):
        finish(product(a_ref, b_ref), rest[0] if has_add else None, rest[-1])

    def body_k_steps(a_ref, b_ref, *rest):
        o_ref, acc_ref = rest[-2:]
        kk = pl.program_id(2)

        @pl.when(kk == 0)
        def _():
            acc_ref[...] = jnp.zeros_like(acc_ref)

        acc_ref[...] += product(a_ref, b_ref)

        @pl.when(kk == nk - 1)
        def _():
            finish(acc_ref[...], rest[0] if has_add else None, o_ref)

    return _call(body_one_step if nk == 1 else body_k_steps, (a, b, add) if has_add else (a, b), name=name,
                 grid=(m // tm, n // tn, nk), in_specs=[a_spec, b_spec] + ([o_spec] if has_add else []), out_specs=o_spec,
                 out_shape=jax.ShapeDtypeStruct(o_shape, out_dtype),
                 scratch_shapes=[] if nk == 1 else [pltpu.VMEM((tm, tn), f32)],
                 semantics=("parallel", "parallel", "arbitrary"), side=side)


def cast_bf16(a, *, name, side=None):
    m, n = a.shape
    tm = _tile(m, 512, 16)

    def body(a_ref, o_ref):
        o_ref[...] = a_ref[...].astype(bf16)

    return _call(body, (a,), name=name, grid=(m // tm,), in_specs=[pl.BlockSpec((tm, n), lambda i: (i, 0))],
                 out_specs=pl.BlockSpec((tm, n), lambda i: (i, 0)), out_shape=jax.ShapeDtypeStruct((m, n), bf16),
                 semantics=("parallel",), side=side)


def _ln_stats(z):
    mu = jnp.mean(z, axis=-1, keepdims=True)
    zc = z - mu
    var = jnp.mean(zc * zc, axis=-1, keepdims=True)
    rstd = lax.rsqrt(var + NORM_EPS)
    return zc * rstd, rstd


def ln_fwd(z, g, b, *, name):
    s, d = z.shape
    tm = _tile(s, 256, 16)

    def body(z_ref, g_ref, b_ref, y_ref, yb_ref):
        xhat, _ = _ln_stats(z_ref[...])
        y = xhat * g_ref[...] + b_ref[...]
        y_ref[...] = y
        yb_ref[...] = y.astype(bf16)

    row = pl.BlockSpec((tm, d), lambda i: (i, 0))
    vec = pl.BlockSpec((1, d), lambda i: (0, 0))
    return pl.pallas_call(body, name=name, grid=(s // tm,), in_specs=[row, vec, vec], out_specs=[row, row],
                          out_shape=[jax.ShapeDtypeStruct((s, d), f32), jax.ShapeDtypeStruct((s, d), bf16)],
                          compiler_params=_params("parallel"))(z, g, b)


def _ln_bwd_tile(dy, z, g):
    xhat, rstd = _ln_stats(z)
    dxh = dy * g
    m1 = jnp.mean(dxh, axis=-1, keepdims=True)
    m2 = jnp.mean(dxh * xhat, axis=-1, keepdims=True)
    dz = rstd * (dxh - m1 - xhat * m2)
    return dz, jnp.sum(dy * xhat, axis=0, keepdims=True), jnp.sum(dy, axis=0, keepdims=True)


def ln_bwd(dy, z, g, *, name):
    s, d = z.shape
    tm = _tile(s, 256, 16)

    def body(dy_ref, z_ref, g_ref, dz_ref, dzb_ref, dg_ref, db_ref):
        @pl.when(pl.program_id(0) == 0)
        def _():
            dg_ref[...] = jnp.zeros_like(dg_ref)
            db_ref[...] = jnp.zeros_like(db_ref)

        dz, dg, db = _ln_bwd_tile(dy_ref[...], z_ref[...], g_ref[...])
        dz_ref[...] = dz
        dzb_ref[...] = dz.astype(bf16)
        dg_ref[...] += dg
        db_ref[...] += db

    row = pl.BlockSpec((tm, d), lambda i: (i, 0))
    vec = pl.BlockSpec((1, d), lambda i: (0, 0))
    return pl.pallas_call(body, name=name, grid=(s // tm,), in_specs=[row, row, vec], out_specs=[row, row, vec, vec],
                          out_shape=[jax.ShapeDtypeStruct((s, d), f32), jax.ShapeDtypeStruct((s, d), bf16),
                                     jax.ShapeDtypeStruct((1, d), f32), jax.ShapeDtypeStruct((1, d), f32)],
                          compiler_params=_params("arbitrary"))(dy, z, g)


def ln_loss(z, target, g, b, *, name):
    s, d = z.shape
    tm = _tile(s, 256, 16)
    nt = s // tm

    def body(z_ref, t_ref, g_ref, b_ref, dz_ref, dzb_ref, dg_ref, db_ref, loss_ref, lacc_ref):
        i = pl.program_id(0)

        @pl.when(i == 0)
        def _():
            dg_ref[...] = jnp.zeros_like(dg_ref)
            db_ref[...] = jnp.zeros_like(db_ref)
            lacc_ref[...] = jnp.zeros_like(lacc_ref)

        zv, gv = z_ref[...], g_ref[...]
        xhat, _ = _ln_stats(zv)
        err = xhat * gv + b_ref[...] - t_ref[...]
        lacc_ref[...] += jnp.sum(err * err, axis=0, keepdims=True)
        dz, dg, db = _ln_bwd_tile(err * (1.0 / d), zv, gv)
        dz_ref[...] = dz
        dzb_ref[...] = dz.astype(bf16)
        dg_ref[...] += dg
        db_ref[...] += db

        @pl.when(i == nt - 1)
        def _():
            loss_ref[...] = (0.5 / d) * jnp.sum(lacc_ref[...], axis=1, keepdims=True)

    row = pl.BlockSpec((tm, d), lambda i: (i, 0))
    vec = pl.BlockSpec((1, d), lambda i: (0, 0))
    return pl.pallas_call(body, name=name, grid=(nt,), in_specs=[row, row, vec, vec],
                          out_specs=[row, row, vec, vec, pl.BlockSpec((1, 1), lambda i: (0, 0))],
                          out_shape=[jax.ShapeDtypeStruct((s, d), f32), jax.ShapeDtypeStruct((s, d), bf16),
                                     jax.ShapeDtypeStruct((1, d), f32), jax.ShapeDtypeStruct((1, d), f32),
                                     jax.ShapeDtypeStruct((1, 1), f32)],
                          scratch_shapes=[pltpu.VMEM((1, d), f32)],
                          compiler_params=_params("arbitrary"))(z, target, g, b)


def merge_fwd(gates, ys, yg, *, name):
    s, d = ys.shape
    tm = _tile(s, 256, 16)

    def body(gt_ref, ys_ref, yg_ref, o_ref):
        o_ref[...] = (_sigmoid(gt_ref[:, :d].astype(f32)) * ys_ref[...].astype(f32)
                      + _sigmoid(gt_ref[:, d:].astype(f32)) * yg_ref[...].astype(f32)).astype(bf16)

    row = pl.BlockSpec((tm, d), lambda i: (i, 0))
    return pl.pallas_call(body, name=name, grid=(s // tm,), in_specs=[pl.BlockSpec((tm, 2 * d), lambda i: (i, 0)), row, row],
                          out_specs=row, out_shape=jax.ShapeDtypeStruct((s, d), bf16),
                          compiler_params=_params("parallel"))(gates, ys, yg)


def merge_bwd(gates, ys, yg, dmix, *, name):
    s, d = ys.shape
    tm = _tile(s, 256, 16)

    def body(gt_ref, ys_ref, yg_ref, dm_ref, dys_ref, dyg_ref, dgt_ref):
        dm = dm_ref[...]
        sa, sb = _sigmoid(gt_ref[:, :d].astype(f32)), _sigmoid(gt_ref[:, d:].astype(f32))
        dys_ref[...] = (dm * sa).astype(bf16)
        dyg_ref[...] = (dm * sb).astype(bf16)
        dgt_ref[:, :d] = (dm * ys_ref[...].astype(f32) * sa * (1.0 - sa)).astype(bf16)
        dgt_ref[:, d:] = (dm * yg_ref[...].astype(f32) * sb * (1.0 - sb)).astype(bf16)

    row = pl.BlockSpec((tm, d), lambda i: (i, 0))
    wide = pl.BlockSpec((tm, 2 * d), lambda i: (i, 0))
    return pl.pallas_call(body, name=name, grid=(s // tm,), in_specs=[wide, row, row, row], out_specs=[row, row, wide],
                          out_shape=[jax.ShapeDtypeStruct((s, d), bf16), jax.ShapeDtypeStruct((s, d), bf16),
                                     jax.ShapeDtypeStruct((s, 2 * d), bf16)],
                          compiler_params=_params("parallel"))(gates, ys, yg, dmix)


def _shift_down(ext, j):
    return ext if j == 0 else pltpu.roll(ext, j, 0)


def _shift_up(ext, j):
    return ext if j == 0 else pltpu.roll(ext, ext.shape[0] - j, 0)


def _conv_taps(ext, width):
    return [_shift_down(ext, width - 1 - j)[HALO:] for j in range(width)]


def _causal_conv(taps, w_ref):
    acc = None
    for j, tap in enumerate(taps):
        term = w_ref[j:j + 1, :] * tap
        acc = term if acc is None else acc + term
    return acc


def _conv_grads(dy_ext, x, w_ref, width, rows):
    ahead = [_shift_up(dy_ext, width - 1 - j)[:rows] for j in range(width)]
    dx = None
    for j in range(width):
        term = w_ref[j:j + 1, :] * ahead[j]
        dx = term if dx is None else dx + term
    return dx, [jnp.sum(x * ahead[j], axis=0, keepdims=True) for j in range(width)]


def _rows_to_block(rows, n_rows, cols):
    r = lax.broadcasted_iota(jnp.int32, (n_rows, cols), 0)
    out = jnp.zeros((n_rows, cols), f32)
    for j, v in enumerate(rows):
        out = out + jnp.where(r == j, v, 0.0)
    return out


def _silu_and_grad(v):
    sg = _sigmoid(v)
    return v * sg, sg * (1.0 + v * (1.0 - sg))


HALO_BF16 = 16


def ffn_act_fwd(hpre, cwb, *, name):
    s = hpre.shape[0]
    tm = _tile(s, 512, 16)
    hb = tm // HALO_BF16

    def body(hg_ref, hgp_ref, hu_ref, hup_ref, cg_ref, cu_ref, o_ref, g_ref, u_ref):
        first = pl.program_id(1) == 0

        def conv(h_ref, hp_ref, c_ref):
            prev = jnp.where(first, 0.0, hp_ref[...].astype(f32)[HALO_BF16 - HALO:])
            ext = jnp.concatenate([prev, h_ref[...].astype(f32)], axis=0)
            return _causal_conv(_conv_taps(ext, 3), c_ref.at[0]) + c_ref[0, 3:4, :]

        g = conv(hg_ref, hgp_ref, cg_ref)
        u = conv(hu_ref, hup_ref, cu_ref)
        g_ref[...] = g.astype(bf16)
        u_ref[...] = u.astype(bf16)
        o_ref[...] = (g * _sigmoid(g) * u).astype(bf16)

    def tile(off):
        return pl.BlockSpec((tm, FF_PAD), lambda j, i: (i, j + off))

    def halo(off):
        return pl.BlockSpec((HALO_BF16, FF_PAD), lambda j, i: (jnp.maximum(i * hb - 1, 0), j + off))

    def taps(off):
        return pl.BlockSpec((1, 8, FF_PAD), lambda j, i: (j + off, 0, 0))

    out = pl.BlockSpec((tm, FF_PAD), lambda j, i: (i, j))
    return pl.pallas_call(body, name=name, grid=(4, s // tm),
                          in_specs=[tile(0), halo(0), tile(4), halo(4), taps(0), taps(4)], out_specs=[out, out, out],
                          out_shape=[jax.ShapeDtypeStruct((s, 4 * FF_PAD), bf16)] * 3,
                          compiler_params=_params("parallel", "parallel"))(hpre, hpre, hpre, hpre, cwb, cwb)


def ffn_act_bwd(hpre, conv_g, conv_u, dact, cwb, *, name):
    s = hpre.shape[0]
    tm = _tile(s, 512, 16)
    hb = tm // HALO_BF16
    nt = s // tm
    last_hb = s // HALO_BF16 - 1

    def body(hg_ref, hu_ref, g_ref, gn_ref, u_ref, un_ref, d_ref, dn_ref, cg_ref, cu_ref, dh_ref, dcg_ref, dcu_ref,
             buf_ref, sems):
        j, i = pl.program_id(0), pl.program_id(1)
        step = j * nt + i
        slot = step % 2

        def writes(from_slot):
            rows = pl.ds(pl.multiple_of(i * tm, tm), tm)
            return [pltpu.make_async_copy(buf_ref.at[from_slot, half],
                                          dh_ref.at[rows, pl.ds(pl.multiple_of((j + 4 * half) * FF_PAD, 128), FF_PAD)],
                                          sems.at[from_slot, half]) for half in (0, 1)]

        @pl.when(i == 0)
        def _():
            dcg_ref[...] = jnp.zeros_like(dcg_ref)
            dcu_ref[...] = jnp.zeros_like(dcu_ref)

        @pl.when(step >= 2)
        def _():
            for cp in writes(slot):
                cp.wait()

        def with_future(t_ref, n_ref):
            return jnp.concatenate([t_ref[...].astype(f32), n_ref[...].astype(f32)[:HALO]], axis=0)

        g, u = with_future(g_ref, gn_ref), with_future(u_ref, un_ref)
        d = jnp.concatenate([d_ref[...].astype(f32), jnp.where(i == nt - 1, 0.0, dn_ref[...].astype(f32)[:HALO])], axis=0)
        act, dact_dg = _silu_and_grad(g)
        dg = d * u * dact_dg
        du = d * act
        dhg, dwg = _conv_grads(dg, hg_ref[...].astype(f32), cg_ref.at[0], 3, tm)
        dhu, dwu = _conv_grads(du, hu_ref[...].astype(f32), cu_ref.at[0], 3, tm)
        buf_ref[slot, 0] = dhg.astype(bf16)
        buf_ref[slot, 1] = dhu.astype(bf16)
        for cp in writes(slot):
            cp.start()
        dcg_ref[0] += _rows_to_block(dwg + [jnp.sum(dg[:tm], axis=0, keepdims=True)], 8, FF_PAD)
        dcu_ref[0] += _rows_to_block(dwu + [jnp.sum(du[:tm], axis=0, keepdims=True)], 8, FF_PAD)

        @pl.when(step == 4 * nt - 1)
        def _():
            for cp in writes(slot) + writes(1 - slot):
                cp.wait()

    def tile(off):
        return pl.BlockSpec((tm, FF_PAD), lambda j, i: (i, j + off))

    nxt = pl.BlockSpec((HALO_BF16, FF_PAD), lambda j, i: (jnp.minimum((i + 1) * hb, last_hb), j))
    taps = [pl.BlockSpec((1, 8, FF_PAD), lambda j, i, off=off: (j + off, 0, 0)) for off in (0, 4)]
    dh, dcg, dcu = pl.pallas_call(
        body, name=name, grid=(4, nt),
        in_specs=[tile(0), tile(4), tile(0), nxt, tile(0), nxt, tile(0), nxt] + taps,
        out_specs=[ANY, taps[0], taps[0]],
        out_shape=[jax.ShapeDtypeStruct((s, 8 * FF_PAD), bf16),
                   jax.ShapeDtypeStruct((4, 8, FF_PAD), f32), jax.ShapeDtypeStruct((4, 8, FF_PAD), f32)],
        scratch_shapes=[pltpu.VMEM((2, 2, tm, FF_PAD), bf16), pltpu.SemaphoreType.DMA((2, 2))],
        compiler_params=_params("arbitrary", "arbitrary"),
    )(hpre, hpre, conv_g, conv_g, conv_u, conv_u, dact, dact, cwb, cwb)
    return dh, jnp.concatenate([dcg, dcu], axis=0)


MEM_SCALE = MEM_HEAD_DIM ** -0.5


def _softmax_rows(sc):
    m = jnp.max(sc, axis=-1, keepdims=True)
    e = jnp.exp(sc - m)
    return e / jnp.sum(e, axis=-1, keepdims=True)


def memattn_fwd(qm, kv, *, name):
    s = qm.shape[0]
    mlen = kv.shape[0]
    tm = _tile(s, 512, 16)

    def body(q_ref, kv_ref, o_ref):
        for h in range(MEM_HEADS):
            lo = h * MEM_HEAD_DIM
            q = q_ref[:, lo:lo + MEM_HEAD_DIM]
            k = kv_ref[:, lo:lo + MEM_HEAD_DIM]
            v = kv_ref[:, MEM_W + lo:MEM_W + lo + MEM_HEAD_DIM]
            p = _softmax_rows(_d16(q, k, NT) * MEM_SCALE)
            o_ref[:, lo:lo + MEM_HEAD_DIM] = _d16(p, v, NN).astype(bf16)

    return pl.pallas_call(body, name=name, grid=(s // tm,),
                          in_specs=[pl.BlockSpec((tm, MEM_W), lambda i: (i, 0)), pl.BlockSpec((mlen, 2 * MEM_W), lambda i: (0, 0))],
                          out_specs=pl.BlockSpec((tm, MEM_W), lambda i: (i, 0)),
                          out_shape=jax.ShapeDtypeStruct((s, MEM_W), bf16), compiler_params=_params("parallel"))(qm, kv)


def memattn_bwd(qm, kv, dout, *, name):
    s = qm.shape[0]
    mlen = kv.shape[0]
    tm = _tile(s, 512, 16)

    def body(q_ref, kv_ref, do_ref, dq_ref, dkv_ref):
        @pl.when(pl.program_id(0) == 0)
        def _():
            dkv_ref[...] = jnp.zeros_like(dkv_ref)

        for h in range(MEM_HEADS):
            lo = h * MEM_HEAD_DIM
            q = q_ref[:, lo:lo + MEM_HEAD_DIM]
            k = kv_ref[:, lo:lo + MEM_HEAD_DIM]
            v = kv_ref[:, MEM_W + lo:MEM_W + lo + MEM_HEAD_DIM]
            do = do_ref[:, lo:lo + MEM_HEAD_DIM]
            p = _softmax_rows(_d16(q, k, NT) * MEM_SCALE)
            dp = _d16(do, v, NT)
            ds = p * (dp - jnp.sum(p * dp, axis=-1, keepdims=True)) * MEM_SCALE
            dq_ref[:, lo:lo + MEM_HEAD_DIM] = _d16(ds, k, NN).astype(bf16)
            dkv_ref[:, lo:lo + MEM_HEAD_DIM] += _d16(ds, q, TN)
            dkv_ref[:, MEM_W + lo:MEM_W + lo + MEM_HEAD_DIM] += _d16(p, do, TN)

    row = pl.BlockSpec((tm, MEM_W), lambda i: (i, 0))
    full = pl.BlockSpec((mlen, 2 * MEM_W), lambda i: (0, 0))
    return pl.pallas_call(body, name=name, grid=(s // tm,), in_specs=[row, full, row], out_specs=[row, full],
                          out_shape=[jax.ShapeDtypeStruct((s, MEM_W), bf16), jax.ShapeDtypeStruct((mlen, 2 * MEM_W), f32)],
                          compiler_params=_params("arbitrary"))(qm, kv, dout)


SWA_SCALE = SWA_HEAD_DIM ** -0.5
SWA_GROUP = SWA_HEADS // SWA_KV_HEADS
SWA_IN_W = 1408
K_COL, V_COL, BA_COL = SWA_Q // 128, SWA_Q // 128 + 1, SWA_Q // 128 + 2


def _swa_mask(n):
    qi = lax.broadcasted_iota(jnp.int32, (BLOCK, 2 * BLOCK), 0)
    kj = lax.broadcasted_iota(jnp.int32, (BLOCK, 2 * BLOCK), 1)
    dist = qi + BLOCK - kj
    return (dist >= 0) & (dist < BLOCK) & ((n > 0) | (kj >= BLOCK))


def _swa_probs(q, k, bias, sink, mask):
    heads = range(len(q))
    sc = [jnp.where(mask, _d16(q[h], k[h], NT) * SWA_SCALE + bias[h], NEG_INF) for h in heads]
    m = [jnp.maximum(jnp.max(sc[h], axis=-1, keepdims=True), sink[h]) for h in heads]
    e = [jnp.exp(sc[h] - m[h]) for h in heads]
    es = [jnp.exp(sink[h] - m[h]) for h in heads]
    inv = [1.0 / (jnp.sum(e[h], axis=-1, keepdims=True) + es[h]) for h in heads]
    return e, es, inv


def _swa_heads(ref):
    return [ref[:, h * SWA_HEAD_DIM:(h + 1) * SWA_HEAD_DIM] for h in range(SWA_HEADS)]


def _swa_kv_of_heads(band):
    kv = [band[:, g * SWA_HEAD_DIM:(g + 1) * SWA_HEAD_DIM] for g in range(SWA_KV_HEADS)]
    return [kv[h // SWA_GROUP] for h in range(SWA_HEADS)]


def _swa_specs():
    q_spec = pl.BlockSpec((BLOCK, SWA_Q), lambda n: (n, 0))

    def band(col):
        return [pl.BlockSpec((BLOCK, SWA_KV), lambda n: (jnp.maximum(n - 1, 0), col)),
                pl.BlockSpec((BLOCK, SWA_KV), lambda n: (n, col))]

    bias_spec = pl.BlockSpec((SWA_HEADS, BLOCK, 2 * BLOCK), lambda n: (0, 0, 0))
    sink_spec = pl.BlockSpec((1, SWA_HEADS), lambda n: (0, 0))
    return [q_spec] + band(K_COL) + band(V_COL) + [bias_spec, sink_spec]


def swa_fwd(swa_in, bias, sinks, *, name, side=None):
    s = swa_in.shape[0]

    def body(q_ref, kp_ref, kc_ref, vp_ref, vc_ref, bias_ref, sink_ref, o_ref):
        mask = _swa_mask(pl.program_id(0))
        kb = jnp.concatenate([kp_ref[...], kc_ref[...]], axis=0)
        vb = jnp.concatenate([vp_ref[...], vc_ref[...]], axis=0)
        heads = range(SWA_HEADS)
        k, v = _swa_kv_of_heads(kb), _swa_kv_of_heads(vb)
        e, _, inv = _swa_probs(_swa_heads(q_ref), k, [bias_ref[h] for h in heads], [sink_ref[:, h:h + 1] for h in heads], mask)
        outs = [_d16(e[h] * inv[h], v[h], NN) for h in heads]
        for h in heads:
            o_ref[:, h * SWA_HEAD_DIM:(h + 1) * SWA_HEAD_DIM] = outs[h].astype(bf16)

    return _call(body, (swa_in, swa_in, swa_in, swa_in, swa_in, bias, sinks), name=name, grid=(s // BLOCK,),
                 in_specs=_swa_specs(), out_specs=pl.BlockSpec((BLOCK, SWA_Q), lambda n: (n, 0)),
                 out_shape=jax.ShapeDtypeStruct((s, SWA_Q), bf16), semantics=("parallel",), side=side)


def swa_bwd(swa_in, bias, sinks, dout, *, name, side=None):
    s = swa_in.shape[0]

    def body(q_ref, kp_ref, kc_ref, vp_ref, vc_ref, bias_ref, sink_ref, do_ref,
             dq_ref, dkc_ref, dkp_ref, dvc_ref, dvp_ref, dbias_ref, dsink_ref):
        n = pl.program_id(0)

        @pl.when(n == 0)
        def _():
            dbias_ref[...] = jnp.zeros_like(dbias_ref)
            dsink_ref[...] = jnp.zeros_like(dsink_ref)

        mask = _swa_mask(n)
        kb = jnp.concatenate([kp_ref[...], kc_ref[...]], axis=0)
        vb = jnp.concatenate([vp_ref[...], vc_ref[...]], axis=0)
        lane = lax.broadcasted_iota(jnp.int32, (1, 128), 1)
        hs = range(SWA_HEADS)
        q, do = _swa_heads(q_ref), _swa_heads(do_ref)
        k, v = _swa_kv_of_heads(kb), _swa_kv_of_heads(vb)
        e, es, inv = _swa_probs(q, k, [bias_ref[h] for h in hs], [sink_ref[:, h:h + 1] for h in hs], mask)
        p = [e[h] * inv[h] for h in hs]
        dp = [_d16(do[h], v[h], NT) for h in hs]
        delta = [jnp.sum(p[h] * dp[h], axis=-1, keepdims=True) for h in hs]
        ds = [p[h] * (dp[h] - delta[h]) for h in hs]
        dss = [ds[h] * SWA_SCALE for h in hs]
        dq = [_d16(dss[h], k[h], NN) for h in hs]
        dks = [_d16(dss[h], q[h], TN) for h in hs]
        dvs = [_d16(p[h], do[h], TN) for h in hs]
        dsink = jnp.zeros((1, 128), f32)
        for h in hs:
            dbias_ref[h] += ds[h]
            dq_ref[:, h * SWA_HEAD_DIM:(h + 1) * SWA_HEAD_DIM] = dq[h]
            dsink = dsink + jnp.where(lane == h, -jnp.sum(es[h] * inv[h] * delta[h], axis=0, keepdims=True), 0.0)
        for g in range(SWA_KV_HEADS):
            kl = g * SWA_HEAD_DIM
            group = range(g * SWA_GROUP, (g + 1) * SWA_GROUP)
            dk = functools.reduce(lambda a, b: a + b, [dks[h] for h in group])
            dv = functools.reduce(lambda a, b: a + b, [dvs[h] for h in group])
            dkp_ref[:, kl:kl + SWA_HEAD_DIM] = dk[:BLOCK]
            dkc_ref[:, kl:kl + SWA_HEAD_DIM] = dk[BLOCK:]
            dvp_ref[:, kl:kl + SWA_HEAD_DIM] = dv[:BLOCK]
            dvc_ref[:, kl:kl + SWA_HEAD_DIM] = dv[BLOCK:]
        dsink_ref[...] += dsink

    qs = pl.BlockSpec((BLOCK, SWA_Q), lambda n: (n, 0))
    ks = pl.BlockSpec((BLOCK, SWA_KV), lambda n: (n, 0))
    return _call(
        body, (swa_in, swa_in, swa_in, swa_in, swa_in, bias, sinks, dout), name=name, grid=(s // BLOCK,),
        in_specs=_swa_specs() + [qs],
        out_specs=[qs, ks, ks, ks, ks, pl.BlockSpec((SWA_HEADS, BLOCK, 2 * BLOCK), lambda n: (0, 0, 0)),
                   pl.BlockSpec((1, 128), lambda n: (0, 0))],
        out_shape=[jax.ShapeDtypeStruct((s, SWA_Q), f32)] + [jax.ShapeDtypeStruct((s, SWA_KV), f32)] * 4
        + [jax.ShapeDtypeStruct((SWA_HEADS, BLOCK, 2 * BLOCK), f32), jax.ShapeDtypeStruct((1, 128), f32)],
        semantics=("arbitrary",), side=side)


def swa_in_grad(dq, dkc, dkp, dvc, dvp, dba, *, name):
    s = dq.shape[0]
    nb = s // BLOCK

    def body(dq_ref, dkc_ref, dkp_ref, dvc_ref, dvp_ref, dba_ref, o_ref):
        has_next = pl.program_id(0) < nb - 1
        o_ref[:, :SWA_Q] = dq_ref[...].astype(bf16)
        o_ref[:, SWA_Q:SWA_Q + SWA_KV] = (dkc_ref[...] + jnp.where(has_next, dkp_ref[...], 0.0)).astype(bf16)
        o_ref[:, SWA_Q + SWA_KV:SWA_Q + 2 * SWA_KV] = (dvc_ref[...] + jnp.where(has_next, dvp_ref[...], 0.0)).astype(bf16)
        o_ref[:, SWA_Q + 2 * SWA_KV:] = dba_ref[...].astype(bf16)

    cur = pl.BlockSpec((BLOCK, SWA_KV), lambda n: (n, 0))
    nxt = pl.BlockSpec((BLOCK, SWA_KV), lambda n: (jnp.minimum(n + 1, nb - 1), 0))
    return pl.pallas_call(body, name=name, grid=(nb,),
                          in_specs=[pl.BlockSpec((BLOCK, SWA_Q), lambda n: (n, 0)), cur, nxt, cur, nxt, cur],
                          out_specs=pl.BlockSpec((BLOCK, SWA_IN_W), lambda n: (n, 0)),
                          out_shape=jax.ShapeDtypeStruct((s, SWA_IN_W), bf16),
                          compiler_params=_params("parallel"))(dq, dkc, dkp, dvc, dvp, dba)


def _bucket_onehot():
    qi = jnp.arange(BLOCK)[:, None]
    kj = jnp.arange(2 * BLOCK)[None, :]
    dist = jnp.maximum(qi + BLOCK - kj, 0)
    max_exact = REL_BUCKETS // 2
    dd = jnp.maximum(dist, 1).astype(f32)
    large = max_exact + (jnp.log(dd / max_exact) / math.log(REL_MAX_DIST / max_exact) * (REL_BUCKETS - max_exact)).astype(jnp.int32)
    bucket = jnp.where(dist < max_exact, dist, jnp.minimum(large, REL_BUCKETS - 1)).reshape(-1)
    return (bucket[None, :] == jnp.arange(REL_BUCKETS)[:, None]).astype(f32)


def _gbeta_fn(ba, alog_row, dt_row):
    col = lax.broadcasted_iota(jnp.int32, ba.shape, 1)
    v = ba + dt_row
    softplus = jnp.maximum(v, 0.0) + jnp.log(1.0 + jnp.exp(-jnp.abs(v)))
    g = -jnp.exp(alog_row) * softplus
    return jnp.where(col < GDN_HEADS, _sigmoid(ba), jnp.where(col < 2 * GDN_HEADS, g, 0.0))


def gbeta_fwd(swa_in, alog_row, dt_row, *, name):
    s = swa_in.shape[0]
    tm = _tile(s, 512, 8)

    def body(ba_ref, a_ref, d_ref, o_ref):
        o_ref[...] = _gbeta_fn(ba_ref[...], a_ref[...], d_ref[...])

    vec = pl.BlockSpec((1, 128), lambda i: (0, 0))
    return pl.pallas_call(body, name=name, grid=(s // tm,), in_specs=[pl.BlockSpec((tm, 128), lambda i: (i, BA_COL)), vec, vec],
                          out_specs=pl.BlockSpec((tm, 128), lambda i: (i, 0)), out_shape=jax.ShapeDtypeStruct((s, 128), f32),
                          compiler_params=_params("parallel"))(swa_in, alog_row, dt_row)


def gbeta_bwd(swa_in, alog_row, dt_row, dgbeta, *, name):
    s = swa_in.shape[0]
    tm = _tile(s, 512, 8)

    def body(ba_ref, a_ref, d_ref, dgb_ref, dba_ref, da_ref, dd_ref):
        @pl.when(pl.program_id(0) == 0)
        def _():
            da_ref[...] = jnp.zeros_like(da_ref)
            dd_ref[...] = jnp.zeros_like(dd_ref)

        _, pull = jax.vjp(_gbeta_fn, ba_ref[...], a_ref[...], d_ref[...])
        dba, da, dd = pull(dgb_ref[...])
        dba_ref[...] = dba
        da_ref[...] += da
        dd_ref[...] += dd

    vec = pl.BlockSpec((1, 128), lambda i: (0, 0))
    row = pl.BlockSpec((tm, 128), lambda i: (i, 0))
    return pl.pallas_call(body, name=name, grid=(s // tm,),
                          in_specs=[pl.BlockSpec((tm, 128), lambda i: (i, BA_COL)), vec, vec, row], out_specs=[row, vec, vec],
                          out_shape=[jax.ShapeDtypeStruct((s, 128), f32), jax.ShapeDtypeStruct((1, 128), f32),
                                     jax.ShapeDtypeStruct((1, 128), f32)],
                          compiler_params=_params("arbitrary"))(swa_in, alog_row, dt_row, dgbeta)


QKV_W = 3 * GDN_W


def gdn_pre_fwd(gdn_in, convw, *, name):
    s = gdn_in.shape[0]
    tm = _tile(s, 256, 16)
    hb = tm // HALO

    def body(x_ref, xp_ref, w_ref, q_ref, k_ref, v_ref, pre_ref):
        prev = jnp.where(pl.program_id(0) == 0, 0.0, xp_ref[...])
        pre = _causal_conv(_conv_taps(jnp.concatenate([prev, x_ref[...]], axis=0), GDN_CONV), w_ref)
        pre_ref[...] = pre
        act = pre * _sigmoid(pre)
        for h in range(GDN_HEADS):
            lo = h * GDN_HEAD_DIM
            for off, o_ref in ((0, q_ref), (GDN_W, k_ref)):
                seg = act[:, off + lo:off + lo + GDN_HEAD_DIM]
                o_ref[:, lo:lo + GDN_HEAD_DIM] = seg * lax.rsqrt(jnp.sum(seg * seg, axis=-1, keepdims=True) + 1e-6)
        v_ref[...] = act[:, 2 * GDN_W:]

    out = pl.BlockSpec((tm, GDN_W), lambda i: (i, 0))
    return pl.pallas_call(body, name=name, grid=(s // tm,),
                          in_specs=[pl.BlockSpec((tm, QKV_W), lambda i: (i, 0)),
                                    pl.BlockSpec((HALO, QKV_W), lambda i: (jnp.maximum(i * hb - 1, 0), 0)),
                                    pl.BlockSpec((8, QKV_W), lambda i: (0, 0))],
                          out_specs=[out, out, out, pl.BlockSpec((tm, QKV_W), lambda i: (i, 0))],
                          out_shape=[jax.ShapeDtypeStruct((s, GDN_W), f32)] * 3 + [jax.ShapeDtypeStruct((s, QKV_W), f32)],
                          compiler_params=_params("parallel"))(gdn_in, gdn_in, convw)


def gdn_pre_bwd(gdn_in, conv_out, convw, dqn, dkn, dv, dgz, *, name):
    s = gdn_in.shape[0]
    tm = _tile(s, 256, 16)
    hb = tm // HALO
    nt = s // tm
    last_hb = s // HALO - 1

    def body(x_ref, pre_ref, pren_ref, w_ref, dq_ref, dqx_ref, dk_ref, dkx_ref, dv_ref, dvx_ref, dz_ref, o_ref, dw_ref):
        i = pl.program_id(0)
        last = i == nt - 1

        @pl.when(i == 0)
        def _():
            dw_ref[...] = jnp.zeros_like(dw_ref)

        pre = jnp.concatenate([pre_ref[...], pren_ref[...]], axis=0)
        act, dact_dpre = _silu_and_grad(pre)

        def with_future(t_ref, n_ref):
            return jnp.concatenate([t_ref[...], jnp.where(last, 0.0, n_ref[...])], axis=0)

        dqe, dke, dve = with_future(dq_ref, dqx_ref), with_future(dk_ref, dkx_ref), with_future(dv_ref, dvx_ref)
        parts = []
        for off, dn in ((0, dqe), (GDN_W, dke)):
            for h in range(GDN_HEADS):
                lo = h * GDN_HEAD_DIM
                seg = act[:, off + lo:off + lo + GDN_HEAD_DIM]
                r = lax.rsqrt(jnp.sum(seg * seg, axis=-1, keepdims=True) + 1e-6)
                nrm = seg * r
                dseg = dn[:, lo:lo + GDN_HEAD_DIM]
                parts.append(r * (dseg - nrm * jnp.sum(dseg * nrm, axis=-1, keepdims=True)))
        dpre = jnp.concatenate(parts + [dve], axis=1) * dact_dpre
        dx, dw = _conv_grads(dpre, x_ref[...], w_ref, GDN_CONV, tm)
        o_ref[:, :QKV_W] = dx.astype(bf16)
        o_ref[:, QKV_W:] = dz_ref[...].astype(bf16)
        dw_ref[...] += _rows_to_block(dw, 8, QKV_W)

    row = pl.BlockSpec((tm, GDN_W), lambda i: (i, 0))
    fut = pl.BlockSpec((HALO, GDN_W), lambda i: (jnp.minimum((i + 1) * hb, last_hb), 0))
    wide = pl.BlockSpec((tm, QKV_W), lambda i: (i, 0))
    return pl.pallas_call(
        body, name=name, grid=(nt,),
        in_specs=[wide, wide, pl.BlockSpec((HALO, QKV_W), lambda i: (jnp.minimum((i + 1) * hb, last_hb), 0)),
                  pl.BlockSpec((8, QKV_W), lambda i: (0, 0)), row, fut, row, fut, row, fut, row],
        out_specs=[pl.BlockSpec((tm, 4 * GDN_W), lambda i: (i, 0)), pl.BlockSpec((8, QKV_W), lambda i: (0, 0))],
        out_shape=[jax.ShapeDtypeStruct((s, 4 * GDN_W), bf16), jax.ShapeDtypeStruct((8, QKV_W), f32)],
        compiler_params=_params("arbitrary"),
    )(gdn_in, conv_out, conv_out, convw, dqn, dqn, dkn, dkn, dv, dv, dgz)


def _gdn_post_head(o, z, nw):
    return o * lax.rsqrt(jnp.mean(o * o, axis=-1, keepdims=True) + 1e-6) * nw * (z * _sigmoid(z))


def gdn_post_fwd(o, gdn_in, nw, *, name):
    s = o.shape[0]
    tm = _tile(s, 256, 16)

    def body(o_ref, z_ref, nw_ref, y_ref):
        for h in range(GDN_HEADS):
            sl = slice(h * GDN_HEAD_DIM, (h + 1) * GDN_HEAD_DIM)
            y_ref[:, sl] = _gdn_post_head(o_ref[:, sl], z_ref[:, sl], nw_ref[...]).astype(bf16)

    row = pl.BlockSpec((tm, GDN_W), lambda i: (i, 0))
    return pl.pallas_call(body, name=name, grid=(s // tm,),
                          in_specs=[row, pl.BlockSpec((tm, GDN_W), lambda i: (i, 3)), pl.BlockSpec((1, 128), lambda i: (0, 0))],
                          out_specs=row, out_shape=jax.ShapeDtypeStruct((s, GDN_W), bf16),
                          compiler_params=_params("parallel"))(o, gdn_in, nw)


def gdn_post_bwd(o, gdn_in, nw, dy, *, name):
    s = o.shape[0]
    tm = _tile(s, 256, 16)

    def body(o_ref, z_ref, nw_ref, dy_ref, do_ref, dz_ref, dnw_ref):
        @pl.when(pl.program_id(0) == 0)
        def _():
            dnw_ref[...] = jnp.zeros_like(dnw_ref)

        dnw = jnp.zeros((1, 128), f32)
        for h in range(GDN_HEADS):
            sl = slice(h * GDN_HEAD_DIM, (h + 1) * GDN_HEAD_DIM)
            _, pull = jax.vjp(_gdn_post_head, o_ref[:, sl], z_ref[:, sl], nw_ref[...])
            do, dz, dn = pull(dy_ref[:, sl])
            do_ref[:, sl] = do
            dz_ref[:, sl] = dz
            dnw = dnw + dn
        dnw_ref[...] += dnw

    row = pl.BlockSpec((tm, GDN_W), lambda i: (i, 0))
    vec = pl.BlockSpec((1, 128), lambda i: (0, 0))
    return pl.pallas_call(body, name=name, grid=(s // tm,),
                          in_specs=[row, pl.BlockSpec((tm, GDN_W), lambda i: (i, 3)), vec, row], out_specs=[row, row, vec],
                          out_shape=[jax.ShapeDtypeStruct((s, GDN_W), f32), jax.ShapeDtypeStruct((s, GDN_W), f32),
                                     jax.ShapeDtypeStruct((1, 128), f32)],
                          compiler_params=_params("arbitrary"))(o, gdn_in, nw, dy)


def _dot_high(a, b, dims=NN):
    return lax.dot_general(a, b, (dims, ((), ())), precision=lax.Precision.HIGH, preferred_element_type=f32)


@jax.custom_vjp
def _unit_lower_inverses(a):
    c = a[0].shape[0]
    n = range(len(a))
    eye = (lax.broadcasted_iota(jnp.int32, (c, c), 0) == lax.broadcasted_iota(jnp.int32, (c, c), 1)).astype(f32)
    inv = [eye - a[i] for i in n]
    pw = [_dot_high(a[i], a[i]) for i in n]
    width = 2
    while width < c:
        inv = [inv[i] + _dot_high(inv[i], pw[i]) for i in n]
        width *= 2
        if width < c:
            pw = [_dot_high(pw[i], pw[i]) for i in n]
    return inv


def _unit_lower_inverses_fwd(a):
    inv = _unit_lower_inverses(a)
    return inv, inv


def _unit_lower_inverses_bwd(inv, g):
    return ([-_dot_high(_dot_high(x, gx, TN), x, NT) for x, gx in zip(inv, g)],)


_unit_lower_inverses.defvjp(_unit_lower_inverses_fwd, _unit_lower_inverses_bwd)


@jax.custom_vjp
def _known_inverses(a, inv):
    return inv


_known_inverses.defvjp(lambda a, inv: (inv, inv),
                       lambda inv, g: (_unit_lower_inverses_bwd(inv, g)[0], [jnp.zeros_like(x) for x in inv]))


def _gdn_chunks(q, k, v, gb, state, kept_inverses=None):
    c = GDN_CHUNK
    heads = range(len(q))
    r = lax.broadcasted_iota(jnp.int32, (c, c), 0)
    cc = lax.broadcasted_iota(jnp.int32, (c, c), 1)
    tril, strict = r >= cc, r > cc
    eye = (r == cc).astype(f32)

    def dhi(a, b):
        return jnp.dot(a, b, precision=lax.Precision.HIGH, preferred_element_type=f32)

    beta = [gb[:, h:h + 1] for h in heads]
    cum_cols = dhi(tril.astype(f32), gb)
    cum_rows = dhi(gb.T, (r <= cc).astype(f32))
    gi = [jnp.broadcast_to(cum_cols[:, GDN_HEADS + h:GDN_HEADS + h + 1], (c, c)) for h in heads]
    gj = [jnp.broadcast_to(cum_rows[GDN_HEADS + h:GDN_HEADS + h + 1, :], (c, c)) for h in heads]
    decay = [jnp.where(tril, jnp.exp(jnp.where(tril, gi[h] - gj[h], 0.0)), 0.0) for h in heads]
    kb = [k[h] * beta[h] for h in heads]
    vb = [v[h] * beta[h] for h in heads]
    a = [jnp.where(strict, _d16(kb[h], k[h], NT) * decay[h], 0.0) for h in heads]
    tinv = _unit_lower_inverses(a) if kept_inverses is None else _known_inverses(a, kept_inverses)
    gc = [gi[h][:, 0:1] for h in heads]
    egc = [jnp.exp(gc[h]) for h in heads]
    u = [dhi(tinv[h], vb[h]) for h in heads]
    w = [dhi(tinv[h], kb[h] * egc[h]) for h in heads]
    qs = [q[h] * (GDN_HEAD_DIM ** -0.5) for h in heads]
    attn = [jnp.where(tril, _d16(qs[h], k[h], NT) * decay[h], 0.0) for h in heads]
    g_last = [gi[h][c - 1:c, 0:1] for h in heads]
    v_new = [u[h] - _d16(w[h], state[h], NN) for h in heads]
    out = [_d16(qs[h] * egc[h], state[h], NN) + _d16(attn[h], v_new[h], NN) for h in heads]
    new_state = [state[h] * jnp.exp(g_last[h]) + _d16(k[h] * jnp.exp(g_last[h] - gc[h]), v_new[h], TN) for h in heads]
    return out, new_state, tinv


def _head_cols(ref):
    return [ref[:, h * GDN_HEAD_DIM:(h + 1) * GDN_HEAD_DIM] for h in range(GDN_HEADS)]


def gdn_scan_fwd(qn, kn, v, gbeta, *, name, side=None):
    s = qn.shape[0]
    nc = s // GDN_CHUNK

    def body(q_ref, k_ref, v_ref, gb_ref, o_ref, st_ref, inv_ref, state_ref):
        @pl.when(pl.program_id(0) == 0)
        def _():
            state_ref[...] = jnp.zeros_like(state_ref)

        states = [state_ref[h] for h in range(GDN_HEADS)]
        outs, new, inverses = _gdn_chunks(_head_cols(q_ref), _head_cols(k_ref), _head_cols(v_ref), gb_ref[...], states)
        for h in range(GDN_HEADS):
            st_ref[0, h] = states[h]
            inv_ref[0, h] = inverses[h]
            o_ref[:, h * GDN_HEAD_DIM:(h + 1) * GDN_HEAD_DIM] = outs[h]
            state_ref[h] = new[h]

    row = pl.BlockSpec((GDN_CHUNK, GDN_W), lambda n: (n, 0))
    return _call(
        body, (qn, kn, v, gbeta), name=name, grid=(nc,),
        in_specs=[row, row, row, pl.BlockSpec((GDN_CHUNK, 128), lambda n: (n, 0))],
        out_specs=[row, pl.BlockSpec((1, GDN_HEADS, GDN_HEAD_DIM, GDN_HEAD_DIM), lambda n: (n, 0, 0, 0)),
                   pl.BlockSpec((1, GDN_HEADS, GDN_CHUNK, GDN_CHUNK), lambda n: (n, 0, 0, 0))],
        out_shape=[jax.ShapeDtypeStruct((s, GDN_W), f32),
                   jax.ShapeDtypeStruct((nc, GDN_HEADS, GDN_HEAD_DIM, GDN_HEAD_DIM), f32),
                   jax.ShapeDtypeStruct((nc, GDN_HEADS, GDN_CHUNK, GDN_CHUNK), f32)],
        scratch_shapes=[pltpu.VMEM((GDN_HEADS, GDN_HEAD_DIM, GDN_HEAD_DIM), f32)], semantics=("arbitrary",), side=side)


def gdn_scan_bwd(qn, kn, v, gbeta, states, inverses, dout, *, name, side=None):
    s = qn.shape[0]
    nc = s // GDN_CHUNK

    def body(q_ref, k_ref, v_ref, gb_ref, st_ref, inv_ref, do_ref, dq_ref, dk_ref, dv_ref, dgb_ref, dstate_ref):
        @pl.when(pl.program_id(0) == 0)
        def _():
            dstate_ref[...] = jnp.zeros_like(dstate_ref)

        kept = [inv_ref[0, h] for h in range(GDN_HEADS)]
        _, pull = jax.vjp(lambda *args: _gdn_chunks(*args, kept_inverses=kept)[:2],
                          _head_cols(q_ref), _head_cols(k_ref), _head_cols(v_ref), gb_ref[...],
                          [st_ref[0, h] for h in range(GDN_HEADS)])
        dq, dk, dv, dgb, dst = pull((_head_cols(do_ref), [dstate_ref[h] for h in range(GDN_HEADS)]))
        for h in range(GDN_HEADS):
            sl = slice(h * GDN_HEAD_DIM, (h + 1) * GDN_HEAD_DIM)
            dq_ref[:, sl] = dq[h]
            dk_ref[:, sl] = dk[h]
            dv_ref[:, sl] = dv[h]
            dstate_ref[h] = dst[h]
        dgb_ref[...] = dgb

    row = pl.BlockSpec((GDN_CHUNK, GDN_W), lambda n: (nc - 1 - n, 0))
    gb = pl.BlockSpec((GDN_CHUNK, 128), lambda n: (nc - 1 - n, 0))
    return _call(
        body, (qn, kn, v, gbeta, states, inverses, dout), name=name, grid=(nc,),
        in_specs=[row, row, row, gb, pl.BlockSpec((1, GDN_HEADS, GDN_HEAD_DIM, GDN_HEAD_DIM), lambda n: (nc - 1 - n, 0, 0, 0)),
                  pl.BlockSpec((1, GDN_HEADS, GDN_CHUNK, GDN_CHUNK), lambda n: (nc - 1 - n, 0, 0, 0)), row],
        out_specs=[row, row, row, gb],
        out_shape=[jax.ShapeDtypeStruct((s, GDN_W), f32)] * 3 + [jax.ShapeDtypeStruct((s, 128), f32)],
        scratch_shapes=[pltpu.VMEM((GDN_HEADS, GDN_HEAD_DIM, GDN_HEAD_DIM), f32)], semantics=("arbitrary",), side=side)


def _adamw_update(w, g, m, v):
    nm = ADAM_B1 * m + (1.0 - ADAM_B1) * g
    nv = ADAM_B2 * v + (1.0 - ADAM_B2) * (g * g)
    m_hat = nm / (1.0 - ADAM_B1 ** ADAM_STEP)
    v_hat = nv / (1.0 - ADAM_B2 ** ADAM_STEP)
    return -ADAM_LR * (m_hat / (jnp.sqrt(v_hat) + ADAM_EPS) + ADAM_WD * w), nm, nv


def adamw(w, g, m, v, *, name):
    r, c = w.shape
    tr = _tile(r, 256, 8)

    def body(w_ref, g_ref, m_ref, v_ref, d_ref, nm_ref, nv_ref):
        d_ref[...], nm_ref[...], nv_ref[...] = _adamw_update(w_ref[...], g_ref[...], m_ref[...], v_ref[...])

    spec = pl.BlockSpec((tr, c), lambda i: (i, 0))
    return pl.pallas_call(body, name=name, grid=(r // tr,), in_specs=[spec] * 4, out_specs=[spec] * 3,
                          out_shape=[jax.ShapeDtypeStruct((r, c), f32)] * 3, compiler_params=_params("parallel"))(w, g, m, v)


def adamw_of_partial_sums(w, parts, m, v, *, name):
    r, c = w.shape
    cp = parts.shape[2]
    tr = _tile(r, 256, 16)

    def body(w_ref, p_ref, m_ref, v_ref, g_ref, d_ref, nm_ref, nv_ref):
        part = [p_ref[k, :, :c].astype(f32) for k in range(4)]
        g = ((part[3] + part[0]) + part[1]) + part[2]
        g_ref[...] = g
        d_ref[...], nm_ref[...], nv_ref[...] = _adamw_update(w_ref[...], g, m_ref[...], v_ref[...])

    spec = pl.BlockSpec((tr, c), lambda i: (i, 0))
    return pl.pallas_call(body, name=name, grid=(r // tr,),
                          in_specs=[spec, pl.BlockSpec((4, tr, cp), lambda i: (0, i, 0)), spec, spec], out_specs=[spec] * 4,
                          out_shape=[jax.ShapeDtypeStruct((r, c), f32)] * 4,
                          compiler_params=_params("parallel"))(w, parts, m, v)


def _pos():
    return lax.axis_index("x"), lax.axis_index("y"), lax.axis_index("c")


ANY = pl.BlockSpec(memory_space=pl.ANY)


class Side(NamedTuple):
    ins: list
    outs: list
    aliases: dict
    sems: list
    start: Callable
    wait: Callable


def join_sides(*sides):
    def spans(key):
        out, off = [], 0
        for sd in sides:
            out.append(slice(off, off + len(getattr(sd, key))))
            off += len(getattr(sd, key))
        return out

    i_sp, o_sp, s_sp = spans("ins"), spans("outs"), spans("sems")
    aliases = {i_sp[n].start + i: o_sp[n].start + o for n, sd in enumerate(sides) for i, o in sd.aliases.items()}

    def each(what):
        def run(ins, outs, sems):
            for n, sd in enumerate(sides):
                getattr(sd, what)(ins[i_sp[n]], outs[o_sp[n]], sems[s_sp[n]])
        return run

    return Side([a for sd in sides for a in sd.ins], [o for sd in sides for o in sd.outs], aliases,
                [s for sd in sides for s in sd.sems], each("start"), each("wait"))


def _side_body(body, side, n_in, n_out, n_scratch, grid):
    ns_in, ns_out = len(side.ins), len(side.outs)

    def wrapped(*refs):
        cut = [n_in, ns_in, n_out, ns_out, n_scratch]
        parts, off = [], 0
        for c in cut:
            parts.append(refs[off:off + c])
            off += c
        ins, s_ins, outs, s_outs, scratch = parts
        sems = refs[off:]
        if grid:
            ids = [pl.program_id(d) for d in range(len(grid))]
            first = functools.reduce(jnp.logical_and, [i == 0 for i in ids])
            last = functools.reduce(jnp.logical_and, [i == g - 1 for i, g in zip(ids, grid)])
            pl.when(first)(lambda: side.start(s_ins, s_outs, sems))
            body(*ins, *outs, *scratch)
            pl.when(last)(lambda: side.wait(s_ins, s_outs, sems))
        else:
            side.start(s_ins, s_outs, sems)
            side.wait(s_ins, s_outs, sems)

    return wrapped


def _call(body, args, *, name, grid, in_specs, out_specs, out_shape, semantics, scratch_shapes=(), side=None):
    if side is None:
        return pl.pallas_call(body, name=name, grid=grid, in_specs=in_specs, out_specs=out_specs, out_shape=out_shape,
                              scratch_shapes=list(scratch_shapes), compiler_params=_params(*semantics))(*args)
    single = not isinstance(out_shape, (list, tuple))
    shapes, specs = ([out_shape], [out_specs]) if single else (list(out_shape), list(out_specs))
    n_in, n_out = len(in_specs), len(shapes)
    res = pl.pallas_call(
        _side_body(body, side, n_in, n_out, len(scratch_shapes), grid), name=name, grid=grid,
        in_specs=list(in_specs) + [ANY] * len(side.ins), out_specs=specs + [ANY] * len(side.outs),
        out_shape=shapes + list(side.outs), scratch_shapes=list(scratch_shapes) + list(side.sems),
        input_output_aliases={n_in + i: n_out + o for i, o in side.aliases.items()},
        compiler_params=_params(*(["arbitrary"] * len(grid))),
    )(*args, *side.ins)
    return (res[0] if single else res[:n_out]), list(res[n_out:])


def run_side(side, *, name):
    return pl.pallas_call(_side_body(None, side, 0, 0, 0, ()), name=name, in_specs=[ANY] * len(side.ins),
                          out_specs=[ANY] * len(side.outs), out_shape=list(side.outs), scratch_shapes=list(side.sems),
                          input_output_aliases=dict(side.aliases))(*side.ins)


def _remote(src, dst, send, recv, k, to):
    return pltpu.make_async_remote_copy(src_ref=src, dst_ref=dst, send_sem=send.at[k], recv_sem=recv.at[k], device_id=to,
                                        device_id_type=MESH)


def gather_first(shards):
    na = len(shards)

    def copies(x_refs, out_refs, sems):
        send, recv, local = sems
        x, y, cc = _pos()
        me = 4 * x + 2 * y + cc
        peers = [(x, y, 1 - cc), (1 - x, y, cc), (x, 1 - y, cc), (1 - x, 1 - y, cc)]
        mine = [pltpu.make_async_copy(x_refs[a], out_refs[a].at[me], local.at[a]) for a in range(na)]
        sent = [_remote(x_refs[a], out_refs[a].at[me], send, recv, 4 * a + k, p) for a in range(na) for k, p in enumerate(peers)]
        landed = [_remote(x_refs[a], out_refs[a].at[4 * p[0] + 2 * p[1] + p[2]], send, recv, 4 * a + k, p)
                  for a in range(na) for k, p in enumerate(peers)]
        return mine, sent, landed

    def start(x_refs, out_refs, sems):
        mine, sent, _ = copies(x_refs, out_refs, sems)
        for cp in mine + sent:
            cp.start()

    def wait(x_refs, out_refs, sems):
        mine, sent, landed = copies(x_refs, out_refs, sems)
        for cp in sent:
            cp.wait_send()
        for cp in landed:
            cp.wait_recv()
        for cp in mine:
            cp.wait()

    return Side(list(shards), [jax.ShapeDtypeStruct((N_DEV,) + s.shape, s.dtype) for s in shards], {},
                [pltpu.SemaphoreType.DMA((4 * na,)), pltpu.SemaphoreType.DMA((4 * na,)), pltpu.SemaphoreType.DMA((na,))],
                start, wait)


def gather_second(slots):
    na = len(slots)

    def copies(out_refs, sems):
        send, recv = sems
        x, y, cc = _pos()
        chips = [(1 - x, y), (x, 1 - y), (1 - x, 1 - y)]
        sent, landed = [], []
        for a in range(na):
            for j, (px, py) in enumerate(chips):
                row = out_refs[a].at[4 * px + 2 * py + cc]
                sent.append(_remote(row, row, send, recv, 3 * a + j, (x, y, 1 - cc)))
                landed.append(_remote(row, out_refs[a].at[4 * px + 2 * py + 1 - cc], send, recv, 3 * a + j, (x, y, 1 - cc)))
        return sent, landed

    def start(_, out_refs, sems):
        for cp in copies(out_refs, sems)[0]:
            cp.start()

    def wait(_, out_refs, sems):
        sent, landed = copies(out_refs, sems)
        for cp in sent:
            cp.wait_send()
        for cp in landed:
            cp.wait_recv()

    return Side(list(slots), [jax.ShapeDtypeStruct(s.shape, s.dtype) for s in slots], {a: a for a in range(na)},
                [pltpu.SemaphoreType.DMA((3 * na,)), pltpu.SemaphoreType.DMA((3 * na,))], start, wait)


def _rows_of(ref, lead, rows):
    if rows is None:
        return ref if lead is None else ref.at[lead]
    cut = pl.ds(rows[0], rows[1])
    return ref.at[:, cut] if lead is None else ref.at[lead, cut]


def _side_into(arrays, out_shapes, into):
    na = len(arrays)
    if into is None:
        return list(arrays), out_shapes, {}
    return list(arrays) + list(into), out_shapes, {na + a: a for a in range(na)}


def grad_to_sibling(chunks, rows=None, into=None):
    na = len(chunks)

    def start(g_refs, out_refs, sems):
        send, recv = sems
        x, y, cc = _pos()
        for a in range(na):
            for q in range(4):
                _remote(_rows_of(g_refs[a], 2 * q + 1 - cc, rows), _rows_of(out_refs[a], q, rows), send, recv, a,
                        (x, y, 1 - cc)).start()

    def wait(g_refs, out_refs, sems):
        send, recv = sems
        x, y, cc = _pos()
        for a in range(na):
            whole = _rows_of(out_refs[a], None, rows)
            _remote(whole, whole, send, recv, a, (x, y, 1 - cc)).wait()

    ins, outs, aliases = _side_into(chunks, [jax.ShapeDtypeStruct((4,) + g.shape[1:], g.dtype) for g in chunks], into)
    return Side(ins, outs, aliases, [pltpu.SemaphoreType.DMA((na,)), pltpu.SemaphoreType.DMA((na,))], start, wait)


def grad_to_chips(parts, rows=None, into=None):
    na = len(parts)

    def copies(p_refs, out_refs, sems):
        send, recv, local = sems
        x, y, cc = _pos()
        chips = [(1 - x, y), (x, 1 - y), (1 - x, 1 - y)]
        mine = [pltpu.make_async_copy(_rows_of(p_refs[a], 2 * x + y, rows), _rows_of(out_refs[a], 3, rows), local.at[a])
                for a in range(na)]
        sent = [_remote(_rows_of(p_refs[a], 2 * px + py, rows), _rows_of(out_refs[a], k, rows), send, recv, 3 * a + k,
                        (px, py, cc)) for a in range(na) for k, (px, py) in enumerate(chips)]
        return mine, sent

    def start(p_refs, out_refs, sems):
        mine, sent = copies(p_refs, out_refs, sems)
        for cp in mine + sent:
            cp.start()

    def wait(p_refs, out_refs, sems):
        mine, sent = copies(p_refs, out_refs, sems)
        for cp in sent:
            cp.wait()
        for cp in mine:
            cp.wait()

    ins, outs, aliases = _side_into(parts, [jax.ShapeDtypeStruct(p.shape, p.dtype) for p in parts], into)
    return Side(ins, outs, aliases,
                [pltpu.SemaphoreType.DMA((3 * na,)), pltpu.SemaphoreType.DMA((3 * na,)), pltpu.SemaphoreType.DMA((na,))],
                start, wait)


def add_sibling(chunks, recv, *, name):
    _, r, c = chunks.shape
    tr = r if r <= 1024 else _tile(r, 512, 16)
    core = lax.axis_index("c").astype(jnp.int32).reshape(1)

    def body(core_ref, a_ref, b_ref, o_ref):
        o_ref[...] = (a_ref[...] + b_ref[...]).astype(bf16)

    return pl.pallas_call(
        body, name=name,
        grid_spec=pltpu.PrefetchScalarGridSpec(
            num_scalar_prefetch=1, grid=(4, r // tr),
            in_specs=[pl.BlockSpec((1, tr, c), lambda q, i, core_ref: (2 * q + core_ref[0], i, 0)),
                      pl.BlockSpec((1, tr, c), lambda q, i, core_ref: (q, i, 0))],
            out_specs=pl.BlockSpec((1, tr, c), lambda q, i, core_ref: (q, i, 0))),
        out_shape=jax.ShapeDtypeStruct((4, r, c), bf16), compiler_params=_params("parallel", "parallel"),
    )(core, chunks, recv)


def all_reduce_small(vec, *, name):
    r, c = vec.shape

    def body(v_ref, out_ref, buf_ref, send_sems, recv_sems):
        x, y, cc = _pos()
        my_id = 4 * x + 2 * y + cc
        buf_ref[my_id] = v_ref[...]
        flips = [(fx, fy, fc) for fx in (0, 1) for fy in (0, 1) for fc in (0, 1)][1:]
        cps = []
        for k, (fx, fy, fc) in enumerate(flips):
            peer = ((1 - x) if fx else x, (1 - y) if fy else y, (1 - cc) if fc else cc)
            cps.append(pltpu.make_async_remote_copy(src_ref=v_ref, dst_ref=buf_ref.at[my_id], send_sem=send_sems.at[k],
                                                    recv_sem=recv_sems.at[k], device_id=peer, device_id_type=MESH))
        for cp in cps:
            cp.start()
        for cp in cps:
            cp.wait()
        acc = buf_ref[0]
        for d in range(1, N_DEV):
            acc = acc + buf_ref[d]
        out_ref[...] = acc

    vm = pl.BlockSpec(memory_space=pltpu.VMEM)
    return pl.pallas_call(body, name=name, in_specs=[vm], out_specs=vm, out_shape=jax.ShapeDtypeStruct((r, c), f32),
                          scratch_shapes=[pltpu.VMEM((N_DEV, r, c), f32), pltpu.SemaphoreType.DMA((7,)),
                                          pltpu.SemaphoreType.DMA((7,))])(vec)


def _pack(parts, rows, dtype):
    flat = jnp.concatenate([p.reshape(-1).astype(dtype) for p in parts])
    return jnp.pad(flat, (0, rows * PACK_COLS - flat.shape[0])).reshape(rows, PACK_COLS)


def _unpack(flat, shapes):
    out, off = [], 0
    for shp in shapes:
        n = shp[0] * shp[1]
        out.append(flat[..., off:off + n].reshape(flat.shape[:-1] + tuple(shp)))
        off += n
    return out


def _from_column_shards(g):
    _, r, c = g.shape
    return jnp.transpose(g, (1, 0, 2)).reshape(r, N_DEV * c)


def _column_shards(full):
    r, c8 = full.shape
    return jnp.transpose(full.reshape(r, N_DEV, c8 // N_DEV), (1, 0, 2))


W_IN_SHARD = IN_DIM // N_DEV
W_IN_PAD = 1280
W_IN_PARTS = (("swa", 0, 0, 1280), ("swa", 1280, 5376, 5392), ("gdn", 0, 1280, 5376), ("gates", 0, 5392, IN_DIM))
W_IN_WIDTHS = {"swa": SWA_IN_W, "gdn": 4 * GDN_W, "gates": 2 * D_MODEL}


def _w_in_segments():
    segs = []
    for part, p0, g0, g1 in W_IN_PARTS:
        for j in range(N_DEV):
            lo, hi = max(g0, W_IN_SHARD * j), min(g1, W_IN_SHARD * (j + 1))
            if lo < hi:
                segs.append((part, p0 + lo - g0, j, lo - W_IN_SHARD * j, hi - lo))
    return segs


def split_w_in(shards, *, name):
    dt = shards.dtype
    tm = 256

    def body(w_ref, swa_ref, gdn_ref, gates_ref):
        out = {"swa": swa_ref, "gdn": gdn_ref, "gates": gates_ref}
        swa_ref[:, SWA_Q + 2 * SWA_KV + 2 * GDN_HEADS:] = jnp.zeros((tm, SWA_IN_W - SWA_Q - 2 * SWA_KV - 2 * GDN_HEADS), dt)
        for part, p0, j, l0, n in _w_in_segments():
            out[part][:, p0:p0 + n] = w_ref[j, :, l0:l0 + n]

    return pl.pallas_call(body, name=name, grid=(D_MODEL // tm,),
                          in_specs=[pl.BlockSpec((N_DEV, tm, W_IN_PAD), lambda i: (0, i, 0))],
                          out_specs=[pl.BlockSpec((tm, W_IN_WIDTHS[p]), lambda i: (i, 0)) for p in ("swa", "gdn", "gates")],
                          out_shape=[jax.ShapeDtypeStruct((D_MODEL, W_IN_WIDTHS[p]), dt) for p in ("swa", "gdn", "gates")],
                          compiler_params=_params("parallel"))(shards)


def merge_w_in_grad(d_swa, d_gdn, d_gates, *, name):
    tm = 256

    def body(swa_ref, gdn_ref, gates_ref, w_ref):
        src = {"swa": swa_ref, "gdn": gdn_ref, "gates": gates_ref}
        w_ref[:, :, W_IN_SHARD:] = jnp.zeros((N_DEV, tm, W_IN_PAD - W_IN_SHARD), f32)
        for part, p0, j, l0, n in _w_in_segments():
            w_ref[j, :, l0:l0 + n] = src[part][:, p0:p0 + n]

    return pl.pallas_call(body, name=name, grid=(D_MODEL // tm,),
                          in_specs=[pl.BlockSpec((tm, W_IN_WIDTHS[p]), lambda i: (i, 0)) for p in ("swa", "gdn", "gates")],
                          out_specs=pl.BlockSpec((N_DEV, tm, W_IN_PAD), lambda i: (0, i, 0)),
                          out_shape=jax.ShapeDtypeStruct((N_DEV, D_MODEL, W_IN_PAD), f32),
                          compiler_params=_params("parallel"))(d_swa, d_gdn, d_gates)


def kernel(x, mem, w_in, rel_bias, swa_sinks, gdn_conv_w, gdn_a_log, gdn_dt_bias, gdn_norm_w, w_br_swa, w_br_gdn, w_mix_o, ln1_g, ln1_b, w_mem_q, w_mem_kv, w_mem_o, ln2_g, ln2_b, w_up, ffn_conv_w, ffn_conv_b, w_down, ln3_g, ln3_b, loss_target, m_w_in, m_rel_bias, m_swa_sinks, m_gdn_conv_w, m_gdn_a_log, m_gdn_dt_bias, m_gdn_norm_w, m_w_br_swa, m_w_br_gdn, m_w_mix_o, m_ln1_g, m_ln1_b, m_w_mem_q, m_w_mem_kv, m_w_mem_o, m_ln2_g, m_ln2_b, m_w_up, m_ffn_conv_w, m_ffn_conv_b, m_w_down, m_ln3_g, m_ln3_b, v_w_in, v_rel_bias, v_swa_sinks, v_gdn_conv_w, v_gdn_a_log, v_gdn_dt_bias, v_gdn_norm_w, v_w_br_swa, v_w_br_gdn, v_w_mix_o, v_ln1_g, v_ln1_b, v_w_mem_q, v_w_mem_kv, v_w_mem_o, v_ln2_g, v_ln2_b, v_w_up, v_ffn_conv_w, v_ffn_conv_b, v_w_down, v_ln3_g, v_ln3_b):
    env = dict(locals())
    w2 = {n: (env[n][0] if env[n].ndim == 3 else env[n]) for n in WEIGHTS}
    m2 = {n: (env["m_" + n][0] if env["m_" + n].ndim == 3 else env["m_" + n]) for n in WEIGHTS}
    v2 = {n: (env["v_" + n][0] if env["v_" + n].ndim == 3 else env["v_" + n]) for n in WEIGHTS}
    xs, mems, target = x[0], mem[0], loss_target[0]
    my_id = 4 * lax.axis_index("x") + 2 * lax.axis_index("y") + lax.axis_index("c")
    pad_ff = FF_PAD - FF_SHARD

    pad_cols = {"w_in": W_IN_PAD - W_IN_SHARD, "w_up": pad_ff}
    mid = ("w_br_swa", "w_br_gdn", "w_mem_o", "w_mix_o", "w_mem_q", "w_mem_kv")
    mine = {n: jnp.pad(w2[n], ((0, 0), (0, pad_cols.get(n, 0)))).astype(bf16) for n in ("w_in", "w_up", "w_down") + mid}
    xb, got_in = cast_bf16(xs, name="cast_x", side=gather_first([mine["w_in"]]))
    got_in = run_side(gather_second(got_in), name="gather_w_in_pass_on")
    w_swa, w_gdn, w_gates = split_w_in(got_in[0], name="split_w_in")
    n_ffn, n_gdn = 3 * FF_SHARD, GDN_CONV * (QKV_W // N_DEV)
    conv_mine = jnp.concatenate([w2["ffn_conv_w"].reshape(-1), w2["gdn_conv_w"].reshape(-1)])[None]
    conv_rows = lax.dynamic_update_slice(jnp.zeros((N_DEV, n_ffn + n_gdn), f32), conv_mine, (my_id, 0))
    conv_all = all_reduce_small(_pack([conv_rows], CONV_ROWS, f32), name="gather_conv_w")
    conv_all = conv_all.reshape(-1)[:N_DEV * (n_ffn + n_gdn)].reshape(N_DEV, n_ffn + n_gdn)
    cwb = jnp.concatenate([conv_all[:, :n_ffn].reshape(N_DEV, 3, FF_SHARD), w2["ffn_conv_b"].reshape(N_DEV, 1, FF_SHARD),
                           jnp.zeros((N_DEV, 4, FF_SHARD), f32)], axis=1)
    cwb = jnp.pad(cwb, ((0, 0), (0, 0), (0, pad_ff)))
    convw = jnp.transpose(conv_all[:, n_ffn:].reshape(N_DEV, GDN_CONV, QKV_W // N_DEV), (1, 0, 2)).reshape(GDN_CONV, QKV_W)
    convw = jnp.pad(convw, ((0, 4), (0, 0)))
    onehot = _bucket_onehot()
    bias = mm(w2["rel_bias"].T, onehot, "nn", hi=True, tn=4096, name="rel_bias_table").reshape(SWA_HEADS, BLOCK, 2 * BLOCK)
    alog_row = jnp.pad(w2["gdn_a_log"], ((0, 0), (GDN_HEADS, 128 - 2 * GDN_HEADS)))
    dt_row = jnp.pad(w2["gdn_dt_bias"], ((0, 0), (GDN_HEADS, 128 - 2 * GDN_HEADS)))

    memb = cast_bf16(mems, name="cast_mem")
    gates, mid_got = mm(xb, w_gates, "nn", out_dtype=bf16, name="proj_gates", side=gather_first([mine[n] for n in mid]))
    gdn_in, mid_got = mm(xb, w_gdn, "nn", name="proj_gdn", side=gather_second(mid_got))
    got = dict(zip(mid, mid_got))
    w_br_swa, w_br_gdn, w_mem_o = (_from_column_shards(got[n]) for n in ("w_br_swa", "w_br_gdn", "w_mem_o"))
    w_mix_o = got["w_mix_o"].reshape(D_MODEL, D_MODEL)
    w_mem_q = got["w_mem_q"].reshape(D_MODEL, MEM_W)
    w_mem_kv = got["w_mem_kv"].reshape(D_MODEL, 2 * MEM_W)
    swa_in = mm(xb, w_swa, "nn", tn=SWA_IN_W, name="proj_swa")
    attn, down_got = swa_fwd(swa_in, bias, w2["swa_sinks"], name="swa_fwd", side=gather_first([mine["w_down"]]))
    qn, kn, vv, gdn_conv = gdn_pre_fwd(gdn_in, convw, name="gdn_pre_fwd")
    gbeta = gbeta_fwd(swa_in, alog_row, dt_row, name="gbeta_fwd")
    (o_gdn, states, inverses), up_got = gdn_scan_fwd(qn, kn, vv, gbeta, name="gdn_scan_fwd",
                                                     side=gather_first([mine["w_up"]]))
    ygd = gdn_post_fwd(o_gdn, gdn_in, w2["gdn_norm_w"], name="gdn_post_fwd")
    y_swa, down_got = mm(attn, w_br_swa, "nn", out_dtype=bf16, name="br_swa", side=gather_second(down_got))
    y_gdn, up_got = mm(ygd, w_br_gdn, "nn", out_dtype=bf16, name="br_gdn", side=gather_second(up_got))
    w_up_blocked = up_got[0]
    w_down_p = jnp.pad(down_got[0].reshape(4, FF_SHARD, D_MODEL), ((0, 0), (0, pad_ff), (0, 0))).reshape(4 * FF_PAD, D_MODEL)
    mixed = merge_fwd(gates, y_swa, y_gdn, name="merge_fwd")
    z1 = mm(mixed, w_mix_o, "nn", add=xs, add_scale=ALPHA, name="mix_o")
    x1, x1b = ln_fwd(z1, w2["ln1_g"], w2["ln1_b"], name="ln1_fwd")
    qm = mm(x1b, w_mem_q, "nn", name="mem_q")
    kv = mm(memb, w_mem_kv, "nn", name="mem_kv")
    om = memattn_fwd(qm, kv, name="memattn_fwd")
    z2 = mm(om, w_mem_o, "nn", add=x1, add_scale=ALPHA, name="mem_o")
    x2, x2b = ln_fwd(z2, w2["ln2_g"], w2["ln2_b"], name="ln2_fwd")
    hpre = mm(x2b, w_up_blocked, "nn", b_blocked=True, out_dtype=bf16, name="ffn_up")
    act, conv_g, conv_u = ffn_act_fwd(hpre, cwb, name="ffn_act_fwd")
    z3 = mm(act, w_down_p, "nn", add=x2, add_scale=ALPHA, tk=2 * FF_PAD, name="ffn_down")
    dz3, dz3b, d_ln3g, d_ln3b, loss = ln_loss(z3, target, w2["ln3_g"], w2["ln3_b"], name="ln3_loss")

    dact = mm(dz3b, w_down_p, "nt", tn=FF_PAD, out_dtype=bf16, name="d_act")
    d_wdown_p = mm(act, dz3b, "tn", tm=FF_PAD, name="dw_down")
    d_hpre, d_cwb = ffn_act_bwd(hpre, conv_g, conv_u, dact, cwb, name="ffn_act_bwd")
    def sibling_sums(names, chunks, received):
        return [add_sibling(c, r, name="grad_add_sibling_" + n) for n, c, r in zip(names, chunks, received)]

    dx2 = mm(d_hpre, w_up_blocked, "nt", b_blocked=True, k_shards=2, add=dz3, add_scale=ALPHA, name="d_x2")
    d_wup = mm(x2b, d_hpre, "tn", out_blocked=True, name="dw_up")
    ffn = ("w_up", "w_down")
    ffn_chunks = [d_wup, d_wdown_p.reshape(4, FF_PAD, D_MODEL)[:, :FF_SHARD].reshape(N_DEV, FF_SHARD // 2, D_MODEL)]
    dz2, dz2b, d_ln2g, d_ln2b = ln_bwd(dx2, z2, w2["ln2_g"], name="ln2_bwd")
    d_om, down_received = mm(dz2b, w_mem_o, "nt", name="d_om", side=grad_to_sibling(ffn_chunks[1:]))
    d_wmemo = mm(om, dz2b, "tn", name="dw_mem_o")
    dqm, dkv = memattn_bwd(qm, kv, d_om, name="memattn_bwd")
    dx1 = mm(dqm, w_mem_q, "nt", add=dz2, add_scale=ALPHA, name="d_x1")
    d_wmemq = mm(x1b, dqm, "tn", name="dw_mem_q")
    d_wmemkv = mm(memb, dkv, "tn", name="dw_mem_kv")
    dz1, dz1b, d_ln1g, d_ln1b = ln_bwd(dx1, z1, w2["ln1_g"], name="ln1_bwd")
    half = D_MODEL // 2
    dmix, up_received = mm(dz1b, w_mix_o, "nt", name="d_mixed", side=grad_to_sibling(ffn_chunks[:1], rows=(0, half)))
    d_wmixo, up_received = mm(mixed, dz1b, "tn", tk=4096, name="dw_mix_o",
                              side=grad_to_sibling(ffn_chunks[:1], rows=(half, half), into=up_received))
    ffn_sums = sibling_sums(ffn, ffn_chunks, up_received + down_received)
    dys, dyg, d_gates = merge_bwd(gates, y_swa, y_gdn, dmix, name="merge_bwd")
    d_attn = mm(dys, w_br_swa, "nt", name="d_attn")
    d_wbrswa = mm(attn, dys, "tn", tk=4096, name="dw_br_swa")
    d_ygd = mm(dyg, w_br_gdn, "nt", name="d_ygd")
    d_wbrgdn = mm(ygd, dyg, "tn", tk=4096, name="dw_br_gdn")
    mid_chunks = [_column_shards(d_wbrswa), _column_shards(d_wbrgdn), _column_shards(d_wmemo),
                  d_wmixo.reshape(N_DEV, D_MODEL // N_DEV, D_MODEL), d_wmemq.reshape(N_DEV, D_MODEL // N_DEV, MEM_W),
                  d_wmemkv.reshape(N_DEV, D_MODEL // N_DEV, 2 * MEM_W)]
    d_o, d_gz, d_normw = gdn_post_bwd(o_gdn, gdn_in, w2["gdn_norm_w"], d_ygd, name="gdn_post_bwd")
    (dqn, dkn, dvv, dgbeta), received = gdn_scan_bwd(
        qn, kn, vv, gbeta, states, inverses, d_o, name="gdn_scan_bwd",
        side=join_sides(grad_to_chips(ffn_sums), grad_to_sibling(mid_chunks)))
    chip_parts = dict(zip(ffn, received[:2]))
    mid_sums = sibling_sums(mid, mid_chunks, received[2:])
    d_gdn_in, d_convw = gdn_pre_bwd(gdn_in, gdn_conv, convw, dqn, dkn, dvv, d_gz, name="gdn_pre_bwd")
    d_ba, d_alog, d_dt = gbeta_bwd(swa_in, alog_row, dt_row, dgbeta, name="gbeta_bwd")
    (dq, dkc, dkp, dvc, dvp, d_bias, d_sinks), received = swa_bwd(swa_in, bias, w2["swa_sinks"], d_attn, name="swa_bwd",
                                                                  side=grad_to_chips(mid_sums))
    chip_parts.update(zip(mid, received))
    d_swa_in = swa_in_grad(dq, dkc, dkp, dvc, dvp, d_ba, name="swa_in_grad")
    d_relbias = mm(d_bias.reshape(SWA_HEADS, -1), onehot, "nt", hi=True, tk=4096, name="d_rel_bias").T
    d_wgates = mm(xb, d_gates, "tn", tk=4096, name="dw_gates")
    d_wgdn = mm(xb, d_gdn_in, "tn", tk=4096, name="dw_gdn")
    d_wswa = mm(xb, d_swa_in, "tn", tn=SWA_IN_W, name="dw_swa")
    in_chunks = [merge_w_in_grad(d_wswa, d_wgdn, d_wgates, name="merge_w_in_grad")]
    gx, received = mm(d_swa_in, w_swa, "nt", add=dz1, add_scale=ALPHA, tk=SWA_IN_W, name="dx_swa",
                      side=grad_to_sibling(in_chunks))
    in_sums = sibling_sums(("w_in",), in_chunks, received)
    gx, received = mm(d_gates, w_gates, "nt", add=gx, name="dx_gates", side=grad_to_chips(in_sums, rows=(0, half)))
    gx, received = mm(d_gdn_in, w_gdn, "nt", add=gx, name="dx_gdn",
                      side=grad_to_chips(in_sums, rows=(half, half), into=received))
    chip_parts["w_in"] = received[0]
    grads = {}

    gsmall = {
        "rel_bias": d_relbias, "swa_sinks": d_sinks[:, :SWA_HEADS], "gdn_a_log": d_alog[:, GDN_HEADS:2 * GDN_HEADS],
        "gdn_dt_bias": d_dt[:, GDN_HEADS:2 * GDN_HEADS], "gdn_norm_w": d_normw, "ln1_g": d_ln1g, "ln1_b": d_ln1b,
        "ln2_g": d_ln2g, "ln2_b": d_ln2b, "ln3_g": d_ln3g, "ln3_b": d_ln3b,
        "ffn_conv_b": d_cwb[:, 3, :FF_SHARD].reshape(1, 2 * D_FF),
        "ffn_conv_w": jnp.transpose(d_cwb[:, :3, :FF_SHARD], (1, 0, 2)).reshape(3, 2 * D_FF),
        "gdn_conv_w": d_convw[:GDN_CONV],
    }
    small_shapes = [shp for _, shp in SMALL] + [(3, 2 * D_FF), (GDN_CONV, QKV_W)]
    small_names = [n for n, _ in SMALL] + ["ffn_conv_w", "gdn_conv_w"]
    small_sum = all_reduce_small(_pack([gsmall[n] for n in small_names], AR_ROWS, f32), name="all_reduce_small")
    grads.update(zip(small_names, _unpack(small_sum.reshape(-1), small_shapes)))
    grads["ffn_conv_w"] = lax.dynamic_slice_in_dim(grads["ffn_conv_w"], my_id * FF_SHARD, FF_SHARD, axis=1)
    grads["gdn_conv_w"] = lax.dynamic_slice_in_dim(grads["gdn_conv_w"], my_id * (QKV_W // N_DEV), QKV_W // N_DEV, axis=1)

    big = [n for n, shp, _ in SHARDED if shp[0] * shp[1] > 8192]
    tiny = [n for n in WEIGHTS if n not in big]
    delta, new_m, new_v = {}, {}, {}
    for n in big:
        grads[n], delta[n], new_m[n], new_v[n] = adamw_of_partial_sums(w2[n], chip_parts[n], m2[n], v2[n], name="adamw_" + n)
    tiny_shapes = [w2[n].shape for n in tiny]
    packed = [_pack([src[n] for n in tiny], SMALL_ROWS, f32) for src in (w2, grads, m2, v2)]
    for dst, res in zip((delta, new_m, new_v), adamw(*packed, name="adamw_small")):
        dst.update(zip(tiny, _unpack(res.reshape(-1), tiny_shapes)))

    def shaped(d):
        return [d[n].reshape(env[n].shape) for n in WEIGHTS]

    loss_all = lax.psum(loss[0, 0], ("x", "y", "c"))
    return (loss_all, gx[None], *shaped(grads), *shaped(delta), *shaped(new_m), *shaped(new_v))
```

```python
import functools
import math
from typing import Callable, NamedTuple

import jax
import jax.numpy as jnp
from jax import lax
from jax.experimental import pallas as pl
from jax.experimental.pallas import tpu as pltpu

f32 = jnp.float32
bf16 = jnp.bfloat16
HI = lax.Precision.HIGHEST
MESH = pl.DeviceIdType.MESH

D_MODEL = 2048
N_DEV = 8
SWA_HEADS, SWA_KV_HEADS, SWA_HEAD_DIM, BLOCK = 16, 2, 64, 128
REL_BUCKETS, REL_MAX_DIST = 32, 128
GDN_HEADS, GDN_HEAD_DIM, GDN_CONV, GDN_CHUNK = 8, 128, 4, 64
MEM_HEADS, MEM_HEAD_DIM = 4, 128
D_FF = 5504
FF_SHARD = 2 * D_FF // N_DEV
FF_PAD = 1408
NORM_EPS = 1e-5
ALPHA = 2.0 ** 0.25
NEG_INF = -1e30
SWA_Q, SWA_KV, GDN_W, MEM_W = 1024, 128, 1024, 512
IN_DIM = 9488
HALO = 8

ADAM_LR, ADAM_B1, ADAM_B2, ADAM_EPS, ADAM_WD, ADAM_STEP = 0.001, 0.9, 0.999, 1e-08, 0.01, 10

PACK_COLS = 1024
SMALL_ROWS = 32
AR_ROWS = 72
CONV_ROWS = 48

SHARDED = (
    ("w_in", (2048, 1186), 1), ("w_br_swa", (1024, 256), 1), ("w_br_gdn", (1024, 256), 1),
    ("w_mix_o", (256, 2048), 0), ("w_mem_q", (256, 512), 0), ("w_mem_kv", (256, 1024), 0),
    ("w_mem_o", (512, 256), 1), ("w_up", (2048, 1376), 1), ("w_down", (688, 2048), 0),
    ("ffn_conv_w", (3, 1376), 1), ("gdn_conv_w", (4, 384), 1),
)
SMALL = (
    ("rel_bias", (32, 16)), ("swa_sinks", (1, 16)), ("gdn_a_log", (1, 8)), ("gdn_dt_bias", (1, 8)),
    ("gdn_norm_w", (1, 128)), ("ln1_g", (1, 2048)), ("ln1_b", (1, 2048)), ("ln2_g", (1, 2048)),
    ("ln2_b", (1, 2048)), ("ln3_g", (1, 2048)), ("ln3_b", (1, 2048)), ("ffn_conv_b", (1, 11008)),
)
WEIGHTS = ("w_in", "rel_bias", "swa_sinks", "gdn_conv_w", "gdn_a_log", "gdn_dt_bias", "gdn_norm_w", "w_br_swa",
           "w_br_gdn", "w_mix_o", "ln1_g", "ln1_b", "w_mem_q", "w_mem_kv", "w_mem_o", "ln2_g", "ln2_b", "w_up",
           "ffn_conv_w", "ffn_conv_b", "w_down", "ln3_g", "ln3_b")


def _tile(n, target, align):
    if n <= target:
        return n
    t = (target // align) * align
    while t >= align:
        if n % t == 0:
            return t
        t -= align
    return n


VMEM_LIMIT_BYTES = 56 * 1024 * 1024


def _params(*sem):
    return pltpu.CompilerParams(dimension_semantics=sem, vmem_limit_bytes=VMEM_LIMIT_BYTES)


def _sigmoid(v):
    return jax.nn.sigmoid(v)


def _d16(a, b, dims):
    return lax.dot_general(a.astype(bf16), b.astype(bf16), (dims, ((), ())), preferred_element_type=f32)


NN = ((1,), (0,))
NT = ((1,), (1,))
TN = ((0,), (0,))


def mm(a, b, mode, *, name, add=None, add_scale=1.0, out_dtype=f32, hi=False, tm=1024, tn=1024, tk=2048,
       b_blocked=False, out_blocked=False, k_shards=1, side=None):
    if b_blocked:
        nb, rows, width = b.shape
        if mode == "nn":
            (m, k), n, tn = a.shape, nb * width, width
        else:
            (m, k), n, tk = a.shape, rows, k_shards * width
    elif mode == "nn":
        (m, k), (_, n) = a.shape, b.shape
    elif mode == "nt":
        (m, k), (n, _) = a.shape, b.shape
    else:
        (k, m), (_, n) = a.shape, b.shape
    if out_blocked:
        tn = n // N_DEV
    tm, tn, tk = _tile(m, tm, 8 if mode != "tn" else 128), _tile(n, tn, 128), _tile(k, tk, 128 if mode != "tn" else 8)
    nk = k // tk
    dims = {"nn": NN, "nt": NT, "tn": TN}[mode]
    a_spec = pl.BlockSpec((tk, tm), lambda i, j, kk: (kk, i)) if mode == "tn" else pl.BlockSpec((tm, tk), lambda i, j, kk: (i, kk))
    if b_blocked:
        b_spec = (pl.BlockSpec((None, tk, tn), lambda i, j, kk: (j, kk, 0)) if mode == "nn"
                  else pl.BlockSpec((k_shards, tn, tk // k_shards), lambda i, j, kk: (kk, j, 0)))
    else:
        b_spec = pl.BlockSpec((tn, tk), lambda i, j, kk: (j, kk)) if mode == "nt" else pl.BlockSpec((tk, tn), lambda i, j, kk: (kk, j))
    if out_blocked:
        o_spec, o_shape = pl.BlockSpec((None, tm, tn), lambda i, j, kk: (j, i, 0)), (N_DEV, m, tn)
    else:
        o_spec, o_shape = pl.BlockSpec((tm, tn), lambda i, j, kk: (i, j)), (m, n)
    has_add = add is not None

    def product(a_ref, b_ref):
        if hi:
            return lax.dot_general(a_ref[...], b_ref[...], (dims, ((), ())), precision=HI, preferred_element_type=f32)
        if b_blocked and mode == "nt":
            width = tk // k_shards
            parts = [_d16(a_ref[:, s * width:(s + 1) * width], b_ref[s], dims) for s in range(k_shards)]
            return functools.reduce(lambda p, q: p + q, parts)
        return _d16(a_ref[...], b_ref[...], dims)

    def finish(r, add_ref, o_ref):
        if has_add:
            r = r + add_scale * add_ref[...]
        o_ref[...] = r.astype(out_dtype)

    def body_one_step(a_ref, b_ref, *rest):
        finish(product(a_ref, b_ref), rest[0] if has_add else None, rest[-1])

    def body_k_steps(a_ref, b_ref, *rest):
        o_ref, acc_ref = rest[-2:]
        kk = pl.program_id(2)

        @pl.when(kk == 0)
        def _():
            acc_ref[...] = jnp.zeros_like(acc_ref)

        acc_ref[...] += product(a_ref, b_ref)

        @pl.when(kk == nk - 1)
        def _():
            finish(acc_ref[...], rest[0] if has_add else None, o_ref)

    return _call(body_one_step if nk == 1 else body_k_steps, (a, b, add) if has_add else (a, b), name=name,
                 grid=(m // tm, n // tn, nk), in_specs=[a_spec, b_spec] + ([o_spec] if has_add else []), out_specs=o_spec,
                 out_shape=jax.ShapeDtypeStruct(o_shape, out_dtype),
                 scratch_shapes=[] if nk == 1 else [pltpu.VMEM((tm, tn), f32)],
                 semantics=("parallel", "parallel", "arbitrary"), side=side)


def cast_bf16(a, *, name, side=None):
    m, n = a.shape
    tm = _tile(m, 512, 16)

    def body(a_ref, o_ref):
        o_ref[...] = a_ref[...].astype(bf16)

    return _call(body, (a,), name=name, grid=(m // tm,), in_specs=[pl.BlockSpec((tm, n), lambda i: (i, 0))],
                 out_specs=pl.BlockSpec((tm, n), lambda i: (i, 0)), out_shape=jax.ShapeDtypeStruct((m, n), bf16),
                 semantics=("parallel",), side=side)


def _ln_stats(z):
    mu = jnp.mean(z, axis=-1, keepdims=True)
    zc = z - mu
    var = jnp.mean(zc * zc, axis=-1, keepdims=True)
    rstd = lax.rsqrt(var + NORM_EPS)
    return zc * rstd, rstd


def ln_fwd(z, g, b, *, name):
    s, d = z.shape
    tm = _tile(s, 512, 16)

    def body(z_ref, g_ref, b_ref, y_ref, yb_ref):
        xhat, _ = _ln_stats(z_ref[...])
        y = xhat * g_ref[...] + b_ref[...]
        y_ref[...] = y
        yb_ref[...] = y.astype(bf16)

    row = pl.BlockSpec((tm, d), lambda i: (i, 0))
    vec = pl.BlockSpec((1, d), lambda i: (0, 0))
    return pl.pallas_call(body, name=name, grid=(s // tm,), in_specs=[row, vec, vec], out_specs=[row, row],
                          out_shape=[jax.ShapeDtypeStruct((s, d), f32), jax.ShapeDtypeStruct((s, d), bf16)],
                          compiler_params=_params("parallel"))(z, g, b)


def _ln_bwd_tile(dy, z, g):
    xhat, rstd = _ln_stats(z)
    dxh = dy * g
    m1 = jnp.mean(dxh, axis=-1, keepdims=True)
    m2 = jnp.mean(dxh * xhat, axis=-1, keepdims=True)
    dz = rstd * (dxh - m1 - xhat * m2)
    return dz, jnp.sum(dy * xhat, axis=0, keepdims=True), jnp.sum(dy, axis=0, keepdims=True)


def ln_bwd(dy, z, g, *, name):
    s, d = z.shape
    tm = _tile(s, 256, 16)

    def body(dy_ref, z_ref, g_ref, dz_ref, dzb_ref, dg_ref, db_ref):
        @pl.when(pl.program_id(0) == 0)
        def _():
            dg_ref[...] = jnp.zeros_like(dg_ref)
            db_ref[...] = jnp.zeros_like(db_ref)

        dz, dg, db = _ln_bwd_tile(dy_ref[...], z_ref[...], g_ref[...])
        dz_ref[...] = dz
        dzb_ref[...] = dz.astype(bf16)
        dg_ref[...] += dg
        db_ref[...] += db

    row = pl.BlockSpec((tm, d), lambda i: (i, 0))
    vec = pl.BlockSpec((1, d), lambda i: (0, 0))
    return pl.pallas_call(body, name=name, grid=(s // tm,), in_specs=[row, row, vec], out_specs=[row, row, vec, vec],
                          out_shape=[jax.ShapeDtypeStruct((s, d), f32), jax.ShapeDtypeStruct((s, d), bf16),
                                     jax.ShapeDtypeStruct((1, d), f32), jax.ShapeDtypeStruct((1, d), f32)],
                          compiler_params=_params("arbitrary"))(dy, z, g)


def ln_loss(z, target, g, b, *, name):
    s, d = z.shape
    tm = _tile(s, 256, 16)
    nt = s // tm

    def body(z_ref, t_ref, g_ref, b_ref, dz_ref, dzb_ref, dg_ref, db_ref, loss_ref, lacc_ref):
        i = pl.program_id(0)

        @pl.when(i == 0)
        def _():
            dg_ref[...] = jnp.zeros_like(dg_ref)
            db_ref[...] = jnp.zeros_like(db_ref)
            lacc_ref[...] = jnp.zeros_like(lacc_ref)

        zv, gv = z_ref[...], g_ref[...]
        xhat, _ = _ln_stats(zv)
        err = xhat * gv + b_ref[...] - t_ref[...]
        lacc_ref[...] += jnp.sum(err * err, axis=0, keepdims=True)
        dz, dg, db = _ln_bwd_tile(err * (1.0 / d), zv, gv)
        dz_ref[...] = dz
        dzb_ref[...] = dz.astype(bf16)
        dg_ref[...] += dg
        db_ref[...] += db

        @pl.when(i == nt - 1)
        def _():
            loss_ref[...] = (0.5 / d) * jnp.sum(lacc_ref[...], axis=1, keepdims=True)

    row = pl.BlockSpec((tm, d), lambda i: (i, 0))
    vec = pl.BlockSpec((1, d), lambda i: (0, 0))
    return pl.pallas_call(body, name=name, grid=(nt,), in_specs=[row, row, vec, vec],
                          out_specs=[row, row, vec, vec, pl.BlockSpec((1, 1), lambda i: (0, 0))],
                          out_shape=[jax.ShapeDtypeStruct((s, d), f32), jax.ShapeDtypeStruct((s, d), bf16),
                                     jax.ShapeDtypeStruct((1, d), f32), jax.ShapeDtypeStruct((1, d), f32),
                                     jax.ShapeDtypeStruct((1, 1), f32)],
                          scratch_shapes=[pltpu.VMEM((1, d), f32)],
                          compiler_params=_params("arbitrary"))(z, target, g, b)


def merge_fwd(gates, ys, yg, *, name):
    s, d = ys.shape
    tm = _tile(s, 512, 16)

    def body(gt_ref, ys_ref, yg_ref, o_ref):
        o_ref[...] = (_sigmoid(gt_ref[:, :d].astype(f32)) * ys_ref[...].astype(f32)
                      + _sigmoid(gt_ref[:, d:].astype(f32)) * yg_ref[...].astype(f32)).astype(bf16)

    row = pl.BlockSpec((tm, d), lambda i: (i, 0))
    return pl.pallas_call(body, name=name, grid=(s // tm,), in_specs=[pl.BlockSpec((tm, 2 * d), lambda i: (i, 0)), row, row],
                          out_specs=row, out_shape=jax.ShapeDtypeStruct((s, d), bf16),
                          compiler_params=_params("parallel"))(gates, ys, yg)


def merge_bwd(gates, ys, yg, dmix, *, name):
    s, d = ys.shape
    tm = _tile(s, 256, 16)

    def body(gt_ref, ys_ref, yg_ref, dm_ref, dys_ref, dyg_ref, dgt_ref):
        dm = dm_ref[...]
        sa, sb = _sigmoid(gt_ref[:, :d].astype(f32)), _sigmoid(gt_ref[:, d:].astype(f32))
        dys_ref[...] = (dm * sa).astype(bf16)
        dyg_ref[...] = (dm * sb).astype(bf16)
        dgt_ref[:, :d] = (dm * ys_ref[...].astype(f32) * sa * (1.0 - sa)).astype(bf16)
        dgt_ref[:, d:] = (dm * yg_ref[...].astype(f32) * sb * (1.0 - sb)).astype(bf16)

    row = pl.BlockSpec((tm, d), lambda i: (i, 0))
    wide = pl.BlockSpec((tm, 2 * d), lambda i: (i, 0))
    return pl.pallas_call(body, name=name, grid=(s // tm,), in_specs=[wide, row, row, row], out_specs=[row, row, wide],
                          out_shape=[jax.ShapeDtypeStruct((s, d), bf16), jax.ShapeDtypeStruct((s, d), bf16),
                                     jax.ShapeDtypeStruct((s, 2 * d), bf16)],
                          compiler_params=_params("parallel"))(gates, ys, yg, dmix)


def _shift_down(ext, j):
    return ext if j == 0 else pltpu.roll(ext, j, 0)


def _shift_up(ext, j):
    return ext if j == 0 else pltpu.roll(ext, ext.shape[0] - j, 0)


def _conv_taps(ext, width):
    return [_shift_down(ext, width - 1 - j)[HALO:] for j in range(width)]


def _causal_conv(taps, w_ref):
    acc = None
    for j, tap in enumerate(taps):
        term = w_ref[j:j + 1, :] * tap
        acc = term if acc is None else acc + term
    return acc


def _conv_grads(dy_ext, x, w_ref, width, rows):
    ahead = [_shift_up(dy_ext, width - 1 - j)[:rows] for j in range(width)]
    dx = None
    for j in range(width):
        term = w_ref[j:j + 1, :] * ahead[j]
        dx = term if dx is None else dx + term
    return dx, [jnp.sum(x * ahead[j], axis=0, keepdims=True) for j in range(width)]


def _rows_to_block(rows, n_rows, cols):
    r = lax.broadcasted_iota(jnp.int32, (n_rows, cols), 0)
    out = jnp.zeros((n_rows, cols), f32)
    for j, v in enumerate(rows):
        out = out + jnp.where(r == j, v, 0.0)
    return out


def _silu_and_grad(v):
    sg = _sigmoid(v)
    return v * sg, sg * (1.0 + v * (1.0 - sg))


HALO_BF16 = 16


def ffn_act_fwd(hpre, cwb, *, name):
    s = hpre.shape[0]
    tm = _tile(s, 512, 16)
    hb = tm // HALO_BF16

    def body(hg_ref, hgp_ref, hu_ref, hup_ref, cg_ref, cu_ref, o_ref, g_ref, u_ref):
        first = pl.program_id(1) == 0

        def conv(h_ref, hp_ref, c_ref):
            prev = jnp.where(first, 0.0, hp_ref[...].astype(f32)[HALO_BF16 - HALO:])
            ext = jnp.concatenate([prev, h_ref[...].astype(f32)], axis=0)
            return _causal_conv(_conv_taps(ext, 3), c_ref.at[0]) + c_ref[0, 3:4, :]

        g = conv(hg_ref, hgp_ref, cg_ref)
        u = conv(hu_ref, hup_ref, cu_ref)
        g_ref[...] = g.astype(bf16)
        u_ref[...] = u.astype(bf16)
        o_ref[...] = (g * _sigmoid(g) * u).astype(bf16)

    def tile(off):
        return pl.BlockSpec((tm, FF_PAD), lambda j, i: (i, j + off))

    def halo(off):
        return pl.BlockSpec((HALO_BF16, FF_PAD), lambda j, i: (jnp.maximum(i * hb - 1, 0), j + off))

    def taps(off):
        return pl.BlockSpec((1, 8, FF_PAD), lambda j, i: (j + off, 0, 0))

    out = pl.BlockSpec((tm, FF_PAD), lambda j, i: (i, j))
    return pl.pallas_call(body, name=name, grid=(4, s // tm),
                          in_specs=[tile(0), halo(0), tile(4), halo(4), taps(0), taps(4)], out_specs=[out, out, out],
                          out_shape=[jax.ShapeDtypeStruct((s, 4 * FF_PAD), bf16)] * 3,
                          compiler_params=_params("parallel", "parallel"))(hpre, hpre, hpre, hpre, cwb, cwb)


def ffn_act_bwd(hpre, conv_g, conv_u, dact, cwb, *, name):
    s = hpre.shape[0]
    tm = _tile(s, 512, 16)
    hb = tm // HALO_BF16
    nt = s // tm
    last_hb = s // HALO_BF16 - 1

    def body(hg_ref, hu_ref, g_ref, gn_ref, u_ref, un_ref, d_ref, dn_ref, cg_ref, cu_ref, dh_ref, dcg_ref, dcu_ref,
             buf_ref, sems):
        j, i = pl.program_id(0), pl.program_id(1)
        step = j * nt + i
        slot = step % 2

        def writes(from_slot):
            rows = pl.ds(pl.multiple_of(i * tm, tm), tm)
            return [pltpu.make_async_copy(buf_ref.at[from_slot, half],
                                          dh_ref.at[rows, pl.ds(pl.multiple_of((j + 4 * half) * FF_PAD, 128), FF_PAD)],
                                          sems.at[from_slot, half]) for half in (0, 1)]

        @pl.when(i == 0)
        def _():
            dcg_ref[...] = jnp.zeros_like(dcg_ref)
            dcu_ref[...] = jnp.zeros_like(dcu_ref)

        @pl.when(step >= 2)
        def _():
            for cp in writes(slot):
                cp.wait()

        def with_future(t_ref, n_ref):
            return jnp.concatenate([t_ref[...].astype(f32), n_ref[...].astype(f32)[:HALO]], axis=0)

        g, u = with_future(g_ref, gn_ref), with_future(u_ref, un_ref)
        d = jnp.concatenate([d_ref[...].astype(f32), jnp.where(i == nt - 1, 0.0, dn_ref[...].astype(f32)[:HALO])], axis=0)
        act, dact_dg = _silu_and_grad(g)
        dg = d * u * dact_dg
        du = d * act
        dhg, dwg = _conv_grads(dg, hg_ref[...].astype(f32), cg_ref.at[0], 3, tm)
        dhu, dwu = _conv_grads(du, hu_ref[...].astype(f32), cu_ref.at[0], 3, tm)
        buf_ref[slot, 0] = dhg.astype(bf16)
        buf_ref[slot, 1] = dhu.astype(bf16)
        for cp in writes(slot):
            cp.start()
        dcg_ref[0] += _rows_to_block(dwg + [jnp.sum(dg[:tm], axis=0, keepdims=True)], 8, FF_PAD)
        dcu_ref[0] += _rows_to_block(dwu + [jnp.sum(du[:tm], axis=0, keepdims=True)], 8, FF_PAD)

        @pl.when(step == 4 * nt - 1)
        def _():
            for cp in writes(slot) + writes(1 - slot):
                cp.wait()

    def tile(off):
        return pl.BlockSpec((tm, FF_PAD), lambda j, i: (i, j + off))

    nxt = pl.BlockSpec((HALO_BF16, FF_PAD), lambda j, i: (jnp.minimum((i + 1) * hb, last_hb), j))
    taps = [pl.BlockSpec((1, 8, FF_PAD), lambda j, i, off=off: (j + off, 0, 0)) for off in (0, 4)]
    dh, dcg, dcu = pl.pallas_call(
        body, name=name, grid=(4, nt),
        in_specs=[tile(0), tile(4), tile(0), nxt, tile(0), nxt, tile(0), nxt] + taps,
        out_specs=[ANY, taps[0], taps[0]],
        out_shape=[jax.ShapeDtypeStruct((s, 8 * FF_PAD), bf16),
                   jax.ShapeDtypeStruct((4, 8, FF_PAD), f32), jax.ShapeDtypeStruct((4, 8, FF_PAD), f32)],
        scratch_shapes=[pltpu.VMEM((2, 2, tm, FF_PAD), bf16), pltpu.SemaphoreType.DMA((2, 2))],
        compiler_params=_params("arbitrary", "arbitrary"),
    )(hpre, hpre, conv_g, conv_g, conv_u, conv_u, dact, dact, cwb, cwb)
    return dh, jnp.concatenate([dcg, dcu], axis=0)


MEM_SCALE = MEM_HEAD_DIM ** -0.5


def _softmax_rows(sc):
    m = jnp.max(sc, axis=-1, keepdims=True)
    e = jnp.exp(sc - m)
    return e / jnp.sum(e, axis=-1, keepdims=True)


def memattn_fwd(qm, kv, *, name):
    s = qm.shape[0]
    mlen = kv.shape[0]
    tm = _tile(s, 512, 16)

    def body(q_ref, kv_ref, o_ref):
        for h in range(MEM_HEADS):
            lo = h * MEM_HEAD_DIM
            q = q_ref[:, lo:lo + MEM_HEAD_DIM]
            k = kv_ref[:, lo:lo + MEM_HEAD_DIM]
            v = kv_ref[:, MEM_W + lo:MEM_W + lo + MEM_HEAD_DIM]
            p = _softmax_rows(_d16(q, k, NT) * MEM_SCALE)
            o_ref[:, lo:lo + MEM_HEAD_DIM] = _d16(p, v, NN).astype(bf16)

    return pl.pallas_call(body, name=name, grid=(s // tm,),
                          in_specs=[pl.BlockSpec((tm, MEM_W), lambda i: (i, 0)), pl.BlockSpec((mlen, 2 * MEM_W), lambda i: (0, 0))],
                          out_specs=pl.BlockSpec((tm, MEM_W), lambda i: (i, 0)),
                          out_shape=jax.ShapeDtypeStruct((s, MEM_W), bf16), compiler_params=_params("parallel"))(qm, kv)


def memattn_bwd(qm, kv, dout, *, name):
    s = qm.shape[0]
    mlen = kv.shape[0]
    tm = _tile(s, 512, 16)

    def body(q_ref, kv_ref, do_ref, dq_ref, dkv_ref):
        @pl.when(pl.program_id(0) == 0)
        def _():
            dkv_ref[...] = jnp.zeros_like(dkv_ref)

        for h in range(MEM_HEADS):
            lo = h * MEM_HEAD_DIM
            q = q_ref[:, lo:lo + MEM_HEAD_DIM]
            k = kv_ref[:, lo:lo + MEM_HEAD_DIM]
            v = kv_ref[:, MEM_W + lo:MEM_W + lo + MEM_HEAD_DIM]
            do = do_ref[:, lo:lo + MEM_HEAD_DIM]
            p = _softmax_rows(_d16(q, k, NT) * MEM_SCALE)
            dp = _d16(do, v, NT)
            ds = p * (dp - jnp.sum(p * dp, axis=-1, keepdims=True)) * MEM_SCALE
            dq_ref[:, lo:lo + MEM_HEAD_DIM] = _d16(ds, k, NN).astype(bf16)
            dkv_ref[:, lo:lo + MEM_HEAD_DIM] += _d16(ds, q, TN)
            dkv_ref[:, MEM_W + lo:MEM_W + lo + MEM_HEAD_DIM] += _d16(p, do, TN)

    row = pl.BlockSpec((tm, MEM_W), lambda i: (i, 0))
    full = pl.BlockSpec((mlen, 2 * MEM_W), lambda i: (0, 0))
    return pl.pallas_call(body, name=name, grid=(s // tm,), in_specs=[row, full, row], out_specs=[row, full],
                          out_shape=[jax.ShapeDtypeStruct((s, MEM_W), bf16), jax.ShapeDtypeStruct((mlen, 2 * MEM_W), f32)],
                          compiler_params=_params("arbitrary"))(qm, kv, dout)


SWA_SCALE = SWA_HEAD_DIM ** -0.5
SWA_GROUP = SWA_HEADS // SWA_KV_HEADS
SWA_IN_W = 1408
K_COL, V_COL, BA_COL = SWA_Q // 128, SWA_Q // 128 + 1, SWA_Q // 128 + 2


def _swa_mask(n):
    qi = lax.broadcasted_iota(jnp.int32, (BLOCK, 2 * BLOCK), 0)
    kj = lax.broadcasted_iota(jnp.int32, (BLOCK, 2 * BLOCK), 1)
    dist = qi + BLOCK - kj
    return (dist >= 0) & (dist < BLOCK) & ((n > 0) | (kj >= BLOCK))


def _swa_probs(q, k, bias, sink, mask):
    heads = range(len(q))
    sc = [jnp.where(mask, _d16(q[h], k[h], NT) * SWA_SCALE + bias[h], NEG_INF) for h in heads]
    m = [jnp.maximum(jnp.max(sc[h], axis=-1, keepdims=True), sink[h]) for h in heads]
    e = [jnp.exp(sc[h] - m[h]) for h in heads]
    es = [jnp.exp(sink[h] - m[h]) for h in heads]
    inv = [1.0 / (jnp.sum(e[h], axis=-1, keepdims=True) + es[h]) for h in heads]
    return e, es, inv


def _swa_heads(ref):
    return [ref[:, h * SWA_HEAD_DIM:(h + 1) * SWA_HEAD_DIM] for h in range(SWA_HEADS)]


def _swa_kv_of_heads(band):
    kv = [band[:, g * SWA_HEAD_DIM:(g + 1) * SWA_HEAD_DIM] for g in range(SWA_KV_HEADS)]
    return [kv[h // SWA_GROUP] for h in range(SWA_HEADS)]


def _swa_specs():
    q_spec = pl.BlockSpec((BLOCK, SWA_Q), lambda n: (n, 0))

    def band(col):
        return [pl.BlockSpec((BLOCK, SWA_KV), lambda n: (jnp.maximum(n - 1, 0), col)),
                pl.BlockSpec((BLOCK, SWA_KV), lambda n: (n, col))]

    bias_spec = pl.BlockSpec((SWA_HEADS, BLOCK, 2 * BLOCK), lambda n: (0, 0, 0))
    sink_spec = pl.BlockSpec((1, SWA_HEADS), lambda n: (0, 0))
    return [q_spec] + band(K_COL) + band(V_COL) + [bias_spec, sink_spec]


def swa_fwd(swa_in, bias, sinks, *, name, side=None):
    s = swa_in.shape[0]

    def body(q_ref, kp_ref, kc_ref, vp_ref, vc_ref, bias_ref, sink_ref, o_ref):
        mask = _swa_mask(pl.program_id(0))
        kb = jnp.concatenate([kp_ref[...], kc_ref[...]], axis=0)
        vb = jnp.concatenate([vp_ref[...], vc_ref[...]], axis=0)
        heads = range(SWA_HEADS)
        k, v = _swa_kv_of_heads(kb), _swa_kv_of_heads(vb)
        e, _, inv = _swa_probs(_swa_heads(q_ref), k, [bias_ref[h] for h in heads], [sink_ref[:, h:h + 1] for h in heads], mask)
        outs = [_d16(e[h] * inv[h], v[h], NN) for h in heads]
        for h in heads:
            o_ref[:, h * SWA_HEAD_DIM:(h + 1) * SWA_HEAD_DIM] = outs[h].astype(bf16)

    return _call(body, (swa_in, swa_in, swa_in, swa_in, swa_in, bias, sinks), name=name, grid=(s // BLOCK,),
                 in_specs=_swa_specs(), out_specs=pl.BlockSpec((BLOCK, SWA_Q), lambda n: (n, 0)),
                 out_shape=jax.ShapeDtypeStruct((s, SWA_Q), bf16), semantics=("parallel",), side=side)


def swa_bwd(swa_in, bias, sinks, dout, *, name, side=None):
    s = swa_in.shape[0]

    def body(q_ref, kp_ref, kc_ref, vp_ref, vc_ref, bias_ref, sink_ref, do_ref,
             dq_ref, dkc_ref, dkp_ref, dvc_ref, dvp_ref, dbias_ref, dsink_ref):
        n = pl.program_id(0)

        @pl.when(n == 0)
        def _():
            dbias_ref[...] = jnp.zeros_like(dbias_ref)
            dsink_ref[...] = jnp.zeros_like(dsink_ref)

        mask = _swa_mask(n)
        kb = jnp.concatenate([kp_ref[...], kc_ref[...]], axis=0)
        vb = jnp.concatenate([vp_ref[...], vc_ref[...]], axis=0)
        lane = lax.broadcasted_iota(jnp.int32, (1, 128), 1)
        hs = range(SWA_HEADS)
        q, do = _swa_heads(q_ref), _swa_heads(do_ref)
        k, v = _swa_kv_of_heads(kb), _swa_kv_of_heads(vb)
        e, es, inv = _swa_probs(q, k, [bias_ref[h] for h in hs], [sink_ref[:, h:h + 1] for h in hs], mask)
        p = [e[h] * inv[h] for h in hs]
        dp = [_d16(do[h], v[h], NT) for h in hs]
        delta = [jnp.sum(p[h] * dp[h], axis=-1, keepdims=True) for h in hs]
        ds = [p[h] * (dp[h] - delta[h]) for h in hs]
        dss = [ds[h] * SWA_SCALE for h in hs]
        dq = [_d16(dss[h], k[h], NN) for h in hs]
        dks = [_d16(dss[h], q[h], TN) for h in hs]
        dvs = [_d16(p[h], do[h], TN) for h in hs]
        dsink = jnp.zeros((1, 128), f32)
        for h in hs:
            dbias_ref[h] += ds[h]
            dq_ref[:, h * SWA_HEAD_DIM:(h + 1) * SWA_HEAD_DIM] = dq[h]
            dsink = dsink + jnp.where(lane == h, -jnp.sum(es[h] * inv[h] * delta[h], axis=0, keepdims=True), 0.0)
        for g in range(SWA_KV_HEADS):
            kl = g * SWA_HEAD_DIM
            group = range(g * SWA_GROUP, (g + 1) * SWA_GROUP)
            dk = functools.reduce(lambda a, b: a + b, [dks[h] for h in group])
            dv = functools.reduce(lambda a, b: a + b, [dvs[h] for h in group])
            dkp_ref[:, kl:kl + SWA_HEAD_DIM] = dk[:BLOCK]
            dkc_ref[:, kl:kl + SWA_HEAD_DIM] = dk[BLOCK:]
            dvp_ref[:, kl:kl + SWA_HEAD_DIM] = dv[:BLOCK]
            dvc_ref[:, kl:kl + SWA_HEAD_DIM] = dv[BLOCK:]
        dsink_ref[...] += dsink

    qs = pl.BlockSpec((BLOCK, SWA_Q), lambda n: (n, 0))
    ks = pl.BlockSpec((BLOCK, SWA_KV), lambda n: (n, 0))
    return _call(
        body, (swa_in, swa_in, swa_in, swa_in, swa_in, bias, sinks, dout), name=name, grid=(s // BLOCK,),
        in_specs=_swa_specs() + [qs],
        out_specs=[qs, ks, ks, ks, ks, pl.BlockSpec((SWA_HEADS, BLOCK, 2 * BLOCK), lambda n: (0, 0, 0)),
                   pl.BlockSpec((1, 128), lambda n: (0, 0))],
        out_shape=[jax.ShapeDtypeStruct((s, SWA_Q), f32)] + [jax.ShapeDtypeStruct((s, SWA_KV), f32)] * 4
        + [jax.ShapeDtypeStruct((SWA_HEADS, BLOCK, 2 * BLOCK), f32), jax.ShapeDtypeStruct((1, 128), f32)],
        semantics=("arbitrary",), side=side)


def swa_in_grad(dq, dkc, dkp, dvc, dvp, dba, *, name):
    s = dq.shape[0]
    nb = s // BLOCK

    def body(dq_ref, dkc_ref, dkp_ref, dvc_ref, dvp_ref, dba_ref, o_ref):
        has_next = pl.program_id(0) < nb - 1
        o_ref[:, :SWA_Q] = dq_ref[...].astype(bf16)
        o_ref[:, SWA_Q:SWA_Q + SWA_KV] = (dkc_ref[...] + jnp.where(has_next, dkp_ref[...], 0.0)).astype(bf16)
        o_ref[:, SWA_Q + SWA_KV:SWA_Q + 2 * SWA_KV] = (dvc_ref[...] + jnp.where(has_next, dvp_ref[...], 0.0)).astype(bf16)
        o_ref[:, SWA_Q + 2 * SWA_KV:] = dba_ref[...].astype(bf16)

    cur = pl.BlockSpec((BLOCK, SWA_KV), lambda n: (n, 0))
    nxt = pl.BlockSpec((BLOCK, SWA_KV), lambda n: (jnp.minimum(n + 1, nb - 1), 0))
    return pl.pallas_call(body, name=name, grid=(nb,),
                          in_specs=[pl.BlockSpec((BLOCK, SWA_Q), lambda n: (n, 0)), cur, nxt, cur, nxt, cur],
                          out_specs=pl.BlockSpec((BLOCK, SWA_IN_W), lambda n: (n, 0)),
                          out_shape=jax.ShapeDtypeStruct((s, SWA_IN_W), bf16),
                          compiler_params=_params("parallel"))(dq, dkc, dkp, dvc, dvp, dba)


def _bucket_onehot():
    qi = jnp.arange(BLOCK)[:, None]
    kj = jnp.arange(2 * BLOCK)[None, :]
    dist = jnp.maximum(qi + BLOCK - kj, 0)
    max_exact = REL_BUCKETS // 2
    dd = jnp.maximum(dist, 1).astype(f32)
    large = max_exact + (jnp.log(dd / max_exact) / math.log(REL_MAX_DIST / max_exact) * (REL_BUCKETS - max_exact)).astype(jnp.int32)
    bucket = jnp.where(dist < max_exact, dist, jnp.minimum(large, REL_BUCKETS - 1)).reshape(-1)
    return (bucket[None, :] == jnp.arange(REL_BUCKETS)[:, None]).astype(f32)


def _gbeta_fn(ba, alog_row, dt_row):
    col = lax.broadcasted_iota(jnp.int32, ba.shape, 1)
    v = ba + dt_row
    softplus = jnp.maximum(v, 0.0) + jnp.log(1.0 + jnp.exp(-jnp.abs(v)))
    g = -jnp.exp(alog_row) * softplus
    return jnp.where(col < GDN_HEADS, _sigmoid(ba), jnp.where(col < 2 * GDN_HEADS, g, 0.0))


def gbeta_fwd(swa_in, alog_row, dt_row, *, name):
    s = swa_in.shape[0]
    tm = _tile(s, 512, 8)

    def body(ba_ref, a_ref, d_ref, o_ref):
        o_ref[...] = _gbeta_fn(ba_ref[...], a_ref[...], d_ref[...])

    vec = pl.BlockSpec((1, 128), lambda i: (0, 0))
    return pl.pallas_call(body, name=name, grid=(s // tm,), in_specs=[pl.BlockSpec((tm, 128), lambda i: (i, BA_COL)), vec, vec],
                          out_specs=pl.BlockSpec((tm, 128), lambda i: (i, 0)), out_shape=jax.ShapeDtypeStruct((s, 128), f32),
                          compiler_params=_params("parallel"))(swa_in, alog_row, dt_row)


def gbeta_bwd(swa_in, alog_row, dt_row, dgbeta, *, name):
    s = swa_in.shape[0]
    tm = _tile(s, 512, 8)

    def body(ba_ref, a_ref, d_ref, dgb_ref, dba_ref, da_ref, dd_ref):
        @pl.when(pl.program_id(0) == 0)
        def _():
            da_ref[...] = jnp.zeros_like(da_ref)
            dd_ref[...] = jnp.zeros_like(dd_ref)

        _, pull = jax.vjp(_gbeta_fn, ba_ref[...], a_ref[...], d_ref[...])
        dba, da, dd = pull(dgb_ref[...])
        dba_ref[...] = dba
        da_ref[...] += da
        dd_ref[...] += dd

    vec = pl.BlockSpec((1, 128), lambda i: (0, 0))
    row = pl.BlockSpec((tm, 128), lambda i: (i, 0))
    return pl.pallas_call(body, name=name, grid=(s // tm,),
                          in_specs=[pl.BlockSpec((tm, 128), lambda i: (i, BA_COL)), vec, vec, row], out_specs=[row, vec, vec],
                          out_shape=[jax.ShapeDtypeStruct((s, 128), f32), jax.ShapeDtypeStruct((1, 128), f32),
                                     jax.ShapeDtypeStruct((1, 128), f32)],
                          compiler_params=_params("arbitrary"))(swa_in, alog_row, dt_row, dgbeta)


QKV_W = 3 * GDN_W


def gdn_pre_fwd(gdn_in, convw, *, name):
    s = gdn_in.shape[0]
    tm = _tile(s, 256, 16)
    hb = tm // HALO

    def body(x_ref, xp_ref, w_ref, q_ref, k_ref, v_ref, pre_ref):
        prev = jnp.where(pl.program_id(0) == 0, 0.0, xp_ref[...])
        pre = _causal_conv(_conv_taps(jnp.concatenate([prev, x_ref[...]], axis=0), GDN_CONV), w_ref)
        pre_ref[...] = pre
        act = pre * _sigmoid(pre)
        for h in range(GDN_HEADS):
            lo = h * GDN_HEAD_DIM
            for off, o_ref in ((0, q_ref), (GDN_W, k_ref)):
                seg = act[:, off + lo:off + lo + GDN_HEAD_DIM]
                o_ref[:, lo:lo + GDN_HEAD_DIM] = seg * lax.rsqrt(jnp.sum(seg * seg, axis=-1, keepdims=True) + 1e-6)
        v_ref[...] = act[:, 2 * GDN_W:]

    out = pl.BlockSpec((tm, GDN_W), lambda i: (i, 0))
    return pl.pallas_call(body, name=name, grid=(s // tm,),
                          in_specs=[pl.BlockSpec((tm, QKV_W), lambda i: (i, 0)),
                                    pl.BlockSpec((HALO, QKV_W), lambda i: (jnp.maximum(i * hb - 1, 0), 0)),
                                    pl.BlockSpec((8, QKV_W), lambda i: (0, 0))],
                          out_specs=[out, out, out, pl.BlockSpec((tm, QKV_W), lambda i: (i, 0))],
                          out_shape=[jax.ShapeDtypeStruct((s, GDN_W), f32)] * 3 + [jax.ShapeDtypeStruct((s, QKV_W), f32)],
                          compiler_params=_params("parallel"))(gdn_in, gdn_in, convw)


def gdn_pre_bwd(gdn_in, conv_out, convw, dqn, dkn, dv, dgz, *, name):
    s = gdn_in.shape[0]
    tm = _tile(s, 256, 16)
    hb = tm // HALO
    nt = s // tm
    last_hb = s // HALO - 1

    def body(x_ref, pre_ref, pren_ref, w_ref, dq_ref, dqx_ref, dk_ref, dkx_ref, dv_ref, dvx_ref, dz_ref, o_ref, dw_ref):
        i = pl.program_id(0)
        last = i == nt - 1

        @pl.when(i == 0)
        def _():
            dw_ref[...] = jnp.zeros_like(dw_ref)

        pre = jnp.concatenate([pre_ref[...], pren_ref[...]], axis=0)
        act, dact_dpre = _silu_and_grad(pre)

        def with_future(t_ref, n_ref):
            return jnp.concatenate([t_ref[...], jnp.where(last, 0.0, n_ref[...])], axis=0)

        dqe, dke, dve = with_future(dq_ref, dqx_ref), with_future(dk_ref, dkx_ref), with_future(dv_ref, dvx_ref)
        parts = []
        for off, dn in ((0, dqe), (GDN_W, dke)):
            for h in range(GDN_HEADS):
                lo = h * GDN_HEAD_DIM
                seg = act[:, off + lo:off + lo + GDN_HEAD_DIM]
                r = lax.rsqrt(jnp.sum(seg * seg, axis=-1, keepdims=True) + 1e-6)
                nrm = seg * r
                dseg = dn[:, lo:lo + GDN_HEAD_DIM]
                parts.append(r * (dseg - nrm * jnp.sum(dseg * nrm, axis=-1, keepdims=True)))
        dpre = jnp.concatenate(parts + [dve], axis=1) * dact_dpre
        dx, dw = _conv_grads(dpre, x_ref[...], w_ref, GDN_CONV, tm)
        o_ref[:, :QKV_W] = dx.astype(bf16)
        o_ref[:, QKV_W:] = dz_ref[...].astype(bf16)
        dw_ref[...] += _rows_to_block(dw, 8, QKV_W)

    row = pl.BlockSpec((tm, GDN_W), lambda i: (i, 0))
    fut = pl.BlockSpec((HALO, GDN_W), lambda i: (jnp.minimum((i + 1) * hb, last_hb), 0))
    wide = pl.BlockSpec((tm, QKV_W), lambda i: (i, 0))
    return pl.pallas_call(
        body, name=name, grid=(nt,),
        in_specs=[wide, wide, pl.BlockSpec((HALO, QKV_W), lambda i: (jnp.minimum((i + 1) * hb, last_hb), 0)),
                  pl.BlockSpec((8, QKV_W), lambda i: (0, 0)), row, fut, row, fut, row, fut, row],
        out_specs=[pl.BlockSpec((tm, 4 * GDN_W), lambda i: (i, 0)), pl.BlockSpec((8, QKV_W), lambda i: (0, 0))],
        out_shape=[jax.ShapeDtypeStruct((s, 4 * GDN_W), bf16), jax.ShapeDtypeStruct((8, QKV_W), f32)],
        compiler_params=_params("arbitrary"),
    )(gdn_in, conv_out, conv_out, convw, dqn, dqn, dkn, dkn, dv, dv, dgz)


def _gdn_post_head(o, z, nw):
    return o * lax.rsqrt(jnp.mean(o * o, axis=-1, keepdims=True) + 1e-6) * nw * (z * _sigmoid(z))


def gdn_post_fwd(o, gdn_in, nw, *, name):
    s = o.shape[0]
    tm = _tile(s, 512, 16)

    def body(o_ref, z_ref, nw_ref, y_ref):
        for h in range(GDN_HEADS):
            sl = slice(h * GDN_HEAD_DIM, (h + 1) * GDN_HEAD_DIM)
            y_ref[:, sl] = _gdn_post_head(o_ref[:, sl], z_ref[:, sl], nw_ref[...]).astype(bf16)

    row = pl.BlockSpec((tm, GDN_W), lambda i: (i, 0))
    return pl.pallas_call(body, name=name, grid=(s // tm,),
                          in_specs=[row, pl.BlockSpec((tm, GDN_W), lambda i: (i, 3)), pl.BlockSpec((1, 128), lambda i: (0, 0))],
                          out_specs=row, out_shape=jax.ShapeDtypeStruct((s, GDN_W), bf16),
                          compiler_params=_params("parallel"))(o, gdn_in, nw)


def gdn_post_bwd(o, gdn_in, nw, dy, *, name):
    s = o.shape[0]
    tm = _tile(s, 512, 16)

    def body(o_ref, z_ref, nw_ref, dy_ref, do_ref, dz_ref, dnw_ref):
        @pl.when(pl.program_id(0) == 0)
        def _():
            dnw_ref[...] = jnp.zeros_like(dnw_ref)

        dnw = jnp.zeros((1, 128), f32)
        for h in range(GDN_HEADS):
            sl = slice(h * GDN_HEAD_DIM, (h + 1) * GDN_HEAD_DIM)
            _, pull = jax.vjp(_gdn_post_head, o_ref[:, sl], z_ref[:, sl], nw_ref[...])
            do, dz, dn = pull(dy_ref[:, sl])
            do_ref[:, sl] = do
            dz_ref[:, sl] = dz
            dnw = dnw + dn
        dnw_ref[...] += dnw

    row = pl.BlockSpec((tm, GDN_W), lambda i: (i, 0))
    vec = pl.BlockSpec((1, 128), lambda i: (0, 0))
    return pl.pallas_call(body, name=name, grid=(s // tm,),
                          in_specs=[row, pl.BlockSpec((tm, GDN_W), lambda i: (i, 3)), vec, row], out_specs=[row, row, vec],
                          out_shape=[jax.ShapeDtypeStruct((s, GDN_W), f32), jax.ShapeDtypeStruct((s, GDN_W), f32),
                                     jax.ShapeDtypeStruct((1, 128), f32)],
                          compiler_params=_params("arbitrary"))(o, gdn_in, nw, dy)


def _dot_high(a, b, dims=NN):
    return lax.dot_general(a, b, (dims, ((), ())), precision=lax.Precision.HIGH, preferred_element_type=f32)


@jax.custom_vjp
def _unit_lower_inverses(a):
    c = a[0].shape[0]
    n = range(len(a))
    eye = (lax.broadcasted_iota(jnp.int32, (c, c), 0) == lax.broadcasted_iota(jnp.int32, (c, c), 1)).astype(f32)
    inv = [eye - a[i] for i in n]
    pw = [_dot_high(a[i], a[i]) for i in n]
    width = 2
    while width < c:
        inv = [inv[i] + _dot_high(inv[i], pw[i]) for i in n]
        width *= 2
        if width < c:
            pw = [_dot_high(pw[i], pw[i]) for i in n]
    return inv


def _unit_lower_inverses_fwd(a):
    inv = _unit_lower_inverses(a)
    return inv, inv


def _unit_lower_inverses_bwd(inv, g):
    return ([-_dot_high(_dot_high(x, gx, TN), x, NT) for x, gx in zip(inv, g)],)


_unit_lower_inverses.defvjp(_unit_lower_inverses_fwd, _unit_lower_inverses_bwd)


@jax.custom_vjp
def _known_inverses(a, inv):
    return inv


_known_inverses.defvjp(lambda a, inv: (inv, inv),
                       lambda inv, g: (_unit_lower_inverses_bwd(inv, g)[0], [jnp.zeros_like(x) for x in inv]))


def _gdn_chunks(q, k, v, gb, state, kept_inverses=None):
    c = GDN_CHUNK
    heads = range(len(q))
    r = lax.broadcasted_iota(jnp.int32, (c, c), 0)
    cc = lax.broadcasted_iota(jnp.int32, (c, c), 1)
    tril, strict = r >= cc, r > cc
    eye = (r == cc).astype(f32)

    def dhi(a, b):
        return jnp.dot(a, b, precision=lax.Precision.HIGH, preferred_element_type=f32)

    beta = [gb[:, h:h + 1] for h in heads]
    cum_cols = dhi(tril.astype(f32), gb)
    cum_rows = dhi(gb.T, (r <= cc).astype(f32))
    gi = [jnp.broadcast_to(cum_cols[:, GDN_HEADS + h:GDN_HEADS + h + 1], (c, c)) for h in heads]
    gj = [jnp.broadcast_to(cum_rows[GDN_HEADS + h:GDN_HEADS + h + 1, :], (c, c)) for h in heads]
    decay = [jnp.where(tril, jnp.exp(jnp.where(tril, gi[h] - gj[h], 0.0)), 0.0) for h in heads]
    kb = [k[h] * beta[h] for h in heads]
    vb = [v[h] * beta[h] for h in heads]
    a = [jnp.where(strict, _d16(kb[h], k[h], NT) * decay[h], 0.0) for h in heads]
    tinv = _unit_lower_inverses(a) if kept_inverses is None else _known_inverses(a, kept_inverses)
    gc = [gi[h][:, 0:1] for h in heads]
    egc = [jnp.exp(gc[h]) for h in heads]
    u = [dhi(tinv[h], vb[h]) for h in heads]
    w = [dhi(tinv[h], kb[h] * egc[h]) for h in heads]
    qs = [q[h] * (GDN_HEAD_DIM ** -0.5) for h in heads]
    attn = [jnp.where(tril, _d16(qs[h], k[h], NT) * decay[h], 0.0) for h in heads]
    g_last = [gi[h][c - 1:c, 0:1] for h in heads]
    v_new = [u[h] - _d16(w[h], state[h], NN) for h in heads]
    out = [_d16(qs[h] * egc[h], state[h], NN) + _d16(attn[h], v_new[h], NN) for h in heads]
    new_state = [state[h] * jnp.exp(g_last[h]) + _d16(k[h] * jnp.exp(g_last[h] - gc[h]), v_new[h], TN) for h in heads]
    return out, new_state, tinv


def _head_cols(ref):
    return [ref[:, h * GDN_HEAD_DIM:(h + 1) * GDN_HEAD_DIM] for h in range(GDN_HEADS)]


def gdn_scan_fwd(qn, kn, v, gbeta, *, name, side=None):
    s = qn.shape[0]
    nc = s // GDN_CHUNK

    def body(q_ref, k_ref, v_ref, gb_ref, o_ref, st_ref, inv_ref, state_ref):
        @pl.when(pl.program_id(0) == 0)
        def _():
            state_ref[...] = jnp.zeros_like(state_ref)

        states = [state_ref[h] for h in range(GDN_HEADS)]
        outs, new, inverses = _gdn_chunks(_head_cols(q_ref), _head_cols(k_ref), _head_cols(v_ref), gb_ref[...], states)
        for h in range(GDN_HEADS):
            st_ref[0, h] = states[h]
            inv_ref[0, h] = inverses[h]
            o_ref[:, h * GDN_HEAD_DIM:(h + 1) * GDN_HEAD_DIM] = outs[h]
            state_ref[h] = new[h]

    row = pl.BlockSpec((GDN_CHUNK, GDN_W), lambda n: (n, 0))
    return _call(
        body, (qn, kn, v, gbeta), name=name, grid=(nc,),
        in_specs=[row, row, row, pl.BlockSpec((GDN_CHUNK, 128), lambda n: (n, 0))],
        out_specs=[row, pl.BlockSpec((1, GDN_HEADS, GDN_HEAD_DIM, GDN_HEAD_DIM), lambda n: (n, 0, 0, 0)),
                   pl.BlockSpec((1, GDN_HEADS, GDN_CHUNK, GDN_CHUNK), lambda n: (n, 0, 0, 0))],
        out_shape=[jax.ShapeDtypeStruct((s, GDN_W), f32),
                   jax.ShapeDtypeStruct((nc, GDN_HEADS, GDN_HEAD_DIM, GDN_HEAD_DIM), f32),
                   jax.ShapeDtypeStruct((nc, GDN_HEADS, GDN_CHUNK, GDN_CHUNK), f32)],
        scratch_shapes=[pltpu.VMEM((GDN_HEADS, GDN_HEAD_DIM, GDN_HEAD_DIM), f32)], semantics=("arbitrary",), side=side)


def gdn_scan_bwd(qn, kn, v, gbeta, states, inverses, dout, *, name, side=None):
    s = qn.shape[0]
    nc = s // GDN_CHUNK

    def body(q_ref, k_ref, v_ref, gb_ref, st_ref, inv_ref, do_ref, dq_ref, dk_ref, dv_ref, dgb_ref, dstate_ref):
        @pl.when(pl.program_id(0) == 0)
        def _():
            dstate_ref[...] = jnp.zeros_like(dstate_ref)

        kept = [inv_ref[0, h] for h in range(GDN_HEADS)]
        _, pull = jax.vjp(lambda *args: _gdn_chunks(*args, kept_inverses=kept)[:2],
                          _head_cols(q_ref), _head_cols(k_ref), _head_cols(v_ref), gb_ref[...],
                          [st_ref[0, h] for h in range(GDN_HEADS)])
        dq, dk, dv, dgb, dst = pull((_head_cols(do_ref), [dstate_ref[h] for h in range(GDN_HEADS)]))
        for h in range(GDN_HEADS):
            sl = slice(h * GDN_HEAD_DIM, (h + 1) * GDN_HEAD_DIM)
            dq_ref[:, sl] = dq[h]
            dk_ref[:, sl] = dk[h]
            dv_ref[:, sl] = dv[h]
            dstate_ref[h] = dst[h]
        dgb_ref[...] = dgb

    row = pl.BlockSpec((GDN_CHUNK, GDN_W), lambda n: (nc - 1 - n, 0))
    gb = pl.BlockSpec((GDN_CHUNK, 128), lambda n: (nc - 1 - n, 0))
    return _call(
        body, (qn, kn, v, gbeta, states, inverses, dout), name=name, grid=(nc,),
        in_specs=[row, row, row, gb, pl.BlockSpec((1, GDN_HEADS, GDN_HEAD_DIM, GDN_HEAD_DIM), lambda n: (nc - 1 - n, 0, 0, 0)),
                  pl.BlockSpec((1, GDN_HEADS, GDN_CHUNK, GDN_CHUNK), lambda n: (nc - 1 - n, 0, 0, 0)), row],
        out_specs=[row, row, row, gb],
        out_shape=[jax.ShapeDtypeStruct((s, GDN_W), f32)] * 3 + [jax.ShapeDtypeStruct((s, 128), f32)],
        scratch_shapes=[pltpu.VMEM((GDN_HEADS, GDN_HEAD_DIM, GDN_HEAD_DIM), f32)], semantics=("arbitrary",), side=side)


def _adamw_update(w, g, m, v):
    nm = ADAM_B1 * m + (1.0 - ADAM_B1) * g
    nv = ADAM_B2 * v + (1.0 - ADAM_B2) * (g * g)
    m_hat = nm / (1.0 - ADAM_B1 ** ADAM_STEP)
    v_hat = nv / (1.0 - ADAM_B2 ** ADAM_STEP)
    return -ADAM_LR * (m_hat / (jnp.sqrt(v_hat) + ADAM_EPS) + ADAM_WD * w), nm, nv


def adamw(w, g, m, v, *, name):
    r, c = w.shape
    tr = _tile(r, 256, 8)

    def body(w_ref, g_ref, m_ref, v_ref, d_ref, nm_ref, nv_ref):
        d_ref[...], nm_ref[...], nv_ref[...] = _adamw_update(w_ref[...], g_ref[...], m_ref[...], v_ref[...])

    spec = pl.BlockSpec((tr, c), lambda i: (i, 0))
    return pl.pallas_call(body, name=name, grid=(r // tr,), in_specs=[spec] * 4, out_specs=[spec] * 3,
                          out_shape=[jax.ShapeDtypeStruct((r, c), f32)] * 3, compiler_params=_params("parallel"))(w, g, m, v)


def adamw_of_partial_sums(w, parts, m, v, *, name):
    r, c = w.shape
    cp = parts.shape[2]
    tr = _tile(r, 256, 16)

    def body(w_ref, p_ref, m_ref, v_ref, g_ref, d_ref, nm_ref, nv_ref):
        part = [p_ref[k, :, :c].astype(f32) for k in range(4)]
        g = ((part[3] + part[0]) + part[1]) + part[2]
        g_ref[...] = g
        d_ref[...], nm_ref[...], nv_ref[...] = _adamw_update(w_ref[...], g, m_ref[...], v_ref[...])

    spec = pl.BlockSpec((tr, c), lambda i: (i, 0))
    return pl.pallas_call(body, name=name, grid=(r // tr,),
                          in_specs=[spec, pl.BlockSpec((4, tr, cp), lambda i: (0, i, 0)), spec, spec], out_specs=[spec] * 4,
                          out_shape=[jax.ShapeDtypeStruct((r, c), f32)] * 4,
                          compiler_params=_params("parallel"))(w, parts, m, v)


def _pos():
    return lax.axis_index("x"), lax.axis_index("y"), lax.axis_index("c")


ANY = pl.BlockSpec(memory_space=pl.ANY)


class Side(NamedTuple):
    ins: list
    outs: list
    aliases: dict
    sems: list
    start: Callable
    wait: Callable


def join_sides(*sides):
    def spans(key):
        out, off = [], 0
        for sd in sides:
            out.append(slice(off, off + len(getattr(sd, key))))
            off += len(getattr(sd, key))
        return out

    i_sp, o_sp, s_sp = spans("ins"), spans("outs"), spans("sems")
    aliases = {i_sp[n].start + i: o_sp[n].start + o for n, sd in enumerate(sides) for i, o in sd.aliases.items()}

    def each(what):
        def run(ins, outs, sems):
            for n, sd in enumerate(sides):
                getattr(sd, what)(ins[i_sp[n]], outs[o_sp[n]], sems[s_sp[n]])
        return run

    return Side([a for sd in sides for a in sd.ins], [o for sd in sides for o in sd.outs], aliases,
                [s for sd in sides for s in sd.sems], each("start"), each("wait"))


def _side_body(body, side, n_in, n_out, n_scratch, grid):
    ns_in, ns_out = len(side.ins), len(side.outs)

    def wrapped(*refs):
        cut = [n_in, ns_in, n_out, ns_out, n_scratch]
        parts, off = [], 0
        for c in cut:
            parts.append(refs[off:off + c])
            off += c
        ins, s_ins, outs, s_outs, scratch = parts
        sems = refs[off:]
        if grid:
            ids = [pl.program_id(d) for d in range(len(grid))]
            first = functools.reduce(jnp.logical_and, [i == 0 for i in ids])
            last = functools.reduce(jnp.logical_and, [i == g - 1 for i, g in zip(ids, grid)])
            pl.when(first)(lambda: side.start(s_ins, s_outs, sems))
            body(*ins, *outs, *scratch)
            pl.when(last)(lambda: side.wait(s_ins, s_outs, sems))
        else:
            side.start(s_ins, s_outs, sems)
            side.wait(s_ins, s_outs, sems)

    return wrapped


def _call(body, args, *, name, grid, in_specs, out_specs, out_shape, semantics, scratch_shapes=(), side=None):
    if side is None:
        return pl.pallas_call(body, name=name, grid=grid, in_specs=in_specs, out_specs=out_specs, out_shape=out_shape,
                              scratch_shapes=list(scratch_shapes), compiler_params=_params(*semantics))(*args)
    single = not isinstance(out_shape, (list, tuple))
    shapes, specs = ([out_shape], [out_specs]) if single else (list(out_shape), list(out_specs))
    n_in, n_out = len(in_specs), len(shapes)
    res = pl.pallas_call(
        _side_body(body, side, n_in, n_out, len(scratch_shapes), grid), name=name, grid=grid,
        in_specs=list(in_specs) + [ANY] * len(side.ins), out_specs=specs + [ANY] * len(side.outs),
        out_shape=shapes + list(side.outs), scratch_shapes=list(scratch_shapes) + list(side.sems),
        input_output_aliases={n_in + i: n_out + o for i, o in side.aliases.items()},
        compiler_params=_params(*(["arbitrary"] * len(grid))),
    )(*args, *side.ins)
    return (res[0] if single else res[:n_out]), list(res[n_out:])


def run_side(side, *, name):
    return pl.pallas_call(_side_body(None, side, 0, 0, 0, ()), name=name, in_specs=[ANY] * len(side.ins),
                          out_specs=[ANY] * len(side.outs), out_shape=list(side.outs), scratch_shapes=list(side.sems),
                          input_output_aliases=dict(side.aliases))(*side.ins)


def _remote(src, dst, send, recv, k, to):
    return pltpu.make_async_remote_copy(src_ref=src, dst_ref=dst, send_sem=send.at[k], recv_sem=recv.at[k], device_id=to,
                                        device_id_type=MESH)


def gather_first(shards):
    na = len(shards)

    def copies(x_refs, out_refs, sems):
        send, recv, local = sems
        x, y, cc = _pos()
        me = 4 * x + 2 * y + cc
        peers = [(x, y, 1 - cc), (1 - x, y, cc), (x, 1 - y, cc), (1 - x, 1 - y, cc)]
        mine = [pltpu.make_async_copy(x_refs[a], out_refs[a].at[me], local.at[a]) for a in range(na)]
        sent = [_remote(x_refs[a], out_refs[a].at[me], send, recv, 4 * a + k, p) for a in range(na) for k, p in enumerate(peers)]
        landed = [_remote(x_refs[a], out_refs[a].at[4 * p[0] + 2 * p[1] + p[2]], send, recv, 4 * a + k, p)
                  for a in range(na) for k, p in enumerate(peers)]
        return mine, sent, landed

    def start(x_refs, out_refs, sems):
        mine, sent, _ = copies(x_refs, out_refs, sems)
        for cp in mine + sent:
            cp.start()

    def wait(x_refs, out_refs, sems):
        mine, sent, landed = copies(x_refs, out_refs, sems)
        for cp in sent:
            cp.wait_send()
        for cp in landed:
            cp.wait_recv()
        for cp in mine:
            cp.wait()

    return Side(list(shards), [jax.ShapeDtypeStruct((N_DEV,) + s.shape, s.dtype) for s in shards], {},
                [pltpu.SemaphoreType.DMA((4 * na,)), pltpu.SemaphoreType.DMA((4 * na,)), pltpu.SemaphoreType.DMA((na,))],
                start, wait)


def gather_second(slots):
    na = len(slots)

    def copies(out_refs, sems):
        send, recv = sems
        x, y, cc = _pos()
        chips = [(1 - x, y), (x, 1 - y), (1 - x, 1 - y)]
        sent, landed = [], []
        for a in range(na):
            for j, (px, py) in enumerate(chips):
                row = out_refs[a].at[4 * px + 2 * py + cc]
                sent.append(_remote(row, row, send, recv, 3 * a + j, (x, y, 1 - cc)))
                landed.append(_remote(row, out_refs[a].at[4 * px + 2 * py + 1 - cc], send, recv, 3 * a + j, (x, y, 1 - cc)))
        return sent, landed

    def start(_, out_refs, sems):
        for cp in copies(out_refs, sems)[0]:
            cp.start()

    def wait(_, out_refs, sems):
        sent, landed = copies(out_refs, sems)
        for cp in sent:
            cp.wait_send()
        for cp in landed:
            cp.wait_recv()

    return Side(list(slots), [jax.ShapeDtypeStruct(s.shape, s.dtype) for s in slots], {a: a for a in range(na)},
                [pltpu.SemaphoreType.DMA((3 * na,)), pltpu.SemaphoreType.DMA((3 * na,))], start, wait)


def _rows_of(ref, lead, rows):
    if rows is None:
        return ref if lead is None else ref.at[lead]
    cut = pl.ds(rows[0], rows[1])
    return ref.at[:, cut] if lead is None else ref.at[lead, cut]


def _side_into(arrays, out_shapes, into):
    na = len(arrays)
    if into is None:
        return list(arrays), out_shapes, {}
    return list(arrays) + list(into), out_shapes, {na + a: a for a in range(na)}


def grad_to_sibling(chunks, rows=None, into=None):
    na = len(chunks)

    def start(g_refs, out_refs, sems):
        send, recv = sems
        x, y, cc = _pos()
        for a in range(na):
            for q in range(4):
                _remote(_rows_of(g_refs[a], 2 * q + 1 - cc, rows), _rows_of(out_refs[a], q, rows), send, recv, a,
                        (x, y, 1 - cc)).start()

    def wait(g_refs, out_refs, sems):
        send, recv = sems
        x, y, cc = _pos()
        for a in range(na):
            whole = _rows_of(out_refs[a], None, rows)
            _remote(whole, whole, send, recv, a, (x, y, 1 - cc)).wait()

    ins, outs, aliases = _side_into(chunks, [jax.ShapeDtypeStruct((4,) + g.shape[1:], g.dtype) for g in chunks], into)
    return Side(ins, outs, aliases, [pltpu.SemaphoreType.DMA((na,)), pltpu.SemaphoreType.DMA((na,))], start, wait)


def grad_to_chips(parts, rows=None, into=None):
    na = len(parts)

    def copies(p_refs, out_refs, sems):
        send, recv, local = sems
        x, y, cc = _pos()
        chips = [(1 - x, y), (x, 1 - y), (1 - x, 1 - y)]
        mine = [pltpu.make_async_copy(_rows_of(p_refs[a], 2 * x + y, rows), _rows_of(out_refs[a], 3, rows), local.at[a])
                for a in range(na)]
        sent = [_remote(_rows_of(p_refs[a], 2 * px + py, rows), _rows_of(out_refs[a], k, rows), send, recv, 3 * a + k,
                        (px, py, cc)) for a in range(na) for k, (px, py) in enumerate(chips)]
        return mine, sent

    def start(p_refs, out_refs, sems):
        mine, sent = copies(p_refs, out_refs, sems)
        for cp in mine + sent:
            cp.start()

    def wait(p_refs, out_refs, sems):
        mine, sent = copies(p_refs, out_refs, sems)
        for cp in sent:
            cp.wait()
        for cp in mine:
            cp.wait()

    ins, outs, aliases = _side_into(parts, [jax.ShapeDtypeStruct(p.shape, p.dtype) for p in parts], into)
    return Side(ins, outs, aliases,
                [pltpu.SemaphoreType.DMA((3 * na,)), pltpu.SemaphoreType.DMA((3 * na,)), pltpu.SemaphoreType.DMA((na,))],
                start, wait)


def add_sibling(chunks, recv, *, name):
    _, r, c = chunks.shape
    tr = r if r <= 1024 else _tile(r, 512, 16)
    core = lax.axis_index("c").astype(jnp.int32).reshape(1)

    def body(core_ref, a_ref, b_ref, o_ref):
        o_ref[...] = (a_ref[...] + b_ref[...]).astype(bf16)

    return pl.pallas_call(
        body, name=name,
        grid_spec=pltpu.PrefetchScalarGridSpec(
            num_scalar_prefetch=1, grid=(4, r // tr),
            in_specs=[pl.BlockSpec((1, tr, c), lambda q, i, core_ref: (2 * q + core_ref[0], i, 0)),
                      pl.BlockSpec((1, tr, c), lambda q, i, core_ref: (q, i, 0))],
            out_specs=pl.BlockSpec((1, tr, c), lambda q, i, core_ref: (q, i, 0))),
        out_shape=jax.ShapeDtypeStruct((4, r, c), bf16), compiler_params=_params("parallel", "parallel"),
    )(core, chunks, recv)


def all_reduce_small(vec, *, name):
    r, c = vec.shape

    def body(v_ref, out_ref, buf_ref, send_sems, recv_sems):
        x, y, cc = _pos()
        my_id = 4 * x + 2 * y + cc
        buf_ref[my_id] = v_ref[...]
        flips = [(fx, fy, fc) for fx in (0, 1) for fy in (0, 1) for fc in (0, 1)][1:]
        cps = []
        for k, (fx, fy, fc) in enumerate(flips):
            peer = ((1 - x) if fx else x, (1 - y) if fy else y, (1 - cc) if fc else cc)
            cps.append(pltpu.make_async_remote_copy(src_ref=v_ref, dst_ref=buf_ref.at[my_id], send_sem=send_sems.at[k],
                                                    recv_sem=recv_sems.at[k], device_id=peer, device_id_type=MESH))
        for cp in cps:
            cp.start()
        for cp in cps:
            cp.wait()
        acc = buf_ref[0]
        for d in range(1, N_DEV):
            acc = acc + buf_ref[d]
        out_ref[...] = acc

    vm = pl.BlockSpec(memory_space=pltpu.VMEM)
    return pl.pallas_call(body, name=name, in_specs=[vm], out_specs=vm, out_shape=jax.ShapeDtypeStruct((r, c), f32),
                          scratch_shapes=[pltpu.VMEM((N_DEV, r, c), f32), pltpu.SemaphoreType.DMA((7,)),
                                          pltpu.SemaphoreType.DMA((7,))])(vec)


def _pack(parts, rows, dtype):
    flat = jnp.concatenate([p.reshape(-1).astype(dtype) for p in parts])
    return jnp.pad(flat, (0, rows * PACK_COLS - flat.shape[0])).reshape(rows, PACK_COLS)


def _unpack(flat, shapes):
    out, off = [], 0
    for shp in shapes:
        n = shp[0] * shp[1]
        out.append(flat[..., off:off + n].reshape(flat.shape[:-1] + tuple(shp)))
        off += n
    return out


def _from_column_shards(g):
    _, r, c = g.shape
    return jnp.transpose(g, (1, 0, 2)).reshape(r, N_DEV * c)


def _column_shards(full):
    r, c8 = full.shape
    return jnp.transpose(full.reshape(r, N_DEV, c8 // N_DEV), (1, 0, 2))


W_IN_SHARD = IN_DIM // N_DEV
W_IN_PAD = 1280
W_IN_PARTS = (("swa", 0, 0, 1280), ("swa", 1280, 5376, 5392), ("gdn", 0, 1280, 5376), ("gates", 0, 5392, IN_DIM))
W_IN_WIDTHS = {"swa": SWA_IN_W, "gdn": 4 * GDN_W, "gates": 2 * D_MODEL}


def _w_in_segments():
    segs = []
    for part, p0, g0, g1 in W_IN_PARTS:
        for j in range(N_DEV):
            lo, hi = max(g0, W_IN_SHARD * j), min(g1, W_IN_SHARD * (j + 1))
            if lo < hi:
                segs.append((part, p0 + lo - g0, j, lo - W_IN_SHARD * j, hi - lo))
    return segs


def split_w_in(shards, *, name):
    dt = shards.dtype
    tm = 256

    def body(w_ref, swa_ref, gdn_ref, gates_ref):
        out = {"swa": swa_ref, "gdn": gdn_ref, "gates": gates_ref}
        swa_ref[:, SWA_Q + 2 * SWA_KV + 2 * GDN_HEADS:] = jnp.zeros((tm, SWA_IN_W - SWA_Q - 2 * SWA_KV - 2 * GDN_HEADS), dt)
        for part, p0, j, l0, n in _w_in_segments():
            out[part][:, p0:p0 + n] = w_ref[j, :, l0:l0 + n]

    return pl.pallas_call(body, name=name, grid=(D_MODEL // tm,),
                          in_specs=[pl.BlockSpec((N_DEV, tm, W_IN_PAD), lambda i: (0, i, 0))],
                          out_specs=[pl.BlockSpec((tm, W_IN_WIDTHS[p]), lambda i: (i, 0)) for p in ("swa", "gdn", "gates")],
                          out_shape=[jax.ShapeDtypeStruct((D_MODEL, W_IN_WIDTHS[p]), dt) for p in ("swa", "gdn", "gates")],
                          compiler_params=_params("parallel"))(shards)


def merge_w_in_grad(d_swa, d_gdn, d_gates, *, name):
    tm = 256

    def body(swa_ref, gdn_ref, gates_ref, w_ref):
        src = {"swa": swa_ref, "gdn": gdn_ref, "gates": gates_ref}
        w_ref[:, :, W_IN_SHARD:] = jnp.zeros((N_DEV, tm, W_IN_PAD - W_IN_SHARD), f32)
        for part, p0, j, l0, n in _w_in_segments():
            w_ref[j, :, l0:l0 + n] = src[part][:, p0:p0 + n]

    return pl.pallas_call(body, name=name, grid=(D_MODEL // tm,),
                          in_specs=[pl.BlockSpec((tm, W_IN_WIDTHS[p]), lambda i: (i, 0)) for p in ("swa", "gdn", "gates")],
                          out_specs=pl.BlockSpec((N_DEV, tm, W_IN_PAD), lambda i: (0, i, 0)),
                          out_shape=jax.ShapeDtypeStruct((N_DEV, D_MODEL, W_IN_PAD), f32),
                          compiler_params=_params("parallel"))(d_swa, d_gdn, d_gates)


def kernel(x, mem, w_in, rel_bias, swa_sinks, gdn_conv_w, gdn_a_log, gdn_dt_bias, gdn_norm_w, w_br_swa, w_br_gdn, w_mix_o, ln1_g, ln1_b, w_mem_q, w_mem_kv, w_mem_o, ln2_g, ln2_b, w_up, ffn_conv_w, ffn_conv_b, w_down, ln3_g, ln3_b, loss_target, m_w_in, m_rel_bias, m_swa_sinks, m_gdn_conv_w, m_gdn_a_log, m_gdn_dt_bias, m_gdn_norm_w, m_w_br_swa, m_w_br_gdn, m_w_mix_o, m_ln1_g, m_ln1_b, m_w_mem_q, m_w_mem_kv, m_w_mem_o, m_ln2_g, m_ln2_b, m_w_up, m_ffn_conv_w, m_ffn_conv_b, m_w_down, m_ln3_g, m_ln3_b, v_w_in, v_rel_bias, v_swa_sinks, v_gdn_conv_w, v_gdn_a_log, v_gdn_dt_bias, v_gdn_norm_w, v_w_br_swa, v_w_br_gdn, v_w_mix_o, v_ln1_g, v_ln1_b, v_w_mem_q, v_w_mem_kv, v_w_mem_o, v_ln2_g, v_ln2_b, v_w_up, v_ffn_conv_w, v_ffn_conv_b, v_w_down, v_ln3_g, v_ln3_b):
    env = dict(locals())
    w2 = {n: (env[n][0] if env[n].ndim == 3 else env[n]) for n in WEIGHTS}
    m2 = {n: (env["m_" + n][0] if env["m_" + n].ndim == 3 else env["m_" + n]) for n in WEIGHTS}
    v2 = {n: (env["v_" + n][0] if env["v_" + n].ndim == 3 else env["v_" + n]) for n in WEIGHTS}
    xs, mems, target = x[0], mem[0], loss_target[0]
    my_id = 4 * lax.axis_index("x") + 2 * lax.axis_index("y") + lax.axis_index("c")
    pad_ff = FF_PAD - FF_SHARD

    pad_cols = {"w_in": W_IN_PAD - W_IN_SHARD, "w_up": pad_ff}
    mid = ("w_br_swa", "w_br_gdn", "w_mem_o", "w_mix_o", "w_mem_q", "w_mem_kv")
    mine = {n: jnp.pad(w2[n], ((0, 0), (0, pad_cols.get(n, 0)))).astype(bf16) for n in ("w_in", "w_up", "w_down") + mid}
    xb, got_in = cast_bf16(xs, name="cast_x", side=gather_first([mine["w_in"]]))
    got_in = run_side(gather_second(got_in), name="gather_w_in_pass_on")
    w_swa, w_gdn, w_gates = split_w_in(got_in[0], name="split_w_in")
    n_ffn, n_gdn = 3 * FF_SHARD, GDN_CONV * (QKV_W // N_DEV)
    conv_mine = jnp.concatenate([w2["ffn_conv_w"].reshape(-1), w2["gdn_conv_w"].reshape(-1)])[None]
    conv_rows = lax.dynamic_update_slice(jnp.zeros((N_DEV, n_ffn + n_gdn), f32), conv_mine, (my_id, 0))
    conv_all = all_reduce_small(_pack([conv_rows], CONV_ROWS, f32), name="gather_conv_w")
    conv_all = conv_all.reshape(-1)[:N_DEV * (n_ffn + n_gdn)].reshape(N_DEV, n_ffn + n_gdn)
    cwb = jnp.concatenate([conv_all[:, :n_ffn].reshape(N_DEV, 3, FF_SHARD), w2["ffn_conv_b"].reshape(N_DEV, 1, FF_SHARD),
                           jnp.zeros((N_DEV, 4, FF_SHARD), f32)], axis=1)
    cwb = jnp.pad(cwb, ((0, 0), (0, 0), (0, pad_ff)))
    convw = jnp.transpose(conv_all[:, n_ffn:].reshape(N_DEV, GDN_CONV, QKV_W // N_DEV), (1, 0, 2)).reshape(GDN_CONV, QKV_W)
    convw = jnp.pad(convw, ((0, 4), (0, 0)))
    onehot = _bucket_onehot()
    bias = mm(w2["rel_bias"].T, onehot, "nn", hi=True, tn=4096, name="rel_bias_table").reshape(SWA_HEADS, BLOCK, 2 * BLOCK)
    alog_row = jnp.pad(w2["gdn_a_log"], ((0, 0), (GDN_HEADS, 128 - 2 * GDN_HEADS)))
    dt_row = jnp.pad(w2["gdn_dt_bias"], ((0, 0), (GDN_HEADS, 128 - 2 * GDN_HEADS)))

    memb = cast_bf16(mems, name="cast_mem")
    gates, mid_got = mm(xb, w_gates, "nn", out_dtype=bf16, name="proj_gates", side=gather_first([mine[n] for n in mid]))
    gdn_in, mid_got = mm(xb, w_gdn, "nn", name="proj_gdn", side=gather_second(mid_got))
    got = dict(zip(mid, mid_got))
    w_br_swa, w_br_gdn, w_mem_o = (_from_column_shards(got[n]) for n in ("w_br_swa", "w_br_gdn", "w_mem_o"))
    w_mix_o = got["w_mix_o"].reshape(D_MODEL, D_MODEL)
    w_mem_q = got["w_mem_q"].reshape(D_MODEL, MEM_W)
    w_mem_kv = got["w_mem_kv"].reshape(D_MODEL, 2 * MEM_W)
    swa_in = mm(xb, w_swa, "nn", tn=SWA_IN_W, name="proj_swa")
    attn, down_got = swa_fwd(swa_in, bias, w2["swa_sinks"], name="swa_fwd", side=gather_first([mine["w_down"]]))
    qn, kn, vv, gdn_conv = gdn_pre_fwd(gdn_in, convw, name="gdn_pre_fwd")
    gbeta = gbeta_fwd(swa_in, alog_row, dt_row, name="gbeta_fwd")
    (o_gdn, states, inverses), up_got = gdn_scan_fwd(qn, kn, vv, gbeta, name="gdn_scan_fwd",
                                                     side=gather_first([mine["w_up"]]))
    ygd = gdn_post_fwd(o_gdn, gdn_in, w2["gdn_norm_w"], name="gdn_post_fwd")
    y_swa, down_got = mm(attn, w_br_swa, "nn", out_dtype=bf16, name="br_swa", side=gather_second(down_got))
    y_gdn, up_got = mm(ygd, w_br_gdn, "nn", out_dtype=bf16, name="br_gdn", side=gather_second(up_got))
    w_up_blocked = up_got[0]
    w_down_p = jnp.pad(down_got[0].reshape(4, FF_SHARD, D_MODEL), ((0, 0), (0, pad_ff), (0, 0))).reshape(4 * FF_PAD, D_MODEL)
    mixed = merge_fwd(gates, y_swa, y_gdn, name="merge_fwd")
    z1 = mm(mixed, w_mix_o, "nn", add=xs, add_scale=ALPHA, name="mix_o")
    x1, x1b = ln_fwd(z1, w2["ln1_g"], w2["ln1_b"], name="ln1_fwd")
    qm = mm(x1b, w_mem_q, "nn", name="mem_q")
    kv = mm(memb, w_mem_kv, "nn", name="mem_kv")
    om = memattn_fwd(qm, kv, name="memattn_fwd")
    z2 = mm(om, w_mem_o, "nn", add=x1, add_scale=ALPHA, name="mem_o")
    x2, x2b = ln_fwd(z2, w2["ln2_g"], w2["ln2_b"], name="ln2_fwd")
    hpre = mm(x2b, w_up_blocked, "nn", b_blocked=True, out_dtype=bf16, name="ffn_up")
    act, conv_g, conv_u = ffn_act_fwd(hpre, cwb, name="ffn_act_fwd")
    z3 = mm(act, w_down_p, "nn", add=x2, add_scale=ALPHA, tk=2 * FF_PAD, name="ffn_down")
    dz3, dz3b, d_ln3g, d_ln3b, loss = ln_loss(z3, target, w2["ln3_g"], w2["ln3_b"], name="ln3_loss")

    dact = mm(dz3b, w_down_p, "nt", tn=FF_PAD, out_dtype=bf16, name="d_act")
    d_wdown_p = mm(act, dz3b, "tn", tm=FF_PAD, name="dw_down")
    d_hpre, d_cwb = ffn_act_bwd(hpre, conv_g, conv_u, dact, cwb, name="ffn_act_bwd")
    def sibling_sums(names, chunks, received):
        return [add_sibling(c, r, name="grad_add_sibling_" + n) for n, c, r in zip(names, chunks, received)]

    dx2 = mm(d_hpre, w_up_blocked, "nt", b_blocked=True, k_shards=2, add=dz3, add_scale=ALPHA, name="d_x2")
    d_wup = mm(x2b, d_hpre, "tn", out_blocked=True, name="dw_up")
    ffn = ("w_up", "w_down")
    ffn_chunks = [d_wup, d_wdown_p.reshape(4, FF_PAD, D_MODEL)[:, :FF_SHARD].reshape(N_DEV, FF_SHARD // 2, D_MODEL)]
    dz2, dz2b, d_ln2g, d_ln2b = ln_bwd(dx2, z2, w2["ln2_g"], name="ln2_bwd")
    d_om, down_received = mm(dz2b, w_mem_o, "nt", name="d_om", side=grad_to_sibling(ffn_chunks[1:]))
    d_wmemo = mm(om, dz2b, "tn", name="dw_mem_o")
    dqm, dkv = memattn_bwd(qm, kv, d_om, name="memattn_bwd")
    dx1 = mm(dqm, w_mem_q, "nt", add=dz2, add_scale=ALPHA, name="d_x1")
    d_wmemq = mm(x1b, dqm, "tn", name="dw_mem_q")
    d_wmemkv = mm(memb, dkv, "tn", name="dw_mem_kv")
    dz1, dz1b, d_ln1g, d_ln1b = ln_bwd(dx1, z1, w2["ln1_g"], name="ln1_bwd")
    half = D_MODEL // 2
    dmix, up_received = mm(dz1b, w_mix_o, "nt", name="d_mixed", side=grad_to_sibling(ffn_chunks[:1], rows=(0, half)))
    d_wmixo, up_received = mm(mixed, dz1b, "tn", tk=4096, name="dw_mix_o",
                              side=grad_to_sibling(ffn_chunks[:1], rows=(half, half), into=up_received))
    ffn_sums = sibling_sums(ffn, ffn_chunks, up_received + down_received)
    dys, dyg, d_gates = merge_bwd(gates, y_swa, y_gdn, dmix, name="merge_bwd")
    d_attn = mm(dys, w_br_swa, "nt", name="d_attn")
    d_wbrswa = mm(attn, dys, "tn", tk=4096, name="dw_br_swa")
    d_ygd = mm(dyg, w_br_gdn, "nt", name="d_ygd")
    d_wbrgdn = mm(ygd, dyg, "tn", tk=4096, name="dw_br_gdn")
    mid_chunks = [_column_shards(d_wbrswa), _column_shards(d_wbrgdn), _column_shards(d_wmemo),
                  d_wmixo.reshape(N_DEV, D_MODEL // N_DEV, D_MODEL), d_wmemq.reshape(N_DEV, D_MODEL // N_DEV, MEM_W),
                  d_wmemkv.reshape(N_DEV, D_MODEL // N_DEV, 2 * MEM_W)]
    d_o, d_gz, d_normw = gdn_post_bwd(o_gdn, gdn_in, w2["gdn_norm_w"], d_ygd, name="gdn_post_bwd")
    (dqn, dkn, dvv, dgbeta), received = gdn_scan_bwd(
        qn, kn, vv, gbeta, states, inverses, d_o, name="gdn_scan_bwd",
        side=join_sides(grad_to_chips(ffn_sums), grad_to_sibling(mid_chunks)))
    chip_parts = dict(zip(ffn, received[:2]))
    mid_sums = sibling_sums(mid, mid_chunks, received[2:])
    d_gdn_in, d_convw = gdn_pre_bwd(gdn_in, gdn_conv, convw, dqn, dkn, dvv, d_gz, name="gdn_pre_bwd")
    d_ba, d_alog, d_dt = gbeta_bwd(swa_in, alog_row, dt_row, dgbeta, name="gbeta_bwd")
    (dq, dkc, dkp, dvc, dvp, d_bias, d_sinks), received = swa_bwd(swa_in, bias, w2["swa_sinks"], d_attn, name="swa_bwd",
                                                                  side=grad_to_chips(mid_sums))
    chip_parts.update(zip(mid, received))
    d_swa_in = swa_in_grad(dq, dkc, dkp, dvc, dvp, d_ba, name="swa_in_grad")
    d_relbias = mm(d_bias.reshape(SWA_HEADS, -1), onehot, "nt", hi=True, tk=4096, name="d_rel_bias").T
    d_wgates = mm(xb, d_gates, "tn", tk=4096, name="dw_gates")
    d_wgdn = mm(xb, d_gdn_in, "tn", tk=4096, name="dw_gdn")
    d_wswa = mm(xb, d_swa_in, "tn", tn=SWA_IN_W, name="dw_swa")
    in_chunks = [merge_w_in_grad(d_wswa, d_wgdn, d_wgates, name="merge_w_in_grad")]
    gx, received = mm(d_swa_in, w_swa, "nt", add=dz1, add_scale=ALPHA, tk=SWA_IN_W, name="dx_swa",
                      side=grad_to_sibling(in_chunks))
    in_sums = sibling_sums(("w_in",), in_chunks, received)
    gx, received = mm(d_gates, w_gates, "nt", add=gx, name="dx_gates", side=grad_to_chips(in_sums, rows=(0, half)))
    gx, received = mm(d_gdn_in, w_gdn, "nt", add=gx, name="dx_gdn",
                      side=grad_to_chips(in_sums, rows=(half, half), into=received))
    chip_parts["w_in"] = received[0]
    grads = {}

    gsmall = {
        "rel_bias": d_relbias, "swa_sinks": d_sinks[:, :SWA_HEADS], "gdn_a_log": d_alog[:, GDN_HEADS:2 * GDN_HEADS],
        "gdn_dt_bias": d_dt[:, GDN_HEADS:2 * GDN_HEADS], "gdn_norm_w": d_normw, "ln1_g": d_ln1g, "ln1_b": d_ln1b,
        "ln2_g": d_ln2g, "ln2_b": d_ln2b, "ln3_g": d_ln3g, "ln3_b": d_ln3b,
        "ffn_conv_b": d_cwb[:, 3, :FF_SHARD].reshape(1, 2 * D_FF),
        "ffn_conv_w": jnp.transpose(d_cwb[:, :3, :FF_SHARD], (1, 0, 2)).reshape(3, 2 * D_FF),
        "gdn_conv_w": d_convw[:GDN_CONV],
    }
    small_shapes = [shp for _, shp in SMALL] + [(3, 2 * D_FF), (GDN_CONV, QKV_W)]
    small_names = [n for n, _ in SMALL] + ["ffn_conv_w", "gdn_conv_w"]
    small_sum = all_reduce_small(_pack([gsmall[n] for n in small_names], AR_ROWS, f32), name="all_reduce_small")
    grads.update(zip(small_names, _unpack(small_sum.reshape(-1), small_shapes)))
    grads["ffn_conv_w"] = lax.dynamic_slice_in_dim(grads["ffn_conv_w"], my_id * FF_SHARD, FF_SHARD, axis=1)
    grads["gdn_conv_w"] = lax.dynamic_slice_in_dim(grads["gdn_conv_w"], my_id * (QKV_W // N_DEV), QKV_W // N_DEV, axis=1)

    big = [n for n, shp, _ in SHARDED if shp[0] * shp[1] > 8192]
    tiny = [n for n in WEIGHTS if n not in big]
    delta, new_m, new_v = {}, {}, {}
    for n in big:
        grads[n], delta[n], new_m[n], new_v[n] = adamw_of_partial_sums(w2[n], chip_parts[n], m2[n], v2[n], name="adamw_" + n)
    tiny_shapes = [w2[n].shape for n in tiny]
    packed = [_pack([src[n] for n in tiny], SMALL_ROWS, f32) for src in (w2, grads, m2, v2)]
    for dst, res in zip((delta, new_m, new_v), adamw(*packed, name="adamw_small")):
        dst.update(zip(tiny, _unpack(res.reshape(-1), tiny_shapes)))

    def shaped(d):
        return [d[n].reshape(env[n].shape) for n in WEIGHTS]

    loss_all = lax.psum(loss[0, 0], ("x", "y", "c"))
    return (loss_all, gx[None], *shaped(grads), *shaped(delta), *shaped(new_m), *shaped(new_v))
```

```python
import functools
import math
from typing import Callable, NamedTuple

import jax
import jax.numpy as jnp
from jax import lax
from jax.experimental import pallas as pl
from jax.experimental.pallas import tpu as pltpu

f32 = jnp.float32
bf16 = jnp.bfloat16
HI = lax.Precision.HIGHEST
MESH = pl.DeviceIdType.MESH

D_MODEL = 2048
N_DEV = 8
SWA_HEADS, SWA_KV_HEADS, SWA_HEAD_DIM, BLOCK = 16, 2, 64, 128
REL_BUCKETS, REL_MAX_DIST = 32, 128
GDN_HEADS, GDN_HEAD_DIM, GDN_CONV, GDN_CHUNK = 8, 128, 4, 64
MEM_HEADS, MEM_HEAD_DIM = 4, 128
D_FF = 5504
FF_SHARD = 2 * D_FF // N_DEV
FF_PAD = 1408
NORM_EPS = 1e-5
ALPHA = 2.0 ** 0.25
NEG_INF = -1e30
SWA_Q, SWA_KV, GDN_W, MEM_W = 1024, 128, 1024, 512
IN_DIM = 9488
HALO = 8

ADAM_LR, ADAM_B1, ADAM_B2, ADAM_EPS, ADAM_WD, ADAM_STEP = 0.001, 0.9, 0.999, 1e-08, 0.01, 10

PACK_COLS = 1024
SMALL_ROWS = 32
AR_ROWS = 72
CONV_ROWS = 48

SHARDED = (
    ("w_in", (2048, 1186), 1), ("w_br_swa", (1024, 256), 1), ("w_br_gdn", (1024, 256), 1),
    ("w_mix_o", (256, 2048), 0), ("w_mem_q", (256, 512), 0), ("w_mem_kv", (256, 1024), 0),
    ("w_mem_o", (512, 256), 1), ("w_up", (2048, 1376), 1), ("w_down", (688, 2048), 0),
    ("ffn_conv_w", (3, 1376), 1), ("gdn_conv_w", (4, 384), 1),
)
SMALL = (
    ("rel_bias", (32, 16)), ("swa_sinks", (1, 16)), ("gdn_a_log", (1, 8)), ("gdn_dt_bias", (1, 8)),
    ("gdn_norm_w", (1, 128)), ("ln1_g", (1, 2048)), ("ln1_b", (1, 2048)), ("ln2_g", (1, 2048)),
    ("ln2_b", (1, 2048)), ("ln3_g", (1, 2048)), ("ln3_b", (1, 2048)), ("ffn_conv_b", (1, 11008)),
)
WEIGHTS = ("w_in", "rel_bias", "swa_sinks", "gdn_conv_w", "gdn_a_log", "gdn_dt_bias", "gdn_norm_w", "w_br_swa",
           "w_br_gdn", "w_mix_o", "ln1_g", "ln1_b", "w_mem_q", "w_mem_kv", "w_mem_o", "ln2_g", "ln2_b", "w_up",
           "ffn_conv_w", "ffn_conv_b", "w_down", "ln3_g", "ln3_b")


def _tile(n, target, align):
    if n <= target:
        return n
    t = (target // align) * align
    while t >= align:
        if n % t == 0:
            return t
        t -= align
    return n


VMEM_LIMIT_BYTES = 56 * 1024 * 1024


def _params(*sem):
    return pltpu.CompilerParams(dimension_semantics=sem, vmem_limit_bytes=VMEM_LIMIT_BYTES)


def _sigmoid(v):
    return jax.nn.sigmoid(v)


def _d16(a, b, dims):
    return lax.dot_general(a.astype(bf16), b.astype(bf16), (dims, ((), ())), preferred_element_type=f32)


NN = ((1,), (0,))
NT = ((1,), (1,))
TN = ((0,), (0,))


def mm(a, b, mode, *, name, add=None, add_scale=1.0, out_dtype=f32, hi=False, tm=1024, tn=1024, tk=2048,
       b_blocked=False, out_blocked=False, k_shards=1, side=None):
    if b_blocked:
        nb, rows, width = b.shape
        if mode == "nn":
            (m, k), n, tn = a.shape, nb * width, width
        else:
            (m, k), n, tk = a.shape, rows, k_shards * width
    elif mode == "nn":
        (m, k), (_, n) = a.shape, b.shape
    elif mode == "nt":
        (m, k), (n, _) = a.shape, b.shape
    else:
        (k, m), (_, n) = a.shape, b.shape
    if out_blocked:
        tn = n // N_DEV
    tm, tn, tk = _tile(m, tm, 8 if mode != "tn" else 128), _tile(n, tn, 128), _tile(k, tk, 128 if mode != "tn" else 8)
    nk = k // tk
    dims = {"nn": NN, "nt": NT, "tn": TN}[mode]
    a_spec = pl.BlockSpec((tk, tm), lambda i, j, kk: (kk, i)) if mode == "tn" else pl.BlockSpec((tm, tk), lambda i, j, kk: (i, kk))
    if b_blocked:
        b_spec = (pl.BlockSpec((None, tk, tn), lambda i, j, kk: (j, kk, 0)) if mode == "nn"
                  else pl.BlockSpec((k_shards, tn, tk // k_shards), lambda i, j, kk: (kk, j, 0)))
    else:
        b_spec = pl.BlockSpec((tn, tk), lambda i, j, kk: (j, kk)) if mode == "nt" else pl.BlockSpec((tk, tn), lambda i, j, kk: (kk, j))
    if out_blocked:
        o_spec, o_shape = pl.BlockSpec((None, tm, tn), lambda i, j, kk: (j, i, 0)), (N_DEV, m, tn)
    else:
        o_spec, o_shape = pl.BlockSpec((tm, tn), lambda i, j, kk: (i, j)), (m, n)
    has_add = add is not None

    def product(a_ref, b_ref):
        if hi:
            return lax.dot_general(a_ref[...], b_ref[...], (dims, ((), ())), precision=HI, preferred_element_type=f32)
        if b_blocked and mode == "nt":
            width = tk // k_shards
            parts = [_d16(a_ref[:, s * width:(s + 1) * width], b_ref[s], dims) for s in range(k_shards)]
            return functools.reduce(lambda p, q: p + q, parts)
        return _d16(a_ref[...], b_ref[...], dims)

    def finish(r, add_ref, o_ref):
        if has_add:
            r = r + add_scale * add_ref[...]
        o_ref[...] = r.astype(out_dtype)

    def body_one_step(a_ref, b_ref, *rest):
        finish(product(a_ref, b_ref), rest[0] if has_add else None, rest[-1])

    def body_k_steps(a_ref, b_ref, *rest):
        o_ref, acc_ref = rest[-2:]
        kk = pl.program_id(2)

        @pl.when(kk == 0)
        def _():
            acc_ref[...] = jnp.zeros_like(acc_ref)

        acc_ref[...] += product(a_ref, b_ref)

        @pl.when(kk == nk - 1)
        def _():
            finish(acc_ref[...], rest[0] if has_add else None, o_ref)

    return _call(body_one_step if nk == 1 else body_k_steps, (a, b, add) if has_add else (a, b), name=name,
                 grid=(m // tm, n // tn, nk), in_specs=[a_spec, b_spec] + ([o_spec] if has_add else []), out_specs=o_spec,
                 out_shape=jax.ShapeDtypeStruct(o_shape, out_dtype),
                 scratch_shapes=[] if nk == 1 else [pltpu.VMEM((tm, tn), f32)],
                 semantics=("parallel", "parallel", "arbitrary"), side=side)


def cast_bf16(a, *, name, side=None):
    m, n = a.shape
    tm = _tile(m, 512, 16)

    def body(a_ref, o_ref):
        o_ref[...] = a_ref[...].astype(bf16)

    return _call(body, (a,), name=name, grid=(m // tm,), in_specs=[pl.BlockSpec((tm, n), lambda i: (i, 0))],
                 out_specs=pl.BlockSpec((tm, n), lambda i: (i, 0)), out_shape=jax.ShapeDtypeStruct((m, n), bf16),
                 semantics=("parallel",), side=side)


def _ln_stats(z):
    mu = jnp.mean(z, axis=-1, keepdims=True)
    zc = z - mu
    var = jnp.mean(zc * zc, axis=-1, keepdims=True)
    rstd = lax.rsqrt(var + NORM_EPS)
    return zc * rstd, rstd


def ln_fwd(z, g, b, *, name):
    s, d = z.shape
    tm = _tile(s, 512, 16)

    def body(z_ref, g_ref, b_ref, y_ref, yb_ref):
        xhat, _ = _ln_stats(z_ref[...])
        y = xhat * g_ref[...] + b_ref[...]
        y_ref[...] = y
        yb_ref[...] = y.astype(bf16)

    row = pl.BlockSpec((tm, d), lambda i: (i, 0))
    vec = pl.BlockSpec((1, d), lambda i: (0, 0))
    return pl.pallas_call(body, name=name, grid=(s // tm,), in_specs=[row, vec, vec], out_specs=[row, row],
                          out_shape=[jax.ShapeDtypeStruct((s, d), f32), jax.ShapeDtypeStruct((s, d), bf16)],
                          compiler_params=_params("parallel"))(z, g, b)


def _ln_bwd_tile(dy, z, g):
    xhat, rstd = _ln_stats(z)
    dxh = dy * g
    m1 = jnp.mean(dxh, axis=-1, keepdims=True)
    m2 = jnp.mean(dxh * xhat, axis=-1, keepdims=True)
    dz = rstd * (dxh - m1 - xhat * m2)
    return dz, jnp.sum(dy * xhat, axis=0, keepdims=True), jnp.sum(dy, axis=0, keepdims=True)


def ln_bwd(dy, z, g, *, name):
    s, d = z.shape
    tm = _tile(s, 256, 16)

    def body(dy_ref, z_ref, g_ref, dz_ref, dzb_ref, dg_ref, db_ref):
        @pl.when(pl.program_id(0) == 0)
        def _():
            dg_ref[...] = jnp.zeros_like(dg_ref)
            db_ref[...] = jnp.zeros_like(db_ref)

        dz, dg, db = _ln_bwd_tile(dy_ref[...], z_ref[...], g_ref[...])
        dz_ref[...] = dz
        dzb_ref[...] = dz.astype(bf16)
        dg_ref[...] += dg
        db_ref[...] += db

    row = pl.BlockSpec((tm, d), lambda i: (i, 0))
    vec = pl.BlockSpec((1, d), lambda i: (0, 0))
    return pl.pallas_call(body, name=name, grid=(s // tm,), in_specs=[row, row, vec], out_specs=[row, row, vec, vec],
                          out_shape=[jax.ShapeDtypeStruct((s, d), f32), jax.ShapeDtypeStruct((s, d), bf16),
                                     jax.ShapeDtypeStruct((1, d), f32), jax.ShapeDtypeStruct((1, d), f32)],
                          compiler_params=_params("arbitrary"))(dy, z, g)


def ln_loss(z, target, g, b, *, name):
    s, d = z.shape
    tm = _tile(s, 256, 16)
    nt = s // tm

    def body(z_ref, t_ref, g_ref, b_ref, dz_ref, dzb_ref, dg_ref, db_ref, loss_ref, lacc_ref):
        i = pl.program_id(0)

        @pl.when(i == 0)
        def _():
            dg_ref[...] = jnp.zeros_like(dg_ref)
            db_ref[...] = jnp.zeros_like(db_ref)
            lacc_ref[...] = jnp.zeros_like(lacc_ref)

        zv, gv = z_ref[...], g_ref[...]
        xhat, _ = _ln_stats(zv)
        err = xhat * gv + b_ref[...] - t_ref[...]
        lacc_ref[...] += jnp.sum(err * err, axis=0, keepdims=True)
        dz, dg, db = _ln_bwd_tile(err * (1.0 / d), zv, gv)
        dz_ref[...] = dz
        dzb_ref[...] = dz.astype(bf16)
        dg_ref[...] += dg
        db_ref[...] += db

        @pl.when(i == nt - 1)
        def _():
            loss_ref[...] = (0.5 / d) * jnp.sum(lacc_ref[...], axis=1, keepdims=True)

    row = pl.BlockSpec((tm, d), lambda i: (i, 0))
    vec = pl.BlockSpec((1, d), lambda i: (0, 0))
    return pl.pallas_call(body, name=name, grid=(nt,), in_specs=[row, row, vec, vec],
                          out_specs=[row, row, vec, vec, pl.BlockSpec((1, 1), lambda i: (0, 0))],
                          out_shape=[jax.ShapeDtypeStruct((s, d), f32), jax.ShapeDtypeStruct((s, d), bf16),
                                     jax.ShapeDtypeStruct((1, d), f32), jax.ShapeDtypeStruct((1, d), f32),
                                     jax.ShapeDtypeStruct((1, 1), f32)],
                          scratch_shapes=[pltpu.VMEM((1, d), f32)],
                          compiler_params=_params("arbitrary"))(z, target, g, b)


def merge_fwd(gates, ys, yg, *, name):
    s, d = ys.shape
    tm = _tile(s, 512, 16)

    def body(gt_ref, ys_ref, yg_ref, o_ref):
        o_ref[...] = (_sigmoid(gt_ref[:, :d].astype(f32)) * ys_ref[...].astype(f32)
                      + _sigmoid(gt_ref[:, d:].astype(f32)) * yg_ref[...].astype(f32)).astype(bf16)

    row = pl.BlockSpec((tm, d), lambda i: (i, 0))
    return pl.pallas_call(body, name=name, grid=(s // tm,), in_specs=[pl.BlockSpec((tm, 2 * d), lambda i: (i, 0)), row, row],
                          out_specs=row, out_shape=jax.ShapeDtypeStruct((s, d), bf16),
                          compiler_params=_params("parallel"))(gates, ys, yg)


def merge_bwd(gates, ys, yg, dmix, *, name):
    s, d = ys.shape
    tm = _tile(s, 256, 16)

    def body(gt_ref, ys_ref, yg_ref, dm_ref, dys_ref, dyg_ref, dgt_ref):
        dm = dm_ref[...]
        sa, sb = _sigmoid(gt_ref[:, :d].astype(f32)), _sigmoid(gt_ref[:, d:].astype(f32))
        dys_ref[...] = (dm * sa).astype(bf16)
        dyg_ref[...] = (dm * sb).astype(bf16)
        dgt_ref[:, :d] = (dm * ys_ref[...].astype(f32) * sa * (1.0 - sa)).astype(bf16)
        dgt_ref[:, d:] = (dm * yg_ref[...].astype(f32) * sb * (1.0 - sb)).astype(bf16)

    row = pl.BlockSpec((tm, d), lambda i: (i, 0))
    wide = pl.BlockSpec((tm, 2 * d), lambda i: (i, 0))
    return pl.pallas_call(body, name=name, grid=(s // tm,), in_specs=[wide, row, row, row], out_specs=[row, row, wide],
                          out_shape=[jax.ShapeDtypeStruct((s, d), bf16), jax.ShapeDtypeStruct((s, d), bf16),
                                     jax.ShapeDtypeStruct((s, 2 * d), bf16)],
                          compiler_params=_params("parallel"))(gates, ys, yg, dmix)


def _shift_down(ext, j):
    return ext if j == 0 else pltpu.roll(ext, j, 0)


def _shift_up(ext, j):
    return ext if j == 0 else pltpu.roll(ext, ext.shape[0] - j, 0)


def _conv_taps(ext, width):
    return [_shift_down(ext, width - 1 - j)[HALO:] for j in range(width)]


def _causal_conv(taps, w_ref):
    acc = None
    for j, tap in enumerate(taps):
        term = w_ref[j:j + 1, :] * tap
        acc = term if acc is None else acc + term
    return acc


def _conv_grads(dy_ext, x, w_ref, width, rows):
    ahead = [_shift_up(dy_ext, width - 1 - j)[:rows] for j in range(width)]
    dx = None
    for j in range(width):
        term = w_ref[j:j + 1, :] * ahead[j]
        dx = term if dx is None else dx + term
    return dx, [jnp.sum(x * ahead[j], axis=0, keepdims=True) for j in range(width)]


def _rows_to_block(rows, n_rows, cols):
    r = lax.broadcasted_iota(jnp.int32, (n_rows, cols), 0)
    out = jnp.zeros((n_rows, cols), f32)
    for j, v in enumerate(rows):
        out = out + jnp.where(r == j, v, 0.0)
    return out


def _silu_and_grad(v):
    sg = _sigmoid(v)
    return v * sg, sg * (1.0 + v * (1.0 - sg))


HALO_BF16 = 16


def ffn_act_fwd(hpre, cwb, *, name):
    s = hpre.shape[0]
    tm = _tile(s, 512, 16)
    hb = tm // HALO_BF16

    def body(hg_ref, hgp_ref, hu_ref, hup_ref, cg_ref, cu_ref, o_ref, g_ref, u_ref):
        first = pl.program_id(1) == 0

        def conv(h_ref, hp_ref, c_ref):
            prev = jnp.where(first, 0.0, hp_ref[...].astype(f32)[HALO_BF16 - HALO:])
            ext = jnp.concatenate([prev, h_ref[...].astype(f32)], axis=0)
            return _causal_conv(_conv_taps(ext, 3), c_ref.at[0]) + c_ref[0, 3:4, :]

        g = conv(hg_ref, hgp_ref, cg_ref)
        u = conv(hu_ref, hup_ref, cu_ref)
        g_ref[...] = g.astype(bf16)
        u_ref[...] = u.astype(bf16)
        o_ref[...] = (g * _sigmoid(g) * u).astype(bf16)

    def tile(off):
        return pl.BlockSpec((tm, FF_PAD), lambda j, i: (i, j + off))

    def halo(off):
        return pl.BlockSpec((HALO_BF16, FF_PAD), lambda j, i: (jnp.maximum(i * hb - 1, 0), j + off))

    def taps(off):
        return pl.BlockSpec((1, 8, FF_PAD), lambda j, i: (j + off, 0, 0))

    out = pl.BlockSpec((tm, FF_PAD), lambda j, i: (i, j))
    return pl.pallas_call(body, name=name, grid=(4, s // tm),
                          in_specs=[tile(0), halo(0), tile(4), halo(4), taps(0), taps(4)], out_specs=[out, out, out],
                          out_shape=[jax.ShapeDtypeStruct((s, 4 * FF_PAD), bf16)] * 3,
                          compiler_params=_params("parallel", "parallel"))(hpre, hpre, hpre, hpre, cwb, cwb)


def ffn_act_bwd(hpre, conv_g, conv_u, dact, cwb, *, name):
    s = hpre.shape[0]
    tm = _tile(s, 512, 16)
    hb = tm // HALO_BF16
    nt = s // tm
    last_hb = s // HALO_BF16 - 1

    def body(hg_ref, hu_ref, g_ref, gn_ref, u_ref, un_ref, d_ref, dn_ref, cg_ref, cu_ref, dh_ref, dcg_ref, dcu_ref,
             buf_ref, sems):
        j, i = pl.program_id(0), pl.program_id(1)
        step = j * nt + i
        slot = step % 2

        def writes(from_slot):
            rows = pl.ds(pl.multiple_of(i * tm, tm), tm)
            return [pltpu.make_async_copy(buf_ref.at[from_slot, half],
                                          dh_ref.at[rows, pl.ds(pl.multiple_of((j + 4 * half) * FF_PAD, 128), FF_PAD)],
                                          sems.at[from_slot, half]) for half in (0, 1)]

        @pl.when(i == 0)
        def _():
            dcg_ref[...] = jnp.zeros_like(dcg_ref)
            dcu_ref[...] = jnp.zeros_like(dcu_ref)

        @pl.when(step >= 2)
        def _():
            for cp in writes(slot):
                cp.wait()

        def with_future(t_ref, n_ref):
            return jnp.concatenate([t_ref[...].astype(f32), n_ref[...].astype(f32)[:HALO]], axis=0)

        g, u = with_future(g_ref, gn_ref), with_future(u_ref, un_ref)
        d = jnp.concatenate([d_ref[...].astype(f32), jnp.where(i == nt - 1, 0.0, dn_ref[...].astype(f32)[:HALO])], axis=0)
        act, dact_dg = _silu_and_grad(g)
        dg = d * u * dact_dg
        du = d * act
        dhg, dwg = _conv_grads(dg, hg_ref[...].astype(f32), cg_ref.at[0], 3, tm)
        dhu, dwu = _conv_grads(du, hu_ref[...].astype(f32), cu_ref.at[0], 3, tm)
        buf_ref[slot, 0] = dhg.astype(bf16)
        buf_ref[slot, 1] = dhu.astype(bf16)
        for cp in writes(slot):
            cp.start()
        dcg_ref[0] += _rows_to_block(dwg + [jnp.sum(dg[:tm], axis=0, keepdims=True)], 8, FF_PAD)
        dcu_ref[0] += _rows_to_block(dwu + [jnp.sum(du[:tm], axis=0, keepdims=True)], 8, FF_PAD)

        @pl.when(step == 4 * nt - 1)
        def _():
            for cp in writes(slot) + writes(1 - slot):
                cp.wait()

    def tile(off):
        return pl.BlockSpec((tm, FF_PAD), lambda j, i: (i, j + off))

    nxt = pl.BlockSpec((HALO_BF16, FF_PAD), lambda j, i: (jnp.minimum((i + 1) * hb, last_hb), j))
    taps = [pl.BlockSpec((1, 8, FF_PAD), lambda j, i, off=off: (j + off, 0, 0)) for off in (0, 4)]
    dh, dcg, dcu = pl.pallas_call(
        body, name=name, grid=(4, nt),
        in_specs=[tile(0), tile(4), tile(0), nxt, tile(0), nxt, tile(0), nxt] + taps,
        out_specs=[ANY, taps[0], taps[0]],
        out_shape=[jax.ShapeDtypeStruct((s, 8 * FF_PAD), bf16),
                   jax.ShapeDtypeStruct((4, 8, FF_PAD), f32), jax.ShapeDtypeStruct((4, 8, FF_PAD), f32)],
        scratch_shapes=[pltpu.VMEM((2, 2, tm, FF_PAD), bf16), pltpu.SemaphoreType.DMA((2, 2))],
        compiler_params=_params("arbitrary", "arbitrary"),
    )(hpre, hpre, conv_g, conv_g, conv_u, conv_u, dact, dact, cwb, cwb)
    return dh, jnp.concatenate([dcg, dcu], axis=0)


MEM_SCALE = MEM_HEAD_DIM ** -0.5


def _softmax_rows(sc):
    m = jnp.max(sc, axis=-1, keepdims=True)
    e = jnp.exp(sc - m)
    return e / jnp.sum(e, axis=-1, keepdims=True)


def memattn_fwd(qm, kv, *, name):
    s = qm.shape[0]
    mlen = kv.shape[0]
    tm = _tile(s, 512, 16)

    def body(q_ref, kv_ref, o_ref):
        for h in range(MEM_HEADS):
            lo = h * MEM_HEAD_DIM
            q = q_ref[:, lo:lo + MEM_HEAD_DIM]
            k = kv_ref[:, lo:lo + MEM_HEAD_DIM]
            v = kv_ref[:, MEM_W + lo:MEM_W + lo + MEM_HEAD_DIM]
            p = _softmax_rows(_d16(q, k, NT) * MEM_SCALE)
            o_ref[:, lo:lo + MEM_HEAD_DIM] = _d16(p, v, NN).astype(bf16)

    return pl.pallas_call(body, name=name, grid=(s // tm,),
                          in_specs=[pl.BlockSpec((tm, MEM_W), lambda i: (i, 0)), pl.BlockSpec((mlen, 2 * MEM_W), lambda i: (0, 0))],
                          out_specs=pl.BlockSpec((tm, MEM_W), lambda i: (i, 0)),
                          out_shape=jax.ShapeDtypeStruct((s, MEM_W), bf16), compiler_params=_params("parallel"))(qm, kv)


def memattn_bwd(qm, kv, dout, *, name):
    s = qm.shape[0]
    mlen = kv.shape[0]
    tm = _tile(s, 512, 16)

    def body(q_ref, kv_ref, do_ref, dq_ref, dkv_ref):
        @pl.when(pl.program_id(0) == 0)
        def _():
            dkv_ref[...] = jnp.zeros_like(dkv_ref)

        for h in range(MEM_HEADS):
            lo = h * MEM_HEAD_DIM
            q = q_ref[:, lo:lo + MEM_HEAD_DIM]
            k = kv_ref[:, lo:lo + MEM_HEAD_DIM]
            v = kv_ref[:, MEM_W + lo:MEM_W + lo + MEM_HEAD_DIM]
            do = do_ref[:, lo:lo + MEM_HEAD_DIM]
            p = _softmax_rows(_d16(q, k, NT) * MEM_SCALE)
            dp = _d16(do, v, NT)
            ds = p * (dp - jnp.sum(p * dp, axis=-1, keepdims=True)) * MEM_SCALE
            dq_ref[:, lo:lo + MEM_HEAD_DIM] = _d16(ds, k, NN).astype(bf16)
            dkv_ref[:, lo:lo + MEM_HEAD_DIM] += _d16(ds, q, TN)
            dkv_ref[:, MEM_W + lo:MEM_W + lo + MEM_HEAD_DIM] += _d16(p, do, TN)

    row = pl.BlockSpec((tm, MEM_W), lambda i: (i, 0))
    full = pl.BlockSpec((mlen, 2 * MEM_W), lambda i: (0, 0))
    return pl.pallas_call(body, name=name, grid=(s // tm,), in_specs=[row, full, row], out_specs=[row, full],
                          out_shape=[jax.ShapeDtypeStruct((s, MEM_W), bf16), jax.ShapeDtypeStruct((mlen, 2 * MEM_W), f32)],
                          compiler_params=_params("arbitrary"))(qm, kv, dout)


SWA_SCALE = SWA_HEAD_DIM ** -0.5
SWA_GROUP = SWA_HEADS // SWA_KV_HEADS
SWA_IN_W = 1408
K_COL, V_COL, BA_COL = SWA_Q // 128, SWA_Q // 128 + 1, SWA_Q // 128 + 2


def _swa_mask(n):
    qi = lax.broadcasted_iota(jnp.int32, (BLOCK, 2 * BLOCK), 0)
    kj = lax.broadcasted_iota(jnp.int32, (BLOCK, 2 * BLOCK), 1)
    dist = qi + BLOCK - kj
    return (dist >= 0) & (dist < BLOCK) & ((n > 0) | (kj >= BLOCK))


def _swa_probs(q, k, bias, sink, mask):
    heads = range(len(q))
    sc = [jnp.where(mask, _d16(q[h], k[h], NT) * SWA_SCALE + bias[h], NEG_INF) for h in heads]
    m = [jnp.maximum(jnp.max(sc[h], axis=-1, keepdims=True), sink[h]) for h in heads]
    e = [jnp.exp(sc[h] - m[h]) for h in heads]
    es = [jnp.exp(sink[h] - m[h]) for h in heads]
    inv = [1.0 / (jnp.sum(e[h], axis=-1, keepdims=True) + es[h]) for h in heads]
    return e, es, inv


def _swa_heads(ref):
    return [ref[:, h * SWA_HEAD_DIM:(h + 1) * SWA_HEAD_DIM] for h in range(SWA_HEADS)]


def _swa_kv_of_heads(band):
    kv = [band[:, g * SWA_HEAD_DIM:(g + 1) * SWA_HEAD_DIM] for g in range(SWA_KV_HEADS)]
    return [kv[h // SWA_GROUP] for h in range(SWA_HEADS)]


def _swa_specs():
    q_spec = pl.BlockSpec((BLOCK, SWA_Q), lambda n: (n, 0))

    def band(col):
        return [pl.BlockSpec((BLOCK, SWA_KV), lambda n: (jnp.maximum(n - 1, 0), col)),
                pl.BlockSpec((BLOCK, SWA_KV), lambda n: (n, col))]

    bias_spec = pl.BlockSpec((SWA_HEADS, BLOCK, 2 * BLOCK), lambda n: (0, 0, 0))
    sink_spec = pl.BlockSpec((1, SWA_HEADS), lambda n: (0, 0))
    return [q_spec] + band(K_COL) + band(V_COL) + [bias_spec, sink_spec]


def swa_fwd(swa_in, bias, sinks, *, name, side=None):
    s = swa_in.shape[0]

    def body(q_ref, kp_ref, kc_ref, vp_ref, vc_ref, bias_ref, sink_ref, o_ref):
        mask = _swa_mask(pl.program_id(0))
        kb = jnp.concatenate([kp_ref[...], kc_ref[...]], axis=0)
        vb = jnp.concatenate([vp_ref[...], vc_ref[...]], axis=0)
        heads = range(SWA_HEADS)
        k, v = _swa_kv_of_heads(kb), _swa_kv_of_heads(vb)
        e, _, inv = _swa_probs(_swa_heads(q_ref), k, [bias_ref[h] for h in heads], [sink_ref[:, h:h + 1] for h in heads], mask)
        outs = [_d16(e[h] * inv[h], v[h], NN) for h in heads]
        for h in heads:
            o_ref[:, h * SWA_HEAD_DIM:(h + 1) * SWA_HEAD_DIM] = outs[h].astype(bf16)

    return _call(body, (swa_in, swa_in, swa_in, swa_in, swa_in, bias, sinks), name=name, grid=(s // BLOCK,),
                 in_specs=_swa_specs(), out_specs=pl.BlockSpec((BLOCK, SWA_Q), lambda n: (n, 0)),
                 out_shape=jax.ShapeDtypeStruct((s, SWA_Q), bf16), semantics=("parallel",), side=side)


def swa_bwd(swa_in, bias, sinks, dout, *, name, side=None):
    s = swa_in.shape[0]

    def body(q_ref, kp_ref, kc_ref, vp_ref, vc_ref, bias_ref, sink_ref, do_ref,
             dq_ref, dkc_ref, dkp_ref, dvc_ref, dvp_ref, dbias_ref, dsink_ref):
        n = pl.program_id(0)

        @pl.when(n == 0)
        def _():
            dbias_ref[...] = jnp.zeros_like(dbias_ref)
            dsink_ref[...] = jnp.zeros_like(dsink_ref)

        mask = _swa_mask(n)
        kb = jnp.concatenate([kp_ref[...], kc_ref[...]], axis=0)
        vb = jnp.concatenate([vp_ref[...], vc_ref[...]], axis=0)
        lane = lax.broadcasted_iota(jnp.int32, (1, 128), 1)
        hs = range(SWA_HEADS)
        q, do = _swa_heads(q_ref), _swa_heads(do_ref)
        k, v = _swa_kv_of_heads(kb), _swa_kv_of_heads(vb)
        e, es, inv = _swa_probs(q, k, [bias_ref[h] for h in hs], [sink_ref[:, h:h + 1] for h in hs], mask)
        p = [e[h] * inv[h] for h in hs]
        dp = [_d16(do[h], v[h], NT) for h in hs]
        delta = [jnp.sum(p[h] * dp[h], axis=-1, keepdims=True) for h in hs]
        ds = [p[h] * (dp[h] - delta[h]) for h in hs]
        dss = [ds[h] * SWA_SCALE for h in hs]
        dq = [_d16(dss[h], k[h], NN) for h in hs]
        dks = [_d16(dss[h], q[h], TN) for h in hs]
        dvs = [_d16(p[h], do[h], TN) for h in hs]
        dsink = jnp.zeros((1, 128), f32)
        for h in hs:
            dbias_ref[h] += ds[h]
            dq_ref[:, h * SWA_HEAD_DIM:(h + 1) * SWA_HEAD_DIM] = dq[h]
            dsink = dsink + jnp.where(lane == h, -jnp.sum(es[h] * inv[h] * delta[h], axis=0, keepdims=True), 0.0)
        for g in range(SWA_KV_HEADS):
            kl = g * SWA_HEAD_DIM
            group = range(g * SWA_GROUP, (g + 1) * SWA_GROUP)
            dk = functools.reduce(lambda a, b: a + b, [dks[h] for h in group])
            dv = functools.reduce(lambda a, b: a + b, [dvs[h] for h in group])
            dkp_ref[:, kl:kl + SWA_HEAD_DIM] = dk[:BLOCK]
            dkc_ref[:, kl:kl + SWA_HEAD_DIM] = dk[BLOCK:]
            dvp_ref[:, kl:kl + SWA_HEAD_DIM] = dv[:BLOCK]
            dvc_ref[:, kl:kl + SWA_HEAD_DIM] = dv[BLOCK:]
        dsink_ref[...] += dsink

    qs = pl.BlockSpec((BLOCK, SWA_Q), lambda n: (n, 0))
    ks = pl.BlockSpec((BLOCK, SWA_KV), lambda n: (n, 0))
    return _call(
        body, (swa_in, swa_in, swa_in, swa_in, swa_in, bias, sinks, dout), name=name, grid=(s // BLOCK,),
        in_specs=_swa_specs() + [qs],
        out_specs=[qs, ks, ks, ks, ks, pl.BlockSpec((SWA_HEADS, BLOCK, 2 * BLOCK), lambda n: (0, 0, 0)),
                   pl.BlockSpec((1, 128), lambda n: (0, 0))],
        out_shape=[jax.ShapeDtypeStruct((s, SWA_Q), f32)] + [jax.ShapeDtypeStruct((s, SWA_KV), f32)] * 4
        + [jax.ShapeDtypeStruct((SWA_HEADS, BLOCK, 2 * BLOCK), f32), jax.ShapeDtypeStruct((1, 128), f32)],
        semantics=("arbitrary",), side=side)


def swa_in_grad(dq, dkc, dkp, dvc, dvp, dba, *, name):
    s = dq.shape[0]
    nb = s // BLOCK

    def body(dq_ref, dkc_ref, dkp_ref, dvc_ref, dvp_ref, dba_ref, o_ref):
        has_next = pl.program_id(0) < nb - 1
        o_ref[:, :SWA_Q] = dq_ref[...].astype(bf16)
        o_ref[:, SWA_Q:SWA_Q + SWA_KV] = (dkc_ref[...] + jnp.where(has_next, dkp_ref[...], 0.0)).astype(bf16)
        o_ref[:, SWA_Q + SWA_KV:SWA_Q + 2 * SWA_KV] = (dvc_ref[...] + jnp.where(has_next, dvp_ref[...], 0.0)).astype(bf16)
        o_ref[:, SWA_Q + 2 * SWA_KV:] = dba_ref[...].astype(bf16)

    cur = pl.BlockSpec((BLOCK, SWA_KV), lambda n: (n, 0))
    nxt = pl.BlockSpec((BLOCK, SWA_KV), lambda n: (jnp.minimum(n + 1, nb - 1), 0))
    return pl.pallas_call(body, name=name, grid=(nb,),
                          in_specs=[pl.BlockSpec((BLOCK, SWA_Q), lambda n: (n, 0)), cur, nxt, cur, nxt, cur],
                          out_specs=pl.BlockSpec((BLOCK, SWA_IN_W), lambda n: (n, 0)),
                          out_shape=jax.ShapeDtypeStruct((s, SWA_IN_W), bf16),
                          compiler_params=_params("parallel"))(dq, dkc, dkp, dvc, dvp, dba)


def _bucket_onehot():
    qi = jnp.arange(BLOCK)[:, None]
    kj = jnp.arange(2 * BLOCK)[None, :]
    dist = jnp.maximum(qi + BLOCK - kj, 0)
    max_exact = REL_BUCKETS // 2
    dd = jnp.maximum(dist, 1).astype(f32)
    large = max_exact + (jnp.log(dd / max_exact) / math.log(REL_MAX_DIST / max_exact) * (REL_BUCKETS - max_exact)).astype(jnp.int32)
    bucket = jnp.where(dist < max_exact, dist, jnp.minimum(large, REL_BUCKETS - 1)).reshape(-1)
    return (bucket[None, :] == jnp.arange(REL_BUCKETS)[:, None]).astype(f32)


def _gbeta_fn(ba, alog_row, dt_row):
    col = lax.broadcasted_iota(jnp.int32, ba.shape, 1)
    v = ba + dt_row
    softplus = jnp.maximum(v, 0.0) + jnp.log(1.0 + jnp.exp(-jnp.abs(v)))
    g = -jnp.exp(alog_row) * softplus
    return jnp.where(col < GDN_HEADS, _sigmoid(ba), jnp.where(col < 2 * GDN_HEADS, g, 0.0))


def gbeta_fwd(swa_in, alog_row, dt_row, *, name):
    s = swa_in.shape[0]
    tm = _tile(s, 512, 8)

    def body(ba_ref, a_ref, d_ref, o_ref):
        o_ref[...] = _gbeta_fn(ba_ref[...], a_ref[...], d_ref[...])

    vec = pl.BlockSpec((1, 128), lambda i: (0, 0))
    return pl.pallas_call(body, name=name, grid=(s // tm,), in_specs=[pl.BlockSpec((tm, 128), lambda i: (i, BA_COL)), vec, vec],
                          out_specs=pl.BlockSpec((tm, 128), lambda i: (i, 0)), out_shape=jax.ShapeDtypeStruct((s, 128), f32),
                          compiler_params=_params("parallel"))(swa_in, alog_row, dt_row)


def gbeta_bwd(swa_in, alog_row, dt_row, dgbeta, *, name):
    s = swa_in.shape[0]
    tm = _tile(s, 512, 8)

    def body(ba_ref, a_ref, d_ref, dgb_ref, dba_ref, da_ref, dd_ref):
        @pl.when(pl.program_id(0) == 0)
        def _():
            da_ref[...] = jnp.zeros_like(da_ref)
            dd_ref[...] = jnp.zeros_like(dd_ref)

        _, pull = jax.vjp(_gbeta_fn, ba_ref[...], a_ref[...], d_ref[...])
        dba, da, dd = pull(dgb_ref[...])
        dba_ref[...] = dba
        da_ref[...] += da
        dd_ref[...] += dd

    vec = pl.BlockSpec((1, 128), lambda i: (0, 0))
    row = pl.BlockSpec((tm, 128), lambda i: (i, 0))
    return pl.pallas_call(body, name=name, grid=(s // tm,),
                          in_specs=[pl.BlockSpec((tm, 128), lambda i: (i, BA_COL)), vec, vec, row], out_specs=[row, vec, vec],
                          out_shape=[jax.ShapeDtypeStruct((s, 128), f32), jax.ShapeDtypeStruct((1, 128), f32),
                                     jax.ShapeDtypeStruct((1, 128), f32)],
                          compiler_params=_params("arbitrary"))(swa_in, alog_row, dt_row, dgbeta)


QKV_W = 3 * GDN_W


def gdn_pre_fwd(gdn_in, convw, *, name):
    s = gdn_in.shape[0]
    tm = _tile(s, 256, 16)
    hb = tm // HALO

    def body(x_ref, xp_ref, w_ref, q_ref, k_ref, v_ref, pre_ref):
        prev = jnp.where(pl.program_id(0) == 0, 0.0, xp_ref[...])
        pre = _causal_conv(_conv_taps(jnp.concatenate([prev, x_ref[...]], axis=0), GDN_CONV), w_ref)
        pre_ref[...] = pre
        act = pre * _sigmoid(pre)
        for h in range(GDN_HEADS):
            lo = h * GDN_HEAD_DIM
            for off, o_ref in ((0, q_ref), (GDN_W, k_ref)):
                seg = act[:, off + lo:off + lo + GDN_HEAD_DIM]
                o_ref[:, lo:lo + GDN_HEAD_DIM] = seg * lax.rsqrt(jnp.sum(seg * seg, axis=-1, keepdims=True) + 1e-6)
        v_ref[...] = act[:, 2 * GDN_W:]

    out = pl.BlockSpec((tm, GDN_W), lambda i: (i, 0))
    return pl.pallas_call(body, name=name, grid=(s // tm,),
                          in_specs=[pl.BlockSpec((tm, QKV_W), lambda i: (i, 0)),
                                    pl.BlockSpec((HALO, QKV_W), lambda i: (jnp.maximum(i * hb - 1, 0), 0)),
                                    pl.BlockSpec((8, QKV_W), lambda i: (0, 0))],
                          out_specs=[out, out, out, pl.BlockSpec((tm, QKV_W), lambda i: (i, 0))],
                          out_shape=[jax.ShapeDtypeStruct((s, GDN_W), f32)] * 3 + [jax.ShapeDtypeStruct((s, QKV_W), f32)],
                          compiler_params=_params("parallel"))(gdn_in, gdn_in, convw)


def gdn_pre_bwd(gdn_in, conv_out, convw, dqn, dkn, dv, dgz, *, name):
    s = gdn_in.shape[0]
    tm = _tile(s, 256, 16)
    hb = tm // HALO
    nt = s // tm
    last_hb = s // HALO - 1

    def body(x_ref, pre_ref, pren_ref, w_ref, dq_ref, dqx_ref, dk_ref, dkx_ref, dv_ref, dvx_ref, dz_ref, o_ref, dw_ref):
        i = pl.program_id(0)
        last = i == nt - 1

        @pl.when(i == 0)
        def _():
            dw_ref[...] = jnp.zeros_like(dw_ref)

        pre = jnp.concatenate([pre_ref[...], pren_ref[...]], axis=0)
        act, dact_dpre = _silu_and_grad(pre)

        def with_future(t_ref, n_ref):
            return jnp.concatenate([t_ref[...], jnp.where(last, 0.0, n_ref[...])], axis=0)

        dqe, dke, dve = with_future(dq_ref, dqx_ref), with_future(dk_ref, dkx_ref), with_future(dv_ref, dvx_ref)
        parts = []
        for off, dn in ((0, dqe), (GDN_W, dke)):
            for h in range(GDN_HEADS):
                lo = h * GDN_HEAD_DIM
                seg = act[:, off + lo:off + lo + GDN_HEAD_DIM]
                r = lax.rsqrt(jnp.sum(seg * seg, axis=-1, keepdims=True) + 1e-6)
                nrm = seg * r
                dseg = dn[:, lo:lo + GDN_HEAD_DIM]
                parts.append(r * (dseg - nrm * jnp.sum(dseg * nrm, axis=-1, keepdims=True)))
        dpre = jnp.concatenate(parts + [dve], axis=1) * dact_dpre
        dx, dw = _conv_grads(dpre, x_ref[...], w_ref, GDN_CONV, tm)
        o_ref[:, :QKV_W] = dx.astype(bf16)
        o_ref[:, QKV_W:] = dz_ref[...].astype(bf16)
        dw_ref[...] += _rows_to_block(dw, 8, QKV_W)

    row = pl.BlockSpec((tm, GDN_W), lambda i: (i, 0))
    fut = pl.BlockSpec((HALO, GDN_W), lambda i: (jnp.minimum((i + 1) * hb, last_hb), 0))
    wide = pl.BlockSpec((tm, QKV_W), lambda i: (i, 0))
    return pl.pallas_call(
        body, name=name, grid=(nt,),
        in_specs=[wide, wide, pl.BlockSpec((HALO, QKV_W), lambda i: (jnp.minimum((i + 1) * hb, last_hb), 0)),
                  pl.BlockSpec((8, QKV_W), lambda i: (0, 0)), row, fut, row, fut, row, fut, row],
        out_specs=[pl.BlockSpec((tm, 4 * GDN_W), lambda i: (i, 0)), pl.BlockSpec((8, QKV_W), lambda i: (0, 0))],
        out_shape=[jax.ShapeDtypeStruct((s, 4 * GDN_W), bf16), jax.ShapeDtypeStruct((8, QKV_W), f32)],
        compiler_params=_params("arbitrary"),
    )(gdn_in, conv_out, conv_out, convw, dqn, dqn, dkn, dkn, dv, dv, dgz)


def _gdn_post_head(o, z, nw):
    return o * lax.rsqrt(jnp.mean(o * o, axis=-1, keepdims=True) + 1e-6) * nw * (z * _sigmoid(z))


def gdn_post_fwd(o, gdn_in, nw, *, name):
    s = o.shape[0]
    tm = _tile(s, 512, 16)

    def body(o_ref, z_ref, nw_ref, y_ref):
        for h in range(GDN_HEADS):
            sl = slice(h * GDN_HEAD_DIM, (h + 1) * GDN_HEAD_DIM)
            y_ref[:, sl] = _gdn_post_head(o_ref[:, sl], z_ref[:, sl], nw_ref[...]).astype(bf16)

    row = pl.BlockSpec((tm, GDN_W), lambda i: (i, 0))
    return pl.pallas_call(body, name=name, grid=(s // tm,),
                          in_specs=[row, pl.BlockSpec((tm, GDN_W), lambda i: (i, 3)), pl.BlockSpec((1, 128), lambda i: (0, 0))],
                          out_specs=row, out_shape=jax.ShapeDtypeStruct((s, GDN_W), bf16),
                          compiler_params=_params("parallel"))(o, gdn_in, nw)


def gdn_post_bwd(o, gdn_in, nw, dy, *, name):
    s = o.shape[0]
    tm = _tile(s, 512, 16)

    def body(o_ref, z_ref, nw_ref, dy_ref, do_ref, dz_ref, dnw_ref):
        @pl.when(pl.program_id(0) == 0)
        def _():
            dnw_ref[...] = jnp.zeros_like(dnw_ref)

        dnw = jnp.zeros((1, 128), f32)
        for h in range(GDN_HEADS):
            sl = slice(h * GDN_HEAD_DIM, (h + 1) * GDN_HEAD_DIM)
            _, pull = jax.vjp(_gdn_post_head, o_ref[:, sl], z_ref[:, sl], nw_ref[...])
            do, dz, dn = pull(dy_ref[:, sl])
            do_ref[:, sl] = do
            dz_ref[:, sl] = dz
            dnw = dnw + dn
        dnw_ref[...] += dnw

    row = pl.BlockSpec((tm, GDN_W), lambda i: (i, 0))
    vec = pl.BlockSpec((1, 128), lambda i: (0, 0))
    return pl.pallas_call(body, name=name, grid=(s // tm,),
                          in_specs=[row, pl.BlockSpec((tm, GDN_W), lambda i: (i, 3)), vec, row], out_specs=[row, row, vec],
                          out_shape=[jax.ShapeDtypeStruct((s, GDN_W), f32), jax.ShapeDtypeStruct((s, GDN_W), f32),
                                     jax.ShapeDtypeStruct((1, 128), f32)],
                          compiler_params=_params("arbitrary"))(o, gdn_in, nw, dy)


def _dot_high(a, b, dims=NN):
    return lax.dot_general(a, b, (dims, ((), ())), precision=lax.Precision.HIGH, preferred_element_type=f32)


@jax.custom_vjp
def _unit_lower_inverses(a):
    c = a[0].shape[0]
    n = range(len(a))
    eye = (lax.broadcasted_iota(jnp.int32, (c, c), 0) == lax.broadcasted_iota(jnp.int32, (c, c), 1)).astype(f32)
    inv = [eye - a[i] for i in n]
    pw = [_dot_high(a[i], a[i]) for i in n]
    width = 2
    while width < c:
        inv = [inv[i] + _dot_high(inv[i], pw[i]) for i in n]
        width *= 2
        if width < c:
            pw = [_dot_high(pw[i], pw[i]) for i in n]
    return inv


def _unit_lower_inverses_fwd(a):
    inv = _unit_lower_inverses(a)
    return inv, inv


def _unit_lower_inverses_bwd(inv, g):
    return ([-_dot_high(_dot_high(x, gx, TN), x, NT) for x, gx in zip(inv, g)],)


_unit_lower_inverses.defvjp(_unit_lower_inverses_fwd, _unit_lower_inverses_bwd)


@jax.custom_vjp
def _known_inverses(a, inv):
    return inv


_known_inverses.defvjp(lambda a, inv: (inv, inv),
                       lambda inv, g: (_unit_lower_inverses_bwd(inv, g)[0], [jnp.zeros_like(x) for x in inv]))


def _gdn_chunks(q, k, v, gb, state, kept_inverses=None):
    c = GDN_CHUNK
    heads = range(len(q))
    r = lax.broadcasted_iota(jnp.int32, (c, c), 0)
    cc = lax.broadcasted_iota(jnp.int32, (c, c), 1)
    tril, strict = r >= cc, r > cc
    eye = (r == cc).astype(f32)

    def dhi(a, b):
        return jnp.dot(a, b, precision=lax.Precision.HIGH, preferred_element_type=f32)

    beta = [gb[:, h:h + 1] for h in heads]
    cum_cols = dhi(tril.astype(f32), gb)
    cum_rows = dhi(gb.T, (r <= cc).astype(f32))
    gi = [jnp.broadcast_to(cum_cols[:, GDN_HEADS + h:GDN_HEADS + h + 1], (c, c)) for h in heads]
    gj = [jnp.broadcast_to(cum_rows[GDN_HEADS + h:GDN_HEADS + h + 1, :], (c, c)) for h in heads]
    decay = [jnp.where(tril, jnp.exp(jnp.where(tril, gi[h] - gj[h], 0.0)), 0.0) for h in heads]
    kb = [k[h] * beta[h] for h in heads]
    vb = [v[h] * beta[h] for h in heads]
    qs = [q[h] * (GDN_HEAD_DIM ** -0.5) for h in heads]
    qk = [_d16(jnp.concatenate([qs[h], kb[h]], axis=0), k[h], NT) for h in heads]
    a = [jnp.where(strict, qk[h][c:] * decay[h], 0.0) for h in heads]
    tinv = _unit_lower_inverses(a) if kept_inverses is None else _known_inverses(a, kept_inverses)
    gc = [gi[h][:, 0:1] for h in heads]
    egc = [jnp.exp(gc[h]) for h in heads]
    uw = [dhi(tinv[h], jnp.concatenate([vb[h], kb[h] * egc[h]], axis=1)) for h in heads]
    u = [uw[h][:, :GDN_HEAD_DIM] for h in heads]
    w = [uw[h][:, GDN_HEAD_DIM:] for h in heads]
    attn = [jnp.where(tril, qk[h][:c] * decay[h], 0.0) for h in heads]
    g_last = [gi[h][c - 1:c, 0:1] for h in heads]
    on_state = [_d16(jnp.concatenate([w[h], qs[h] * egc[h]], axis=0), state[h], NN) for h in heads]
    v_new = [u[h] - on_state[h][:c] for h in heads]
    out = [on_state[h][c:] + _d16(attn[h], v_new[h], NN) for h in heads]
    new_state = [state[h] * jnp.exp(g_last[h]) + _d16(k[h] * jnp.exp(g_last[h] - gc[h]), v_new[h], TN) for h in heads]
    return out, new_state, tinv


def _head_cols(ref):
    return [ref[:, h * GDN_HEAD_DIM:(h + 1) * GDN_HEAD_DIM] for h in range(GDN_HEADS)]


def gdn_scan_fwd(qn, kn, v, gbeta, *, name, side=None):
    s = qn.shape[0]
    nc = s // GDN_CHUNK

    def body(q_ref, k_ref, v_ref, gb_ref, o_ref, st_ref, inv_ref, state_ref):
        @pl.when(pl.program_id(0) == 0)
        def _():
            state_ref[...] = jnp.zeros_like(state_ref)

        states = [state_ref[h] for h in range(GDN_HEADS)]
        outs, new, inverses = _gdn_chunks(_head_cols(q_ref), _head_cols(k_ref), _head_cols(v_ref), gb_ref[...], states)
        for h in range(GDN_HEADS):
            st_ref[0, h] = states[h]
            inv_ref[0, h] = inverses[h]
            o_ref[:, h * GDN_HEAD_DIM:(h + 1) * GDN_HEAD_DIM] = outs[h]
            state_ref[h] = new[h]

    row = pl.BlockSpec((GDN_CHUNK, GDN_W), lambda n: (n, 0))
    return _call(
        body, (qn, kn, v, gbeta), name=name, grid=(nc,),
        in_specs=[row, row, row, pl.BlockSpec((GDN_CHUNK, 128), lambda n: (n, 0))],
        out_specs=[row, pl.BlockSpec((1, GDN_HEADS, GDN_HEAD_DIM, GDN_HEAD_DIM), lambda n: (n, 0, 0, 0)),
                   pl.BlockSpec((1, GDN_HEADS, GDN_CHUNK, GDN_CHUNK), lambda n: (n, 0, 0, 0))],
        out_shape=[jax.ShapeDtypeStruct((s, GDN_W), f32),
                   jax.ShapeDtypeStruct((nc, GDN_HEADS, GDN_HEAD_DIM, GDN_HEAD_DIM), f32),
                   jax.ShapeDtypeStruct((nc, GDN_HEADS, GDN_CHUNK, GDN_CHUNK), f32)],
        scratch_shapes=[pltpu.VMEM((GDN_HEADS, GDN_HEAD_DIM, GDN_HEAD_DIM), f32)], semantics=("arbitrary",), side=side)


def gdn_scan_bwd(qn, kn, v, gbeta, states, inverses, dout, *, name, side=None):
    s = qn.shape[0]
    nc = s // GDN_CHUNK

    def body(q_ref, k_ref, v_ref, gb_ref, st_ref, inv_ref, do_ref, dq_ref, dk_ref, dv_ref, dgb_ref, dstate_ref):
        @pl.when(pl.program_id(0) == 0)
        def _():
            dstate_ref[...] = jnp.zeros_like(dstate_ref)

        kept = [inv_ref[0, h] for h in range(GDN_HEADS)]
        _, pull = jax.vjp(lambda *args: _gdn_chunks(*args, kept_inverses=kept)[:2],
                          _head_cols(q_ref), _head_cols(k_ref), _head_cols(v_ref), gb_ref[...],
                          [st_ref[0, h] for h in range(GDN_HEADS)])
        dq, dk, dv, dgb, dst = pull((_head_cols(do_ref), [dstate_ref[h] for h in range(GDN_HEADS)]))
        for h in range(GDN_HEADS):
            sl = slice(h * GDN_HEAD_DIM, (h + 1) * GDN_HEAD_DIM)
            dq_ref[:, sl] = dq[h]
            dk_ref[:, sl] = dk[h]
            dv_ref[:, sl] = dv[h]
            dstate_ref[h] = dst[h]
        dgb_ref[...] = dgb

    row = pl.BlockSpec((GDN_CHUNK, GDN_W), lambda n: (nc - 1 - n, 0))
    gb = pl.BlockSpec((GDN_CHUNK, 128), lambda n: (nc - 1 - n, 0))
    return _call(
        body, (qn, kn, v, gbeta, states, inverses, dout), name=name, grid=(nc,),
        in_specs=[row, row, row, gb, pl.BlockSpec((1, GDN_HEADS, GDN_HEAD_DIM, GDN_HEAD_DIM), lambda n: (nc - 1 - n, 0, 0, 0)),
                  pl.BlockSpec((1, GDN_HEADS, GDN_CHUNK, GDN_CHUNK), lambda n: (nc - 1 - n, 0, 0, 0)), row],
        out_specs=[row, row, row, gb],
        out_shape=[jax.ShapeDtypeStruct((s, GDN_W), f32)] * 3 + [jax.ShapeDtypeStruct((s, 128), f32)],
        scratch_shapes=[pltpu.VMEM((GDN_HEADS, GDN_HEAD_DIM, GDN_HEAD_DIM), f32)], semantics=("arbitrary",), side=side)


def _adamw_update(w, g, m, v):
    nm = ADAM_B1 * m + (1.0 - ADAM_B1) * g
    nv = ADAM_B2 * v + (1.0 - ADAM_B2) * (g * g)
    m_hat = nm / (1.0 - ADAM_B1 ** ADAM_STEP)
    v_hat = nv / (1.0 - ADAM_B2 ** ADAM_STEP)
    return -ADAM_LR * (m_hat / (jnp.sqrt(v_hat) + ADAM_EPS) + ADAM_WD * w), nm, nv


def adamw(w, g, m, v, *, name):
    r, c = w.shape
    tr = _tile(r, 256, 8)

    def body(w_ref, g_ref, m_ref, v_ref, d_ref, nm_ref, nv_ref):
        d_ref[...], nm_ref[...], nv_ref[...] = _adamw_update(w_ref[...], g_ref[...], m_ref[...], v_ref[...])

    spec = pl.BlockSpec((tr, c), lambda i: (i, 0))
    return pl.pallas_call(body, name=name, grid=(r // tr,), in_specs=[spec] * 4, out_specs=[spec] * 3,
                          out_shape=[jax.ShapeDtypeStruct((r, c), f32)] * 3, compiler_params=_params("parallel"))(w, g, m, v)


def adamw_of_partial_sums(w, parts, m, v, *, name):
    r, c = w.shape
    cp = parts.shape[2]
    tr = _tile(r, 256, 16)

    def body(w_ref, p_ref, m_ref, v_ref, g_ref, d_ref, nm_ref, nv_ref):
        part = [p_ref[k, :, :c].astype(f32) for k in range(4)]
        g = ((part[3] + part[0]) + part[1]) + part[2]
        g_ref[...] = g
        d_ref[...], nm_ref[...], nv_ref[...] = _adamw_update(w_ref[...], g, m_ref[...], v_ref[...])

    spec = pl.BlockSpec((tr, c), lambda i: (i, 0))
    return pl.pallas_call(body, name=name, grid=(r // tr,),
                          in_specs=[spec, pl.BlockSpec((4, tr, cp), lambda i: (0, i, 0)), spec, spec], out_specs=[spec] * 4,
                          out_shape=[jax.ShapeDtypeStruct((r, c), f32)] * 4,
                          compiler_params=_params("parallel"))(w, parts, m, v)


def _pos():
    return lax.axis_index("x"), lax.axis_index("y"), lax.axis_index("c")


ANY = pl.BlockSpec(memory_space=pl.ANY)


class Side(NamedTuple):
    ins: list
    outs: list
    aliases: dict
    sems: list
    start: Callable
    wait: Callable


def join_sides(*sides):
    def spans(key):
        out, off = [], 0
        for sd in sides:
            out.append(slice(off, off + len(getattr(sd, key))))
            off += len(getattr(sd, key))
        return out

    i_sp, o_sp, s_sp = spans("ins"), spans("outs"), spans("sems")
    aliases = {i_sp[n].start + i: o_sp[n].start + o for n, sd in enumerate(sides) for i, o in sd.aliases.items()}

    def each(what):
        def run(ins, outs, sems):
            for n, sd in enumerate(sides):
                getattr(sd, what)(ins[i_sp[n]], outs[o_sp[n]], sems[s_sp[n]])
        return run

    return Side([a for sd in sides for a in sd.ins], [o for sd in sides for o in sd.outs], aliases,
                [s for sd in sides for s in sd.sems], each("start"), each("wait"))


def _side_body(body, side, n_in, n_out, n_scratch, grid):
    ns_in, ns_out = len(side.ins), len(side.outs)

    def wrapped(*refs):
        cut = [n_in, ns_in, n_out, ns_out, n_scratch]
        parts, off = [], 0
        for c in cut:
            parts.append(refs[off:off + c])
            off += c
        ins, s_ins, outs, s_outs, scratch = parts
        sems = refs[off:]
        if grid:
            ids = [pl.program_id(d) for d in range(len(grid))]
            first = functools.reduce(jnp.logical_and, [i == 0 for i in ids])
            last = functools.reduce(jnp.logical_and, [i == g - 1 for i, g in zip(ids, grid)])
            pl.when(first)(lambda: side.start(s_ins, s_outs, sems))
            body(*ins, *outs, *scratch)
            pl.when(last)(lambda: side.wait(s_ins, s_outs, sems))
        else:
            side.start(s_ins, s_outs, sems)
            side.wait(s_ins, s_outs, sems)

    return wrapped


def _call(body, args, *, name, grid, in_specs, out_specs, out_shape, semantics, scratch_shapes=(), side=None):
    if side is None:
        return pl.pallas_call(body, name=name, grid=grid, in_specs=in_specs, out_specs=out_specs, out_shape=out_shape,
                              scratch_shapes=list(scratch_shapes), compiler_params=_params(*semantics))(*args)
    single = not isinstance(out_shape, (list, tuple))
    shapes, specs = ([out_shape], [out_specs]) if single else (list(out_shape), list(out_specs))
    n_in, n_out = len(in_specs), len(shapes)
    res = pl.pallas_call(
        _side_body(body, side, n_in, n_out, len(scratch_shapes), grid), name=name, grid=grid,
        in_specs=list(in_specs) + [ANY] * len(side.ins), out_specs=specs + [ANY] * len(side.outs),
        out_shape=shapes + list(side.outs), scratch_shapes=list(scratch_shapes) + list(side.sems),
        input_output_aliases={n_in + i: n_out + o for i, o in side.aliases.items()},
        compiler_params=_params(*(["arbitrary"] * len(grid))),
    )(*args, *side.ins)
    return (res[0] if single else res[:n_out]), list(res[n_out:])


def run_side(side, *, name):
    return pl.pallas_call(_side_body(None, side, 0, 0, 0, ()), name=name, in_specs=[ANY] * len(side.ins),
                          out_specs=[ANY] * len(side.outs), out_shape=list(side.outs), scratch_shapes=list(side.sems),
                          input_output_aliases=dict(side.aliases))(*side.ins)


def _remote(src, dst, send, recv, k, to):
    return pltpu.make_async_remote_copy(src_ref=src, dst_ref=dst, send_sem=send.at[k], recv_sem=recv.at[k], device_id=to,
                                        device_id_type=MESH)


def gather_first(shards):
    na = len(shards)

    def copies(x_refs, out_refs, sems):
        send, recv, local = sems
        x, y, cc = _pos()
        me = 4 * x + 2 * y + cc
        peers = [(x, y, 1 - cc), (1 - x, y, cc), (x, 1 - y, cc), (1 - x, 1 - y, cc)]
        mine = [pltpu.make_async_copy(x_refs[a], out_refs[a].at[me], local.at[a]) for a in range(na)]
        sent = [_remote(x_refs[a], out_refs[a].at[me], send, recv, 4 * a + k, p) for a in range(na) for k, p in enumerate(peers)]
        landed = [_remote(x_refs[a], out_refs[a].at[4 * p[0] + 2 * p[1] + p[2]], send, recv, 4 * a + k, p)
                  for a in range(na) for k, p in enumerate(peers)]
        return mine, sent, landed

    def start(x_refs, out_refs, sems):
        mine, sent, _ = copies(x_refs, out_refs, sems)
        for cp in mine + sent:
            cp.start()

    def wait(x_refs, out_refs, sems):
        mine, sent, landed = copies(x_refs, out_refs, sems)
        for cp in sent:
            cp.wait_send()
        for cp in landed:
            cp.wait_recv()
        for cp in mine:
            cp.wait()

    return Side(list(shards), [jax.ShapeDtypeStruct((N_DEV,) + s.shape, s.dtype) for s in shards], {},
                [pltpu.SemaphoreType.DMA((4 * na,)), pltpu.SemaphoreType.DMA((4 * na,)), pltpu.SemaphoreType.DMA((na,))],
                start, wait)


def gather_second(slots):
    na = len(slots)

    def copies(out_refs, sems):
        send, recv = sems
        x, y, cc = _pos()
        chips = [(1 - x, y), (x, 1 - y), (1 - x, 1 - y)]
        sent, landed = [], []
        for a in range(na):
            for j, (px, py) in enumerate(chips):
                row = out_refs[a].at[4 * px + 2 * py + cc]
                sent.append(_remote(row, row, send, recv, 3 * a + j, (x, y, 1 - cc)))
                landed.append(_remote(row, out_refs[a].at[4 * px + 2 * py + 1 - cc], send, recv, 3 * a + j, (x, y, 1 - cc)))
        return sent, landed

    def start(_, out_refs, sems):
        for cp in copies(out_refs, sems)[0]:
            cp.start()

    def wait(_, out_refs, sems):
        sent, landed = copies(out_refs, sems)
        for cp in sent:
            cp.wait_send()
        for cp in landed:
            cp.wait_recv()

    return Side(list(slots), [jax.ShapeDtypeStruct(s.shape, s.dtype) for s in slots], {a: a for a in range(na)},
                [pltpu.SemaphoreType.DMA((3 * na,)), pltpu.SemaphoreType.DMA((3 * na,))], start, wait)


def _rows_of(ref, lead, rows):
    if rows is None:
        return ref if lead is None else ref.at[lead]
    cut = pl.ds(rows[0], rows[1])
    return ref.at[:, cut] if lead is None else ref.at[lead, cut]


def _side_into(arrays, out_shapes, into):
    na = len(arrays)
    if into is None:
        return list(arrays), out_shapes, {}
    return list(arrays) + list(into), out_shapes, {na + a: a for a in range(na)}


def grad_to_sibling(chunks, rows=None, into=None):
    na = len(chunks)

    def start(g_refs, out_refs, sems):
        send, recv = sems
        x, y, cc = _pos()
        for a in range(na):
            for q in range(4):
                _remote(_rows_of(g_refs[a], 2 * q + 1 - cc, rows), _rows_of(out_refs[a], q, rows), send, recv, a,
                        (x, y, 1 - cc)).start()

    def wait(g_refs, out_refs, sems):
        send, recv = sems
        x, y, cc = _pos()
        for a in range(na):
            whole = _rows_of(out_refs[a], None, rows)
            _remote(whole, whole, send, recv, a, (x, y, 1 - cc)).wait()

    ins, outs, aliases = _side_into(chunks, [jax.ShapeDtypeStruct((4,) + g.shape[1:], g.dtype) for g in chunks], into)
    return Side(ins, outs, aliases, [pltpu.SemaphoreType.DMA((na,)), pltpu.SemaphoreType.DMA((na,))], start, wait)


def grad_to_chips(parts, rows=None, into=None):
    na = len(parts)

    def copies(p_refs, out_refs, sems):
        send, recv, local = sems
        x, y, cc = _pos()
        chips = [(1 - x, y), (x, 1 - y), (1 - x, 1 - y)]
        mine = [pltpu.make_async_copy(_rows_of(p_refs[a], 2 * x + y, rows), _rows_of(out_refs[a], 3, rows), local.at[a])
                for a in range(na)]
        sent = [_remote(_rows_of(p_refs[a], 2 * px + py, rows), _rows_of(out_refs[a], k, rows), send, recv, 3 * a + k,
                        (px, py, cc)) for a in range(na) for k, (px, py) in enumerate(chips)]
        return mine, sent

    def start(p_refs, out_refs, sems):
        mine, sent = copies(p_refs, out_refs, sems)
        for cp in mine + sent:
            cp.start()

    def wait(p_refs, out_refs, sems):
        mine, sent = copies(p_refs, out_refs, sems)
        for cp in sent:
            cp.wait()
        for cp in mine:
            cp.wait()

    ins, outs, aliases = _side_into(parts, [jax.ShapeDtypeStruct(p.shape, p.dtype) for p in parts], into)
    return Side(ins, outs, aliases,
                [pltpu.SemaphoreType.DMA((3 * na,)), pltpu.SemaphoreType.DMA((3 * na,)), pltpu.SemaphoreType.DMA((na,))],
                start, wait)


def add_sibling(chunks, recv, *, name):
    _, r, c = chunks.shape
    tr = r if r <= 1024 else _tile(r, 512, 16)
    core = lax.axis_index("c").astype(jnp.int32).reshape(1)

    def body(core_ref, a_ref, b_ref, o_ref):
        o_ref[...] = (a_ref[...] + b_ref[...]).astype(bf16)

    return pl.pallas_call(
        body, name=name,
        grid_spec=pltpu.PrefetchScalarGridSpec(
            num_scalar_prefetch=1, grid=(4, r // tr),
            in_specs=[pl.BlockSpec((1, tr, c), lambda q, i, core_ref: (2 * q + core_ref[0], i, 0)),
                      pl.BlockSpec((1, tr, c), lambda q, i, core_ref: (q, i, 0))],
            out_specs=pl.BlockSpec((1, tr, c), lambda q, i, core_ref: (q, i, 0))),
        out_shape=jax.ShapeDtypeStruct((4, r, c), bf16), compiler_params=_params("parallel", "parallel"),
    )(core, chunks, recv)


def all_reduce_small(vec, *, name):
    r, c = vec.shape

    def body(v_ref, out_ref, buf_ref, send_sems, recv_sems):
        x, y, cc = _pos()
        my_id = 4 * x + 2 * y + cc
        buf_ref[my_id] = v_ref[...]
        flips = [(fx, fy, fc) for fx in (0, 1) for fy in (0, 1) for fc in (0, 1)][1:]
        cps = []
        for k, (fx, fy, fc) in enumerate(flips):
            peer = ((1 - x) if fx else x, (1 - y) if fy else y, (1 - cc) if fc else cc)
            cps.append(pltpu.make_async_remote_copy(src_ref=v_ref, dst_ref=buf_ref.at[my_id], send_sem=send_sems.at[k],
                                                    recv_sem=recv_sems.at[k], device_id=peer, device_id_type=MESH))
        for cp in cps:
            cp.start()
        for cp in cps:
            cp.wait()
        acc = buf_ref[0]
        for d in range(1, N_DEV):
            acc = acc + buf_ref[d]
        out_ref[...] = acc

    vm = pl.BlockSpec(memory_space=pltpu.VMEM)
    return pl.pallas_call(body, name=name, in_specs=[vm], out_specs=vm, out_shape=jax.ShapeDtypeStruct((r, c), f32),
                          scratch_shapes=[pltpu.VMEM((N_DEV, r, c), f32), pltpu.SemaphoreType.DMA((7,)),
                                          pltpu.SemaphoreType.DMA((7,))])(vec)


def _pack(parts, rows, dtype):
    flat = jnp.concatenate([p.reshape(-1).astype(dtype) for p in parts])
    return jnp.pad(flat, (0, rows * PACK_COLS - flat.shape[0])).reshape(rows, PACK_COLS)


def _unpack(flat, shapes):
    out, off = [], 0
    for shp in shapes:
        n = shp[0] * shp[1]
        out.append(flat[..., off:off + n].reshape(flat.shape[:-1] + tuple(shp)))
        off += n
    return out


def _from_column_shards(g):
    _, r, c = g.shape
    return jnp.transpose(g, (1, 0, 2)).reshape(r, N_DEV * c)


def _column_shards(full):
    r, c8 = full.shape
    return jnp.transpose(full.reshape(r, N_DEV, c8 // N_DEV), (1, 0, 2))


W_IN_SHARD = IN_DIM // N_DEV
W_IN_PAD = 1280
W_IN_PARTS = (("swa", 0, 0, 1280), ("swa", 1280, 5376, 5392), ("gdn", 0, 1280, 5376), ("gates", 0, 5392, IN_DIM))
W_IN_WIDTHS = {"swa": SWA_IN_W, "gdn": 4 * GDN_W, "gates": 2 * D_MODEL}


def _w_in_segments():
    segs = []
    for part, p0, g0, g1 in W_IN_PARTS:
        for j in range(N_DEV):
            lo, hi = max(g0, W_IN_SHARD * j), min(g1, W_IN_SHARD * (j + 1))
            if lo < hi:
                segs.append((part, p0 + lo - g0, j, lo - W_IN_SHARD * j, hi - lo))
    return segs


def split_w_in(shards, *, name):
    dt = shards.dtype
    tm = 256

    def body(w_ref, swa_ref, gdn_ref, gates_ref):
        out = {"swa": swa_ref, "gdn": gdn_ref, "gates": gates_ref}
        swa_ref[:, SWA_Q + 2 * SWA_KV + 2 * GDN_HEADS:] = jnp.zeros((tm, SWA_IN_W - SWA_Q - 2 * SWA_KV - 2 * GDN_HEADS), dt)
        for part, p0, j, l0, n in _w_in_segments():
            out[part][:, p0:p0 + n] = w_ref[j, :, l0:l0 + n]

    return pl.pallas_call(body, name=name, grid=(D_MODEL // tm,),
                          in_specs=[pl.BlockSpec((N_DEV, tm, W_IN_PAD), lambda i: (0, i, 0))],
                          out_specs=[pl.BlockSpec((tm, W_IN_WIDTHS[p]), lambda i: (i, 0)) for p in ("swa", "gdn", "gates")],
                          out_shape=[jax.ShapeDtypeStruct((D_MODEL, W_IN_WIDTHS[p]), dt) for p in ("swa", "gdn", "gates")],
                          compiler_params=_params("parallel"))(shards)


def merge_w_in_grad(d_swa, d_gdn, d_gates, *, name):
    tm = 256

    def body(swa_ref, gdn_ref, gates_ref, w_ref):
        src = {"swa": swa_ref, "gdn": gdn_ref, "gates": gates_ref}
        w_ref[:, :, W_IN_SHARD:] = jnp.zeros((N_DEV, tm, W_IN_PAD - W_IN_SHARD), f32)
        for part, p0, j, l0, n in _w_in_segments():
            w_ref[j, :, l0:l0 + n] = src[part][:, p0:p0 + n]

    return pl.pallas_call(body, name=name, grid=(D_MODEL // tm,),
                          in_specs=[pl.BlockSpec((tm, W_IN_WIDTHS[p]), lambda i: (i, 0)) for p in ("swa", "gdn", "gates")],
                          out_specs=pl.BlockSpec((N_DEV, tm, W_IN_PAD), lambda i: (0, i, 0)),
                          out_shape=jax.ShapeDtypeStruct((N_DEV, D_MODEL, W_IN_PAD), f32),
                          compiler_params=_params("parallel"))(d_swa, d_gdn, d_gates)


def kernel(x, mem, w_in, rel_bias, swa_sinks, gdn_conv_w, gdn_a_log, gdn_dt_bias, gdn_norm_w, w_br_swa, w_br_gdn, w_mix_o, ln1_g, ln1_b, w_mem_q, w_mem_kv, w_mem_o, ln2_g, ln2_b, w_up, ffn_conv_w, ffn_conv_b, w_down, ln3_g, ln3_b, loss_target, m_w_in, m_rel_bias, m_swa_sinks, m_gdn_conv_w, m_gdn_a_log, m_gdn_dt_bias, m_gdn_norm_w, m_w_br_swa, m_w_br_gdn, m_w_mix_o, m_ln1_g, m_ln1_b, m_w_mem_q, m_w_mem_kv, m_w_mem_o, m_ln2_g, m_ln2_b, m_w_up, m_ffn_conv_w, m_ffn_conv_b, m_w_down, m_ln3_g, m_ln3_b, v_w_in, v_rel_bias, v_swa_sinks, v_gdn_conv_w, v_gdn_a_log, v_gdn_dt_bias, v_gdn_norm_w, v_w_br_swa, v_w_br_gdn, v_w_mix_o, v_ln1_g, v_ln1_b, v_w_mem_q, v_w_mem_kv, v_w_mem_o, v_ln2_g, v_ln2_b, v_w_up, v_ffn_conv_w, v_ffn_conv_b, v_w_down, v_ln3_g, v_ln3_b):
    env = dict(locals())
    w2 = {n: (env[n][0] if env[n].ndim == 3 else env[n]) for n in WEIGHTS}
    m2 = {n: (env["m_" + n][0] if env["m_" + n].ndim == 3 else env["m_" + n]) for n in WEIGHTS}
    v2 = {n: (env["v_" + n][0] if env["v_" + n].ndim == 3 else env["v_" + n]) for n in WEIGHTS}
    xs, mems, target = x[0], mem[0], loss_target[0]
    my_id = 4 * lax.axis_index("x") + 2 * lax.axis_index("y") + lax.axis_index("c")
    pad_ff = FF_PAD - FF_SHARD

    pad_cols = {"w_in": W_IN_PAD - W_IN_SHARD, "w_up": pad_ff}
    mid = ("w_br_swa", "w_br_gdn", "w_mem_o", "w_mix_o", "w_mem_q", "w_mem_kv")
    mine = {n: jnp.pad(w2[n], ((0, 0), (0, pad_cols.get(n, 0)))).astype(bf16) for n in ("w_in", "w_up", "w_down") + mid}
    xb, got_in = cast_bf16(xs, name="cast_x", side=gather_first([mine["w_in"]]))
    got_in = run_side(gather_second(got_in), name="gather_w_in_pass_on")
    w_swa, w_gdn, w_gates = split_w_in(got_in[0], name="split_w_in")
    n_ffn, n_gdn = 3 * FF_SHARD, GDN_CONV * (QKV_W // N_DEV)
    conv_mine = jnp.concatenate([w2["ffn_conv_w"].reshape(-1), w2["gdn_conv_w"].reshape(-1)])[None]
    conv_rows = lax.dynamic_update_slice(jnp.zeros((N_DEV, n_ffn + n_gdn), f32), conv_mine, (my_id, 0))
    conv_all = all_reduce_small(_pack([conv_rows], CONV_ROWS, f32), name="gather_conv_w")
    conv_all = conv_all.reshape(-1)[:N_DEV * (n_ffn + n_gdn)].reshape(N_DEV, n_ffn + n_gdn)
    cwb = jnp.concatenate([conv_all[:, :n_ffn].reshape(N_DEV, 3, FF_SHARD), w2["ffn_conv_b"].reshape(N_DEV, 1, FF_SHARD),
                           jnp.zeros((N_DEV, 4, FF_SHARD), f32)], axis=1)
    cwb = jnp.pad(cwb, ((0, 0), (0, 0), (0, pad_ff)))
    convw = jnp.transpose(conv_all[:, n_ffn:].reshape(N_DEV, GDN_CONV, QKV_W // N_DEV), (1, 0, 2)).reshape(GDN_CONV, QKV_W)
    convw = jnp.pad(convw, ((0, 4), (0, 0)))
    onehot = _bucket_onehot()
    bias = mm(w2["rel_bias"].T, onehot, "nn", hi=True, tn=4096, name="rel_bias_table").reshape(SWA_HEADS, BLOCK, 2 * BLOCK)
    alog_row = jnp.pad(w2["gdn_a_log"], ((0, 0), (GDN_HEADS, 128 - 2 * GDN_HEADS)))
    dt_row = jnp.pad(w2["gdn_dt_bias"], ((0, 0), (GDN_HEADS, 128 - 2 * GDN_HEADS)))

    memb = cast_bf16(mems, name="cast_mem")
    gates, mid_got = mm(xb, w_gates, "nn", out_dtype=bf16, name="proj_gates", side=gather_first([mine[n] for n in mid]))
    gdn_in, mid_got = mm(xb, w_gdn, "nn", name="proj_gdn", side=gather_second(mid_got))
    got = dict(zip(mid, mid_got))
    w_br_swa, w_br_gdn, w_mem_o = (_from_column_shards(got[n]) for n in ("w_br_swa", "w_br_gdn", "w_mem_o"))
    w_mix_o = got["w_mix_o"].reshape(D_MODEL, D_MODEL)
    w_mem_q = got["w_mem_q"].reshape(D_MODEL, MEM_W)
    w_mem_kv = got["w_mem_kv"].reshape(D_MODEL, 2 * MEM_W)
    swa_in = mm(xb, w_swa, "nn", tn=SWA_IN_W, name="proj_swa")
    attn, down_got = swa_fwd(swa_in, bias, w2["swa_sinks"], name="swa_fwd", side=gather_first([mine["w_down"]]))
    qn, kn, vv, gdn_conv = gdn_pre_fwd(gdn_in, convw, name="gdn_pre_fwd")
    gbeta = gbeta_fwd(swa_in, alog_row, dt_row, name="gbeta_fwd")
    (o_gdn, states, inverses), up_got = gdn_scan_fwd(qn, kn, vv, gbeta, name="gdn_scan_fwd",
                                                     side=gather_first([mine["w_up"]]))
    ygd = gdn_post_fwd(o_gdn, gdn_in, w2["gdn_norm_w"], name="gdn_post_fwd")
    y_swa, down_got = mm(attn, w_br_swa, "nn", out_dtype=bf16, name="br_swa", side=gather_second(down_got))
    y_gdn, up_got = mm(ygd, w_br_gdn, "nn", out_dtype=bf16, name="br_gdn", side=gather_second(up_got))
    w_up_blocked = up_got[0]
    w_down_p = jnp.pad(down_got[0].reshape(4, FF_SHARD, D_MODEL), ((0, 0), (0, pad_ff), (0, 0))).reshape(4 * FF_PAD, D_MODEL)
    mixed = merge_fwd(gates, y_swa, y_gdn, name="merge_fwd")
    z1 = mm(mixed, w_mix_o, "nn", add=xs, add_scale=ALPHA, name="mix_o")
    x1, x1b = ln_fwd(z1, w2["ln1_g"], w2["ln1_b"], name="ln1_fwd")
    qm = mm(x1b, w_mem_q, "nn", name="mem_q")
    kv = mm(memb, w_mem_kv, "nn", name="mem_kv")
    om = memattn_fwd(qm, kv, name="memattn_fwd")
    z2 = mm(om, w_mem_o, "nn", add=x1, add_scale=ALPHA, name="mem_o")
    x2, x2b = ln_fwd(z2, w2["ln2_g"], w2["ln2_b"], name="ln2_fwd")
    hpre = mm(x2b, w_up_blocked, "nn", b_blocked=True, out_dtype=bf16, name="ffn_up")
    act, conv_g, conv_u = ffn_act_fwd(hpre, cwb, name="ffn_act_fwd")
    z3 = mm(act, w_down_p, "nn", add=x2, add_scale=ALPHA, tk=2 * FF_PAD, name="ffn_down")
    dz3, dz3b, d_ln3g, d_ln3b, loss = ln_loss(z3, target, w2["ln3_g"], w2["ln3_b"], name="ln3_loss")

    dact = mm(dz3b, w_down_p, "nt", tn=FF_PAD, out_dtype=bf16, name="d_act")
    d_wdown_p = mm(act, dz3b, "tn", tm=FF_PAD, name="dw_down")
    d_hpre, d_cwb = ffn_act_bwd(hpre, conv_g, conv_u, dact, cwb, name="ffn_act_bwd")
    def sibling_sums(names, chunks, received):
        return [add_sibling(c, r, name="grad_add_sibling_" + n) for n, c, r in zip(names, chunks, received)]

    dx2 = mm(d_hpre, w_up_blocked, "nt", b_blocked=True, k_shards=2, add=dz3, add_scale=ALPHA, name="d_x2")
    d_wup = mm(x2b, d_hpre, "tn", out_blocked=True, name="dw_up")
    ffn = ("w_up", "w_down")
    ffn_chunks = [d_wup, d_wdown_p.reshape(4, FF_PAD, D_MODEL)[:, :FF_SHARD].reshape(N_DEV, FF_SHARD // 2, D_MODEL)]
    dz2, dz2b, d_ln2g, d_ln2b = ln_bwd(dx2, z2, w2["ln2_g"], name="ln2_bwd")
    d_om, down_received = mm(dz2b, w_mem_o, "nt", name="d_om", side=grad_to_sibling(ffn_chunks[1:]))
    d_wmemo = mm(om, dz2b, "tn", name="dw_mem_o")
    dqm, dkv = memattn_bwd(qm, kv, d_om, name="memattn_bwd")
    dx1 = mm(dqm, w_mem_q, "nt", add=dz2, add_scale=ALPHA, name="d_x1")
    d_wmemq = mm(x1b, dqm, "tn", name="dw_mem_q")
    d_wmemkv = mm(memb, dkv, "tn", name="dw_mem_kv")
    dz1, dz1b, d_ln1g, d_ln1b = ln_bwd(dx1, z1, w2["ln1_g"], name="ln1_bwd")
    half = D_MODEL // 2
    dmix, up_received = mm(dz1b, w_mix_o, "nt", name="d_mixed", side=grad_to_sibling(ffn_chunks[:1], rows=(0, half)))
    d_wmixo, up_received = mm(mixed, dz1b, "tn", tk=4096, name="dw_mix_o",
                              side=grad_to_sibling(ffn_chunks[:1], rows=(half, half), into=up_received))
    ffn_sums = sibling_sums(ffn, ffn_chunks, up_received + down_received)
    dys, dyg, d_gates = merge_bwd(gates, y_swa, y_gdn, dmix, name="merge_bwd")
    d_attn = mm(dys, w_br_swa, "nt", name="d_attn")
    d_wbrswa = mm(attn, dys, "tn", tk=4096, name="dw_br_swa")
    d_ygd = mm(dyg, w_br_gdn, "nt", name="d_ygd")
    d_wbrgdn = mm(ygd, dyg, "tn", tk=4096, name="dw_br_gdn")
    mid_chunks = [_column_shards(d_wbrswa), _column_shards(d_wbrgdn), _column_shards(d_wmemo),
                  d_wmixo.reshape(N_DEV, D_MODEL // N_DEV, D_MODEL), d_wmemq.reshape(N_DEV, D_MODEL // N_DEV, MEM_W),
                  d_wmemkv.reshape(N_DEV, D_MODEL // N_DEV, 2 * MEM_W)]
    d_o, d_gz, d_normw = gdn_post_bwd(o_gdn, gdn_in, w2["gdn_norm_w"], d_ygd, name="gdn_post_bwd")
    (dqn, dkn, dvv, dgbeta), received = gdn_scan_bwd(
        qn, kn, vv, gbeta, states, inverses, d_o, name="gdn_scan_bwd",
        side=join_sides(grad_to_chips(ffn_sums), grad_to_sibling(mid_chunks)))
    chip_parts = dict(zip(ffn, received[:2]))
    mid_sums = sibling_sums(mid, mid_chunks, received[2:])
    d_gdn_in, d_convw = gdn_pre_bwd(gdn_in, gdn_conv, convw, dqn, dkn, dvv, d_gz, name="gdn_pre_bwd")
    d_ba, d_alog, d_dt = gbeta_bwd(swa_in, alog_row, dt_row, dgbeta, name="gbeta_bwd")
    (dq, dkc, dkp, dvc, dvp, d_bias, d_sinks), received = swa_bwd(swa_in, bias, w2["swa_sinks"], d_attn, name="swa_bwd",
                                                                  side=grad_to_chips(mid_sums))
    chip_parts.update(zip(mid, received))
    d_swa_in = swa_in_grad(dq, dkc, dkp, dvc, dvp, d_ba, name="swa_in_grad")
    d_relbias = mm(d_bias.reshape(SWA_HEADS, -1), onehot, "nt", hi=True, tk=4096, name="d_rel_bias").T
    d_wgates = mm(xb, d_gates, "tn", tk=4096, name="dw_gates")
    d_wgdn = mm(xb, d_gdn_in, "tn", tk=4096, name="dw_gdn")
    d_wswa = mm(xb, d_swa_in, "tn", tn=SWA_IN_W, name="dw_swa")
    in_chunks = [merge_w_in_grad(d_wswa, d_wgdn, d_wgates, name="merge_w_in_grad")]
    gx, received = mm(d_swa_in, w_swa, "nt", add=dz1, add_scale=ALPHA, tk=SWA_IN_W, name="dx_swa",
                      side=grad_to_sibling(in_chunks))
    in_sums = sibling_sums(("w_in",), in_chunks, received)
    gx, received = mm(d_gates, w_gates, "nt", add=gx, name="dx_gates", side=grad_to_chips(in_sums, rows=(0, half)))
    gx, received = mm(d_gdn_in, w_gdn, "nt", add=gx, name="dx_gdn",
                      side=grad_to_chips(in_sums, rows=(half, half), into=received))
    chip_parts["w_in"] = received[0]
    grads = {}

    gsmall = {
        "rel_bias": d_relbias, "swa_sinks": d_sinks[:, :SWA_HEADS], "gdn_a_log": d_alog[:, GDN_HEADS:2 * GDN_HEADS],
        "gdn_dt_bias": d_dt[:, GDN_HEADS:2 * GDN_HEADS], "gdn_norm_w": d_normw, "ln1_g": d_ln1g, "ln1_b": d_ln1b,
        "ln2_g": d_ln2g, "ln2_b": d_ln2b, "ln3_g": d_ln3g, "ln3_b": d_ln3b,
        "ffn_conv_b": d_cwb[:, 3, :FF_SHARD].reshape(1, 2 * D_FF),
        "ffn_conv_w": jnp.transpose(d_cwb[:, :3, :FF_SHARD], (1, 0, 2)).reshape(3, 2 * D_FF),
        "gdn_conv_w": d_convw[:GDN_CONV],
    }
    small_shapes = [shp for _, shp in SMALL] + [(3, 2 * D_FF), (GDN_CONV, QKV_W)]
    small_names = [n for n, _ in SMALL] + ["ffn_conv_w", "gdn_conv_w"]
    small_sum = all_reduce_small(_pack([gsmall[n] for n in small_names], AR_ROWS, f32), name="all_reduce_small")
    grads.update(zip(small_names, _unpack(small_sum.reshape(-1), small_shapes)))
    grads["ffn_conv_w"] = lax.dynamic_slice_in_dim(grads["ffn_conv_w"], my_id * FF_SHARD, FF_SHARD, axis=1)
    grads["gdn_conv_w"] = lax.dynamic_slice_in_dim(grads["gdn_conv_w"], my_id * (QKV_W // N_DEV), QKV_W // N_DEV, axis=1)

    big = [n for n, shp, _ in SHARDED if shp[0] * shp[1] > 8192]
    tiny = [n for n in WEIGHTS if n not in big]
    delta, new_m, new_v = {}, {}, {}
    for n in big:
        grads[n], delta[n], new_m[n], new_v[n] = adamw_of_partial_sums(w2[n], chip_parts[n], m2[n], v2[n], name="adamw_" + n)
    tiny_shapes = [w2[n].shape for n in tiny]
    packed = [_pack([src[n] for n in tiny], SMALL_ROWS, f32) for src in (w2, grads, m2, v2)]
    for dst, res in zip((delta, new_m, new_v), adamw(*packed, name="adamw_small")):
        dst.update(zip(tiny, _unpack(res.reshape(-1), tiny_shapes)))

    def shaped(d):
        return [d[n].reshape(env[n].shape) for n in WEIGHTS]

    loss_all = lax.psum(loss[0, 0], ("x", "y", "c"))
    return (loss_all, gx[None], *shaped(grads), *shaped(delta), *shaped(new_m), *shaped(new_v))
```

```python
import functools
import math
from typing import Callable, NamedTuple

import jax
import jax.numpy as jnp
from jax import lax
from jax.experimental import pallas as pl
from jax.experimental.pallas import tpu as pltpu

f32 = jnp.float32
bf16 = jnp.bfloat16
HI = lax.Precision.HIGHEST
MESH = pl.DeviceIdType.MESH

D_MODEL = 2048
N_DEV = 8
SWA_HEADS, SWA_KV_HEADS, SWA_HEAD_DIM, BLOCK = 16, 2, 64, 128
REL_BUCKETS, REL_MAX_DIST = 32, 128
GDN_HEADS, GDN_HEAD_DIM, GDN_CONV, GDN_CHUNK = 8, 128, 4, 64
MEM_HEADS, MEM_HEAD_DIM = 4, 128
D_FF = 5504
FF_SHARD = 2 * D_FF // N_DEV
FF_PAD = 1408
NORM_EPS = 1e-5
ALPHA = 2.0 ** 0.25
NEG_INF = -1e30
SWA_Q, SWA_KV, GDN_W, MEM_W = 1024, 128, 1024, 512
IN_DIM = 9488
HALO = 8

ADAM_LR, ADAM_B1, ADAM_B2, ADAM_EPS, ADAM_WD, ADAM_STEP = 0.001, 0.9, 0.999, 1e-08, 0.01, 10

PACK_COLS = 1024
SMALL_ROWS = 32
AR_ROWS = 72
CONV_ROWS = 48

SHARDED = (
    ("w_in", (2048, 1186), 1), ("w_br_swa", (1024, 256), 1), ("w_br_gdn", (1024, 256), 1),
    ("w_mix_o", (256, 2048), 0), ("w_mem_q", (256, 512), 0), ("w_mem_kv", (256, 1024), 0),
    ("w_mem_o", (512, 256), 1), ("w_up", (2048, 1376), 1), ("w_down", (688, 2048), 0),
    ("ffn_conv_w", (3, 1376), 1), ("gdn_conv_w", (4, 384), 1),
)
SMALL = (
    ("rel_bias", (32, 16)), ("swa_sinks", (1, 16)), ("gdn_a_log", (1, 8)), ("gdn_dt_bias", (1, 8)),
    ("gdn_norm_w", (1, 128)), ("ln1_g", (1, 2048)), ("ln1_b", (1, 2048)), ("ln2_g", (1, 2048)),
    ("ln2_b", (1, 2048)), ("ln3_g", (1, 2048)), ("ln3_b", (1, 2048)), ("ffn_conv_b", (1, 11008)),
)
WEIGHTS = ("w_in", "rel_bias", "swa_sinks", "gdn_conv_w", "gdn_a_log", "gdn_dt_bias", "gdn_norm_w", "w_br_swa",
           "w_br_gdn", "w_mix_o", "ln1_g", "ln1_b", "w_mem_q", "w_mem_kv", "w_mem_o", "ln2_g", "ln2_b", "w_up",
           "ffn_conv_w", "ffn_conv_b", "w_down", "ln3_g", "ln3_b")


def _tile(n, target, align):
    if n <= target:
        return n
    t = (target // align) * align
    while t >= align:
        if n % t == 0:
            return t
        t -= align
    return n


VMEM_LIMIT_BYTES = 56 * 1024 * 1024


def _params(*sem):
    return pltpu.CompilerParams(dimension_semantics=sem, vmem_limit_bytes=VMEM_LIMIT_BYTES)


def _sigmoid(v):
    return jax.nn.sigmoid(v)


def _d16(a, b, dims):
    return lax.dot_general(a.astype(bf16), b.astype(bf16), (dims, ((), ())), preferred_element_type=f32)


NN = ((1,), (0,))
NT = ((1,), (1,))
TN = ((0,), (0,))


def mm(a, b, mode, *, name, add=None, add_scale=1.0, out_dtype=f32, hi=False, tm=1024, tn=1024, tk=2048,
       b_blocked=False, out_blocked=False, k_shards=1, side=None):
    if b_blocked:
        nb, rows, width = b.shape
        if mode == "nn":
            (m, k), n, tn = a.shape, nb * width, width
        else:
            (m, k), n, tk = a.shape, rows, k_shards * width
    elif mode == "nn":
        (m, k), (_, n) = a.shape, b.shape
    elif mode == "nt":
        (m, k), (n, _) = a.shape, b.shape
    else:
        (k, m), (_, n) = a.shape, b.shape
    if out_blocked:
        tn = n // N_DEV
    tm, tn, tk = _tile(m, tm, 8 if mode != "tn" else 128), _tile(n, tn, 128), _tile(k, tk, 128 if mode != "tn" else 8)
    nk = k // tk
    dims = {"nn": NN, "nt": NT, "tn": TN}[mode]
    a_spec = pl.BlockSpec((tk, tm), lambda i, j, kk: (kk, i)) if mode == "tn" else pl.BlockSpec((tm, tk), lambda i, j, kk: (i, kk))
    if b_blocked:
        b_spec = (pl.BlockSpec((None, tk, tn), lambda i, j, kk: (j, kk, 0)) if mode == "nn"
                  else pl.BlockSpec((k_shards, tn, tk // k_shards), lambda i, j, kk: (kk, j, 0)))
    else:
        b_spec = pl.BlockSpec((tn, tk), lambda i, j, kk: (j, kk)) if mode == "nt" else pl.BlockSpec((tk, tn), lambda i, j, kk: (kk, j))
    if out_blocked:
        o_spec, o_shape = pl.BlockSpec((None, tm, tn), lambda i, j, kk: (j, i, 0)), (N_DEV, m, tn)
    else:
        o_spec, o_shape = pl.BlockSpec((tm, tn), lambda i, j, kk: (i, j)), (m, n)
    has_add = add is not None

    def product(a_ref, b_ref):
        if hi:
            return lax.dot_general(a_ref[...], b_ref[...], (dims, ((), ())), precision=HI, preferred_element_type=f32)
        if b_blocked and mode == "nt":
            width = tk // k_shards
            parts = [_d16(a_ref[:, s * width:(s + 1) * width], b_ref[s], dims) for s in range(k_shards)]
            return functools.reduce(lambda p, q: p + q, parts)
        return _d16(a_ref[...], b_ref[...], dims)

    def finish(r, add_ref, o_ref):
        if has_add:
            r = r + add_scale * add_ref[...]
        o_ref[...] = r.astype(out_dtype)

    def body_one_step(a_ref, b_ref, *rest):
        finish(product(a_ref, b_ref), rest[0] if has_add else None, rest[-1])

    def body_k_steps(a_ref, b_ref, *rest):
        o_ref, acc_ref = rest[-2:]
        kk = pl.program_id(2)

        @pl.when(kk == 0)
        def _():
            acc_ref[...] = jnp.zeros_like(acc_ref)

        acc_ref[...] += product(a_ref, b_ref)

        @pl.when(kk == nk - 1)
        def _():
            finish(acc_ref[...], rest[0] if has_add else None, o_ref)

    return _call(body_one_step if nk == 1 else body_k_steps, (a, b, add) if has_add else (a, b), name=name,
                 grid=(m // tm, n // tn, nk), in_specs=[a_spec, b_spec] + ([o_spec] if has_add else []), out_specs=o_spec,
                 out_shape=jax.ShapeDtypeStruct(o_shape, out_dtype),
                 scratch_shapes=[] if nk == 1 else [pltpu.VMEM((tm, tn), f32)],
                 semantics=("parallel", "parallel", "arbitrary"), side=side)


def cast_bf16(a, *, name, side=None):
    m, n = a.shape
    tm = _tile(m, 512, 16)

    def body(a_ref, o_ref):
        o_ref[...] = a_ref[...].astype(bf16)

    return _call(body, (a,), name=name, grid=(m // tm,), in_specs=[pl.BlockSpec((tm, n), lambda i: (i, 0))],
                 out_specs=pl.BlockSpec((tm, n), lambda i: (i, 0)), out_shape=jax.ShapeDtypeStruct((m, n), bf16),
                 semantics=("parallel",), side=side)


def _ln_stats(z):
    mu = jnp.mean(z, axis=-1, keepdims=True)
    zc = z - mu
    var = jnp.mean(zc * zc, axis=-1, keepdims=True)
    rstd = lax.rsqrt(var + NORM_EPS)
    return zc * rstd, rstd


def ln_fwd(z, g, b, *, name):
    s, d = z.shape
    tm = _tile(s, 512, 16)

    def body(z_ref, g_ref, b_ref, y_ref, yb_ref):
        xhat, _ = _ln_stats(z_ref[...])
        y = xhat * g_ref[...] + b_ref[...]
        y_ref[...] = y
        yb_ref[...] = y.astype(bf16)

    row = pl.BlockSpec((tm, d), lambda i: (i, 0))
    vec = pl.BlockSpec((1, d), lambda i: (0, 0))
    return pl.pallas_call(body, name=name, grid=(s // tm,), in_specs=[row, vec, vec], out_specs=[row, row],
                          out_shape=[jax.ShapeDtypeStruct((s, d), f32), jax.ShapeDtypeStruct((s, d), bf16)],
                          compiler_params=_params("parallel"))(z, g, b)


def _ln_bwd_tile(dy, z, g):
    xhat, rstd = _ln_stats(z)
    dxh = dy * g
    m1 = jnp.mean(dxh, axis=-1, keepdims=True)
    m2 = jnp.mean(dxh * xhat, axis=-1, keepdims=True)
    dz = rstd * (dxh - m1 - xhat * m2)
    return dz, jnp.sum(dy * xhat, axis=0, keepdims=True), jnp.sum(dy, axis=0, keepdims=True)


def ln_bwd(dy, z, g, *, name):
    s, d = z.shape
    tm = _tile(s, 256, 16)

    def body(dy_ref, z_ref, g_ref, dz_ref, dzb_ref, dg_ref, db_ref):
        @pl.when(pl.program_id(0) == 0)
        def _():
            dg_ref[...] = jnp.zeros_like(dg_ref)
            db_ref[...] = jnp.zeros_like(db_ref)

        dz, dg, db = _ln_bwd_tile(dy_ref[...], z_ref[...], g_ref[...])
        dz_ref[...] = dz
        dzb_ref[...] = dz.astype(bf16)
        dg_ref[...] += dg
        db_ref[...] += db

    row = pl.BlockSpec((tm, d), lambda i: (i, 0))
    vec = pl.BlockSpec((1, d), lambda i: (0, 0))
    return pl.pallas_call(body, name=name, grid=(s // tm,), in_specs=[row, row, vec], out_specs=[row, row, vec, vec],
                          out_shape=[jax.ShapeDtypeStruct((s, d), f32), jax.ShapeDtypeStruct((s, d), bf16),
                                     jax.ShapeDtypeStruct((1, d), f32), jax.ShapeDtypeStruct((1, d), f32)],
                          compiler_params=_params("arbitrary"))(dy, z, g)


def ln_loss(z, target, g, b, *, name):
    s, d = z.shape
    tm = _tile(s, 256, 16)
    nt = s // tm

    def body(z_ref, t_ref, g_ref, b_ref, dz_ref, dzb_ref, dg_ref, db_ref, loss_ref, lacc_ref):
        i = pl.program_id(0)

        @pl.when(i == 0)
        def _():
            dg_ref[...] = jnp.zeros_like(dg_ref)
            db_ref[...] = jnp.zeros_like(db_ref)
            lacc_ref[...] = jnp.zeros_like(lacc_ref)

        zv, gv = z_ref[...], g_ref[...]
        xhat, _ = _ln_stats(zv)
        err = xhat * gv + b_ref[...] - t_ref[...]
        lacc_ref[...] += jnp.sum(err * err, axis=0, keepdims=True)
        dz, dg, db = _ln_bwd_tile(err * (1.0 / d), zv, gv)
        dz_ref[...] = dz
        dzb_ref[...] = dz.astype(bf16)
        dg_ref[...] += dg
        db_ref[...] += db

        @pl.when(i == nt - 1)
        def _():
            loss_ref[...] = (0.5 / d) * jnp.sum(lacc_ref[...], axis=1, keepdims=True)

    row = pl.BlockSpec((tm, d), lambda i: (i, 0))
    vec = pl.BlockSpec((1, d), lambda i: (0, 0))
    return pl.pallas_call(body, name=name, grid=(nt,), in_specs=[row, row, vec, vec],
                          out_specs=[row, row, vec, vec, pl.BlockSpec((1, 1), lambda i: (0, 0))],
                          out_shape=[jax.ShapeDtypeStruct((s, d), f32), jax.ShapeDtypeStruct((s, d), bf16),
                                     jax.ShapeDtypeStruct((1, d), f32), jax.ShapeDtypeStruct((1, d), f32),
                                     jax.ShapeDtypeStruct((1, 1), f32)],
                          scratch_shapes=[pltpu.VMEM((1, d), f32)],
                          compiler_params=_params("arbitrary"))(z, target, g, b)


def merge_fwd(gates, ys, yg, *, name):
    s, d = ys.shape
    tm = _tile(s, 512, 16)

    def body(gt_ref, ys_ref, yg_ref, o_ref):
        o_ref[...] = (_sigmoid(gt_ref[:, :d].astype(f32)) * ys_ref[...].astype(f32)
                      + _sigmoid(gt_ref[:, d:].astype(f32)) * yg_ref[...].astype(f32)).astype(bf16)

    row = pl.BlockSpec((tm, d), lambda i: (i, 0))
    return pl.pallas_call(body, name=name, grid=(s // tm,), in_specs=[pl.BlockSpec((tm, 2 * d), lambda i: (i, 0)), row, row],
                          out_specs=row, out_shape=jax.ShapeDtypeStruct((s, d), bf16),
                          compiler_params=_params("parallel"))(gates, ys, yg)


def merge_bwd(gates, ys, yg, dmix, *, name):
    s, d = ys.shape
    tm = _tile(s, 512, 16)

    def body(gt_ref, ys_ref, yg_ref, dm_ref, dys_ref, dyg_ref, dgt_ref):
        dm = dm_ref[...]
        sa, sb = _sigmoid(gt_ref[:, :d].astype(f32)), _sigmoid(gt_ref[:, d:].astype(f32))
        dys_ref[...] = (dm * sa).astype(bf16)
        dyg_ref[...] = (dm * sb).astype(bf16)
        dgt_ref[:, :d] = (dm * ys_ref[...].astype(f32) * sa * (1.0 - sa)).astype(bf16)
        dgt_ref[:, d:] = (dm * yg_ref[...].astype(f32) * sb * (1.0 - sb)).astype(bf16)

    row = pl.BlockSpec((tm, d), lambda i: (i, 0))
    wide = pl.BlockSpec((tm, 2 * d), lambda i: (i, 0))
    return pl.pallas_call(body, name=name, grid=(s // tm,), in_specs=[wide, row, row, row], out_specs=[row, row, wide],
                          out_shape=[jax.ShapeDtypeStruct((s, d), bf16), jax.ShapeDtypeStruct((s, d), bf16),
                                     jax.ShapeDtypeStruct((s, 2 * d), bf16)],
                          compiler_params=_params("parallel"))(gates, ys, yg, dmix)


def _shift_down(ext, j):
    return ext if j == 0 else pltpu.roll(ext, j, 0)


def _shift_up(ext, j):
    return ext if j == 0 else pltpu.roll(ext, ext.shape[0] - j, 0)


def _conv_taps(ext, width):
    return [_shift_down(ext, width - 1 - j)[HALO:] for j in range(width)]


def _causal_conv(taps, w_ref):
    acc = None
    for j, tap in enumerate(taps):
        term = w_ref[j:j + 1, :] * tap
        acc = term if acc is None else acc + term
    return acc


def _conv_grads(dy_ext, x, w_ref, width, rows):
    ahead = [_shift_up(dy_ext, width - 1 - j)[:rows] for j in range(width)]
    dx = None
    for j in range(width):
        term = w_ref[j:j + 1, :] * ahead[j]
        dx = term if dx is None else dx + term
    return dx, [jnp.sum(x * ahead[j], axis=0, keepdims=True) for j in range(width)]


def _rows_to_block(rows, n_rows, cols):
    r = lax.broadcasted_iota(jnp.int32, (n_rows, cols), 0)
    out = jnp.zeros((n_rows, cols), f32)
    for j, v in enumerate(rows):
        out = out + jnp.where(r == j, v, 0.0)
    return out


def _silu_and_grad(v):
    sg = _sigmoid(v)
    return v * sg, sg * (1.0 + v * (1.0 - sg))


HALO_BF16 = 16


def ffn_act_fwd(hpre, cwb, *, name):
    s = hpre.shape[0]
    tm = _tile(s, 512, 16)
    hb = tm // HALO_BF16

    def body(hg_ref, hgp_ref, hu_ref, hup_ref, cg_ref, cu_ref, o_ref, g_ref, u_ref):
        first = pl.program_id(1) == 0

        def conv(h_ref, hp_ref, c_ref):
            prev = jnp.where(first, 0.0, hp_ref[...].astype(f32)[HALO_BF16 - HALO:])
            ext = jnp.concatenate([prev, h_ref[...].astype(f32)], axis=0)
            return _causal_conv(_conv_taps(ext, 3), c_ref.at[0]) + c_ref[0, 3:4, :]

        g = conv(hg_ref, hgp_ref, cg_ref)
        u = conv(hu_ref, hup_ref, cu_ref)
        g_ref[...] = g.astype(bf16)
        u_ref[...] = u.astype(bf16)
        o_ref[...] = (g * _sigmoid(g) * u).astype(bf16)

    def tile(off):
        return pl.BlockSpec((tm, FF_PAD), lambda j, i: (i, j + off))

    def halo(off):
        return pl.BlockSpec((HALO_BF16, FF_PAD), lambda j, i: (jnp.maximum(i * hb - 1, 0), j + off))

    def taps(off):
        return pl.BlockSpec((1, 8, FF_PAD), lambda j, i: (j + off, 0, 0))

    out = pl.BlockSpec((tm, FF_PAD), lambda j, i: (i, j))
    return pl.pallas_call(body, name=name, grid=(4, s // tm),
                          in_specs=[tile(0), halo(0), tile(4), halo(4), taps(0), taps(4)], out_specs=[out, out, out],
                          out_shape=[jax.ShapeDtypeStruct((s, 4 * FF_PAD), bf16)] * 3,
                          compiler_params=_params("parallel", "parallel"))(hpre, hpre, hpre, hpre, cwb, cwb)


def ffn_act_bwd(hpre, conv_g, conv_u, dact, cwb, *, name):
    s = hpre.shape[0]
    tm = _tile(s, 512, 16)
    hb = tm // HALO_BF16
    nt = s // tm
    last_hb = s // HALO_BF16 - 1

    def body(hg_ref, hu_ref, g_ref, gn_ref, u_ref, un_ref, d_ref, dn_ref, cg_ref, cu_ref, dh_ref, dcg_ref, dcu_ref,
             buf_ref, sems):
        j, i = pl.program_id(0), pl.program_id(1)
        step = j * nt + i
        slot = step % 2

        def writes(from_slot):
            rows = pl.ds(pl.multiple_of(i * tm, tm), tm)
            return [pltpu.make_async_copy(buf_ref.at[from_slot, half],
                                          dh_ref.at[rows, pl.ds(pl.multiple_of((j + 4 * half) * FF_PAD, 128), FF_PAD)],
                                          sems.at[from_slot, half]) for half in (0, 1)]

        @pl.when(i == 0)
        def _():
            dcg_ref[...] = jnp.zeros_like(dcg_ref)
            dcu_ref[...] = jnp.zeros_like(dcu_ref)

        @pl.when(step >= 2)
        def _():
            for cp in writes(slot):
                cp.wait()

        def with_future(t_ref, n_ref):
            return jnp.concatenate([t_ref[...].astype(f32), n_ref[...].astype(f32)[:HALO]], axis=0)

        g, u = with_future(g_ref, gn_ref), with_future(u_ref, un_ref)
        d = jnp.concatenate([d_ref[...].astype(f32), jnp.where(i == nt - 1, 0.0, dn_ref[...].astype(f32)[:HALO])], axis=0)
        act, dact_dg = _silu_and_grad(g)
        dg = d * u * dact_dg
        du = d * act
        dhg, dwg = _conv_grads(dg, hg_ref[...].astype(f32), cg_ref.at[0], 3, tm)
        dhu, dwu = _conv_grads(du, hu_ref[...].astype(f32), cu_ref.at[0], 3, tm)
        buf_ref[slot, 0] = dhg.astype(bf16)
        buf_ref[slot, 1] = dhu.astype(bf16)
        for cp in writes(slot):
            cp.start()
        dcg_ref[0] += _rows_to_block(dwg + [jnp.sum(dg[:tm], axis=0, keepdims=True)], 8, FF_PAD)
        dcu_ref[0] += _rows_to_block(dwu + [jnp.sum(du[:tm], axis=0, keepdims=True)], 8, FF_PAD)

        @pl.when(step == 4 * nt - 1)
        def _():
            for cp in writes(slot) + writes(1 - slot):
                cp.wait()

    def tile(off):
        return pl.BlockSpec((tm, FF_PAD), lambda j, i: (i, j + off))

    nxt = pl.BlockSpec((HALO_BF16, FF_PAD), lambda j, i: (jnp.minimum((i + 1) * hb, last_hb), j))
    taps = [pl.BlockSpec((1, 8, FF_PAD), lambda j, i, off=off: (j + off, 0, 0)) for off in (0, 4)]
    dh, dcg, dcu = pl.pallas_call(
        body, name=name, grid=(4, nt),
        in_specs=[tile(0), tile(4), tile(0), nxt, tile(0), nxt, tile(0), nxt] + taps,
        out_specs=[ANY, taps[0], taps[0]],
        out_shape=[jax.ShapeDtypeStruct((s, 8 * FF_PAD), bf16),
                   jax.ShapeDtypeStruct((4, 8, FF_PAD), f32), jax.ShapeDtypeStruct((4, 8, FF_PAD), f32)],
        scratch_shapes=[pltpu.VMEM((2, 2, tm, FF_PAD), bf16), pltpu.SemaphoreType.DMA((2, 2))],
        compiler_params=_params("arbitrary", "arbitrary"),
    )(hpre, hpre, conv_g, conv_g, conv_u, conv_u, dact, dact, cwb, cwb)
    return dh, jnp.concatenate([dcg, dcu], axis=0)


MEM_SCALE = MEM_HEAD_DIM ** -0.5


def _softmax_rows(sc):
    m = jnp.max(sc, axis=-1, keepdims=True)
    e = jnp.exp(sc - m)
    return e / jnp.sum(e, axis=-1, keepdims=True)


def memattn_fwd(qm, kv, *, name):
    s = qm.shape[0]
    mlen = kv.shape[0]
    tm = _tile(s, 512, 16)

    def body(q_ref, kv_ref, o_ref):
        for h in range(MEM_HEADS):
            lo = h * MEM_HEAD_DIM
            q = q_ref[:, lo:lo + MEM_HEAD_DIM]
            k = kv_ref[:, lo:lo + MEM_HEAD_DIM]
            v = kv_ref[:, MEM_W + lo:MEM_W + lo + MEM_HEAD_DIM]
            p = _softmax_rows(_d16(q, k, NT) * MEM_SCALE)
            o_ref[:, lo:lo + MEM_HEAD_DIM] = _d16(p, v, NN).astype(bf16)

    return pl.pallas_call(body, name=name, grid=(s // tm,),
                          in_specs=[pl.BlockSpec((tm, MEM_W), lambda i: (i, 0)), pl.BlockSpec((mlen, 2 * MEM_W), lambda i: (0, 0))],
                          out_specs=pl.BlockSpec((tm, MEM_W), lambda i: (i, 0)),
                          out_shape=jax.ShapeDtypeStruct((s, MEM_W), bf16), compiler_params=_params("parallel"))(qm, kv)


def memattn_bwd(qm, kv, dout, *, name):
    s = qm.shape[0]
    mlen = kv.shape[0]
    tm = _tile(s, 512, 16)

    def body(q_ref, kv_ref, do_ref, dq_ref, dkv_ref):
        @pl.when(pl.program_id(0) == 0)
        def _():
            dkv_ref[...] = jnp.zeros_like(dkv_ref)

        for h in range(MEM_HEADS):
            lo = h * MEM_HEAD_DIM
            q = q_ref[:, lo:lo + MEM_HEAD_DIM]
            k = kv_ref[:, lo:lo + MEM_HEAD_DIM]
            v = kv_ref[:, MEM_W + lo:MEM_W + lo + MEM_HEAD_DIM]
            do = do_ref[:, lo:lo + MEM_HEAD_DIM]
            p = _softmax_rows(_d16(q, k, NT) * MEM_SCALE)
            dp = _d16(do, v, NT)
            ds = p * (dp - jnp.sum(p * dp, axis=-1, keepdims=True)) * MEM_SCALE
            dq_ref[:, lo:lo + MEM_HEAD_DIM] = _d16(ds, k, NN).astype(bf16)
            dkv_ref[:, lo:lo + MEM_HEAD_DIM] += _d16(ds, q, TN)
            dkv_ref[:, MEM_W + lo:MEM_W + lo + MEM_HEAD_DIM] += _d16(p, do, TN)

    row = pl.BlockSpec((tm, MEM_W), lambda i: (i, 0))
    full = pl.BlockSpec((mlen, 2 * MEM_W), lambda i: (0, 0))
    return pl.pallas_call(body, name=name, grid=(s // tm,), in_specs=[row, full, row], out_specs=[row, full],
                          out_shape=[jax.ShapeDtypeStruct((s, MEM_W), bf16), jax.ShapeDtypeStruct((mlen, 2 * MEM_W), f32)],
                          compiler_params=_params("arbitrary"))(qm, kv, dout)


SWA_SCALE = SWA_HEAD_DIM ** -0.5
SWA_GROUP = SWA_HEADS // SWA_KV_HEADS
SWA_IN_W = 1408
K_COL, V_COL, BA_COL = SWA_Q // 128, SWA_Q // 128 + 1, SWA_Q // 128 + 2


def _swa_mask(n):
    qi = lax.broadcasted_iota(jnp.int32, (BLOCK, 2 * BLOCK), 0)
    kj = lax.broadcasted_iota(jnp.int32, (BLOCK, 2 * BLOCK), 1)
    dist = qi + BLOCK - kj
    return (dist >= 0) & (dist < BLOCK) & ((n > 0) | (kj >= BLOCK))


def _swa_probs(q, k, bias, sink, mask):
    heads = range(len(q))
    sc = [jnp.where(mask, _d16(q[h], k[h], NT) * SWA_SCALE + bias[h], NEG_INF) for h in heads]
    m = [jnp.maximum(jnp.max(sc[h], axis=-1, keepdims=True), sink[h]) for h in heads]
    e = [jnp.exp(sc[h] - m[h]) for h in heads]
    es = [jnp.exp(sink[h] - m[h]) for h in heads]
    inv = [1.0 / (jnp.sum(e[h], axis=-1, keepdims=True) + es[h]) for h in heads]
    return e, es, inv


def _swa_heads(ref):
    return [ref[:, h * SWA_HEAD_DIM:(h + 1) * SWA_HEAD_DIM] for h in range(SWA_HEADS)]


def _swa_kv_of_heads(band):
    kv = [band[:, g * SWA_HEAD_DIM:(g + 1) * SWA_HEAD_DIM] for g in range(SWA_KV_HEADS)]
    return [kv[h // SWA_GROUP] for h in range(SWA_HEADS)]


def _swa_specs():
    q_spec = pl.BlockSpec((BLOCK, SWA_Q), lambda n: (n, 0))

    def band(col):
        return [pl.BlockSpec((BLOCK, SWA_KV), lambda n: (jnp.maximum(n - 1, 0), col)),
                pl.BlockSpec((BLOCK, SWA_KV), lambda n: (n, col))]

    bias_spec = pl.BlockSpec((SWA_HEADS, BLOCK, 2 * BLOCK), lambda n: (0, 0, 0))
    sink_spec = pl.BlockSpec((1, SWA_HEADS), lambda n: (0, 0))
    return [q_spec] + band(K_COL) + band(V_COL) + [bias_spec, sink_spec]


def swa_fwd(swa_in, bias, sinks, *, name, side=None):
    s = swa_in.shape[0]

    def body(q_ref, kp_ref, kc_ref, vp_ref, vc_ref, bias_ref, sink_ref, o_ref):
        mask = _swa_mask(pl.program_id(0))
        kb = jnp.concatenate([kp_ref[...], kc_ref[...]], axis=0)
        vb = jnp.concatenate([vp_ref[...], vc_ref[...]], axis=0)
        heads = range(SWA_HEADS)
        k, v = _swa_kv_of_heads(kb), _swa_kv_of_heads(vb)
        e, _, inv = _swa_probs(_swa_heads(q_ref), k, [bias_ref[h] for h in heads], [sink_ref[:, h:h + 1] for h in heads], mask)
        outs = [_d16(e[h] * inv[h], v[h], NN) for h in heads]
        for h in heads:
            o_ref[:, h * SWA_HEAD_DIM:(h + 1) * SWA_HEAD_DIM] = outs[h].astype(bf16)

    return _call(body, (swa_in, swa_in, swa_in, swa_in, swa_in, bias, sinks), name=name, grid=(s // BLOCK,),
                 in_specs=_swa_specs(), out_specs=pl.BlockSpec((BLOCK, SWA_Q), lambda n: (n, 0)),
                 out_shape=jax.ShapeDtypeStruct((s, SWA_Q), bf16), semantics=("parallel",), side=side)


def swa_bwd(swa_in, bias, sinks, dout, *, name, side=None):
    s = swa_in.shape[0]

    def body(q_ref, kp_ref, kc_ref, vp_ref, vc_ref, bias_ref, sink_ref, do_ref,
             dq_ref, dkc_ref, dkp_ref, dvc_ref, dvp_ref, dbias_ref, dsink_ref):
        n = pl.program_id(0)

        @pl.when(n == 0)
        def _():
            dbias_ref[...] = jnp.zeros_like(dbias_ref)
            dsink_ref[...] = jnp.zeros_like(dsink_ref)

        mask = _swa_mask(n)
        kb = jnp.concatenate([kp_ref[...], kc_ref[...]], axis=0)
        vb = jnp.concatenate([vp_ref[...], vc_ref[...]], axis=0)
        lane = lax.broadcasted_iota(jnp.int32, (1, 128), 1)
        hs = range(SWA_HEADS)
        q, do = _swa_heads(q_ref), _swa_heads(do_ref)
        k, v = _swa_kv_of_heads(kb), _swa_kv_of_heads(vb)
        e, es, inv = _swa_probs(q, k, [bias_ref[h] for h in hs], [sink_ref[:, h:h + 1] for h in hs], mask)
        p = [e[h] * inv[h] for h in hs]
        dp = [_d16(do[h], v[h], NT) for h in hs]
        delta = [jnp.sum(p[h] * dp[h], axis=-1, keepdims=True) for h in hs]
        ds = [p[h] * (dp[h] - delta[h]) for h in hs]
        dss = [ds[h] * SWA_SCALE for h in hs]
        dq = [_d16(dss[h], k[h], NN) for h in hs]
        dks = [_d16(dss[h], q[h], TN) for h in hs]
        dvs = [_d16(p[h], do[h], TN) for h in hs]
        dsink = jnp.zeros((1, 128), f32)
        for h in hs:
            dbias_ref[h] += ds[h]
            dq_ref[:, h * SWA_HEAD_DIM:(h + 1) * SWA_HEAD_DIM] = dq[h]
            dsink = dsink + jnp.where(lane == h, -jnp.sum(es[h] * inv[h] * delta[h], axis=0, keepdims=True), 0.0)
        for g in range(SWA_KV_HEADS):
            kl = g * SWA_HEAD_DIM
            group = range(g * SWA_GROUP, (g + 1) * SWA_GROUP)
            dk = functools.reduce(lambda a, b: a + b, [dks[h] for h in group])
            dv = functools.reduce(lambda a, b: a + b, [dvs[h] for h in group])
            dkp_ref[:, kl:kl + SWA_HEAD_DIM] = dk[:BLOCK]
            dkc_ref[:, kl:kl + SWA_HEAD_DIM] = dk[BLOCK:]
            dvp_ref[:, kl:kl + SWA_HEAD_DIM] = dv[:BLOCK]
            dvc_ref[:, kl:kl + SWA_HEAD_DIM] = dv[BLOCK:]
        dsink_ref[...] += dsink

    qs = pl.BlockSpec((BLOCK, SWA_Q), lambda n: (n, 0))
    ks = pl.BlockSpec((BLOCK, SWA_KV), lambda n: (n, 0))
    return _call(
        body, (swa_in, swa_in, swa_in, swa_in, swa_in, bias, sinks, dout), name=name, grid=(s // BLOCK,),
        in_specs=_swa_specs() + [qs],
        out_specs=[qs, ks, ks, ks, ks, pl.BlockSpec((SWA_HEADS, BLOCK, 2 * BLOCK), lambda n: (0, 0, 0)),
                   pl.BlockSpec((1, 128), lambda n: (0, 0))],
        out_shape=[jax.ShapeDtypeStruct((s, SWA_Q), f32)] + [jax.ShapeDtypeStruct((s, SWA_KV), f32)] * 4
        + [jax.ShapeDtypeStruct((SWA_HEADS, BLOCK, 2 * BLOCK), f32), jax.ShapeDtypeStruct((1, 128), f32)],
        semantics=("arbitrary",), side=side)


def swa_in_grad(dq, dkc, dkp, dvc, dvp, dba, *, name):
    s = dq.shape[0]
    nb = s // BLOCK

    def body(dq_ref, dkc_ref, dkp_ref, dvc_ref, dvp_ref, dba_ref, o_ref):
        has_next = pl.program_id(0) < nb - 1
        o_ref[:, :SWA_Q] = dq_ref[...].astype(bf16)
        o_ref[:, SWA_Q:SWA_Q + SWA_KV] = (dkc_ref[...] + jnp.where(has_next, dkp_ref[...], 0.0)).astype(bf16)
        o_ref[:, SWA_Q + SWA_KV:SWA_Q + 2 * SWA_KV] = (dvc_ref[...] + jnp.where(has_next, dvp_ref[...], 0.0)).astype(bf16)
        o_ref[:, SWA_Q + 2 * SWA_KV:] = dba_ref[...].astype(bf16)

    cur = pl.BlockSpec((BLOCK, SWA_KV), lambda n: (n, 0))
    nxt = pl.BlockSpec((BLOCK, SWA_KV), lambda n: (jnp.minimum(n + 1, nb - 1), 0))
    return pl.pallas_call(body, name=name, grid=(nb,),
                          in_specs=[pl.BlockSpec((BLOCK, SWA_Q), lambda n: (n, 0)), cur, nxt, cur, nxt, cur],
                          out_specs=pl.BlockSpec((BLOCK, SWA_IN_W), lambda n: (n, 0)),
                          out_shape=jax.ShapeDtypeStruct((s, SWA_IN_W), bf16),
                          compiler_params=_params("parallel"))(dq, dkc, dkp, dvc, dvp, dba)


def _bucket_onehot():
    qi = jnp.arange(BLOCK)[:, None]
    kj = jnp.arange(2 * BLOCK)[None, :]
    dist = jnp.maximum(qi + BLOCK - kj, 0)
    max_exact = REL_BUCKETS // 2
    dd = jnp.maximum(dist, 1).astype(f32)
    large = max_exact + (jnp.log(dd / max_exact) / math.log(REL_MAX_DIST / max_exact) * (REL_BUCKETS - max_exact)).astype(jnp.int32)
    bucket = jnp.where(dist < max_exact, dist, jnp.minimum(large, REL_BUCKETS - 1)).reshape(-1)
    return (bucket[None, :] == jnp.arange(REL_BUCKETS)[:, None]).astype(f32)


def _gbeta_fn(ba, alog_row, dt_row):
    col = lax.broadcasted_iota(jnp.int32, ba.shape, 1)
    v = ba + dt_row
    softplus = jnp.maximum(v, 0.0) + jnp.log(1.0 + jnp.exp(-jnp.abs(v)))
    g = -jnp.exp(alog_row) * softplus
    return jnp.where(col < GDN_HEADS, _sigmoid(ba), jnp.where(col < 2 * GDN_HEADS, g, 0.0))


def gbeta_fwd(swa_in, alog_row, dt_row, *, name):
    s = swa_in.shape[0]
    tm = _tile(s, 512, 8)

    def body(ba_ref, a_ref, d_ref, o_ref):
        o_ref[...] = _gbeta_fn(ba_ref[...], a_ref[...], d_ref[...])

    vec = pl.BlockSpec((1, 128), lambda i: (0, 0))
    return pl.pallas_call(body, name=name, grid=(s // tm,), in_specs=[pl.BlockSpec((tm, 128), lambda i: (i, BA_COL)), vec, vec],
                          out_specs=pl.BlockSpec((tm, 128), lambda i: (i, 0)), out_shape=jax.ShapeDtypeStruct((s, 128), f32),
                          compiler_params=_params("parallel"))(swa_in, alog_row, dt_row)


def gbeta_bwd(swa_in, alog_row, dt_row, dgbeta, *, name):
    s = swa_in.shape[0]
    tm = _tile(s, 512, 8)

    def body(ba_ref, a_ref, d_ref, dgb_ref, dba_ref, da_ref, dd_ref):
        @pl.when(pl.program_id(0) == 0)
        def _():
            da_ref[...] = jnp.zeros_like(da_ref)
            dd_ref[...] = jnp.zeros_like(dd_ref)

        _, pull = jax.vjp(_gbeta_fn, ba_ref[...], a_ref[...], d_ref[...])
        dba, da, dd = pull(dgb_ref[...])
        dba_ref[...] = dba
        da_ref[...] += da
        dd_ref[...] += dd

    vec = pl.BlockSpec((1, 128), lambda i: (0, 0))
    row = pl.BlockSpec((tm, 128), lambda i: (i, 0))
    return pl.pallas_call(body, name=name, grid=(s // tm,),
                          in_specs=[pl.BlockSpec((tm, 128), lambda i: (i, BA_COL)), vec, vec, row], out_specs=[row, vec, vec],
                          out_shape=[jax.ShapeDtypeStruct((s, 128), f32), jax.ShapeDtypeStruct((1, 128), f32),
                                     jax.ShapeDtypeStruct((1, 128), f32)],
                          compiler_params=_params("arbitrary"))(swa_in, alog_row, dt_row, dgbeta)


QKV_W = 3 * GDN_W


def gdn_pre_fwd(gdn_in, convw, *, name):
    s = gdn_in.shape[0]
    tm = _tile(s, 256, 16)
    hb = tm // HALO

    def body(x_ref, xp_ref, w_ref, q_ref, k_ref, v_ref, pre_ref):
        prev = jnp.where(pl.program_id(0) == 0, 0.0, xp_ref[...])
        pre = _causal_conv(_conv_taps(jnp.concatenate([prev, x_ref[...]], axis=0), GDN_CONV), w_ref)
        pre_ref[...] = pre
        act = pre * _sigmoid(pre)
        for h in range(GDN_HEADS):
            lo = h * GDN_HEAD_DIM
            for off, o_ref in ((0, q_ref), (GDN_W, k_ref)):
                seg = act[:, off + lo:off + lo + GDN_HEAD_DIM]
                o_ref[:, lo:lo + GDN_HEAD_DIM] = seg * lax.rsqrt(jnp.sum(seg * seg, axis=-1, keepdims=True) + 1e-6)
        v_ref[...] = act[:, 2 * GDN_W:]

    out = pl.BlockSpec((tm, GDN_W), lambda i: (i, 0))
    return pl.pallas_call(body, name=name, grid=(s // tm,),
                          in_specs=[pl.BlockSpec((tm, QKV_W), lambda i: (i, 0)),
                                    pl.BlockSpec((HALO, QKV_W), lambda i: (jnp.maximum(i * hb - 1, 0), 0)),
                                    pl.BlockSpec((8, QKV_W), lambda i: (0, 0))],
                          out_specs=[out, out, out, pl.BlockSpec((tm, QKV_W), lambda i: (i, 0))],
                          out_shape=[jax.ShapeDtypeStruct((s, GDN_W), f32)] * 3 + [jax.ShapeDtypeStruct((s, QKV_W), f32)],
                          compiler_params=_params("parallel"))(gdn_in, gdn_in, convw)


def gdn_pre_bwd(gdn_in, conv_out, convw, dqn, dkn, dv, dgz, *, name):
    s = gdn_in.shape[0]
    tm = _tile(s, 256, 16)
    hb = tm // HALO
    nt = s // tm
    last_hb = s // HALO - 1

    def body(x_ref, pre_ref, pren_ref, w_ref, dq_ref, dqx_ref, dk_ref, dkx_ref, dv_ref, dvx_ref, dz_ref, o_ref, dw_ref):
        i = pl.program_id(0)
        last = i == nt - 1

        @pl.when(i == 0)
        def _():
            dw_ref[...] = jnp.zeros_like(dw_ref)

        pre = jnp.concatenate([pre_ref[...], pren_ref[...]], axis=0)
        act, dact_dpre = _silu_and_grad(pre)

        def with_future(t_ref, n_ref):
            return jnp.concatenate([t_ref[...], jnp.where(last, 0.0, n_ref[...])], axis=0)

        dqe, dke, dve = with_future(dq_ref, dqx_ref), with_future(dk_ref, dkx_ref), with_future(dv_ref, dvx_ref)
        parts = []
        for off, dn in ((0, dqe), (GDN_W, dke)):
            for h in range(GDN_HEADS):
                lo = h * GDN_HEAD_DIM
                seg = act[:, off + lo:off + lo + GDN_HEAD_DIM]
                r = lax.rsqrt(jnp.sum(seg * seg, axis=-1, keepdims=True) + 1e-6)
                nrm = seg * r
                dseg = dn[:, lo:lo + GDN_HEAD_DIM]
                parts.append(r * (dseg - nrm * jnp.sum(dseg * nrm, axis=-1, keepdims=True)))
        dpre = jnp.concatenate(parts + [dve], axis=1) * dact_dpre
        dx, dw = _conv_grads(dpre, x_ref[...], w_ref, GDN_CONV, tm)
        o_ref[:, :QKV_W] = dx.astype(bf16)
        o_ref[:, QKV_W:] = dz_ref[...].astype(bf16)
        dw_ref[...] += _rows_to_block(dw, 8, QKV_W)

    row = pl.BlockSpec((tm, GDN_W), lambda i: (i, 0))
    fut = pl.BlockSpec((HALO, GDN_W), lambda i: (jnp.minimum((i + 1) * hb, last_hb), 0))
    wide = pl.BlockSpec((tm, QKV_W), lambda i: (i, 0))
    return pl.pallas_call(
        body, name=name, grid=(nt,),
        in_specs=[wide, wide, pl.BlockSpec((HALO, QKV_W), lambda i: (jnp.minimum((i + 1) * hb, last_hb), 0)),
                  pl.BlockSpec((8, QKV_W), lambda i: (0, 0)), row, fut, row, fut, row, fut, row],
        out_specs=[pl.BlockSpec((tm, 4 * GDN_W), lambda i: (i, 0)), pl.BlockSpec((8, QKV_W), lambda i: (0, 0))],
        out_shape=[jax.ShapeDtypeStruct((s, 4 * GDN_W), bf16), jax.ShapeDtypeStruct((8, QKV_W), f32)],
        compiler_params=_params("arbitrary"),
    )(gdn_in, conv_out, conv_out, convw, dqn, dqn, dkn, dkn, dv, dv, dgz)


def _gdn_post_head(o, z, nw):
    return o * lax.rsqrt(jnp.mean(o * o, axis=-1, keepdims=True) + 1e-6) * nw * (z * _sigmoid(z))


def gdn_post_fwd(o, gdn_in, nw, *, name):
    s = o.shape[0]
    tm = _tile(s, 512, 16)

    def body(o_ref, z_ref, nw_ref, y_ref):
        for h in range(GDN_HEADS):
            sl = slice(h * GDN_HEAD_DIM, (h + 1) * GDN_HEAD_DIM)
            y_ref[:, sl] = _gdn_post_head(o_ref[:, sl], z_ref[:, sl], nw_ref[...]).astype(bf16)

    row = pl.BlockSpec((tm, GDN_W), lambda i: (i, 0))
    return pl.pallas_call(body, name=name, grid=(s // tm,),
                          in_specs=[row, pl.BlockSpec((tm, GDN_W), lambda i: (i, 3)), pl.BlockSpec((1, 128), lambda i: (0, 0))],
                          out_specs=row, out_shape=jax.ShapeDtypeStruct((s, GDN_W), bf16),
                          compiler_params=_params("parallel"))(o, gdn_in, nw)


def gdn_post_bwd(o, gdn_in, nw, dy, *, name):
    s = o.shape[0]
    tm = _tile(s, 512, 16)

    def body(o_ref, z_ref, nw_ref, dy_ref, do_ref, dz_ref, dnw_ref):
        @pl.when(pl.program_id(0) == 0)
        def _():
            dnw_ref[...] = jnp.zeros_like(dnw_ref)

        dnw = jnp.zeros((1, 128), f32)
        for h in range(GDN_HEADS):
            sl = slice(h * GDN_HEAD_DIM, (h + 1) * GDN_HEAD_DIM)
            _, pull = jax.vjp(_gdn_post_head, o_ref[:, sl], z_ref[:, sl], nw_ref[...])
            do, dz, dn = pull(dy_ref[:, sl])
            do_ref[:, sl] = do
            dz_ref[:, sl] = dz
            dnw = dnw + dn
        dnw_ref[...] += dnw

    row = pl.BlockSpec((tm, GDN_W), lambda i: (i, 0))
    vec = pl.BlockSpec((1, 128), lambda i: (0, 0))
    return pl.pallas_call(body, name=name, grid=(s // tm,),
                          in_specs=[row, pl.BlockSpec((tm, GDN_W), lambda i: (i, 3)), vec, row], out_specs=[row, row, vec],
                          out_shape=[jax.ShapeDtypeStruct((s, GDN_W), f32), jax.ShapeDtypeStruct((s, GDN_W), f32),
                                     jax.ShapeDtypeStruct((1, 128), f32)],
                          compiler_params=_params("arbitrary"))(o, gdn_in, nw, dy)


def _dot_high(a, b, dims=NN):
    return lax.dot_general(a, b, (dims, ((), ())), precision=lax.Precision.HIGH, preferred_element_type=f32)


@jax.custom_vjp
def _unit_lower_inverses(a):
    c = a[0].shape[0]
    n = range(len(a))
    eye = (lax.broadcasted_iota(jnp.int32, (c, c), 0) == lax.broadcasted_iota(jnp.int32, (c, c), 1)).astype(f32)
    inv = [eye - a[i] for i in n]
    pw = [_dot_high(a[i], a[i]) for i in n]
    width = 2
    while width < c:
        inv = [inv[i] + _dot_high(inv[i], pw[i]) for i in n]
        width *= 2
        if width < c:
            pw = [_dot_high(pw[i], pw[i]) for i in n]
    return inv


def _unit_lower_inverses_fwd(a):
    inv = _unit_lower_inverses(a)
    return inv, inv


def _unit_lower_inverses_bwd(inv, g):
    return ([-_dot_high(_dot_high(x, gx, TN), x, NT) for x, gx in zip(inv, g)],)


_unit_lower_inverses.defvjp(_unit_lower_inverses_fwd, _unit_lower_inverses_bwd)


@jax.custom_vjp
def _known_inverses(a, inv):
    return inv


_known_inverses.defvjp(lambda a, inv: (inv, inv),
                       lambda inv, g: (_unit_lower_inverses_bwd(inv, g)[0], [jnp.zeros_like(x) for x in inv]))


def _gdn_chunks(q, k, v, gb, state, kept_inverses=None):
    c = GDN_CHUNK
    heads = range(len(q))
    r = lax.broadcasted_iota(jnp.int32, (c, c), 0)
    cc = lax.broadcasted_iota(jnp.int32, (c, c), 1)
    tril, strict = r >= cc, r > cc
    eye = (r == cc).astype(f32)

    def dhi(a, b):
        return jnp.dot(a, b, precision=lax.Precision.HIGH, preferred_element_type=f32)

    beta = [gb[:, h:h + 1] for h in heads]
    cum_cols = dhi(tril.astype(f32), gb)
    cum_rows = dhi(gb.T, (r <= cc).astype(f32))
    gi = [jnp.broadcast_to(cum_cols[:, GDN_HEADS + h:GDN_HEADS + h + 1], (c, c)) for h in heads]
    gj = [jnp.broadcast_to(cum_rows[GDN_HEADS + h:GDN_HEADS + h + 1, :], (c, c)) for h in heads]
    decay = [jnp.where(tril, jnp.exp(jnp.where(tril, gi[h] - gj[h], 0.0)), 0.0) for h in heads]
    kb = [k[h] * beta[h] for h in heads]
    vb = [v[h] * beta[h] for h in heads]
    qs = [q[h] * (GDN_HEAD_DIM ** -0.5) for h in heads]
    qk = [_d16(jnp.concatenate([qs[h], kb[h]], axis=0), k[h], NT) for h in heads]
    a = [jnp.where(strict, qk[h][c:] * decay[h], 0.0) for h in heads]
    tinv = _unit_lower_inverses(a) if kept_inverses is None else _known_inverses(a, kept_inverses)
    gc = [gi[h][:, 0:1] for h in heads]
    egc = [jnp.exp(gc[h]) for h in heads]
    uw = [dhi(tinv[h], jnp.concatenate([vb[h], kb[h] * egc[h]], axis=1)) for h in heads]
    u = [uw[h][:, :GDN_HEAD_DIM] for h in heads]
    w = [uw[h][:, GDN_HEAD_DIM:] for h in heads]
    attn = [jnp.where(tril, qk[h][:c] * decay[h], 0.0) for h in heads]
    g_last = [gi[h][c - 1:c, 0:1] for h in heads]
    on_state = [_d16(jnp.concatenate([w[h], qs[h] * egc[h]], axis=0), state[h], NN) for h in heads]
    v_new = [u[h] - on_state[h][:c] for h in heads]
    out = [on_state[h][c:] + _d16(attn[h], v_new[h], NN) for h in heads]
    new_state = [state[h] * jnp.exp(g_last[h]) + _d16(k[h] * jnp.exp(g_last[h] - gc[h]), v_new[h], TN) for h in heads]
    return out, new_state, tinv


def _head_cols(ref):
    return [ref[:, h * GDN_HEAD_DIM:(h + 1) * GDN_HEAD_DIM] for h in range(GDN_HEADS)]


def gdn_scan_fwd(qn, kn, v, gbeta, *, name, side=None):
    s = qn.shape[0]
    nc = s // GDN_CHUNK

    def body(q_ref, k_ref, v_ref, gb_ref, o_ref, st_ref, inv_ref, state_ref):
        @pl.when(pl.program_id(0) == 0)
        def _():
            state_ref[...] = jnp.zeros_like(state_ref)

        states = [state_ref[h] for h in range(GDN_HEADS)]
        outs, new, inverses = _gdn_chunks(_head_cols(q_ref), _head_cols(k_ref), _head_cols(v_ref), gb_ref[...], states)
        for h in range(GDN_HEADS):
            st_ref[0, h] = states[h]
            inv_ref[0, h] = inverses[h]
            o_ref[:, h * GDN_HEAD_DIM:(h + 1) * GDN_HEAD_DIM] = outs[h]
            state_ref[h] = new[h]

    row = pl.BlockSpec((GDN_CHUNK, GDN_W), lambda n: (n, 0))
    return _call(
        body, (qn, kn, v, gbeta), name=name, grid=(nc,),
        in_specs=[row, row, row, pl.BlockSpec((GDN_CHUNK, 128), lambda n: (n, 0))],
        out_specs=[row, pl.BlockSpec((1, GDN_HEADS, GDN_HEAD_DIM, GDN_HEAD_DIM), lambda n: (n, 0, 0, 0)),
                   pl.BlockSpec((1, GDN_HEADS, GDN_CHUNK, GDN_CHUNK), lambda n: (n, 0, 0, 0))],
        out_shape=[jax.ShapeDtypeStruct((s, GDN_W), f32),
                   jax.ShapeDtypeStruct((nc, GDN_HEADS, GDN_HEAD_DIM, GDN_HEAD_DIM), f32),
                   jax.ShapeDtypeStruct((nc, GDN_HEADS, GDN_CHUNK, GDN_CHUNK), f32)],
        scratch_shapes=[pltpu.VMEM((GDN_HEADS, GDN_HEAD_DIM, GDN_HEAD_DIM), f32)], semantics=("arbitrary",), side=side)


def gdn_scan_bwd(qn, kn, v, gbeta, states, inverses, dout, *, name, side=None):
    s = qn.shape[0]
    nc = s // GDN_CHUNK

    def body(q_ref, k_ref, v_ref, gb_ref, st_ref, inv_ref, do_ref, dq_ref, dk_ref, dv_ref, dgb_ref, dstate_ref):
        @pl.when(pl.program_id(0) == 0)
        def _():
            dstate_ref[...] = jnp.zeros_like(dstate_ref)

        kept = [inv_ref[0, h] for h in range(GDN_HEADS)]
        _, pull = jax.vjp(lambda *args: _gdn_chunks(*args, kept_inverses=kept)[:2],
                          _head_cols(q_ref), _head_cols(k_ref), _head_cols(v_ref), gb_ref[...],
                          [st_ref[0, h] for h in range(GDN_HEADS)])
        dq, dk, dv, dgb, dst = pull((_head_cols(do_ref), [dstate_ref[h] for h in range(GDN_HEADS)]))
        for h in range(GDN_HEADS):
            sl = slice(h * GDN_HEAD_DIM, (h + 1) * GDN_HEAD_DIM)
            dq_ref[:, sl] = dq[h]
            dk_ref[:, sl] = dk[h]
            dv_ref[:, sl] = dv[h]
            dstate_ref[h] = dst[h]
        dgb_ref[...] = dgb

    row = pl.BlockSpec((GDN_CHUNK, GDN_W), lambda n: (nc - 1 - n, 0))
    gb = pl.BlockSpec((GDN_CHUNK, 128), lambda n: (nc - 1 - n, 0))
    return _call(
        body, (qn, kn, v, gbeta, states, inverses, dout), name=name, grid=(nc,),
        in_specs=[row, row, row, gb, pl.BlockSpec((1, GDN_HEADS, GDN_HEAD_DIM, GDN_HEAD_DIM), lambda n: (nc - 1 - n, 0, 0, 0)),
                  pl.BlockSpec((1, GDN_HEADS, GDN_CHUNK, GDN_CHUNK), lambda n: (nc - 1 - n, 0, 0, 0)), row],
        out_specs=[row, row, row, gb],
        out_shape=[jax.ShapeDtypeStruct((s, GDN_W), f32)] * 3 + [jax.ShapeDtypeStruct((s, 128), f32)],
        scratch_shapes=[pltpu.VMEM((GDN_HEADS, GDN_HEAD_DIM, GDN_HEAD_DIM), f32)], semantics=("arbitrary",), side=side)


def _adamw_update(w, g, m, v):
    nm = ADAM_B1 * m + (1.0 - ADAM_B1) * g
    nv = ADAM_B2 * v + (1.0 - ADAM_B2) * (g * g)
    m_hat = nm / (1.0 - ADAM_B1 ** ADAM_STEP)
    v_hat = nv / (1.0 - ADAM_B2 ** ADAM_STEP)
    return -ADAM_LR * (m_hat / (jnp.sqrt(v_hat) + ADAM_EPS) + ADAM_WD * w), nm, nv


def adamw(w, g, m, v, *, name):
    r, c = w.shape
    tr = _tile(r, 256, 8)

    def body(w_ref, g_ref, m_ref, v_ref, d_ref, nm_ref, nv_ref):
        d_ref[...], nm_ref[...], nv_ref[...] = _adamw_update(w_ref[...], g_ref[...], m_ref[...], v_ref[...])

    spec = pl.BlockSpec((tr, c), lambda i: (i, 0))
    return pl.pallas_call(body, name=name, grid=(r // tr,), in_specs=[spec] * 4, out_specs=[spec] * 3,
                          out_shape=[jax.ShapeDtypeStruct((r, c), f32)] * 3, compiler_params=_params("parallel"))(w, g, m, v)


def adamw_of_partial_sums(w, parts, m, v, *, name):
    r, c = w.shape
    cp = parts.shape[2]
    tr = _tile(r, 256, 16)

    def body(w_ref, p_ref, m_ref, v_ref, g_ref, d_ref, nm_ref, nv_ref):
        part = [p_ref[k, :, :c].astype(f32) for k in range(4)]
        g = ((part[3] + part[0]) + part[1]) + part[2]
        g_ref[...] = g
        d_ref[...], nm_ref[...], nv_ref[...] = _adamw_update(w_ref[...], g, m_ref[...], v_ref[...])

    spec = pl.BlockSpec((tr, c), lambda i: (i, 0))
    return pl.pallas_call(body, name=name, grid=(r // tr,),
                          in_specs=[spec, pl.BlockSpec((4, tr, cp), lambda i: (0, i, 0)), spec, spec], out_specs=[spec] * 4,
                          out_shape=[jax.ShapeDtypeStruct((r, c), f32)] * 4,
                          compiler_params=_params("parallel"))(w, parts, m, v)


def _pos():
    return lax.axis_index("x"), lax.axis_index("y"), lax.axis_index("c")


ANY = pl.BlockSpec(memory_space=pl.ANY)


class Side(NamedTuple):
    ins: list
    outs: list
    aliases: dict
    sems: list
    start: Callable
    wait: Callable


def join_sides(*sides):
    def spans(key):
        out, off = [], 0
        for sd in sides:
            out.append(slice(off, off + len(getattr(sd, key))))
            off += len(getattr(sd, key))
        return out

    i_sp, o_sp, s_sp = spans("ins"), spans("outs"), spans("sems")
    aliases = {i_sp[n].start + i: o_sp[n].start + o for n, sd in enumerate(sides) for i, o in sd.aliases.items()}

    def each(what):
        def run(ins, outs, sems):
            for n, sd in enumerate(sides):
                getattr(sd, what)(ins[i_sp[n]], outs[o_sp[n]], sems[s_sp[n]])
        return run

    return Side([a for sd in sides for a in sd.ins], [o for sd in sides for o in sd.outs], aliases,
                [s for sd in sides for s in sd.sems], each("start"), each("wait"))


def _side_body(body, side, n_in, n_out, n_scratch, grid):
    ns_in, ns_out = len(side.ins), len(side.outs)

    def wrapped(*refs):
        cut = [n_in, ns_in, n_out, ns_out, n_scratch]
        parts, off = [], 0
        for c in cut:
            parts.append(refs[off:off + c])
            off += c
        ins, s_ins, outs, s_outs, scratch = parts
        sems = refs[off:]
        if grid:
            ids = [pl.program_id(d) for d in range(len(grid))]
            first = functools.reduce(jnp.logical_and, [i == 0 for i in ids])
            last = functools.reduce(jnp.logical_and, [i == g - 1 for i, g in zip(ids, grid)])
            pl.when(first)(lambda: side.start(s_ins, s_outs, sems))
            body(*ins, *outs, *scratch)
            pl.when(last)(lambda: side.wait(s_ins, s_outs, sems))
        else:
            side.start(s_ins, s_outs, sems)
            side.wait(s_ins, s_outs, sems)

    return wrapped


def _call(body, args, *, name, grid, in_specs, out_specs, out_shape, semantics, scratch_shapes=(), side=None):
    if side is None:
        return pl.pallas_call(body, name=name, grid=grid, in_specs=in_specs, out_specs=out_specs, out_shape=out_shape,
                              scratch_shapes=list(scratch_shapes), compiler_params=_params(*semantics))(*args)
    single = not isinstance(out_shape, (list, tuple))
    shapes, specs = ([out_shape], [out_specs]) if single else (list(out_shape), list(out_specs))
    n_in, n_out = len(in_specs), len(shapes)
    res = pl.pallas_call(
        _side_body(body, side, n_in, n_out, len(scratch_shapes), grid), name=name, grid=grid,
        in_specs=list(in_specs) + [ANY] * len(side.ins), out_specs=specs + [ANY] * len(side.outs),
        out_shape=shapes + list(side.outs), scratch_shapes=list(scratch_shapes) + list(side.sems),
        input_output_aliases={n_in + i: n_out + o for i, o in side.aliases.items()},
        compiler_params=_params(*(["arbitrary"] * len(grid))),
    )(*args, *side.ins)
    return (res[0] if single else res[:n_out]), list(res[n_out:])


def run_side(side, *, name):
    return pl.pallas_call(_side_body(None, side, 0, 0, 0, ()), name=name, in_specs=[ANY] * len(side.ins),
                          out_specs=[ANY] * len(side.outs), out_shape=list(side.outs), scratch_shapes=list(side.sems),
                          input_output_aliases=dict(side.aliases))(*side.ins)


def _remote(src, dst, send, recv, k, to):
    return pltpu.make_async_remote_copy(src_ref=src, dst_ref=dst, send_sem=send.at[k], recv_sem=recv.at[k], device_id=to,
                                        device_id_type=MESH)


def gather_first(shards):
    na = len(shards)

    def copies(x_refs, out_refs, sems):
        send, recv, local = sems
        x, y, cc = _pos()
        me = 4 * x + 2 * y + cc
        peers = [(x, y, 1 - cc), (1 - x, y, cc), (x, 1 - y, cc), (1 - x, 1 - y, cc)]
        mine = [pltpu.make_async_copy(x_refs[a], out_refs[a].at[me], local.at[a]) for a in range(na)]
        sent = [_remote(x_refs[a], out_refs[a].at[me], send, recv, 4 * a + k, p) for a in range(na) for k, p in enumerate(peers)]
        landed = [_remote(x_refs[a], out_refs[a].at[4 * p[0] + 2 * p[1] + p[2]], send, recv, 4 * a + k, p)
                  for a in range(na) for k, p in enumerate(peers)]
        return mine, sent, landed

    def start(x_refs, out_refs, sems):
        mine, sent, _ = copies(x_refs, out_refs, sems)
        for cp in mine + sent:
            cp.start()

    def wait(x_refs, out_refs, sems):
        mine, sent, landed = copies(x_refs, out_refs, sems)
        for cp in sent:
            cp.wait_send()
        for cp in landed:
            cp.wait_recv()
        for cp in mine:
            cp.wait()

    return Side(list(shards), [jax.ShapeDtypeStruct((N_DEV,) + s.shape, s.dtype) for s in shards], {},
                [pltpu.SemaphoreType.DMA((4 * na,)), pltpu.SemaphoreType.DMA((4 * na,)), pltpu.SemaphoreType.DMA((na,))],
                start, wait)


def gather_second(slots):
    na = len(slots)

    def copies(out_refs, sems):
        send, recv = sems
        x, y, cc = _pos()
        chips = [(1 - x, y), (x, 1 - y), (1 - x, 1 - y)]
        sent, landed = [], []
        for a in range(na):
            for j, (px, py) in enumerate(chips):
                row = out_refs[a].at[4 * px + 2 * py + cc]
                sent.append(_remote(row, row, send, recv, 3 * a + j, (x, y, 1 - cc)))
                landed.append(_remote(row, out_refs[a].at[4 * px + 2 * py + 1 - cc], send, recv, 3 * a + j, (x, y, 1 - cc)))
        return sent, landed

    def start(_, out_refs, sems):
        for cp in copies(out_refs, sems)[0]:
            cp.start()

    def wait(_, out_refs, sems):
        sent, landed = copies(out_refs, sems)
        for cp in sent:
            cp.wait_send()
        for cp in landed:
            cp.wait_recv()

    return Side(list(slots), [jax.ShapeDtypeStruct(s.shape, s.dtype) for s in slots], {a: a for a in range(na)},
                [pltpu.SemaphoreType.DMA((3 * na,)), pltpu.SemaphoreType.DMA((3 * na,))], start, wait)


def _rows_of(ref, lead, rows):
    if rows is None:
        return ref if lead is None else ref.at[lead]
    cut = pl.ds(rows[0], rows[1])
    return ref.at[:, cut] if lead is None else ref.at[lead, cut]


def _side_into(arrays, out_shapes, into):
    na = len(arrays)
    if into is None:
        return list(arrays), out_shapes, {}
    return list(arrays) + list(into), out_shapes, {na + a: a for a in range(na)}


def grad_to_sibling(chunks, rows=None, into=None):
    na = len(chunks)

    def start(g_refs, out_refs, sems):
        send, recv = sems
        x, y, cc = _pos()
        for a in range(na):
            for q in range(4):
                _remote(_rows_of(g_refs[a], 2 * q + 1 - cc, rows), _rows_of(out_refs[a], q, rows), send, recv, a,
                        (x, y, 1 - cc)).start()

    def wait(g_refs, out_refs, sems):
        send, recv = sems
        x, y, cc = _pos()
        for a in range(na):
            whole = _rows_of(out_refs[a], None, rows)
            _remote(whole, whole, send, recv, a, (x, y, 1 - cc)).wait()

    ins, outs, aliases = _side_into(chunks, [jax.ShapeDtypeStruct((4,) + g.shape[1:], g.dtype) for g in chunks], into)
    return Side(ins, outs, aliases, [pltpu.SemaphoreType.DMA((na,)), pltpu.SemaphoreType.DMA((na,))], start, wait)


def grad_to_chips(parts, rows=None, into=None):
    na = len(parts)

    def copies(p_refs, out_refs, sems):
        send, recv, local = sems
        x, y, cc = _pos()
        chips = [(1 - x, y), (x, 1 - y), (1 - x, 1 - y)]
        mine = [pltpu.make_async_copy(_rows_of(p_refs[a], 2 * x + y, rows), _rows_of(out_refs[a], 3, rows), local.at[a])
                for a in range(na)]
        sent = [_remote(_rows_of(p_refs[a], 2 * px + py, rows), _rows_of(out_refs[a], k, rows), send, recv, 3 * a + k,
                        (px, py, cc)) for a in range(na) for k, (px, py) in enumerate(chips)]
        return mine, sent

    def start(p_refs, out_refs, sems):
        mine, sent = copies(p_refs, out_refs, sems)
        for cp in mine + sent:
            cp.start()

    def wait(p_refs, out_refs, sems):
        mine, sent = copies(p_refs, out_refs, sems)
        for cp in sent:
            cp.wait()
        for cp in mine:
            cp.wait()

    ins, outs, aliases = _side_into(parts, [jax.ShapeDtypeStruct(p.shape, p.dtype) for p in parts], into)
    return Side(ins, outs, aliases,
                [pltpu.SemaphoreType.DMA((3 * na,)), pltpu.SemaphoreType.DMA((3 * na,)), pltpu.SemaphoreType.DMA((na,))],
                start, wait)


def add_sibling(chunks, recv, *, name):
    _, r, c = chunks.shape
    tr = r if r <= 1024 else _tile(r, 512, 16)
    core = lax.axis_index("c").astype(jnp.int32).reshape(1)

    def body(core_ref, a_ref, b_ref, o_ref):
        o_ref[...] = (a_ref[...] + b_ref[...]).astype(bf16)

    return pl.pallas_call(
        body, name=name,
        grid_spec=pltpu.PrefetchScalarGridSpec(
            num_scalar_prefetch=1, grid=(4, r // tr),
            in_specs=[pl.BlockSpec((1, tr, c), lambda q, i, core_ref: (2 * q + core_ref[0], i, 0)),
                      pl.BlockSpec((1, tr, c), lambda q, i, core_ref: (q, i, 0))],
            out_specs=pl.BlockSpec((1, tr, c), lambda q, i, core_ref: (q, i, 0))),
        out_shape=jax.ShapeDtypeStruct((4, r, c), bf16), compiler_params=_params("parallel", "parallel"),
    )(core, chunks, recv)


def all_reduce_small(vec, *, name):
    r, c = vec.shape

    def body(v_ref, out_ref, buf_ref, send_sems, recv_sems):
        x, y, cc = _pos()
        my_id = 4 * x + 2 * y + cc
        buf_ref[my_id] = v_ref[...]
        flips = [(fx, fy, fc) for fx in (0, 1) for fy in (0, 1) for fc in (0, 1)][1:]
        cps = []
        for k, (fx, fy, fc) in enumerate(flips):
            peer = ((1 - x) if fx else x, (1 - y) if fy else y, (1 - cc) if fc else cc)
            cps.append(pltpu.make_async_remote_copy(src_ref=v_ref, dst_ref=buf_ref.at[my_id], send_sem=send_sems.at[k],
                                                    recv_sem=recv_sems.at[k], device_id=peer, device_id_type=MESH))
        for cp in cps:
            cp.start()
        for cp in cps:
            cp.wait()
        acc = buf_ref[0]
        for d in range(1, N_DEV):
            acc = acc + buf_ref[d]
        out_ref[...] = acc

    vm = pl.BlockSpec(memory_space=pltpu.VMEM)
    return pl.pallas_call(body, name=name, in_specs=[vm], out_specs=vm, out_shape=jax.ShapeDtypeStruct((r, c), f32),
                          scratch_shapes=[pltpu.VMEM((N_DEV, r, c), f32), pltpu.SemaphoreType.DMA((7,)),
                                          pltpu.SemaphoreType.DMA((7,))])(vec)


def _pack(parts, rows, dtype):
    flat = jnp.concatenate([p.reshape(-1).astype(dtype) for p in parts])
    return jnp.pad(flat, (0, rows * PACK_COLS - flat.shape[0])).reshape(rows, PACK_COLS)


def _unpack(flat, shapes):
    out, off = [], 0
    for shp in shapes:
        n = shp[0] * shp[1]
        out.append(flat[..., off:off + n].reshape(flat.shape[:-1] + tuple(shp)))
        off += n
    return out


def _from_column_shards(g):
    _, r, c = g.shape
    return jnp.transpose(g, (1, 0, 2)).reshape(r, N_DEV * c)


def _column_shards(full):
    r, c8 = full.shape
    return jnp.transpose(full.reshape(r, N_DEV, c8 // N_DEV), (1, 0, 2))


W_IN_SHARD = IN_DIM // N_DEV
W_IN_PAD = 1280
W_IN_PARTS = (("swa", 0, 0, 1280), ("swa", 1280, 5376, 5392), ("gdn", 0, 1280, 5376), ("gates", 0, 5392, IN_DIM))
W_IN_WIDTHS = {"swa": SWA_IN_W, "gdn": 4 * GDN_W, "gates": 2 * D_MODEL}


def _w_in_segments():
    segs = []
    for part, p0, g0, g1 in W_IN_PARTS:
        for j in range(N_DEV):
            lo, hi = max(g0, W_IN_SHARD * j), min(g1, W_IN_SHARD * (j + 1))
            if lo < hi:
                segs.append((part, p0 + lo - g0, j, lo - W_IN_SHARD * j, hi - lo))
    return segs


def split_w_in(shards, *, name):
    dt = shards.dtype
    tm = 256

    def body(w_ref, swa_ref, gdn_ref, gates_ref):
        out = {"swa": swa_ref, "gdn": gdn_ref, "gates": gates_ref}
        swa_ref[:, SWA_Q + 2 * SWA_KV + 2 * GDN_HEADS:] = jnp.zeros((tm, SWA_IN_W - SWA_Q - 2 * SWA_KV - 2 * GDN_HEADS), dt)
        for part, p0, j, l0, n in _w_in_segments():
            out[part][:, p0:p0 + n] = w_ref[j, :, l0:l0 + n]

    return pl.pallas_call(body, name=name, grid=(D_MODEL // tm,),
                          in_specs=[pl.BlockSpec((N_DEV, tm, W_IN_PAD), lambda i: (0, i, 0))],
                          out_specs=[pl.BlockSpec((tm, W_IN_WIDTHS[p]), lambda i: (i, 0)) for p in ("swa", "gdn", "gates")],
                          out_shape=[jax.ShapeDtypeStruct((D_MODEL, W_IN_WIDTHS[p]), dt) for p in ("swa", "gdn", "gates")],
                          compiler_params=_params("parallel"))(shards)


def merge_w_in_grad(d_swa, d_gdn, d_gates, *, name):
    tm = 256

    def body(swa_ref, gdn_ref, gates_ref, w_ref):
        src = {"swa": swa_ref, "gdn": gdn_ref, "gates": gates_ref}
        w_ref[:, :, W_IN_SHARD:] = jnp.zeros((N_DEV, tm, W_IN_PAD - W_IN_SHARD), f32)
        for part, p0, j, l0, n in _w_in_segments():
            w_ref[j, :, l0:l0 + n] = src[part][:, p0:p0 + n]

    return pl.pallas_call(body, name=name, grid=(D_MODEL // tm,),
                          in_specs=[pl.BlockSpec((tm, W_IN_WIDTHS[p]), lambda i: (i, 0)) for p in ("swa", "gdn", "gates")],
                          out_specs=pl.BlockSpec((N_DEV, tm, W_IN_PAD), lambda i: (0, i, 0)),
                          out_shape=jax.ShapeDtypeStruct((N_DEV, D_MODEL, W_IN_PAD), f32),
                          compiler_params=_params("parallel"))(d_swa, d_gdn, d_gates)


def kernel(x, mem, w_in, rel_bias, swa_sinks, gdn_conv_w, gdn_a_log, gdn_dt_bias, gdn_norm_w, w_br_swa, w_br_gdn, w_mix_o, ln1_g, ln1_b, w_mem_q, w_mem_kv, w_mem_o, ln2_g, ln2_b, w_up, ffn_conv_w, ffn_conv_b, w_down, ln3_g, ln3_b, loss_target, m_w_in, m_rel_bias, m_swa_sinks, m_gdn_conv_w, m_gdn_a_log, m_gdn_dt_bias, m_gdn_norm_w, m_w_br_swa, m_w_br_gdn, m_w_mix_o, m_ln1_g, m_ln1_b, m_w_mem_q, m_w_mem_kv, m_w_mem_o, m_ln2_g, m_ln2_b, m_w_up, m_ffn_conv_w, m_ffn_conv_b, m_w_down, m_ln3_g, m_ln3_b, v_w_in, v_rel_bias, v_swa_sinks, v_gdn_conv_w, v_gdn_a_log, v_gdn_dt_bias, v_gdn_norm_w, v_w_br_swa, v_w_br_gdn, v_w_mix_o, v_ln1_g, v_ln1_b, v_w_mem_q, v_w_mem_kv, v_w_mem_o, v_ln2_g, v_ln2_b, v_w_up, v_ffn_conv_w, v_ffn_conv_b, v_w_down, v_ln3_g, v_ln3_b):
    env = dict(locals())
    w2 = {n: (env[n][0] if env[n].ndim == 3 else env[n]) for n in WEIGHTS}
    m2 = {n: (env["m_" + n][0] if env["m_" + n].ndim == 3 else env["m_" + n]) for n in WEIGHTS}
    v2 = {n: (env["v_" + n][0] if env["v_" + n].ndim == 3 else env["v_" + n]) for n in WEIGHTS}
    xs, mems, target = x[0], mem[0], loss_target[0]
    my_id = 4 * lax.axis_index("x") + 2 * lax.axis_index("y") + lax.axis_index("c")
    pad_ff = FF_PAD - FF_SHARD

    pad_cols = {"w_in": W_IN_PAD - W_IN_SHARD, "w_up": pad_ff}
    mid = ("w_br_swa", "w_br_gdn", "w_mem_o", "w_mix_o", "w_mem_q", "w_mem_kv")
    mine = {n: jnp.pad(w2[n], ((0, 0), (0, pad_cols.get(n, 0)))).astype(bf16) for n in ("w_in", "w_up", "w_down") + mid}
    xb, got_in = cast_bf16(xs, name="cast_x", side=gather_first([mine["w_in"]]))
    got_in = run_side(gather_second(got_in), name="gather_w_in_pass_on")
    w_swa, w_gdn, w_gates = split_w_in(got_in[0], name="split_w_in")
    n_ffn, n_gdn = 3 * FF_SHARD, GDN_CONV * (QKV_W // N_DEV)
    conv_mine = jnp.concatenate([w2["ffn_conv_w"].reshape(-1), w2["gdn_conv_w"].reshape(-1)])[None]
    conv_rows = lax.dynamic_update_slice(jnp.zeros((N_DEV, n_ffn + n_gdn), f32), conv_mine, (my_id, 0))
    conv_all = all_reduce_small(_pack([conv_rows], CONV_ROWS, f32), name="gather_conv_w")
    conv_all = conv_all.reshape(-1)[:N_DEV * (n_ffn + n_gdn)].reshape(N_DEV, n_ffn + n_gdn)
    cwb = jnp.concatenate([conv_all[:, :n_ffn].reshape(N_DEV, 3, FF_SHARD), w2["ffn_conv_b"].reshape(N_DEV, 1, FF_SHARD),
                           jnp.zeros((N_DEV, 4, FF_SHARD), f32)], axis=1)
    cwb = jnp.pad(cwb, ((0, 0), (0, 0), (0, pad_ff)))
    convw = jnp.transpose(conv_all[:, n_ffn:].reshape(N_DEV, GDN_CONV, QKV_W // N_DEV), (1, 0, 2)).reshape(GDN_CONV, QKV_W)
    convw = jnp.pad(convw, ((0, 4), (0, 0)))
    onehot = _bucket_onehot()
    bias = mm(w2["rel_bias"].T, onehot, "nn", hi=True, tn=4096, name="rel_bias_table").reshape(SWA_HEADS, BLOCK, 2 * BLOCK)
    alog_row = jnp.pad(w2["gdn_a_log"], ((0, 0), (GDN_HEADS, 128 - 2 * GDN_HEADS)))
    dt_row = jnp.pad(w2["gdn_dt_bias"], ((0, 0), (GDN_HEADS, 128 - 2 * GDN_HEADS)))

    memb = cast_bf16(mems, name="cast_mem")
    gates, mid_got = mm(xb, w_gates, "nn", out_dtype=bf16, name="proj_gates", side=gather_first([mine[n] for n in mid]))
    gdn_in, mid_got = mm(xb, w_gdn, "nn", name="proj_gdn", side=gather_second(mid_got))
    got = dict(zip(mid, mid_got))
    w_br_swa, w_br_gdn, w_mem_o = (_from_column_shards(got[n]) for n in ("w_br_swa", "w_br_gdn", "w_mem_o"))
    w_mix_o = got["w_mix_o"].reshape(D_MODEL, D_MODEL)
    w_mem_q = got["w_mem_q"].reshape(D_MODEL, MEM_W)
    w_mem_kv = got["w_mem_kv"].reshape(D_MODEL, 2 * MEM_W)
    swa_in = mm(xb, w_swa, "nn", tn=SWA_IN_W, name="proj_swa")
    attn, down_got = swa_fwd(swa_in, bias, w2["swa_sinks"], name="swa_fwd", side=gather_first([mine["w_down"]]))
    qn, kn, vv, gdn_conv = gdn_pre_fwd(gdn_in, convw, name="gdn_pre_fwd")
    gbeta = gbeta_fwd(swa_in, alog_row, dt_row, name="gbeta_fwd")
    (o_gdn, states, inverses), up_got = gdn_scan_fwd(qn, kn, vv, gbeta, name="gdn_scan_fwd",
                                                     side=gather_first([mine["w_up"]]))
    ygd = gdn_post_fwd(o_gdn, gdn_in, w2["gdn_norm_w"], name="gdn_post_fwd")
    y_swa, down_got = mm(attn, w_br_swa, "nn", out_dtype=bf16, name="br_swa", side=gather_second(down_got))
    y_gdn, up_got = mm(ygd, w_br_gdn, "nn", out_dtype=bf16, name="br_gdn", side=gather_second(up_got))
    w_up_blocked = up_got[0]
    w_down_p = jnp.pad(down_got[0].reshape(4, FF_SHARD, D_MODEL), ((0, 0), (0, pad_ff), (0, 0))).reshape(4 * FF_PAD, D_MODEL)
    mixed = merge_fwd(gates, y_swa, y_gdn, name="merge_fwd")
    z1 = mm(mixed, w_mix_o, "nn", add=xs, add_scale=ALPHA, name="mix_o")
    x1, x1b = ln_fwd(z1, w2["ln1_g"], w2["ln1_b"], name="ln1_fwd")
    qm = mm(x1b, w_mem_q, "nn", name="mem_q")
    kv = mm(memb, w_mem_kv, "nn", name="mem_kv")
    om = memattn_fwd(qm, kv, name="memattn_fwd")
    z2 = mm(om, w_mem_o, "nn", add=x1, add_scale=ALPHA, name="mem_o")
    x2, x2b = ln_fwd(z2, w2["ln2_g"], w2["ln2_b"], name="ln2_fwd")
    hpre = mm(x2b, w_up_blocked, "nn", b_blocked=True, out_dtype=bf16, name="ffn_up")
    act, conv_g, conv_u = ffn_act_fwd(hpre, cwb, name="ffn_act_fwd")
    z3 = mm(act, w_down_p, "nn", add=x2, add_scale=ALPHA, tk=2 * FF_PAD, name="ffn_down")
    dz3, dz3b, d_ln3g, d_ln3b, loss = ln_loss(z3, target, w2["ln3_g"], w2["ln3_b"], name="ln3_loss")

    dact = mm(dz3b, w_down_p, "nt", tn=FF_PAD, out_dtype=bf16, name="d_act")
    d_wdown_p = mm(act, dz3b, "tn", tm=FF_PAD, name="dw_down")
    d_hpre, d_cwb = ffn_act_bwd(hpre, conv_g, conv_u, dact, cwb, name="ffn_act_bwd")
    def sibling_sums(names, chunks, received):
        return [add_sibling(c, r, name="grad_add_sibling_" + n) for n, c, r in zip(names, chunks, received)]

    dx2 = mm(d_hpre, w_up_blocked, "nt", b_blocked=True, k_shards=2, add=dz3, add_scale=ALPHA, name="d_x2")
    d_wup = mm(x2b, d_hpre, "tn", out_blocked=True, name="dw_up")
    ffn = ("w_up", "w_down")
    ffn_chunks = [d_wup, d_wdown_p.reshape(4, FF_PAD, D_MODEL)[:, :FF_SHARD].reshape(N_DEV, FF_SHARD // 2, D_MODEL)]
    dz2, dz2b, d_ln2g, d_ln2b = ln_bwd(dx2, z2, w2["ln2_g"], name="ln2_bwd")
    d_om, down_received = mm(dz2b, w_mem_o, "nt", name="d_om", side=grad_to_sibling(ffn_chunks[1:]))
    d_wmemo = mm(om, dz2b, "tn", name="dw_mem_o")
    dqm, dkv = memattn_bwd(qm, kv, d_om, name="memattn_bwd")
    dx1 = mm(dqm, w_mem_q, "nt", add=dz2, add_scale=ALPHA, name="d_x1")
    d_wmemq = mm(x1b, dqm, "tn", name="dw_mem_q")
    d_wmemkv = mm(memb, dkv, "tn", name="dw_mem_kv")
    dz1, dz1b, d_ln1g, d_ln1b = ln_bwd(dx1, z1, w2["ln1_g"], name="ln1_bwd")
    half = D_MODEL // 2
    dmix, up_received = mm(dz1b, w_mix_o, "nt", name="d_mixed", side=grad_to_sibling(ffn_chunks[:1], rows=(0, half)))
    d_wmixo, up_received = mm(mixed, dz1b, "tn", tk=4096, name="dw_mix_o",
                              side=grad_to_sibling(ffn_chunks[:1], rows=(half, half), into=up_received))
    ffn_sums = sibling_sums(ffn, ffn_chunks, up_received + down_received)
    dys, dyg, d_gates = merge_bwd(gates, y_swa, y_gdn, dmix, name="merge_bwd")
    d_attn = mm(dys, w_br_swa, "nt", name="d_attn")
    d_wbrswa = mm(attn, dys, "tn", tk=4096, name="dw_br_swa")
    d_ygd = mm(dyg, w_br_gdn, "nt", name="d_ygd")
    d_wbrgdn = mm(ygd, dyg, "tn", tk=4096, name="dw_br_gdn")
    mid_chunks = [_column_shards(d_wbrswa), _column_shards(d_wbrgdn), _column_shards(d_wmemo),
                  d_wmixo.reshape(N_DEV, D_MODEL // N_DEV, D_MODEL), d_wmemq.reshape(N_DEV, D_MODEL // N_DEV, MEM_W),
                  d_wmemkv.reshape(N_DEV, D_MODEL // N_DEV, 2 * MEM_W)]
    d_o, d_gz, d_normw = gdn_post_bwd(o_gdn, gdn_in, w2["gdn_norm_w"], d_ygd, name="gdn_post_bwd")
    (dqn, dkn, dvv, dgbeta), received = gdn_scan_bwd(
        qn, kn, vv, gbeta, states, inverses, d_o, name="gdn_scan_bwd",
        side=join_sides(grad_to_chips(ffn_sums), grad_to_sibling(mid_chunks)))
    chip_parts = dict(zip(ffn, received[:2]))
    mid_sums = sibling_sums(mid, mid_chunks, received[2:])
    d_gdn_in, d_convw = gdn_pre_bwd(gdn_in, gdn_conv, convw, dqn, dkn, dvv, d_gz, name="gdn_pre_bwd")
    d_ba, d_alog, d_dt = gbeta_bwd(swa_in, alog_row, dt_row, dgbeta, name="gbeta_bwd")
    (dq, dkc, dkp, dvc, dvp, d_bias, d_sinks), received = swa_bwd(swa_in, bias, w2["swa_sinks"], d_attn, name="swa_bwd",
                                                                  side=grad_to_chips(mid_sums))
    chip_parts.update(zip(mid, received))
    d_swa_in = swa_in_grad(dq, dkc, dkp, dvc, dvp, d_ba, name="swa_in_grad")
    d_relbias = mm(d_bias.reshape(SWA_HEADS, -1), onehot, "nt", hi=True, tk=4096, name="d_rel_bias").T
    d_wgates = mm(xb, d_gates, "tn", tk=4096, name="dw_gates")
    d_wgdn = mm(xb, d_gdn_in, "tn", tk=4096, name="dw_gdn")
    d_wswa = mm(xb, d_swa_in, "tn", tn=SWA_IN_W, name="dw_swa")
    in_chunks = [merge_w_in_grad(d_wswa, d_wgdn, d_wgates, name="merge_w_in_grad")]
    gx, received = mm(d_swa_in, w_swa, "nt", add=dz1, add_scale=ALPHA, tk=SWA_IN_W, name="dx_swa",
                      side=grad_to_sibling(in_chunks))
    in_sums = sibling_sums(("w_in",), in_chunks, received)
    gx, received = mm(d_gates, w_gates, "nt", add=gx, name="dx_gates", side=grad_to_chips(in_sums, rows=(0, half)))
    gx, received = mm(d_gdn_in, w_gdn, "nt", add=gx, name="dx_gdn",
                      side=grad_to_chips(in_sums, rows=(half, half), into=received))
    chip_parts["w_in"] = received[0]
    grads = {}

    gsmall = {
        "rel_bias": d_relbias, "swa_sinks": d_sinks[:, :SWA_HEADS], "gdn_a_log": d_alog[:, GDN_HEADS:2 * GDN_HEADS],
        "gdn_dt_bias": d_dt[:, GDN_HEADS:2 * GDN_HEADS], "gdn_norm_w": d_normw, "ln1_g": d_ln1g, "ln1_b": d_ln1b,
        "ln2_g": d_ln2g, "ln2_b": d_ln2b, "ln3_g": d_ln3g, "ln3_b": d_ln3b,
        "ffn_conv_b": d_cwb[:, 3, :FF_SHARD].reshape(1, 2 * D_FF),
        "ffn_conv_w": jnp.transpose(d_cwb[:, :3, :FF_SHARD], (1, 0, 2)).reshape(3, 2 * D_FF),
        "gdn_conv_w": d_convw[:GDN_CONV],
    }
    small_shapes = [shp for _, shp in SMALL] + [(3, 2 * D_FF), (GDN_CONV, QKV_W)]
    small_names = [n for n, _ in SMALL] + ["ffn_conv_w", "gdn_conv_w"]
    small_sum = all_reduce_small(_pack([gsmall[n] for n in small_names], AR_ROWS, f32), name="all_reduce_small")
    grads.update(zip(small_names, _unpack(small_sum.reshape(-1), small_shapes)))
    grads["ffn_conv_w"] = lax.dynamic_slice_in_dim(grads["ffn_conv_w"], my_id * FF_SHARD, FF_SHARD, axis=1)
    grads["gdn_conv_w"] = lax.dynamic_slice_in_dim(grads["gdn_conv_w"], my_id * (QKV_W // N_DEV), QKV_W // N_DEV, axis=1)

    big = [n for n, shp, _ in SHARDED if shp[0] * shp[1] > 8192]
    tiny = [n for n in WEIGHTS if n not in big]
    delta, new_m, new_v = {}, {}, {}
    for n in big:
        grads[n], delta[n], new_m[n], new_v[n] = adamw_of_partial_sums(w2[n], chip_parts[n], m2[n], v2[n], name="adamw_" + n)
    tiny_shapes = [w2[n].shape for n in tiny]
    packed = [_pack([src[n] for n in tiny], SMALL_ROWS, f32) for src in (w2, grads, m2, v2)]
    for dst, res in zip((delta, new_m, new_v), adamw(*packed, name="adamw_small")):
        dst.update(zip(tiny, _unpack(res.reshape(-1), tiny_shapes)))

    def shaped(d):
        return [d[n].reshape(env[n].shape) for n in WEIGHTS]

    loss_all = lax.psum(loss[0, 0], ("x", "y", "c"))
    return (loss_all, gx[None], *shaped(grads), *shaped(delta), *shaped(new_m), *shaped(new_v))
```
